```python
import math
import jax
import jax.numpy as jnp
from jax import lax
import numpy as np

D_MODEL = 1024
BATCH = 32
SEQ = 256
DEPTH = 4
DEC_BATCH = 4
DEC_SEQ = 4096
PAST_LEN = 256

GRID_W = 64
N_BRANCH = 3
BRANCH_W = D_MODEL // 2
HY_ORDER = 2
HY_WIDTH = BRANCH_W
HY_POS_BANDS = 16
HY_POS_DIM = 1 + 2 * HY_POS_BANDS
HY_FILT_HIDDEN = 64
HY_DECAY_TARGET = 1e-2
HY_FAST_DECAY = 0.3
HY_SLOW_DECAY = 1.5
NA_HEADS = 8
NA_HEAD_DIM = BRANCH_W // NA_HEADS
NA_WIN_ROWS = 8
NA_WIN_COLS = 16
DA_HEADS = 8
DA_HEAD_DIM = BRANCH_W // (2 * DA_HEADS)
DA_V_DIM = 2 * DA_HEAD_DIM
D_FF = -(-(8 * D_MODEL) // (3 * 256)) * 256
IN_W = 3 * HY_WIDTH + 3 * BRANCH_W + 3 * BRANCH_W + N_BRANCH * D_MODEL
ROPE_BASE = 10000.0
EPS = 1e-6
Q_BLOCK = 128
NEG_INF = -1e30

kernel_name = 'hybrid_diffusion_trunk_step'


def rmsnorm(x, g):
    xf = x.astype(jnp.float32)
    y = xf * lax.rsqrt(jnp.mean(jnp.square(xf), axis=-1, keepdims=True) + EPS)
    return (y * g.astype(jnp.float32)).astype(x.dtype)


def modulated_norm(x, g, shift, scale):
    return rmsnorm(x, g) * (1 + scale[:, None, :]) + shift[:, None, :]


def short_conv(u, w, b):
    up = jnp.pad(u, ((0, 0), (1, 1), (0, 0)))
    return up[:, :-2] * w[0] + up[:, 1:-1] * w[1] + up[:, 2:] * w[2] + b


def hyena_filters(L, w1, b1, w2, b2, w3, freq):
    f32 = jnp.float32
    pos = jnp.arange(L, dtype=f32)
    t = pos / L
    bands = jnp.linspace(1e-4, HY_POS_BANDS - 1, HY_POS_BANDS, dtype=f32)
    ang = (2 * math.pi / L) * pos[:, None] * bands[None, :]
    feats = jnp.concatenate([t[:, None], jnp.cos(ang), -jnp.sin(ang)], axis=-1)
    fr = freq.astype(f32)
    h = jnp.sin(fr * (feats @ w1.astype(f32) + b1.astype(f32)))
    h = jnp.sin(fr * (h @ w2.astype(f32) + b2.astype(f32)))
    h = (h @ w3.astype(f32)).reshape(L, HY_ORDER, 2, HY_WIDTH)
    deltas = jnp.linspace(math.log(HY_DECAY_TARGET) / HY_SLOW_DECAY,
                          math.log(HY_DECAY_TARGET) / HY_FAST_DECAY, HY_WIDTH, dtype=f32)
    decay = jnp.exp(-t[:, None] * jnp.abs(deltas)[None, :])
    h = h * decay[:, None, None, :]
    h_fwd, h_bwd = h[:, :, 0], h[:, :, 1]
    h_full = jnp.concatenate([h_fwd, jnp.zeros((1, HY_ORDER, HY_WIDTH), f32), h_bwd[1:][::-1]], axis=0)
    h_full = h_full / jnp.sum(jnp.abs(h_full), axis=0, keepdims=True)
    return jnp.fft.rfft(h_full, n=2 * L, axis=0)


def long_conv(z, h_freq, bias):
    L = z.shape[1]
    zf = z.astype(jnp.float32)
    y = jnp.fft.irfft(jnp.fft.rfft(zf, n=2 * L, axis=1) * h_freq[None], n=2 * L, axis=1)[:, :L]
    return (y + zf * bias.astype(jnp.float32)).astype(z.dtype)


def hyena_mixer(u, conv_w, conv_b, fw1, fb1, fw2, fb2, fw3, ffreq, fbias):
    L = u.shape[1]
    v, x1, x2 = jnp.split(short_conv(u, conv_w, conv_b), 3, axis=-1)
    hf = hyena_filters(L, fw1, fb1, fw2, fb2, fw3, ffreq)
    z = x1 * long_conv(v, hf[:, 0], fbias[0])
    z = x2 * long_conv(z, hf[:, 1], fbias[1])
    return z


def map_query_blocks(fn, q):
    B, L = q.shape[:2]
    nb = L // Q_BLOCK
    qb = jnp.swapaxes(q, 0, 1).reshape((nb, Q_BLOCK, B) + q.shape[2:])
    out = lax.map(fn, qb)
    out = out.reshape((L, B) + out.shape[3:])
    return jnp.swapaxes(out, 0, 1)


def context_attention(q, k, v):
    B, L, H, dh = q.shape
    scale = dh ** -0.5

    def block(qb):
        s = jnp.einsum('qbhd,bkhd->bhqk', qb, k).astype(jnp.float32) * scale
        p = jax.nn.softmax(s, axis=-1).astype(v.dtype)
        return jnp.einsum('bhqk,bkhd->qbhd', p, v)

    return map_query_blocks(block, q).reshape(B, L, H * dh)


def neighbourhood_attention(q, k, v, k_ctx, v_ctx, rpb):
    B, L, H, dh = q.shape
    rows = L // GRID_W
    wr = min(NA_WIN_ROWS, rows)
    scale = dh ** -0.5
    r = jnp.arange(rows)
    r0 = jnp.clip(r - wr // 2, 0, rows - wr)
    row_idx = r0[:, None] + jnp.arange(wr)[None, :]
    c = jnp.arange(GRID_W)
    c0 = jnp.clip(c - NA_WIN_COLS // 2, 0, GRID_W - NA_WIN_COLS)
    col_ok = (c[None, :] >= c0[:, None]) & (c[None, :] < c0[:, None] + NA_WIN_COLS)
    dr = row_idx - r[:, None]
    dc = jnp.clip(c[None, :] - c[:, None], -(NA_WIN_COLS - 1), NA_WIN_COLS - 1)
    bias = rpb[:, dr[:, None, :, None] + NA_WIN_ROWS - 1, dc[None, :, None, :] + NA_WIN_COLS - 1]
    qg = q.reshape(B, rows, GRID_W, H, dh)
    kg = k.reshape(B, rows, GRID_W, H, dh)[:, row_idx]
    vg = v.reshape(B, rows, GRID_W, H, dh)[:, row_idx]
    s_lat = jnp.einsum('brqhd,brwkhd->bhrqwk', qg, kg).astype(jnp.float32) * scale + bias.astype(jnp.float32)[None]
    s_lat = jnp.where(col_ok[None, None, None, :, None, :], s_lat, NEG_INF)
    s_ctx = jnp.einsum('brqhd,bkhd->bhrqk', qg, k_ctx).astype(jnp.float32) * scale
    n_lat = wr * GRID_W
    s = jnp.concatenate([s_lat.reshape(B, H, rows, GRID_W, n_lat), s_ctx], axis=-1)
    p = jax.nn.softmax(s, axis=-1).astype(v.dtype)
    p_lat = p[..., :n_lat].reshape(B, H, rows, GRID_W, wr, GRID_W)
    p_ctx = p[..., n_lat:]
    o = jnp.einsum('bhrqwk,brwkhd->brqhd', p_lat, vg) + jnp.einsum('bhrqk,bkhd->brqhd', p_ctx, v_ctx)
    return o.reshape(B, L, H * dh)


def axial_rope(x):
    L = x.shape[1]
    f32 = jnp.float32
    pos = jnp.arange(L)
    row = (pos // GRID_W).astype(f32)
    col = (pos % GRID_W).astype(f32)
    n_freq = DA_HEAD_DIM // 4
    inv = ROPE_BASE ** (-jnp.arange(n_freq, dtype=f32) / n_freq)
    ang = jnp.concatenate([row[:, None] * inv[None, :], col[:, None] * inv[None, :]], axis=-1)
    cos = jnp.cos(ang)[None, :, None, None, :]
    sin = jnp.sin(ang)[None, :, None, None, :]
    xp = x.astype(f32).reshape(x.shape[:-1] + (DA_HEAD_DIM // 2, 2))
    x0, x1 = xp[..., 0], xp[..., 1]
    out = jnp.stack([x0 * cos - x1 * sin, x0 * sin + x1 * cos], axis=-1).reshape(x.shape)
    return out.astype(x.dtype)


def diff_attention(q, k, v, lam, lam_init, subln):
    B, L, H = q.shape[:3]
    scale = DA_HEAD_DIM ** -0.5

    def block(qb):
        s = jnp.einsum('qbhid,bkhid->bhiqk', qb, k).astype(jnp.float32) * scale
        p = jax.nn.softmax(s, axis=-1)
        a = (p[:, :, 0] - lam * p[:, :, 1]).astype(v.dtype)
        return jnp.einsum('bhqk,bkhd->qbhd', a, v)

    o = map_query_blocks(block, q)
    o = rmsnorm(o, subln) * (1 - lam_init)
    return o.reshape(B, L, H * DA_V_DIM)


def trunk_layer(x, mod, ctx_cache, lam_init, norm_mix, norm_ffn, w_in, hy_conv_w, hy_conv_b,
                hy_filt_w1, hy_filt_b1, hy_filt_w2, hy_filt_b2, hy_filt_w3, hy_filt_freq, hy_bias,
                na_rpb, da_lambda, da_subln, w_lift, w_out, w_ffn_in, w_ffn_out):
    B, L, _ = x.shape
    shift1, scale1, gate1, shift2, scale2, gate2 = jnp.split(mod, 6, axis=-1)
    h = modulated_norm(x, norm_mix, shift1, scale1)
    u = h @ w_in
    u_hy, u_na, u_da, u_gate = jnp.split(
        u, [3 * HY_WIDTH, 3 * HY_WIDTH + 3 * BRANCH_W, 3 * HY_WIDTH + 6 * BRANCH_W], axis=-1)
    y_hy = hyena_mixer(u_hy, hy_conv_w, hy_conv_b, hy_filt_w1, hy_filt_b1, hy_filt_w2, hy_filt_b2,
                       hy_filt_w3, hy_filt_freq, hy_bias)
    na_q, na_k, na_v = [t.reshape(B, L, NA_HEADS, NA_HEAD_DIM) for t in jnp.split(u_na, 3, axis=-1)]
    da_q, da_k, da_v = jnp.split(u_da, 3, axis=-1)
    da_q = da_q.reshape(B, L, DA_HEADS, 2, DA_HEAD_DIM)
    da_k = da_k.reshape(B, L, DA_HEADS, 2, DA_HEAD_DIM)
    da_v = da_v.reshape(B, L, DA_HEADS, DA_V_DIM)
    lam_p = da_lambda.astype(jnp.float32)
    lam = jnp.exp(jnp.sum(lam_p[0] * lam_p[1])) - jnp.exp(jnp.sum(lam_p[2] * lam_p[3])) + lam_init
    if ctx_cache is None:
        y_na = context_attention(na_q, na_k, na_v)
        y_da = diff_attention(da_q, da_k, da_v, lam, lam_init, da_subln)
        ctx_out = (na_k, na_v, da_k.reshape(B, L, DA_HEADS, 2 * DA_HEAD_DIM), da_v)
    else:
        na_k_ctx, na_v_ctx, da_k_ctx, da_v_ctx = ctx_cache
        Lc = da_k_ctx.shape[1]
        y_na = neighbourhood_attention(na_q, na_k, na_v, na_k_ctx, na_v_ctx, na_rpb)
        k_all = jnp.concatenate([axial_rope(da_k), da_k_ctx.reshape(B, Lc, DA_HEADS, 2, DA_HEAD_DIM)], axis=1)
        v_all = jnp.concatenate([da_v, da_v_ctx], axis=1)
        y_da = diff_attention(axial_rope(da_q), k_all, v_all, lam, lam_init, da_subln)
        ctx_out = None
    g = jax.nn.sigmoid(u_gate).reshape(B, L, N_BRANCH, D_MODEL)
    merged = (g[:, :, 0] * (y_hy @ w_lift[0]) + g[:, :, 1] * (y_na @ w_lift[1])
              + g[:, :, 2] * (y_da @ w_lift[2]))
    x = x + gate1[:, None, :] * (merged @ w_out)
    h2 = modulated_norm(x, norm_ffn, shift2, scale2)
    f_gate, f_up = jnp.split(h2 @ w_ffn_in, 2, axis=-1)
    x = x + gate2[:, None, :] * ((jax.nn.silu(f_gate) * f_up) @ w_ffn_out)
    return x, ctx_out


def setup_inputs(seed: int = 0) -> dict:
    key = jax.random.key(seed)
    ks = jax.random.split(key, 32)

    def nrm(k, shape, scale):
        return jax.random.normal(k, shape, jnp.float32) * scale

    return {
        'x_prompt': nrm(ks[0], (BATCH, SEQ, D_MODEL), 1.0),
        'x_sample': nrm(ks[1], (DEC_BATCH, DEC_SEQ, D_MODEL), 1.0),
        'cache_na_k': nrm(ks[2], (DEC_BATCH, DEPTH, PAST_LEN, NA_HEADS, NA_HEAD_DIM), 1.0),
        'cache_na_v': nrm(ks[3], (DEC_BATCH, DEPTH, PAST_LEN, NA_HEADS, NA_HEAD_DIM), 1.0),
        'cache_da_k': nrm(ks[4], (DEC_BATCH, DEPTH, PAST_LEN, DA_HEADS, 2 * DA_HEAD_DIM), 1.0),
        'cache_da_v': nrm(ks[5], (DEC_BATCH, DEPTH, PAST_LEN, DA_HEADS, DA_V_DIM), 1.0),
        'c': nrm(ks[6], (DEC_BATCH, D_MODEL), 1.0),
        'c_ctx': nrm(ks[7], (D_MODEL,), 0.5),
        'w_ada': nrm(ks[8], (DEPTH, D_MODEL, 6 * D_MODEL), 0.5 * D_MODEL ** -0.5),
        'b_ada': nrm(ks[9], (DEPTH, 6 * D_MODEL), 0.01),
        'norm_mix': 1.0 + nrm(ks[10], (DEPTH, D_MODEL), 0.02),
        'norm_ffn': 1.0 + nrm(ks[11], (DEPTH, D_MODEL), 0.02),
        'w_in': nrm(ks[12], (DEPTH, D_MODEL, IN_W), D_MODEL ** -0.5),
        'hy_conv_w': nrm(ks[13], (DEPTH, 3, 3 * HY_WIDTH), 3 ** -0.5),
        'hy_conv_b': nrm(ks[14], (DEPTH, 3 * HY_WIDTH), 0.02),
        'hy_filt_w1': nrm(ks[15], (DEPTH, HY_POS_DIM, HY_FILT_HIDDEN), HY_POS_DIM ** -0.5),
        'hy_filt_b1': nrm(ks[16], (DEPTH, HY_FILT_HIDDEN), 0.02),
        'hy_filt_w2': nrm(ks[17], (DEPTH, HY_FILT_HIDDEN, HY_FILT_HIDDEN), HY_FILT_HIDDEN ** -0.5),
        'hy_filt_b2': nrm(ks[18], (DEPTH, HY_FILT_HIDDEN), 0.02),
        'hy_filt_w3': nrm(ks[19], (DEPTH, HY_FILT_HIDDEN, HY_ORDER * 2 * HY_WIDTH), HY_FILT_HIDDEN ** -0.5),
        'hy_filt_freq': 1.0 + nrm(ks[20], (DEPTH, HY_FILT_HIDDEN), 0.02),
        'hy_bias': nrm(ks[21], (DEPTH, HY_ORDER, HY_WIDTH), 0.5),
        'na_rpb': nrm(ks[22], (DEPTH, NA_HEADS, 2 * NA_WIN_ROWS - 1, 2 * NA_WIN_COLS - 1), 0.02),
        'da_lambda': nrm(ks[23], (DEPTH, 4, DA_HEAD_DIM), 0.1),
        'da_subln': 1.0 + nrm(ks[24], (DEPTH, DA_V_DIM), 0.02),
        'w_lift': nrm(ks[25], (DEPTH, N_BRANCH, BRANCH_W, D_MODEL), BRANCH_W ** -0.5),
        'w_out': nrm(ks[26], (DEPTH, D_MODEL, D_MODEL), D_MODEL ** -0.5),
        'w_ffn_in': nrm(ks[27], (DEPTH, D_MODEL, 2 * D_FF), D_MODEL ** -0.5),
        'w_ffn_out': nrm(ks[28], (DEPTH, D_FF, D_MODEL), D_FF ** -0.5),
        'norm_final': 1.0 + nrm(ks[29], (D_MODEL,), 0.02),
    }


def reference(x_prompt, x_sample, cache_na_k, cache_na_v, cache_da_k, cache_da_v, c, c_ctx,
              w_ada, b_ada, norm_mix, norm_ffn, w_in, hy_conv_w, hy_conv_b, hy_filt_w1, hy_filt_b1,
              hy_filt_w2, hy_filt_b2, hy_filt_w3, hy_filt_freq, hy_bias, na_rpb, da_lambda, da_subln,
              w_lift, w_out, w_ffn_in, w_ffn_out, norm_final):
    xp = x_prompt
    xs = x_sample
    na_ks, na_vs, da_ks, da_vs = [], [], [], []
    for l in range(DEPTH):
        lam_init = 0.8 - 0.6 * math.exp(-0.3 * l)
        mod_ctx = jax.nn.silu(c_ctx[None, :]) @ w_ada[l] + b_ada[l]
        mod_lat = jax.nn.silu(c) @ w_ada[l] + b_ada[l]
        lp = (norm_mix[l], norm_ffn[l], w_in[l], hy_conv_w[l], hy_conv_b[l], hy_filt_w1[l], hy_filt_b1[l],
              hy_filt_w2[l], hy_filt_b2[l], hy_filt_w3[l], hy_filt_freq[l], hy_bias[l], na_rpb[l],
              da_lambda[l], da_subln[l], w_lift[l], w_out[l], w_ffn_in[l], w_ffn_out[l])
        xp, ctx_kv = trunk_layer(xp, mod_ctx, None, lam_init, *lp)
        na_ks.append(ctx_kv[0])
        na_vs.append(ctx_kv[1])
        da_ks.append(ctx_kv[2])
        da_vs.append(ctx_kv[3])
        xs, _ = trunk_layer(xs, mod_lat, (cache_na_k[:, l], cache_na_v[:, l], cache_da_k[:, l], cache_da_v[:, l]),
                            lam_init, *lp)
    y_prompt = rmsnorm(xp, norm_final)
    y_sample = rmsnorm(xs, norm_final)
    new_na_k = jnp.stack(na_ks, axis=1)
    new_na_v = jnp.stack(na_vs, axis=1)
    new_da_k = jnp.stack(da_ks, axis=1)
    new_da_v = jnp.stack(da_vs, axis=1)
    return (y_prompt, y_sample, new_na_k, new_na_v, new_da_k, new_da_v)
```

```python
import functools
import math

import numpy as np
import jax
import jax.numpy as jnp
from jax import lax
from jax.experimental import pallas as pl
from jax.experimental.pallas import tpu as pltpu

F32 = jnp.float32
BF16 = jnp.bfloat16
HIGHEST = lax.Precision.HIGHEST

D_MODEL = 1024
BATCH = 32
SEQ = 256
DEPTH = 4
DEC_BATCH = 4
DEC_SEQ = 4096
PAST_LEN = 256
GRID_W = 64
GRID_H = DEC_SEQ // GRID_W
BRANCH_W = 512
HY_POS_BANDS = 16
HY_POS_DIM = 1 + 2 * HY_POS_BANDS
HY_FILT_HIDDEN = 64
HY_DECAY_TARGET = 1e-2
HY_FAST_DECAY = 0.3
HY_SLOW_DECAY = 1.5
NA_HEADS = 8
NA_HEAD_DIM = 64
NA_WIN_ROWS = 8
NA_WIN_COLS = 16
DA_HEADS = 8
DA_HEAD_DIM = 32
DA_V_DIM = 64
D_FF = 2816
MIX_W = 9 * BRANCH_W
ROPE_BASE = 10000.0
EPS = 1e-6
NEG_INF = -1e30

VMEM_LIMIT_BYTES = 56 * 1024 * 1024

FFT_N = 2 * DEC_SEQ
FFT_NO = 64
FFT_NI = 128
FFT_HALF = FFT_NO // 2


def _cparams(*sem):
    return pltpu.CompilerParams(dimension_semantics=sem, vmem_limit_bytes=VMEM_LIMIT_BYTES)


def _sigmoid(x):
    return 1.0 / (1.0 + jnp.exp(-x))


def _rms(x, g):
    return x * lax.rsqrt(jnp.mean(x * x, axis=-1, keepdims=True) + EPS) * g


def _modnorm(x, g, shift, scale):
    return _rms(x, g) * (1.0 + scale) + shift


def _bdot(a, b):
    return jnp.dot(a.astype(BF16), b.astype(BF16), preferred_element_type=F32)


def _bdot_nt(a, b):
    return lax.dot_general(a.astype(BF16), b.astype(BF16), (((1,), (1,)), ((), ())),
                           preferred_element_type=F32)


def _mod_kernel(c_ref, w_ref, b_ref, o_ref):
    c = c_ref[...]
    s = c * _sigmoid(c)
    o_ref[0] = jnp.dot(s, w_ref[0], precision=HIGHEST, preferred_element_type=F32) + b_ref[0]


def _modulation(cc, w_ada, b_ada):
    nt = 6
    return pl.pallas_call(
        _mod_kernel,
        grid=(DEPTH, nt),
        in_specs=[
            pl.BlockSpec((8, D_MODEL), lambda l, j: (0, 0)),
            pl.BlockSpec((1, D_MODEL, D_MODEL), lambda l, j: (l, 0, j)),
            pl.BlockSpec((1, 1, D_MODEL), lambda l, j: (l, 0, j)),
        ],
        out_specs=pl.BlockSpec((1, 8, D_MODEL), lambda l, j: (l, 0, j)),
        out_shape=jax.ShapeDtypeStruct((DEPTH, 8, 6 * D_MODEL), F32),
        compiler_params=_cparams("arbitrary", "arbitrary"),
        name="modulation",
    )(cc, w_ada, b_ada.reshape(DEPTH, 1, 6 * D_MODEL))


def _in_kernel(x_ref, g_ref, mod_ref, w_ref, o_ref, h_ref):
    @pl.when(pl.program_id(1) == 0)
    def _():
        m = mod_ref[0]
        h = _modnorm(x_ref[...], g_ref[...], m[:, 0:D_MODEL], m[:, D_MODEL:2 * D_MODEL])
        h_ref[...] = h.astype(BF16)

    o_ref[...] = jnp.dot(h_ref[...], w_ref[...], preferred_element_type=F32).astype(o_ref.dtype)


def _in_proj(x, g, mod, w, rows_per_mod, out_dtype):
    T = x.shape[0]
    tm, tn = 1024, 512
    per = rows_per_mod // tm
    return pl.pallas_call(
        _in_kernel,
        grid=(T // tm, MIX_W // tn),
        in_specs=[
            pl.BlockSpec((tm, D_MODEL), lambda i, j: (i, 0)),
            pl.BlockSpec((1, D_MODEL), lambda i, j: (0, 0)),
            pl.BlockSpec((1, 1, 6 * D_MODEL), lambda i, j: (i // per, 0, 0)),
            pl.BlockSpec((D_MODEL, tn), lambda i, j: (0, j)),
        ],
        out_specs=pl.BlockSpec((tm, tn), lambda i, j: (i, j)),
        out_shape=jax.ShapeDtypeStruct((T, MIX_W), out_dtype),
        scratch_shapes=[pltpu.VMEM((tm, D_MODEL), BF16)],
        compiler_params=_cparams("arbitrary", "arbitrary"),
        name="in_proj",
    )(x, g, mod, w)


def _mid_kernel(x_ref, g_ref, mod_ref, yh_ref, yn_ref, yd_ref, wg_ref, wl_ref, wo_ref, o_ref):
    m = mod_ref[0]
    x = x_ref[...]
    h = _modnorm(x, g_ref[...], m[:, 0:D_MODEL], m[:, D_MODEL:2 * D_MODEL]).astype(BF16)
    merged = None
    for br, y_ref in enumerate((yh_ref, yn_ref, yd_ref)):
        gate = _sigmoid(jnp.dot(h, wg_ref[:, br * D_MODEL:(br + 1) * D_MODEL], preferred_element_type=F32))
        lift = jnp.dot(y_ref[...].astype(BF16), wl_ref[br], preferred_element_type=F32)
        t = gate * lift
        merged = t if merged is None else merged + t
    o_ref[...] = x + m[:, 2 * D_MODEL:3 * D_MODEL] * _bdot(merged, wo_ref[...])


def _merge_out(x, g, mod, y_hy, y_na, y_da, w_gate, w_lift, w_out, rows_per_mod):
    T = x.shape[0]
    tm = 512
    per = rows_per_mod // tm
    row = lambda i: (i, 0)
    const2 = lambda i: (0, 0)
    return pl.pallas_call(
        _mid_kernel,
        grid=(T // tm,),
        in_specs=[
            pl.BlockSpec((tm, D_MODEL), row),
            pl.BlockSpec((1, D_MODEL), const2),
            pl.BlockSpec((1, 1, 6 * D_MODEL), lambda i: (i // per, 0, 0)),
            pl.BlockSpec((tm, BRANCH_W), row),
            pl.BlockSpec((tm, BRANCH_W), row),
            pl.BlockSpec((tm, BRANCH_W), row),
            pl.BlockSpec((D_MODEL, 3 * D_MODEL), const2),
            pl.BlockSpec((3, BRANCH_W, D_MODEL), lambda i: (0, 0, 0)),
            pl.BlockSpec((D_MODEL, D_MODEL), const2),
        ],
        out_specs=pl.BlockSpec((tm, D_MODEL), row),
        out_shape=jax.ShapeDtypeStruct((T, D_MODEL), F32),
        compiler_params=_cparams("arbitrary"),
        name="merge_out",
    )(x, g, mod, y_hy, y_na, y_da, w_gate, w_lift, w_out)


FFN_CHUNK = D_FF // 2


def _ffn_kernel(x_ref, g_ref, mod_ref, w1g_ref, w1u_ref, w2_ref, gf_ref, o_ref, h_ref, acc_ref, *, final):
    k = pl.program_id(1)

    @pl.when(k == 0)
    def _():
        m = mod_ref[0]
        h = _modnorm(x_ref[...], g_ref[...], m[:, 3 * D_MODEL:4 * D_MODEL], m[:, 4 * D_MODEL:5 * D_MODEL])
        h_ref[...] = h.astype(BF16)

    h = h_ref[...]
    a = jnp.dot(h, w1g_ref[...], preferred_element_type=F32)
    b = jnp.dot(h, w1u_ref[...], preferred_element_type=F32)
    part = _bdot(a * _sigmoid(a) * b, w2_ref[...])

    @pl.when(k == 0)
    def _():
        acc_ref[...] = part

    @pl.when(k == 1)
    def _():
        m = mod_ref[0]
        xn = x_ref[...] + m[:, 5 * D_MODEL:6 * D_MODEL] * (acc_ref[...] + part)
        if final:
            xn = _rms(xn, gf_ref[...])
        o_ref[...] = xn


def _ffn(x, g, mod, w_ffn_in, w_ffn_out, g_final, rows_per_mod, final):
    T = x.shape[0]
    tm = 512
    per = rows_per_mod // tm
    return pl.pallas_call(
        functools.partial(_ffn_kernel, final=final),
        grid=(T // tm, 2),
        in_specs=[
            pl.BlockSpec((tm, D_MODEL), lambda i, k: (i, 0)),
            pl.BlockSpec((1, D_MODEL), lambda i, k: (0, 0)),
            pl.BlockSpec((1, 1, 6 * D_MODEL), lambda i, k: (i // per, 0, 0)),
            pl.BlockSpec((D_MODEL, FFN_CHUNK), lambda i, k: (0, k)),
            pl.BlockSpec((D_MODEL, FFN_CHUNK), lambda i, k: (0, 2 + k)),
            pl.BlockSpec((FFN_CHUNK, D_MODEL), lambda i, k: (k, 0)),
            pl.BlockSpec((1, D_MODEL), lambda i, k: (0, 0)),
        ],
        out_specs=pl.BlockSpec((tm, D_MODEL), lambda i, k: (i, 0)),
        out_shape=jax.ShapeDtypeStruct((T, D_MODEL), F32),
        scratch_shapes=[pltpu.VMEM((tm, D_MODEL), BF16), pltpu.VMEM((tm, D_MODEL), F32)],
        compiler_params=_cparams("arbitrary", "arbitrary"),
        name="ffn",
    )(x, g, mod, w_ffn_in, w_ffn_in, w_ffn_out, g_final)


def _da_lambda(lam_ref, lam_init):
    lp = lam_ref[...]
    a = jnp.sum(lp[0:1] * lp[1:2], axis=1, keepdims=True)
    b = jnp.sum(lp[2:3] * lp[3:4], axis=1, keepdims=True)
    return jnp.exp(a) - jnp.exp(b) + lam_init


def _softmax_rows(s):
    m = jnp.max(s, axis=-1, keepdims=True)
    p = jnp.exp(s - m)
    return p, jnp.sum(p, axis=-1, keepdims=True)


def _ctx_attn_kernel(nq_ref, nk_ref, nv_ref, dq_ref, dk_ref, dv_ref, lam_ref, sub_ref, yn_ref, yd_ref, *, lam_init):
    lam = _da_lambda(lam_ref, lam_init)
    na_scale = NA_HEAD_DIM ** -0.5
    da_scale = DA_HEAD_DIM ** -0.5
    for h in range(NA_HEADS):
        sl = slice(h * NA_HEAD_DIM, (h + 1) * NA_HEAD_DIM)
        s = _bdot_nt(nq_ref[:, sl], nk_ref[:, sl]) * na_scale
        p, l = _softmax_rows(s)
        yn_ref[:, sl] = _bdot(p, nv_ref[:, sl]) / l
    for h in range(DA_HEADS):
        ps = []
        for i in range(2):
            sl = slice((2 * h + i) * DA_HEAD_DIM, (2 * h + i + 1) * DA_HEAD_DIM)
            s = _bdot_nt(dq_ref[:, sl], dk_ref[:, sl]) * da_scale
            p, l = _softmax_rows(s)
            ps.append(p / l)
        a = ps[0] - lam * ps[1]
        sl = slice(h * DA_V_DIM, (h + 1) * DA_V_DIM)
        o = _bdot(a, dv_ref[:, sl])
        yd_ref[:, sl] = _rms(o, sub_ref[...]) * (1.0 - lam_init)


def _ctx_attention(u, da_lambda, da_subln, lam_init):
    col = lambda j: pl.BlockSpec((SEQ, BRANCH_W), lambda b, j=j: (b, j))
    out = pl.BlockSpec((SEQ, BRANCH_W), lambda b: (b, 0))
    shape = jax.ShapeDtypeStruct((BATCH * SEQ, BRANCH_W), F32)
    return pl.pallas_call(
        functools.partial(_ctx_attn_kernel, lam_init=lam_init),
        grid=(BATCH,),
        in_specs=[col(3), col(4), col(5), col(6), col(7), col(8),
                  pl.BlockSpec((4, DA_HEAD_DIM), lambda b: (0, 0)),
                  pl.BlockSpec((1, DA_V_DIM), lambda b: (0, 0))],
        out_specs=[out, out],
        out_shape=[shape, shape],
        compiler_params=_cparams("arbitrary"),
        name="ctx_attention",
    )(u, u, u, u, u, u, da_lambda, da_subln)


def _rpb_table_kernel(rpb_ref, o_ref):
    qc = lax.broadcasted_iota(jnp.int32, (GRID_W, GRID_W), 0)
    kc = lax.broadcasted_iota(jnp.int32, (GRID_W, GRID_W), 1)
    dc = jnp.clip(kc - qc, -(NA_WIN_COLS - 1), NA_WIN_COLS - 1) + (NA_WIN_COLS - 1)
    r = rpb_ref[0, 0]
    for dr in range(2 * NA_WIN_ROWS - 1):
        acc = jnp.zeros((GRID_W, GRID_W), F32)
        for d in range(2 * NA_WIN_COLS - 1):
            acc = jnp.where(dc == d, r[dr:dr + 1, d:d + 1], acc)
        o_ref[0, 0, dr] = acc


def _rpb_table(na_rpb):
    n_dr, n_dc = 2 * NA_WIN_ROWS - 1, 2 * NA_WIN_COLS - 1
    return pl.pallas_call(
        _rpb_table_kernel,
        grid=(DEPTH, NA_HEADS),
        in_specs=[pl.BlockSpec((1, 1, n_dr, n_dc), lambda l, h: (l, h, 0, 0))],
        out_specs=pl.BlockSpec((1, 1, n_dr, GRID_W, GRID_W), lambda l, h: (l, h, 0, 0, 0)),
        out_shape=jax.ShapeDtypeStruct((DEPTH, NA_HEADS, n_dr, GRID_W, GRID_W), F32),
        compiler_params=_cparams("arbitrary", "arbitrary"),
        name="rpb_table",
    )(na_rpb)


def _na_kernel(q_ref, k_ref, v_ref, kc_ref, vc_ref, bias_ref, o_ref):
    r = pl.program_id(1)
    r0 = jnp.clip(r - NA_WIN_ROWS // 2, 0, GRID_H - NA_WIN_ROWS)
    start = pl.multiple_of(r0 * GRID_W, GRID_W)
    n_lat = NA_WIN_ROWS * GRID_W
    kwin = k_ref[pl.ds(start, n_lat), :]
    vwin = v_ref[pl.ds(start, n_lat), :]
    qc = lax.broadcasted_iota(jnp.int32, (GRID_W, n_lat), 0)
    kcol = lax.broadcasted_iota(jnp.int32, (GRID_W, n_lat), 1) % GRID_W
    c0 = jnp.clip(qc - NA_WIN_COLS // 2, 0, GRID_W - NA_WIN_COLS)
    col_ok = (kcol >= c0) & (kcol < c0 + NA_WIN_COLS)
    scale = NA_HEAD_DIM ** -0.5
    dr0 = r0 - r + (NA_WIN_ROWS - 1)
    for h in range(NA_HEADS):
        sl = slice(h * NA_HEAD_DIM, (h + 1) * NA_HEAD_DIM)
        q = q_ref[:, sl]
        bias = jnp.concatenate([bias_ref[h, dr0 + w] for w in range(NA_WIN_ROWS)], axis=1)
        s_lat = _bdot_nt(q, kwin[:, sl]) * scale + bias
        s_lat = jnp.where(col_ok, s_lat, NEG_INF)
        s_ctx = _bdot_nt(q, kc_ref[0, :, sl]) * scale
        m = jnp.maximum(jnp.max(s_lat, axis=-1, keepdims=True), jnp.max(s_ctx, axis=-1, keepdims=True))
        p_lat = jnp.exp(s_lat - m)
        p_ctx = jnp.exp(s_ctx - m)
        l = jnp.sum(p_lat, axis=-1, keepdims=True) + jnp.sum(p_ctx, axis=-1, keepdims=True)
        o = _bdot(p_lat, vwin[:, sl]) + _bdot(p_ctx, vc_ref[0, :, sl])
        o_ref[:, sl] = o / l


def _nbr_attention(u, k_ctx, v_ctx, bias):
    n_dr = 2 * NA_WIN_ROWS - 1
    return pl.pallas_call(
        _na_kernel,
        grid=(DEC_BATCH, GRID_H),
        in_specs=[
            pl.BlockSpec((GRID_W, BRANCH_W), lambda b, r: (b * GRID_H + r, 3)),
            pl.BlockSpec((DEC_SEQ, BRANCH_W), lambda b, r: (b, 4)),
            pl.BlockSpec((DEC_SEQ, BRANCH_W), lambda b, r: (b, 5)),
            pl.BlockSpec((1, PAST_LEN, BRANCH_W), lambda b, r: (b, 0, 0)),
            pl.BlockSpec((1, PAST_LEN, BRANCH_W), lambda b, r: (b, 0, 0)),
            pl.BlockSpec((NA_HEADS, n_dr, GRID_W, GRID_W), lambda b, r: (0, 0, 0, 0)),
        ],
        out_specs=pl.BlockSpec((GRID_W, BRANCH_W), lambda b, r: (b * GRID_H + r, 0)),
        out_shape=jax.ShapeDtypeStruct((DEC_BATCH * DEC_SEQ, BRANCH_W), F32),
        compiler_params=_cparams("arbitrary", "arbitrary"),
        name="nbr_attention",
    )(u, u, u, k_ctx, v_ctx, bias)


DA_TQ = 256
DA_KEYS = DEC_SEQ + PAST_LEN


def _rope(x, cos, sin_signed):
    n = x.shape[-1]
    lane = lax.broadcasted_iota(jnp.int32, x.shape, 1)
    partner = jnp.where(lane % 2 == 0, pltpu.roll(x, n - 1, axis=1), pltpu.roll(x, 1, axis=1))
    return x * cos + partner * sin_signed


def _da_prep_kernel(q_ref, k_ref, v_ref, kc_ref, vc_ref, cos_ref, sin_ref, qo_ref, kt_ref, vo_ref):
    t = pl.program_id(1)

    @pl.when(t < DEC_SEQ // DA_TQ)
    def _():
        cos = cos_ref[...]
        sin = sin_ref[...]
        q = _rope(q_ref[...].astype(F32), cos, sin) * (DA_HEAD_DIM ** -0.5)
        qo_ref[...] = q.astype(BF16)
        k = _rope(k_ref[...].astype(F32), cos, sin)
        kt_ref[0] = k.T.astype(BF16)
        vo_ref[0] = v_ref[...].astype(BF16)

    @pl.when(t == DEC_SEQ // DA_TQ)
    def _():
        kt_ref[0] = kc_ref[0].T.astype(BF16)
        vo_ref[0] = vc_ref[0].astype(BF16)


def _da_prep(u, k_ctx, v_ctx, cos, sin):
    nt = DEC_SEQ // DA_TQ
    last = nt - 1
    rowblk = lambda j: pl.BlockSpec((DA_TQ, BRANCH_W), lambda b, t, j=j: (b * nt + jnp.minimum(t, last), j))
    tab = pl.BlockSpec((DA_TQ, BRANCH_W), lambda b, t: (jnp.minimum(t, last), 0))
    ctx = pl.BlockSpec((1, PAST_LEN, BRANCH_W), lambda b, t: (b, 0, 0))
    return pl.pallas_call(
        _da_prep_kernel,
        grid=(DEC_BATCH, nt + 1),
        in_specs=[rowblk(6), rowblk(7), rowblk(8), ctx, ctx, tab, tab],
        out_specs=[
            pl.BlockSpec((DA_TQ, BRANCH_W), lambda b, t: (b * nt + jnp.minimum(t, last), 0)),
            pl.BlockSpec((1, BRANCH_W, DA_TQ), lambda b, t: (b, 0, t)),
            pl.BlockSpec((1, DA_TQ, BRANCH_W), lambda b, t: (b, t, 0)),
        ],
        out_shape=[
            jax.ShapeDtypeStruct((DEC_BATCH * DEC_SEQ, BRANCH_W), BF16),
            jax.ShapeDtypeStruct((DEC_BATCH, BRANCH_W, DA_KEYS), BF16),
            jax.ShapeDtypeStruct((DEC_BATCH, DA_KEYS, BRANCH_W), BF16),
        ],
        compiler_params=_cparams("arbitrary", "arbitrary"),
        name="da_prep",
    )(u, u, u, k_ctx, v_ctx, cos, sin)


def _da_kernel(q_ref, kt_ref, v_ref, lam_ref, sub_ref, o_ref, *, lam_init):
    lam = _da_lambda(lam_ref, lam_init)
    for h in range(DA_HEADS):
        ps = []
        for i in range(2):
            c = (2 * h + i) * DA_HEAD_DIM
            s = jnp.dot(q_ref[:, c:c + DA_HEAD_DIM], kt_ref[0, c:c + DA_HEAD_DIM, :], preferred_element_type=F32)
            p, l = _softmax_rows(s)
            ps.append((p, l))
        a = ps[0][0] * (1.0 / ps[0][1]) - ps[1][0] * (lam / ps[1][1])
        sl = slice(h * DA_V_DIM, (h + 1) * DA_V_DIM)
        o = jnp.dot(a.astype(BF16), v_ref[0, :, sl], preferred_element_type=F32)
        o_ref[:, sl] = _rms(o, sub_ref[...]) * (1.0 - lam_init)


def _diff_attention(q, kt, v, da_lambda, da_subln, lam_init):
    nt = DEC_SEQ // DA_TQ
    return pl.pallas_call(
        functools.partial(_da_kernel, lam_init=lam_init),
        grid=(DEC_BATCH, nt),
        in_specs=[
            pl.BlockSpec((DA_TQ, BRANCH_W), lambda b, t: (b * nt + t, 0)),
            pl.BlockSpec((1, BRANCH_W, DA_KEYS), lambda b, t: (b, 0, 0)),
            pl.BlockSpec((1, DA_KEYS, BRANCH_W), lambda b, t: (b, 0, 0)),
            pl.BlockSpec((4, DA_HEAD_DIM), lambda b, t: (0, 0)),
            pl.BlockSpec((1, DA_V_DIM), lambda b, t: (0, 0)),
        ],
        out_specs=pl.BlockSpec((DA_TQ, BRANCH_W), lambda b, t: (b * nt + t, 0)),
        out_shape=jax.ShapeDtypeStruct((DEC_BATCH * DEC_SEQ, BRANCH_W), F32),
        compiler_params=_cparams("arbitrary", "arbitrary"),
        name="diff_attention",
    )(q, kt, v, da_lambda, da_subln)


def _rope_tables():
    pos = np.arange(DEC_SEQ)
    row = (pos // GRID_W).astype(np.float32)
    col = (pos % GRID_W).astype(np.float32)
    n_freq = DA_HEAD_DIM // 4
    inv = (np.float32(ROPE_BASE) ** (-np.arange(n_freq, dtype=np.float32) / n_freq)).astype(np.float32)
    ang = np.concatenate([row[:, None] * inv[None, :], col[:, None] * inv[None, :]], axis=-1)
    ang = ang.astype(np.float64)
    cos = np.repeat(np.cos(ang), 2, axis=-1)
    sin = np.repeat(np.sin(ang), 2, axis=-1)
    sign = np.where(np.arange(DA_HEAD_DIM) % 2 == 0, -1.0, 1.0)
    reps = BRANCH_W // DA_HEAD_DIM
    cos = np.tile(cos, (1, reps)).astype(np.float32)
    sin = np.tile(sin * sign[None, :], (1, reps)).astype(np.float32)
    return jnp.asarray(cos), jnp.asarray(sin)


def _filt_kernel(feat_ref, w1_ref, b1_ref, w2_ref, b2_ref, fr_ref, w3f_ref, w3b_ref, dec_ref, hf_ref, hb_ref):
    fr = fr_ref[0]
    h = jnp.sin(fr * (jnp.dot(feat_ref[...], w1_ref[0], precision=HIGHEST, preferred_element_type=F32) + b1_ref[0]))
    h = jnp.sin(fr * (jnp.dot(h, w2_ref[0], precision=HIGHEST, preferred_element_type=F32) + b2_ref[0]))
    dec = dec_ref[...]
    hf = jnp.dot(h, w3f_ref[0], precision=HIGHEST, preferred_element_type=F32) * dec
    hb = jnp.dot(h, w3b_ref[0], precision=HIGHEST, preferred_element_type=F32) * dec
    row = lax.broadcasted_iota(jnp.int32, hb.shape, 0)
    hb = jnp.where(row == 0, 0.0, hb)
    nrm = jnp.sum(jnp.abs(hf), axis=0, keepdims=True) + jnp.sum(jnp.abs(hb), axis=0, keepdims=True)
    hf_ref[0, 0] = hf / nrm
    hb_ref[0, 0] = hb / nrm


def _hyena_pos_tables(L):
    f32 = np.float32
    pos = np.arange(L, dtype=f32)
    t = (pos / f32(L)).astype(f32)
    bands = np.linspace(1e-4, HY_POS_BANDS - 1, HY_POS_BANDS, dtype=f32)
    ang = (f32(2 * math.pi / L) * pos[:, None] * bands[None, :]).astype(np.float64)
    feats = np.zeros((L, HY_FILT_HIDDEN), f32)
    feats[:, 0] = t
    feats[:, 1:1 + HY_POS_BANDS] = np.cos(ang)
    feats[:, 1 + HY_POS_BANDS:HY_POS_DIM] = -np.sin(ang)
    deltas = np.linspace(math.log(HY_DECAY_TARGET) / HY_SLOW_DECAY,
                         math.log(HY_DECAY_TARGET) / HY_FAST_DECAY, BRANCH_W, dtype=f32)
    decay = np.exp((-t[:, None] * np.abs(deltas)[None, :]).astype(np.float64)).astype(f32)
    return jnp.asarray(feats), jnp.asarray(decay)


def _hyena_filters(L, w1p, b1, w2, b2, w3, freq):
    feats, decay = _hyena_pos_tables(L)
    cb = 128
    ncb = BRANCH_W // cb
    small = lambda shape: pl.BlockSpec((1,) + shape, lambda l, o, c: (l, 0, 0))
    shape = jax.ShapeDtypeStruct((DEPTH, 2, L, BRANCH_W), F32)
    out = pl.BlockSpec((1, 1, L, cb), lambda l, o, c: (l, o, 0, c))
    return pl.pallas_call(
        _filt_kernel,
        grid=(DEPTH, 2, ncb),
        in_specs=[
            pl.BlockSpec((L, HY_FILT_HIDDEN), lambda l, o, c: (0, 0)),
            small((HY_FILT_HIDDEN, HY_FILT_HIDDEN)), small((1, HY_FILT_HIDDEN)),
            small((HY_FILT_HIDDEN, HY_FILT_HIDDEN)), small((1, HY_FILT_HIDDEN)),
            small((1, HY_FILT_HIDDEN)),
            pl.BlockSpec((1, HY_FILT_HIDDEN, cb), lambda l, o, c: (l, 0, o * 2 * ncb + c)),
            pl.BlockSpec((1, HY_FILT_HIDDEN, cb), lambda l, o, c: (l, 0, o * 2 * ncb + ncb + c)),
            pl.BlockSpec((L, cb), lambda l, o, c: (0, c)),
        ],
        out_specs=[out, out],
        out_shape=[shape, shape],
        compiler_params=_cparams("arbitrary", "arbitrary", "arbitrary"),
        name=f"hyena_filters_{L}",
    )(feats, w1p, b1, w2, b2, freq, w3, w3, decay)


def _short_conv(u, w_ref, b_ref, seq_len):
    n = u.shape[0]
    t = lax.broadcasted_iota(jnp.int32, u.shape, 0) % seq_len
    prev = jnp.where(t == 0, 0.0, pltpu.roll(u, 1, axis=0))
    nxt = jnp.where(t == seq_len - 1, 0.0, pltpu.roll(u, n - 1, axis=0))
    return prev * w_ref[0:1, :] + u * w_ref[1:2, :] + nxt * w_ref[2:3, :] + b_ref[...]


def _dft_direct_mats():
    n, half = 2 * SEQ, SEQ
    k = np.arange(n)[:, None].astype(np.float64)
    t = np.arange(half)[None, :].astype(np.float64)
    ang = 2 * np.pi * k * t / n
    fr, fi = np.cos(ang), -np.sin(ang)
    mf = np.block([[fr, -fi], [fi, fr]])
    gr, gi = np.cos(ang).T / n, np.sin(ang).T / n
    mi = np.block([[gr, -gi], [gi, gr]])
    return mf.astype(np.float32), mi.astype(np.float32)


def _spec_direct_kernel(hf_ref, hb_ref, m_ref, o_ref):
    m = m_ref[...]
    wf = jnp.dot(m, hf_ref[0, 0], precision=HIGHEST, preferred_element_type=F32)
    wb = jnp.dot(m, hb_ref[0, 0], precision=HIGHEST, preferred_element_type=F32)
    n = 2 * SEQ
    o_ref[0, 0, 0:n] = wf[0:n] + wb[0:n]
    o_ref[0, 0, n:2 * n] = wf[n:2 * n] - wb[n:2 * n]


def _spec_direct(hf, hb, mf_real):
    n = 2 * SEQ
    blk = pl.BlockSpec((1, 1, SEQ, BRANCH_W), lambda l, o: (l, o, 0, 0))
    return pl.pallas_call(
        _spec_direct_kernel,
        grid=(DEPTH, 2),
        in_specs=[blk, blk, pl.BlockSpec((2 * n, SEQ), lambda l, o: (0, 0))],
        out_specs=pl.BlockSpec((1, 1, 2 * n, BRANCH_W), lambda l, o: (l, o, 0, 0)),
        out_shape=jax.ShapeDtypeStruct((DEPTH, 2, 2 * n, BRANCH_W), F32),
        compiler_params=_cparams("arbitrary", "arbitrary"),
        name="hyena_spectrum_direct",
    )(hf, hb, mf_real)


def _lconv_direct_kernel(s_ref, g_ref, cws_ref, cbs_ref, cwg_ref, cbg_ref, h_ref, bias_ref, mf_ref, mi_ref, o_ref,
                         *, conv_sig):
    n = 2 * SEQ
    sig = s_ref[...]
    if conv_sig:
        sig = _short_conv(sig, cws_ref, cbs_ref, SEQ)
    gate = _short_conv(g_ref[...], cwg_ref, cbg_ref, SEQ)
    z = jnp.dot(mf_ref[...], sig.astype(BF16), preferred_element_type=F32)
    zr, zi = z[0:n], z[n:2 * n]
    hr, hi = h_ref[0:n], h_ref[n:2 * n]
    y = jnp.concatenate([zr * hr - zi * hi, zr * hi + zi * hr], axis=0)
    y = jnp.dot(mi_ref[...], y.astype(BF16), preferred_element_type=F32)
    o_ref[...] = gate * (y + sig * bias_ref[...])


def _lconv_direct(sig, sig_col, gate_src, gate_col, conv_w, conv_b, spec, bias, mf, mi, conv_sig):
    n = 2 * SEQ
    rows = 2 * SEQ
    T = sig.shape[0]
    return pl.pallas_call(
        functools.partial(_lconv_direct_kernel, conv_sig=conv_sig),
        grid=(T // rows,),
        in_specs=[
            pl.BlockSpec((rows, BRANCH_W), lambda p: (p, sig_col)),
            pl.BlockSpec((rows, BRANCH_W), lambda p: (p, gate_col)),
            pl.BlockSpec((3, BRANCH_W), lambda p: (0, 0)),
            pl.BlockSpec((1, BRANCH_W), lambda p: (0, 0)),
            pl.BlockSpec((3, BRANCH_W), lambda p: (0, gate_col)),
            pl.BlockSpec((1, BRANCH_W), lambda p: (0, gate_col)),
            pl.BlockSpec((2 * n, BRANCH_W), lambda p: (0, 0)),
            pl.BlockSpec((1, BRANCH_W), lambda p: (0, 0)),
            pl.BlockSpec((2 * n, rows), lambda p: (0, 0)),
            pl.BlockSpec((rows, 2 * n), lambda p: (0, 0)),
        ],
        out_specs=pl.BlockSpec((rows, BRANCH_W), lambda p: (p, 0)),
        out_shape=jax.ShapeDtypeStruct((T, BRANCH_W), F32),
        compiler_params=_cparams("arbitrary"),
        name="hyena_lconv_direct",
    )(sig, gate_src, conv_w, conv_b, conv_w, conv_b, spec, bias, mf, mi)


def _dft_two_stage_mats():
    no, ni, half, n = FFT_NO, FFT_NI, FFT_HALF, FFT_N
    f64 = np.float64
    k1 = np.arange(no, dtype=f64)
    n_o = np.arange(half, dtype=f64)
    n_i = np.arange(ni, dtype=f64)
    ang = 2 * np.pi * (n_i[:, None, None] * k1[None, :, None] / n + k1[None, :, None] * n_o[None, None, :] / no)
    tr, ti = np.cos(ang), -np.sin(ang)
    m1 = np.concatenate([np.concatenate([tr, -ti], axis=2), np.concatenate([ti, tr], axis=2)], axis=1)
    k2 = np.arange(ni, dtype=f64)
    ang2 = 2 * np.pi * k2[:, None] * n_i[None, :] / ni
    f2r, f2i = np.cos(ang2), -np.sin(ang2)
    m2 = np.block([[f2r, -f2i], [f2i, f2r]])
    m2c = np.block([[f2r, f2i], [-f2i, f2r]])
    sr, si = np.transpose(tr, (0, 2, 1)) / n, -np.transpose(ti, (0, 2, 1)) / n
    m3 = np.concatenate([np.concatenate([sr, -si], axis=2), np.concatenate([si, sr], axis=2)], axis=1)
    return (m1.astype(np.float32), m2.astype(np.float32), m2c.astype(np.float32), m3.astype(np.float32))


def _fwd_stage1(za_ref, zb_ref, m1_ref, w_ref):
    def body(ni, carry):
        a = za_ref[pl.ds(ni, FFT_HALF, stride=FFT_NI), :]
        if zb_ref is None:
            out = jnp.dot(m1_ref[ni][:, 0:FFT_HALF], a.astype(m1_ref.dtype), preferred_element_type=F32,
                          precision=HIGHEST if m1_ref.dtype == F32 else None)
        else:
            b = zb_ref[pl.ds(ni, FFT_HALF, stride=FFT_NI), :]
            out = jnp.dot(m1_ref[ni], jnp.concatenate([a, b], axis=0).astype(BF16), preferred_element_type=F32)
        w_ref[pl.ds(ni, FFT_NO, stride=2 * FFT_NI), :] = out[0:FFT_NO]
        w_ref[pl.ds(FFT_NI + ni, FFT_NO, stride=2 * FFT_NI), :] = out[FFT_NO:2 * FFT_NO]
        return carry

    lax.fori_loop(0, FFT_NI, body, 0)


def _spec_two_stage_kernel(hf_ref, hb_ref, m1_ref, m2_ref, o_ref, wf_ref, wb_ref):
    _fwd_stage1(hf_ref.at[0, 0], None, m1_ref, wf_ref)
    _fwd_stage1(hb_ref.at[0, 0], None, m1_ref, wb_ref)
    blk = 2 * FFT_NI

    def body(k1, carry):
        rows = pl.ds(pl.multiple_of(k1 * blk, blk), blk)
        xf = jnp.dot(m2_ref[...], wf_ref[rows, :], precision=HIGHEST, preferred_element_type=F32)
        xb = jnp.dot(m2_ref[...], wb_ref[rows, :], precision=HIGHEST, preferred_element_type=F32)
        o_ref[0, 0, rows, :] = jnp.concatenate(
            [xf[0:FFT_NI] + xb[0:FFT_NI], xf[FFT_NI:blk] - xb[FFT_NI:blk]], axis=0)
        return carry

    lax.fori_loop(0, FFT_NO, body, 0)


LCONV_CB = 128


def _spec_two_stage(hf, hb, m1, m2):
    cb = LCONV_CB
    blk = pl.BlockSpec((1, 1, DEC_SEQ, cb), lambda l, o, c: (l, o, 0, c))
    return pl.pallas_call(
        _spec_two_stage_kernel,
        grid=(DEPTH, 2, BRANCH_W // cb),
        in_specs=[blk, blk,
                  pl.BlockSpec((FFT_NI, 2 * FFT_NO, 2 * FFT_HALF), lambda l, o, c: (0, 0, 0)),
                  pl.BlockSpec((2 * FFT_NI, 2 * FFT_NI), lambda l, o, c: (0, 0))],
        out_specs=pl.BlockSpec((1, 1, 2 * FFT_N, cb), lambda l, o, c: (l, o, 0, c)),
        out_shape=jax.ShapeDtypeStruct((DEPTH, 2, 2 * FFT_N, BRANCH_W), F32),
        scratch_shapes=[pltpu.VMEM((2 * FFT_N, cb), F32), pltpu.VMEM((2 * FFT_N, cb), F32)],
        compiler_params=_cparams("arbitrary", "arbitrary", "arbitrary"),
        name="hyena_spectrum_two_stage",
    )(hf, hb, m1, m2)


def _lconv_two_stage_kernel(s_ref, g_ref, cws_ref, cbs_ref, cwg_ref, cbg_ref, h_ref, bias_ref,
                            m1_ref, m2_ref, m2c_ref, m3_ref, o_ref, z_ref, w_ref, *, conv_sig):
    for b in range(2):
        sig = s_ref[b].astype(F32)
        if conv_sig:
            sig = _short_conv(sig, cws_ref, cbs_ref, DEC_SEQ)
        z_ref[b] = sig
    _fwd_stage1(z_ref.at[0], z_ref.at[1], m1_ref, w_ref)
    blk = 2 * FFT_NI

    def mid(k1, carry):
        rows = pl.ds(pl.multiple_of(k1 * blk, blk), blk)
        x = jnp.dot(m2_ref[...], w_ref[rows, :].astype(BF16), preferred_element_type=F32)
        h = h_ref[rows, :]
        xr, xi = x[0:FFT_NI], x[FFT_NI:blk]
        hr, hi = h[0:FFT_NI], h[FFT_NI:blk]
        y = jnp.concatenate([xr * hr - xi * hi, xr * hi + xi * hr], axis=0)
        w_ref[rows, :] = jnp.dot(m2c_ref[...], y.astype(BF16), preferred_element_type=F32)
        return carry

    lax.fori_loop(0, FFT_NO, mid, 0)

    def last(ni, carry):
        cr = w_ref[pl.ds(ni, FFT_NO, stride=blk), :]
        ci = w_ref[pl.ds(FFT_NI + ni, FFT_NO, stride=blk), :]
        y = jnp.dot(m3_ref[ni], jnp.concatenate([cr, ci], axis=0).astype(BF16), preferred_element_type=F32)
        o_ref[0, pl.ds(ni, FFT_HALF, stride=FFT_NI), :] = y[0:FFT_HALF]
        o_ref[1, pl.ds(ni, FFT_HALF, stride=FFT_NI), :] = y[FFT_HALF:2 * FFT_HALF]
        return carry

    lax.fori_loop(0, FFT_NI, last, 0)
    for b in range(2):
        gate = _short_conv(g_ref[b].astype(F32), cwg_ref, cbg_ref, DEC_SEQ)
        sig = z_ref[b]
        o_ref[b] = gate * (o_ref[b] + sig * bias_ref[...])


def _lconv_two_stage(sig, sig_col, gate_src, gate_col, conv_w, conv_b, spec, bias, mats, conv_sig):
    cb = LCONV_CB
    ncb = BRANCH_W // cb
    m1, m2, m2c, m3 = mats
    const3 = lambda c, p: (0, 0, 0)
    const2 = lambda c, p: (0, 0)
    return pl.pallas_call(
        functools.partial(_lconv_two_stage_kernel, conv_sig=conv_sig),
        grid=(ncb, DEC_BATCH // 2),
        in_specs=[
            pl.BlockSpec((2, DEC_SEQ, cb), lambda c, p: (p, 0, sig_col * ncb + c)),
            pl.BlockSpec((2, DEC_SEQ, cb), lambda c, p: (p, 0, gate_col * ncb + c)),
            pl.BlockSpec((3, cb), lambda c, p: (0, c)),
            pl.BlockSpec((1, cb), lambda c, p: (0, c)),
            pl.BlockSpec((3, cb), lambda c, p: (0, gate_col * ncb + c)),
            pl.BlockSpec((1, cb), lambda c, p: (0, gate_col * ncb + c)),
            pl.BlockSpec((2 * FFT_N, cb), lambda c, p: (0, c)),
            pl.BlockSpec((1, cb), lambda c, p: (0, c)),
            pl.BlockSpec(m1.shape, const3),
            pl.BlockSpec(m2.shape, const2),
            pl.BlockSpec(m2c.shape, const2),
            pl.BlockSpec(m3.shape, const3),
        ],
        out_specs=pl.BlockSpec((2, DEC_SEQ, cb), lambda c, p: (p, 0, c)),
        out_shape=jax.ShapeDtypeStruct((DEC_BATCH, DEC_SEQ, BRANCH_W), F32),
        scratch_shapes=[pltpu.VMEM((2, DEC_SEQ, cb), F32), pltpu.VMEM((2 * FFT_N, cb), F32)],
        compiler_params=_cparams("arbitrary", "arbitrary"),
        name="hyena_lconv_two_stage",
    )(sig, gate_src, conv_w, conv_b, conv_w, conv_b, spec, bias, m1, m2, m2c, m3)


def kernel(x_prompt, x_sample, cache_na_k, cache_na_v, cache_da_k, cache_da_v, c, c_ctx, w_ada, b_ada, norm_mix,
           norm_ffn, w_in, hy_conv_w, hy_conv_b, hy_filt_w1, hy_filt_b1, hy_filt_w2, hy_filt_b2, hy_filt_w3,
           hy_filt_freq, hy_bias, na_rpb, da_lambda, da_subln, w_lift, w_out, w_ffn_in, w_ffn_out, norm_final):
    TP, TS = BATCH * SEQ, DEC_BATCH * DEC_SEQ
    xp = x_prompt.reshape(TP, D_MODEL)
    xs = x_sample.reshape(TS, D_MODEL)

    cc = jnp.concatenate([c_ctx[None, :], c, jnp.zeros((8 - 1 - DEC_BATCH, D_MODEL), F32)], axis=0)
    mod = _modulation(cc, w_ada, b_ada)
    mod_p = mod[:, 0:1].reshape(DEPTH, 1, 1, 6 * D_MODEL)
    mod_s = mod[:, 1:1 + DEC_BATCH].reshape(DEPTH, DEC_BATCH, 1, 6 * D_MODEL)

    w_mix = w_in[:, :, :MIX_W].astype(BF16)
    w_gate = w_in[:, :, MIX_W:].astype(BF16)
    w_lift_b = w_lift.astype(BF16)
    w_out_b = w_out.astype(BF16)
    w_ffn_in_b = w_ffn_in.astype(BF16)
    w_ffn_out_b = w_ffn_out.astype(BF16)
    g_mix = norm_mix.reshape(DEPTH, 1, D_MODEL)
    g_ffn = norm_ffn.reshape(DEPTH, 1, D_MODEL)
    g_fin = norm_final.reshape(1, D_MODEL)
    subln = da_subln.reshape(DEPTH, 1, DA_V_DIM)

    w1p = jnp.pad(hy_filt_w1, ((0, 0), (0, HY_FILT_HIDDEN - HY_POS_DIM), (0, 0)))
    b1 = hy_filt_b1.reshape(DEPTH, 1, HY_FILT_HIDDEN)
    b2 = hy_filt_b2.reshape(DEPTH, 1, HY_FILT_HIDDEN)
    fr = hy_filt_freq.reshape(DEPTH, 1, HY_FILT_HIDDEN)
    mf, mi = _dft_direct_mats()
    mats = _dft_two_stage_mats()
    hf_p, hb_p = _hyena_filters(SEQ, w1p, b1, hy_filt_w2, b2, hy_filt_w3, fr)
    hf_s, hb_s = _hyena_filters(DEC_SEQ, w1p, b1, hy_filt_w2, b2, hy_filt_w3, fr)
    spec_p = _spec_direct(hf_p, hb_p, jnp.asarray(mf[:, 0:SEQ]))
    spec_s = _spec_two_stage(hf_s, hb_s, jnp.asarray(mats[0]), jnp.asarray(mats[1]))
    mf_b, mi_b = jnp.asarray(mf, dtype=BF16), jnp.asarray(mi, dtype=BF16)
    mats_b = tuple(jnp.asarray(m, dtype=BF16) for m in mats)
    conv_b = hy_conv_b.reshape(DEPTH, 1, 3 * BRANCH_W)

    rpb = _rpb_table(na_rpb)
    cos, sin = _rope_tables()
    ck_na = cache_na_k.reshape(DEC_BATCH, DEPTH, PAST_LEN, BRANCH_W)
    cv_na = cache_na_v.reshape(DEC_BATCH, DEPTH, PAST_LEN, BRANCH_W)
    ck_da = cache_da_k.reshape(DEC_BATCH, DEPTH, PAST_LEN, BRANCH_W)
    cv_da = cache_da_v.reshape(DEC_BATCH, DEPTH, PAST_LEN, BRANCH_W)

    caches = []
    for l in range(DEPTH):
        lam_init = 0.8 - 0.6 * math.exp(-0.3 * l)
        final = l == DEPTH - 1

        u = _in_proj(xp, g_mix[l], mod_p[l], w_mix[l], TP, F32)
        caches.append(u[:, 4 * BRANCH_W:5 * BRANCH_W])
        caches.append(u[:, 5 * BRANCH_W:6 * BRANCH_W])
        caches.append(u[:, 7 * BRANCH_W:8 * BRANCH_W])
        caches.append(u[:, 8 * BRANCH_W:9 * BRANCH_W])
        z1 = _lconv_direct(u, 0, u, 1, hy_conv_w[l], conv_b[l], spec_p[l, 0], hy_bias[l, 0:1], mf_b, mi_b, True)
        y_hy = _lconv_direct(z1, 0, u, 2, hy_conv_w[l], conv_b[l], spec_p[l, 1], hy_bias[l, 1:2], mf_b, mi_b, False)
        y_na, y_da = _ctx_attention(u, da_lambda[l], subln[l], lam_init)
        xp = _merge_out(xp, g_mix[l], mod_p[l], y_hy, y_na, y_da, w_gate[l], w_lift_b[l], w_out_b[l], TP)
        xp = _ffn(xp, g_ffn[l], mod_p[l], w_ffn_in_b[l], w_ffn_out_b[l], g_fin, TP, final)

        u = _in_proj(xs, g_mix[l], mod_s[l], w_mix[l], DEC_SEQ, BF16)
        u3 = u.reshape(DEC_BATCH, DEC_SEQ, MIX_W)
        z1 = _lconv_two_stage(u3, 0, u3, 1, hy_conv_w[l], conv_b[l], spec_s[l, 0], hy_bias[l, 0:1], mats_b, True)
        y_hy = _lconv_two_stage(z1, 0, u3, 2, hy_conv_w[l], conv_b[l], spec_s[l, 1], hy_bias[l, 1:2], mats_b, False)
        y_hy = y_hy.reshape(TS, BRANCH_W)
        y_na = _nbr_attention(u, ck_na[:, l], cv_na[:, l], rpb[l])
        q, kt, v = _da_prep(u, ck_da[:, l], cv_da[:, l], cos, sin)
        y_da = _diff_attention(q, kt, v, da_lambda[l], subln[l], lam_init)
        xs = _merge_out(xs, g_mix[l], mod_s[l], y_hy, y_na, y_da, w_gate[l], w_lift_b[l], w_out_b[l], DEC_SEQ)
        xs = _ffn(xs, g_ffn[l], mod_s[l], w_ffn_in_b[l], w_ffn_out_b[l], g_fin, DEC_SEQ, final)

    def stack(i, last):
        parts = [caches[4 * l + i].reshape(BATCH, SEQ, 8, last) for l in range(DEPTH)]
        return jnp.stack(parts, axis=1)

    y_prompt = xp.reshape(BATCH, SEQ, D_MODEL)
    y_sample = xs.reshape(DEC_BATCH, DEC_SEQ, D_MODEL)
    return (y_prompt, y_sample, stack(0, NA_HEAD_DIM), stack(1, NA_HEAD_DIM),
            stack(2, 2 * DA_HEAD_DIM), stack(3, DA_V_DIM))
```

```python
import functools
import math

import numpy as np
import jax
import jax.numpy as jnp
from jax import lax
from jax.experimental import pallas as pl
from jax.experimental.pallas import tpu as pltpu

F32 = jnp.float32
BF16 = jnp.bfloat16
HIGHEST = lax.Precision.HIGHEST

D_MODEL = 1024
BATCH = 32
SEQ = 256
DEPTH = 4
DEC_BATCH = 4
DEC_SEQ = 4096
PAST_LEN = 256
GRID_W = 64
GRID_H = DEC_SEQ // GRID_W
BRANCH_W = 512
HY_POS_BANDS = 16
HY_POS_DIM = 1 + 2 * HY_POS_BANDS
HY_FILT_HIDDEN = 64
HY_DECAY_TARGET = 1e-2
HY_FAST_DECAY = 0.3
HY_SLOW_DECAY = 1.5
NA_HEADS = 8
NA_HEAD_DIM = 64
NA_WIN_ROWS = 8
NA_WIN_COLS = 16
DA_HEADS = 8
DA_HEAD_DIM = 32
DA_V_DIM = 64
D_FF = 2816
MIX_W = 9 * BRANCH_W
ROPE_BASE = 10000.0
EPS = 1e-6
NEG_INF = -1e30

VMEM_LIMIT_BYTES = 56 * 1024 * 1024

FFT_N = 2 * DEC_SEQ
FFT_NO = 64
FFT_NI = 128
FFT_HALF = FFT_NO // 2
FFT_UNROLL = 4
LCONV_CB = 128


def _cparams(*sem):
    return pltpu.CompilerParams(dimension_semantics=sem, vmem_limit_bytes=VMEM_LIMIT_BYTES)


def _sigmoid(x):
    return 1.0 / (1.0 + jnp.exp(-x))


def _rms(x, g):
    return x * lax.rsqrt(jnp.mean(x * x, axis=-1, keepdims=True) + EPS) * g


def _modnorm(x, g, shift, scale):
    return _rms(x, g) * (1.0 + scale) + shift


def _bdot(a, b):
    return jnp.dot(a.astype(BF16), b.astype(BF16), preferred_element_type=F32)


def _bdot_nt(a, b):
    return lax.dot_general(a.astype(BF16), b.astype(BF16), (((1,), (1,)), ((), ())),
                           preferred_element_type=F32)


def _mod_kernel(c_ref, w_ref, b_ref, o_ref):
    c = c_ref[...]
    s = c * _sigmoid(c)
    o_ref[0] = jnp.dot(s, w_ref[0], precision=HIGHEST, preferred_element_type=F32) + b_ref[0]


def _modulation(cc, w_ada, b_ada):
    nt = 6
    return pl.pallas_call(
        _mod_kernel,
        grid=(DEPTH, nt),
        in_specs=[
            pl.BlockSpec((8, D_MODEL), lambda l, j: (0, 0)),
            pl.BlockSpec((1, D_MODEL, D_MODEL), lambda l, j: (l, 0, j)),
            pl.BlockSpec((1, 1, D_MODEL), lambda l, j: (l, 0, j)),
        ],
        out_specs=pl.BlockSpec((1, 8, D_MODEL), lambda l, j: (l, 0, j)),
        out_shape=jax.ShapeDtypeStruct((DEPTH, 8, 6 * D_MODEL), F32),
        compiler_params=_cparams("arbitrary", "arbitrary"),
        name="modulation",
    )(cc, w_ada, b_ada.reshape(DEPTH, 1, 6 * D_MODEL))


def _in_kernel(x_ref, g_ref, mod_ref, w_ref, o_ref, h_ref):
    @pl.when(pl.program_id(1) == 0)
    def _():
        m = mod_ref[0]
        h = _modnorm(x_ref[...], g_ref[...], m[:, 0:D_MODEL], m[:, D_MODEL:2 * D_MODEL])
        h_ref[...] = h.astype(BF16)

    o_ref[...] = jnp.dot(h_ref[...], w_ref[...], preferred_element_type=F32).astype(o_ref.dtype)


def _in_proj(x, g, mod, w, rows_per_mod, out_dtype):
    T = x.shape[0]
    tm, tn = 1024, 512
    per = rows_per_mod // tm
    return pl.pallas_call(
        _in_kernel,
        grid=(T // tm, MIX_W // tn),
        in_specs=[
            pl.BlockSpec((tm, D_MODEL), lambda i, j: (i, 0)),
            pl.BlockSpec((1, D_MODEL), lambda i, j: (0, 0)),
            pl.BlockSpec((1, 1, 6 * D_MODEL), lambda i, j: (i // per, 0, 0)),
            pl.BlockSpec((D_MODEL, tn), lambda i, j: (0, j)),
        ],
        out_specs=pl.BlockSpec((tm, tn), lambda i, j: (i, j)),
        out_shape=jax.ShapeDtypeStruct((T, MIX_W), out_dtype),
        scratch_shapes=[pltpu.VMEM((tm, D_MODEL), BF16)],
        compiler_params=_cparams("arbitrary", "arbitrary"),
        name="in_proj",
    )(x, g, mod, w)


def _mid_kernel(x_ref, g_ref, mod_ref, yh_ref, yn_ref, yd_ref, wg_ref, wl_ref, wo_ref, o_ref):
    m = mod_ref[0]
    x = x_ref[...]
    h = _modnorm(x, g_ref[...], m[:, 0:D_MODEL], m[:, D_MODEL:2 * D_MODEL]).astype(BF16)
    merged = None
    for br, y_ref in enumerate((yh_ref, yn_ref, yd_ref)):
        gate = _sigmoid(jnp.dot(h, wg_ref[:, br * D_MODEL:(br + 1) * D_MODEL], preferred_element_type=F32))
        lift = jnp.dot(y_ref[...].astype(BF16), wl_ref[br], preferred_element_type=F32)
        t = gate * lift
        merged = t if merged is None else merged + t
    o_ref[...] = x + m[:, 2 * D_MODEL:3 * D_MODEL] * _bdot(merged, wo_ref[...])


def _merge_out(x, g, mod, y_hy, y_na, y_da, w_gate, w_lift, w_out, rows_per_mod):
    T = x.shape[0]
    tm = 512
    per = rows_per_mod // tm
    row = lambda i: (i, 0)
    const2 = lambda i: (0, 0)
    return pl.pallas_call(
        _mid_kernel,
        grid=(T // tm,),
        in_specs=[
            pl.BlockSpec((tm, D_MODEL), row),
            pl.BlockSpec((1, D_MODEL), const2),
            pl.BlockSpec((1, 1, 6 * D_MODEL), lambda i: (i // per, 0, 0)),
            pl.BlockSpec((tm, BRANCH_W), row),
            pl.BlockSpec((tm, BRANCH_W), row),
            pl.BlockSpec((tm, BRANCH_W), row),
            pl.BlockSpec((D_MODEL, 3 * D_MODEL), const2),
            pl.BlockSpec((3, BRANCH_W, D_MODEL), lambda i: (0, 0, 0)),
            pl.BlockSpec((D_MODEL, D_MODEL), const2),
        ],
        out_specs=pl.BlockSpec((tm, D_MODEL), row),
        out_shape=jax.ShapeDtypeStruct((T, D_MODEL), F32),
        compiler_params=_cparams("arbitrary"),
        name="merge_out",
    )(x, g, mod, y_hy, y_na, y_da, w_gate, w_lift, w_out)


FFN_CHUNK = D_FF // 2


def _ffn_kernel(x_ref, g_ref, mod_ref, w1g_ref, w1u_ref, w2_ref, gf_ref, o_ref, h_ref, acc_ref, *, final):
    k = pl.program_id(1)

    @pl.when(k == 0)
    def _():
        m = mod_ref[0]
        h = _modnorm(x_ref[...], g_ref[...], m[:, 3 * D_MODEL:4 * D_MODEL], m[:, 4 * D_MODEL:5 * D_MODEL])
        h_ref[...] = h.astype(BF16)

    h = h_ref[...]
    a = jnp.dot(h, w1g_ref[...], preferred_element_type=F32)
    b = jnp.dot(h, w1u_ref[...], preferred_element_type=F32)
    part = _bdot(a * _sigmoid(a) * b, w2_ref[...])

    @pl.when(k == 0)
    def _():
        acc_ref[...] = part

    @pl.when(k == 1)
    def _():
        m = mod_ref[0]
        xn = x_ref[...] + m[:, 5 * D_MODEL:6 * D_MODEL] * (acc_ref[...] + part)
        if final:
            xn = _rms(xn, gf_ref[...])
        o_ref[...] = xn


def _ffn(x, g, mod, w_ffn_in, w_ffn_out, g_final, rows_per_mod, final):
    T = x.shape[0]
    tm = 512
    per = rows_per_mod // tm
    return pl.pallas_call(
        functools.partial(_ffn_kernel, final=final),
        grid=(T // tm, 2),
        in_specs=[
            pl.BlockSpec((tm, D_MODEL), lambda i, k: (i, 0)),
            pl.BlockSpec((1, D_MODEL), lambda i, k: (0, 0)),
            pl.BlockSpec((1, 1, 6 * D_MODEL), lambda i, k: (i // per, 0, 0)),
            pl.BlockSpec((D_MODEL, FFN_CHUNK), lambda i, k: (0, k)),
            pl.BlockSpec((D_MODEL, FFN_CHUNK), lambda i, k: (0, 2 + k)),
            pl.BlockSpec((FFN_CHUNK, D_MODEL), lambda i, k: (k, 0)),
            pl.BlockSpec((1, D_MODEL), lambda i, k: (0, 0)),
        ],
        out_specs=pl.BlockSpec((tm, D_MODEL), lambda i, k: (i, 0)),
        out_shape=jax.ShapeDtypeStruct((T, D_MODEL), F32),
        scratch_shapes=[pltpu.VMEM((tm, D_MODEL), BF16), pltpu.VMEM((tm, D_MODEL), F32)],
        compiler_params=_cparams("arbitrary", "arbitrary"),
        name="ffn",
    )(x, g, mod, w_ffn_in, w_ffn_in, w_ffn_out, g_final)


def _da_lambda(lam_ref, lam_init):
    lp = lam_ref[...]
    a = jnp.sum(lp[0:1] * lp[1:2], axis=1, keepdims=True)
    b = jnp.sum(lp[2:3] * lp[3:4], axis=1, keepdims=True)
    return jnp.exp(a) - jnp.exp(b) + lam_init


def _softmax_rows(s):
    m = jnp.max(s, axis=-1, keepdims=True)
    p = jnp.exp(s - m)
    return p, jnp.sum(p, axis=-1, keepdims=True)


def _ctx_attn_kernel(nq_ref, nk_ref, nv_ref, dq_ref, dk_ref, dv_ref, lam_ref, sub_ref, yn_ref, yd_ref, *, lam_init):
    lam = _da_lambda(lam_ref, lam_init)
    na_scale = NA_HEAD_DIM ** -0.5
    da_scale = DA_HEAD_DIM ** -0.5
    for h in range(NA_HEADS):
        sl = slice(h * NA_HEAD_DIM, (h + 1) * NA_HEAD_DIM)
        s = _bdot_nt(nq_ref[:, sl], nk_ref[:, sl]) * na_scale
        p, l = _softmax_rows(s)
        yn_ref[:, sl] = _bdot(p, nv_ref[:, sl]) / l
    for h in range(DA_HEADS):
        ps = []
        for i in range(2):
            sl = slice((2 * h + i) * DA_HEAD_DIM, (2 * h + i + 1) * DA_HEAD_DIM)
            s = _bdot_nt(dq_ref[:, sl], dk_ref[:, sl]) * da_scale
            p, l = _softmax_rows(s)
            ps.append(p / l)
        a = ps[0] - lam * ps[1]
        sl = slice(h * DA_V_DIM, (h + 1) * DA_V_DIM)
        o = _bdot(a, dv_ref[:, sl])
        yd_ref[:, sl] = _rms(o, sub_ref[...]) * (1.0 - lam_init)


def _ctx_attention(u, da_lambda, da_subln, lam_init):
    col = lambda j: pl.BlockSpec((SEQ, BRANCH_W), lambda b, j=j: (b, j))
    out = pl.BlockSpec((SEQ, BRANCH_W), lambda b: (b, 0))
    shape = jax.ShapeDtypeStruct((BATCH * SEQ, BRANCH_W), F32)
    return pl.pallas_call(
        functools.partial(_ctx_attn_kernel, lam_init=lam_init),
        grid=(BATCH,),
        in_specs=[col(3), col(4), col(5), col(6), col(7), col(8),
                  pl.BlockSpec((4, DA_HEAD_DIM), lambda b: (0, 0)),
                  pl.BlockSpec((1, DA_V_DIM), lambda b: (0, 0))],
        out_specs=[out, out],
        out_shape=[shape, shape],
        compiler_params=_cparams("arbitrary"),
        name="ctx_attention",
    )(u, u, u, u, u, u, da_lambda, da_subln)


def _rpb_table_kernel(rpb_ref, o_ref):
    qc = lax.broadcasted_iota(jnp.int32, (GRID_W, GRID_W), 0)
    kc = lax.broadcasted_iota(jnp.int32, (GRID_W, GRID_W), 1)
    dc = jnp.clip(kc - qc, -(NA_WIN_COLS - 1), NA_WIN_COLS - 1) + (NA_WIN_COLS - 1)
    r = rpb_ref[0, 0]
    for dr in range(2 * NA_WIN_ROWS - 1):
        acc = jnp.zeros((GRID_W, GRID_W), F32)
        for d in range(2 * NA_WIN_COLS - 1):
            acc = jnp.where(dc == d, r[dr:dr + 1, d:d + 1], acc)
        o_ref[0, 0, dr] = acc


def _rpb_table(na_rpb):
    n_dr, n_dc = 2 * NA_WIN_ROWS - 1, 2 * NA_WIN_COLS - 1
    return pl.pallas_call(
        _rpb_table_kernel,
        grid=(DEPTH, NA_HEADS),
        in_specs=[pl.BlockSpec((1, 1, n_dr, n_dc), lambda l, h: (l, h, 0, 0))],
        out_specs=pl.BlockSpec((1, 1, n_dr, GRID_W, GRID_W), lambda l, h: (l, h, 0, 0, 0)),
        out_shape=jax.ShapeDtypeStruct((DEPTH, NA_HEADS, n_dr, GRID_W, GRID_W), F32),
        compiler_params=_cparams("arbitrary", "arbitrary"),
        name="rpb_table",
    )(na_rpb)


NA_ROWS_PER_STEP = 4


def _na_kernel(q_ref, k_ref, v_ref, kc_ref, vc_ref, bias_ref, o_ref):
    n_lat = NA_WIN_ROWS * GRID_W
    qc = lax.broadcasted_iota(jnp.int32, (GRID_W, n_lat), 0)
    kcol = lax.broadcasted_iota(jnp.int32, (GRID_W, n_lat), 1) % GRID_W
    c0 = jnp.clip(qc - NA_WIN_COLS // 2, 0, GRID_W - NA_WIN_COLS)
    col_ok = (kcol >= c0) & (kcol < c0 + NA_WIN_COLS)
    scale = NA_HEAD_DIM ** -0.5
    for rr in range(NA_ROWS_PER_STEP):
        r = pl.program_id(1) * NA_ROWS_PER_STEP + rr
        r0 = jnp.clip(r - NA_WIN_ROWS // 2, 0, GRID_H - NA_WIN_ROWS)
        start = pl.multiple_of(r0 * GRID_W, GRID_W)
        kwin = k_ref[pl.ds(start, n_lat), :]
        vwin = v_ref[pl.ds(start, n_lat), :]
        dr0 = r0 - r + (NA_WIN_ROWS - 1)
        rows = slice(rr * GRID_W, (rr + 1) * GRID_W)
        for h in range(NA_HEADS):
            sl = slice(h * NA_HEAD_DIM, (h + 1) * NA_HEAD_DIM)
            q = q_ref[rows, sl]
            bias = jnp.concatenate([bias_ref[h, dr0 + w] for w in range(NA_WIN_ROWS)], axis=1)
            s_lat = _bdot_nt(q, kwin[:, sl]) * scale + bias
            s_lat = jnp.where(col_ok, s_lat, NEG_INF)
            s_ctx = _bdot_nt(q, kc_ref[0, :, sl]) * scale
            m = jnp.maximum(jnp.max(s_lat, axis=-1, keepdims=True), jnp.max(s_ctx, axis=-1, keepdims=True))
            p_lat = jnp.exp(s_lat - m)
            p_ctx = jnp.exp(s_ctx - m)
            l = jnp.sum(p_lat, axis=-1, keepdims=True) + jnp.sum(p_ctx, axis=-1, keepdims=True)
            o = _bdot(p_lat, vwin[:, sl]) + _bdot(p_ctx, vc_ref[0, :, sl])
            o_ref[rows, sl] = o / l


def _nbr_attention(u, k_ctx, v_ctx, bias):
    n_dr = 2 * NA_WIN_ROWS - 1
    steps = GRID_H // NA_ROWS_PER_STEP
    rows = NA_ROWS_PER_STEP * GRID_W
    return pl.pallas_call(
        _na_kernel,
        grid=(DEC_BATCH, steps),
        in_specs=[
            pl.BlockSpec((rows, BRANCH_W), lambda b, r: (b * steps + r, 3)),
            pl.BlockSpec((DEC_SEQ, BRANCH_W), lambda b, r: (b, 4)),
            pl.BlockSpec((DEC_SEQ, BRANCH_W), lambda b, r: (b, 5)),
            pl.BlockSpec((1, PAST_LEN, BRANCH_W), lambda b, r: (b, 0, 0)),
            pl.BlockSpec((1, PAST_LEN, BRANCH_W), lambda b, r: (b, 0, 0)),
            pl.BlockSpec((NA_HEADS, n_dr, GRID_W, GRID_W), lambda b, r: (0, 0, 0, 0)),
        ],
        out_specs=pl.BlockSpec((rows, BRANCH_W), lambda b, r: (b * steps + r, 0)),
        out_shape=jax.ShapeDtypeStruct((DEC_BATCH * DEC_SEQ, BRANCH_W), F32),
        compiler_params=_cparams("arbitrary", "arbitrary"),
        name="nbr_attention",
    )(u, u, u, k_ctx, v_ctx, bias)


DA_TQ = 256
DA_KEYS = DEC_SEQ + PAST_LEN


def _rope(x, cos, sin_signed):
    n = x.shape[-1]
    lane = lax.broadcasted_iota(jnp.int32, x.shape, 1)
    partner = jnp.where(lane % 2 == 0, pltpu.roll(x, n - 1, axis=1), pltpu.roll(x, 1, axis=1))
    return x * cos + partner * sin_signed


def _da_prep_kernel(q_ref, k_ref, v_ref, kc_ref, vc_ref, cos_ref, sin_ref, qo_ref, kt_ref, vo_ref):
    t = pl.program_id(1)

    def put_v(v):
        ones = jnp.ones((DA_TQ, DA_V_DIM), BF16)
        for h in range(DA_HEADS):
            vo_ref[0, h, :, 0:DA_V_DIM] = v[:, h * DA_V_DIM:(h + 1) * DA_V_DIM].astype(BF16)
            vo_ref[0, h, :, DA_V_DIM:2 * DA_V_DIM] = ones

    @pl.when(t < DEC_SEQ // DA_TQ)
    def _():
        cos = cos_ref[...]
        sin = sin_ref[...]
        q = _rope(q_ref[...].astype(F32), cos, sin) * (DA_HEAD_DIM ** -0.5 * math.log2(math.e))
        qo_ref[...] = q.astype(BF16)
        k = _rope(k_ref[...].astype(F32), cos, sin)
        kt_ref[0] = k.T.astype(BF16)
        put_v(v_ref[...])

    @pl.when(t == DEC_SEQ // DA_TQ)
    def _():
        kt_ref[0] = kc_ref[0].T.astype(BF16)
        put_v(vc_ref[0])


def _da_prep(u, k_ctx, v_ctx, cos, sin):
    nt = DEC_SEQ // DA_TQ
    last = nt - 1
    rowblk = lambda j: pl.BlockSpec((DA_TQ, BRANCH_W), lambda b, t, j=j: (b * nt + jnp.minimum(t, last), j))
    tab = pl.BlockSpec((DA_TQ, BRANCH_W), lambda b, t: (jnp.minimum(t, last), 0))
    ctx = pl.BlockSpec((1, PAST_LEN, BRANCH_W), lambda b, t: (b, 0, 0))
    return pl.pallas_call(
        _da_prep_kernel,
        grid=(DEC_BATCH, nt + 1),
        in_specs=[rowblk(6), rowblk(7), rowblk(8), ctx, ctx, tab, tab],
        out_specs=[
            pl.BlockSpec((DA_TQ, BRANCH_W), lambda b, t: (b * nt + jnp.minimum(t, last), 0)),
            pl.BlockSpec((1, BRANCH_W, DA_TQ), lambda b, t: (b, 0, t)),
            pl.BlockSpec((1, DA_HEADS, DA_TQ, 2 * DA_V_DIM), lambda b, t: (b, 0, t, 0)),
        ],
        out_shape=[
            jax.ShapeDtypeStruct((DEC_BATCH * DEC_SEQ, BRANCH_W), BF16),
            jax.ShapeDtypeStruct((DEC_BATCH, BRANCH_W, DA_KEYS), BF16),
            jax.ShapeDtypeStruct((DEC_BATCH, DA_HEADS, DA_KEYS, 2 * DA_V_DIM), BF16),
        ],
        compiler_params=_cparams("arbitrary", "arbitrary"),
        name="da_prep",
    )(u, u, u, k_ctx, v_ctx, cos, sin)


def _da_kernel(q_ref, kt_ref, v_ref, lam_ref, sub_ref, o_ref, *, lam_init):
    lam = _da_lambda(lam_ref, lam_init)
    for h in range(DA_HEADS):
        os = []
        for i in range(2):
            c = (2 * h + i) * DA_HEAD_DIM
            s = jnp.dot(q_ref[:, c:c + DA_HEAD_DIM], kt_ref[0, c:c + DA_HEAD_DIM, :], preferred_element_type=F32)
            p = jnp.exp2(s - jnp.max(s, axis=-1, keepdims=True)).astype(BF16)
            oe = jnp.dot(p, v_ref[0, h], preferred_element_type=F32)
            os.append(oe[:, 0:DA_V_DIM] / oe[:, DA_V_DIM:DA_V_DIM + 1])
        o = os[0] - lam * os[1]
        sl = slice(h * DA_V_DIM, (h + 1) * DA_V_DIM)
        o_ref[:, sl] = _rms(o, sub_ref[...]) * (1.0 - lam_init)


def _diff_attention(q, kt, v, da_lambda, da_subln, lam_init):
    nt = DEC_SEQ // DA_TQ
    return pl.pallas_call(
        functools.partial(_da_kernel, lam_init=lam_init),
        grid=(DEC_BATCH, nt),
        in_specs=[
            pl.BlockSpec((DA_TQ, BRANCH_W), lambda b, t: (b * nt + t, 0)),
            pl.BlockSpec((1, BRANCH_W, DA_KEYS), lambda b, t: (b, 0, 0)),
            pl.BlockSpec((1, DA_HEADS, DA_KEYS, 2 * DA_V_DIM), lambda b, t: (b, 0, 0, 0)),
            pl.BlockSpec((4, DA_HEAD_DIM), lambda b, t: (0, 0)),
            pl.BlockSpec((1, DA_V_DIM), lambda b, t: (0, 0)),
        ],
        out_specs=pl.BlockSpec((DA_TQ, BRANCH_W), lambda b, t: (b * nt + t, 0)),
        out_shape=jax.ShapeDtypeStruct((DEC_BATCH * DEC_SEQ, BRANCH_W), F32),
        compiler_params=_cparams("arbitrary", "arbitrary"),
        name="diff_attention",
    )(q, kt, v, da_lambda, da_subln)


def _rope_tables():
    pos = np.arange(DEC_SEQ)
    row = (pos // GRID_W).astype(np.float32)
    col = (pos % GRID_W).astype(np.float32)
    n_freq = DA_HEAD_DIM // 4
    inv = (np.float32(ROPE_BASE) ** (-np.arange(n_freq, dtype=np.float32) / n_freq)).astype(np.float32)
    ang = np.concatenate([row[:, None] * inv[None, :], col[:, None] * inv[None, :]], axis=-1)
    ang = ang.astype(np.float64)
    cos = np.repeat(np.cos(ang), 2, axis=-1)
    sin = np.repeat(np.sin(ang), 2, axis=-1)
    sign = np.where(np.arange(DA_HEAD_DIM) % 2 == 0, -1.0, 1.0)
    reps = BRANCH_W // DA_HEAD_DIM
    cos = np.tile(cos, (1, reps)).astype(np.float32)
    sin = np.tile(sin * sign[None, :], (1, reps)).astype(np.float32)
    return jnp.asarray(cos), jnp.asarray(sin)


def _filt_hidden_kernel(feat_ref, w1_ref, b1_ref, w2_ref, b2_ref, fr_ref, o_ref):
    fr = fr_ref[0]
    h = jnp.sin(fr * (jnp.dot(feat_ref[...], w1_ref[0], precision=HIGHEST, preferred_element_type=F32) + b1_ref[0]))
    o_ref[0] = jnp.sin(fr * (jnp.dot(h, w2_ref[0], precision=HIGHEST, preferred_element_type=F32) + b2_ref[0]))


def _filt_kernel(h_ref, w3f_ref, w3b_ref, dec_ref, hf_ref, hb_ref):
    h = h_ref[0]
    dec = dec_ref[...]
    hf = jnp.dot(h, w3f_ref[0], precision=HIGHEST, preferred_element_type=F32) * dec
    hb = jnp.dot(h, w3b_ref[0], precision=HIGHEST, preferred_element_type=F32) * dec
    row = lax.broadcasted_iota(jnp.int32, hb.shape, 0)
    hb = jnp.where(row == 0, 0.0, hb)
    nrm = jnp.sum(jnp.abs(hf), axis=0, keepdims=True) + jnp.sum(jnp.abs(hb), axis=0, keepdims=True)
    hf_ref[0, 0] = hf / nrm
    hb_ref[0, 0] = hb / nrm


def _hyena_pos_tables(L):
    f32 = np.float32
    pos = np.arange(L, dtype=f32)
    t = (pos / f32(L)).astype(f32)
    bands = np.linspace(1e-4, HY_POS_BANDS - 1, HY_POS_BANDS, dtype=f32)
    ang = (f32(2 * math.pi / L) * pos[:, None] * bands[None, :]).astype(np.float64)
    feats = np.zeros((L, HY_FILT_HIDDEN), f32)
    feats[:, 0] = t
    feats[:, 1:1 + HY_POS_BANDS] = np.cos(ang)
    feats[:, 1 + HY_POS_BANDS:HY_POS_DIM] = -np.sin(ang)
    deltas = np.linspace(math.log(HY_DECAY_TARGET) / HY_SLOW_DECAY,
                         math.log(HY_DECAY_TARGET) / HY_FAST_DECAY, BRANCH_W, dtype=f32)
    decay = np.exp((-t[:, None] * np.abs(deltas)[None, :]).astype(np.float64)).astype(f32)
    return jnp.asarray(feats), jnp.asarray(decay)


def _hyena_filters(L, w1p, b1, w2, b2, w3, freq):
    feats, decay = _hyena_pos_tables(L)
    cb = 128
    ncb = BRANCH_W // cb
    small = lambda shape: pl.BlockSpec((1,) + shape, lambda l: (l, 0, 0))
    hidden = pl.pallas_call(
        _filt_hidden_kernel,
        grid=(DEPTH,),
        in_specs=[
            pl.BlockSpec((L, HY_FILT_HIDDEN), lambda l: (0, 0)),
            small((HY_FILT_HIDDEN, HY_FILT_HIDDEN)), small((1, HY_FILT_HIDDEN)),
            small((HY_FILT_HIDDEN, HY_FILT_HIDDEN)), small((1, HY_FILT_HIDDEN)),
            small((1, HY_FILT_HIDDEN)),
        ],
        out_specs=pl.BlockSpec((1, L, HY_FILT_HIDDEN), lambda l: (l, 0, 0)),
        out_shape=jax.ShapeDtypeStruct((DEPTH, L, HY_FILT_HIDDEN), F32),
        compiler_params=_cparams("arbitrary"),
        name=f"hyena_filter_hidden_{L}",
    )(feats, w1p, b1, w2, b2, freq)
    shape = jax.ShapeDtypeStruct((DEPTH, 2, L, BRANCH_W), F32)
    out = pl.BlockSpec((1, 1, L, cb), lambda l, o, c: (l, o, 0, c))
    return pl.pallas_call(
        _filt_kernel,
        grid=(DEPTH, 2, ncb),
        in_specs=[
            pl.BlockSpec((1, L, HY_FILT_HIDDEN), lambda l, o, c: (l, 0, 0)),
            pl.BlockSpec((1, HY_FILT_HIDDEN, cb), lambda l, o, c: (l, 0, o * 2 * ncb + c)),
            pl.BlockSpec((1, HY_FILT_HIDDEN, cb), lambda l, o, c: (l, 0, o * 2 * ncb + ncb + c)),
            pl.BlockSpec((L, cb), lambda l, o, c: (0, c)),
        ],
        out_specs=[out, out],
        out_shape=[shape, shape],
        compiler_params=_cparams("arbitrary", "arbitrary", "arbitrary"),
        name=f"hyena_filters_{L}",
    )(hidden, w3, w3, decay)


def _short_conv(u, w_ref, b_ref, seq_len):
    n = u.shape[0]
    t = lax.broadcasted_iota(jnp.int32, u.shape, 0) % seq_len
    prev = jnp.where(t == 0, 0.0, pltpu.roll(u, 1, axis=0))
    nxt = jnp.where(t == seq_len - 1, 0.0, pltpu.roll(u, n - 1, axis=0))
    return prev * w_ref[0:1, :] + u * w_ref[1:2, :] + nxt * w_ref[2:3, :] + b_ref[...]


def _dft_direct_mats():
    n, half = 2 * SEQ, SEQ
    k = np.arange(n)[:, None].astype(np.float64)
    t = np.arange(half)[None, :].astype(np.float64)
    ang = 2 * np.pi * k * t / n
    fr, fi = np.cos(ang), -np.sin(ang)
    mf = np.block([[fr, -fi], [fi, fr]])
    gr, gi = np.cos(ang).T / n, np.sin(ang).T / n
    mi = np.block([[gr, -gi], [gi, gr]])
    return mf.astype(np.float32), mi.astype(np.float32)


def _spec_direct_kernel(hf_ref, hb_ref, m_ref, o_ref):
    m = m_ref[...]
    wf = jnp.dot(m, hf_ref[0, 0], precision=HIGHEST, preferred_element_type=F32)
    wb = jnp.dot(m, hb_ref[0, 0], precision=HIGHEST, preferred_element_type=F32)
    n = 2 * SEQ
    o_ref[0, 0, 0:n] = wf[0:n] + wb[0:n]
    o_ref[0, 0, n:2 * n] = wf[n:2 * n] - wb[n:2 * n]


def _spec_direct(hf, hb, mf_real):
    n = 2 * SEQ
    blk = pl.BlockSpec((1, 1, SEQ, BRANCH_W), lambda l, o: (l, o, 0, 0))
    return pl.pallas_call(
        _spec_direct_kernel,
        grid=(DEPTH, 2),
        in_specs=[blk, blk, pl.BlockSpec((2 * n, SEQ), lambda l, o: (0, 0))],
        out_specs=pl.BlockSpec((1, 1, 2 * n, BRANCH_W), lambda l, o: (l, o, 0, 0)),
        out_shape=jax.ShapeDtypeStruct((DEPTH, 2, 2 * n, BRANCH_W), F32),
        compiler_params=_cparams("arbitrary", "arbitrary"),
        name="hyena_spectrum_direct",
    )(hf, hb, mf_real)


def _lconv_direct_kernel(s_ref, g_ref, cws_ref, cbs_ref, cwg_ref, cbg_ref, h_ref, bias_ref, mf_ref, mi_ref, o_ref,
                         *, conv_sig):
    n = 2 * SEQ
    sig = s_ref[...]
    if conv_sig:
        sig = _short_conv(sig, cws_ref, cbs_ref, SEQ)
    gate = _short_conv(g_ref[...], cwg_ref, cbg_ref, SEQ)
    z = jnp.dot(mf_ref[...], sig.astype(BF16), preferred_element_type=F32)
    zr, zi = z[0:n], z[n:2 * n]
    hr, hi = h_ref[0:n], h_ref[n:2 * n]
    y = jnp.concatenate([zr * hr - zi * hi, zr * hi + zi * hr], axis=0)
    y = jnp.dot(mi_ref[...], y.astype(BF16), preferred_element_type=F32)
    o_ref[...] = gate * (y + sig * bias_ref[...])


def _lconv_direct(sig, sig_col, gate_src, gate_col, conv_w, conv_b, spec, bias, mf, mi, conv_sig):
    n = 2 * SEQ
    rows = 2 * SEQ
    T = sig.shape[0]
    return pl.pallas_call(
        functools.partial(_lconv_direct_kernel, conv_sig=conv_sig),
        grid=(T // rows,),
        in_specs=[
            pl.BlockSpec((rows, BRANCH_W), lambda p: (p, sig_col)),
            pl.BlockSpec((rows, BRANCH_W), lambda p: (p, gate_col)),
            pl.BlockSpec((3, BRANCH_W), lambda p: (0, 0)),
            pl.BlockSpec((1, BRANCH_W), lambda p: (0, 0)),
            pl.BlockSpec((3, BRANCH_W), lambda p: (0, gate_col)),
            pl.BlockSpec((1, BRANCH_W), lambda p: (0, gate_col)),
            pl.BlockSpec((2 * n, BRANCH_W), lambda p: (0, 0)),
            pl.BlockSpec((1, BRANCH_W), lambda p: (0, 0)),
            pl.BlockSpec((2 * n, rows), lambda p: (0, 0)),
            pl.BlockSpec((rows, 2 * n), lambda p: (0, 0)),
        ],
        out_specs=pl.BlockSpec((rows, BRANCH_W), lambda p: (p, 0)),
        out_shape=jax.ShapeDtypeStruct((T, BRANCH_W), F32),
        compiler_params=_cparams("arbitrary"),
        name="hyena_lconv_direct",
    )(sig, gate_src, conv_w, conv_b, conv_w, conv_b, spec, bias, mf, mi)


def _dft_two_stage_mats():
    no, ni, half, n = FFT_NO, FFT_NI, FFT_HALF, FFT_N
    f64 = np.float64
    k1 = np.arange(no, dtype=f64)
    n_o = np.arange(half, dtype=f64)
    n_i = np.arange(ni, dtype=f64)
    ang = 2 * np.pi * (n_i[:, None, None] * k1[None, :, None] / n + k1[None, :, None] * n_o[None, None, :] / no)
    tr, ti = np.cos(ang), -np.sin(ang)
    m1 = np.concatenate([np.concatenate([tr, -ti], axis=2), np.concatenate([ti, tr], axis=2)], axis=1)
    k2 = np.arange(ni, dtype=f64)
    ang2 = 2 * np.pi * k2[:, None] * n_i[None, :] / ni
    f2r, f2i = np.cos(ang2), -np.sin(ang2)
    m2 = np.block([[f2r, -f2i], [f2i, f2r]])
    m2c = np.block([[f2r, f2i], [-f2i, f2r]])
    sr, si = np.transpose(tr, (0, 2, 1)) / n, -np.transpose(ti, (0, 2, 1)) / n
    m3 = np.concatenate([np.concatenate([sr, -si], axis=2), np.concatenate([si, sr], axis=2)], axis=1)
    return (m1.astype(np.float32), m2.astype(np.float32), m2c.astype(np.float32), m3.astype(np.float32))


def _fwd_stage1(za_ref, zb_ref, m1_ref, w_ref):
    def body(ni, carry):
        a = za_ref[pl.ds(ni, FFT_HALF, stride=FFT_NI), :]
        if zb_ref is None:
            out = jnp.dot(m1_ref[ni][:, 0:FFT_HALF], a.astype(BF16), preferred_element_type=F32)
        else:
            b = zb_ref[pl.ds(ni, FFT_HALF, stride=FFT_NI), :]
            out = jnp.dot(m1_ref[ni], jnp.concatenate([a, b], axis=0).astype(BF16), preferred_element_type=F32)
        w_ref[pl.ds(ni, FFT_NO, stride=2 * FFT_NI), :] = out[0:FFT_NO]
        w_ref[pl.ds(FFT_NI + ni, FFT_NO, stride=2 * FFT_NI), :] = out[FFT_NO:2 * FFT_NO]
        return carry

    lax.fori_loop(0, FFT_NI, body, 0, unroll=FFT_UNROLL)


def _spec_two_stage_kernel(hf_ref, hb_ref, m1_ref, m2_ref, o_ref, wf_ref, wb_ref):
    _fwd_stage1(hf_ref.at[0, 0], None, m1_ref, wf_ref)
    _fwd_stage1(hb_ref.at[0, 0], None, m1_ref, wb_ref)
    blk = 2 * FFT_NI

    def body(k1, carry):
        rows = pl.ds(pl.multiple_of(k1 * blk, blk), blk)
        xf = jnp.dot(m2_ref[...], wf_ref[rows, :].astype(BF16), preferred_element_type=F32)
        xb = jnp.dot(m2_ref[...], wb_ref[rows, :].astype(BF16), preferred_element_type=F32)
        o_ref[0, 0, rows, :] = jnp.concatenate(
            [xf[0:FFT_NI] + xb[0:FFT_NI], xf[FFT_NI:blk] - xb[FFT_NI:blk]], axis=0)
        return carry

    lax.fori_loop(0, FFT_NO, body, 0, unroll=FFT_UNROLL)


def _spec_two_stage(hf, hb, m1, m2):
    cb = LCONV_CB
    blk = pl.BlockSpec((1, 1, DEC_SEQ, cb), lambda l, o, c: (l, o, 0, c))
    return pl.pallas_call(
        _spec_two_stage_kernel,
        grid=(DEPTH, 2, BRANCH_W // cb),
        in_specs=[blk, blk,
                  pl.BlockSpec((FFT_NI, 2 * FFT_NO, 2 * FFT_HALF), lambda l, o, c: (0, 0, 0)),
                  pl.BlockSpec((2 * FFT_NI, 2 * FFT_NI), lambda l, o, c: (0, 0))],
        out_specs=pl.BlockSpec((1, 1, 2 * FFT_N, cb), lambda l, o, c: (l, o, 0, c)),
        out_shape=jax.ShapeDtypeStruct((DEPTH, 2, 2 * FFT_N, BRANCH_W), F32),
        scratch_shapes=[pltpu.VMEM((2 * FFT_N, cb), F32), pltpu.VMEM((2 * FFT_N, cb), F32)],
        compiler_params=_cparams("arbitrary", "arbitrary", "arbitrary"),
        name="hyena_spectrum_two_stage",
    )(hf, hb, m1, m2)


def _lconv_two_stage_kernel(s_ref, g_ref, cws_ref, cbs_ref, cwg_ref, cbg_ref, h_ref, bias_ref,
                            m1_ref, m2_ref, m2c_ref, m3_ref, o_ref, z_ref, w_ref, *, conv_sig):
    for b in range(2):
        sig = s_ref[b].astype(F32)
        if conv_sig:
            sig = _short_conv(sig, cws_ref, cbs_ref, DEC_SEQ)
        z_ref[b] = sig
    _fwd_stage1(z_ref.at[0], z_ref.at[1], m1_ref, w_ref)
    blk = 2 * FFT_NI

    def mid(k1, carry):
        rows = pl.ds(pl.multiple_of(k1 * blk, blk), blk)
        x = jnp.dot(m2_ref[...], w_ref[rows, :].astype(BF16), preferred_element_type=F32)
        h = h_ref[rows, :]
        xr, xi = x[0:FFT_NI], x[FFT_NI:blk]
        hr, hi = h[0:FFT_NI], h[FFT_NI:blk]
        y = jnp.concatenate([xr * hr - xi * hi, xr * hi + xi * hr], axis=0)
        w_ref[rows, :] = jnp.dot(m2c_ref[...], y.astype(BF16), preferred_element_type=F32)
        return carry

    lax.fori_loop(0, FFT_NO, mid, 0, unroll=FFT_UNROLL)

    def last(ni, carry):
        cr = w_ref[pl.ds(ni, FFT_NO, stride=blk), :]
        ci = w_ref[pl.ds(FFT_NI + ni, FFT_NO, stride=blk), :]
        y = jnp.dot(m3_ref[ni], jnp.concatenate([cr, ci], axis=0).astype(BF16), preferred_element_type=F32)
        o_ref[0, pl.ds(ni, FFT_HALF, stride=FFT_NI), :] = y[0:FFT_HALF]
        o_ref[1, pl.ds(ni, FFT_HALF, stride=FFT_NI), :] = y[FFT_HALF:2 * FFT_HALF]
        return carry

    lax.fori_loop(0, FFT_NI, last, 0, unroll=FFT_UNROLL)
    for b in range(2):
        gate = _short_conv(g_ref[b].astype(F32), cwg_ref, cbg_ref, DEC_SEQ)
        sig = z_ref[b]
        o_ref[b] = gate * (o_ref[b] + sig * bias_ref[...])


def _lconv_two_stage(sig, sig_col, gate_src, gate_col, conv_w, conv_b, spec, bias, mats, conv_sig):
    cb = LCONV_CB
    ncb = BRANCH_W // cb
    m1, m2, m2c, m3 = mats
    const3 = lambda c, p: (0, 0, 0)
    const2 = lambda c, p: (0, 0)
    return pl.pallas_call(
        functools.partial(_lconv_two_stage_kernel, conv_sig=conv_sig),
        grid=(ncb, DEC_BATCH // 2),
        in_specs=[
            pl.BlockSpec((2, DEC_SEQ, cb), lambda c, p: (p, 0, sig_col * ncb + c)),
            pl.BlockSpec((2, DEC_SEQ, cb), lambda c, p: (p, 0, gate_col * ncb + c)),
            pl.BlockSpec((3, cb), lambda c, p: (0, c)),
            pl.BlockSpec((1, cb), lambda c, p: (0, c)),
            pl.BlockSpec((3, cb), lambda c, p: (0, gate_col * ncb + c)),
            pl.BlockSpec((1, cb), lambda c, p: (0, gate_col * ncb + c)),
            pl.BlockSpec((2 * FFT_N, cb), lambda c, p: (0, c)),
            pl.BlockSpec((1, cb), lambda c, p: (0, c)),
            pl.BlockSpec(m1.shape, const3),
            pl.BlockSpec(m2.shape, const2),
            pl.BlockSpec(m2c.shape, const2),
            pl.BlockSpec(m3.shape, const3),
        ],
        out_specs=pl.BlockSpec((2, DEC_SEQ, cb), lambda c, p: (p, 0, c)),
        out_shape=jax.ShapeDtypeStruct((DEC_BATCH, DEC_SEQ, BRANCH_W), F32),
        scratch_shapes=[pltpu.VMEM((2, DEC_SEQ, cb), F32), pltpu.VMEM((2 * FFT_N, cb), F32)],
        compiler_params=_cparams("arbitrary", "arbitrary"),
        name="hyena_lconv_two_stage",
    )(sig, gate_src, conv_w, conv_b, conv_w, conv_b, spec, bias, m1, m2, m2c, m3)


def kernel(x_prompt, x_sample, cache_na_k, cache_na_v, cache_da_k, cache_da_v, c, c_ctx, w_ada, b_ada, norm_mix,
           norm_ffn, w_in, hy_conv_w, hy_conv_b, hy_filt_w1, hy_filt_b1, hy_filt_w2, hy_filt_b2, hy_filt_w3,
           hy_filt_freq, hy_bias, na_rpb, da_lambda, da_subln, w_lift, w_out, w_ffn_in, w_ffn_out, norm_final):
    TP, TS = BATCH * SEQ, DEC_BATCH * DEC_SEQ
    xp = x_prompt.reshape(TP, D_MODEL)
    xs = x_sample.reshape(TS, D_MODEL)

    cc = jnp.concatenate([c_ctx[None, :], c, jnp.zeros((8 - 1 - DEC_BATCH, D_MODEL), F32)], axis=0)
    mod = _modulation(cc, w_ada, b_ada)
    mod_p = mod[:, 0:1].reshape(DEPTH, 1, 1, 6 * D_MODEL)
    mod_s = mod[:, 1:1 + DEC_BATCH].reshape(DEPTH, DEC_BATCH, 1, 6 * D_MODEL)

    w_mix = w_in[:, :, :MIX_W].astype(BF16)
    w_gate = w_in[:, :, MIX_W:].astype(BF16)
    w_lift_b = w_lift.astype(BF16)
    w_out_b = w_out.astype(BF16)
    w_ffn_in_b = w_ffn_in.astype(BF16)
    w_ffn_out_b = w_ffn_out.astype(BF16)
    g_mix = norm_mix.reshape(DEPTH, 1, D_MODEL)
    g_ffn = norm_ffn.reshape(DEPTH, 1, D_MODEL)
    g_fin = norm_final.reshape(1, D_MODEL)
    subln = da_subln.reshape(DEPTH, 1, DA_V_DIM)

    w1p = jnp.pad(hy_filt_w1, ((0, 0), (0, HY_FILT_HIDDEN - HY_POS_DIM), (0, 0)))
    b1 = hy_filt_b1.reshape(DEPTH, 1, HY_FILT_HIDDEN)
    b2 = hy_filt_b2.reshape(DEPTH, 1, HY_FILT_HIDDEN)
    fr = hy_filt_freq.reshape(DEPTH, 1, HY_FILT_HIDDEN)
    mf, mi = _dft_direct_mats()
    mats = _dft_two_stage_mats()
    hf_p, hb_p = _hyena_filters(SEQ, w1p, b1, hy_filt_w2, b2, hy_filt_w3, fr)
    hf_s, hb_s = _hyena_filters(DEC_SEQ, w1p, b1, hy_filt_w2, b2, hy_filt_w3, fr)
    spec_p = _spec_direct(hf_p, hb_p, jnp.asarray(mf[:, 0:SEQ]))
    mf_b, mi_b = jnp.asarray(mf, dtype=BF16), jnp.asarray(mi, dtype=BF16)
    mats_b = tuple(jnp.asarray(m, dtype=BF16) for m in mats)
    spec_s = _spec_two_stage(hf_s, hb_s, mats_b[0], mats_b[1])
    conv_b = hy_conv_b.reshape(DEPTH, 1, 3 * BRANCH_W)

    rpb = _rpb_table(na_rpb)
    cos, sin = _rope_tables()
    ck_na = cache_na_k.reshape(DEC_BATCH, DEPTH, PAST_LEN, BRANCH_W)
    cv_na = cache_na_v.reshape(DEC_BATCH, DEPTH, PAST_LEN, BRANCH_W)
    ck_da = cache_da_k.reshape(DEC_BATCH, DEPTH, PAST_LEN, BRANCH_W)
    cv_da = cache_da_v.reshape(DEC_BATCH, DEPTH, PAST_LEN, BRANCH_W)

    caches = []
    for l in range(DEPTH):
        lam_init = 0.8 - 0.6 * math.exp(-0.3 * l)
        final = l == DEPTH - 1

        u = _in_proj(xp, g_mix[l], mod_p[l], w_mix[l], TP, F32)
        caches.append(u[:, 4 * BRANCH_W:5 * BRANCH_W])
        caches.append(u[:, 5 * BRANCH_W:6 * BRANCH_W])
        caches.append(u[:, 7 * BRANCH_W:8 * BRANCH_W])
        caches.append(u[:, 8 * BRANCH_W:9 * BRANCH_W])
        z1 = _lconv_direct(u, 0, u, 1, hy_conv_w[l], conv_b[l], spec_p[l, 0], hy_bias[l, 0:1], mf_b, mi_b, True)
        y_hy = _lconv_direct(z1, 0, u, 2, hy_conv_w[l], conv_b[l], spec_p[l, 1], hy_bias[l, 1:2], mf_b, mi_b, False)
        y_na, y_da = _ctx_attention(u, da_lambda[l], subln[l], lam_init)
        xp = _merge_out(xp, g_mix[l], mod_p[l], y_hy, y_na, y_da, w_gate[l], w_lift_b[l], w_out_b[l], TP)
        xp = _ffn(xp, g_ffn[l], mod_p[l], w_ffn_in_b[l], w_ffn_out_b[l], g_fin, TP, final)

        u = _in_proj(xs, g_mix[l], mod_s[l], w_mix[l], DEC_SEQ, BF16)
        u3 = u.reshape(DEC_BATCH, DEC_SEQ, MIX_W)
        z1 = _lconv_two_stage(u3, 0, u3, 1, hy_conv_w[l], conv_b[l], spec_s[l, 0], hy_bias[l, 0:1], mats_b, True)
        y_hy = _lconv_two_stage(z1, 0, u3, 2, hy_conv_w[l], conv_b[l], spec_s[l, 1], hy_bias[l, 1:2], mats_b, False)
        y_hy = y_hy.reshape(TS, BRANCH_W)
        y_na = _nbr_attention(u, ck_na[:, l], cv_na[:, l], rpb[l])
        q, kt, v = _da_prep(u, ck_da[:, l], cv_da[:, l], cos, sin)
        y_da = _diff_attention(q, kt, v, da_lambda[l], subln[l], lam_init)
        xs = _merge_out(xs, g_mix[l], mod_s[l], y_hy, y_na, y_da, w_gate[l], w_lift_b[l], w_out_b[l], DEC_SEQ)
        xs = _ffn(xs, g_ffn[l], mod_s[l], w_ffn_in_b[l], w_ffn_out_b[l], g_fin, DEC_SEQ, final)

    def stack(i, last):
        parts = [caches[4 * l + i].reshape(BATCH, SEQ, 8, last) for l in range(DEPTH)]
        return jnp.stack(parts, axis=1)

    y_prompt = xp.reshape(BATCH, SEQ, D_MODEL)
    y_sample = xs.reshape(DEC_BATCH, DEC_SEQ, D_MODEL)
    return (y_prompt, y_sample, stack(0, NA_HEAD_DIM), stack(1, NA_HEAD_DIM),
            stack(2, 2 * DA_HEAD_DIM), stack(3, DA_V_DIM))
```

```python
import functools
import math

import numpy as np
import jax
import jax.numpy as jnp
from jax import lax
from jax.experimental import pallas as pl
from jax.experimental.pallas import tpu as pltpu

F32 = jnp.float32
BF16 = jnp.bfloat16
HIGHEST = lax.Precision.HIGHEST

D_MODEL = 1024
BATCH = 32
SEQ = 256
DEPTH = 4
DEC_BATCH = 4
DEC_SEQ = 4096
PAST_LEN = 256
GRID_W = 64
GRID_H = DEC_SEQ // GRID_W
BRANCH_W = 512
HY_POS_BANDS = 16
HY_POS_DIM = 1 + 2 * HY_POS_BANDS
HY_FILT_HIDDEN = 64
HY_DECAY_TARGET = 1e-2
HY_FAST_DECAY = 0.3
HY_SLOW_DECAY = 1.5
NA_HEADS = 8
NA_HEAD_DIM = 64
NA_WIN_ROWS = 8
NA_WIN_COLS = 16
DA_HEADS = 8
DA_HEAD_DIM = 32
DA_V_DIM = 64
D_FF = 2816
MIX_W = 9 * BRANCH_W
ROPE_BASE = 10000.0
EPS = 1e-6
NEG_INF = -1e30

VMEM_LIMIT_BYTES = 56 * 1024 * 1024

FFT_N = 2 * DEC_SEQ
FFT_NO = 64
FFT_NI = 128
FFT_HALF = FFT_NO // 2
FFT_UNROLL = 4
LCONV_CB = 128


def _cparams(*sem):
    return pltpu.CompilerParams(dimension_semantics=sem, vmem_limit_bytes=VMEM_LIMIT_BYTES)


def _sigmoid(x):
    return 1.0 / (1.0 + jnp.exp(-x))


def _rms(x, g):
    return x * lax.rsqrt(jnp.mean(x * x, axis=-1, keepdims=True) + EPS) * g


def _modnorm(x, g, shift, scale):
    return _rms(x, g) * (1.0 + scale) + shift


def _bdot(a, b):
    return jnp.dot(a.astype(BF16), b.astype(BF16), preferred_element_type=F32)


def _bdot_nt(a, b):
    return lax.dot_general(a.astype(BF16), b.astype(BF16), (((1,), (1,)), ((), ())),
                           preferred_element_type=F32)


def _mod_kernel(c_ref, w_ref, b_ref, o_ref):
    c = c_ref[...]
    s = c * _sigmoid(c)
    o_ref[0] = jnp.dot(s, w_ref[0], precision=HIGHEST, preferred_element_type=F32) + b_ref[0]


def _modulation(cc, w_ada, b_ada):
    nt = 6
    return pl.pallas_call(
        _mod_kernel,
        grid=(DEPTH, nt),
        in_specs=[
            pl.BlockSpec((8, D_MODEL), lambda l, j: (0, 0)),
            pl.BlockSpec((1, D_MODEL, D_MODEL), lambda l, j: (l, 0, j)),
            pl.BlockSpec((1, 1, D_MODEL), lambda l, j: (l, 0, j)),
        ],
        out_specs=pl.BlockSpec((1, 8, D_MODEL), lambda l, j: (l, 0, j)),
        out_shape=jax.ShapeDtypeStruct((DEPTH, 8, 6 * D_MODEL), F32),
        compiler_params=_cparams("arbitrary", "arbitrary"),
        name="modulation",
    )(cc, w_ada, b_ada.reshape(DEPTH, 1, 6 * D_MODEL))


def _in_kernel(x_ref, g_ref, mod_ref, w_ref, o_ref, h_ref):
    @pl.when(pl.program_id(1) == 0)
    def _():
        m = mod_ref[0]
        h = _modnorm(x_ref[...], g_ref[...], m[:, 0:D_MODEL], m[:, D_MODEL:2 * D_MODEL])
        h_ref[...] = h.astype(BF16)

    o_ref[...] = jnp.dot(h_ref[...], w_ref[...], preferred_element_type=F32).astype(o_ref.dtype)


def _in_proj(x, g, mod, w, rows_per_mod, out_dtype):
    T = x.shape[0]
    tm, tn = 1024, 512
    per = rows_per_mod // tm
    return pl.pallas_call(
        _in_kernel,
        grid=(T // tm, MIX_W // tn),
        in_specs=[
            pl.BlockSpec((tm, D_MODEL), lambda i, j: (i, 0)),
            pl.BlockSpec((1, D_MODEL), lambda i, j: (0, 0)),
            pl.BlockSpec((1, 1, 6 * D_MODEL), lambda i, j: (i // per, 0, 0)),
            pl.BlockSpec((D_MODEL, tn), lambda i, j: (0, j)),
        ],
        out_specs=pl.BlockSpec((tm, tn), lambda i, j: (i, j)),
        out_shape=jax.ShapeDtypeStruct((T, MIX_W), out_dtype),
        scratch_shapes=[pltpu.VMEM((tm, D_MODEL), BF16)],
        compiler_params=_cparams("arbitrary", "arbitrary"),
        name="in_proj",
    )(x, g, mod, w)


def _mid_kernel(x_ref, g_ref, mod_ref, yh_ref, yn_ref, yd_ref, wg_ref, wl_ref, wo_ref, o_ref):
    m = mod_ref[0]
    x = x_ref[...]
    h = _modnorm(x, g_ref[...], m[:, 0:D_MODEL], m[:, D_MODEL:2 * D_MODEL]).astype(BF16)
    merged = None
    for br, y_ref in enumerate((yh_ref, yn_ref, yd_ref)):
        gate = _sigmoid(jnp.dot(h, wg_ref[:, br * D_MODEL:(br + 1) * D_MODEL], preferred_element_type=F32))
        lift = jnp.dot(y_ref[...].astype(BF16), wl_ref[br], preferred_element_type=F32)
        t = gate * lift
        merged = t if merged is None else merged + t
    o_ref[...] = x + m[:, 2 * D_MODEL:3 * D_MODEL] * _bdot(merged, wo_ref[...])


def _merge_out(x, g, mod, y_hy, y_na, y_da, w_gate, w_lift, w_out, rows_per_mod):
    T = x.shape[0]
    tm = 512
    per = rows_per_mod // tm
    row = lambda i: (i, 0)
    const2 = lambda i: (0, 0)
    return pl.pallas_call(
        _mid_kernel,
        grid=(T // tm,),
        in_specs=[
            pl.BlockSpec((tm, D_MODEL), row),
            pl.BlockSpec((1, D_MODEL), const2),
            pl.BlockSpec((1, 1, 6 * D_MODEL), lambda i: (i // per, 0, 0)),
            pl.BlockSpec((tm, BRANCH_W), row),
            pl.BlockSpec((tm, BRANCH_W), row),
            pl.BlockSpec((tm, BRANCH_W), row),
            pl.BlockSpec((D_MODEL, 3 * D_MODEL), const2),
            pl.BlockSpec((3, BRANCH_W, D_MODEL), lambda i: (0, 0, 0)),
            pl.BlockSpec((D_MODEL, D_MODEL), const2),
        ],
        out_specs=pl.BlockSpec((tm, D_MODEL), row),
        out_shape=jax.ShapeDtypeStruct((T, D_MODEL), F32),
        compiler_params=_cparams("arbitrary"),
        name="merge_out",
    )(x, g, mod, y_hy, y_na, y_da, w_gate, w_lift, w_out)


FFN_CHUNK = D_FF // 2


def _ffn_kernel(x_ref, g_ref, mod_ref, w1g_ref, w1u_ref, w2_ref, gf_ref, o_ref, h_ref, acc_ref, *, final):
    k = pl.program_id(1)

    @pl.when(k == 0)
    def _():
        m = mod_ref[0]
        h = _modnorm(x_ref[...], g_ref[...], m[:, 3 * D_MODEL:4 * D_MODEL], m[:, 4 * D_MODEL:5 * D_MODEL])
        h_ref[...] = h.astype(BF16)

    h = h_ref[...]
    a = jnp.dot(h, w1g_ref[...], preferred_element_type=F32)
    b = jnp.dot(h, w1u_ref[...], preferred_element_type=F32)
    part = _bdot(a * _sigmoid(a) * b, w2_ref[...])

    @pl.when(k == 0)
    def _():
        acc_ref[...] = part

    @pl.when(k == 1)
    def _():
        m = mod_ref[0]
        xn = x_ref[...] + m[:, 5 * D_MODEL:6 * D_MODEL] * (acc_ref[...] + part)
        if final:
            xn = _rms(xn, gf_ref[...])
        o_ref[...] = xn


def _ffn(x, g, mod, w_ffn_in, w_ffn_out, g_final, rows_per_mod, final):
    T = x.shape[0]
    tm = 512
    per = rows_per_mod // tm
    return pl.pallas_call(
        functools.partial(_ffn_kernel, final=final),
        grid=(T // tm, 2),
        in_specs=[
            pl.BlockSpec((tm, D_MODEL), lambda i, k: (i, 0)),
            pl.BlockSpec((1, D_MODEL), lambda i, k: (0, 0)),
            pl.BlockSpec((1, 1, 6 * D_MODEL), lambda i, k: (i // per, 0, 0)),
            pl.BlockSpec((D_MODEL, FFN_CHUNK), lambda i, k: (0, k)),
            pl.BlockSpec((D_MODEL, FFN_CHUNK), lambda i, k: (0, 2 + k)),
            pl.BlockSpec((FFN_CHUNK, D_MODEL), lambda i, k: (k, 0)),
            pl.BlockSpec((1, D_MODEL), lambda i, k: (0, 0)),
        ],
        out_specs=pl.BlockSpec((tm, D_MODEL), lambda i, k: (i, 0)),
        out_shape=jax.ShapeDtypeStruct((T, D_MODEL), F32),
        scratch_shapes=[pltpu.VMEM((tm, D_MODEL), BF16), pltpu.VMEM((tm, D_MODEL), F32)],
        compiler_params=_cparams("arbitrary", "arbitrary"),
        name="ffn",
    )(x, g, mod, w_ffn_in, w_ffn_in, w_ffn_out, g_final)


def _da_lambda(lam_ref, lam_init):
    lp = lam_ref[...]
    a = jnp.sum(lp[0:1] * lp[1:2], axis=1, keepdims=True)
    b = jnp.sum(lp[2:3] * lp[3:4], axis=1, keepdims=True)
    return jnp.exp(a) - jnp.exp(b) + lam_init


def _softmax_rows(s):
    m = jnp.max(s, axis=-1, keepdims=True)
    p = jnp.exp(s - m)
    return p, jnp.sum(p, axis=-1, keepdims=True)


def _ctx_attn_kernel(nq_ref, nk_ref, nv_ref, dq_ref, dk_ref, dv_ref, lam_ref, sub_ref, yn_ref, yd_ref, *, lam_init):
    lam = _da_lambda(lam_ref, lam_init)
    na_scale = NA_HEAD_DIM ** -0.5
    da_scale = DA_HEAD_DIM ** -0.5
    for h in range(NA_HEADS):
        sl = slice(h * NA_HEAD_DIM, (h + 1) * NA_HEAD_DIM)
        s = _bdot_nt(nq_ref[:, sl], nk_ref[:, sl]) * na_scale
        p, l = _softmax_rows(s)
        yn_ref[:, sl] = _bdot(p, nv_ref[:, sl]) / l
    for h in range(DA_HEADS):
        ps = []
        for i in range(2):
            sl = slice((2 * h + i) * DA_HEAD_DIM, (2 * h + i + 1) * DA_HEAD_DIM)
            s = _bdot_nt(dq_ref[:, sl], dk_ref[:, sl]) * da_scale
            p, l = _softmax_rows(s)
            ps.append(p / l)
        a = ps[0] - lam * ps[1]
        sl = slice(h * DA_V_DIM, (h + 1) * DA_V_DIM)
        o = _bdot(a, dv_ref[:, sl])
        yd_ref[:, sl] = _rms(o, sub_ref[...]) * (1.0 - lam_init)


def _ctx_attention(u, da_lambda, da_subln, lam_init):
    col = lambda j: pl.BlockSpec((SEQ, BRANCH_W), lambda b, j=j: (b, j))
    out = pl.BlockSpec((SEQ, BRANCH_W), lambda b: (b, 0))
    shape = jax.ShapeDtypeStruct((BATCH * SEQ, BRANCH_W), F32)
    return pl.pallas_call(
        functools.partial(_ctx_attn_kernel, lam_init=lam_init),
        grid=(BATCH,),
        in_specs=[col(3), col(4), col(5), col(6), col(7), col(8),
                  pl.BlockSpec((4, DA_HEAD_DIM), lambda b: (0, 0)),
                  pl.BlockSpec((1, DA_V_DIM), lambda b: (0, 0))],
        out_specs=[out, out],
        out_shape=[shape, shape],
        compiler_params=_cparams("arbitrary"),
        name="ctx_attention",
    )(u, u, u, u, u, u, da_lambda, da_subln)


def _rpb_table_kernel(rpb_ref, o_ref):
    qc = lax.broadcasted_iota(jnp.int32, (GRID_W, GRID_W), 0)
    kc = lax.broadcasted_iota(jnp.int32, (GRID_W, GRID_W), 1)
    dc = jnp.clip(kc - qc, -(NA_WIN_COLS - 1), NA_WIN_COLS - 1) + (NA_WIN_COLS - 1)
    r = rpb_ref[0, 0]
    for dr in range(2 * NA_WIN_ROWS - 1):
        acc = jnp.zeros((GRID_W, GRID_W), F32)
        for d in range(2 * NA_WIN_COLS - 1):
            acc = jnp.where(dc == d, r[dr:dr + 1, d:d + 1], acc)
        o_ref[0, 0, dr] = acc


def _rpb_table(na_rpb):
    n_dr, n_dc = 2 * NA_WIN_ROWS - 1, 2 * NA_WIN_COLS - 1
    return pl.pallas_call(
        _rpb_table_kernel,
        grid=(DEPTH, NA_HEADS),
        in_specs=[pl.BlockSpec((1, 1, n_dr, n_dc), lambda l, h: (l, h, 0, 0))],
        out_specs=pl.BlockSpec((1, 1, n_dr, GRID_W, GRID_W), lambda l, h: (l, h, 0, 0, 0)),
        out_shape=jax.ShapeDtypeStruct((DEPTH, NA_HEADS, n_dr, GRID_W, GRID_W), F32),
        compiler_params=_cparams("arbitrary", "arbitrary"),
        name="rpb_table",
    )(na_rpb)


NA_ROWS_PER_STEP = 4


def _na_kernel(q_ref, k_ref, v_ref, kc_ref, vc_ref, bias_ref, o_ref):
    n_lat = NA_WIN_ROWS * GRID_W
    qc = lax.broadcasted_iota(jnp.int32, (GRID_W, n_lat), 0)
    kcol = lax.broadcasted_iota(jnp.int32, (GRID_W, n_lat), 1) % GRID_W
    c0 = jnp.clip(qc - NA_WIN_COLS // 2, 0, GRID_W - NA_WIN_COLS)
    col_ok = (kcol >= c0) & (kcol < c0 + NA_WIN_COLS)
    scale = NA_HEAD_DIM ** -0.5
    for rr in range(NA_ROWS_PER_STEP):
        r = pl.program_id(1) * NA_ROWS_PER_STEP + rr
        r0 = jnp.clip(r - NA_WIN_ROWS // 2, 0, GRID_H - NA_WIN_ROWS)
        start = pl.multiple_of(r0 * GRID_W, GRID_W)
        kwin = k_ref[pl.ds(start, n_lat), :]
        vwin = v_ref[pl.ds(start, n_lat), :]
        dr0 = r0 - r + (NA_WIN_ROWS - 1)
        rows = slice(rr * GRID_W, (rr + 1) * GRID_W)
        for h in range(NA_HEADS):
            sl = slice(h * NA_HEAD_DIM, (h + 1) * NA_HEAD_DIM)
            q = q_ref[rows, sl]
            bias = jnp.concatenate([bias_ref[h, dr0 + w] for w in range(NA_WIN_ROWS)], axis=1)
            s_lat = _bdot_nt(q, kwin[:, sl]) * scale + bias
            s_lat = jnp.where(col_ok, s_lat, NEG_INF)
            s_ctx = _bdot_nt(q, kc_ref[0, :, sl]) * scale
            m = jnp.maximum(jnp.max(s_lat, axis=-1, keepdims=True), jnp.max(s_ctx, axis=-1, keepdims=True))
            p_lat = jnp.exp(s_lat - m)
            p_ctx = jnp.exp(s_ctx - m)
            l = jnp.sum(p_lat, axis=-1, keepdims=True) + jnp.sum(p_ctx, axis=-1, keepdims=True)
            o = _bdot(p_lat, vwin[:, sl]) + _bdot(p_ctx, vc_ref[0, :, sl])
            o_ref[rows, sl] = o / l


def _nbr_attention(u, k_ctx, v_ctx, bias):
    n_dr = 2 * NA_WIN_ROWS - 1
    steps = GRID_H // NA_ROWS_PER_STEP
    rows = NA_ROWS_PER_STEP * GRID_W
    return pl.pallas_call(
        _na_kernel,
        grid=(DEC_BATCH, steps),
        in_specs=[
            pl.BlockSpec((rows, BRANCH_W), lambda b, r: (b * steps + r, 3)),
            pl.BlockSpec((DEC_SEQ, BRANCH_W), lambda b, r: (b, 4)),
            pl.BlockSpec((DEC_SEQ, BRANCH_W), lambda b, r: (b, 5)),
            pl.BlockSpec((1, PAST_LEN, BRANCH_W), lambda b, r: (b, 0, 0)),
            pl.BlockSpec((1, PAST_LEN, BRANCH_W), lambda b, r: (b, 0, 0)),
            pl.BlockSpec((NA_HEADS, n_dr, GRID_W, GRID_W), lambda b, r: (0, 0, 0, 0)),
        ],
        out_specs=pl.BlockSpec((rows, BRANCH_W), lambda b, r: (b * steps + r, 0)),
        out_shape=jax.ShapeDtypeStruct((DEC_BATCH * DEC_SEQ, BRANCH_W), F32),
        compiler_params=_cparams("arbitrary", "arbitrary"),
        name="nbr_attention",
    )(u, u, u, k_ctx, v_ctx, bias)


DA_TQ = 256
DA_KEYS = DEC_SEQ + PAST_LEN


def _rope(x, cos, sin_signed):
    n = x.shape[-1]
    lane = lax.broadcasted_iota(jnp.int32, x.shape, 1)
    partner = jnp.where(lane % 2 == 0, pltpu.roll(x, n - 1, axis=1), pltpu.roll(x, 1, axis=1))
    return x * cos + partner * sin_signed


DA_ONES_ROWS = 16
DA_MAPS_PER_TILE = 128 // DA_HEAD_DIM


def _da_prep_kernel(q_ref, k_ref, v_ref, kc_ref, vc_ref, cos_ref, sin_ref, qt_ref, ko_ref, vt_ref):
    t = pl.program_id(1)

    def put_v(v):
        vt = v.astype(F32).T.astype(BF16)
        ones = jnp.ones((DA_ONES_ROWS, DA_TQ), BF16)
        for h in range(DA_HEADS):
            vt_ref[0, h, 0:DA_V_DIM, :] = vt[h * DA_V_DIM:(h + 1) * DA_V_DIM]
            vt_ref[0, h, DA_V_DIM:DA_V_DIM + DA_ONES_ROWS, :] = ones

    @pl.when(t < DEC_SEQ // DA_TQ)
    def _():
        cos = cos_ref[...]
        sin = sin_ref[...]
        q = _rope(q_ref[...].astype(F32), cos, sin) * (DA_HEAD_DIM ** -0.5 * math.log2(math.e))
        qt_ref[0] = q.T.astype(BF16)
        ko_ref[0] = _rope(k_ref[...].astype(F32), cos, sin).astype(BF16)
        put_v(v_ref[...])

    @pl.when(t == DEC_SEQ // DA_TQ)
    def _():
        ko_ref[0] = kc_ref[0].astype(BF16)
        put_v(vc_ref[0])


def _da_prep(u, k_ctx, v_ctx, cos, sin):
    nt = DEC_SEQ // DA_TQ
    last = nt - 1
    rowblk = lambda j: pl.BlockSpec((DA_TQ, BRANCH_W), lambda b, t, j=j: (b * nt + jnp.minimum(t, last), j))
    tab = pl.BlockSpec((DA_TQ, BRANCH_W), lambda b, t: (jnp.minimum(t, last), 0))
    ctx = pl.BlockSpec((1, PAST_LEN, BRANCH_W), lambda b, t: (b, 0, 0))
    vrows = DA_V_DIM + DA_ONES_ROWS
    return pl.pallas_call(
        _da_prep_kernel,
        grid=(DEC_BATCH, nt + 1),
        in_specs=[rowblk(6), rowblk(7), rowblk(8), ctx, ctx, tab, tab],
        out_specs=[
            pl.BlockSpec((1, BRANCH_W, DA_TQ), lambda b, t: (b, 0, jnp.minimum(t, last))),
            pl.BlockSpec((1, DA_TQ, BRANCH_W), lambda b, t: (b, t, 0)),
            pl.BlockSpec((1, DA_HEADS, vrows, DA_TQ), lambda b, t: (b, 0, 0, t)),
        ],
        out_shape=[
            jax.ShapeDtypeStruct((DEC_BATCH, BRANCH_W, DEC_SEQ), BF16),
            jax.ShapeDtypeStruct((DEC_BATCH, DA_KEYS, BRANCH_W), BF16),
            jax.ShapeDtypeStruct((DEC_BATCH, DA_HEADS, vrows, DA_KEYS), BF16),
        ],
        compiler_params=_cparams("arbitrary", "arbitrary"),
        name="da_prep",
    )(u, u, u, k_ctx, v_ctx, cos, sin)


def _da_kernel(qt_ref, k_ref, vt_ref, lam_ref, sub_ref, o_ref, acc_ref, *, lam_init):
    lam = _da_lambda(lam_ref, lam_init)
    row = lax.broadcasted_iota(jnp.int32, (128, DA_TQ), 0)
    qg = qt_ref[0]
    zero = jnp.zeros_like(qg)
    qbd = jnp.concatenate(
        [jnp.where((row >= j * DA_HEAD_DIM) & (row < (j + 1) * DA_HEAD_DIM), qg, zero) for j in range(DA_MAPS_PER_TILE)],
        axis=1)
    st = jnp.dot(k_ref[0], qbd, preferred_element_type=F32)
    n_slab = DA_KEYS // 256
    mx = jnp.max(jnp.max(st.reshape(n_slab, 256, DA_MAPS_PER_TILE * DA_TQ), axis=0), axis=0, keepdims=True)
    pt = jnp.exp2(st - mx).astype(BF16)
    heads = DA_MAPS_PER_TILE // 2
    for h in range(heads):
        oe = jnp.dot(vt_ref[0, h], pt[:, 2 * h * DA_TQ:(2 * h + 2) * DA_TQ], preferred_element_type=F32)
        os = [oe[0:DA_V_DIM, i * DA_TQ:(i + 1) * DA_TQ] / oe[DA_V_DIM:DA_V_DIM + 1, i * DA_TQ:(i + 1) * DA_TQ]
              for i in range(2)]
        ot = os[0] - lam * os[1]
        ot = ot * lax.rsqrt(jnp.mean(ot * ot, axis=0, keepdims=True) + EPS) * sub_ref[...]
        acc_ref[h * DA_V_DIM:(h + 1) * DA_V_DIM, :] = ot * (1.0 - lam_init)
    o_ref[...] = acc_ref[...].T


def _diff_attention(qt, k, vt, da_lambda, subln_col, lam_init):
    nt = DEC_SEQ // DA_TQ
    vrows = DA_V_DIM + DA_ONES_ROWS
    groups = BRANCH_W // 128
    heads = DA_MAPS_PER_TILE // 2
    return pl.pallas_call(
        functools.partial(_da_kernel, lam_init=lam_init),
        grid=(DEC_BATCH, groups, nt),
        in_specs=[
            pl.BlockSpec((1, 128, DA_TQ), lambda b, g, t: (b, g, t)),
            pl.BlockSpec((1, DA_KEYS, 128), lambda b, g, t: (b, 0, g)),
            pl.BlockSpec((1, heads, vrows, DA_KEYS), lambda b, g, t: (b, g, 0, 0)),
            pl.BlockSpec((4, DA_HEAD_DIM), lambda b, g, t: (0, 0)),
            pl.BlockSpec((DA_V_DIM, 1), lambda b, g, t: (0, 0)),
        ],
        out_specs=pl.BlockSpec((DA_TQ, 128), lambda b, g, t: (b * nt + t, g)),
        out_shape=jax.ShapeDtypeStruct((DEC_BATCH * DEC_SEQ, BRANCH_W), F32),
        scratch_shapes=[pltpu.VMEM((128, DA_TQ), F32)],
        compiler_params=_cparams("arbitrary", "arbitrary", "arbitrary"),
        name="diff_attention",
    )(qt, k, vt, da_lambda, subln_col)


def _rope_tables():
    pos = np.arange(DEC_SEQ)
    row = (pos // GRID_W).astype(np.float32)
    col = (pos % GRID_W).astype(np.float32)
    n_freq = DA_HEAD_DIM // 4
    inv = (np.float32(ROPE_BASE) ** (-np.arange(n_freq, dtype=np.float32) / n_freq)).astype(np.float32)
    ang = np.concatenate([row[:, None] * inv[None, :], col[:, None] * inv[None, :]], axis=-1)
    ang = ang.astype(np.float64)
    cos = np.repeat(np.cos(ang), 2, axis=-1)
    sin = np.repeat(np.sin(ang), 2, axis=-1)
    sign = np.where(np.arange(DA_HEAD_DIM) % 2 == 0, -1.0, 1.0)
    reps = BRANCH_W // DA_HEAD_DIM
    cos = np.tile(cos, (1, reps)).astype(np.float32)
    sin = np.tile(sin * sign[None, :], (1, reps)).astype(np.float32)
    return jnp.asarray(cos), jnp.asarray(sin)


def _filt_hidden_kernel(feat_ref, w1_ref, b1_ref, w2_ref, b2_ref, fr_ref, o_ref):
    fr = fr_ref[0]
    h = jnp.sin(fr * (jnp.dot(feat_ref[...], w1_ref[0], precision=HIGHEST, preferred_element_type=F32) + b1_ref[0]))
    o_ref[0] = jnp.sin(fr * (jnp.dot(h, w2_ref[0], precision=HIGHEST, preferred_element_type=F32) + b2_ref[0]))


def _filt_kernel(h_ref, w3f_ref, w3b_ref, dec_ref, hf_ref, hb_ref):
    h = h_ref[0]
    dec = dec_ref[...]
    hf = jnp.dot(h, w3f_ref[0], precision=HIGHEST, preferred_element_type=F32) * dec
    hb = jnp.dot(h, w3b_ref[0], precision=HIGHEST, preferred_element_type=F32) * dec
    row = lax.broadcasted_iota(jnp.int32, hb.shape, 0)
    hb = jnp.where(row == 0, 0.0, hb)
    nrm = jnp.sum(jnp.abs(hf), axis=0, keepdims=True) + jnp.sum(jnp.abs(hb), axis=0, keepdims=True)
    hf_ref[0, 0] = hf / nrm
    hb_ref[0, 0] = hb / nrm


def _hyena_pos_tables(L):
    f32 = np.float32
    pos = np.arange(L, dtype=f32)
    t = (pos / f32(L)).astype(f32)
    bands = np.linspace(1e-4, HY_POS_BANDS - 1, HY_POS_BANDS, dtype=f32)
    ang = (f32(2 * math.pi / L) * pos[:, None] * bands[None, :]).astype(np.float64)
    feats = np.zeros((L, HY_FILT_HIDDEN), f32)
    feats[:, 0] = t
    feats[:, 1:1 + HY_POS_BANDS] = np.cos(ang)
    feats[:, 1 + HY_POS_BANDS:HY_POS_DIM] = -np.sin(ang)
    deltas = np.linspace(math.log(HY_DECAY_TARGET) / HY_SLOW_DECAY,
                         math.log(HY_DECAY_TARGET) / HY_FAST_DECAY, BRANCH_W, dtype=f32)
    decay = np.exp((-t[:, None] * np.abs(deltas)[None, :]).astype(np.float64)).astype(f32)
    return jnp.asarray(feats), jnp.asarray(decay)


def _hyena_filters(L, w1p, b1, w2, b2, w3, freq):
    feats, decay = _hyena_pos_tables(L)
    cb = 128
    ncb = BRANCH_W // cb
    small = lambda shape: pl.BlockSpec((1,) + shape, lambda l: (l, 0, 0))
    hidden = pl.pallas_call(
        _filt_hidden_kernel,
        grid=(DEPTH,),
        in_specs=[
            pl.BlockSpec((L, HY_FILT_HIDDEN), lambda l: (0, 0)),
            small((HY_FILT_HIDDEN, HY_FILT_HIDDEN)), small((1, HY_FILT_HIDDEN)),
            small((HY_FILT_HIDDEN, HY_FILT_HIDDEN)), small((1, HY_FILT_HIDDEN)),
            small((1, HY_FILT_HIDDEN)),
        ],
        out_specs=pl.BlockSpec((1, L, HY_FILT_HIDDEN), lambda l: (l, 0, 0)),
        out_shape=jax.ShapeDtypeStruct((DEPTH, L, HY_FILT_HIDDEN), F32),
        compiler_params=_cparams("arbitrary"),
        name=f"hyena_filter_hidden_{L}",
    )(feats, w1p, b1, w2, b2, freq)
    shape = jax.ShapeDtypeStruct((DEPTH, 2, L, BRANCH_W), F32)
    out = pl.BlockSpec((1, 1, L, cb), lambda l, o, c: (l, o, 0, c))
    return pl.pallas_call(
        _filt_kernel,
        grid=(DEPTH, 2, ncb),
        in_specs=[
            pl.BlockSpec((1, L, HY_FILT_HIDDEN), lambda l, o, c: (l, 0, 0)),
            pl.BlockSpec((1, HY_FILT_HIDDEN, cb), lambda l, o, c: (l, 0, o * 2 * ncb + c)),
            pl.BlockSpec((1, HY_FILT_HIDDEN, cb), lambda l, o, c: (l, 0, o * 2 * ncb + ncb + c)),
            pl.BlockSpec((L, cb), lambda l, o, c: (0, c)),
        ],
        out_specs=[out, out],
        out_shape=[shape, shape],
        compiler_params=_cparams("arbitrary", "arbitrary", "arbitrary"),
        name=f"hyena_filters_{L}",
    )(hidden, w3, w3, decay)


def _short_conv(u, w_ref, b_ref, seq_len):
    n = u.shape[0]
    t = lax.broadcasted_iota(jnp.int32, u.shape, 0) % seq_len
    prev = jnp.where(t == 0, 0.0, pltpu.roll(u, 1, axis=0))
    nxt = jnp.where(t == seq_len - 1, 0.0, pltpu.roll(u, n - 1, axis=0))
    return prev * w_ref[0:1, :] + u * w_ref[1:2, :] + nxt * w_ref[2:3, :] + b_ref[...]


def _dft_direct_mats():
    n, half = 2 * SEQ, SEQ
    k = np.arange(n)[:, None].astype(np.float64)
    t = np.arange(half)[None, :].astype(np.float64)
    ang = 2 * np.pi * k * t / n
    fr, fi = np.cos(ang), -np.sin(ang)
    mf = np.block([[fr, -fi], [fi, fr]])
    gr, gi = np.cos(ang).T / n, np.sin(ang).T / n
    mi = np.block([[gr, -gi], [gi, gr]])
    return mf.astype(np.float32), mi.astype(np.float32)


def _spec_direct_kernel(hf_ref, hb_ref, m_ref, o_ref):
    m = m_ref[...]
    wf = jnp.dot(m, hf_ref[0, 0], precision=HIGHEST, preferred_element_type=F32)
    wb = jnp.dot(m, hb_ref[0, 0], precision=HIGHEST, preferred_element_type=F32)
    n = 2 * SEQ
    o_ref[0, 0, 0:n] = wf[0:n] + wb[0:n]
    o_ref[0, 0, n:2 * n] = wf[n:2 * n] - wb[n:2 * n]


def _spec_direct(hf, hb, mf_real):
    n = 2 * SEQ
    blk = pl.BlockSpec((1, 1, SEQ, BRANCH_W), lambda l, o: (l, o, 0, 0))
    return pl.pallas_call(
        _spec_direct_kernel,
        grid=(DEPTH, 2),
        in_specs=[blk, blk, pl.BlockSpec((2 * n, SEQ), lambda l, o: (0, 0))],
        out_specs=pl.BlockSpec((1, 1, 2 * n, BRANCH_W), lambda l, o: (l, o, 0, 0)),
        out_shape=jax.ShapeDtypeStruct((DEPTH, 2, 2 * n, BRANCH_W), F32),
        compiler_params=_cparams("arbitrary", "arbitrary"),
        name="hyena_spectrum_direct",
    )(hf, hb, mf_real)


def _lconv_direct_kernel(s_ref, g_ref, cws_ref, cbs_ref, cwg_ref, cbg_ref, h_ref, bias_ref, mf_ref, mi_ref, o_ref,
                         *, conv_sig):
    n = 2 * SEQ
    sig = s_ref[...]
    if conv_sig:
        sig = _short_conv(sig, cws_ref, cbs_ref, SEQ)
    gate = _short_conv(g_ref[...], cwg_ref, cbg_ref, SEQ)
    z = jnp.dot(mf_ref[...], sig.astype(BF16), preferred_element_type=F32)
    zr, zi = z[0:n], z[n:2 * n]
    hr, hi = h_ref[0:n], h_ref[n:2 * n]
    y = jnp.concatenate([zr * hr - zi * hi, zr * hi + zi * hr], axis=0)
    y = jnp.dot(mi_ref[...], y.astype(BF16), preferred_element_type=F32)
    o_ref[...] = gate * (y + sig * bias_ref[...])


def _lconv_direct(sig, sig_col, gate_src, gate_col, conv_w, conv_b, spec, bias, mf, mi, conv_sig):
    n = 2 * SEQ
    rows = 2 * SEQ
    T = sig.shape[0]
    return pl.pallas_call(
        functools.partial(_lconv_direct_kernel, conv_sig=conv_sig),
        grid=(T // rows,),
        in_specs=[
            pl.BlockSpec((rows, BRANCH_W), lambda p: (p, sig_col)),
            pl.BlockSpec((rows, BRANCH_W), lambda p: (p, gate_col)),
            pl.BlockSpec((3, BRANCH_W), lambda p: (0, 0)),
            pl.BlockSpec((1, BRANCH_W), lambda p: (0, 0)),
            pl.BlockSpec((3, BRANCH_W), lambda p: (0, gate_col)),
            pl.BlockSpec((1, BRANCH_W), lambda p: (0, gate_col)),
            pl.BlockSpec((2 * n, BRANCH_W), lambda p: (0, 0)),
            pl.BlockSpec((1, BRANCH_W), lambda p: (0, 0)),
            pl.BlockSpec((2 * n, rows), lambda p: (0, 0)),
            pl.BlockSpec((rows, 2 * n), lambda p: (0, 0)),
        ],
        out_specs=pl.BlockSpec((rows, BRANCH_W), lambda p: (p, 0)),
        out_shape=jax.ShapeDtypeStruct((T, BRANCH_W), F32),
        compiler_params=_cparams("arbitrary"),
        name="hyena_lconv_direct",
    )(sig, gate_src, conv_w, conv_b, conv_w, conv_b, spec, bias, mf, mi)


def _dft_two_stage_mats():
    no, ni, half, n = FFT_NO, FFT_NI, FFT_HALF, FFT_N
    f64 = np.float64
    k1 = np.arange(no, dtype=f64)
    n_o = np.arange(half, dtype=f64)
    n_i = np.arange(ni, dtype=f64)
    ang = 2 * np.pi * (n_i[:, None, None] * k1[None, :, None] / n + k1[None, :, None] * n_o[None, None, :] / no)
    tr, ti = np.cos(ang), -np.sin(ang)
    m1 = np.concatenate([np.concatenate([tr, -ti], axis=2), np.concatenate([ti, tr], axis=2)], axis=1)
    k2 = np.arange(ni, dtype=f64)
    ang2 = 2 * np.pi * k2[:, None] * n_i[None, :] / ni
    f2r, f2i = np.cos(ang2), -np.sin(ang2)
    m2 = np.block([[f2r, -f2i], [f2i, f2r]])
    m2c = np.block([[f2r, f2i], [-f2i, f2r]])
    sr, si = np.transpose(tr, (0, 2, 1)) / n, -np.transpose(ti, (0, 2, 1)) / n
    m3 = np.concatenate([np.concatenate([sr, -si], axis=2), np.concatenate([si, sr], axis=2)], axis=1)
    return (m1.astype(np.float32), m2.astype(np.float32), m2c.astype(np.float32), m3.astype(np.float32))


def _fwd_stage1(za_ref, zb_ref, m1_ref, w_ref):
    def body(ni, carry):
        a = za_ref[pl.ds(ni, FFT_HALF, stride=FFT_NI), :]
        if zb_ref is None:
            out = jnp.dot(m1_ref[ni][:, 0:FFT_HALF], a.astype(BF16), preferred_element_type=F32)
        else:
            b = zb_ref[pl.ds(ni, FFT_HALF, stride=FFT_NI), :]
            out = jnp.dot(m1_ref[ni], jnp.concatenate([a, b], axis=0).astype(BF16), preferred_element_type=F32)
        w_ref[pl.ds(ni, FFT_NO, stride=2 * FFT_NI), :] = out[0:FFT_NO]
        w_ref[pl.ds(FFT_NI + ni, FFT_NO, stride=2 * FFT_NI), :] = out[FFT_NO:2 * FFT_NO]
        return carry

    lax.fori_loop(0, FFT_NI, body, 0, unroll=FFT_UNROLL)


def _spec_two_stage_kernel(hf_ref, hb_ref, m1_ref, m2_ref, o_ref, wf_ref, wb_ref):
    _fwd_stage1(hf_ref.at[0, 0], None, m1_ref, wf_ref)
    _fwd_stage1(hb_ref.at[0, 0], None, m1_ref, wb_ref)
    blk = 2 * FFT_NI

    def body(k1, carry):
        rows = pl.ds(pl.multiple_of(k1 * blk, blk), blk)
        xf = jnp.dot(m2_ref[...], wf_ref[rows, :].astype(BF16), preferred_element_type=F32)
        xb = jnp.dot(m2_ref[...], wb_ref[rows, :].astype(BF16), preferred_element_type=F32)
        o_ref[0, 0, rows, :] = jnp.concatenate(
            [xf[0:FFT_NI] + xb[0:FFT_NI], xf[FFT_NI:blk] - xb[FFT_NI:blk]], axis=0)
        return carry

    lax.fori_loop(0, FFT_NO, body, 0, unroll=FFT_UNROLL)


def _spec_two_stage(hf, hb, m1, m2):
    cb = LCONV_CB
    blk = pl.BlockSpec((1, 1, DEC_SEQ, cb), lambda l, o, c: (l, o, 0, c))
    return pl.pallas_call(
        _spec_two_stage_kernel,
        grid=(DEPTH, 2, BRANCH_W // cb),
        in_specs=[blk, blk,
                  pl.BlockSpec((FFT_NI, 2 * FFT_NO, 2 * FFT_HALF), lambda l, o, c: (0, 0, 0)),
                  pl.BlockSpec((2 * FFT_NI, 2 * FFT_NI), lambda l, o, c: (0, 0))],
        out_specs=pl.BlockSpec((1, 1, 2 * FFT_N, cb), lambda l, o, c: (l, o, 0, c)),
        out_shape=jax.ShapeDtypeStruct((DEPTH, 2, 2 * FFT_N, BRANCH_W), F32),
        scratch_shapes=[pltpu.VMEM((2 * FFT_N, cb), F32), pltpu.VMEM((2 * FFT_N, cb), F32)],
        compiler_params=_cparams("arbitrary", "arbitrary", "arbitrary"),
        name="hyena_spectrum_two_stage",
    )(hf, hb, m1, m2)


def _lconv_two_stage_kernel(s_ref, g_ref, cws_ref, cbs_ref, cwg_ref, cbg_ref, h_ref, bias_ref,
                            m1_ref, m2_ref, m2c_ref, m3_ref, o_ref, z_ref, w_ref, *, conv_sig):
    for b in range(2):
        sig = s_ref[b].astype(F32)
        if conv_sig:
            sig = _short_conv(sig, cws_ref, cbs_ref, DEC_SEQ)
        z_ref[b] = sig
    _fwd_stage1(z_ref.at[0], z_ref.at[1], m1_ref, w_ref)
    blk = 2 * FFT_NI

    def mid(k1, carry):
        rows = pl.ds(pl.multiple_of(k1 * blk, blk), blk)
        x = jnp.dot(m2_ref[...], w_ref[rows, :].astype(BF16), preferred_element_type=F32)
        h = h_ref[rows, :]
        xr, xi = x[0:FFT_NI], x[FFT_NI:blk]
        hr, hi = h[0:FFT_NI], h[FFT_NI:blk]
        y = jnp.concatenate([xr * hr - xi * hi, xr * hi + xi * hr], axis=0)
        w_ref[rows, :] = jnp.dot(m2c_ref[...], y.astype(BF16), preferred_element_type=F32)
        return carry

    lax.fori_loop(0, FFT_NO, mid, 0, unroll=FFT_UNROLL)

    def last(ni, carry):
        cr = w_ref[pl.ds(ni, FFT_NO, stride=blk), :]
        ci = w_ref[pl.ds(FFT_NI + ni, FFT_NO, stride=blk), :]
        y = jnp.dot(m3_ref[ni], jnp.concatenate([cr, ci], axis=0).astype(BF16), preferred_element_type=F32)
        o_ref[0, pl.ds(ni, FFT_HALF, stride=FFT_NI), :] = y[0:FFT_HALF]
        o_ref[1, pl.ds(ni, FFT_HALF, stride=FFT_NI), :] = y[FFT_HALF:2 * FFT_HALF]
        return carry

    lax.fori_loop(0, FFT_NI, last, 0, unroll=FFT_UNROLL)
    for b in range(2):
        gate = _short_conv(g_ref[b].astype(F32), cwg_ref, cbg_ref, DEC_SEQ)
        sig = z_ref[b]
        o_ref[b] = gate * (o_ref[b] + sig * bias_ref[...])


def _lconv_two_stage(sig, sig_col, gate_src, gate_col, conv_w, conv_b, spec, bias, mats, conv_sig):
    cb = LCONV_CB
    ncb = BRANCH_W // cb
    m1, m2, m2c, m3 = mats
    const3 = lambda c, p: (0, 0, 0)
    const2 = lambda c, p: (0, 0)
    return pl.pallas_call(
        functools.partial(_lconv_two_stage_kernel, conv_sig=conv_sig),
        grid=(ncb, DEC_BATCH // 2),
        in_specs=[
            pl.BlockSpec((2, DEC_SEQ, cb), lambda c, p: (p, 0, sig_col * ncb + c)),
            pl.BlockSpec((2, DEC_SEQ, cb), lambda c, p: (p, 0, gate_col * ncb + c)),
            pl.BlockSpec((3, cb), lambda c, p: (0, c)),
            pl.BlockSpec((1, cb), lambda c, p: (0, c)),
            pl.BlockSpec((3, cb), lambda c, p: (0, gate_col * ncb + c)),
            pl.BlockSpec((1, cb), lambda c, p: (0, gate_col * ncb + c)),
            pl.BlockSpec((2 * FFT_N, cb), lambda c, p: (0, c)),
            pl.BlockSpec((1, cb), lambda c, p: (0, c)),
            pl.BlockSpec(m1.shape, const3),
            pl.BlockSpec(m2.shape, const2),
            pl.BlockSpec(m2c.shape, const2),
            pl.BlockSpec(m3.shape, const3),
        ],
        out_specs=pl.BlockSpec((2, DEC_SEQ, cb), lambda c, p: (p, 0, c)),
        out_shape=jax.ShapeDtypeStruct((DEC_BATCH, DEC_SEQ, BRANCH_W), F32),
        scratch_shapes=[pltpu.VMEM((2, DEC_SEQ, cb), F32), pltpu.VMEM((2 * FFT_N, cb), F32)],
        compiler_params=_cparams("arbitrary", "arbitrary"),
        name="hyena_lconv_two_stage",
    )(sig, gate_src, conv_w, conv_b, conv_w, conv_b, spec, bias, m1, m2, m2c, m3)


def kernel(x_prompt, x_sample, cache_na_k, cache_na_v, cache_da_k, cache_da_v, c, c_ctx, w_ada, b_ada, norm_mix,
           norm_ffn, w_in, hy_conv_w, hy_conv_b, hy_filt_w1, hy_filt_b1, hy_filt_w2, hy_filt_b2, hy_filt_w3,
           hy_filt_freq, hy_bias, na_rpb, da_lambda, da_subln, w_lift, w_out, w_ffn_in, w_ffn_out, norm_final):
    TP, TS = BATCH * SEQ, DEC_BATCH * DEC_SEQ
    xp = x_prompt.reshape(TP, D_MODEL)
    xs = x_sample.reshape(TS, D_MODEL)

    cc = jnp.concatenate([c_ctx[None, :], c, jnp.zeros((8 - 1 - DEC_BATCH, D_MODEL), F32)], axis=0)
    mod = _modulation(cc, w_ada, b_ada)
    mod_p = mod[:, 0:1].reshape(DEPTH, 1, 1, 6 * D_MODEL)
    mod_s = mod[:, 1:1 + DEC_BATCH].reshape(DEPTH, DEC_BATCH, 1, 6 * D_MODEL)

    w_mix = w_in[:, :, :MIX_W].astype(BF16)
    w_gate = w_in[:, :, MIX_W:].astype(BF16)
    w_lift_b = w_lift.astype(BF16)
    w_out_b = w_out.astype(BF16)
    w_ffn_in_b = w_ffn_in.astype(BF16)
    w_ffn_out_b = w_ffn_out.astype(BF16)
    g_mix = norm_mix.reshape(DEPTH, 1, D_MODEL)
    g_ffn = norm_ffn.reshape(DEPTH, 1, D_MODEL)
    g_fin = norm_final.reshape(1, D_MODEL)
    subln = da_subln.reshape(DEPTH, 1, DA_V_DIM)
    subln_col = da_subln.reshape(DEPTH, DA_V_DIM, 1)

    w1p = jnp.pad(hy_filt_w1, ((0, 0), (0, HY_FILT_HIDDEN - HY_POS_DIM), (0, 0)))
    b1 = hy_filt_b1.reshape(DEPTH, 1, HY_FILT_HIDDEN)
    b2 = hy_filt_b2.reshape(DEPTH, 1, HY_FILT_HIDDEN)
    fr = hy_filt_freq.reshape(DEPTH, 1, HY_FILT_HIDDEN)
    mf, mi = _dft_direct_mats()
    mats = _dft_two_stage_mats()
    hf_p, hb_p = _hyena_filters(SEQ, w1p, b1, hy_filt_w2, b2, hy_filt_w3, fr)
    hf_s, hb_s = _hyena_filters(DEC_SEQ, w1p, b1, hy_filt_w2, b2, hy_filt_w3, fr)
    spec_p = _spec_direct(hf_p, hb_p, jnp.asarray(mf[:, 0:SEQ]))
    mf_b, mi_b = jnp.asarray(mf, dtype=BF16), jnp.asarray(mi, dtype=BF16)
    mats_b = tuple(jnp.asarray(m, dtype=BF16) for m in mats)
    spec_s = _spec_two_stage(hf_s, hb_s, mats_b[0], mats_b[1])
    conv_b = hy_conv_b.reshape(DEPTH, 1, 3 * BRANCH_W)

    rpb = _rpb_table(na_rpb)
    cos, sin = _rope_tables()
    ck_na = cache_na_k.reshape(DEC_BATCH, DEPTH, PAST_LEN, BRANCH_W)
    cv_na = cache_na_v.reshape(DEC_BATCH, DEPTH, PAST_LEN, BRANCH_W)
    ck_da = cache_da_k.reshape(DEC_BATCH, DEPTH, PAST_LEN, BRANCH_W)
    cv_da = cache_da_v.reshape(DEC_BATCH, DEPTH, PAST_LEN, BRANCH_W)

    caches = []
    for l in range(DEPTH):
        lam_init = 0.8 - 0.6 * math.exp(-0.3 * l)
        final = l == DEPTH - 1

        u = _in_proj(xp, g_mix[l], mod_p[l], w_mix[l], TP, F32)
        caches.append(u[:, 4 * BRANCH_W:5 * BRANCH_W])
        caches.append(u[:, 5 * BRANCH_W:6 * BRANCH_W])
        caches.append(u[:, 7 * BRANCH_W:8 * BRANCH_W])
        caches.append(u[:, 8 * BRANCH_W:9 * BRANCH_W])
        z1 = _lconv_direct(u, 0, u, 1, hy_conv_w[l], conv_b[l], spec_p[l, 0], hy_bias[l, 0:1], mf_b, mi_b, True)
        y_hy = _lconv_direct(z1, 0, u, 2, hy_conv_w[l], conv_b[l], spec_p[l, 1], hy_bias[l, 1:2], mf_b, mi_b, False)
        y_na, y_da = _ctx_attention(u, da_lambda[l], subln[l], lam_init)
        xp = _merge_out(xp, g_mix[l], mod_p[l], y_hy, y_na, y_da, w_gate[l], w_lift_b[l], w_out_b[l], TP)
        xp = _ffn(xp, g_ffn[l], mod_p[l], w_ffn_in_b[l], w_ffn_out_b[l], g_fin, TP, final)

        u = _in_proj(xs, g_mix[l], mod_s[l], w_mix[l], DEC_SEQ, BF16)
        u3 = u.reshape(DEC_BATCH, DEC_SEQ, MIX_W)
        z1 = _lconv_two_stage(u3, 0, u3, 1, hy_conv_w[l], conv_b[l], spec_s[l, 0], hy_bias[l, 0:1], mats_b, True)
        y_hy = _lconv_two_stage(z1, 0, u3, 2, hy_conv_w[l], conv_b[l], spec_s[l, 1], hy_bias[l, 1:2], mats_b, False)
        y_hy = y_hy.reshape(TS, BRANCH_W)
        y_na = _nbr_attention(u, ck_na[:, l], cv_na[:, l], rpb[l])
        q, kt, v = _da_prep(u, ck_da[:, l], cv_da[:, l], cos, sin)
        y_da = _diff_attention(q, kt, v, da_lambda[l], subln_col[l], lam_init)
        xs = _merge_out(xs, g_mix[l], mod_s[l], y_hy, y_na, y_da, w_gate[l], w_lift_b[l], w_out_b[l], DEC_SEQ)
        xs = _ffn(xs, g_ffn[l], mod_s[l], w_ffn_in_b[l], w_ffn_out_b[l], g_fin, DEC_SEQ, final)

    def stack(i, last):
        parts = [caches[4 * l + i].reshape(BATCH, SEQ, 8, last) for l in range(DEPTH)]
        return jnp.stack(parts, axis=1)

    y_prompt = xp.reshape(BATCH, SEQ, D_MODEL)
    y_sample = xs.reshape(DEC_BATCH, DEC_SEQ, D_MODEL)
    return (y_prompt, y_sample, stack(0, NA_HEAD_DIM), stack(1, NA_HEAD_DIM),
            stack(2, 2 * DA_HEAD_DIM), stack(3, DA_V_DIM))
```

```python
import functools
import math

import numpy as np
import jax
import jax.numpy as jnp
from jax import lax
from jax.experimental import pallas as pl
from jax.experimental.pallas import tpu as pltpu

F32 = jnp.float32
BF16 = jnp.bfloat16
HIGHEST = lax.Precision.HIGHEST

D_MODEL = 1024
BATCH = 32
SEQ = 256
DEPTH = 4
DEC_BATCH = 4
DEC_SEQ = 4096
PAST_LEN = 256
GRID_W = 64
GRID_H = DEC_SEQ // GRID_W
BRANCH_W = 512
HY_POS_BANDS = 16
HY_POS_DIM = 1 + 2 * HY_POS_BANDS
HY_FILT_HIDDEN = 64
HY_DECAY_TARGET = 1e-2
HY_FAST_DECAY = 0.3
HY_SLOW_DECAY = 1.5
NA_HEADS = 8
NA_HEAD_DIM = 64
NA_WIN_ROWS = 8
NA_WIN_COLS = 16
DA_HEADS = 8
DA_HEAD_DIM = 32
DA_V_DIM = 64
D_FF = 2816
MIX_W = 9 * BRANCH_W
ROPE_BASE = 10000.0
EPS = 1e-6
NEG_INF = -1e30

VMEM_LIMIT_BYTES = 56 * 1024 * 1024

FFT_N = 2 * DEC_SEQ
FFT_NO = 64
FFT_NI = 128
FFT_HALF = FFT_NO // 2
FFT_UNROLL = 4
LCONV_CB = 128


def _cparams(*sem):
    return pltpu.CompilerParams(dimension_semantics=sem, vmem_limit_bytes=VMEM_LIMIT_BYTES)


def _sigmoid(x):
    return 1.0 / (1.0 + jnp.exp(-x))


def _rms(x, g):
    return x * lax.rsqrt(jnp.mean(x * x, axis=-1, keepdims=True) + EPS) * g


def _modnorm(x, g, shift, scale):
    return _rms(x, g) * (1.0 + scale) + shift


def _bdot(a, b):
    return jnp.dot(a.astype(BF16), b.astype(BF16), preferred_element_type=F32)


def _bdot_nt(a, b):
    return lax.dot_general(a.astype(BF16), b.astype(BF16), (((1,), (1,)), ((), ())),
                           preferred_element_type=F32)


def _mod_kernel(c_ref, w_ref, b_ref, o_ref):
    c = c_ref[...]
    s = c * _sigmoid(c)
    o_ref[0] = jnp.dot(s, w_ref[0], precision=HIGHEST, preferred_element_type=F32) + b_ref[0]


def _modulation(cc, w_ada, b_ada):
    nt = 6
    return pl.pallas_call(
        _mod_kernel,
        grid=(DEPTH, nt),
        in_specs=[
            pl.BlockSpec((8, D_MODEL), lambda l, j: (0, 0)),
            pl.BlockSpec((1, D_MODEL, D_MODEL), lambda l, j: (l, 0, j)),
            pl.BlockSpec((1, 1, D_MODEL), lambda l, j: (l, 0, j)),
        ],
        out_specs=pl.BlockSpec((1, 8, D_MODEL), lambda l, j: (l, 0, j)),
        out_shape=jax.ShapeDtypeStruct((DEPTH, 8, 6 * D_MODEL), F32),
        compiler_params=_cparams("arbitrary", "arbitrary"),
        name="modulation",
    )(cc, w_ada, b_ada.reshape(DEPTH, 1, 6 * D_MODEL))


IN_TM = 1024
IN_TN = 3 * BRANCH_W
CACHE_BLOCKS = (4, 5, 7, 8)


def _in_kernel(*refs, n_alias, cache_tiles):
    x_ref, g_ref, mod_ref, w_ref = refs[:4]
    o_ref = refs[4 + n_alias]
    cache_refs = refs[5 + n_alias:5 + n_alias + len(cache_tiles)]
    h_ref = refs[-1]
    j = pl.program_id(1)

    @pl.when(j == 0)
    def _():
        m = mod_ref[0]
        h = _modnorm(x_ref[...], g_ref[...], m[:, 0:D_MODEL], m[:, D_MODEL:2 * D_MODEL])
        h_ref[...] = h.astype(BF16)

    res = jnp.dot(h_ref[...], w_ref[...], preferred_element_type=F32)
    o_ref[...] = res.astype(o_ref.dtype)
    for (tile, off), c_ref in zip(cache_tiles, cache_refs):
        @pl.when(j == tile)
        def _(c_ref=c_ref, off=off):
            c_ref[...] = res[:, off:off + BRANCH_W].reshape(c_ref.shape)


def _in_proj(x, g, mod, w, rows_per_mod, out_dtype, layer=None, caches=None):
    T = x.shape[0]
    tm, tn = IN_TM, IN_TN
    per = rows_per_mod // tm
    in_specs = [
        pl.BlockSpec((tm, D_MODEL), lambda i, j: (i, 0)),
        pl.BlockSpec((1, D_MODEL), lambda i, j: (0, 0)),
        pl.BlockSpec((1, 1, 6 * D_MODEL), lambda i, j: (i // per, 0, 0)),
        pl.BlockSpec((D_MODEL, tn), lambda i, j: (0, j)),
    ]
    out_specs = [pl.BlockSpec((tm, tn), lambda i, j: (i, j))]
    out_shape = [jax.ShapeDtypeStruct((T, MIX_W), out_dtype)]
    args = [x, g, mod, w]
    cache_tiles, aliases = (), {}
    if layer is not None:
        seqs = tm // SEQ
        cache_tiles = tuple(divmod(c * BRANCH_W, tn) for c in CACHE_BLOCKS)
        out_specs += [pl.BlockSpec((seqs, 1, SEQ, BRANCH_W), lambda i, j: (i, layer, 0, 0))] * len(CACHE_BLOCKS)
        out_shape += [jax.ShapeDtypeStruct((T // SEQ, DEPTH, SEQ, BRANCH_W), F32)] * len(CACHE_BLOCKS)
        if caches is not None:
            in_specs += [pl.BlockSpec(memory_space=pl.ANY)] * len(caches)
            aliases = {4 + n: 1 + n for n in range(len(caches))}
            args += list(caches)
    n_alias = len(args) - 4
    outs = pl.pallas_call(
        functools.partial(_in_kernel, n_alias=n_alias, cache_tiles=cache_tiles),
        grid=(T // tm, MIX_W // tn),
        in_specs=in_specs,
        out_specs=out_specs,
        out_shape=out_shape,
        input_output_aliases=aliases,
        scratch_shapes=[pltpu.VMEM((tm, D_MODEL), BF16)],
        compiler_params=_cparams("arbitrary", "arbitrary"),
        name="in_proj",
    )(*args)
    return outs[0] if layer is None else (outs[0], tuple(outs[1:]))


def _mid_kernel(x_ref, g_ref, mod_ref, yh_ref, yn_ref, yd_ref, wg_ref, wl_ref, wo_ref, o_ref):
    m = mod_ref[0]
    x = x_ref[...]
    h = _modnorm(x, g_ref[...], m[:, 0:D_MODEL], m[:, D_MODEL:2 * D_MODEL]).astype(BF16)
    merged = None
    for br, y_ref in enumerate((yh_ref, yn_ref, yd_ref)):
        gate = _sigmoid(jnp.dot(h, wg_ref[:, br * D_MODEL:(br + 1) * D_MODEL], preferred_element_type=F32))
        lift = jnp.dot(y_ref[...].astype(BF16), wl_ref[br], preferred_element_type=F32)
        t = gate * lift
        merged = t if merged is None else merged + t
    o_ref[...] = x + m[:, 2 * D_MODEL:3 * D_MODEL] * _bdot(merged, wo_ref[...])


def _merge_out(x, g, mod, y_hy, y_na, y_da, w_gate, w_lift, w_out, rows_per_mod):
    T = x.shape[0]
    tm = 512
    per = rows_per_mod // tm
    row = lambda i: (i, 0)
    const2 = lambda i: (0, 0)
    return pl.pallas_call(
        _mid_kernel,
        grid=(T // tm,),
        in_specs=[
            pl.BlockSpec((tm, D_MODEL), row),
            pl.BlockSpec((1, D_MODEL), const2),
            pl.BlockSpec((1, 1, 6 * D_MODEL), lambda i: (i // per, 0, 0)),
            pl.BlockSpec((tm, BRANCH_W), row),
            pl.BlockSpec((tm, BRANCH_W), row),
            pl.BlockSpec((tm, BRANCH_W), row),
            pl.BlockSpec((D_MODEL, 3 * D_MODEL), const2),
            pl.BlockSpec((3, BRANCH_W, D_MODEL), lambda i: (0, 0, 0)),
            pl.BlockSpec((D_MODEL, D_MODEL), const2),
        ],
        out_specs=pl.BlockSpec((tm, D_MODEL), row),
        out_shape=jax.ShapeDtypeStruct((T, D_MODEL), F32),
        compiler_params=_cparams("arbitrary"),
        name="merge_out",
    )(x, g, mod, y_hy, y_na, y_da, w_gate, w_lift, w_out)


FFN_CHUNK = D_FF // 2


def _ffn_kernel(x_ref, g_ref, mod_ref, w1g_ref, w1u_ref, w2_ref, gf_ref, o_ref, h_ref, acc_ref, *, final):
    k = pl.program_id(1)

    @pl.when(k == 0)
    def _():
        m = mod_ref[0]
        h = _modnorm(x_ref[...], g_ref[...], m[:, 3 * D_MODEL:4 * D_MODEL], m[:, 4 * D_MODEL:5 * D_MODEL])
        h_ref[...] = h.astype(BF16)

    h = h_ref[...]
    a = jnp.dot(h, w1g_ref[...], preferred_element_type=F32)
    b = jnp.dot(h, w1u_ref[...], preferred_element_type=F32)
    part = _bdot(a * _sigmoid(a) * b, w2_ref[...])

    @pl.when(k == 0)
    def _():
        acc_ref[...] = part

    @pl.when(k == 1)
    def _():
        m = mod_ref[0]
        xn = x_ref[...] + m[:, 5 * D_MODEL:6 * D_MODEL] * (acc_ref[...] + part)
        if final:
            xn = _rms(xn, gf_ref[...])
        o_ref[...] = xn


def _ffn(x, g, mod, w_ffn_in, w_ffn_out, g_final, rows_per_mod, final):
    T = x.shape[0]
    tm = 512
    per = rows_per_mod // tm
    return pl.pallas_call(
        functools.partial(_ffn_kernel, final=final),
        grid=(T // tm, 2),
        in_specs=[
            pl.BlockSpec((tm, D_MODEL), lambda i, k: (i, 0)),
            pl.BlockSpec((1, D_MODEL), lambda i, k: (0, 0)),
            pl.BlockSpec((1, 1, 6 * D_MODEL), lambda i, k: (i // per, 0, 0)),
            pl.BlockSpec((D_MODEL, FFN_CHUNK), lambda i, k: (0, k)),
            pl.BlockSpec((D_MODEL, FFN_CHUNK), lambda i, k: (0, 2 + k)),
            pl.BlockSpec((FFN_CHUNK, D_MODEL), lambda i, k: (k, 0)),
            pl.BlockSpec((1, D_MODEL), lambda i, k: (0, 0)),
        ],
        out_specs=pl.BlockSpec((tm, D_MODEL), lambda i, k: (i, 0)),
        out_shape=jax.ShapeDtypeStruct((T, D_MODEL), F32),
        scratch_shapes=[pltpu.VMEM((tm, D_MODEL), BF16), pltpu.VMEM((tm, D_MODEL), F32)],
        compiler_params=_cparams("arbitrary", "arbitrary"),
        name="ffn",
    )(x, g, mod, w_ffn_in, w_ffn_in, w_ffn_out, g_final)


def _da_lambda(lam_ref, lam_init):
    lp = lam_ref[...]
    a = jnp.sum(lp[0:1] * lp[1:2], axis=1, keepdims=True)
    b = jnp.sum(lp[2:3] * lp[3:4], axis=1, keepdims=True)
    return jnp.exp(a) - jnp.exp(b) + lam_init


def _softmax_rows(s):
    m = jnp.max(s, axis=-1, keepdims=True)
    p = jnp.exp(s - m)
    return p, jnp.sum(p, axis=-1, keepdims=True)


ATT_ONES_ROWS = 16
LOG2E = math.log2(math.e)


def _masked_q_blocks(qt, d):
    row = lax.broadcasted_iota(jnp.int32, qt.shape, 0)
    zero = jnp.zeros_like(qt)
    return jnp.concatenate([jnp.where((row >= j * d) & (row < (j + 1) * d), qt, zero) for j in range(128 // d)], axis=1)


def _colmax(st):
    keys, n = st.shape
    return jnp.max(jnp.max(st.reshape(keys // 256, 256, n), axis=0), axis=0, keepdims=True)


def _ctx_attn_kernel(nq_ref, nk_ref, nv_ref, dq_ref, dk_ref, dv_ref, lam_ref, sub_ref, yn_ref, yd_ref, acc_ref,
                     *, lam_init):
    lam = _da_lambda(lam_ref, lam_init)
    ones = jnp.ones((ATT_ONES_ROWS, SEQ), BF16)

    def attend(q_ref, k_ref, v_ref, d, maps_per_head, finish):
        qt = (q_ref[...] * (d ** -0.5 * LOG2E)).T.astype(BF16)
        vt = v_ref[...].T.astype(BF16)
        kb = k_ref[...].astype(BF16)
        dv = NA_HEAD_DIM
        heads_per_group = 128 // (d * maps_per_head)
        w = maps_per_head * SEQ
        for g in range(BRANCH_W // 128):
            st = jnp.dot(kb[:, g * 128:(g + 1) * 128], _masked_q_blocks(qt[g * 128:(g + 1) * 128], d),
                         preferred_element_type=F32)
            pt = jnp.exp2(st - _colmax(st)).astype(BF16)
            for j in range(heads_per_group):
                h = g * heads_per_group + j
                ve = jnp.concatenate([vt[h * dv:(h + 1) * dv], ones], axis=0)
                oe = jnp.dot(ve, pt[:, j * w:(j + 1) * w], preferred_element_type=F32)
                os = [oe[0:dv, i * SEQ:(i + 1) * SEQ] / oe[dv:dv + 1, i * SEQ:(i + 1) * SEQ]
                      for i in range(maps_per_head)]
                acc_ref[h * dv:(h + 1) * dv, :] = finish(os)

    attend(nq_ref, nk_ref, nv_ref, NA_HEAD_DIM, 1, lambda os: os[0])
    yn_ref[...] = acc_ref[...].T

    def da_finish(os):
        ot = os[0] - lam * os[1]
        ot = ot * lax.rsqrt(jnp.mean(ot * ot, axis=0, keepdims=True) + EPS) * sub_ref[...]
        return ot * (1.0 - lam_init)

    attend(dq_ref, dk_ref, dv_ref, DA_HEAD_DIM, 2, da_finish)
    yd_ref[...] = acc_ref[...].T


def _ctx_attention(u, da_lambda, subln_col, lam_init):
    col = lambda j: pl.BlockSpec((SEQ, BRANCH_W), lambda b, j=j: (b, j))
    out = pl.BlockSpec((SEQ, BRANCH_W), lambda b: (b, 0))
    shape = jax.ShapeDtypeStruct((BATCH * SEQ, BRANCH_W), F32)
    return pl.pallas_call(
        functools.partial(_ctx_attn_kernel, lam_init=lam_init),
        grid=(BATCH,),
        in_specs=[col(3), col(4), col(5), col(6), col(7), col(8),
                  pl.BlockSpec((4, DA_HEAD_DIM), lambda b: (0, 0)),
                  pl.BlockSpec((DA_V_DIM, 1), lambda b: (0, 0))],
        out_specs=[out, out],
        out_shape=[shape, shape],
        scratch_shapes=[pltpu.VMEM((BRANCH_W, SEQ), F32)],
        compiler_params=_cparams("arbitrary"),
        name="ctx_attention",
    )(u, u, u, u, u, u, da_lambda, subln_col)


def _rpb_table_kernel(rpb_ref, o_ref):
    qc = lax.broadcasted_iota(jnp.int32, (GRID_W, GRID_W), 0)
    kc = lax.broadcasted_iota(jnp.int32, (GRID_W, GRID_W), 1)
    dc = jnp.clip(kc - qc, -(NA_WIN_COLS - 1), NA_WIN_COLS - 1) + (NA_WIN_COLS - 1)
    r = rpb_ref[0, 0]
    for dr in range(2 * NA_WIN_ROWS - 1):
        acc = jnp.zeros((GRID_W, GRID_W), F32)
        for d in range(2 * NA_WIN_COLS - 1):
            acc = jnp.where(dc == d, r[dr:dr + 1, d:d + 1], acc)
        o_ref[0, 0, dr] = acc


def _rpb_table(na_rpb):
    n_dr, n_dc = 2 * NA_WIN_ROWS - 1, 2 * NA_WIN_COLS - 1
    return pl.pallas_call(
        _rpb_table_kernel,
        grid=(DEPTH, NA_HEADS),
        in_specs=[pl.BlockSpec((1, 1, n_dr, n_dc), lambda l, h: (l, h, 0, 0))],
        out_specs=pl.BlockSpec((1, 1, n_dr, GRID_W, GRID_W), lambda l, h: (l, h, 0, 0, 0)),
        out_shape=jax.ShapeDtypeStruct((DEPTH, NA_HEADS, n_dr, GRID_W, GRID_W), F32),
        compiler_params=_cparams("arbitrary", "arbitrary"),
        name="rpb_table",
    )(na_rpb)


NA_ROWS_PER_STEP = 4


def _na_kernel(q_ref, k_ref, v_ref, kc_ref, vc_ref, bias_ref, o_ref):
    n_lat = NA_WIN_ROWS * GRID_W
    qc = lax.broadcasted_iota(jnp.int32, (GRID_W, n_lat), 0)
    kcol = lax.broadcasted_iota(jnp.int32, (GRID_W, n_lat), 1) % GRID_W
    c0 = jnp.clip(qc - NA_WIN_COLS // 2, 0, GRID_W - NA_WIN_COLS)
    col_ok = (kcol >= c0) & (kcol < c0 + NA_WIN_COLS)
    scale = NA_HEAD_DIM ** -0.5
    for rr in range(NA_ROWS_PER_STEP):
        r = pl.program_id(1) * NA_ROWS_PER_STEP + rr
        r0 = jnp.clip(r - NA_WIN_ROWS // 2, 0, GRID_H - NA_WIN_ROWS)
        start = pl.multiple_of(r0 * GRID_W, GRID_W)
        kwin = k_ref[pl.ds(start, n_lat), :]
        vwin = v_ref[pl.ds(start, n_lat), :]
        dr0 = r0 - r + (NA_WIN_ROWS - 1)
        rows = slice(rr * GRID_W, (rr + 1) * GRID_W)
        for h in range(NA_HEADS):
            sl = slice(h * NA_HEAD_DIM, (h + 1) * NA_HEAD_DIM)
            q = q_ref[rows, sl]
            bias = jnp.concatenate([bias_ref[h, dr0 + w] for w in range(NA_WIN_ROWS)], axis=1)
            s_lat = _bdot_nt(q, kwin[:, sl]) * scale + bias
            s_lat = jnp.where(col_ok, s_lat, NEG_INF)
            s_ctx = _bdot_nt(q, kc_ref[0, :, sl]) * scale
            m = jnp.maximum(jnp.max(s_lat, axis=-1, keepdims=True), jnp.max(s_ctx, axis=-1, keepdims=True))
            p_lat = jnp.exp(s_lat - m)
            p_ctx = jnp.exp(s_ctx - m)
            l = jnp.sum(p_lat, axis=-1, keepdims=True) + jnp.sum(p_ctx, axis=-1, keepdims=True)
            o = _bdot(p_lat, vwin[:, sl]) + _bdot(p_ctx, vc_ref[0, :, sl])
            o_ref[rows, sl] = o / l


def _nbr_attention(u, k_ctx, v_ctx, bias):
    n_dr = 2 * NA_WIN_ROWS - 1
    steps = GRID_H // NA_ROWS_PER_STEP
    rows = NA_ROWS_PER_STEP * GRID_W
    return pl.pallas_call(
        _na_kernel,
        grid=(DEC_BATCH, steps),
        in_specs=[
            pl.BlockSpec((rows, BRANCH_W), lambda b, r: (b * steps + r, 3)),
            pl.BlockSpec((DEC_SEQ, BRANCH_W), lambda b, r: (b, 4)),
            pl.BlockSpec((DEC_SEQ, BRANCH_W), lambda b, r: (b, 5)),
            pl.BlockSpec((1, PAST_LEN, BRANCH_W), lambda b, r: (b, 0, 0)),
            pl.BlockSpec((1, PAST_LEN, BRANCH_W), lambda b, r: (b, 0, 0)),
            pl.BlockSpec((NA_HEADS, n_dr, GRID_W, GRID_W), lambda b, r: (0, 0, 0, 0)),
        ],
        out_specs=pl.BlockSpec((rows, BRANCH_W), lambda b, r: (b * steps + r, 0)),
        out_shape=jax.ShapeDtypeStruct((DEC_BATCH * DEC_SEQ, BRANCH_W), F32),
        compiler_params=_cparams("arbitrary", "arbitrary"),
        name="nbr_attention",
    )(u, u, u, k_ctx, v_ctx, bias)


DA_TQ = 256
DA_KEYS = DEC_SEQ + PAST_LEN


def _rope(x, cos, sin_signed):
    n = x.shape[-1]
    lane = lax.broadcasted_iota(jnp.int32, x.shape, 1)
    partner = jnp.where(lane % 2 == 0, pltpu.roll(x, n - 1, axis=1), pltpu.roll(x, 1, axis=1))
    return x * cos + partner * sin_signed


DA_ONES_ROWS = ATT_ONES_ROWS
DA_MAPS_PER_TILE = 128 // DA_HEAD_DIM


def _da_prep_kernel(q_ref, k_ref, v_ref, kc_ref, vc_ref, cos_ref, sin_ref, qt_ref, ko_ref, vt_ref):
    t = pl.program_id(1)

    def put_v(v):
        vt = v.astype(F32).T.astype(BF16)
        ones = jnp.ones((DA_ONES_ROWS, DA_TQ), BF16)
        for h in range(DA_HEADS):
            vt_ref[0, h, 0:DA_V_DIM, :] = vt[h * DA_V_DIM:(h + 1) * DA_V_DIM]
            vt_ref[0, h, DA_V_DIM:DA_V_DIM + DA_ONES_ROWS, :] = ones

    @pl.when(t < DEC_SEQ // DA_TQ)
    def _():
        cos = cos_ref[...]
        sin = sin_ref[...]
        q = _rope(q_ref[...].astype(F32), cos, sin) * (DA_HEAD_DIM ** -0.5 * LOG2E)
        qt_ref[0] = q.T.astype(BF16)
        ko_ref[0] = _rope(k_ref[...].astype(F32), cos, sin).astype(BF16)
        put_v(v_ref[...])

    @pl.when(t == DEC_SEQ // DA_TQ)
    def _():
        ko_ref[0] = kc_ref[0].astype(BF16)
        put_v(vc_ref[0])


def _da_prep(u, k_ctx, v_ctx, cos, sin):
    nt = DEC_SEQ // DA_TQ
    last = nt - 1
    rowblk = lambda j: pl.BlockSpec((DA_TQ, BRANCH_W), lambda b, t, j=j: (b * nt + jnp.minimum(t, last), j))
    tab = pl.BlockSpec((DA_TQ, BRANCH_W), lambda b, t: (jnp.minimum(t, last), 0))
    ctx = pl.BlockSpec((1, PAST_LEN, BRANCH_W), lambda b, t: (b, 0, 0))
    vrows = DA_V_DIM + DA_ONES_ROWS
    return pl.pallas_call(
        _da_prep_kernel,
        grid=(DEC_BATCH, nt + 1),
        in_specs=[rowblk(6), rowblk(7), rowblk(8), ctx, ctx, tab, tab],
        out_specs=[
            pl.BlockSpec((1, BRANCH_W, DA_TQ), lambda b, t: (b, 0, jnp.minimum(t, last))),
            pl.BlockSpec((1, DA_TQ, BRANCH_W), lambda b, t: (b, t, 0)),
            pl.BlockSpec((1, DA_HEADS, vrows, DA_TQ), lambda b, t: (b, 0, 0, t)),
        ],
        out_shape=[
            jax.ShapeDtypeStruct((DEC_BATCH, BRANCH_W, DEC_SEQ), BF16),
            jax.ShapeDtypeStruct((DEC_BATCH, DA_KEYS, BRANCH_W), BF16),
            jax.ShapeDtypeStruct((DEC_BATCH, DA_HEADS, vrows, DA_KEYS), BF16),
        ],
        compiler_params=_cparams("arbitrary", "arbitrary"),
        name="da_prep",
    )(u, u, u, k_ctx, v_ctx, cos, sin)


def _da_kernel(qt_ref, k_ref, vt_ref, lam_ref, sub_ref, o_ref, acc_ref, *, lam_init):
    lam = _da_lambda(lam_ref, lam_init)
    st = jnp.dot(k_ref[0], _masked_q_blocks(qt_ref[0], DA_HEAD_DIM), preferred_element_type=F32)
    pt = jnp.exp2(st - _colmax(st)).astype(BF16)
    heads = DA_MAPS_PER_TILE // 2
    for h in range(heads):
        oe = jnp.dot(vt_ref[0, h], pt[:, 2 * h * DA_TQ:(2 * h + 2) * DA_TQ], preferred_element_type=F32)
        os = [oe[0:DA_V_DIM, i * DA_TQ:(i + 1) * DA_TQ] / oe[DA_V_DIM:DA_V_DIM + 1, i * DA_TQ:(i + 1) * DA_TQ]
              for i in range(2)]
        ot = os[0] - lam * os[1]
        ot = ot * lax.rsqrt(jnp.mean(ot * ot, axis=0, keepdims=True) + EPS) * sub_ref[...]
        acc_ref[h * DA_V_DIM:(h + 1) * DA_V_DIM, :] = ot * (1.0 - lam_init)
    o_ref[...] = acc_ref[...].T


def _diff_attention(qt, k, vt, da_lambda, subln_col, lam_init):
    nt = DEC_SEQ // DA_TQ
    vrows = DA_V_DIM + DA_ONES_ROWS
    groups = BRANCH_W // 128
    heads = DA_MAPS_PER_TILE // 2
    return pl.pallas_call(
        functools.partial(_da_kernel, lam_init=lam_init),
        grid=(DEC_BATCH, groups, nt),
        in_specs=[
            pl.BlockSpec((1, 128, DA_TQ), lambda b, g, t: (b, g, t)),
            pl.BlockSpec((1, DA_KEYS, 128), lambda b, g, t: (b, 0, g)),
            pl.BlockSpec((1, heads, vrows, DA_KEYS), lambda b, g, t: (b, g, 0, 0)),
            pl.BlockSpec((4, DA_HEAD_DIM), lambda b, g, t: (0, 0)),
            pl.BlockSpec((DA_V_DIM, 1), lambda b, g, t: (0, 0)),
        ],
        out_specs=pl.BlockSpec((DA_TQ, 128), lambda b, g, t: (b * nt + t, g)),
        out_shape=jax.ShapeDtypeStruct((DEC_BATCH * DEC_SEQ, BRANCH_W), F32),
        scratch_shapes=[pltpu.VMEM((128, DA_TQ), F32)],
        compiler_params=_cparams("arbitrary", "arbitrary", "arbitrary"),
        name="diff_attention",
    )(qt, k, vt, da_lambda, subln_col)


def _rope_tables():
    pos = np.arange(DEC_SEQ)
    row = (pos // GRID_W).astype(np.float32)
    col = (pos % GRID_W).astype(np.float32)
    n_freq = DA_HEAD_DIM // 4
    inv = (np.float32(ROPE_BASE) ** (-np.arange(n_freq, dtype=np.float32) / n_freq)).astype(np.float32)
    ang = np.concatenate([row[:, None] * inv[None, :], col[:, None] * inv[None, :]], axis=-1)
    ang = ang.astype(np.float64)
    cos = np.repeat(np.cos(ang), 2, axis=-1)
    sin = np.repeat(np.sin(ang), 2, axis=-1)
    sign = np.where(np.arange(DA_HEAD_DIM) % 2 == 0, -1.0, 1.0)
    reps = BRANCH_W // DA_HEAD_DIM
    cos = np.tile(cos, (1, reps)).astype(np.float32)
    sin = np.tile(sin * sign[None, :], (1, reps)).astype(np.float32)
    return jnp.asarray(cos), jnp.asarray(sin)


def _filt_hidden_kernel(feat_ref, w1_ref, b1_ref, w2_ref, b2_ref, fr_ref, o_ref):
    fr = fr_ref[0]
    h = jnp.sin(fr * (jnp.dot(feat_ref[...], w1_ref[0], precision=HIGHEST, preferred_element_type=F32) + b1_ref[0]))
    o_ref[0] = jnp.sin(fr * (jnp.dot(h, w2_ref[0], precision=HIGHEST, preferred_element_type=F32) + b2_ref[0]))


def _filt_kernel(h_ref, w3f_ref, w3b_ref, dec_ref, hf_ref, hb_ref):
    h = h_ref[0]
    dec = dec_ref[...]
    hf = jnp.dot(h, w3f_ref[0], precision=HIGHEST, preferred_element_type=F32) * dec
    hb = jnp.dot(h, w3b_ref[0], precision=HIGHEST, preferred_element_type=F32) * dec
    row = lax.broadcasted_iota(jnp.int32, hb.shape, 0)
    hb = jnp.where(row == 0, 0.0, hb)
    nrm = jnp.sum(jnp.abs(hf), axis=0, keepdims=True) + jnp.sum(jnp.abs(hb), axis=0, keepdims=True)
    hf_ref[0, 0] = hf / nrm
    hb_ref[0, 0] = hb / nrm


def _hyena_pos_tables(L):
    f32 = np.float32
    pos = np.arange(L, dtype=f32)
    t = (pos / f32(L)).astype(f32)
    bands = np.linspace(1e-4, HY_POS_BANDS - 1, HY_POS_BANDS, dtype=f32)
    ang = (f32(2 * math.pi / L) * pos[:, None] * bands[None, :]).astype(np.float64)
    feats = np.zeros((L, HY_FILT_HIDDEN), f32)
    feats[:, 0] = t
    feats[:, 1:1 + HY_POS_BANDS] = np.cos(ang)
    feats[:, 1 + HY_POS_BANDS:HY_POS_DIM] = -np.sin(ang)
    deltas = np.linspace(math.log(HY_DECAY_TARGET) / HY_SLOW_DECAY,
                         math.log(HY_DECAY_TARGET) / HY_FAST_DECAY, BRANCH_W, dtype=f32)
    decay = np.exp((-t[:, None] * np.abs(deltas)[None, :]).astype(np.float64)).astype(f32)
    return jnp.asarray(feats), jnp.asarray(decay)


def _hyena_filters(L, w1p, b1, w2, b2, w3, freq):
    feats, decay = _hyena_pos_tables(L)
    cb = 128
    ncb = BRANCH_W // cb
    small = lambda shape: pl.BlockSpec((1,) + shape, lambda l: (l, 0, 0))
    hidden = pl.pallas_call(
        _filt_hidden_kernel,
        grid=(DEPTH,),
        in_specs=[
            pl.BlockSpec((L, HY_FILT_HIDDEN), lambda l: (0, 0)),
            small((HY_FILT_HIDDEN, HY_FILT_HIDDEN)), small((1, HY_FILT_HIDDEN)),
            small((HY_FILT_HIDDEN, HY_FILT_HIDDEN)), small((1, HY_FILT_HIDDEN)),
            small((1, HY_FILT_HIDDEN)),
        ],
        out_specs=pl.BlockSpec((1, L, HY_FILT_HIDDEN), lambda l: (l, 0, 0)),
        out_shape=jax.ShapeDtypeStruct((DEPTH, L, HY_FILT_HIDDEN), F32),
        compiler_params=_cparams("arbitrary"),
        name=f"hyena_filter_hidden_{L}",
    )(feats, w1p, b1, w2, b2, freq)
    shape = jax.ShapeDtypeStruct((DEPTH, 2, L, BRANCH_W), F32)
    out = pl.BlockSpec((1, 1, L, cb), lambda l, o, c: (l, o, 0, c))
    return pl.pallas_call(
        _filt_kernel,
        grid=(DEPTH, 2, ncb),
        in_specs=[
            pl.BlockSpec((1, L, HY_FILT_HIDDEN), lambda l, o, c: (l, 0, 0)),
            pl.BlockSpec((1, HY_FILT_HIDDEN, cb), lambda l, o, c: (l, 0, o * 2 * ncb + c)),
            pl.BlockSpec((1, HY_FILT_HIDDEN, cb), lambda l, o, c: (l, 0, o * 2 * ncb + ncb + c)),
            pl.BlockSpec((L, cb), lambda l, o, c: (0, c)),
        ],
        out_specs=[out, out],
        out_shape=[shape, shape],
        compiler_params=_cparams("arbitrary", "arbitrary", "arbitrary"),
        name=f"hyena_filters_{L}",
    )(hidden, w3, w3, decay)


def _short_conv(u, w_ref, b_ref, seq_len):
    n = u.shape[0]
    t = lax.broadcasted_iota(jnp.int32, u.shape, 0) % seq_len
    prev = jnp.where(t == 0, 0.0, pltpu.roll(u, 1, axis=0))
    nxt = jnp.where(t == seq_len - 1, 0.0, pltpu.roll(u, n - 1, axis=0))
    return prev * w_ref[0:1, :] + u * w_ref[1:2, :] + nxt * w_ref[2:3, :] + b_ref[...]


def _dft_direct_mats():
    n, half = 2 * SEQ, SEQ
    k = np.arange(n)[:, None].astype(np.float64)
    t = np.arange(half)[None, :].astype(np.float64)
    ang = 2 * np.pi * k * t / n
    fr, fi = np.cos(ang), -np.sin(ang)
    mf = np.block([[fr, -fi], [fi, fr]])
    gr, gi = np.cos(ang).T / n, np.sin(ang).T / n
    mi = np.block([[gr, -gi], [gi, gr]])
    return mf.astype(np.float32), mi.astype(np.float32)


def _spec_direct_kernel(hf_ref, hb_ref, m_ref, o_ref):
    m = m_ref[...]
    wf = jnp.dot(m, hf_ref[0, 0], precision=HIGHEST, preferred_element_type=F32)
    wb = jnp.dot(m, hb_ref[0, 0], precision=HIGHEST, preferred_element_type=F32)
    n = 2 * SEQ
    o_ref[0, 0, 0:n] = wf[0:n] + wb[0:n]
    o_ref[0, 0, n:2 * n] = wf[n:2 * n] - wb[n:2 * n]


def _spec_direct(hf, hb, mf_real):
    n = 2 * SEQ
    blk = pl.BlockSpec((1, 1, SEQ, BRANCH_W), lambda l, o: (l, o, 0, 0))
    return pl.pallas_call(
        _spec_direct_kernel,
        grid=(DEPTH, 2),
        in_specs=[blk, blk, pl.BlockSpec((2 * n, SEQ), lambda l, o: (0, 0))],
        out_specs=pl.BlockSpec((1, 1, 2 * n, BRANCH_W), lambda l, o: (l, o, 0, 0)),
        out_shape=jax.ShapeDtypeStruct((DEPTH, 2, 2 * n, BRANCH_W), F32),
        compiler_params=_cparams("arbitrary", "arbitrary"),
        name="hyena_spectrum_direct",
    )(hf, hb, mf_real)


def _lconv_direct_kernel(s_ref, g_ref, cws_ref, cbs_ref, cwg_ref, cbg_ref, h_ref, bias_ref, mf_ref, mi_ref, o_ref,
                         *, conv_sig):
    n = 2 * SEQ
    sig = s_ref[...]
    if conv_sig:
        sig = _short_conv(sig, cws_ref, cbs_ref, SEQ)
    gate = _short_conv(g_ref[...], cwg_ref, cbg_ref, SEQ)
    z = jnp.dot(mf_ref[...], sig.astype(BF16), preferred_element_type=F32)
    zr, zi = z[0:n], z[n:2 * n]
    hr, hi = h_ref[0:n], h_ref[n:2 * n]
    y = jnp.concatenate([zr * hr - zi * hi, zr * hi + zi * hr], axis=0)
    y = jnp.dot(mi_ref[...], y.astype(BF16), preferred_element_type=F32)
    o_ref[...] = gate * (y + sig * bias_ref[...])


def _lconv_direct(sig, sig_col, gate_src, gate_col, conv_w, conv_b, spec, bias, mf, mi, conv_sig):
    n = 2 * SEQ
    rows = 2 * SEQ
    T = sig.shape[0]
    return pl.pallas_call(
        functools.partial(_lconv_direct_kernel, conv_sig=conv_sig),
        grid=(T // rows,),
        in_specs=[
            pl.BlockSpec((rows, BRANCH_W), lambda p: (p, sig_col)),
            pl.BlockSpec((rows, BRANCH_W), lambda p: (p, gate_col)),
            pl.BlockSpec((3, BRANCH_W), lambda p: (0, 0)),
            pl.BlockSpec((1, BRANCH_W), lambda p: (0, 0)),
            pl.BlockSpec((3, BRANCH_W), lambda p: (0, gate_col)),
            pl.BlockSpec((1, BRANCH_W), lambda p: (0, gate_col)),
            pl.BlockSpec((2 * n, BRANCH_W), lambda p: (0, 0)),
            pl.BlockSpec((1, BRANCH_W), lambda p: (0, 0)),
            pl.BlockSpec((2 * n, rows), lambda p: (0, 0)),
            pl.BlockSpec((rows, 2 * n), lambda p: (0, 0)),
        ],
        out_specs=pl.BlockSpec((rows, BRANCH_W), lambda p: (p, 0)),
        out_shape=jax.ShapeDtypeStruct((T, BRANCH_W), F32),
        compiler_params=_cparams("arbitrary"),
        name="hyena_lconv_direct",
    )(sig, gate_src, conv_w, conv_b, conv_w, conv_b, spec, bias, mf, mi)


def _dft_two_stage_mats():
    no, ni, half, n = FFT_NO, FFT_NI, FFT_HALF, FFT_N
    f64 = np.float64
    k1 = np.arange(no, dtype=f64)
    n_o = np.arange(half, dtype=f64)
    n_i = np.arange(ni, dtype=f64)
    ang = 2 * np.pi * (n_i[:, None, None] * k1[None, :, None] / n + k1[None, :, None] * n_o[None, None, :] / no)
    tr, ti = np.cos(ang), -np.sin(ang)
    m1 = np.concatenate([np.concatenate([tr, -ti], axis=2), np.concatenate([ti, tr], axis=2)], axis=1)
    k2 = np.arange(ni, dtype=f64)
    ang2 = 2 * np.pi * k2[:, None] * n_i[None, :] / ni
    f2r, f2i = np.cos(ang2), -np.sin(ang2)
    m2 = np.block([[f2r, -f2i], [f2i, f2r]])
    m2c = np.block([[f2r, f2i], [-f2i, f2r]])
    sr, si = np.transpose(tr, (0, 2, 1)) / n, -np.transpose(ti, (0, 2, 1)) / n
    m3 = np.concatenate([np.concatenate([sr, -si], axis=2), np.concatenate([si, sr], axis=2)], axis=1)
    return (m1.astype(np.float32), m2.astype(np.float32), m2c.astype(np.float32), m3.astype(np.float32))


def _fwd_stage1(za_ref, zb_ref, m1_ref, w_ref):
    def body(ni, carry):
        a = za_ref[pl.ds(ni, FFT_HALF, stride=FFT_NI), :]
        if zb_ref is None:
            out = jnp.dot(m1_ref[ni][:, 0:FFT_HALF], a.astype(BF16), preferred_element_type=F32)
        else:
            b = zb_ref[pl.ds(ni, FFT_HALF, stride=FFT_NI), :]
            out = jnp.dot(m1_ref[ni], jnp.concatenate([a, b], axis=0).astype(BF16), preferred_element_type=F32)
        w_ref[pl.ds(ni, FFT_NO, stride=2 * FFT_NI), :] = out[0:FFT_NO]
        w_ref[pl.ds(FFT_NI + ni, FFT_NO, stride=2 * FFT_NI), :] = out[FFT_NO:2 * FFT_NO]
        return carry

    lax.fori_loop(0, FFT_NI, body, 0, unroll=FFT_UNROLL)


def _spec_two_stage_kernel(hf_ref, hb_ref, m1_ref, m2_ref, o_ref, wf_ref, wb_ref):
    _fwd_stage1(hf_ref.at[0, 0], None, m1_ref, wf_ref)
    _fwd_stage1(hb_ref.at[0, 0], None, m1_ref, wb_ref)
    blk = 2 * FFT_NI

    def body(k1, carry):
        rows = pl.ds(pl.multiple_of(k1 * blk, blk), blk)
        xf = jnp.dot(m2_ref[...], wf_ref[rows, :].astype(BF16), preferred_element_type=F32)
        xb = jnp.dot(m2_ref[...], wb_ref[rows, :].astype(BF16), preferred_element_type=F32)
        o_ref[0, 0, rows, :] = jnp.concatenate(
            [xf[0:FFT_NI] + xb[0:FFT_NI], xf[FFT_NI:blk] - xb[FFT_NI:blk]], axis=0)
        return carry

    lax.fori_loop(0, FFT_NO, body, 0, unroll=FFT_UNROLL)


def _spec_two_stage(hf, hb, m1, m2):
    cb = LCONV_CB
    blk = pl.BlockSpec((1, 1, DEC_SEQ, cb), lambda l, o, c: (l, o, 0, c))
    return pl.pallas_call(
        _spec_two_stage_kernel,
        grid=(DEPTH, 2, BRANCH_W // cb),
        in_specs=[blk, blk,
                  pl.BlockSpec((FFT_NI, 2 * FFT_NO, 2 * FFT_HALF), lambda l, o, c: (0, 0, 0)),
                  pl.BlockSpec((2 * FFT_NI, 2 * FFT_NI), lambda l, o, c: (0, 0))],
        out_specs=pl.BlockSpec((1, 1, 2 * FFT_N, cb), lambda l, o, c: (l, o, 0, c)),
        out_shape=jax.ShapeDtypeStruct((DEPTH, 2, 2 * FFT_N, BRANCH_W), F32),
        scratch_shapes=[pltpu.VMEM((2 * FFT_N, cb), F32), pltpu.VMEM((2 * FFT_N, cb), F32)],
        compiler_params=_cparams("arbitrary", "arbitrary", "arbitrary"),
        name="hyena_spectrum_two_stage",
    )(hf, hb, m1, m2)


def _lconv_two_stage_kernel(s_ref, g_ref, cws_ref, cbs_ref, cwg_ref, cbg_ref, h_ref, bias_ref,
                            m1_ref, m2_ref, m2c_ref, m3_ref, o_ref, z_ref, w_ref, *, conv_sig):
    for b in range(2):
        sig = s_ref[b].astype(F32)
        if conv_sig:
            sig = _short_conv(sig, cws_ref, cbs_ref, DEC_SEQ)
        z_ref[b] = sig
    _fwd_stage1(z_ref.at[0], z_ref.at[1], m1_ref, w_ref)
    blk = 2 * FFT_NI

    def mid(k1, carry):
        rows = pl.ds(pl.multiple_of(k1 * blk, blk), blk)
        x = jnp.dot(m2_ref[...], w_ref[rows, :].astype(BF16), preferred_element_type=F32)
        h = h_ref[rows, :]
        xr, xi = x[0:FFT_NI], x[FFT_NI:blk]
        hr, hi = h[0:FFT_NI], h[FFT_NI:blk]
        y = jnp.concatenate([xr * hr - xi * hi, xr * hi + xi * hr], axis=0)
        w_ref[rows, :] = jnp.dot(m2c_ref[...], y.astype(BF16), preferred_element_type=F32)
        return carry

    lax.fori_loop(0, FFT_NO, mid, 0, unroll=FFT_UNROLL)

    def last(ni, carry):
        cr = w_ref[pl.ds(ni, FFT_NO, stride=blk), :]
        ci = w_ref[pl.ds(FFT_NI + ni, FFT_NO, stride=blk), :]
        y = jnp.dot(m3_ref[ni], jnp.concatenate([cr, ci], axis=0).astype(BF16), preferred_element_type=F32)
        o_ref[0, pl.ds(ni, FFT_HALF, stride=FFT_NI), :] = y[0:FFT_HALF]
        o_ref[1, pl.ds(ni, FFT_HALF, stride=FFT_NI), :] = y[FFT_HALF:2 * FFT_HALF]
        return carry

    lax.fori_loop(0, FFT_NI, last, 0, unroll=FFT_UNROLL)
    for b in range(2):
        gate = _short_conv(g_ref[b].astype(F32), cwg_ref, cbg_ref, DEC_SEQ)
        sig = z_ref[b]
        o_ref[b] = gate * (o_ref[b] + sig * bias_ref[...])


def _lconv_two_stage(sig, sig_col, gate_src, gate_col, conv_w, conv_b, spec, bias, mats, conv_sig):
    cb = LCONV_CB
    ncb = BRANCH_W // cb
    m1, m2, m2c, m3 = mats
    const3 = lambda c, p: (0, 0, 0)
    const2 = lambda c, p: (0, 0)
    return pl.pallas_call(
        functools.partial(_lconv_two_stage_kernel, conv_sig=conv_sig),
        grid=(ncb, DEC_BATCH // 2),
        in_specs=[
            pl.BlockSpec((2, DEC_SEQ, cb), lambda c, p: (p, 0, sig_col * ncb + c)),
            pl.BlockSpec((2, DEC_SEQ, cb), lambda c, p: (p, 0, gate_col * ncb + c)),
            pl.BlockSpec((3, cb), lambda c, p: (0, c)),
            pl.BlockSpec((1, cb), lambda c, p: (0, c)),
            pl.BlockSpec((3, cb), lambda c, p: (0, gate_col * ncb + c)),
            pl.BlockSpec((1, cb), lambda c, p: (0, gate_col * ncb + c)),
            pl.BlockSpec((2 * FFT_N, cb), lambda c, p: (0, c)),
            pl.BlockSpec((1, cb), lambda c, p: (0, c)),
            pl.BlockSpec(m1.shape, const3),
            pl.BlockSpec(m2.shape, const2),
            pl.BlockSpec(m2c.shape, const2),
            pl.BlockSpec(m3.shape, const3),
        ],
        out_specs=pl.BlockSpec((2, DEC_SEQ, cb), lambda c, p: (p, 0, c)),
        out_shape=jax.ShapeDtypeStruct((DEC_BATCH, DEC_SEQ, BRANCH_W), F32),
        scratch_shapes=[pltpu.VMEM((2, DEC_SEQ, cb), F32), pltpu.VMEM((2 * FFT_N, cb), F32)],
        compiler_params=_cparams("arbitrary", "arbitrary"),
        name="hyena_lconv_two_stage",
    )(sig, gate_src, conv_w, conv_b, conv_w, conv_b, spec, bias, m1, m2, m2c, m3)


def kernel(x_prompt, x_sample, cache_na_k, cache_na_v, cache_da_k, cache_da_v, c, c_ctx, w_ada, b_ada, norm_mix,
           norm_ffn, w_in, hy_conv_w, hy_conv_b, hy_filt_w1, hy_filt_b1, hy_filt_w2, hy_filt_b2, hy_filt_w3,
           hy_filt_freq, hy_bias, na_rpb, da_lambda, da_subln, w_lift, w_out, w_ffn_in, w_ffn_out, norm_final):
    TP, TS = BATCH * SEQ, DEC_BATCH * DEC_SEQ
    xp = x_prompt.reshape(TP, D_MODEL)
    xs = x_sample.reshape(TS, D_MODEL)

    cc = jnp.concatenate([c_ctx[None, :], c, jnp.zeros((8 - 1 - DEC_BATCH, D_MODEL), F32)], axis=0)
    mod = _modulation(cc, w_ada, b_ada)
    mod_p = mod[:, 0:1].reshape(DEPTH, 1, 1, 6 * D_MODEL)
    mod_s = mod[:, 1:1 + DEC_BATCH].reshape(DEPTH, DEC_BATCH, 1, 6 * D_MODEL)

    w_mix = w_in[:, :, :MIX_W].astype(BF16)
    w_gate = w_in[:, :, MIX_W:].astype(BF16)
    w_lift_b = w_lift.astype(BF16)
    w_out_b = w_out.astype(BF16)
    w_ffn_in_b = w_ffn_in.astype(BF16)
    w_ffn_out_b = w_ffn_out.astype(BF16)
    g_mix = norm_mix.reshape(DEPTH, 1, D_MODEL)
    g_ffn = norm_ffn.reshape(DEPTH, 1, D_MODEL)
    g_fin = norm_final.reshape(1, D_MODEL)
    subln = da_subln.reshape(DEPTH, 1, DA_V_DIM)
    subln_col = da_subln.reshape(DEPTH, DA_V_DIM, 1)

    w1p = jnp.pad(hy_filt_w1, ((0, 0), (0, HY_FILT_HIDDEN - HY_POS_DIM), (0, 0)))
    b1 = hy_filt_b1.reshape(DEPTH, 1, HY_FILT_HIDDEN)
    b2 = hy_filt_b2.reshape(DEPTH, 1, HY_FILT_HIDDEN)
    fr = hy_filt_freq.reshape(DEPTH, 1, HY_FILT_HIDDEN)
    mf, mi = _dft_direct_mats()
    mats = _dft_two_stage_mats()
    hf_p, hb_p = _hyena_filters(SEQ, w1p, b1, hy_filt_w2, b2, hy_filt_w3, fr)
    hf_s, hb_s = _hyena_filters(DEC_SEQ, w1p, b1, hy_filt_w2, b2, hy_filt_w3, fr)
    spec_p = _spec_direct(hf_p, hb_p, jnp.asarray(mf[:, 0:SEQ]))
    mf_b, mi_b = jnp.asarray(mf, dtype=BF16), jnp.asarray(mi, dtype=BF16)
    mats_b = tuple(jnp.asarray(m, dtype=BF16) for m in mats)
    spec_s = _spec_two_stage(hf_s, hb_s, mats_b[0], mats_b[1])
    conv_b = hy_conv_b.reshape(DEPTH, 1, 3 * BRANCH_W)

    rpb = _rpb_table(na_rpb)
    cos, sin = _rope_tables()
    ck_na = cache_na_k.reshape(DEC_BATCH, DEPTH, PAST_LEN, BRANCH_W)
    cv_na = cache_na_v.reshape(DEC_BATCH, DEPTH, PAST_LEN, BRANCH_W)
    ck_da = cache_da_k.reshape(DEC_BATCH, DEPTH, PAST_LEN, BRANCH_W)
    cv_da = cache_da_v.reshape(DEC_BATCH, DEPTH, PAST_LEN, BRANCH_W)

    caches = None
    for l in range(DEPTH):
        lam_init = 0.8 - 0.6 * math.exp(-0.3 * l)
        final = l == DEPTH - 1

        u, caches = _in_proj(xp, g_mix[l], mod_p[l], w_mix[l], TP, F32, layer=l, caches=caches)
        z1 = _lconv_direct(u, 0, u, 1, hy_conv_w[l], conv_b[l], spec_p[l, 0], hy_bias[l, 0:1], mf_b, mi_b, True)
        y_hy = _lconv_direct(z1, 0, u, 2, hy_conv_w[l], conv_b[l], spec_p[l, 1], hy_bias[l, 1:2], mf_b, mi_b, False)
        y_na, y_da = _ctx_attention(u, da_lambda[l], subln_col[l], lam_init)
        xp = _merge_out(xp, g_mix[l], mod_p[l], y_hy, y_na, y_da, w_gate[l], w_lift_b[l], w_out_b[l], TP)
        xp = _ffn(xp, g_ffn[l], mod_p[l], w_ffn_in_b[l], w_ffn_out_b[l], g_fin, TP, final)

        u = _in_proj(xs, g_mix[l], mod_s[l], w_mix[l], DEC_SEQ, BF16)
        u3 = u.reshape(DEC_BATCH, DEC_SEQ, MIX_W)
        z1 = _lconv_two_stage(u3, 0, u3, 1, hy_conv_w[l], conv_b[l], spec_s[l, 0], hy_bias[l, 0:1], mats_b, True)
        y_hy = _lconv_two_stage(z1, 0, u3, 2, hy_conv_w[l], conv_b[l], spec_s[l, 1], hy_bias[l, 1:2], mats_b, False)
        y_hy = y_hy.reshape(TS, BRANCH_W)
        y_na = _nbr_attention(u, ck_na[:, l], cv_na[:, l], rpb[l])
        q, kt, v = _da_prep(u, ck_da[:, l], cv_da[:, l], cos, sin)
        y_da = _diff_attention(q, kt, v, da_lambda[l], subln_col[l], lam_init)
        xs = _merge_out(xs, g_mix[l], mod_s[l], y_hy, y_na, y_da, w_gate[l], w_lift_b[l], w_out_b[l], DEC_SEQ)
        xs = _ffn(xs, g_ffn[l], mod_s[l], w_ffn_in_b[l], w_ffn_out_b[l], g_fin, DEC_SEQ, final)

    y_prompt = xp.reshape(BATCH, SEQ, D_MODEL)
    y_sample = xs.reshape(DEC_BATCH, DEC_SEQ, D_MODEL)
    heads = lambda a, d: a.reshape(BATCH, DEPTH, SEQ, BRANCH_W // d, d)
    return (y_prompt, y_sample, heads(caches[0], NA_HEAD_DIM), heads(caches[1], NA_HEAD_DIM),
            heads(caches[2], 2 * DA_HEAD_DIM), heads(caches[3], DA_V_DIM))
```

```python
import functools
import math

import numpy as np
import jax
import jax.numpy as jnp
from jax import lax
from jax.experimental import pallas as pl
from jax.experimental.pallas import tpu as pltpu

F32 = jnp.float32
BF16 = jnp.bfloat16
HIGHEST = lax.Precision.HIGHEST

D_MODEL = 1024
BATCH = 32
SEQ = 256
DEPTH = 4
DEC_BATCH = 4
DEC_SEQ = 4096
PAST_LEN = 256
GRID_W = 64
GRID_H = DEC_SEQ // GRID_W
BRANCH_W = 512
HY_POS_BANDS = 16
HY_POS_DIM = 1 + 2 * HY_POS_BANDS
HY_FILT_HIDDEN = 64
HY_DECAY_TARGET = 1e-2
HY_FAST_DECAY = 0.3
HY_SLOW_DECAY = 1.5
NA_HEADS = 8
NA_HEAD_DIM = 64
NA_WIN_ROWS = 8
NA_WIN_COLS = 16
DA_HEADS = 8
DA_HEAD_DIM = 32
DA_V_DIM = 64
D_FF = 2816
MIX_W = 9 * BRANCH_W
ROPE_BASE = 10000.0
EPS = 1e-6
NEG_INF = -1e30

VMEM_LIMIT_BYTES = 56 * 1024 * 1024

FFT_N = 2 * DEC_SEQ
FFT_NO = 64
FFT_NI = 128
FFT_HALF = FFT_NO // 2
FFT_UNROLL = 4
LCONV_CB = 128


def _cparams(*sem):
    return pltpu.CompilerParams(dimension_semantics=sem, vmem_limit_bytes=VMEM_LIMIT_BYTES)


def _sigmoid(x):
    return 1.0 / (1.0 + jnp.exp(-x))


def _rms(x, g):
    return x * lax.rsqrt(jnp.mean(x * x, axis=-1, keepdims=True) + EPS) * g


def _modnorm(x, g, shift, scale):
    return _rms(x, g) * (1.0 + scale) + shift


def _bdot(a, b):
    return jnp.dot(a.astype(BF16), b.astype(BF16), preferred_element_type=F32)


def _bdot_nt(a, b):
    return lax.dot_general(a.astype(BF16), b.astype(BF16), (((1,), (1,)), ((), ())),
                           preferred_element_type=F32)


def _mod_kernel(c_ref, w_ref, b_ref, o_ref):
    c = c_ref[...]
    s = c * _sigmoid(c)
    o_ref[0] = jnp.dot(s, w_ref[0], precision=HIGHEST, preferred_element_type=F32) + b_ref[0]


def _modulation(cc, w_ada, b_ada):
    nt = 6
    return pl.pallas_call(
        _mod_kernel,
        grid=(DEPTH, nt),
        in_specs=[
            pl.BlockSpec((8, D_MODEL), lambda l, j: (0, 0)),
            pl.BlockSpec((1, D_MODEL, D_MODEL), lambda l, j: (l, 0, j)),
            pl.BlockSpec((1, 1, D_MODEL), lambda l, j: (l, 0, j)),
        ],
        out_specs=pl.BlockSpec((1, 8, D_MODEL), lambda l, j: (l, 0, j)),
        out_shape=jax.ShapeDtypeStruct((DEPTH, 8, 6 * D_MODEL), F32),
        compiler_params=_cparams("arbitrary", "arbitrary"),
        name="modulation",
    )(cc, w_ada, b_ada.reshape(DEPTH, 1, 6 * D_MODEL))


IN_TM = 1024
IN_TN = 3 * BRANCH_W
CACHE_BLOCKS = (4, 5, 7, 8)


def _in_kernel(*refs, n_alias, cache_tiles):
    x_ref, g_ref, mod_ref, w_ref = refs[:4]
    o_ref = refs[4 + n_alias]
    cache_refs = refs[5 + n_alias:5 + n_alias + len(cache_tiles)]
    h_ref = refs[-1]
    j = pl.program_id(1)

    @pl.when(j == 0)
    def _():
        m = mod_ref[0]
        h = _modnorm(x_ref[...], g_ref[...], m[:, 0:D_MODEL], m[:, D_MODEL:2 * D_MODEL])
        h_ref[...] = h.astype(BF16)

    res = jnp.dot(h_ref[...], w_ref[...], preferred_element_type=F32)
    o_ref[...] = res.astype(o_ref.dtype)
    for (tile, off), c_ref in zip(cache_tiles, cache_refs):
        @pl.when(j == tile)
        def _(c_ref=c_ref, off=off):
            c_ref[...] = res[:, off:off + BRANCH_W].reshape(c_ref.shape)


def _in_proj(x, g, mod, w, rows_per_mod, out_dtype, layer=None, caches=None):
    T = x.shape[0]
    tm, tn = IN_TM, IN_TN
    per = rows_per_mod // tm
    in_specs = [
        pl.BlockSpec((tm, D_MODEL), lambda i, j: (i, 0)),
        pl.BlockSpec((1, D_MODEL), lambda i, j: (0, 0)),
        pl.BlockSpec((1, 1, 6 * D_MODEL), lambda i, j: (i // per, 0, 0)),
        pl.BlockSpec((D_MODEL, tn), lambda i, j: (0, j)),
    ]
    out_specs = [pl.BlockSpec((tm, tn), lambda i, j: (i, j))]
    out_shape = [jax.ShapeDtypeStruct((T, MIX_W), out_dtype)]
    args = [x, g, mod, w]
    cache_tiles, aliases = (), {}
    if layer is not None:
        seqs = tm // SEQ
        cache_tiles = tuple(divmod(c * BRANCH_W, tn) for c in CACHE_BLOCKS)
        out_specs += [pl.BlockSpec((seqs, 1, SEQ, BRANCH_W), lambda i, j: (i, layer, 0, 0))] * len(CACHE_BLOCKS)
        out_shape += [jax.ShapeDtypeStruct((T // SEQ, DEPTH, SEQ, BRANCH_W), F32)] * len(CACHE_BLOCKS)
        if caches is not None:
            in_specs += [pl.BlockSpec(memory_space=pl.ANY)] * len(caches)
            aliases = {4 + n: 1 + n for n in range(len(caches))}
            args += list(caches)
    n_alias = len(args) - 4
    outs = pl.pallas_call(
        functools.partial(_in_kernel, n_alias=n_alias, cache_tiles=cache_tiles),
        grid=(T // tm, MIX_W // tn),
        in_specs=in_specs,
        out_specs=out_specs,
        out_shape=out_shape,
        input_output_aliases=aliases,
        scratch_shapes=[pltpu.VMEM((tm, D_MODEL), BF16)],
        compiler_params=_cparams("arbitrary", "arbitrary"),
        name="in_proj",
    )(*args)
    return outs[0] if layer is None else (outs[0], tuple(outs[1:]))


def _mid_kernel(x_ref, g_ref, mod_ref, yh_ref, yn_ref, yd_ref, wg_ref, wl_ref, wo_ref, o_ref):
    m = mod_ref[0]
    x = x_ref[...]
    h = _modnorm(x, g_ref[...], m[:, 0:D_MODEL], m[:, D_MODEL:2 * D_MODEL]).astype(BF16)
    merged = None
    for br, y_ref in enumerate((yh_ref, yn_ref, yd_ref)):
        gate = _sigmoid(jnp.dot(h, wg_ref[:, br * D_MODEL:(br + 1) * D_MODEL], preferred_element_type=F32))
        lift = jnp.dot(y_ref[...].astype(BF16), wl_ref[br], preferred_element_type=F32)
        t = gate * lift
        merged = t if merged is None else merged + t
    o_ref[...] = x + m[:, 2 * D_MODEL:3 * D_MODEL] * _bdot(merged, wo_ref[...])


def _merge_out(x, g, mod, y_hy, y_na, y_da, w_gate, w_lift, w_out, rows_per_mod):
    T = x.shape[0]
    tm = 512
    per = rows_per_mod // tm
    row = lambda i: (i, 0)
    const2 = lambda i: (0, 0)
    return pl.pallas_call(
        _mid_kernel,
        grid=(T // tm,),
        in_specs=[
            pl.BlockSpec((tm, D_MODEL), row),
            pl.BlockSpec((1, D_MODEL), const2),
            pl.BlockSpec((1, 1, 6 * D_MODEL), lambda i: (i // per, 0, 0)),
            pl.BlockSpec((tm, BRANCH_W), row),
            pl.BlockSpec((tm, BRANCH_W), row),
            pl.BlockSpec((tm, BRANCH_W), row),
            pl.BlockSpec((D_MODEL, 3 * D_MODEL), const2),
            pl.BlockSpec((3, BRANCH_W, D_MODEL), lambda i: (0, 0, 0)),
            pl.BlockSpec((D_MODEL, D_MODEL), const2),
        ],
        out_specs=pl.BlockSpec((tm, D_MODEL), row),
        out_shape=jax.ShapeDtypeStruct((T, D_MODEL), F32),
        compiler_params=_cparams("arbitrary"),
        name="merge_out",
    )(x, g, mod, y_hy, y_na, y_da, w_gate, w_lift, w_out)


FFN_CHUNK = D_FF // 2


def _ffn_kernel(x_ref, g_ref, mod_ref, w1g_ref, w1u_ref, w2_ref, gf_ref, o_ref, h_ref, acc_ref, *, final):
    k = pl.program_id(1)

    @pl.when(k == 0)
    def _():
        m = mod_ref[0]
        h = _modnorm(x_ref[...], g_ref[...], m[:, 3 * D_MODEL:4 * D_MODEL], m[:, 4 * D_MODEL:5 * D_MODEL])
        h_ref[...] = h.astype(BF16)

    h = h_ref[...]
    a = jnp.dot(h, w1g_ref[...], preferred_element_type=F32)
    b = jnp.dot(h, w1u_ref[...], preferred_element_type=F32)
    part = _bdot(a * _sigmoid(a) * b, w2_ref[...])

    @pl.when(k == 0)
    def _():
        acc_ref[...] = part

    @pl.when(k == 1)
    def _():
        m = mod_ref[0]
        xn = x_ref[...] + m[:, 5 * D_MODEL:6 * D_MODEL] * (acc_ref[...] + part)
        if final:
            xn = _rms(xn, gf_ref[...])
        o_ref[...] = xn


def _ffn(x, g, mod, w_ffn_in, w_ffn_out, g_final, rows_per_mod, final):
    T = x.shape[0]
    tm = 512
    per = rows_per_mod // tm
    return pl.pallas_call(
        functools.partial(_ffn_kernel, final=final),
        grid=(T // tm, 2),
        in_specs=[
            pl.BlockSpec((tm, D_MODEL), lambda i, k: (i, 0)),
            pl.BlockSpec((1, D_MODEL), lambda i, k: (0, 0)),
            pl.BlockSpec((1, 1, 6 * D_MODEL), lambda i, k: (i // per, 0, 0)),
            pl.BlockSpec((D_MODEL, FFN_CHUNK), lambda i, k: (0, k)),
            pl.BlockSpec((D_MODEL, FFN_CHUNK), lambda i, k: (0, 2 + k)),
            pl.BlockSpec((FFN_CHUNK, D_MODEL), lambda i, k: (k, 0)),
            pl.BlockSpec((1, D_MODEL), lambda i, k: (0, 0)),
        ],
        out_specs=pl.BlockSpec((tm, D_MODEL), lambda i, k: (i, 0)),
        out_shape=jax.ShapeDtypeStruct((T, D_MODEL), F32),
        scratch_shapes=[pltpu.VMEM((tm, D_MODEL), BF16), pltpu.VMEM((tm, D_MODEL), F32)],
        compiler_params=_cparams("arbitrary", "arbitrary"),
        name="ffn",
    )(x, g, mod, w_ffn_in, w_ffn_in, w_ffn_out, g_final)


def _da_lambda(lam_ref, lam_init):
    lp = lam_ref[...]
    a = jnp.sum(lp[0:1] * lp[1:2], axis=1, keepdims=True)
    b = jnp.sum(lp[2:3] * lp[3:4], axis=1, keepdims=True)
    return jnp.exp(a) - jnp.exp(b) + lam_init


def _softmax_rows(s):
    m = jnp.max(s, axis=-1, keepdims=True)
    p = jnp.exp(s - m)
    return p, jnp.sum(p, axis=-1, keepdims=True)


ATT_ONES_ROWS = 16
ATT_TQ = 256
ATT_KEYS = DEC_SEQ + PAST_LEN
LOG2E = math.log2(math.e)


def _masked_q_blocks(qt, d):
    row = lax.broadcasted_iota(jnp.int32, qt.shape, 0)
    zero = jnp.zeros_like(qt)
    return jnp.concatenate([jnp.where((row >= j * d) & (row < (j + 1) * d), qt, zero) for j in range(128 // d)], axis=1)


def _colmax(st):
    keys, n = st.shape
    return jnp.max(jnp.max(st.reshape(keys // 256, 256, n), axis=0), axis=0, keepdims=True)


def _ctx_attn_kernel(nq_ref, nk_ref, nv_ref, dq_ref, dk_ref, dv_ref, lam_ref, sub_ref, yn_ref, yd_ref, acc_ref,
                     *, lam_init):
    lam = _da_lambda(lam_ref, lam_init)
    ones = jnp.ones((ATT_ONES_ROWS, SEQ), BF16)

    def attend(q_ref, k_ref, v_ref, d, maps_per_head, finish):
        qt = (q_ref[...] * (d ** -0.5 * LOG2E)).T.astype(BF16)
        vt = v_ref[...].T.astype(BF16)
        kb = k_ref[...].astype(BF16)
        dv = NA_HEAD_DIM
        heads_per_group = 128 // (d * maps_per_head)
        w = maps_per_head * SEQ
        for g in range(BRANCH_W // 128):
            st = jnp.dot(kb[:, g * 128:(g + 1) * 128], _masked_q_blocks(qt[g * 128:(g + 1) * 128], d),
                         preferred_element_type=F32)
            pt = jnp.exp2(st - _colmax(st)).astype(BF16)
            for j in range(heads_per_group):
                h = g * heads_per_group + j
                ve = jnp.concatenate([vt[h * dv:(h + 1) * dv], ones], axis=0)
                oe = jnp.dot(ve, pt[:, j * w:(j + 1) * w], preferred_element_type=F32)
                os = [oe[0:dv, i * SEQ:(i + 1) * SEQ] / oe[dv:dv + 1, i * SEQ:(i + 1) * SEQ]
                      for i in range(maps_per_head)]
                acc_ref[h * dv:(h + 1) * dv, :] = finish(os)

    attend(nq_ref, nk_ref, nv_ref, NA_HEAD_DIM, 1, lambda os: os[0])
    yn_ref[...] = acc_ref[...].T

    def da_finish(os):
        ot = os[0] - lam * os[1]
        ot = ot * lax.rsqrt(jnp.mean(ot * ot, axis=0, keepdims=True) + EPS) * sub_ref[...]
        return ot * (1.0 - lam_init)

    attend(dq_ref, dk_ref, dv_ref, DA_HEAD_DIM, 2, da_finish)
    yd_ref[...] = acc_ref[...].T


def _ctx_attention(u, da_lambda, subln_col, lam_init):
    col = lambda j: pl.BlockSpec((SEQ, BRANCH_W), lambda b, j=j: (b, j))
    out = pl.BlockSpec((SEQ, BRANCH_W), lambda b: (b, 0))
    shape = jax.ShapeDtypeStruct((BATCH * SEQ, BRANCH_W), F32)
    return pl.pallas_call(
        functools.partial(_ctx_attn_kernel, lam_init=lam_init),
        grid=(BATCH,),
        in_specs=[col(3), col(4), col(5), col(6), col(7), col(8),
                  pl.BlockSpec((4, DA_HEAD_DIM), lambda b: (0, 0)),
                  pl.BlockSpec((DA_V_DIM, 1), lambda b: (0, 0))],
        out_specs=[out, out],
        out_shape=[shape, shape],
        scratch_shapes=[pltpu.VMEM((BRANCH_W, SEQ), F32)],
        compiler_params=_cparams("arbitrary"),
        name="ctx_attention",
    )(u, u, u, u, u, u, da_lambda, subln_col)


def _rope(x, cos, sin_signed):
    n = x.shape[-1]
    lane = lax.broadcasted_iota(jnp.int32, x.shape, 1)
    partner = jnp.where(lane % 2 == 0, pltpu.roll(x, n - 1, axis=1), pltpu.roll(x, 1, axis=1))
    return x * cos + partner * sin_signed


def _attn_prep_kernel(q_ref, k_ref, v_ref, kc_ref, vc_ref, cos_ref, sin_ref, qt_ref, ko_ref, vt_ref, *, rope, scale):
    t = pl.program_id(1)
    dv = NA_HEAD_DIM

    def put_v(v):
        vt = v.astype(F32).T.astype(BF16)
        ones = jnp.ones((ATT_ONES_ROWS, ATT_TQ), BF16)
        for h in range(BRANCH_W // dv):
            vt_ref[0, h, 0:dv, :] = vt[h * dv:(h + 1) * dv]
            vt_ref[0, h, dv:dv + ATT_ONES_ROWS, :] = ones

    @pl.when(t < DEC_SEQ // ATT_TQ)
    def _():
        q = q_ref[...].astype(F32)
        k = k_ref[...].astype(F32)
        if rope:
            q = _rope(q, cos_ref[...], sin_ref[...])
            k = _rope(k, cos_ref[...], sin_ref[...])
        qt_ref[0] = (q * scale).T.astype(BF16)
        ko_ref[0] = k.astype(BF16)
        put_v(v_ref[...])

    @pl.when(t == DEC_SEQ // ATT_TQ)
    def _():
        ko_ref[0] = kc_ref[0].astype(BF16)
        put_v(vc_ref[0])


def _attn_prep(u, first_col, k_ctx, v_ctx, cos, sin, rope, head_dim):
    nt = DEC_SEQ // ATT_TQ
    last = nt - 1
    rowblk = lambda j: pl.BlockSpec((ATT_TQ, BRANCH_W), lambda b, t, j=j: (b * nt + jnp.minimum(t, last), j))
    tab = pl.BlockSpec((ATT_TQ, BRANCH_W), lambda b, t: (jnp.minimum(t, last), 0))
    ctx = pl.BlockSpec((1, PAST_LEN, BRANCH_W), lambda b, t: (b, 0, 0))
    heads = BRANCH_W // NA_HEAD_DIM
    vrows = NA_HEAD_DIM + ATT_ONES_ROWS
    return pl.pallas_call(
        functools.partial(_attn_prep_kernel, rope=rope, scale=head_dim ** -0.5 * LOG2E),
        grid=(DEC_BATCH, nt + 1),
        in_specs=[rowblk(first_col), rowblk(first_col + 1), rowblk(first_col + 2), ctx, ctx, tab, tab],
        out_specs=[
            pl.BlockSpec((1, BRANCH_W, ATT_TQ), lambda b, t: (b, 0, jnp.minimum(t, last))),
            pl.BlockSpec((1, ATT_TQ, BRANCH_W), lambda b, t: (b, t, 0)),
            pl.BlockSpec((1, heads, vrows, ATT_TQ), lambda b, t: (b, 0, 0, t)),
        ],
        out_shape=[
            jax.ShapeDtypeStruct((DEC_BATCH, BRANCH_W, DEC_SEQ), BF16),
            jax.ShapeDtypeStruct((DEC_BATCH, ATT_KEYS, BRANCH_W), BF16),
            jax.ShapeDtypeStruct((DEC_BATCH, heads, vrows, ATT_KEYS), BF16),
        ],
        compiler_params=_cparams("arbitrary", "arbitrary"),
        name="attn_prep",
    )(u, u, u, k_ctx, v_ctx, cos, sin)


NA_ROWS = ATT_TQ // GRID_W
NA_UNION = 3 * NA_ROWS
NA_STEPS = GRID_H // NA_ROWS
NA_SLABS = NA_UNION // NA_ROWS
NA_VARIANT_OFFSET = (0, -NA_ROWS, -2 * NA_ROWS)


def _na_variant(s):
    return jnp.minimum(s, 1) + s // (NA_STEPS - 1)


def _na_window_block(s):
    return jnp.clip(s - 1, 0, NA_STEPS - NA_SLABS)


def _na_bias_kernel(rpb_ref, o_ref):
    kc = lax.broadcasted_iota(jnp.int32, (GRID_W, GRID_W), 0)
    qc = lax.broadcasted_iota(jnp.int32, (GRID_W, GRID_W), 1)
    dc = jnp.clip(kc - qc, -(NA_WIN_COLS - 1), NA_WIN_COLS - 1) + (NA_WIN_COLS - 1)
    c0 = jnp.clip(qc - NA_WIN_COLS // 2, 0, GRID_W - NA_WIN_COLS)
    col_ok = (kc >= c0) & (kc < c0 + NA_WIN_COLS)
    r = rpb_ref[0, 0] * LOG2E
    masked = jnp.full((GRID_W, GRID_W), NEG_INF, F32)
    tiles = []
    for dr in range(2 * NA_WIN_ROWS - 1):
        acc = jnp.zeros((GRID_W, GRID_W), F32)
        for d in range(2 * NA_WIN_COLS - 1):
            acc = jnp.where(dc == d, r[dr:dr + 1, d:d + 1], acc)
        tiles.append(jnp.where(col_ok, acc, masked))
    for v, off in enumerate(NA_VARIANT_OFFSET):
        for kr in range(NA_UNION):
            for rr in range(NA_ROWS):
                w0 = (0, rr, NA_UNION - NA_WIN_ROWS)[v]
                dr = kr + off - rr
                inside = w0 <= kr < w0 + NA_WIN_ROWS
                o_ref[0, v, 0, kr * GRID_W:(kr + 1) * GRID_W, rr * GRID_W:(rr + 1) * GRID_W] = (
                    tiles[dr + NA_WIN_ROWS - 1] if inside else masked)


def _na_bias_table(na_rpb):
    n_dr, n_dc = 2 * NA_WIN_ROWS - 1, 2 * NA_WIN_COLS - 1
    nv = len(NA_VARIANT_OFFSET)
    return pl.pallas_call(
        _na_bias_kernel,
        grid=(DEPTH, NA_HEADS),
        in_specs=[pl.BlockSpec((1, 1, n_dr, n_dc), lambda l, h: (l, h, 0, 0))],
        out_specs=pl.BlockSpec((1, nv, 1, NA_UNION * GRID_W, ATT_TQ), lambda l, h: (l, 0, h, 0, 0)),
        out_shape=jax.ShapeDtypeStruct((DEPTH, nv, NA_HEADS, NA_UNION * GRID_W, ATT_TQ), F32),
        compiler_params=_cparams("arbitrary", "arbitrary"),
        name="na_bias_table",
    )(na_rpb)


def _na_kernel(qt_ref, *refs):
    n = NA_SLABS + 1
    k_refs, vt_refs = refs[:n], refs[n:2 * n]
    bias_ref, o_ref, acc_ref = refs[2 * n:]
    dv = NA_HEAD_DIM
    heads_per_group = 128 // dv
    for g in range(BRANCH_W // 128):
        lanes = slice(g * 128, (g + 1) * 128)
        qbd = _masked_q_blocks(qt_ref[0, lanes, :], dv)
        sts = []
        for j, k_ref in enumerate(k_refs):
            st = jnp.dot(k_ref[0, :, lanes], qbd, preferred_element_type=F32)
            if j < NA_SLABS:
                rows = slice(j * ATT_TQ, (j + 1) * ATT_TQ)
                st = st + jnp.concatenate(
                    [bias_ref[0, g * heads_per_group + hh, rows, :] for hh in range(heads_per_group)], axis=1)
            sts.append(st)
        mx = functools.reduce(jnp.maximum, [_colmax(st) for st in sts])
        pts = [jnp.exp2(st - mx).astype(BF16) for st in sts]
        for hh in range(heads_per_group):
            h = g * heads_per_group + hh
            oe = sum(jnp.dot(vt_ref[0, h], pt[:, hh * ATT_TQ:(hh + 1) * ATT_TQ], preferred_element_type=F32)
                     for vt_ref, pt in zip(vt_refs, pts))
            acc_ref[h * dv:(h + 1) * dv, :] = oe[0:dv] / oe[dv:dv + 1]
    o_ref[...] = acc_ref[...].T


def _nbr_attention(qt, k, vt, bias):
    vrows = NA_HEAD_DIM + ATT_ONES_ROWS
    ctx_blk = DEC_SEQ // ATT_TQ
    k_specs = [pl.BlockSpec((1, ATT_TQ, BRANCH_W), lambda b, s, j=j: (b, _na_window_block(s) + j, 0))
               for j in range(NA_SLABS)]
    k_specs.append(pl.BlockSpec((1, ATT_TQ, BRANCH_W), lambda b, s: (b, ctx_blk, 0)))
    vt_specs = [pl.BlockSpec((1, NA_HEADS, vrows, ATT_TQ), lambda b, s, j=j: (b, 0, 0, _na_window_block(s) + j))
                for j in range(NA_SLABS)]
    vt_specs.append(pl.BlockSpec((1, NA_HEADS, vrows, ATT_TQ), lambda b, s: (b, 0, 0, ctx_blk)))
    n = NA_SLABS + 1
    return pl.pallas_call(
        _na_kernel,
        grid=(DEC_BATCH, NA_STEPS),
        in_specs=[pl.BlockSpec((1, BRANCH_W, ATT_TQ), lambda b, s: (b, 0, s))] + k_specs + vt_specs + [
            pl.BlockSpec((1, NA_HEADS, NA_UNION * GRID_W, ATT_TQ), lambda b, s: (_na_variant(s), 0, 0, 0))],
        out_specs=pl.BlockSpec((ATT_TQ, BRANCH_W), lambda b, s: (b * NA_STEPS + s, 0)),
        out_shape=jax.ShapeDtypeStruct((DEC_BATCH * DEC_SEQ, BRANCH_W), F32),
        scratch_shapes=[pltpu.VMEM((BRANCH_W, ATT_TQ), F32)],
        compiler_params=_cparams("arbitrary", "arbitrary"),
        name="nbr_attention",
    )(qt, *([k] * n), *([vt] * n), bias)


DA_TQ = ATT_TQ
DA_KEYS = ATT_KEYS
DA_ONES_ROWS = ATT_ONES_ROWS
DA_MAPS_PER_TILE = 128 // DA_HEAD_DIM


def _da_kernel(qt_ref, k_ref, vt_ref, lam_ref, sub_ref, o_ref, acc_ref, *, lam_init):
    lam = _da_lambda(lam_ref, lam_init)
    st = jnp.dot(k_ref[0], _masked_q_blocks(qt_ref[0], DA_HEAD_DIM), preferred_element_type=F32)
    pt = jnp.exp2(st - _colmax(st)).astype(BF16)
    heads = DA_MAPS_PER_TILE // 2
    for h in range(heads):
        oe = jnp.dot(vt_ref[0, h], pt[:, 2 * h * DA_TQ:(2 * h + 2) * DA_TQ], preferred_element_type=F32)
        os = [oe[0:DA_V_DIM, i * DA_TQ:(i + 1) * DA_TQ] / oe[DA_V_DIM:DA_V_DIM + 1, i * DA_TQ:(i + 1) * DA_TQ]
              for i in range(2)]
        ot = os[0] - lam * os[1]
        ot = ot * lax.rsqrt(jnp.mean(ot * ot, axis=0, keepdims=True) + EPS) * sub_ref[...]
        acc_ref[h * DA_V_DIM:(h + 1) * DA_V_DIM, :] = ot * (1.0 - lam_init)
    o_ref[...] = acc_ref[...].T


def _diff_attention(qt, k, vt, da_lambda, subln_col, lam_init):
    nt = DEC_SEQ // DA_TQ
    vrows = DA_V_DIM + DA_ONES_ROWS
    groups = BRANCH_W // 128
    heads = DA_MAPS_PER_TILE // 2
    return pl.pallas_call(
        functools.partial(_da_kernel, lam_init=lam_init),
        grid=(DEC_BATCH, groups, nt),
        in_specs=[
            pl.BlockSpec((1, 128, DA_TQ), lambda b, g, t: (b, g, t)),
            pl.BlockSpec((1, DA_KEYS, 128), lambda b, g, t: (b, 0, g)),
            pl.BlockSpec((1, heads, vrows, DA_KEYS), lambda b, g, t: (b, g, 0, 0)),
            pl.BlockSpec((4, DA_HEAD_DIM), lambda b, g, t: (0, 0)),
            pl.BlockSpec((DA_V_DIM, 1), lambda b, g, t: (0, 0)),
        ],
        out_specs=pl.BlockSpec((DA_TQ, 128), lambda b, g, t: (b * nt + t, g)),
        out_shape=jax.ShapeDtypeStruct((DEC_BATCH * DEC_SEQ, BRANCH_W), F32),
        scratch_shapes=[pltpu.VMEM((128, DA_TQ), F32)],
        compiler_params=_cparams("arbitrary", "arbitrary", "arbitrary"),
        name="diff_attention",
    )(qt, k, vt, da_lambda, subln_col)


def _rope_tables():
    pos = np.arange(DEC_SEQ)
    row = (pos // GRID_W).astype(np.float32)
    col = (pos % GRID_W).astype(np.float32)
    n_freq = DA_HEAD_DIM // 4
    inv = (np.float32(ROPE_BASE) ** (-np.arange(n_freq, dtype=np.float32) / n_freq)).astype(np.float32)
    ang = np.concatenate([row[:, None] * inv[None, :], col[:, None] * inv[None, :]], axis=-1)
    ang = ang.astype(np.float64)
    cos = np.repeat(np.cos(ang), 2, axis=-1)
    sin = np.repeat(np.sin(ang), 2, axis=-1)
    sign = np.where(np.arange(DA_HEAD_DIM) % 2 == 0, -1.0, 1.0)
    reps = BRANCH_W // DA_HEAD_DIM
    cos = np.tile(cos, (1, reps)).astype(np.float32)
    sin = np.tile(sin * sign[None, :], (1, reps)).astype(np.float32)
    return jnp.asarray(cos), jnp.asarray(sin)


def _filt_hidden_kernel(feat_ref, w1_ref, b1_ref, w2_ref, b2_ref, fr_ref, o_ref):
    fr = fr_ref[0]
    h = jnp.sin(fr * (jnp.dot(feat_ref[...], w1_ref[0], precision=HIGHEST, preferred_element_type=F32) + b1_ref[0]))
    o_ref[0] = jnp.sin(fr * (jnp.dot(h, w2_ref[0], precision=HIGHEST, preferred_element_type=F32) + b2_ref[0]))


def _filt_kernel(h_ref, w3f_ref, w3b_ref, dec_ref, hf_ref, hb_ref):
    h = h_ref[0]
    dec = dec_ref[...]
    hf = jnp.dot(h, w3f_ref[0], precision=HIGHEST, preferred_element_type=F32) * dec
    hb = jnp.dot(h, w3b_ref[0], precision=HIGHEST, preferred_element_type=F32) * dec
    row = lax.broadcasted_iota(jnp.int32, hb.shape, 0)
    hb = jnp.where(row == 0, 0.0, hb)
    nrm = jnp.sum(jnp.abs(hf), axis=0, keepdims=True) + jnp.sum(jnp.abs(hb), axis=0, keepdims=True)
    hf_ref[0, 0] = hf / nrm
    hb_ref[0, 0] = hb / nrm


def _hyena_pos_tables(L):
    f32 = np.float32
    pos = np.arange(L, dtype=f32)
    t = (pos / f32(L)).astype(f32)
    bands = np.linspace(1e-4, HY_POS_BANDS - 1, HY_POS_BANDS, dtype=f32)
    ang = (f32(2 * math.pi / L) * pos[:, None] * bands[None, :]).astype(np.float64)
    feats = np.zeros((L, HY_FILT_HIDDEN), f32)
    feats[:, 0] = t
    feats[:, 1:1 + HY_POS_BANDS] = np.cos(ang)
    feats[:, 1 + HY_POS_BANDS:HY_POS_DIM] = -np.sin(ang)
    deltas = np.linspace(math.log(HY_DECAY_TARGET) / HY_SLOW_DECAY,
                         math.log(HY_DECAY_TARGET) / HY_FAST_DECAY, BRANCH_W, dtype=f32)
    decay = np.exp((-t[:, None] * np.abs(deltas)[None, :]).astype(np.float64)).astype(f32)
    return jnp.asarray(feats), jnp.asarray(decay)


def _hyena_filters(L, w1p, b1, w2, b2, w3, freq):
    feats, decay = _hyena_pos_tables(L)
    cb = 128
    ncb = BRANCH_W // cb
    small = lambda shape: pl.BlockSpec((1,) + shape, lambda l: (l, 0, 0))
    hidden = pl.pallas_call(
        _filt_hidden_kernel,
        grid=(DEPTH,),
        in_specs=[
            pl.BlockSpec((L, HY_FILT_HIDDEN), lambda l: (0, 0)),
            small((HY_FILT_HIDDEN, HY_FILT_HIDDEN)), small((1, HY_FILT_HIDDEN)),
            small((HY_FILT_HIDDEN, HY_FILT_HIDDEN)), small((1, HY_FILT_HIDDEN)),
            small((1, HY_FILT_HIDDEN)),
        ],
        out_specs=pl.BlockSpec((1, L, HY_FILT_HIDDEN), lambda l: (l, 0, 0)),
        out_shape=jax.ShapeDtypeStruct((DEPTH, L, HY_FILT_HIDDEN), F32),
        compiler_params=_cparams("arbitrary"),
        name=f"hyena_filter_hidden_{L}",
    )(feats, w1p, b1, w2, b2, freq)
    shape = jax.ShapeDtypeStruct((DEPTH, 2, L, BRANCH_W), F32)
    out = pl.BlockSpec((1, 1, L, cb), lambda l, o, c: (l, o, 0, c))
    return pl.pallas_call(
        _filt_kernel,
        grid=(DEPTH, 2, ncb),
        in_specs=[
            pl.BlockSpec((1, L, HY_FILT_HIDDEN), lambda l, o, c: (l, 0, 0)),
            pl.BlockSpec((1, HY_FILT_HIDDEN, cb), lambda l, o, c: (l, 0, o * 2 * ncb + c)),
            pl.BlockSpec((1, HY_FILT_HIDDEN, cb), lambda l, o, c: (l, 0, o * 2 * ncb + ncb + c)),
            pl.BlockSpec((L, cb), lambda l, o, c: (0, c)),
        ],
        out_specs=[out, out],
        out_shape=[shape, shape],
        compiler_params=_cparams("arbitrary", "arbitrary", "arbitrary"),
        name=f"hyena_filters_{L}",
    )(hidden, w3, w3, decay)


def _short_conv(u, w_ref, b_ref, seq_len):
    n = u.shape[0]
    t = lax.broadcasted_iota(jnp.int32, u.shape, 0) % seq_len
    prev = jnp.where(t == 0, 0.0, pltpu.roll(u, 1, axis=0))
    nxt = jnp.where(t == seq_len - 1, 0.0, pltpu.roll(u, n - 1, axis=0))
    return prev * w_ref[0:1, :] + u * w_ref[1:2, :] + nxt * w_ref[2:3, :] + b_ref[...]


def _dft_direct_mats():
    n, half = 2 * SEQ, SEQ
    k = np.arange(n)[:, None].astype(np.float64)
    t = np.arange(half)[None, :].astype(np.float64)
    ang = 2 * np.pi * k * t / n
    fr, fi = np.cos(ang), -np.sin(ang)
    mf = np.block([[fr, -fi], [fi, fr]])
    gr, gi = np.cos(ang).T / n, np.sin(ang).T / n
    mi = np.block([[gr, -gi], [gi, gr]])
    return mf.astype(np.float32), mi.astype(np.float32)


def _spec_direct_kernel(hf_ref, hb_ref, m_ref, o_ref):
    m = m_ref[...]
    wf = jnp.dot(m, hf_ref[0, 0], precision=HIGHEST, preferred_element_type=F32)
    wb = jnp.dot(m, hb_ref[0, 0], precision=HIGHEST, preferred_element_type=F32)
    n = 2 * SEQ
    o_ref[0, 0, 0:n] = wf[0:n] + wb[0:n]
    o_ref[0, 0, n:2 * n] = wf[n:2 * n] - wb[n:2 * n]


def _spec_direct(hf, hb, mf_real):
    n = 2 * SEQ
    blk = pl.BlockSpec((1, 1, SEQ, BRANCH_W), lambda l, o: (l, o, 0, 0))
    return pl.pallas_call(
        _spec_direct_kernel,
        grid=(DEPTH, 2),
        in_specs=[blk, blk, pl.BlockSpec((2 * n, SEQ), lambda l, o: (0, 0))],
        out_specs=pl.BlockSpec((1, 1, 2 * n, BRANCH_W), lambda l, o: (l, o, 0, 0)),
        out_shape=jax.ShapeDtypeStruct((DEPTH, 2, 2 * n, BRANCH_W), F32),
        compiler_params=_cparams("arbitrary", "arbitrary"),
        name="hyena_spectrum_direct",
    )(hf, hb, mf_real)


def _lconv_direct_kernel(s_ref, g_ref, cws_ref, cbs_ref, cwg_ref, cbg_ref, h_ref, bias_ref, mf_ref, mi_ref, o_ref,
                         *, conv_sig):
    n = 2 * SEQ
    sig = s_ref[...]
    if conv_sig:
        sig = _short_conv(sig, cws_ref, cbs_ref, SEQ)
    gate = _short_conv(g_ref[...], cwg_ref, cbg_ref, SEQ)
    z = jnp.dot(mf_ref[...], sig.astype(BF16), preferred_element_type=F32)
    zr, zi = z[0:n], z[n:2 * n]
    hr, hi = h_ref[0:n], h_ref[n:2 * n]
    y = jnp.concatenate([zr * hr - zi * hi, zr * hi + zi * hr], axis=0)
    y = jnp.dot(mi_ref[...], y.astype(BF16), preferred_element_type=F32)
    o_ref[...] = gate * (y + sig * bias_ref[...])


def _lconv_direct(sig, sig_col, gate_src, gate_col, conv_w, conv_b, spec, bias, mf, mi, conv_sig):
    n = 2 * SEQ
    rows = 2 * SEQ
    T = sig.shape[0]
    return pl.pallas_call(
        functools.partial(_lconv_direct_kernel, conv_sig=conv_sig),
        grid=(T // rows,),
        in_specs=[
            pl.BlockSpec((rows, BRANCH_W), lambda p: (p, sig_col)),
            pl.BlockSpec((rows, BRANCH_W), lambda p: (p, gate_col)),
            pl.BlockSpec((3, BRANCH_W), lambda p: (0, 0)),
            pl.BlockSpec((1, BRANCH_W), lambda p: (0, 0)),
            pl.BlockSpec((3, BRANCH_W), lambda p: (0, gate_col)),
            pl.BlockSpec((1, BRANCH_W), lambda p: (0, gate_col)),
            pl.BlockSpec((2 * n, BRANCH_W), lambda p: (0, 0)),
            pl.BlockSpec((1, BRANCH_W), lambda p: (0, 0)),
            pl.BlockSpec((2 * n, rows), lambda p: (0, 0)),
            pl.BlockSpec((rows, 2 * n), lambda p: (0, 0)),
        ],
        out_specs=pl.BlockSpec((rows, BRANCH_W), lambda p: (p, 0)),
        out_shape=jax.ShapeDtypeStruct((T, BRANCH_W), F32),
        compiler_params=_cparams("arbitrary"),
        name="hyena_lconv_direct",
    )(sig, gate_src, conv_w, conv_b, conv_w, conv_b, spec, bias, mf, mi)


def _dft_two_stage_mats():
    no, ni, half, n = FFT_NO, FFT_NI, FFT_HALF, FFT_N
    f64 = np.float64
    k1 = np.arange(no, dtype=f64)
    n_o = np.arange(half, dtype=f64)
    n_i = np.arange(ni, dtype=f64)
    ang = 2 * np.pi * (n_i[:, None, None] * k1[None, :, None] / n + k1[None, :, None] * n_o[None, None, :] / no)
    tr, ti = np.cos(ang), -np.sin(ang)
    m1 = np.concatenate([np.concatenate([tr, -ti], axis=2), np.concatenate([ti, tr], axis=2)], axis=1)
    k2 = np.arange(ni, dtype=f64)
    ang2 = 2 * np.pi * k2[:, None] * n_i[None, :] / ni
    f2r, f2i = np.cos(ang2), -np.sin(ang2)
    m2 = np.block([[f2r, -f2i], [f2i, f2r]])
    m2c = np.block([[f2r, f2i], [-f2i, f2r]])
    sr, si = np.transpose(tr, (0, 2, 1)) / n, -np.transpose(ti, (0, 2, 1)) / n
    m3 = np.concatenate([np.concatenate([sr, -si], axis=2), np.concatenate([si, sr], axis=2)], axis=1)
    return (m1.astype(np.float32), m2.astype(np.float32), m2c.astype(np.float32), m3.astype(np.float32))


def _fwd_stage1(za_ref, zb_ref, m1_ref, w_ref):
    def body(ni, carry):
        a = za_ref[pl.ds(ni, FFT_HALF, stride=FFT_NI), :]
        if zb_ref is None:
            out = jnp.dot(m1_ref[ni][:, 0:FFT_HALF], a.astype(BF16), preferred_element_type=F32)
        else:
            b = zb_ref[pl.ds(ni, FFT_HALF, stride=FFT_NI), :]
            out = jnp.dot(m1_ref[ni], jnp.concatenate([a, b], axis=0).astype(BF16), preferred_element_type=F32)
        w_ref[pl.ds(ni, FFT_NO, stride=2 * FFT_NI), :] = out[0:FFT_NO]
        w_ref[pl.ds(FFT_NI + ni, FFT_NO, stride=2 * FFT_NI), :] = out[FFT_NO:2 * FFT_NO]
        return carry

    lax.fori_loop(0, FFT_NI, body, 0, unroll=FFT_UNROLL)


def _spec_two_stage_kernel(hf_ref, hb_ref, m1_ref, m2_ref, o_ref, wf_ref, wb_ref):
    _fwd_stage1(hf_ref.at[0, 0], None, m1_ref, wf_ref)
    _fwd_stage1(hb_ref.at[0, 0], None, m1_ref, wb_ref)
    blk = 2 * FFT_NI

    def body(k1, carry):
        rows = pl.ds(pl.multiple_of(k1 * blk, blk), blk)
        xf = jnp.dot(m2_ref[...], wf_ref[rows, :].astype(BF16), preferred_element_type=F32)
        xb = jnp.dot(m2_ref[...], wb_ref[rows, :].astype(BF16), preferred_element_type=F32)
        o_ref[0, 0, rows, :] = jnp.concatenate(
            [xf[0:FFT_NI] + xb[0:FFT_NI], xf[FFT_NI:blk] - xb[FFT_NI:blk]], axis=0)
        return carry

    lax.fori_loop(0, FFT_NO, body, 0, unroll=FFT_UNROLL)


def _spec_two_stage(hf, hb, m1, m2):
    cb = LCONV_CB
    blk = pl.BlockSpec((1, 1, DEC_SEQ, cb), lambda l, o, c: (l, o, 0, c))
    return pl.pallas_call(
        _spec_two_stage_kernel,
        grid=(DEPTH, 2, BRANCH_W // cb),
        in_specs=[blk, blk,
                  pl.BlockSpec((FFT_NI, 2 * FFT_NO, 2 * FFT_HALF), lambda l, o, c: (0, 0, 0)),
                  pl.BlockSpec((2 * FFT_NI, 2 * FFT_NI), lambda l, o, c: (0, 0))],
        out_specs=pl.BlockSpec((1, 1, 2 * FFT_N, cb), lambda l, o, c: (l, o, 0, c)),
        out_shape=jax.ShapeDtypeStruct((DEPTH, 2, 2 * FFT_N, BRANCH_W), F32),
        scratch_shapes=[pltpu.VMEM((2 * FFT_N, cb), F32), pltpu.VMEM((2 * FFT_N, cb), F32)],
        compiler_params=_cparams("arbitrary", "arbitrary", "arbitrary"),
        name="hyena_spectrum_two_stage",
    )(hf, hb, m1, m2)


def _lconv_two_stage_kernel(s_ref, g_ref, cws_ref, cbs_ref, cwg_ref, cbg_ref, h_ref, bias_ref,
                            m1_ref, m2_ref, m2c_ref, m3_ref, o_ref, z_ref, w_ref, *, conv_sig):
    for b in range(2):
        sig = s_ref[b].astype(F32)
        if conv_sig:
            sig = _short_conv(sig, cws_ref, cbs_ref, DEC_SEQ)
        z_ref[b] = sig
    _fwd_stage1(z_ref.at[0], z_ref.at[1], m1_ref, w_ref)
    blk = 2 * FFT_NI

    def mid(k1, carry):
        rows = pl.ds(pl.multiple_of(k1 * blk, blk), blk)
        x = jnp.dot(m2_ref[...], w_ref[rows, :].astype(BF16), preferred_element_type=F32)
        h = h_ref[rows, :]
        xr, xi = x[0:FFT_NI], x[FFT_NI:blk]
        hr, hi = h[0:FFT_NI], h[FFT_NI:blk]
        y = jnp.concatenate([xr * hr - xi * hi, xr * hi + xi * hr], axis=0)
        w_ref[rows, :] = jnp.dot(m2c_ref[...], y.astype(BF16), preferred_element_type=F32)
        return carry

    lax.fori_loop(0, FFT_NO, mid, 0, unroll=FFT_UNROLL)

    def last(ni, carry):
        cr = w_ref[pl.ds(ni, FFT_NO, stride=blk), :]
        ci = w_ref[pl.ds(FFT_NI + ni, FFT_NO, stride=blk), :]
        y = jnp.dot(m3_ref[ni], jnp.concatenate([cr, ci], axis=0).astype(BF16), preferred_element_type=F32)
        o_ref[0, pl.ds(ni, FFT_HALF, stride=FFT_NI), :] = y[0:FFT_HALF]
        o_ref[1, pl.ds(ni, FFT_HALF, stride=FFT_NI), :] = y[FFT_HALF:2 * FFT_HALF]
        return carry

    lax.fori_loop(0, FFT_NI, last, 0, unroll=FFT_UNROLL)
    for b in range(2):
        gate = _short_conv(g_ref[b].astype(F32), cwg_ref, cbg_ref, DEC_SEQ)
        sig = z_ref[b]
        o_ref[b] = gate * (o_ref[b] + sig * bias_ref[...])


def _lconv_two_stage(sig, sig_col, gate_src, gate_col, conv_w, conv_b, spec, bias, mats, conv_sig):
    cb = LCONV_CB
    ncb = BRANCH_W // cb
    m1, m2, m2c, m3 = mats
    const3 = lambda c, p: (0, 0, 0)
    const2 = lambda c, p: (0, 0)
    return pl.pallas_call(
        functools.partial(_lconv_two_stage_kernel, conv_sig=conv_sig),
        grid=(ncb, DEC_BATCH // 2),
        in_specs=[
            pl.BlockSpec((2, DEC_SEQ, cb), lambda c, p: (p, 0, sig_col * ncb + c)),
            pl.BlockSpec((2, DEC_SEQ, cb), lambda c, p: (p, 0, gate_col * ncb + c)),
            pl.BlockSpec((3, cb), lambda c, p: (0, c)),
            pl.BlockSpec((1, cb), lambda c, p: (0, c)),
            pl.BlockSpec((3, cb), lambda c, p: (0, gate_col * ncb + c)),
            pl.BlockSpec((1, cb), lambda c, p: (0, gate_col * ncb + c)),
            pl.BlockSpec((2 * FFT_N, cb), lambda c, p: (0, c)),
            pl.BlockSpec((1, cb), lambda c, p: (0, c)),
            pl.BlockSpec(m1.shape, const3),
            pl.BlockSpec(m2.shape, const2),
            pl.BlockSpec(m2c.shape, const2),
            pl.BlockSpec(m3.shape, const3),
        ],
        out_specs=pl.BlockSpec((2, DEC_SEQ, cb), lambda c, p: (p, 0, c)),
        out_shape=jax.ShapeDtypeStruct((DEC_BATCH, DEC_SEQ, BRANCH_W), F32),
        scratch_shapes=[pltpu.VMEM((2, DEC_SEQ, cb), F32), pltpu.VMEM((2 * FFT_N, cb), F32)],
        compiler_params=_cparams("arbitrary", "arbitrary"),
        name="hyena_lconv_two_stage",
    )(sig, gate_src, conv_w, conv_b, conv_w, conv_b, spec, bias, m1, m2, m2c, m3)


def kernel(x_prompt, x_sample, cache_na_k, cache_na_v, cache_da_k, cache_da_v, c, c_ctx, w_ada, b_ada, norm_mix,
           norm_ffn, w_in, hy_conv_w, hy_conv_b, hy_filt_w1, hy_filt_b1, hy_filt_w2, hy_filt_b2, hy_filt_w3,
           hy_filt_freq, hy_bias, na_rpb, da_lambda, da_subln, w_lift, w_out, w_ffn_in, w_ffn_out, norm_final):
    TP, TS = BATCH * SEQ, DEC_BATCH * DEC_SEQ
    xp = x_prompt.reshape(TP, D_MODEL)
    xs = x_sample.reshape(TS, D_MODEL)

    cc = jnp.concatenate([c_ctx[None, :], c, jnp.zeros((8 - 1 - DEC_BATCH, D_MODEL), F32)], axis=0)
    mod = _modulation(cc, w_ada, b_ada)
    mod_p = mod[:, 0:1].reshape(DEPTH, 1, 1, 6 * D_MODEL)
    mod_s = mod[:, 1:1 + DEC_BATCH].reshape(DEPTH, DEC_BATCH, 1, 6 * D_MODEL)

    w_mix = w_in[:, :, :MIX_W].astype(BF16)
    w_gate = w_in[:, :, MIX_W:].astype(BF16)
    w_lift_b = w_lift.astype(BF16)
    w_out_b = w_out.astype(BF16)
    w_ffn_in_b = w_ffn_in.astype(BF16)
    w_ffn_out_b = w_ffn_out.astype(BF16)
    g_mix = norm_mix.reshape(DEPTH, 1, D_MODEL)
    g_ffn = norm_ffn.reshape(DEPTH, 1, D_MODEL)
    g_fin = norm_final.reshape(1, D_MODEL)
    subln = da_subln.reshape(DEPTH, 1, DA_V_DIM)
    subln_col = da_subln.reshape(DEPTH, DA_V_DIM, 1)

    w1p = jnp.pad(hy_filt_w1, ((0, 0), (0, HY_FILT_HIDDEN - HY_POS_DIM), (0, 0)))
    b1 = hy_filt_b1.reshape(DEPTH, 1, HY_FILT_HIDDEN)
    b2 = hy_filt_b2.reshape(DEPTH, 1, HY_FILT_HIDDEN)
    fr = hy_filt_freq.reshape(DEPTH, 1, HY_FILT_HIDDEN)
    mf, mi = _dft_direct_mats()
    mats = _dft_two_stage_mats()
    hf_p, hb_p = _hyena_filters(SEQ, w1p, b1, hy_filt_w2, b2, hy_filt_w3, fr)
    hf_s, hb_s = _hyena_filters(DEC_SEQ, w1p, b1, hy_filt_w2, b2, hy_filt_w3, fr)
    spec_p = _spec_direct(hf_p, hb_p, jnp.asarray(mf[:, 0:SEQ]))
    mf_b, mi_b = jnp.asarray(mf, dtype=BF16), jnp.asarray(mi, dtype=BF16)
    mats_b = tuple(jnp.asarray(m, dtype=BF16) for m in mats)
    spec_s = _spec_two_stage(hf_s, hb_s, mats_b[0], mats_b[1])
    conv_b = hy_conv_b.reshape(DEPTH, 1, 3 * BRANCH_W)

    na_bias = _na_bias_table(na_rpb)
    cos, sin = _rope_tables()
    ck_na = cache_na_k.reshape(DEC_BATCH, DEPTH, PAST_LEN, BRANCH_W)
    cv_na = cache_na_v.reshape(DEC_BATCH, DEPTH, PAST_LEN, BRANCH_W)
    ck_da = cache_da_k.reshape(DEC_BATCH, DEPTH, PAST_LEN, BRANCH_W)
    cv_da = cache_da_v.reshape(DEC_BATCH, DEPTH, PAST_LEN, BRANCH_W)

    caches = None
    for l in range(DEPTH):
        lam_init = 0.8 - 0.6 * math.exp(-0.3 * l)
        final = l == DEPTH - 1

        u, caches = _in_proj(xp, g_mix[l], mod_p[l], w_mix[l], TP, F32, layer=l, caches=caches)
        z1 = _lconv_direct(u, 0, u, 1, hy_conv_w[l], conv_b[l], spec_p[l, 0], hy_bias[l, 0:1], mf_b, mi_b, True)
        y_hy = _lconv_direct(z1, 0, u, 2, hy_conv_w[l], conv_b[l], spec_p[l, 1], hy_bias[l, 1:2], mf_b, mi_b, False)
        y_na, y_da = _ctx_attention(u, da_lambda[l], subln_col[l], lam_init)
        xp = _merge_out(xp, g_mix[l], mod_p[l], y_hy, y_na, y_da, w_gate[l], w_lift_b[l], w_out_b[l], TP)
        xp = _ffn(xp, g_ffn[l], mod_p[l], w_ffn_in_b[l], w_ffn_out_b[l], g_fin, TP, final)

        u = _in_proj(xs, g_mix[l], mod_s[l], w_mix[l], DEC_SEQ, BF16)
        u3 = u.reshape(DEC_BATCH, DEC_SEQ, MIX_W)
        z1 = _lconv_two_stage(u3, 0, u3, 1, hy_conv_w[l], conv_b[l], spec_s[l, 0], hy_bias[l, 0:1], mats_b, True)
        y_hy = _lconv_two_stage(z1, 0, u3, 2, hy_conv_w[l], conv_b[l], spec_s[l, 1], hy_bias[l, 1:2], mats_b, False)
        y_hy = y_hy.reshape(TS, BRANCH_W)
        qn, kn, vn = _attn_prep(u, 3, ck_na[:, l], cv_na[:, l], cos, sin, False, NA_HEAD_DIM)
        y_na = _nbr_attention(qn, kn, vn, na_bias[l])
        q, kt, v = _attn_prep(u, 6, ck_da[:, l], cv_da[:, l], cos, sin, True, DA_HEAD_DIM)
        y_da = _diff_attention(q, kt, v, da_lambda[l], subln_col[l], lam_init)
        xs = _merge_out(xs, g_mix[l], mod_s[l], y_hy, y_na, y_da, w_gate[l], w_lift_b[l], w_out_b[l], DEC_SEQ)
        xs = _ffn(xs, g_ffn[l], mod_s[l], w_ffn_in_b[l], w_ffn_out_b[l], g_fin, DEC_SEQ, final)

    y_prompt = xp.reshape(BATCH, SEQ, D_MODEL)
    y_sample = xs.reshape(DEC_BATCH, DEC_SEQ, D_MODEL)
    heads = lambda a, d: a.reshape(BATCH, DEPTH, SEQ, BRANCH_W // d, d)
    return (y_prompt, y_sample, heads(caches[0], NA_HEAD_DIM), heads(caches[1], NA_HEAD_DIM),
            heads(caches[2], 2 * DA_HEAD_DIM), heads(caches[3], DA_V_DIM))
```

```python
import functools
import math

import numpy as np
import jax
import jax.numpy as jnp
from jax import lax
from jax.experimental import pallas as pl
from jax.experimental.pallas import tpu as pltpu

F32 = jnp.float32
BF16 = jnp.bfloat16
HIGHEST = lax.Precision.HIGHEST

D_MODEL = 1024
BATCH = 32
SEQ = 256
DEPTH = 4
DEC_BATCH = 4
DEC_SEQ = 4096
PAST_LEN = 256
GRID_W = 64
GRID_H = DEC_SEQ // GRID_W
BRANCH_W = 512
HY_POS_BANDS = 16
HY_POS_DIM = 1 + 2 * HY_POS_BANDS
HY_FILT_HIDDEN = 64
HY_DECAY_TARGET = 1e-2
HY_FAST_DECAY = 0.3
HY_SLOW_DECAY = 1.5
NA_HEADS = 8
NA_HEAD_DIM = 64
NA_WIN_ROWS = 8
NA_WIN_COLS = 16
DA_HEADS = 8
DA_HEAD_DIM = 32
DA_V_DIM = 64
D_FF = 2816
MIX_W = 9 * BRANCH_W
ROPE_BASE = 10000.0
EPS = 1e-6
NEG_INF = -1e30

VMEM_LIMIT_BYTES = 56 * 1024 * 1024

FFT_N = 2 * DEC_SEQ
FFT_NO = 64
FFT_NI = 128
FFT_HALF = FFT_NO // 2
FFT_UNROLL = 4
LCONV_CB = 128


def _cparams(*sem):
    return pltpu.CompilerParams(dimension_semantics=sem, vmem_limit_bytes=VMEM_LIMIT_BYTES)


def _sigmoid(x):
    return 1.0 / (1.0 + jnp.exp(-x))


def _rms(x, g):
    return x * lax.rsqrt(jnp.mean(x * x, axis=-1, keepdims=True) + EPS) * g


def _modnorm(x, g, shift, scale):
    return _rms(x, g) * (1.0 + scale) + shift


def _bdot(a, b):
    return jnp.dot(a.astype(BF16), b.astype(BF16), preferred_element_type=F32)


def _bdot_nt(a, b):
    return lax.dot_general(a.astype(BF16), b.astype(BF16), (((1,), (1,)), ((), ())),
                           preferred_element_type=F32)


def _mod_kernel(c_ref, w_ref, b_ref, o_ref):
    c = c_ref[...]
    s = c * _sigmoid(c)
    o_ref[0] = jnp.dot(s, w_ref[0], precision=HIGHEST, preferred_element_type=F32) + b_ref[0]


def _modulation(cc, w_ada, b_ada):
    nt = 6
    return pl.pallas_call(
        _mod_kernel,
        grid=(DEPTH, nt),
        in_specs=[
            pl.BlockSpec((8, D_MODEL), lambda l, j: (0, 0)),
            pl.BlockSpec((1, D_MODEL, D_MODEL), lambda l, j: (l, 0, j)),
            pl.BlockSpec((1, 1, D_MODEL), lambda l, j: (l, 0, j)),
        ],
        out_specs=pl.BlockSpec((1, 8, D_MODEL), lambda l, j: (l, 0, j)),
        out_shape=jax.ShapeDtypeStruct((DEPTH, 8, 6 * D_MODEL), F32),
        compiler_params=_cparams("arbitrary", "arbitrary"),
        name="modulation",
    )(cc, w_ada, b_ada.reshape(DEPTH, 1, 6 * D_MODEL))


IN_TM = 1024
IN_TN = 3 * BRANCH_W
CACHE_BLOCKS = (4, 5, 7, 8)


def _in_kernel(*refs, n_alias, cache_tiles):
    x_ref, g_ref, mod_ref, w_ref = refs[:4]
    o_ref = refs[4 + n_alias]
    cache_refs = refs[5 + n_alias:5 + n_alias + len(cache_tiles)]
    h_ref = refs[-1]
    j = pl.program_id(1)

    @pl.when(j == 0)
    def _():
        m = mod_ref[0]
        h = _modnorm(x_ref[...], g_ref[...], m[:, 0:D_MODEL], m[:, D_MODEL:2 * D_MODEL])
        h_ref[...] = h.astype(BF16)

    res = jnp.dot(h_ref[...], w_ref[...], preferred_element_type=F32)
    o_ref[...] = res.astype(o_ref.dtype)
    for (tile, off), c_ref in zip(cache_tiles, cache_refs):
        @pl.when(j == tile)
        def _(c_ref=c_ref, off=off):
            c_ref[...] = res[:, off:off + BRANCH_W].reshape(c_ref.shape)


def _in_proj(x, g, mod, w, l, rows_per_mod, out_dtype, with_caches=False, caches=None):
    layer = l if with_caches else None
    T = x.shape[0]
    tm, tn = IN_TM, IN_TN
    per = rows_per_mod // tm
    in_specs = [
        pl.BlockSpec((tm, D_MODEL), lambda i, j: (i, 0)),
        pl.BlockSpec((1, D_MODEL), lambda i, j: (0, 0)),
        pl.BlockSpec((1, 1, 6 * D_MODEL), lambda i, j: (i // per, 0, 0)),
        pl.BlockSpec((None, D_MODEL, tn), lambda i, j: (l, 0, j)),
    ]
    out_specs = [pl.BlockSpec((tm, tn), lambda i, j: (i, j))]
    out_shape = [jax.ShapeDtypeStruct((T, MIX_W), out_dtype)]
    args = [x, g, mod, w]
    cache_tiles, aliases = (), {}
    if layer is not None:
        seqs = tm // SEQ
        cache_tiles = tuple(divmod(c * BRANCH_W, tn) for c in CACHE_BLOCKS)
        out_specs += [pl.BlockSpec((seqs, 1, SEQ, BRANCH_W), lambda i, j: (i, layer, 0, 0))] * len(CACHE_BLOCKS)
        out_shape += [jax.ShapeDtypeStruct((T // SEQ, DEPTH, SEQ, BRANCH_W), F32)] * len(CACHE_BLOCKS)
        if caches is not None:
            in_specs += [pl.BlockSpec(memory_space=pl.ANY)] * len(caches)
            aliases = {4 + n: 1 + n for n in range(len(caches))}
            args += list(caches)
    n_alias = len(args) - 4
    outs = pl.pallas_call(
        functools.partial(_in_kernel, n_alias=n_alias, cache_tiles=cache_tiles),
        grid=(T // tm, MIX_W // tn),
        in_specs=in_specs,
        out_specs=out_specs,
        out_shape=out_shape,
        input_output_aliases=aliases,
        scratch_shapes=[pltpu.VMEM((tm, D_MODEL), BF16)],
        compiler_params=_cparams("arbitrary", "arbitrary"),
        name="in_proj",
    )(*args)
    return outs[0] if layer is None else (outs[0], tuple(outs[1:]))


def _mid_kernel(x_ref, g_ref, mod_ref, yh_ref, yn_ref, yd_ref, wg_ref, wl_ref, wo_ref, o_ref):
    m = mod_ref[0]
    x = x_ref[...]
    h = _modnorm(x, g_ref[...], m[:, 0:D_MODEL], m[:, D_MODEL:2 * D_MODEL]).astype(BF16)
    merged = None
    for br, y_ref in enumerate((yh_ref, yn_ref, yd_ref)):
        gate = _sigmoid(jnp.dot(h, wg_ref[:, br * D_MODEL:(br + 1) * D_MODEL], preferred_element_type=F32))
        lift = jnp.dot(y_ref[...].astype(BF16), wl_ref[br], preferred_element_type=F32)
        t = gate * lift
        merged = t if merged is None else merged + t
    o_ref[...] = x + m[:, 2 * D_MODEL:3 * D_MODEL] * _bdot(merged, wo_ref[...])


def _merge_out(x, g, mod, y_hy, y_na, y_da, w_gate, w_lift, w_out, l, rows_per_mod):
    T = x.shape[0]
    tm = 512
    per = rows_per_mod // tm
    row = lambda i: (i, 0)
    const2 = lambda i: (0, 0)
    return pl.pallas_call(
        _mid_kernel,
        grid=(T // tm,),
        in_specs=[
            pl.BlockSpec((tm, D_MODEL), row),
            pl.BlockSpec((1, D_MODEL), const2),
            pl.BlockSpec((1, 1, 6 * D_MODEL), lambda i: (i // per, 0, 0)),
            pl.BlockSpec((tm, BRANCH_W), row),
            pl.BlockSpec((tm, BRANCH_W), row),
            pl.BlockSpec((tm, BRANCH_W), row),
            pl.BlockSpec((None, D_MODEL, 3 * D_MODEL), lambda i: (l, 0, 0)),
            pl.BlockSpec((None, 3, BRANCH_W, D_MODEL), lambda i: (l, 0, 0, 0)),
            pl.BlockSpec((None, D_MODEL, D_MODEL), lambda i: (l, 0, 0)),
        ],
        out_specs=pl.BlockSpec((tm, D_MODEL), row),
        out_shape=jax.ShapeDtypeStruct((T, D_MODEL), F32),
        compiler_params=_cparams("arbitrary"),
        name="merge_out",
    )(x, g, mod, y_hy, y_na, y_da, w_gate, w_lift, w_out)


FFN_CHUNK = D_FF // 2


def _ffn_kernel(x_ref, g_ref, mod_ref, w1g_ref, w1u_ref, w2_ref, gf_ref, o_ref, h_ref, acc_ref, *, final):
    k = pl.program_id(1)

    @pl.when(k == 0)
    def _():
        m = mod_ref[0]
        h = _modnorm(x_ref[...], g_ref[...], m[:, 3 * D_MODEL:4 * D_MODEL], m[:, 4 * D_MODEL:5 * D_MODEL])
        h_ref[...] = h.astype(BF16)

    h = h_ref[...]
    a = jnp.dot(h, w1g_ref[...], preferred_element_type=F32)
    b = jnp.dot(h, w1u_ref[...], preferred_element_type=F32)
    part = _bdot(a * _sigmoid(a) * b, w2_ref[...])

    @pl.when(k == 0)
    def _():
        acc_ref[...] = part

    @pl.when(k == 1)
    def _():
        m = mod_ref[0]
        xn = x_ref[...] + m[:, 5 * D_MODEL:6 * D_MODEL] * (acc_ref[...] + part)
        if final:
            xn = _rms(xn, gf_ref[...])
        o_ref[...] = xn


def _ffn(x, g, mod, w_ffn_in, w_ffn_out, g_final, l, rows_per_mod, final):
    T = x.shape[0]
    tm = 512
    per = rows_per_mod // tm
    return pl.pallas_call(
        functools.partial(_ffn_kernel, final=final),
        grid=(T // tm, 2),
        in_specs=[
            pl.BlockSpec((tm, D_MODEL), lambda i, k: (i, 0)),
            pl.BlockSpec((1, D_MODEL), lambda i, k: (0, 0)),
            pl.BlockSpec((1, 1, 6 * D_MODEL), lambda i, k: (i // per, 0, 0)),
            pl.BlockSpec((None, D_MODEL, FFN_CHUNK), lambda i, k: (l, 0, k)),
            pl.BlockSpec((None, D_MODEL, FFN_CHUNK), lambda i, k: (l, 0, 2 + k)),
            pl.BlockSpec((None, FFN_CHUNK, D_MODEL), lambda i, k: (l, k, 0)),
            pl.BlockSpec((1, D_MODEL), lambda i, k: (0, 0)),
        ],
        out_specs=pl.BlockSpec((tm, D_MODEL), lambda i, k: (i, 0)),
        out_shape=jax.ShapeDtypeStruct((T, D_MODEL), F32),
        scratch_shapes=[pltpu.VMEM((tm, D_MODEL), BF16), pltpu.VMEM((tm, D_MODEL), F32)],
        compiler_params=_cparams("arbitrary", "arbitrary"),
        name="ffn",
    )(x, g, mod, w_ffn_in, w_ffn_in, w_ffn_out, g_final)


def _da_lambda(lam_ref, lam_init):
    lp = lam_ref[...]
    a = jnp.sum(lp[0:1] * lp[1:2], axis=1, keepdims=True)
    b = jnp.sum(lp[2:3] * lp[3:4], axis=1, keepdims=True)
    return jnp.exp(a) - jnp.exp(b) + lam_init


def _softmax_rows(s):
    m = jnp.max(s, axis=-1, keepdims=True)
    p = jnp.exp(s - m)
    return p, jnp.sum(p, axis=-1, keepdims=True)


ATT_ONES_ROWS = 16
ATT_TQ = 256
ATT_KEYS = DEC_SEQ + PAST_LEN
LOG2E = math.log2(math.e)


def _masked_q_blocks(qt, d):
    row = lax.broadcasted_iota(jnp.int32, qt.shape, 0)
    zero = jnp.zeros_like(qt)
    return jnp.concatenate([jnp.where((row >= j * d) & (row < (j + 1) * d), qt, zero) for j in range(128 // d)], axis=1)


def _colmax(st):
    keys, n = st.shape
    return jnp.max(jnp.max(st.reshape(keys // 256, 256, n), axis=0), axis=0, keepdims=True)


def _ctx_attn_kernel(nq_ref, nk_ref, nv_ref, dq_ref, dk_ref, dv_ref, lam_ref, sub_ref, yn_ref, yd_ref, acc_ref,
                     *, lam_init):
    lam = _da_lambda(lam_ref, lam_init)
    ones = jnp.ones((ATT_ONES_ROWS, SEQ), BF16)

    def attend(q_ref, k_ref, v_ref, d, maps_per_head, finish):
        qt = (q_ref[...] * (d ** -0.5 * LOG2E)).T.astype(BF16)
        vt = v_ref[...].T.astype(BF16)
        kb = k_ref[...].astype(BF16)
        dv = NA_HEAD_DIM
        heads_per_group = 128 // (d * maps_per_head)
        w = maps_per_head * SEQ
        for g in range(BRANCH_W // 128):
            st = jnp.dot(kb[:, g * 128:(g + 1) * 128], _masked_q_blocks(qt[g * 128:(g + 1) * 128], d),
                         preferred_element_type=F32)
            pt = jnp.exp2(st - _colmax(st)).astype(BF16)
            for j in range(heads_per_group):
                h = g * heads_per_group + j
                ve = jnp.concatenate([vt[h * dv:(h + 1) * dv], ones], axis=0)
                oe = jnp.dot(ve, pt[:, j * w:(j + 1) * w], preferred_element_type=F32)
                os = [oe[0:dv, i * SEQ:(i + 1) * SEQ] / oe[dv:dv + 1, i * SEQ:(i + 1) * SEQ]
                      for i in range(maps_per_head)]
                acc_ref[h * dv:(h + 1) * dv, :] = finish(os)

    attend(nq_ref, nk_ref, nv_ref, NA_HEAD_DIM, 1, lambda os: os[0])
    yn_ref[...] = acc_ref[...].T

    def da_finish(os):
        ot = os[0] - lam * os[1]
        ot = ot * lax.rsqrt(jnp.mean(ot * ot, axis=0, keepdims=True) + EPS) * sub_ref[...]
        return ot * (1.0 - lam_init)

    attend(dq_ref, dk_ref, dv_ref, DA_HEAD_DIM, 2, da_finish)
    yd_ref[...] = acc_ref[...].T


def _ctx_attention(u, da_lambda, subln_col, lam_init):
    col = lambda j: pl.BlockSpec((SEQ, BRANCH_W), lambda b, j=j: (b, j))
    out = pl.BlockSpec((SEQ, BRANCH_W), lambda b: (b, 0))
    shape = jax.ShapeDtypeStruct((BATCH * SEQ, BRANCH_W), F32)
    return pl.pallas_call(
        functools.partial(_ctx_attn_kernel, lam_init=lam_init),
        grid=(BATCH,),
        in_specs=[col(3), col(4), col(5), col(6), col(7), col(8),
                  pl.BlockSpec((4, DA_HEAD_DIM), lambda b: (0, 0)),
                  pl.BlockSpec((DA_V_DIM, 1), lambda b: (0, 0))],
        out_specs=[out, out],
        out_shape=[shape, shape],
        scratch_shapes=[pltpu.VMEM((BRANCH_W, SEQ), F32)],
        compiler_params=_cparams("arbitrary"),
        name="ctx_attention",
    )(u, u, u, u, u, u, da_lambda, subln_col)


def _rope(x, cos, sin_signed):
    n = x.shape[-1]
    lane = lax.broadcasted_iota(jnp.int32, x.shape, 1)
    partner = jnp.where(lane % 2 == 0, pltpu.roll(x, n - 1, axis=1), pltpu.roll(x, 1, axis=1))
    return x * cos + partner * sin_signed


def _attn_prep_kernel(q_ref, k_ref, v_ref, kc_ref, vc_ref, *refs, rope, scale):
    cos_ref, sin_ref = refs[:2] if rope else (None, None)
    qt_ref, ko_ref, vt_ref = refs[-3:]
    t = pl.program_id(1)
    dv = NA_HEAD_DIM

    def put_v(v):
        vt = v.astype(F32).T.astype(BF16)
        ones = jnp.ones((ATT_ONES_ROWS, ATT_TQ), BF16)
        for h in range(BRANCH_W // dv):
            vt_ref[0, h, 0:dv, :] = vt[h * dv:(h + 1) * dv]
            vt_ref[0, h, dv:dv + ATT_ONES_ROWS, :] = ones

    @pl.when(t < DEC_SEQ // ATT_TQ)
    def _():
        q = q_ref[...].astype(F32)
        k = k_ref[...].astype(F32)
        if rope:
            q = _rope(q, cos_ref[...], sin_ref[...])
            k = _rope(k, cos_ref[...], sin_ref[...])
        qt_ref[0] = (q * scale).T.astype(BF16)
        ko_ref[0] = k.astype(BF16)
        put_v(v_ref[...])

    @pl.when(t == DEC_SEQ // ATT_TQ)
    def _():
        ko_ref[0] = kc_ref[0].astype(BF16)
        put_v(vc_ref[0])


def _attn_prep(u, first_col, k_ctx, v_ctx, head_dim, rope_tables=None):
    rope = rope_tables is not None
    nt = DEC_SEQ // ATT_TQ
    last = nt - 1
    rowblk = lambda j: pl.BlockSpec((ATT_TQ, BRANCH_W), lambda b, t, j=j: (b * nt + jnp.minimum(t, last), j))
    tab = pl.BlockSpec((ATT_TQ, BRANCH_W), lambda b, t: (jnp.minimum(t, last), 0))
    ctx = pl.BlockSpec((1, PAST_LEN, BRANCH_W), lambda b, t: (b, 0, 0))
    heads = BRANCH_W // NA_HEAD_DIM
    vrows = NA_HEAD_DIM + ATT_ONES_ROWS
    return pl.pallas_call(
        functools.partial(_attn_prep_kernel, rope=rope, scale=head_dim ** -0.5 * LOG2E),
        grid=(DEC_BATCH, nt + 1),
        in_specs=[rowblk(first_col), rowblk(first_col + 1), rowblk(first_col + 2), ctx, ctx] + [tab, tab] * rope,
        out_specs=[
            pl.BlockSpec((1, BRANCH_W, ATT_TQ), lambda b, t: (b, 0, jnp.minimum(t, last))),
            pl.BlockSpec((1, ATT_TQ, BRANCH_W), lambda b, t: (b, t, 0)),
            pl.BlockSpec((1, heads, vrows, ATT_TQ), lambda b, t: (b, 0, 0, t)),
        ],
        out_shape=[
            jax.ShapeDtypeStruct((DEC_BATCH, BRANCH_W, DEC_SEQ), BF16),
            jax.ShapeDtypeStruct((DEC_BATCH, ATT_KEYS, BRANCH_W), BF16),
            jax.ShapeDtypeStruct((DEC_BATCH, heads, vrows, ATT_KEYS), BF16),
        ],
        compiler_params=_cparams("arbitrary", "arbitrary"),
        name="attn_prep",
    )(u, u, u, k_ctx, v_ctx, *(rope_tables or ()))


NA_ROWS = ATT_TQ // GRID_W
NA_UNION = 3 * NA_ROWS
NA_STEPS = GRID_H // NA_ROWS
NA_SLABS = NA_UNION // NA_ROWS
NA_VARIANT_OFFSET = (0, -NA_ROWS, -2 * NA_ROWS)


def _na_variant(s):
    return jnp.minimum(s, 1) + s // (NA_STEPS - 1)


def _na_window_block(s):
    return jnp.clip(s - 1, 0, NA_STEPS - NA_SLABS)


def _na_bias_kernel(rpb_ref, o_ref):
    kc = lax.broadcasted_iota(jnp.int32, (GRID_W, GRID_W), 0)
    qc = lax.broadcasted_iota(jnp.int32, (GRID_W, GRID_W), 1)
    dc = jnp.clip(kc - qc, -(NA_WIN_COLS - 1), NA_WIN_COLS - 1) + (NA_WIN_COLS - 1)
    c0 = jnp.clip(qc - NA_WIN_COLS // 2, 0, GRID_W - NA_WIN_COLS)
    col_ok = (kc >= c0) & (kc < c0 + NA_WIN_COLS)
    r = rpb_ref[0, 0] * LOG2E
    masked = jnp.full((GRID_W, GRID_W), NEG_INF, F32)
    tiles = []
    for dr in range(2 * NA_WIN_ROWS - 1):
        acc = jnp.zeros((GRID_W, GRID_W), F32)
        for d in range(2 * NA_WIN_COLS - 1):
            acc = jnp.where(dc == d, r[dr:dr + 1, d:d + 1], acc)
        tiles.append(jnp.where(col_ok, acc, masked))
    for v, off in enumerate(NA_VARIANT_OFFSET):
        for kr in range(NA_UNION):
            for rr in range(NA_ROWS):
                w0 = (0, rr, NA_UNION - NA_WIN_ROWS)[v]
                dr = kr + off - rr
                inside = w0 <= kr < w0 + NA_WIN_ROWS
                o_ref[0, v, 0, kr * GRID_W:(kr + 1) * GRID_W, rr * GRID_W:(rr + 1) * GRID_W] = (
                    tiles[dr + NA_WIN_ROWS - 1] if inside else masked)


def _na_bias_table(na_rpb):
    n_dr, n_dc = 2 * NA_WIN_ROWS - 1, 2 * NA_WIN_COLS - 1
    nv = len(NA_VARIANT_OFFSET)
    return pl.pallas_call(
        _na_bias_kernel,
        grid=(DEPTH, NA_HEADS),
        in_specs=[pl.BlockSpec((1, 1, n_dr, n_dc), lambda l, h: (l, h, 0, 0))],
        out_specs=pl.BlockSpec((1, nv, 1, NA_UNION * GRID_W, ATT_TQ), lambda l, h: (l, 0, h, 0, 0)),
        out_shape=jax.ShapeDtypeStruct((DEPTH, nv, NA_HEADS, NA_UNION * GRID_W, ATT_TQ), F32),
        compiler_params=_cparams("arbitrary", "arbitrary"),
        name="na_bias_table",
    )(na_rpb)


def _na_kernel(qt_ref, *refs):
    n = NA_SLABS + 1
    k_refs, vt_refs = refs[:n], refs[n:2 * n]
    bias_ref, o_ref, acc_ref = refs[2 * n:]
    dv = NA_HEAD_DIM
    heads_per_group = 128 // dv
    for g in range(BRANCH_W // 128):
        lanes = slice(g * 128, (g + 1) * 128)
        qbd = _masked_q_blocks(qt_ref[0, lanes, :], dv)
        sts = []
        for j, k_ref in enumerate(k_refs):
            st = jnp.dot(k_ref[0, :, lanes], qbd, preferred_element_type=F32)
            if j < NA_SLABS:
                rows = slice(j * ATT_TQ, (j + 1) * ATT_TQ)
                st = st + jnp.concatenate(
                    [bias_ref[0, g * heads_per_group + hh, rows, :] for hh in range(heads_per_group)], axis=1)
            sts.append(st)
        mx = functools.reduce(jnp.maximum, [_colmax(st) for st in sts])
        pts = [jnp.exp2(st - mx).astype(BF16) for st in sts]
        for hh in range(heads_per_group):
            h = g * heads_per_group + hh
            oe = sum(jnp.dot(vt_ref[0, h], pt[:, hh * ATT_TQ:(hh + 1) * ATT_TQ], preferred_element_type=F32)
                     for vt_ref, pt in zip(vt_refs, pts))
            acc_ref[h * dv:(h + 1) * dv, :] = oe[0:dv] / oe[dv:dv + 1]
    o_ref[...] = acc_ref[...].T


def _nbr_attention(qt, k, vt, bias, l):
    vrows = NA_HEAD_DIM + ATT_ONES_ROWS
    ctx_blk = DEC_SEQ // ATT_TQ
    k_specs = [pl.BlockSpec((1, ATT_TQ, BRANCH_W), lambda b, s, j=j: (b, _na_window_block(s) + j, 0))
               for j in range(NA_SLABS)]
    k_specs.append(pl.BlockSpec((1, ATT_TQ, BRANCH_W), lambda b, s: (b, ctx_blk, 0)))
    vt_specs = [pl.BlockSpec((1, NA_HEADS, vrows, ATT_TQ), lambda b, s, j=j: (b, 0, 0, _na_window_block(s) + j))
                for j in range(NA_SLABS)]
    vt_specs.append(pl.BlockSpec((1, NA_HEADS, vrows, ATT_TQ), lambda b, s: (b, 0, 0, ctx_blk)))
    n = NA_SLABS + 1
    return pl.pallas_call(
        _na_kernel,
        grid=(DEC_BATCH, NA_STEPS),
        in_specs=[pl.BlockSpec((1, BRANCH_W, ATT_TQ), lambda b, s: (b, 0, s))] + k_specs + vt_specs + [
            pl.BlockSpec((None, 1, NA_HEADS, NA_UNION * GRID_W, ATT_TQ), lambda b, s: (l, _na_variant(s), 0, 0, 0))],
        out_specs=pl.BlockSpec((ATT_TQ, BRANCH_W), lambda b, s: (b * NA_STEPS + s, 0)),
        out_shape=jax.ShapeDtypeStruct((DEC_BATCH * DEC_SEQ, BRANCH_W), F32),
        scratch_shapes=[pltpu.VMEM((BRANCH_W, ATT_TQ), F32)],
        compiler_params=_cparams("arbitrary", "arbitrary"),
        name="nbr_attention",
    )(qt, *([k] * n), *([vt] * n), bias)


DA_TQ = ATT_TQ
DA_KEYS = ATT_KEYS
DA_ONES_ROWS = ATT_ONES_ROWS
DA_MAPS_PER_TILE = 128 // DA_HEAD_DIM


def _da_kernel(qt_ref, k_ref, vt_ref, lam_ref, sub_ref, o_ref, acc_ref, *, lam_init):
    lam = _da_lambda(lam_ref, lam_init)
    st = jnp.dot(k_ref[0], _masked_q_blocks(qt_ref[0], DA_HEAD_DIM), preferred_element_type=F32)
    pt = jnp.exp2(st - _colmax(st)).astype(BF16)
    heads = DA_MAPS_PER_TILE // 2
    for h in range(heads):
        oe = jnp.dot(vt_ref[0, h], pt[:, 2 * h * DA_TQ:(2 * h + 2) * DA_TQ], preferred_element_type=F32)
        os = [oe[0:DA_V_DIM, i * DA_TQ:(i + 1) * DA_TQ] / oe[DA_V_DIM:DA_V_DIM + 1, i * DA_TQ:(i + 1) * DA_TQ]
              for i in range(2)]
        ot = os[0] - lam * os[1]
        ot = ot * lax.rsqrt(jnp.mean(ot * ot, axis=0, keepdims=True) + EPS) * sub_ref[...]
        acc_ref[h * DA_V_DIM:(h + 1) * DA_V_DIM, :] = ot * (1.0 - lam_init)
    o_ref[...] = acc_ref[...].T


def _diff_attention(qt, k, vt, da_lambda, subln_col, lam_init):
    nt = DEC_SEQ // DA_TQ
    vrows = DA_V_DIM + DA_ONES_ROWS
    groups = BRANCH_W // 128
    heads = DA_MAPS_PER_TILE // 2
    return pl.pallas_call(
        functools.partial(_da_kernel, lam_init=lam_init),
        grid=(DEC_BATCH, groups, nt),
        in_specs=[
            pl.BlockSpec((1, 128, DA_TQ), lambda b, g, t: (b, g, t)),
            pl.BlockSpec((1, DA_KEYS, 128), lambda b, g, t: (b, 0, g)),
            pl.BlockSpec((1, heads, vrows, DA_KEYS), lambda b, g, t: (b, g, 0, 0)),
            pl.BlockSpec((4, DA_HEAD_DIM), lambda b, g, t: (0, 0)),
            pl.BlockSpec((DA_V_DIM, 1), lambda b, g, t: (0, 0)),
        ],
        out_specs=pl.BlockSpec((DA_TQ, 128), lambda b, g, t: (b * nt + t, g)),
        out_shape=jax.ShapeDtypeStruct((DEC_BATCH * DEC_SEQ, BRANCH_W), F32),
        scratch_shapes=[pltpu.VMEM((128, DA_TQ), F32)],
        compiler_params=_cparams("arbitrary", "arbitrary", "arbitrary"),
        name="diff_attention",
    )(qt, k, vt, da_lambda, subln_col)


def _rope_tables():
    pos = np.arange(DEC_SEQ)
    row = (pos // GRID_W).astype(np.float32)
    col = (pos % GRID_W).astype(np.float32)
    n_freq = DA_HEAD_DIM // 4
    inv = (np.float32(ROPE_BASE) ** (-np.arange(n_freq, dtype=np.float32) / n_freq)).astype(np.float32)
    ang = np.concatenate([row[:, None] * inv[None, :], col[:, None] * inv[None, :]], axis=-1)
    ang = ang.astype(np.float64)
    cos = np.repeat(np.cos(ang), 2, axis=-1)
    sin = np.repeat(np.sin(ang), 2, axis=-1)
    sign = np.where(np.arange(DA_HEAD_DIM) % 2 == 0, -1.0, 1.0)
    reps = BRANCH_W // DA_HEAD_DIM
    cos = np.tile(cos, (1, reps)).astype(np.float32)
    sin = np.tile(sin * sign[None, :], (1, reps)).astype(np.float32)
    return jnp.asarray(cos), jnp.asarray(sin)


def _filt_hidden_kernel(feat_ref, w1_ref, b1_ref, w2_ref, b2_ref, fr_ref, o_ref):
    fr = fr_ref[0]
    h = jnp.sin(fr * (jnp.dot(feat_ref[...], w1_ref[0], precision=HIGHEST, preferred_element_type=F32) + b1_ref[0]))
    o_ref[0] = jnp.sin(fr * (jnp.dot(h, w2_ref[0], precision=HIGHEST, preferred_element_type=F32) + b2_ref[0]))


def _filt_kernel(h_ref, w3f_ref, w3b_ref, dec_ref, o_ref):
    L = dec_ref.shape[0] // 2
    hf = jnp.dot(h_ref[0, 0:L], w3f_ref[0], precision=HIGHEST, preferred_element_type=F32) * dec_ref[0:L]
    hb = jnp.dot(h_ref[0, L:2 * L], w3b_ref[0], precision=HIGHEST, preferred_element_type=F32) * dec_ref[L:2 * L]
    row = lax.broadcasted_iota(jnp.int32, hb.shape, 0)
    hb = jnp.where(row == 0, 0.0, hb)
    nrm = jnp.sum(jnp.abs(hf), axis=0, keepdims=True) + jnp.sum(jnp.abs(hb), axis=0, keepdims=True)
    o_ref[0, 0, 0:L] = hf / nrm
    o_ref[0, 0, L:2 * L] = hb / nrm


def _circular_order(a):
    return np.concatenate([a, a[:1], a[1:][::-1]], axis=0)


def _hyena_pos_tables(L):
    f32 = np.float32
    pos = np.arange(L, dtype=f32)
    t = (pos / f32(L)).astype(f32)
    bands = np.linspace(1e-4, HY_POS_BANDS - 1, HY_POS_BANDS, dtype=f32)
    ang = (f32(2 * math.pi / L) * pos[:, None] * bands[None, :]).astype(np.float64)
    feats = np.zeros((L, HY_FILT_HIDDEN), f32)
    feats[:, 0] = t
    feats[:, 1:1 + HY_POS_BANDS] = np.cos(ang)
    feats[:, 1 + HY_POS_BANDS:HY_POS_DIM] = -np.sin(ang)
    deltas = np.linspace(math.log(HY_DECAY_TARGET) / HY_SLOW_DECAY,
                         math.log(HY_DECAY_TARGET) / HY_FAST_DECAY, BRANCH_W, dtype=f32)
    decay = np.exp((-t[:, None] * np.abs(deltas)[None, :]).astype(np.float64)).astype(f32)
    return jnp.asarray(_circular_order(feats)), jnp.asarray(_circular_order(decay))


def _hyena_filters(half, w1p, b1, w2, b2, w3, freq):
    feats, decay = _hyena_pos_tables(half)
    L = 2 * half
    cb = 128
    ncb = BRANCH_W // cb
    small = lambda shape: pl.BlockSpec((1,) + shape, lambda l: (l, 0, 0))
    hidden = pl.pallas_call(
        _filt_hidden_kernel,
        grid=(DEPTH,),
        in_specs=[
            pl.BlockSpec((L, HY_FILT_HIDDEN), lambda l: (0, 0)),
            small((HY_FILT_HIDDEN, HY_FILT_HIDDEN)), small((1, HY_FILT_HIDDEN)),
            small((HY_FILT_HIDDEN, HY_FILT_HIDDEN)), small((1, HY_FILT_HIDDEN)),
            small((1, HY_FILT_HIDDEN)),
        ],
        out_specs=pl.BlockSpec((1, L, HY_FILT_HIDDEN), lambda l: (l, 0, 0)),
        out_shape=jax.ShapeDtypeStruct((DEPTH, L, HY_FILT_HIDDEN), F32),
        compiler_params=_cparams("arbitrary"),
        name=f"hyena_filter_hidden_{L}",
    )(feats, w1p, b1, w2, b2, freq)
    return pl.pallas_call(
        _filt_kernel,
        grid=(DEPTH, 2, ncb),
        in_specs=[
            pl.BlockSpec((1, L, HY_FILT_HIDDEN), lambda l, o, c: (l, 0, 0)),
            pl.BlockSpec((1, HY_FILT_HIDDEN, cb), lambda l, o, c: (l, 0, o * 2 * ncb + c)),
            pl.BlockSpec((1, HY_FILT_HIDDEN, cb), lambda l, o, c: (l, 0, o * 2 * ncb + ncb + c)),
            pl.BlockSpec((L, cb), lambda l, o, c: (0, c)),
        ],
        out_specs=pl.BlockSpec((1, 1, L, cb), lambda l, o, c: (l, o, 0, c)),
        out_shape=jax.ShapeDtypeStruct((DEPTH, 2, L, BRANCH_W), F32),
        compiler_params=_cparams("arbitrary", "arbitrary", "arbitrary"),
        name=f"hyena_filters_{L}",
    )(hidden, w3, w3, decay)


def _short_conv(u, w_ref, b_ref, seq_len):
    n = u.shape[0]
    t = lax.broadcasted_iota(jnp.int32, u.shape, 0) % seq_len
    prev = jnp.where(t == 0, 0.0, pltpu.roll(u, 1, axis=0))
    nxt = jnp.where(t == seq_len - 1, 0.0, pltpu.roll(u, n - 1, axis=0))
    return prev * w_ref[0:1, :] + u * w_ref[1:2, :] + nxt * w_ref[2:3, :] + b_ref[...]


def _dft_direct_mats():
    n, half = 2 * SEQ, SEQ
    k = np.arange(n)[:, None].astype(np.float64)
    t = np.arange(half)[None, :].astype(np.float64)
    ang = 2 * np.pi * k * t / n
    fr, fi = np.cos(ang), -np.sin(ang)
    mf = np.block([[fr, -fi], [fi, fr]])
    gr, gi = np.cos(ang).T / n, np.sin(ang).T / n
    mi = np.block([[gr, -gi], [gi, gr]])
    return mf.astype(np.float32), mi.astype(np.float32)


def _dft_real_mat():
    n = 2 * SEQ
    ang = 2 * np.pi * np.arange(n)[:, None].astype(np.float64) * np.arange(n)[None, :] / n
    return np.concatenate([np.cos(ang), -np.sin(ang)], axis=0).astype(np.float32)


def _spec_direct_kernel(h_ref, m_ref, o_ref):
    o_ref[0, 0] = jnp.dot(m_ref[...], h_ref[0, 0], precision=HIGHEST, preferred_element_type=F32)


def _spec_direct(h, m_real):
    n = 2 * SEQ
    return pl.pallas_call(
        _spec_direct_kernel,
        grid=(DEPTH, 2),
        in_specs=[pl.BlockSpec((1, 1, n, BRANCH_W), lambda l, o: (l, o, 0, 0)),
                  pl.BlockSpec((2 * n, n), lambda l, o: (0, 0))],
        out_specs=pl.BlockSpec((1, 1, 2 * n, BRANCH_W), lambda l, o: (l, o, 0, 0)),
        out_shape=jax.ShapeDtypeStruct((DEPTH, 2, 2 * n, BRANCH_W), F32),
        compiler_params=_cparams("arbitrary", "arbitrary"),
        name="hyena_spectrum_direct",
    )(h, m_real)


def _lconv_direct_kernel(s_ref, g_ref, cws_ref, cbs_ref, cwg_ref, cbg_ref, h_ref, bias_ref, mf_ref, mi_ref, o_ref,
                         *, conv_sig):
    n = 2 * SEQ
    sig = s_ref[...]
    if conv_sig:
        sig = _short_conv(sig, cws_ref, cbs_ref, SEQ)
    gate = _short_conv(g_ref[...], cwg_ref, cbg_ref, SEQ)
    z = jnp.dot(mf_ref[...], sig.astype(BF16), preferred_element_type=F32)
    zr, zi = z[0:n], z[n:2 * n]
    hr, hi = h_ref[0:n], h_ref[n:2 * n]
    y = jnp.concatenate([zr * hr - zi * hi, zr * hi + zi * hr], axis=0)
    y = jnp.dot(mi_ref[...], y.astype(BF16), preferred_element_type=F32)
    o_ref[...] = gate * (y + sig * bias_ref[...])


def _lconv_direct(sig, sig_col, gate_src, gate_col, conv_w, conv_b, spec, l, order, bias, mf, mi, conv_sig):
    n = 2 * SEQ
    rows = 2 * SEQ
    T = sig.shape[0]
    return pl.pallas_call(
        functools.partial(_lconv_direct_kernel, conv_sig=conv_sig),
        grid=(T // rows,),
        in_specs=[
            pl.BlockSpec((rows, BRANCH_W), lambda p: (p, sig_col)),
            pl.BlockSpec((rows, BRANCH_W), lambda p: (p, gate_col)),
            pl.BlockSpec((3, BRANCH_W), lambda p: (0, 0)),
            pl.BlockSpec((1, BRANCH_W), lambda p: (0, 0)),
            pl.BlockSpec((3, BRANCH_W), lambda p: (0, gate_col)),
            pl.BlockSpec((1, BRANCH_W), lambda p: (0, gate_col)),
            pl.BlockSpec((None, None, 2 * n, BRANCH_W), lambda p: (l, order, 0, 0)),
            pl.BlockSpec((1, BRANCH_W), lambda p: (0, 0)),
            pl.BlockSpec((2 * n, rows), lambda p: (0, 0)),
            pl.BlockSpec((rows, 2 * n), lambda p: (0, 0)),
        ],
        out_specs=pl.BlockSpec((rows, BRANCH_W), lambda p: (p, 0)),
        out_shape=jax.ShapeDtypeStruct((T, BRANCH_W), F32),
        compiler_params=_cparams("arbitrary"),
        name="hyena_lconv_direct",
    )(sig, gate_src, conv_w, conv_b, conv_w, conv_b, spec, bias, mf, mi)


def _dft_two_stage_mats():
    no, ni, half, n = FFT_NO, FFT_NI, FFT_HALF, FFT_N
    f64 = np.float64
    k1 = np.arange(no, dtype=f64)
    n_o = np.arange(half, dtype=f64)
    n_i = np.arange(ni, dtype=f64)
    ang = 2 * np.pi * (n_i[:, None, None] * k1[None, :, None] / n + k1[None, :, None] * n_o[None, None, :] / no)
    tr, ti = np.cos(ang), -np.sin(ang)
    m1 = np.concatenate([np.concatenate([tr, -ti], axis=2), np.concatenate([ti, tr], axis=2)], axis=1)
    k2 = np.arange(ni, dtype=f64)
    ang2 = 2 * np.pi * k2[:, None] * n_i[None, :] / ni
    f2r, f2i = np.cos(ang2), -np.sin(ang2)
    m2 = np.block([[f2r, -f2i], [f2i, f2r]])
    m2c = np.block([[f2r, f2i], [-f2i, f2r]])
    sr, si = np.transpose(tr, (0, 2, 1)) / n, -np.transpose(ti, (0, 2, 1)) / n
    m3 = np.concatenate([np.concatenate([sr, -si], axis=2), np.concatenate([si, sr], axis=2)], axis=1)
    return (m1.astype(np.float32), m2.astype(np.float32), m2c.astype(np.float32), m3.astype(np.float32))


def _dft_stage1_real_mat():
    no, ni, n = FFT_NO, FFT_NI, FFT_N
    k1 = np.arange(no, dtype=np.float64)
    n_o = np.arange(no, dtype=np.float64)
    n_i = np.arange(ni, dtype=np.float64)
    ang = 2 * np.pi * (n_i[:, None, None] * k1[None, :, None] / n + k1[None, :, None] * n_o[None, None, :] / no)
    return np.concatenate([np.cos(ang), -np.sin(ang)], axis=1).astype(np.float32)


def _store_stage1(w_ref, ni, out):
    w_ref[pl.ds(ni, FFT_NO, stride=2 * FFT_NI), :] = out[0:FFT_NO]
    w_ref[pl.ds(FFT_NI + ni, FFT_NO, stride=2 * FFT_NI), :] = out[FFT_NO:2 * FFT_NO]


def _fwd_stage1(za_ref, zb_ref, m1_ref, w_ref):
    def body(ni, carry):
        a = za_ref[pl.ds(ni, FFT_HALF, stride=FFT_NI), :]
        b = zb_ref[pl.ds(ni, FFT_HALF, stride=FFT_NI), :]
        out = jnp.dot(m1_ref[ni], jnp.concatenate([a, b], axis=0).astype(BF16), preferred_element_type=F32)
        _store_stage1(w_ref, ni, out)
        return carry

    lax.fori_loop(0, FFT_NI, body, 0, unroll=FFT_UNROLL)


def _spec_two_stage_kernel(h_ref, m1_ref, m2_ref, o_ref, w_ref):
    h = h_ref.at[0, 0]

    def stage1(ni, carry):
        a = h[pl.ds(ni, FFT_NO, stride=FFT_NI), :]
        _store_stage1(w_ref, ni, jnp.dot(m1_ref[ni], a.astype(BF16), preferred_element_type=F32))
        return carry

    lax.fori_loop(0, FFT_NI, stage1, 0, unroll=FFT_UNROLL)
    blk = 2 * FFT_NI

    def stage2(k1, carry):
        rows = pl.ds(pl.multiple_of(k1 * blk, blk), blk)
        o_ref[0, 0, rows, :] = jnp.dot(m2_ref[...], w_ref[rows, :].astype(BF16), preferred_element_type=F32)
        return carry

    lax.fori_loop(0, FFT_NO, stage2, 0, unroll=FFT_UNROLL)


def _spec_two_stage(h, m1_real, m2):
    cb = LCONV_CB
    return pl.pallas_call(
        _spec_two_stage_kernel,
        grid=(DEPTH, 2, BRANCH_W // cb),
        in_specs=[pl.BlockSpec((1, 1, FFT_N, cb), lambda l, o, c: (l, o, 0, c)),
                  pl.BlockSpec((FFT_NI, 2 * FFT_NO, FFT_NO), lambda l, o, c: (0, 0, 0)),
                  pl.BlockSpec((2 * FFT_NI, 2 * FFT_NI), lambda l, o, c: (0, 0))],
        out_specs=pl.BlockSpec((1, 1, 2 * FFT_N, cb), lambda l, o, c: (l, o, 0, c)),
        out_shape=jax.ShapeDtypeStruct((DEPTH, 2, 2 * FFT_N, BRANCH_W), F32),
        scratch_shapes=[pltpu.VMEM((2 * FFT_N, cb), F32)],
        compiler_params=_cparams("arbitrary", "arbitrary", "arbitrary"),
        name="hyena_spectrum_two_stage",
    )(h, m1_real, m2)


def _lconv_two_stage_kernel(s_ref, g_ref, cws_ref, cbs_ref, cwg_ref, cbg_ref, h_ref, bias_ref,
                            m1_ref, m2_ref, m2c_ref, m3_ref, o_ref, z_ref, w_ref, *, conv_sig):
    for b in range(2):
        sig = s_ref[b].astype(F32)
        if conv_sig:
            sig = _short_conv(sig, cws_ref, cbs_ref, DEC_SEQ)
        z_ref[b] = sig
    _fwd_stage1(z_ref.at[0], z_ref.at[1], m1_ref, w_ref)
    blk = 2 * FFT_NI

    def mid(k1, carry):
        rows = pl.ds(pl.multiple_of(k1 * blk, blk), blk)
        x = jnp.dot(m2_ref[...], w_ref[rows, :].astype(BF16), preferred_element_type=F32)
        h = h_ref[rows, :]
        xr, xi = x[0:FFT_NI], x[FFT_NI:blk]
        hr, hi = h[0:FFT_NI], h[FFT_NI:blk]
        y = jnp.concatenate([xr * hr - xi * hi, xr * hi + xi * hr], axis=0)
        w_ref[rows, :] = jnp.dot(m2c_ref[...], y.astype(BF16), preferred_element_type=F32)
        return carry

    lax.fori_loop(0, FFT_NO, mid, 0, unroll=FFT_UNROLL)

    def last(ni, carry):
        cr = w_ref[pl.ds(ni, FFT_NO, stride=blk), :]
        ci = w_ref[pl.ds(FFT_NI + ni, FFT_NO, stride=blk), :]
        y = jnp.dot(m3_ref[ni], jnp.concatenate([cr, ci], axis=0).astype(BF16), preferred_element_type=F32)
        o_ref[0, pl.ds(ni, FFT_HALF, stride=FFT_NI), :] = y[0:FFT_HALF]
        o_ref[1, pl.ds(ni, FFT_HALF, stride=FFT_NI), :] = y[FFT_HALF:2 * FFT_HALF]
        return carry

    lax.fori_loop(0, FFT_NI, last, 0, unroll=FFT_UNROLL)
    for b in range(2):
        gate = _short_conv(g_ref[b].astype(F32), cwg_ref, cbg_ref, DEC_SEQ)
        sig = z_ref[b]
        o_ref[b] = gate * (o_ref[b] + sig * bias_ref[...])


def _lconv_two_stage(sig, sig_col, gate_src, gate_col, conv_w, conv_b, spec, l, order, bias, mats, conv_sig):
    cb = LCONV_CB
    ncb = BRANCH_W // cb
    m1, m2, m2c, m3 = mats
    const3 = lambda c, p: (0, 0, 0)
    const2 = lambda c, p: (0, 0)
    return pl.pallas_call(
        functools.partial(_lconv_two_stage_kernel, conv_sig=conv_sig),
        grid=(ncb, DEC_BATCH // 2),
        in_specs=[
            pl.BlockSpec((2, DEC_SEQ, cb), lambda c, p: (p, 0, sig_col * ncb + c)),
            pl.BlockSpec((2, DEC_SEQ, cb), lambda c, p: (p, 0, gate_col * ncb + c)),
            pl.BlockSpec((3, cb), lambda c, p: (0, c)),
            pl.BlockSpec((1, cb), lambda c, p: (0, c)),
            pl.BlockSpec((3, cb), lambda c, p: (0, gate_col * ncb + c)),
            pl.BlockSpec((1, cb), lambda c, p: (0, gate_col * ncb + c)),
            pl.BlockSpec((None, None, 2 * FFT_N, cb), lambda c, p: (l, order, 0, c)),
            pl.BlockSpec((1, cb), lambda c, p: (0, c)),
            pl.BlockSpec(m1.shape, const3),
            pl.BlockSpec(m2.shape, const2),
            pl.BlockSpec(m2c.shape, const2),
            pl.BlockSpec(m3.shape, const3),
        ],
        out_specs=pl.BlockSpec((2, DEC_SEQ, cb), lambda c, p: (p, 0, c)),
        out_shape=jax.ShapeDtypeStruct((DEC_BATCH, DEC_SEQ, BRANCH_W), F32),
        scratch_shapes=[pltpu.VMEM((2, DEC_SEQ, cb), F32), pltpu.VMEM((2 * FFT_N, cb), F32)],
        compiler_params=_cparams("arbitrary", "arbitrary"),
        name="hyena_lconv_two_stage",
    )(sig, gate_src, conv_w, conv_b, conv_w, conv_b, spec, bias, m1, m2, m2c, m3)


def kernel(x_prompt, x_sample, cache_na_k, cache_na_v, cache_da_k, cache_da_v, c, c_ctx, w_ada, b_ada, norm_mix,
           norm_ffn, w_in, hy_conv_w, hy_conv_b, hy_filt_w1, hy_filt_b1, hy_filt_w2, hy_filt_b2, hy_filt_w3,
           hy_filt_freq, hy_bias, na_rpb, da_lambda, da_subln, w_lift, w_out, w_ffn_in, w_ffn_out, norm_final):
    TP, TS = BATCH * SEQ, DEC_BATCH * DEC_SEQ
    xp = x_prompt.reshape(TP, D_MODEL)
    xs = x_sample.reshape(TS, D_MODEL)

    cc = jnp.concatenate([c_ctx[None, :], c, jnp.zeros((8 - 1 - DEC_BATCH, D_MODEL), F32)], axis=0)
    mod = _modulation(cc, w_ada, b_ada)
    mod_p = mod[:, 0:1].reshape(DEPTH, 1, 1, 6 * D_MODEL)
    mod_s = mod[:, 1:1 + DEC_BATCH].reshape(DEPTH, DEC_BATCH, 1, 6 * D_MODEL)

    w_mix = w_in[:, :, :MIX_W].astype(BF16)
    w_gate = w_in[:, :, MIX_W:].astype(BF16)
    w_lift_b = w_lift.astype(BF16)
    w_out_b = w_out.astype(BF16)
    w_ffn_in_b = w_ffn_in.astype(BF16)
    w_ffn_out_b = w_ffn_out.astype(BF16)
    g_mix = norm_mix.reshape(DEPTH, 1, D_MODEL)
    g_ffn = norm_ffn.reshape(DEPTH, 1, D_MODEL)
    g_fin = norm_final.reshape(1, D_MODEL)
    subln = da_subln.reshape(DEPTH, 1, DA_V_DIM)
    subln_col = da_subln.reshape(DEPTH, DA_V_DIM, 1)

    w1p = jnp.pad(hy_filt_w1, ((0, 0), (0, HY_FILT_HIDDEN - HY_POS_DIM), (0, 0)))
    b1 = hy_filt_b1.reshape(DEPTH, 1, HY_FILT_HIDDEN)
    b2 = hy_filt_b2.reshape(DEPTH, 1, HY_FILT_HIDDEN)
    fr = hy_filt_freq.reshape(DEPTH, 1, HY_FILT_HIDDEN)
    mf, mi = _dft_direct_mats()
    mats = _dft_two_stage_mats()
    h_p = _hyena_filters(SEQ, w1p, b1, hy_filt_w2, b2, hy_filt_w3, fr)
    h_s = _hyena_filters(DEC_SEQ, w1p, b1, hy_filt_w2, b2, hy_filt_w3, fr)
    spec_p = _spec_direct(h_p, jnp.asarray(_dft_real_mat()))
    mf_b, mi_b = jnp.asarray(mf, dtype=BF16), jnp.asarray(mi, dtype=BF16)
    mats_b = tuple(jnp.asarray(m, dtype=BF16) for m in mats)
    spec_s = _spec_two_stage(h_s, jnp.asarray(_dft_stage1_real_mat(), dtype=BF16), mats_b[1])
    conv_b = hy_conv_b.reshape(DEPTH, 1, 3 * BRANCH_W)

    na_bias = _na_bias_table(na_rpb)
    rope_tables = _rope_tables()
    ck_na = cache_na_k.reshape(DEC_BATCH, DEPTH, PAST_LEN, BRANCH_W)
    cv_na = cache_na_v.reshape(DEC_BATCH, DEPTH, PAST_LEN, BRANCH_W)
    ck_da = cache_da_k.reshape(DEC_BATCH, DEPTH, PAST_LEN, BRANCH_W)
    cv_da = cache_da_v.reshape(DEC_BATCH, DEPTH, PAST_LEN, BRANCH_W)

    caches = None
    for l in range(DEPTH):
        lam_init = 0.8 - 0.6 * math.exp(-0.3 * l)
        final = l == DEPTH - 1

        u, caches = _in_proj(xp, g_mix[l], mod_p[l], w_mix, l, TP, F32, with_caches=True, caches=caches)
        z1 = _lconv_direct(u, 0, u, 1, hy_conv_w[l], conv_b[l], spec_p, l, 0, hy_bias[l, 0:1], mf_b, mi_b, True)
        y_hy = _lconv_direct(z1, 0, u, 2, hy_conv_w[l], conv_b[l], spec_p, l, 1, hy_bias[l, 1:2], mf_b, mi_b, False)
        y_na, y_da = _ctx_attention(u, da_lambda[l], subln_col[l], lam_init)
        xp = _merge_out(xp, g_mix[l], mod_p[l], y_hy, y_na, y_da, w_gate, w_lift_b, w_out_b, l, TP)
        xp = _ffn(xp, g_ffn[l], mod_p[l], w_ffn_in_b, w_ffn_out_b, g_fin, l, TP, final)

        u = _in_proj(xs, g_mix[l], mod_s[l], w_mix, l, DEC_SEQ, BF16)
        u3 = u.reshape(DEC_BATCH, DEC_SEQ, MIX_W)
        z1 = _lconv_two_stage(u3, 0, u3, 1, hy_conv_w[l], conv_b[l], spec_s, l, 0, hy_bias[l, 0:1], mats_b, True)
        y_hy = _lconv_two_stage(z1, 0, u3, 2, hy_conv_w[l], conv_b[l], spec_s, l, 1, hy_bias[l, 1:2], mats_b, False)
        y_hy = y_hy.reshape(TS, BRANCH_W)
        qn, kn, vn = _attn_prep(u, 3, ck_na[:, l], cv_na[:, l], NA_HEAD_DIM)
        y_na = _nbr_attention(qn, kn, vn, na_bias, l)
        q, kt, v = _attn_prep(u, 6, ck_da[:, l], cv_da[:, l], DA_HEAD_DIM, rope_tables)
        y_da = _diff_attention(q, kt, v, da_lambda[l], subln_col[l], lam_init)
        xs = _merge_out(xs, g_mix[l], mod_s[l], y_hy, y_na, y_da, w_gate, w_lift_b, w_out_b, l, DEC_SEQ)
        xs = _ffn(xs, g_ffn[l], mod_s[l], w_ffn_in_b, w_ffn_out_b, g_fin, l, DEC_SEQ, final)

    y_prompt = xp.reshape(BATCH, SEQ, D_MODEL)
    y_sample = xs.reshape(DEC_BATCH, DEC_SEQ, D_MODEL)
    heads = lambda a, d: a.reshape(BATCH, DEPTH, SEQ, BRANCH_W // d, d)
    return (y_prompt, y_sample, heads(caches[0], NA_HEAD_DIM), heads(caches[1], NA_HEAD_DIM),
            heads(caches[2], 2 * DA_HEAD_DIM), heads(caches[3], DA_V_DIM))
```

```python
import functools
import math

import numpy as np
import jax
import jax.numpy as jnp
from jax import lax
from jax.experimental import pallas as pl
from jax.experimental.pallas import tpu as pltpu

F32 = jnp.float32
BF16 = jnp.bfloat16
HIGHEST = lax.Precision.HIGHEST

D_MODEL = 1024
BATCH = 32
SEQ = 256
DEPTH = 4
DEC_BATCH = 4
DEC_SEQ = 4096
PAST_LEN = 256
GRID_W = 64
GRID_H = DEC_SEQ // GRID_W
BRANCH_W = 512
HY_POS_BANDS = 16
HY_POS_DIM = 1 + 2 * HY_POS_BANDS
HY_FILT_HIDDEN = 64
HY_DECAY_TARGET = 1e-2
HY_FAST_DECAY = 0.3
HY_SLOW_DECAY = 1.5
NA_HEADS = 8
NA_HEAD_DIM = 64
NA_WIN_ROWS = 8
NA_WIN_COLS = 16
DA_HEADS = 8
DA_HEAD_DIM = 32
DA_V_DIM = 64
D_FF = 2816
MIX_W = 9 * BRANCH_W
ROPE_BASE = 10000.0
EPS = 1e-6
NEG_INF = -1e30

VMEM_LIMIT_BYTES = 56 * 1024 * 1024

FFT_N = 2 * DEC_SEQ
FFT_NO = 64
FFT_NI = 128
FFT_HALF = FFT_NO // 2
FFT_UNROLL = 4
LCONV_CB = 128


def _cparams(*sem):
    return pltpu.CompilerParams(dimension_semantics=sem, vmem_limit_bytes=VMEM_LIMIT_BYTES)


def _sigmoid(x):
    return 1.0 / (1.0 + jnp.exp(-x))


def _rms(x, g):
    return x * lax.rsqrt(jnp.mean(x * x, axis=-1, keepdims=True) + EPS) * g


def _modnorm(x, g, shift, scale):
    return _rms(x, g) * (1.0 + scale) + shift


def _bdot(a, b):
    return jnp.dot(a.astype(BF16), b.astype(BF16), preferred_element_type=F32)


def _bdot_nt(a, b):
    return lax.dot_general(a.astype(BF16), b.astype(BF16), (((1,), (1,)), ((), ())),
                           preferred_element_type=F32)


def _mod_kernel(c_ref, w_ref, b_ref, o_ref):
    c = c_ref[...]
    s = c * _sigmoid(c)
    o_ref[0] = jnp.dot(s, w_ref[0], precision=HIGHEST, preferred_element_type=F32) + b_ref[0]


def _modulation(cc, w_ada, b_ada):
    nt = 6
    return pl.pallas_call(
        _mod_kernel,
        grid=(DEPTH, nt),
        in_specs=[
            pl.BlockSpec((8, D_MODEL), lambda l, j: (0, 0)),
            pl.BlockSpec((1, D_MODEL, D_MODEL), lambda l, j: (l, 0, j)),
            pl.BlockSpec((1, 1, D_MODEL), lambda l, j: (l, 0, j)),
        ],
        out_specs=pl.BlockSpec((1, 8, D_MODEL), lambda l, j: (l, 0, j)),
        out_shape=jax.ShapeDtypeStruct((DEPTH, 8, 6 * D_MODEL), F32),
        compiler_params=_cparams("arbitrary", "arbitrary"),
        name="modulation",
    )(cc, w_ada, b_ada.reshape(DEPTH, 1, 6 * D_MODEL))


IN_TM = 1024
IN_TN = 3 * BRANCH_W
CACHE_BLOCKS = (4, 5, 7, 8)


def _in_kernel(*refs, n_alias, cache_tiles):
    x_ref, g_ref, mod_ref, w_ref = refs[:4]
    o_ref = refs[4 + n_alias]
    cache_refs = refs[5 + n_alias:5 + n_alias + len(cache_tiles)]
    h_ref = refs[-1]
    j = pl.program_id(1)

    @pl.when(j == 0)
    def _():
        m = mod_ref[0]
        h = _modnorm(x_ref[...], g_ref[...], m[:, 0:D_MODEL], m[:, D_MODEL:2 * D_MODEL])
        h_ref[...] = h.astype(BF16)

    res = jnp.dot(h_ref[...], w_ref[...], preferred_element_type=F32)
    o_ref[...] = res.astype(o_ref.dtype)
    for (tile, off), c_ref in zip(cache_tiles, cache_refs):
        @pl.when(j == tile)
        def _(c_ref=c_ref, off=off):
            c_ref[...] = res[:, off:off + BRANCH_W].reshape(c_ref.shape)


def _in_proj(x, g, mod, w, l, rows_per_mod, out_dtype, caches=None):
    T = x.shape[0]
    tm, tn = IN_TM, IN_TN
    per = rows_per_mod // tm
    in_specs = [
        pl.BlockSpec((tm, D_MODEL), lambda i, j: (i, 0)),
        pl.BlockSpec((1, D_MODEL), lambda i, j: (0, 0)),
        pl.BlockSpec((1, 1, 6 * D_MODEL), lambda i, j: (i // per, 0, 0)),
        pl.BlockSpec((None, D_MODEL, tn), lambda i, j: (l, 0, j)),
    ]
    out_specs = [pl.BlockSpec((tm, tn), lambda i, j: (i, j))]
    out_shape = [jax.ShapeDtypeStruct((T, MIX_W), out_dtype)]
    args = [x, g, mod, w]
    cache_tiles, aliases = (), {}
    if caches is not None:
        seqs = tm // SEQ
        cache_tiles = tuple(divmod(c * BRANCH_W, tn) for c in CACHE_BLOCKS)
        out_specs += [pl.BlockSpec((seqs, 1, SEQ, BRANCH_W), lambda i, j: (i, l, 0, 0))] * len(CACHE_BLOCKS)
        out_shape += [jax.ShapeDtypeStruct(c.shape, c.dtype) for c in caches]
        in_specs += [pl.BlockSpec(memory_space=pl.ANY)] * len(caches)
        aliases = {4 + n: 1 + n for n in range(len(caches))}
        args += list(caches)
    n_alias = len(args) - 4
    outs = pl.pallas_call(
        functools.partial(_in_kernel, n_alias=n_alias, cache_tiles=cache_tiles),
        grid=(T // tm, MIX_W // tn),
        in_specs=in_specs,
        out_specs=out_specs,
        out_shape=out_shape,
        input_output_aliases=aliases,
        scratch_shapes=[pltpu.VMEM((tm, D_MODEL), BF16)],
        compiler_params=_cparams("arbitrary", "arbitrary"),
        name="in_proj",
    )(*args)
    return outs[0] if caches is None else (outs[0], tuple(outs[1:]))


def _mid_kernel(x_ref, g_ref, mod_ref, yh_ref, yn_ref, yd_ref, wg_ref, wl_ref, wo_ref, o_ref):
    m = mod_ref[0]
    x = x_ref[...]
    h = _modnorm(x, g_ref[...], m[:, 0:D_MODEL], m[:, D_MODEL:2 * D_MODEL]).astype(BF16)
    merged = None
    for br, y_ref in enumerate((yh_ref, yn_ref, yd_ref)):
        gate = _sigmoid(jnp.dot(h, wg_ref[:, br * D_MODEL:(br + 1) * D_MODEL], preferred_element_type=F32))
        lift = jnp.dot(y_ref[...].astype(BF16), wl_ref[br], preferred_element_type=F32)
        t = gate * lift
        merged = t if merged is None else merged + t
    o_ref[...] = x + m[:, 2 * D_MODEL:3 * D_MODEL] * _bdot(merged, wo_ref[...])


def _merge_out(x, g, mod, y_hy, y_na, y_da, w_gate, w_lift, w_out, l, rows_per_mod):
    T = x.shape[0]
    tm = 512
    per = rows_per_mod // tm
    row = lambda i: (i, 0)
    const2 = lambda i: (0, 0)
    return pl.pallas_call(
        _mid_kernel,
        grid=(T // tm,),
        in_specs=[
            pl.BlockSpec((tm, D_MODEL), row),
            pl.BlockSpec((1, D_MODEL), const2),
            pl.BlockSpec((1, 1, 6 * D_MODEL), lambda i: (i // per, 0, 0)),
            pl.BlockSpec((tm, BRANCH_W), row),
            pl.BlockSpec((tm, BRANCH_W), row),
            pl.BlockSpec((tm, BRANCH_W), row),
            pl.BlockSpec((None, D_MODEL, 3 * D_MODEL), lambda i: (l, 0, 0)),
            pl.BlockSpec((None, 3, BRANCH_W, D_MODEL), lambda i: (l, 0, 0, 0)),
            pl.BlockSpec((None, D_MODEL, D_MODEL), lambda i: (l, 0, 0)),
        ],
        out_specs=pl.BlockSpec((tm, D_MODEL), row),
        out_shape=jax.ShapeDtypeStruct((T, D_MODEL), F32),
        compiler_params=_cparams("arbitrary"),
        name="merge_out",
    )(x, g, mod, y_hy, y_na, y_da, w_gate, w_lift, w_out)


FFN_CHUNK = D_FF // 2


def _ffn_kernel(x_ref, g_ref, mod_ref, w1g_ref, w1u_ref, w2_ref, gf_ref, o_ref, h_ref, acc_ref, *, final):
    k = pl.program_id(1)

    @pl.when(k == 0)
    def _():
        m = mod_ref[0]
        h = _modnorm(x_ref[...], g_ref[...], m[:, 3 * D_MODEL:4 * D_MODEL], m[:, 4 * D_MODEL:5 * D_MODEL])
        h_ref[...] = h.astype(BF16)

    h = h_ref[...]
    a = jnp.dot(h, w1g_ref[...], preferred_element_type=F32)
    b = jnp.dot(h, w1u_ref[...], preferred_element_type=F32)
    part = _bdot(a * _sigmoid(a) * b, w2_ref[...])

    @pl.when(k == 0)
    def _():
        acc_ref[...] = part

    @pl.when(k == 1)
    def _():
        m = mod_ref[0]
        xn = x_ref[...] + m[:, 5 * D_MODEL:6 * D_MODEL] * (acc_ref[...] + part)
        if final:
            xn = _rms(xn, gf_ref[...])
        o_ref[...] = xn


def _ffn(x, g, mod, w_ffn_in, w_ffn_out, g_final, l, rows_per_mod, final):
    T = x.shape[0]
    tm = 512
    per = rows_per_mod // tm
    return pl.pallas_call(
        functools.partial(_ffn_kernel, final=final),
        grid=(T // tm, 2),
        in_specs=[
            pl.BlockSpec((tm, D_MODEL), lambda i, k: (i, 0)),
            pl.BlockSpec((1, D_MODEL), lambda i, k: (0, 0)),
            pl.BlockSpec((1, 1, 6 * D_MODEL), lambda i, k: (i // per, 0, 0)),
            pl.BlockSpec((None, D_MODEL, FFN_CHUNK), lambda i, k: (l, 0, k)),
            pl.BlockSpec((None, D_MODEL, FFN_CHUNK), lambda i, k: (l, 0, 2 + k)),
            pl.BlockSpec((None, FFN_CHUNK, D_MODEL), lambda i, k: (l, k, 0)),
            pl.BlockSpec((1, D_MODEL), lambda i, k: (0, 0)),
        ],
        out_specs=pl.BlockSpec((tm, D_MODEL), lambda i, k: (i, 0)),
        out_shape=jax.ShapeDtypeStruct((T, D_MODEL), F32),
        scratch_shapes=[pltpu.VMEM((tm, D_MODEL), BF16), pltpu.VMEM((tm, D_MODEL), F32)],
        compiler_params=_cparams("arbitrary", "arbitrary"),
        name="ffn",
    )(x, g, mod, w_ffn_in, w_ffn_in, w_ffn_out, g_final)


def _da_lambda(lam_ref, lam_init):
    lp = lam_ref[...]
    a = jnp.sum(lp[0:1] * lp[1:2], axis=1, keepdims=True)
    b = jnp.sum(lp[2:3] * lp[3:4], axis=1, keepdims=True)
    return jnp.exp(a) - jnp.exp(b) + lam_init


def _softmax_rows(s):
    m = jnp.max(s, axis=-1, keepdims=True)
    p = jnp.exp(s - m)
    return p, jnp.sum(p, axis=-1, keepdims=True)


ATT_ONES_ROWS = 16
ATT_TQ = 256
ATT_KEYS = DEC_SEQ + PAST_LEN
LOG2E = math.log2(math.e)


def _masked_q_blocks(qt, d):
    row = lax.broadcasted_iota(jnp.int32, qt.shape, 0)
    zero = jnp.zeros_like(qt)
    return jnp.concatenate([jnp.where((row >= j * d) & (row < (j + 1) * d), qt, zero) for j in range(128 // d)], axis=1)


def _colmax(st):
    keys, n = st.shape
    return jnp.max(jnp.max(st.reshape(keys // 256, 256, n), axis=0), axis=0, keepdims=True)


def _ctx_attn_kernel(nq_ref, nk_ref, nv_ref, dq_ref, dk_ref, dv_ref, lam_ref, sub_ref, yn_ref, yd_ref, acc_ref,
                     *, lam_init):
    lam = _da_lambda(lam_ref, lam_init)
    ones = jnp.ones((ATT_ONES_ROWS, SEQ), BF16)

    def attend(q_ref, k_ref, v_ref, d, maps_per_head, finish):
        qt = (q_ref[...] * (d ** -0.5 * LOG2E)).T.astype(BF16)
        vt = v_ref[...].T.astype(BF16)
        kb = k_ref[...].astype(BF16)
        dv = NA_HEAD_DIM
        heads_per_group = 128 // (d * maps_per_head)
        w = maps_per_head * SEQ
        for g in range(BRANCH_W // 128):
            st = jnp.dot(kb[:, g * 128:(g + 1) * 128], _masked_q_blocks(qt[g * 128:(g + 1) * 128], d),
                         preferred_element_type=F32)
            pt = jnp.exp2(st - _colmax(st)).astype(BF16)
            for j in range(heads_per_group):
                h = g * heads_per_group + j
                ve = jnp.concatenate([vt[h * dv:(h + 1) * dv], ones], axis=0)
                oe = jnp.dot(ve, pt[:, j * w:(j + 1) * w], preferred_element_type=F32)
                os = [oe[0:dv, i * SEQ:(i + 1) * SEQ] / oe[dv:dv + 1, i * SEQ:(i + 1) * SEQ]
                      for i in range(maps_per_head)]
                acc_ref[h * dv:(h + 1) * dv, :] = finish(os)

    attend(nq_ref, nk_ref, nv_ref, NA_HEAD_DIM, 1, lambda os: os[0])
    yn_ref[...] = acc_ref[...].T

    def da_finish(os):
        ot = os[0] - lam * os[1]
        ot = ot * lax.rsqrt(jnp.mean(ot * ot, axis=0, keepdims=True) + EPS) * sub_ref[...]
        return ot * (1.0 - lam_init)

    attend(dq_ref, dk_ref, dv_ref, DA_HEAD_DIM, 2, da_finish)
    yd_ref[...] = acc_ref[...].T


def _ctx_attention(u, da_lambda, subln_col, lam_init):
    col = lambda j: pl.BlockSpec((SEQ, BRANCH_W), lambda b, j=j: (b, j))
    out = pl.BlockSpec((SEQ, BRANCH_W), lambda b: (b, 0))
    shape = jax.ShapeDtypeStruct((BATCH * SEQ, BRANCH_W), F32)
    return pl.pallas_call(
        functools.partial(_ctx_attn_kernel, lam_init=lam_init),
        grid=(BATCH,),
        in_specs=[col(3), col(4), col(5), col(6), col(7), col(8),
                  pl.BlockSpec((4, DA_HEAD_DIM), lambda b: (0, 0)),
                  pl.BlockSpec((DA_V_DIM, 1), lambda b: (0, 0))],
        out_specs=[out, out],
        out_shape=[shape, shape],
        scratch_shapes=[pltpu.VMEM((BRANCH_W, SEQ), F32)],
        compiler_params=_cparams("arbitrary"),
        name="ctx_attention",
    )(u, u, u, u, u, u, da_lambda, subln_col)


def _rope(x, cos, sin_signed):
    n = x.shape[-1]
    lane = lax.broadcasted_iota(jnp.int32, x.shape, 1)
    partner = jnp.where(lane % 2 == 0, pltpu.roll(x, n - 1, axis=1), pltpu.roll(x, 1, axis=1))
    return x * cos + partner * sin_signed


def _attn_prep_kernel(q_ref, k_ref, v_ref, kc_ref, vc_ref, *refs, rope, scale):
    cos_ref, sin_ref = refs[:2] if rope else (None, None)
    qt_ref, ko_ref, vt_ref = refs[-3:]
    t = pl.program_id(1)
    dv = NA_HEAD_DIM

    def put_v(v):
        vt = v.astype(F32).T.astype(BF16)
        ones = jnp.ones((ATT_ONES_ROWS, ATT_TQ), BF16)
        for h in range(BRANCH_W // dv):
            vt_ref[0, h, 0:dv, :] = vt[h * dv:(h + 1) * dv]
            vt_ref[0, h, dv:dv + ATT_ONES_ROWS, :] = ones

    @pl.when(t < DEC_SEQ // ATT_TQ)
    def _():
        q = q_ref[...].astype(F32)
        k = k_ref[...].astype(F32)
        if rope:
            q = _rope(q, cos_ref[...], sin_ref[...])
            k = _rope(k, cos_ref[...], sin_ref[...])
        qt_ref[0] = (q * scale).T.astype(BF16)
        ko_ref[0] = k.astype(BF16)
        put_v(v_ref[...])

    @pl.when(t == DEC_SEQ // ATT_TQ)
    def _():
        ko_ref[0] = kc_ref[0].astype(BF16)
        put_v(vc_ref[0])


def _attn_prep(u, first_col, k_ctx, v_ctx, head_dim, rope_tables=None):
    rope = rope_tables is not None
    nt = DEC_SEQ // ATT_TQ
    last = nt - 1
    rowblk = lambda j: pl.BlockSpec((ATT_TQ, BRANCH_W), lambda b, t, j=j: (b * nt + jnp.minimum(t, last), j))
    tab = pl.BlockSpec((ATT_TQ, BRANCH_W), lambda b, t: (jnp.minimum(t, last), 0))
    ctx = pl.BlockSpec((1, PAST_LEN, BRANCH_W), lambda b, t: (b, 0, 0))
    heads = BRANCH_W // NA_HEAD_DIM
    vrows = NA_HEAD_DIM + ATT_ONES_ROWS
    return pl.pallas_call(
        functools.partial(_attn_prep_kernel, rope=rope, scale=head_dim ** -0.5 * LOG2E),
        grid=(DEC_BATCH, nt + 1),
        in_specs=[rowblk(first_col), rowblk(first_col + 1), rowblk(first_col + 2), ctx, ctx] + [tab, tab] * rope,
        out_specs=[
            pl.BlockSpec((1, BRANCH_W, ATT_TQ), lambda b, t: (b, 0, jnp.minimum(t, last))),
            pl.BlockSpec((1, ATT_TQ, BRANCH_W), lambda b, t: (b, t, 0)),
            pl.BlockSpec((1, heads, vrows, ATT_TQ), lambda b, t: (b, 0, 0, t)),
        ],
        out_shape=[
            jax.ShapeDtypeStruct((DEC_BATCH, BRANCH_W, DEC_SEQ), BF16),
            jax.ShapeDtypeStruct((DEC_BATCH, ATT_KEYS, BRANCH_W), BF16),
            jax.ShapeDtypeStruct((DEC_BATCH, heads, vrows, ATT_KEYS), BF16),
        ],
        compiler_params=_cparams("arbitrary", "arbitrary"),
        name="attn_prep",
    )(u, u, u, k_ctx, v_ctx, *(rope_tables or ()))


NA_ROWS = ATT_TQ // GRID_W
NA_UNION = 3 * NA_ROWS
NA_STEPS = GRID_H // NA_ROWS
NA_SLABS = NA_UNION // NA_ROWS
NA_VARIANT_OFFSET = (0, -NA_ROWS, -2 * NA_ROWS)


def _na_variant(s):
    return jnp.minimum(s, 1) + s // (NA_STEPS - 1)


def _na_window_block(s):
    return jnp.clip(s - 1, 0, NA_STEPS - NA_SLABS)


def _na_bias_kernel(rpb_ref, o_ref):
    kc = lax.broadcasted_iota(jnp.int32, (GRID_W, GRID_W), 0)
    qc = lax.broadcasted_iota(jnp.int32, (GRID_W, GRID_W), 1)
    dc = jnp.clip(kc - qc, -(NA_WIN_COLS - 1), NA_WIN_COLS - 1) + (NA_WIN_COLS - 1)
    c0 = jnp.clip(qc - NA_WIN_COLS // 2, 0, GRID_W - NA_WIN_COLS)
    col_ok = (kc >= c0) & (kc < c0 + NA_WIN_COLS)
    r = rpb_ref[0, 0] * LOG2E
    masked = jnp.full((GRID_W, GRID_W), NEG_INF, F32)
    tiles = []
    for dr in range(2 * NA_WIN_ROWS - 1):
        acc = jnp.zeros((GRID_W, GRID_W), F32)
        for d in range(2 * NA_WIN_COLS - 1):
            acc = jnp.where(dc == d, r[dr:dr + 1, d:d + 1], acc)
        tiles.append(jnp.where(col_ok, acc, masked))
    for v, off in enumerate(NA_VARIANT_OFFSET):
        for kr in range(NA_UNION):
            for rr in range(NA_ROWS):
                w0 = (0, rr, NA_UNION - NA_WIN_ROWS)[v]
                dr = kr + off - rr
                inside = w0 <= kr < w0 + NA_WIN_ROWS
                o_ref[0, v, 0, kr * GRID_W:(kr + 1) * GRID_W, rr * GRID_W:(rr + 1) * GRID_W] = (
                    tiles[dr + NA_WIN_ROWS - 1] if inside else masked)


def _na_bias_table(na_rpb):
    n_dr, n_dc = 2 * NA_WIN_ROWS - 1, 2 * NA_WIN_COLS - 1
    nv = len(NA_VARIANT_OFFSET)
    return pl.pallas_call(
        _na_bias_kernel,
        grid=(DEPTH, NA_HEADS),
        in_specs=[pl.BlockSpec((1, 1, n_dr, n_dc), lambda l, h: (l, h, 0, 0))],
        out_specs=pl.BlockSpec((1, nv, 1, NA_UNION * GRID_W, ATT_TQ), lambda l, h: (l, 0, h, 0, 0)),
        out_shape=jax.ShapeDtypeStruct((DEPTH, nv, NA_HEADS, NA_UNION * GRID_W, ATT_TQ), F32),
        compiler_params=_cparams("arbitrary", "arbitrary"),
        name="na_bias_table",
    )(na_rpb)


def _na_kernel(qt_ref, *refs):
    n = NA_SLABS + 1
    k_refs, vt_refs = refs[:n], refs[n:2 * n]
    bias_ref, o_ref, acc_ref = refs[2 * n:]
    dv = NA_HEAD_DIM
    heads_per_group = 128 // dv
    for g in range(BRANCH_W // 128):
        lanes = slice(g * 128, (g + 1) * 128)
        qbd = _masked_q_blocks(qt_ref[0, lanes, :], dv)
        sts = []
        for j, k_ref in enumerate(k_refs):
            st = jnp.dot(k_ref[0, :, lanes], qbd, preferred_element_type=F32)
            if j < NA_SLABS:
                rows = slice(j * ATT_TQ, (j + 1) * ATT_TQ)
                st = st + jnp.concatenate(
                    [bias_ref[0, g * heads_per_group + hh, rows, :] for hh in range(heads_per_group)], axis=1)
            sts.append(st)
        mx = functools.reduce(jnp.maximum, [_colmax(st) for st in sts])
        pts = [jnp.exp2(st - mx).astype(BF16) for st in sts]
        for hh in range(heads_per_group):
            h = g * heads_per_group + hh
            oe = sum(jnp.dot(vt_ref[0, h], pt[:, hh * ATT_TQ:(hh + 1) * ATT_TQ], preferred_element_type=F32)
                     for vt_ref, pt in zip(vt_refs, pts))
            acc_ref[h * dv:(h + 1) * dv, :] = oe[0:dv] / oe[dv:dv + 1]
    o_ref[...] = acc_ref[...].T


def _nbr_attention(qt, k, vt, bias, l):
    vrows = NA_HEAD_DIM + ATT_ONES_ROWS
    ctx_blk = DEC_SEQ // ATT_TQ
    k_specs = [pl.BlockSpec((1, ATT_TQ, BRANCH_W), lambda b, s, j=j: (b, _na_window_block(s) + j, 0))
               for j in range(NA_SLABS)]
    k_specs.append(pl.BlockSpec((1, ATT_TQ, BRANCH_W), lambda b, s: (b, ctx_blk, 0)))
    vt_specs = [pl.BlockSpec((1, NA_HEADS, vrows, ATT_TQ), lambda b, s, j=j: (b, 0, 0, _na_window_block(s) + j))
                for j in range(NA_SLABS)]
    vt_specs.append(pl.BlockSpec((1, NA_HEADS, vrows, ATT_TQ), lambda b, s: (b, 0, 0, ctx_blk)))
    n = NA_SLABS + 1
    return pl.pallas_call(
        _na_kernel,
        grid=(DEC_BATCH, NA_STEPS),
        in_specs=[pl.BlockSpec((1, BRANCH_W, ATT_TQ), lambda b, s: (b, 0, s))] + k_specs + vt_specs + [
            pl.BlockSpec((None, 1, NA_HEADS, NA_UNION * GRID_W, ATT_TQ), lambda b, s: (l, _na_variant(s), 0, 0, 0))],
        out_specs=pl.BlockSpec((ATT_TQ, BRANCH_W), lambda b, s: (b * NA_STEPS + s, 0)),
        out_shape=jax.ShapeDtypeStruct((DEC_BATCH * DEC_SEQ, BRANCH_W), F32),
        scratch_shapes=[pltpu.VMEM((BRANCH_W, ATT_TQ), F32)],
        compiler_params=_cparams("arbitrary", "arbitrary"),
        name="nbr_attention",
    )(qt, *([k] * n), *([vt] * n), bias)


DA_TQ = ATT_TQ
DA_KEYS = ATT_KEYS
DA_ONES_ROWS = ATT_ONES_ROWS
DA_MAPS_PER_TILE = 128 // DA_HEAD_DIM


def _da_kernel(qt_ref, k_ref, vt_ref, lam_ref, sub_ref, o_ref, acc_ref, *, lam_init):
    lam = _da_lambda(lam_ref, lam_init)
    st = jnp.dot(k_ref[0], _masked_q_blocks(qt_ref[0], DA_HEAD_DIM), preferred_element_type=F32)
    pt = jnp.exp2(st - _colmax(st)).astype(BF16)
    heads = DA_MAPS_PER_TILE // 2
    for h in range(heads):
        oe = jnp.dot(vt_ref[0, h], pt[:, 2 * h * DA_TQ:(2 * h + 2) * DA_TQ], preferred_element_type=F32)
        os = [oe[0:DA_V_DIM, i * DA_TQ:(i + 1) * DA_TQ] / oe[DA_V_DIM:DA_V_DIM + 1, i * DA_TQ:(i + 1) * DA_TQ]
              for i in range(2)]
        ot = os[0] - lam * os[1]
        ot = ot * lax.rsqrt(jnp.mean(ot * ot, axis=0, keepdims=True) + EPS) * sub_ref[...]
        acc_ref[h * DA_V_DIM:(h + 1) * DA_V_DIM, :] = ot * (1.0 - lam_init)
    o_ref[...] = acc_ref[...].T


def _diff_attention(qt, k, vt, da_lambda, subln_col, lam_init):
    nt = DEC_SEQ // DA_TQ
    vrows = DA_V_DIM + DA_ONES_ROWS
    groups = BRANCH_W // 128
    heads = DA_MAPS_PER_TILE // 2
    return pl.pallas_call(
        functools.partial(_da_kernel, lam_init=lam_init),
        grid=(DEC_BATCH, groups, nt),
        in_specs=[
            pl.BlockSpec((1, 128, DA_TQ), lambda b, g, t: (b, g, t)),
            pl.BlockSpec((1, DA_KEYS, 128), lambda b, g, t: (b, 0, g)),
            pl.BlockSpec((1, heads, vrows, DA_KEYS), lambda b, g, t: (b, g, 0, 0)),
            pl.BlockSpec((4, DA_HEAD_DIM), lambda b, g, t: (0, 0)),
            pl.BlockSpec((DA_V_DIM, 1), lambda b, g, t: (0, 0)),
        ],
        out_specs=pl.BlockSpec((DA_TQ, 128), lambda b, g, t: (b * nt + t, g)),
        out_shape=jax.ShapeDtypeStruct((DEC_BATCH * DEC_SEQ, BRANCH_W), F32),
        scratch_shapes=[pltpu.VMEM((128, DA_TQ), F32)],
        compiler_params=_cparams("arbitrary", "arbitrary", "arbitrary"),
        name="diff_attention",
    )(qt, k, vt, da_lambda, subln_col)


def _rope_tables():
    pos = np.arange(DEC_SEQ)
    row = (pos // GRID_W).astype(np.float32)
    col = (pos % GRID_W).astype(np.float32)
    n_freq = DA_HEAD_DIM // 4
    inv = (np.float32(ROPE_BASE) ** (-np.arange(n_freq, dtype=np.float32) / n_freq)).astype(np.float32)
    ang = np.concatenate([row[:, None] * inv[None, :], col[:, None] * inv[None, :]], axis=-1)
    ang = ang.astype(np.float64)
    cos = np.repeat(np.cos(ang), 2, axis=-1)
    sin = np.repeat(np.sin(ang), 2, axis=-1)
    sign = np.where(np.arange(DA_HEAD_DIM) % 2 == 0, -1.0, 1.0)
    reps = BRANCH_W // DA_HEAD_DIM
    cos = np.tile(cos, (1, reps)).astype(np.float32)
    sin = np.tile(sin * sign[None, :], (1, reps)).astype(np.float32)
    return jnp.asarray(cos), jnp.asarray(sin)


def _filt_hidden_kernel(feat_ref, w1_ref, b1_ref, w2_ref, b2_ref, fr_ref, o_ref):
    fr = fr_ref[0]
    h = jnp.sin(fr * (jnp.dot(feat_ref[...], w1_ref[0], precision=HIGHEST, preferred_element_type=F32) + b1_ref[0]))
    o_ref[0] = jnp.sin(fr * (jnp.dot(h, w2_ref[0], precision=HIGHEST, preferred_element_type=F32) + b2_ref[0]))


def _filt_kernel(h_ref, w3f_ref, w3b_ref, dec_ref, o_ref):
    L = dec_ref.shape[0] // 2
    hf = jnp.dot(h_ref[0, 0:L], w3f_ref[0], precision=HIGHEST, preferred_element_type=F32) * dec_ref[0:L]
    hb = jnp.dot(h_ref[0, L:2 * L], w3b_ref[0], precision=HIGHEST, preferred_element_type=F32) * dec_ref[L:2 * L]
    row = lax.broadcasted_iota(jnp.int32, hb.shape, 0)
    hb = jnp.where(row == 0, 0.0, hb)
    nrm = jnp.sum(jnp.abs(hf), axis=0, keepdims=True) + jnp.sum(jnp.abs(hb), axis=0, keepdims=True)
    o_ref[0, 0, 0:L] = hf / nrm
    o_ref[0, 0, L:2 * L] = hb / nrm


def _circular_order(a):
    return np.concatenate([a, a[:1], a[1:][::-1]], axis=0)


def _hyena_pos_tables(L):
    f32 = np.float32
    pos = np.arange(L, dtype=f32)
    t = (pos / f32(L)).astype(f32)
    bands = np.linspace(1e-4, HY_POS_BANDS - 1, HY_POS_BANDS, dtype=f32)
    ang = (f32(2 * math.pi / L) * pos[:, None] * bands[None, :]).astype(np.float64)
    feats = np.zeros((L, HY_FILT_HIDDEN), f32)
    feats[:, 0] = t
    feats[:, 1:1 + HY_POS_BANDS] = np.cos(ang)
    feats[:, 1 + HY_POS_BANDS:HY_POS_DIM] = -np.sin(ang)
    deltas = np.linspace(math.log(HY_DECAY_TARGET) / HY_SLOW_DECAY,
                         math.log(HY_DECAY_TARGET) / HY_FAST_DECAY, BRANCH_W, dtype=f32)
    decay = np.exp((-t[:, None] * np.abs(deltas)[None, :]).astype(np.float64)).astype(f32)
    return jnp.asarray(_circular_order(feats)), jnp.asarray(_circular_order(decay))


def _hyena_filters(half, w1p, b1, w2, b2, w3, freq):
    feats, decay = _hyena_pos_tables(half)
    L = 2 * half
    cb = 128
    ncb = BRANCH_W // cb
    small = lambda shape: pl.BlockSpec((1,) + shape, lambda l: (l, 0, 0))
    hidden = pl.pallas_call(
        _filt_hidden_kernel,
        grid=(DEPTH,),
        in_specs=[
            pl.BlockSpec((L, HY_FILT_HIDDEN), lambda l: (0, 0)),
            small((HY_FILT_HIDDEN, HY_FILT_HIDDEN)), small((1, HY_FILT_HIDDEN)),
            small((HY_FILT_HIDDEN, HY_FILT_HIDDEN)), small((1, HY_FILT_HIDDEN)),
            small((1, HY_FILT_HIDDEN)),
        ],
        out_specs=pl.BlockSpec((1, L, HY_FILT_HIDDEN), lambda l: (l, 0, 0)),
        out_shape=jax.ShapeDtypeStruct((DEPTH, L, HY_FILT_HIDDEN), F32),
        compiler_params=_cparams("arbitrary"),
        name=f"hyena_filter_hidden_{L}",
    )(feats, w1p, b1, w2, b2, freq)
    return pl.pallas_call(
        _filt_kernel,
        grid=(DEPTH, 2, ncb),
        in_specs=[
            pl.BlockSpec((1, L, HY_FILT_HIDDEN), lambda l, o, c: (l, 0, 0)),
            pl.BlockSpec((1, HY_FILT_HIDDEN, cb), lambda l, o, c: (l, 0, o * 2 * ncb + c)),
            pl.BlockSpec((1, HY_FILT_HIDDEN, cb), lambda l, o, c: (l, 0, o * 2 * ncb + ncb + c)),
            pl.BlockSpec((L, cb), lambda l, o, c: (0, c)),
        ],
        out_specs=pl.BlockSpec((1, 1, L, cb), lambda l, o, c: (l, o, 0, c)),
        out_shape=jax.ShapeDtypeStruct((DEPTH, 2, L, BRANCH_W), F32),
        compiler_params=_cparams("arbitrary", "arbitrary", "arbitrary"),
        name=f"hyena_filters_{L}",
    )(hidden, w3, w3, decay)


def _short_conv(u, w_ref, b_ref, seq_len):
    n = u.shape[0]
    t = lax.broadcasted_iota(jnp.int32, u.shape, 0) % seq_len
    prev = jnp.where(t == 0, 0.0, pltpu.roll(u, 1, axis=0))
    nxt = jnp.where(t == seq_len - 1, 0.0, pltpu.roll(u, n - 1, axis=0))
    return prev * w_ref[0:1, :] + u * w_ref[1:2, :] + nxt * w_ref[2:3, :] + b_ref[...]


def _dft_direct_mats():
    n, half = 2 * SEQ, SEQ
    k = np.arange(n)[:, None].astype(np.float64)
    t = np.arange(half)[None, :].astype(np.float64)
    ang = 2 * np.pi * k * t / n
    fr, fi = np.cos(ang), -np.sin(ang)
    mf = np.block([[fr, -fi], [fi, fr]])
    gr, gi = np.cos(ang).T / n, np.sin(ang).T / n
    mi = np.block([[gr, -gi], [gi, gr]])
    return mf.astype(np.float32), mi.astype(np.float32)


def _dft_real_mat():
    n = 2 * SEQ
    ang = 2 * np.pi * np.arange(n)[:, None].astype(np.float64) * np.arange(n)[None, :] / n
    return np.concatenate([np.cos(ang), -np.sin(ang)], axis=0).astype(np.float32)


def _spec_direct_kernel(h_ref, m_ref, o_ref):
    o_ref[0, 0] = jnp.dot(m_ref[...], h_ref[0, 0], precision=HIGHEST, preferred_element_type=F32)


def _spec_direct(h, m_real):
    n = 2 * SEQ
    return pl.pallas_call(
        _spec_direct_kernel,
        grid=(DEPTH, 2),
        in_specs=[pl.BlockSpec((1, 1, n, BRANCH_W), lambda l, o: (l, o, 0, 0)),
                  pl.BlockSpec((2 * n, n), lambda l, o: (0, 0))],
        out_specs=pl.BlockSpec((1, 1, 2 * n, BRANCH_W), lambda l, o: (l, o, 0, 0)),
        out_shape=jax.ShapeDtypeStruct((DEPTH, 2, 2 * n, BRANCH_W), F32),
        compiler_params=_cparams("arbitrary", "arbitrary"),
        name="hyena_spectrum_direct",
    )(h, m_real)


def _lconv_direct_kernel(s_ref, g_ref, cws_ref, cbs_ref, cwg_ref, cbg_ref, h_ref, bias_ref, mf_ref, mi_ref, o_ref,
                         *, conv_sig):
    n = 2 * SEQ
    sig = s_ref[...]
    if conv_sig:
        sig = _short_conv(sig, cws_ref, cbs_ref, SEQ)
    gate = _short_conv(g_ref[...], cwg_ref, cbg_ref, SEQ)
    z = jnp.dot(mf_ref[...], sig.astype(BF16), preferred_element_type=F32)
    zr, zi = z[0:n], z[n:2 * n]
    hr, hi = h_ref[0:n], h_ref[n:2 * n]
    y = jnp.concatenate([zr * hr - zi * hi, zr * hi + zi * hr], axis=0)
    y = jnp.dot(mi_ref[...], y.astype(BF16), preferred_element_type=F32)
    o_ref[...] = gate * (y + sig * bias_ref[...])


def _lconv_direct(sig, sig_col, gate_src, gate_col, conv_w, conv_b, spec, l, order, bias, mf, mi, conv_sig):
    n = 2 * SEQ
    rows = 2 * SEQ
    T = sig.shape[0]
    return pl.pallas_call(
        functools.partial(_lconv_direct_kernel, conv_sig=conv_sig),
        grid=(T // rows,),
        in_specs=[
            pl.BlockSpec((rows, BRANCH_W), lambda p: (p, sig_col)),
            pl.BlockSpec((rows, BRANCH_W), lambda p: (p, gate_col)),
            pl.BlockSpec((3, BRANCH_W), lambda p: (0, 0)),
            pl.BlockSpec((1, BRANCH_W), lambda p: (0, 0)),
            pl.BlockSpec((3, BRANCH_W), lambda p: (0, gate_col)),
            pl.BlockSpec((1, BRANCH_W), lambda p: (0, gate_col)),
            pl.BlockSpec((None, None, 2 * n, BRANCH_W), lambda p: (l, order, 0, 0)),
            pl.BlockSpec((1, BRANCH_W), lambda p: (0, 0)),
            pl.BlockSpec((2 * n, rows), lambda p: (0, 0)),
            pl.BlockSpec((rows, 2 * n), lambda p: (0, 0)),
        ],
        out_specs=pl.BlockSpec((rows, BRANCH_W), lambda p: (p, 0)),
        out_shape=jax.ShapeDtypeStruct((T, BRANCH_W), F32),
        compiler_params=_cparams("arbitrary"),
        name="hyena_lconv_direct",
    )(sig, gate_src, conv_w, conv_b, conv_w, conv_b, spec, bias, mf, mi)


def _dft_two_stage_mats():
    no, ni, half, n = FFT_NO, FFT_NI, FFT_HALF, FFT_N
    f64 = np.float64
    k1 = np.arange(no, dtype=f64)
    n_o = np.arange(half, dtype=f64)
    n_i = np.arange(ni, dtype=f64)
    ang = 2 * np.pi * (n_i[:, None, None] * k1[None, :, None] / n + k1[None, :, None] * n_o[None, None, :] / no)
    tr, ti = np.cos(ang), -np.sin(ang)
    m1 = np.concatenate([np.concatenate([tr, -ti], axis=2), np.concatenate([ti, tr], axis=2)], axis=1)
    k2 = np.arange(ni, dtype=f64)
    ang2 = 2 * np.pi * k2[:, None] * n_i[None, :] / ni
    f2r, f2i = np.cos(ang2), -np.sin(ang2)
    m2 = np.block([[f2r, -f2i], [f2i, f2r]])
    m2c = np.block([[f2r, f2i], [-f2i, f2r]])
    sr, si = np.transpose(tr, (0, 2, 1)) / n, -np.transpose(ti, (0, 2, 1)) / n
    m3 = np.concatenate([np.concatenate([sr, -si], axis=2), np.concatenate([si, sr], axis=2)], axis=1)
    return (m1.astype(np.float32), m2.astype(np.float32), m2c.astype(np.float32), m3.astype(np.float32))


def _dft_stage1_real_mat():
    no, ni, n = FFT_NO, FFT_NI, FFT_N
    k1 = np.arange(no, dtype=np.float64)
    n_o = np.arange(no, dtype=np.float64)
    n_i = np.arange(ni, dtype=np.float64)
    ang = 2 * np.pi * (n_i[:, None, None] * k1[None, :, None] / n + k1[None, :, None] * n_o[None, None, :] / no)
    return np.concatenate([np.cos(ang), -np.sin(ang)], axis=1).astype(np.float32)


def _store_stage1(w_ref, ni, out):
    w_ref[pl.ds(ni, FFT_NO, stride=2 * FFT_NI), :] = out[0:FFT_NO]
    w_ref[pl.ds(FFT_NI + ni, FFT_NO, stride=2 * FFT_NI), :] = out[FFT_NO:2 * FFT_NO]


def _fwd_stage1(za_ref, zb_ref, m1_ref, w_ref):
    def body(ni, carry):
        a = za_ref[pl.ds(ni, FFT_HALF, stride=FFT_NI), :]
        b = zb_ref[pl.ds(ni, FFT_HALF, stride=FFT_NI), :]
        out = jnp.dot(m1_ref[ni], jnp.concatenate([a, b], axis=0).astype(BF16), preferred_element_type=F32)
        _store_stage1(w_ref, ni, out)
        return carry

    lax.fori_loop(0, FFT_NI, body, 0, unroll=FFT_UNROLL)


def _spec_two_stage_kernel(h_ref, m1_ref, m2_ref, o_ref, w_ref):
    h = h_ref.at[0, 0]

    def stage1(ni, carry):
        a = h[pl.ds(ni, FFT_NO, stride=FFT_NI), :]
        _store_stage1(w_ref, ni, jnp.dot(m1_ref[ni], a.astype(BF16), preferred_element_type=F32))
        return carry

    lax.fori_loop(0, FFT_NI, stage1, 0, unroll=FFT_UNROLL)
    blk = 2 * FFT_NI

    cb = w_ref.shape[1]

    def stage2(kp, carry):
        rows = [pl.ds(pl.multiple_of((2 * kp + j) * blk, blk), blk) for j in range(2)]
        x = jnp.dot(m2_ref[...], jnp.concatenate([w_ref[r, :] for r in rows], axis=1).astype(BF16),
                    preferred_element_type=F32)
        for j in range(2):
            o_ref[0, 0, rows[j], :] = x[:, j * cb:(j + 1) * cb]
        return carry

    lax.fori_loop(0, FFT_NO // 2, stage2, 0, unroll=FFT_UNROLL)


def _spec_two_stage(h, m1_real, m2):
    cb = LCONV_CB
    return pl.pallas_call(
        _spec_two_stage_kernel,
        grid=(DEPTH, 2, BRANCH_W // cb),
        in_specs=[pl.BlockSpec((1, 1, FFT_N, cb), lambda l, o, c: (l, o, 0, c)),
                  pl.BlockSpec((FFT_NI, 2 * FFT_NO, FFT_NO), lambda l, o, c: (0, 0, 0)),
                  pl.BlockSpec((2 * FFT_NI, 2 * FFT_NI), lambda l, o, c: (0, 0))],
        out_specs=pl.BlockSpec((1, 1, 2 * FFT_N, cb), lambda l, o, c: (l, o, 0, c)),
        out_shape=jax.ShapeDtypeStruct((DEPTH, 2, 2 * FFT_N, BRANCH_W), F32),
        scratch_shapes=[pltpu.VMEM((2 * FFT_N, cb), F32)],
        compiler_params=_cparams("arbitrary", "arbitrary", "arbitrary"),
        name="hyena_spectrum_two_stage",
    )(h, m1_real, m2)


def _lconv_two_stage_kernel(s_ref, g_ref, cws_ref, cbs_ref, cwg_ref, cbg_ref, h_ref, bias_ref,
                            m1_ref, m2_ref, m2c_ref, m3_ref, o_ref, z_ref, w_ref, *, conv_sig):
    for b in range(2):
        sig = s_ref[b].astype(F32)
        if conv_sig:
            sig = _short_conv(sig, cws_ref, cbs_ref, DEC_SEQ)
        z_ref[b] = sig
    _fwd_stage1(z_ref.at[0], z_ref.at[1], m1_ref, w_ref)
    blk = 2 * FFT_NI

    cb = w_ref.shape[1]

    def mid(kp, carry):
        rows = [pl.ds(pl.multiple_of((2 * kp + j) * blk, blk), blk) for j in range(2)]
        x = jnp.dot(m2_ref[...], jnp.concatenate([w_ref[r, :] for r in rows], axis=1).astype(BF16),
                    preferred_element_type=F32)
        h = jnp.concatenate([h_ref[r, :] for r in rows], axis=1)
        xr, xi = x[0:FFT_NI], x[FFT_NI:blk]
        hr, hi = h[0:FFT_NI], h[FFT_NI:blk]
        y = jnp.concatenate([xr * hr - xi * hi, xr * hi + xi * hr], axis=0)
        c = jnp.dot(m2c_ref[...], y.astype(BF16), preferred_element_type=F32)
        for j in range(2):
            w_ref[rows[j], :] = c[:, j * cb:(j + 1) * cb]
        return carry

    lax.fori_loop(0, FFT_NO // 2, mid, 0, unroll=FFT_UNROLL)

    def last(ni, carry):
        cr = w_ref[pl.ds(ni, FFT_NO, stride=blk), :]
        ci = w_ref[pl.ds(FFT_NI + ni, FFT_NO, stride=blk), :]
        y = jnp.dot(m3_ref[ni], jnp.concatenate([cr, ci], axis=0).astype(BF16), preferred_element_type=F32)
        o_ref[0, pl.ds(ni, FFT_HALF, stride=FFT_NI), :] = y[0:FFT_HALF]
        o_ref[1, pl.ds(ni, FFT_HALF, stride=FFT_NI), :] = y[FFT_HALF:2 * FFT_HALF]
        return carry

    lax.fori_loop(0, FFT_NI, last, 0, unroll=FFT_UNROLL)
    for b in range(2):
        gate = _short_conv(g_ref[b].astype(F32), cwg_ref, cbg_ref, DEC_SEQ)
        sig = z_ref[b]
        o_ref[b] = gate * (o_ref[b] + sig * bias_ref[...])


def _lconv_two_stage(sig, sig_col, gate_src, gate_col, conv_w, conv_b, spec, l, order, bias, mats, conv_sig):
    cb = LCONV_CB
    ncb = BRANCH_W // cb
    m1, m2, m2c, m3 = mats
    const3 = lambda c, p: (0, 0, 0)
    const2 = lambda c, p: (0, 0)
    return pl.pallas_call(
        functools.partial(_lconv_two_stage_kernel, conv_sig=conv_sig),
        grid=(ncb, DEC_BATCH // 2),
        in_specs=[
            pl.BlockSpec((2, DEC_SEQ, cb), lambda c, p: (p, 0, sig_col * ncb + c)),
            pl.BlockSpec((2, DEC_SEQ, cb), lambda c, p: (p, 0, gate_col * ncb + c)),
            pl.BlockSpec((3, cb), lambda c, p: (0, c)),
            pl.BlockSpec((1, cb), lambda c, p: (0, c)),
            pl.BlockSpec((3, cb), lambda c, p: (0, gate_col * ncb + c)),
            pl.BlockSpec((1, cb), lambda c, p: (0, gate_col * ncb + c)),
            pl.BlockSpec((None, None, 2 * FFT_N, cb), lambda c, p: (l, order, 0, c)),
            pl.BlockSpec((1, cb), lambda c, p: (0, c)),
            pl.BlockSpec(m1.shape, const3),
            pl.BlockSpec(m2.shape, const2),
            pl.BlockSpec(m2c.shape, const2),
            pl.BlockSpec(m3.shape, const3),
        ],
        out_specs=pl.BlockSpec((2, DEC_SEQ, cb), lambda c, p: (p, 0, c)),
        out_shape=jax.ShapeDtypeStruct((DEC_BATCH, DEC_SEQ, BRANCH_W), F32),
        scratch_shapes=[pltpu.VMEM((2, DEC_SEQ, cb), F32), pltpu.VMEM((2 * FFT_N, cb), F32)],
        compiler_params=_cparams("arbitrary", "arbitrary"),
        name="hyena_lconv_two_stage",
    )(sig, gate_src, conv_w, conv_b, conv_w, conv_b, spec, bias, m1, m2, m2c, m3)


def kernel(x_prompt, x_sample, cache_na_k, cache_na_v, cache_da_k, cache_da_v, c, c_ctx, w_ada, b_ada, norm_mix,
           norm_ffn, w_in, hy_conv_w, hy_conv_b, hy_filt_w1, hy_filt_b1, hy_filt_w2, hy_filt_b2, hy_filt_w3,
           hy_filt_freq, hy_bias, na_rpb, da_lambda, da_subln, w_lift, w_out, w_ffn_in, w_ffn_out, norm_final):
    TP, TS = BATCH * SEQ, DEC_BATCH * DEC_SEQ
    xp = x_prompt.reshape(TP, D_MODEL)
    xs = x_sample.reshape(TS, D_MODEL)

    cc = jnp.concatenate([c_ctx[None, :], c, jnp.zeros((8 - 1 - DEC_BATCH, D_MODEL), F32)], axis=0)
    mod = _modulation(cc, w_ada, b_ada)
    mod_p = mod[:, 0:1].reshape(DEPTH, 1, 1, 6 * D_MODEL)
    mod_s = mod[:, 1:1 + DEC_BATCH].reshape(DEPTH, DEC_BATCH, 1, 6 * D_MODEL)

    w_mix = w_in[:, :, :MIX_W].astype(BF16)
    w_gate = w_in[:, :, MIX_W:].astype(BF16)
    w_lift_b = w_lift.astype(BF16)
    w_out_b = w_out.astype(BF16)
    w_ffn_in_b = w_ffn_in.astype(BF16)
    w_ffn_out_b = w_ffn_out.astype(BF16)
    g_mix = norm_mix.reshape(DEPTH, 1, D_MODEL)
    g_ffn = norm_ffn.reshape(DEPTH, 1, D_MODEL)
    g_fin = norm_final.reshape(1, D_MODEL)
    subln = da_subln.reshape(DEPTH, 1, DA_V_DIM)
    subln_col = da_subln.reshape(DEPTH, DA_V_DIM, 1)

    w1p = jnp.pad(hy_filt_w1, ((0, 0), (0, HY_FILT_HIDDEN - HY_POS_DIM), (0, 0)))
    b1 = hy_filt_b1.reshape(DEPTH, 1, HY_FILT_HIDDEN)
    b2 = hy_filt_b2.reshape(DEPTH, 1, HY_FILT_HIDDEN)
    fr = hy_filt_freq.reshape(DEPTH, 1, HY_FILT_HIDDEN)
    mf, mi = _dft_direct_mats()
    mats = _dft_two_stage_mats()
    h_p = _hyena_filters(SEQ, w1p, b1, hy_filt_w2, b2, hy_filt_w3, fr)
    h_s = _hyena_filters(DEC_SEQ, w1p, b1, hy_filt_w2, b2, hy_filt_w3, fr)
    spec_p = _spec_direct(h_p, jnp.asarray(_dft_real_mat()))
    mf_b, mi_b = jnp.asarray(mf, dtype=BF16), jnp.asarray(mi, dtype=BF16)
    mats_b = tuple(jnp.asarray(m, dtype=BF16) for m in mats)
    spec_s = _spec_two_stage(h_s, jnp.asarray(_dft_stage1_real_mat(), dtype=BF16), mats_b[1])
    conv_b = hy_conv_b.reshape(DEPTH, 1, 3 * BRANCH_W)

    na_bias = _na_bias_table(na_rpb)
    rope_tables = _rope_tables()
    ck_na = cache_na_k.reshape(DEC_BATCH, DEPTH, PAST_LEN, BRANCH_W)
    cv_na = cache_na_v.reshape(DEC_BATCH, DEPTH, PAST_LEN, BRANCH_W)
    ck_da = cache_da_k.reshape(DEC_BATCH, DEPTH, PAST_LEN, BRANCH_W)
    cv_da = cache_da_v.reshape(DEC_BATCH, DEPTH, PAST_LEN, BRANCH_W)

    caches = tuple(jnp.zeros((BATCH, DEPTH, SEQ, BRANCH_W), F32) for _ in CACHE_BLOCKS)
    for l in range(DEPTH):
        lam_init = 0.8 - 0.6 * math.exp(-0.3 * l)
        final = l == DEPTH - 1

        u, caches = _in_proj(xp, g_mix[l], mod_p[l], w_mix, l, TP, F32, caches=caches)
        z1 = _lconv_direct(u, 0, u, 1, hy_conv_w[l], conv_b[l], spec_p, l, 0, hy_bias[l, 0:1], mf_b, mi_b, True)
        y_hy = _lconv_direct(z1, 0, u, 2, hy_conv_w[l], conv_b[l], spec_p, l, 1, hy_bias[l, 1:2], mf_b, mi_b, False)
        y_na, y_da = _ctx_attention(u, da_lambda[l], subln_col[l], lam_init)
        xp = _merge_out(xp, g_mix[l], mod_p[l], y_hy, y_na, y_da, w_gate, w_lift_b, w_out_b, l, TP)
        xp = _ffn(xp, g_ffn[l], mod_p[l], w_ffn_in_b, w_ffn_out_b, g_fin, l, TP, final)

        u = _in_proj(xs, g_mix[l], mod_s[l], w_mix, l, DEC_SEQ, BF16)
        u3 = u.reshape(DEC_BATCH, DEC_SEQ, MIX_W)
        z1 = _lconv_two_stage(u3, 0, u3, 1, hy_conv_w[l], conv_b[l], spec_s, l, 0, hy_bias[l, 0:1], mats_b, True)
        y_hy = _lconv_two_stage(z1, 0, u3, 2, hy_conv_w[l], conv_b[l], spec_s, l, 1, hy_bias[l, 1:2], mats_b, False)
        y_hy = y_hy.reshape(TS, BRANCH_W)
        qn, kn, vn = _attn_prep(u, 3, ck_na[:, l], cv_na[:, l], NA_HEAD_DIM)
        y_na = _nbr_attention(qn, kn, vn, na_bias, l)
        q, kt, v = _attn_prep(u, 6, ck_da[:, l], cv_da[:, l], DA_HEAD_DIM, rope_tables)
        y_da = _diff_attention(q, kt, v, da_lambda[l], subln_col[l], lam_init)
        xs = _merge_out(xs, g_mix[l], mod_s[l], y_hy, y_na, y_da, w_gate, w_lift_b, w_out_b, l, DEC_SEQ)
        xs = _ffn(xs, g_ffn[l], mod_s[l], w_ffn_in_b, w_ffn_out_b, g_fin, l, DEC_SEQ, final)

    y_prompt = xp.reshape(BATCH, SEQ, D_MODEL)
    y_sample = xs.reshape(DEC_BATCH, DEC_SEQ, D_MODEL)
    heads = lambda a, d: a.reshape(BATCH, DEPTH, SEQ, BRANCH_W // d, d)
    return (y_prompt, y_sample, heads(caches[0], NA_HEAD_DIM), heads(caches[1], NA_HEAD_DIM),
            heads(caches[2], 2 * DA_HEAD_DIM), heads(caches[3], DA_V_DIM))
```

```python
import functools
import math

import numpy as np
import jax
import jax.numpy as jnp
from jax import lax
from jax.experimental import pallas as pl
from jax.experimental.pallas import tpu as pltpu

F32 = jnp.float32
BF16 = jnp.bfloat16
HIGHEST = lax.Precision.HIGHEST

D_MODEL = 1024
BATCH = 32
SEQ = 256
DEPTH = 4
DEC_BATCH = 4
DEC_SEQ = 4096
PAST_LEN = 256
GRID_W = 64
GRID_H = DEC_SEQ // GRID_W
BRANCH_W = 512
HY_POS_BANDS = 16
HY_POS_DIM = 1 + 2 * HY_POS_BANDS
HY_FILT_HIDDEN = 64
HY_DECAY_TARGET = 1e-2
HY_FAST_DECAY = 0.3
HY_SLOW_DECAY = 1.5
NA_HEADS = 8
NA_HEAD_DIM = 64
NA_WIN_ROWS = 8
NA_WIN_COLS = 16
DA_HEADS = 8
DA_HEAD_DIM = 32
DA_V_DIM = 64
D_FF = 2816
MIX_W = 9 * BRANCH_W
ROPE_BASE = 10000.0
EPS = 1e-6
NEG_INF = -1e30

VMEM_LIMIT_BYTES = 56 * 1024 * 1024

FFT_N = 2 * DEC_SEQ
FFT_NO = 64
FFT_NI = 128
FFT_HALF = FFT_NO // 2
FFT_UNROLL = 4
LCONV_CB = 128


def _cparams(*sem):
    return pltpu.CompilerParams(dimension_semantics=sem, vmem_limit_bytes=VMEM_LIMIT_BYTES)


def _sigmoid(x):
    return 1.0 / (1.0 + jnp.exp(-x))


def _rms(x, g):
    return x * lax.rsqrt(jnp.mean(x * x, axis=-1, keepdims=True) + EPS) * g


def _modnorm(x, g, shift, scale):
    return _rms(x, g) * (1.0 + scale) + shift


def _bdot(a, b):
    return jnp.dot(a.astype(BF16), b.astype(BF16), preferred_element_type=F32)


def _bdot_nt(a, b):
    return lax.dot_general(a.astype(BF16), b.astype(BF16), (((1,), (1,)), ((), ())),
                           preferred_element_type=F32)


def _mod_kernel(c_ref, w_ref, b_ref, o_ref):
    c = c_ref[...]
    s = c * _sigmoid(c)
    o_ref[0] = jnp.dot(s, w_ref[0], precision=HIGHEST, preferred_element_type=F32) + b_ref[0]


def _modulation(cc, w_ada, b_ada):
    nt = 6
    return pl.pallas_call(
        _mod_kernel,
        grid=(DEPTH, nt),
        in_specs=[
            pl.BlockSpec((8, D_MODEL), lambda l, j: (0, 0)),
            pl.BlockSpec((1, D_MODEL, D_MODEL), lambda l, j: (l, 0, j)),
            pl.BlockSpec((1, 1, D_MODEL), lambda l, j: (l, 0, j)),
        ],
        out_specs=pl.BlockSpec((1, 8, D_MODEL), lambda l, j: (l, 0, j)),
        out_shape=jax.ShapeDtypeStruct((DEPTH, 8, 6 * D_MODEL), F32),
        compiler_params=_cparams("arbitrary", "arbitrary"),
        name="modulation",
    )(cc, w_ada, b_ada.reshape(DEPTH, 1, 6 * D_MODEL))


IN_TM = 1024
IN_TN = 3 * BRANCH_W
CACHE_BLOCKS = (4, 5, 7, 8)


def _in_kernel(*refs, n_alias, cache_tiles):
    x_ref, g_ref, mod_ref, w_ref = refs[:4]
    o_ref = refs[4 + n_alias]
    cache_refs = refs[5 + n_alias:5 + n_alias + len(cache_tiles)]
    h_ref = refs[-1]
    j = pl.program_id(1)

    @pl.when(j == 0)
    def _():
        m = mod_ref[0]
        h = _modnorm(x_ref[...], g_ref[...], m[:, 0:D_MODEL], m[:, D_MODEL:2 * D_MODEL])
        h_ref[...] = h.astype(BF16)

    res = jnp.dot(h_ref[...], w_ref[...], preferred_element_type=F32)
    o_ref[...] = res.astype(o_ref.dtype)
    for (tile, off), c_ref in zip(cache_tiles, cache_refs):
        @pl.when(j == tile)
        def _(c_ref=c_ref, off=off):
            c_ref[...] = res[:, off:off + BRANCH_W].reshape(c_ref.shape)


def _in_proj(x, g, mod, w, l, rows_per_mod, out_dtype, caches=None):
    T = x.shape[0]
    tm, tn = IN_TM, IN_TN
    per = rows_per_mod // tm
    in_specs = [
        pl.BlockSpec((tm, D_MODEL), lambda i, j: (i, 0)),
        pl.BlockSpec((1, D_MODEL), lambda i, j: (0, 0)),
        pl.BlockSpec((1, 1, 6 * D_MODEL), lambda i, j: (i // per, 0, 0)),
        pl.BlockSpec((None, D_MODEL, tn), lambda i, j: (l, 0, j)),
    ]
    out_specs = [pl.BlockSpec((tm, tn), lambda i, j: (i, j))]
    out_shape = [jax.ShapeDtypeStruct((T, MIX_W), out_dtype)]
    args = [x, g, mod, w]
    cache_tiles, aliases = (), {}
    if caches is not None:
        seqs = tm // SEQ
        cache_tiles = tuple(divmod(c * BRANCH_W, tn) for c in CACHE_BLOCKS)
        out_specs += [pl.BlockSpec((seqs, 1, SEQ, BRANCH_W), lambda i, j: (i, l, 0, 0))] * len(CACHE_BLOCKS)
        out_shape += [jax.ShapeDtypeStruct(c.shape, c.dtype) for c in caches]
        in_specs += [pl.BlockSpec(memory_space=pl.ANY)] * len(caches)
        aliases = {4 + n: 1 + n for n in range(len(caches))}
        args += list(caches)
    n_alias = len(args) - 4
    outs = pl.pallas_call(
        functools.partial(_in_kernel, n_alias=n_alias, cache_tiles=cache_tiles),
        grid=(T // tm, MIX_W // tn),
        in_specs=in_specs,
        out_specs=out_specs,
        out_shape=out_shape,
        input_output_aliases=aliases,
        scratch_shapes=[pltpu.VMEM((tm, D_MODEL), BF16)],
        compiler_params=_cparams("arbitrary", "arbitrary"),
        name="in_proj",
    )(*args)
    return outs[0] if caches is None else (outs[0], tuple(outs[1:]))


def _mid_kernel(x_ref, g_ref, mod_ref, yh_ref, yn_ref, yd_ref, wg_ref, wl_ref, wo_ref, o_ref):
    m = mod_ref[0]
    x = x_ref[...]
    h = _modnorm(x, g_ref[...], m[:, 0:D_MODEL], m[:, D_MODEL:2 * D_MODEL]).astype(BF16)
    merged = None
    for br, y_ref in enumerate((yh_ref, yn_ref, yd_ref)):
        gate = _sigmoid(jnp.dot(h, wg_ref[:, br * D_MODEL:(br + 1) * D_MODEL], preferred_element_type=F32))
        lift = jnp.dot(y_ref[...].astype(BF16), wl_ref[br], preferred_element_type=F32)
        t = gate * lift
        merged = t if merged is None else merged + t
    o_ref[...] = x + m[:, 2 * D_MODEL:3 * D_MODEL] * _bdot(merged, wo_ref[...])


def _merge_out(x, g, mod, y_hy, y_na, y_da, w_gate, w_lift, w_out, l, rows_per_mod):
    T = x.shape[0]
    tm = 512
    per = rows_per_mod // tm
    row = lambda i: (i, 0)
    const2 = lambda i: (0, 0)
    return pl.pallas_call(
        _mid_kernel,
        grid=(T // tm,),
        in_specs=[
            pl.BlockSpec((tm, D_MODEL), row),
            pl.BlockSpec((1, D_MODEL), const2),
            pl.BlockSpec((1, 1, 6 * D_MODEL), lambda i: (i // per, 0, 0)),
            pl.BlockSpec((tm, BRANCH_W), row),
            pl.BlockSpec((tm, BRANCH_W), row),
            pl.BlockSpec((tm, BRANCH_W), row),
            pl.BlockSpec((None, D_MODEL, 3 * D_MODEL), lambda i: (l, 0, 0)),
            pl.BlockSpec((None, 3, BRANCH_W, D_MODEL), lambda i: (l, 0, 0, 0)),
            pl.BlockSpec((None, D_MODEL, D_MODEL), lambda i: (l, 0, 0)),
        ],
        out_specs=pl.BlockSpec((tm, D_MODEL), row),
        out_shape=jax.ShapeDtypeStruct((T, D_MODEL), F32),
        compiler_params=_cparams("arbitrary"),
        name="merge_out",
    )(x, g, mod, y_hy, y_na, y_da, w_gate, w_lift, w_out)


FFN_CHUNK = D_FF // 2


def _ffn_kernel(x_ref, g_ref, mod_ref, w1g_ref, w1u_ref, w2_ref, gf_ref, o_ref, h_ref, acc_ref, *, final):
    k = pl.program_id(1)

    @pl.when(k == 0)
    def _():
        m = mod_ref[0]
        h = _modnorm(x_ref[...], g_ref[...], m[:, 3 * D_MODEL:4 * D_MODEL], m[:, 4 * D_MODEL:5 * D_MODEL])
        h_ref[...] = h.astype(BF16)

    h = h_ref[...]
    a = jnp.dot(h, w1g_ref[...], preferred_element_type=F32)
    b = jnp.dot(h, w1u_ref[...], preferred_element_type=F32)
    part = _bdot(a * _sigmoid(a) * b, w2_ref[...])

    @pl.when(k == 0)
    def _():
        acc_ref[...] = part

    @pl.when(k == 1)
    def _():
        m = mod_ref[0]
        xn = x_ref[...] + m[:, 5 * D_MODEL:6 * D_MODEL] * (acc_ref[...] + part)
        if final:
            xn = _rms(xn, gf_ref[...])
        o_ref[...] = xn


def _ffn(x, g, mod, w_ffn_in, w_ffn_out, g_final, l, rows_per_mod, final):
    T = x.shape[0]
    tm = 512
    per = rows_per_mod // tm
    return pl.pallas_call(
        functools.partial(_ffn_kernel, final=final),
        grid=(T // tm, 2),
        in_specs=[
            pl.BlockSpec((tm, D_MODEL), lambda i, k: (i, 0)),
            pl.BlockSpec((1, D_MODEL), lambda i, k: (0, 0)),
            pl.BlockSpec((1, 1, 6 * D_MODEL), lambda i, k: (i // per, 0, 0)),
            pl.BlockSpec((None, D_MODEL, FFN_CHUNK), lambda i, k: (l, 0, k)),
            pl.BlockSpec((None, D_MODEL, FFN_CHUNK), lambda i, k: (l, 0, 2 + k)),
            pl.BlockSpec((None, FFN_CHUNK, D_MODEL), lambda i, k: (l, k, 0)),
            pl.BlockSpec((1, D_MODEL), lambda i, k: (0, 0)),
        ],
        out_specs=pl.BlockSpec((tm, D_MODEL), lambda i, k: (i, 0)),
        out_shape=jax.ShapeDtypeStruct((T, D_MODEL), F32),
        scratch_shapes=[pltpu.VMEM((tm, D_MODEL), BF16), pltpu.VMEM((tm, D_MODEL), F32)],
        compiler_params=_cparams("arbitrary", "arbitrary"),
        name="ffn",
    )(x, g, mod, w_ffn_in, w_ffn_in, w_ffn_out, g_final)


def _da_lambda(lam_ref, lam_init):
    lp = lam_ref[...]
    a = jnp.sum(lp[0:1] * lp[1:2], axis=1, keepdims=True)
    b = jnp.sum(lp[2:3] * lp[3:4], axis=1, keepdims=True)
    return jnp.exp(a) - jnp.exp(b) + lam_init


def _softmax_rows(s):
    m = jnp.max(s, axis=-1, keepdims=True)
    p = jnp.exp(s - m)
    return p, jnp.sum(p, axis=-1, keepdims=True)


ATT_ONES_ROWS = 16
ATT_TQ = 256
ATT_KEYS = DEC_SEQ + PAST_LEN
LOG2E = math.log2(math.e)


def _masked_q_blocks(qt, d):
    row = lax.broadcasted_iota(jnp.int32, qt.shape, 0)
    zero = jnp.zeros_like(qt)
    return jnp.concatenate([jnp.where((row >= j * d) & (row < (j + 1) * d), qt, zero) for j in range(128 // d)], axis=1)


def _colmax(st):
    keys, n = st.shape
    return jnp.max(jnp.max(st.reshape(keys // 256, 256, n), axis=0), axis=0, keepdims=True)


def _ctx_attn_kernel(nq_ref, nk_ref, nv_ref, dq_ref, dk_ref, dv_ref, lam_ref, sub_ref, yn_ref, yd_ref, acc_ref,
                     *, lam_init):
    lam = _da_lambda(lam_ref, lam_init)
    ones = jnp.ones((ATT_ONES_ROWS, SEQ), BF16)

    def attend(q_ref, k_ref, v_ref, d, maps_per_head, finish):
        qt = (q_ref[...] * (d ** -0.5 * LOG2E)).T.astype(BF16)
        vt = v_ref[...].T.astype(BF16)
        kb = k_ref[...].astype(BF16)
        dv = NA_HEAD_DIM
        heads_per_group = 128 // (d * maps_per_head)
        w = maps_per_head * SEQ
        for g in range(BRANCH_W // 128):
            st = jnp.dot(kb[:, g * 128:(g + 1) * 128], _masked_q_blocks(qt[g * 128:(g + 1) * 128], d),
                         preferred_element_type=F32)
            pt = jnp.exp2(st - _colmax(st)).astype(BF16)
            for j in range(heads_per_group):
                h = g * heads_per_group + j
                ve = jnp.concatenate([vt[h * dv:(h + 1) * dv], ones], axis=0)
                oe = jnp.dot(ve, pt[:, j * w:(j + 1) * w], preferred_element_type=F32)
                os = [oe[0:dv, i * SEQ:(i + 1) * SEQ] / oe[dv:dv + 1, i * SEQ:(i + 1) * SEQ]
                      for i in range(maps_per_head)]
                acc_ref[h * dv:(h + 1) * dv, :] = finish(os)

    attend(nq_ref, nk_ref, nv_ref, NA_HEAD_DIM, 1, lambda os: os[0])
    yn_ref[...] = acc_ref[...].T

    def da_finish(os):
        ot = os[0] - lam * os[1]
        ot = ot * lax.rsqrt(jnp.mean(ot * ot, axis=0, keepdims=True) + EPS) * sub_ref[...]
        return ot * (1.0 - lam_init)

    attend(dq_ref, dk_ref, dv_ref, DA_HEAD_DIM, 2, da_finish)
    yd_ref[...] = acc_ref[...].T


def _ctx_attention(u, da_lambda, subln_col, lam_init):
    col = lambda j: pl.BlockSpec((SEQ, BRANCH_W), lambda b, j=j: (b, j))
    out = pl.BlockSpec((SEQ, BRANCH_W), lambda b: (b, 0))
    shape = jax.ShapeDtypeStruct((BATCH * SEQ, BRANCH_W), F32)
    return pl.pallas_call(
        functools.partial(_ctx_attn_kernel, lam_init=lam_init),
        grid=(BATCH,),
        in_specs=[col(3), col(4), col(5), col(6), col(7), col(8),
                  pl.BlockSpec((4, DA_HEAD_DIM), lambda b: (0, 0)),
                  pl.BlockSpec((DA_V_DIM, 1), lambda b: (0, 0))],
        out_specs=[out, out],
        out_shape=[shape, shape],
        scratch_shapes=[pltpu.VMEM((BRANCH_W, SEQ), F32)],
        compiler_params=_cparams("arbitrary"),
        name="ctx_attention",
    )(u, u, u, u, u, u, da_lambda, subln_col)


def _rope(x, cos, sin_signed):
    n = x.shape[-1]
    lane = lax.broadcasted_iota(jnp.int32, x.shape, 1)
    partner = jnp.where(lane % 2 == 0, pltpu.roll(x, n - 1, axis=1), pltpu.roll(x, 1, axis=1))
    return x * cos + partner * sin_signed


def _attn_prep_kernel(q_ref, k_ref, v_ref, kc_ref, vc_ref, *refs, rope, scale):
    cos_ref, sin_ref = refs[:2] if rope else (None, None)
    qt_ref, ko_ref, vt_ref = refs[-3:]
    t = pl.program_id(1)
    dv = NA_HEAD_DIM

    def put_v(v):
        vt = v.astype(F32).T.astype(BF16)
        ones = jnp.ones((ATT_ONES_ROWS, ATT_TQ), BF16)
        for h in range(BRANCH_W // dv):
            vt_ref[0, h, 0:dv, :] = vt[h * dv:(h + 1) * dv]
            vt_ref[0, h, dv:dv + ATT_ONES_ROWS, :] = ones

    @pl.when(t < DEC_SEQ // ATT_TQ)
    def _():
        q = q_ref[...].astype(F32)
        k = k_ref[...].astype(F32)
        if rope:
            q = _rope(q, cos_ref[...], sin_ref[...])
            k = _rope(k, cos_ref[...], sin_ref[...])
        qt_ref[0] = (q * scale).T.astype(BF16)
        ko_ref[0] = k.astype(BF16)
        put_v(v_ref[...])

    @pl.when(t == DEC_SEQ // ATT_TQ)
    def _():
        ko_ref[0] = kc_ref[0].astype(BF16)
        put_v(vc_ref[0])


def _attn_prep(u, first_col, k_ctx, v_ctx, head_dim, rope_tables=None):
    rope = rope_tables is not None
    nt = DEC_SEQ // ATT_TQ
    last = nt - 1
    rowblk = lambda j: pl.BlockSpec((ATT_TQ, BRANCH_W), lambda b, t, j=j: (b * nt + jnp.minimum(t, last), j))
    tab = pl.BlockSpec((ATT_TQ, BRANCH_W), lambda b, t: (jnp.minimum(t, last), 0))
    ctx = pl.BlockSpec((1, PAST_LEN, BRANCH_W), lambda b, t: (b, 0, 0))
    heads = BRANCH_W // NA_HEAD_DIM
    vrows = NA_HEAD_DIM + ATT_ONES_ROWS
    return pl.pallas_call(
        functools.partial(_attn_prep_kernel, rope=rope, scale=head_dim ** -0.5 * LOG2E),
        grid=(DEC_BATCH, nt + 1),
        in_specs=[rowblk(first_col), rowblk(first_col + 1), rowblk(first_col + 2), ctx, ctx] + [tab, tab] * rope,
        out_specs=[
            pl.BlockSpec((1, BRANCH_W, ATT_TQ), lambda b, t: (b, 0, jnp.minimum(t, last))),
            pl.BlockSpec((1, ATT_TQ, BRANCH_W), lambda b, t: (b, t, 0)),
            pl.BlockSpec((1, heads, vrows, ATT_TQ), lambda b, t: (b, 0, 0, t)),
        ],
        out_shape=[
            jax.ShapeDtypeStruct((DEC_BATCH, BRANCH_W, DEC_SEQ), BF16),
            jax.ShapeDtypeStruct((DEC_BATCH, ATT_KEYS, BRANCH_W), BF16),
            jax.ShapeDtypeStruct((DEC_BATCH, heads, vrows, ATT_KEYS), BF16),
        ],
        compiler_params=_cparams("arbitrary", "arbitrary"),
        name="attn_prep",
    )(u, u, u, k_ctx, v_ctx, *(rope_tables or ()))


NA_ROWS = ATT_TQ // GRID_W
NA_UNION = 3 * NA_ROWS
NA_STEPS = GRID_H // NA_ROWS
NA_SLABS = NA_UNION // NA_ROWS
NA_VARIANT_OFFSET = (0, -NA_ROWS, -2 * NA_ROWS)


def _na_variant(s):
    return jnp.minimum(s, 1) + s // (NA_STEPS - 1)


def _na_window_block(s):
    return jnp.clip(s - 1, 0, NA_STEPS - NA_SLABS)


def _na_bias_kernel(rpb_ref, o_ref):
    kc = lax.broadcasted_iota(jnp.int32, (GRID_W, GRID_W), 0)
    qc = lax.broadcasted_iota(jnp.int32, (GRID_W, GRID_W), 1)
    dc = jnp.clip(kc - qc, -(NA_WIN_COLS - 1), NA_WIN_COLS - 1) + (NA_WIN_COLS - 1)
    c0 = jnp.clip(qc - NA_WIN_COLS // 2, 0, GRID_W - NA_WIN_COLS)
    col_ok = (kc >= c0) & (kc < c0 + NA_WIN_COLS)
    r = rpb_ref[0, 0] * LOG2E
    masked = jnp.full((GRID_W, GRID_W), NEG_INF, F32)
    tiles = []
    for dr in range(2 * NA_WIN_ROWS - 1):
        acc = jnp.zeros((GRID_W, GRID_W), F32)
        for d in range(2 * NA_WIN_COLS - 1):
            acc = jnp.where(dc == d, r[dr:dr + 1, d:d + 1], acc)
        tiles.append(jnp.where(col_ok, acc, masked))
    for v, off in enumerate(NA_VARIANT_OFFSET):
        for kr in range(NA_UNION):
            for rr in range(NA_ROWS):
                w0 = (0, rr, NA_UNION - NA_WIN_ROWS)[v]
                dr = kr + off - rr
                inside = w0 <= kr < w0 + NA_WIN_ROWS
                o_ref[0, v, 0, kr * GRID_W:(kr + 1) * GRID_W, rr * GRID_W:(rr + 1) * GRID_W] = (
                    tiles[dr + NA_WIN_ROWS - 1] if inside else masked)


def _na_bias_table(na_rpb):
    n_dr, n_dc = 2 * NA_WIN_ROWS - 1, 2 * NA_WIN_COLS - 1
    nv = len(NA_VARIANT_OFFSET)
    return pl.pallas_call(
        _na_bias_kernel,
        grid=(DEPTH, NA_HEADS),
        in_specs=[pl.BlockSpec((1, 1, n_dr, n_dc), lambda l, h: (l, h, 0, 0))],
        out_specs=pl.BlockSpec((1, nv, 1, NA_UNION * GRID_W, ATT_TQ), lambda l, h: (l, 0, h, 0, 0)),
        out_shape=jax.ShapeDtypeStruct((DEPTH, nv, NA_HEADS, NA_UNION * GRID_W, ATT_TQ), F32),
        compiler_params=_cparams("arbitrary", "arbitrary"),
        name="na_bias_table",
    )(na_rpb)


def _na_kernel(qt_ref, *refs):
    n = NA_SLABS + 1
    k_refs, vt_refs = refs[:n], refs[n:2 * n]
    bias_ref, o_ref, acc_ref = refs[2 * n:]
    dv = NA_HEAD_DIM
    heads_per_group = 128 // dv
    for g in range(BRANCH_W // 128):
        lanes = slice(g * 128, (g + 1) * 128)
        qbd = _masked_q_blocks(qt_ref[0, lanes, :], dv)
        sts = []
        for j, k_ref in enumerate(k_refs):
            st = jnp.dot(k_ref[0, :, lanes], qbd, preferred_element_type=F32)
            if j < NA_SLABS:
                rows = slice(j * ATT_TQ, (j + 1) * ATT_TQ)
                st = st + jnp.concatenate(
                    [bias_ref[0, g * heads_per_group + hh, rows, :] for hh in range(heads_per_group)], axis=1)
            sts.append(st)
        mx = functools.reduce(jnp.maximum, [_colmax(st) for st in sts])
        pts = [jnp.exp2(st - mx).astype(BF16) for st in sts]
        for hh in range(heads_per_group):
            h = g * heads_per_group + hh
            oe = sum(jnp.dot(vt_ref[0, h], pt[:, hh * ATT_TQ:(hh + 1) * ATT_TQ], preferred_element_type=F32)
                     for vt_ref, pt in zip(vt_refs, pts))
            acc_ref[h * dv:(h + 1) * dv, :] = oe[0:dv] / oe[dv:dv + 1]
    o_ref[...] = acc_ref[...].T


def _nbr_attention(qt, k, vt, bias, l):
    vrows = NA_HEAD_DIM + ATT_ONES_ROWS
    ctx_blk = DEC_SEQ // ATT_TQ
    k_specs = [pl.BlockSpec((1, ATT_TQ, BRANCH_W), lambda b, s, j=j: (b, _na_window_block(s) + j, 0))
               for j in range(NA_SLABS)]
    k_specs.append(pl.BlockSpec((1, ATT_TQ, BRANCH_W), lambda b, s: (b, ctx_blk, 0)))
    vt_specs = [pl.BlockSpec((1, NA_HEADS, vrows, ATT_TQ), lambda b, s, j=j: (b, 0, 0, _na_window_block(s) + j))
                for j in range(NA_SLABS)]
    vt_specs.append(pl.BlockSpec((1, NA_HEADS, vrows, ATT_TQ), lambda b, s: (b, 0, 0, ctx_blk)))
    n = NA_SLABS + 1
    return pl.pallas_call(
        _na_kernel,
        grid=(DEC_BATCH, NA_STEPS),
        in_specs=[pl.BlockSpec((1, BRANCH_W, ATT_TQ), lambda b, s: (b, 0, s))] + k_specs + vt_specs + [
            pl.BlockSpec((None, 1, NA_HEADS, NA_UNION * GRID_W, ATT_TQ), lambda b, s: (l, _na_variant(s), 0, 0, 0))],
        out_specs=pl.BlockSpec((ATT_TQ, BRANCH_W), lambda b, s: (b * NA_STEPS + s, 0)),
        out_shape=jax.ShapeDtypeStruct((DEC_BATCH * DEC_SEQ, BRANCH_W), F32),
        scratch_shapes=[pltpu.VMEM((BRANCH_W, ATT_TQ), F32)],
        compiler_params=_cparams("arbitrary", "arbitrary"),
        name="nbr_attention",
    )(qt, *([k] * n), *([vt] * n), bias)


DA_TQ = ATT_TQ
DA_KEYS = ATT_KEYS
DA_ONES_ROWS = ATT_ONES_ROWS
DA_MAPS_PER_TILE = 128 // DA_HEAD_DIM


def _da_kernel(qt_ref, k_ref, vt_ref, lam_ref, sub_ref, o_ref, acc_ref, *, lam_init):
    lam = _da_lambda(lam_ref, lam_init)
    heads = DA_MAPS_PER_TILE // 2
    for g in range(BRANCH_W // 128):
        lanes = slice(g * 128, (g + 1) * 128)
        st = jnp.dot(k_ref[0, :, lanes], _masked_q_blocks(qt_ref[0, lanes, :], DA_HEAD_DIM),
                     preferred_element_type=F32)
        pt = jnp.exp2(st - _colmax(st)).astype(BF16)
        for hh in range(heads):
            h = g * heads + hh
            oe = jnp.dot(vt_ref[0, h], pt[:, 2 * hh * DA_TQ:(2 * hh + 2) * DA_TQ], preferred_element_type=F32)
            os = [oe[0:DA_V_DIM, i * DA_TQ:(i + 1) * DA_TQ] / oe[DA_V_DIM:DA_V_DIM + 1, i * DA_TQ:(i + 1) * DA_TQ]
                  for i in range(2)]
            ot = os[0] - lam * os[1]
            ot = ot * lax.rsqrt(jnp.mean(ot * ot, axis=0, keepdims=True) + EPS) * sub_ref[...]
            acc_ref[h * DA_V_DIM:(h + 1) * DA_V_DIM, :] = ot * (1.0 - lam_init)
    o_ref[...] = acc_ref[...].T


def _diff_attention(qt, k, vt, da_lambda, subln_col, lam_init):
    nt = DEC_SEQ // DA_TQ
    vrows = DA_V_DIM + DA_ONES_ROWS
    return pl.pallas_call(
        functools.partial(_da_kernel, lam_init=lam_init),
        grid=(DEC_BATCH, nt),
        in_specs=[
            pl.BlockSpec((1, BRANCH_W, DA_TQ), lambda b, t: (b, 0, t)),
            pl.BlockSpec((1, DA_KEYS, BRANCH_W), lambda b, t: (b, 0, 0)),
            pl.BlockSpec((1, DA_HEADS, vrows, DA_KEYS), lambda b, t: (b, 0, 0, 0)),
            pl.BlockSpec((4, DA_HEAD_DIM), lambda b, t: (0, 0)),
            pl.BlockSpec((DA_V_DIM, 1), lambda b, t: (0, 0)),
        ],
        out_specs=pl.BlockSpec((DA_TQ, BRANCH_W), lambda b, t: (b * nt + t, 0)),
        out_shape=jax.ShapeDtypeStruct((DEC_BATCH * DEC_SEQ, BRANCH_W), F32),
        scratch_shapes=[pltpu.VMEM((BRANCH_W, DA_TQ), F32)],
        compiler_params=_cparams("arbitrary", "arbitrary"),
        name="diff_attention",
    )(qt, k, vt, da_lambda, subln_col)


def _rope_tables():
    pos = np.arange(DEC_SEQ)
    row = (pos // GRID_W).astype(np.float32)
    col = (pos % GRID_W).astype(np.float32)
    n_freq = DA_HEAD_DIM // 4
    inv = (np.float32(ROPE_BASE) ** (-np.arange(n_freq, dtype=np.float32) / n_freq)).astype(np.float32)
    ang = np.concatenate([row[:, None] * inv[None, :], col[:, None] * inv[None, :]], axis=-1)
    ang = ang.astype(np.float64)
    cos = np.repeat(np.cos(ang), 2, axis=-1)
    sin = np.repeat(np.sin(ang), 2, axis=-1)
    sign = np.where(np.arange(DA_HEAD_DIM) % 2 == 0, -1.0, 1.0)
    reps = BRANCH_W // DA_HEAD_DIM
    cos = np.tile(cos, (1, reps)).astype(np.float32)
    sin = np.tile(sin * sign[None, :], (1, reps)).astype(np.float32)
    return jnp.asarray(cos), jnp.asarray(sin)


def _filt_hidden_kernel(feat_ref, w1_ref, b1_ref, w2_ref, b2_ref, fr_ref, o_ref):
    fr = fr_ref[0]
    h = jnp.sin(fr * (jnp.dot(feat_ref[...], w1_ref[0], precision=HIGHEST, preferred_element_type=F32) + b1_ref[0]))
    o_ref[0] = jnp.sin(fr * (jnp.dot(h, w2_ref[0], precision=HIGHEST, preferred_element_type=F32) + b2_ref[0]))


def _filt_kernel(h_ref, w3f_ref, w3b_ref, dec_ref, o_ref):
    L = dec_ref.shape[0] // 2
    hf = jnp.dot(h_ref[0, 0:L], w3f_ref[0], precision=HIGHEST, preferred_element_type=F32) * dec_ref[0:L]
    hb = jnp.dot(h_ref[0, L:2 * L], w3b_ref[0], precision=HIGHEST, preferred_element_type=F32) * dec_ref[L:2 * L]
    row = lax.broadcasted_iota(jnp.int32, hb.shape, 0)
    hb = jnp.where(row == 0, 0.0, hb)
    nrm = jnp.sum(jnp.abs(hf), axis=0, keepdims=True) + jnp.sum(jnp.abs(hb), axis=0, keepdims=True)
    o_ref[0, 0, 0:L] = hf / nrm
    o_ref[0, 0, L:2 * L] = hb / nrm


def _circular_order(a):
    return np.concatenate([a, a[:1], a[1:][::-1]], axis=0)


def _hyena_pos_tables(L):
    f32 = np.float32
    pos = np.arange(L, dtype=f32)
    t = (pos / f32(L)).astype(f32)
    bands = np.linspace(1e-4, HY_POS_BANDS - 1, HY_POS_BANDS, dtype=f32)
    ang = (f32(2 * math.pi / L) * pos[:, None] * bands[None, :]).astype(np.float64)
    feats = np.zeros((L, HY_FILT_HIDDEN), f32)
    feats[:, 0] = t
    feats[:, 1:1 + HY_POS_BANDS] = np.cos(ang)
    feats[:, 1 + HY_POS_BANDS:HY_POS_DIM] = -np.sin(ang)
    deltas = np.linspace(math.log(HY_DECAY_TARGET) / HY_SLOW_DECAY,
                         math.log(HY_DECAY_TARGET) / HY_FAST_DECAY, BRANCH_W, dtype=f32)
    decay = np.exp((-t[:, None] * np.abs(deltas)[None, :]).astype(np.float64)).astype(f32)
    return jnp.asarray(_circular_order(feats)), jnp.asarray(_circular_order(decay))


def _hyena_filters(half, w1p, b1, w2, b2, w3, freq):
    feats, decay = _hyena_pos_tables(half)
    L = 2 * half
    cb = 128
    ncb = BRANCH_W // cb
    small = lambda shape: pl.BlockSpec((1,) + shape, lambda l: (l, 0, 0))
    hidden = pl.pallas_call(
        _filt_hidden_kernel,
        grid=(DEPTH,),
        in_specs=[
            pl.BlockSpec((L, HY_FILT_HIDDEN), lambda l: (0, 0)),
            small((HY_FILT_HIDDEN, HY_FILT_HIDDEN)), small((1, HY_FILT_HIDDEN)),
            small((HY_FILT_HIDDEN, HY_FILT_HIDDEN)), small((1, HY_FILT_HIDDEN)),
            small((1, HY_FILT_HIDDEN)),
        ],
        out_specs=pl.BlockSpec((1, L, HY_FILT_HIDDEN), lambda l: (l, 0, 0)),
        out_shape=jax.ShapeDtypeStruct((DEPTH, L, HY_FILT_HIDDEN), F32),
        compiler_params=_cparams("arbitrary"),
        name=f"hyena_filter_hidden_{L}",
    )(feats, w1p, b1, w2, b2, freq)
    return pl.pallas_call(
        _filt_kernel,
        grid=(DEPTH, 2, ncb),
        in_specs=[
            pl.BlockSpec((1, L, HY_FILT_HIDDEN), lambda l, o, c: (l, 0, 0)),
            pl.BlockSpec((1, HY_FILT_HIDDEN, cb), lambda l, o, c: (l, 0, o * 2 * ncb + c)),
            pl.BlockSpec((1, HY_FILT_HIDDEN, cb), lambda l, o, c: (l, 0, o * 2 * ncb + ncb + c)),
            pl.BlockSpec((L, cb), lambda l, o, c: (0, c)),
        ],
        out_specs=pl.BlockSpec((1, 1, L, cb), lambda l, o, c: (l, o, 0, c)),
        out_shape=jax.ShapeDtypeStruct((DEPTH, 2, L, BRANCH_W), F32),
        compiler_params=_cparams("arbitrary", "arbitrary", "arbitrary"),
        name=f"hyena_filters_{L}",
    )(hidden, w3, w3, decay)


def _short_conv(u, w_ref, b_ref, seq_len):
    n = u.shape[0]
    t = lax.broadcasted_iota(jnp.int32, u.shape, 0) % seq_len
    prev = jnp.where(t == 0, 0.0, pltpu.roll(u, 1, axis=0))
    nxt = jnp.where(t == seq_len - 1, 0.0, pltpu.roll(u, n - 1, axis=0))
    return prev * w_ref[0:1, :] + u * w_ref[1:2, :] + nxt * w_ref[2:3, :] + b_ref[...]


def _dft_direct_mats():
    n, half = 2 * SEQ, SEQ
    k = np.arange(n)[:, None].astype(np.float64)
    t = np.arange(half)[None, :].astype(np.float64)
    ang = 2 * np.pi * k * t / n
    fr, fi = np.cos(ang), -np.sin(ang)
    mf = np.block([[fr, -fi], [fi, fr]])
    gr, gi = np.cos(ang).T / n, np.sin(ang).T / n
    mi = np.block([[gr, -gi], [gi, gr]])
    return mf.astype(np.float32), mi.astype(np.float32)


def _dft_real_mat():
    n = 2 * SEQ
    ang = 2 * np.pi * np.arange(n)[:, None].astype(np.float64) * np.arange(n)[None, :] / n
    return np.concatenate([np.cos(ang), -np.sin(ang)], axis=0).astype(np.float32)


def _spec_direct_kernel(h_ref, m_ref, o_ref):
    o_ref[0, 0] = jnp.dot(m_ref[...], h_ref[0, 0], precision=HIGHEST, preferred_element_type=F32)


def _spec_direct(h, m_real):
    n = 2 * SEQ
    return pl.pallas_call(
        _spec_direct_kernel,
        grid=(DEPTH, 2),
        in_specs=[pl.BlockSpec((1, 1, n, BRANCH_W), lambda l, o: (l, o, 0, 0)),
                  pl.BlockSpec((2 * n, n), lambda l, o: (0, 0))],
        out_specs=pl.BlockSpec((1, 1, 2 * n, BRANCH_W), lambda l, o: (l, o, 0, 0)),
        out_shape=jax.ShapeDtypeStruct((DEPTH, 2, 2 * n, BRANCH_W), F32),
        compiler_params=_cparams("arbitrary", "arbitrary"),
        name="hyena_spectrum_direct",
    )(h, m_real)


def _lconv_direct_kernel(s_ref, g_ref, cws_ref, cbs_ref, cwg_ref, cbg_ref, h_ref, bias_ref, mf_ref, mi_ref, o_ref,
                         *, conv_sig):
    n = 2 * SEQ
    sig = s_ref[...]
    if conv_sig:
        sig = _short_conv(sig, cws_ref, cbs_ref, SEQ)
    gate = _short_conv(g_ref[...], cwg_ref, cbg_ref, SEQ)
    z = jnp.dot(mf_ref[...], sig.astype(BF16), preferred_element_type=F32)
    zr, zi = z[0:n], z[n:2 * n]
    hr, hi = h_ref[0:n], h_ref[n:2 * n]
    y = jnp.concatenate([zr * hr - zi * hi, zr * hi + zi * hr], axis=0)
    y = jnp.dot(mi_ref[...], y.astype(BF16), preferred_element_type=F32)
    o_ref[...] = gate * (y + sig * bias_ref[...])


def _lconv_direct(sig, sig_col, gate_src, gate_col, conv_w, conv_b, spec, l, order, bias, mf, mi, conv_sig):
    n = 2 * SEQ
    rows = 2 * SEQ
    T = sig.shape[0]
    return pl.pallas_call(
        functools.partial(_lconv_direct_kernel, conv_sig=conv_sig),
        grid=(T // rows,),
        in_specs=[
            pl.BlockSpec((rows, BRANCH_W), lambda p: (p, sig_col)),
            pl.BlockSpec((rows, BRANCH_W), lambda p: (p, gate_col)),
            pl.BlockSpec((3, BRANCH_W), lambda p: (0, 0)),
            pl.BlockSpec((1, BRANCH_W), lambda p: (0, 0)),
            pl.BlockSpec((3, BRANCH_W), lambda p: (0, gate_col)),
            pl.BlockSpec((1, BRANCH_W), lambda p: (0, gate_col)),
            pl.BlockSpec((None, None, 2 * n, BRANCH_W), lambda p: (l, order, 0, 0)),
            pl.BlockSpec((1, BRANCH_W), lambda p: (0, 0)),
            pl.BlockSpec((2 * n, rows), lambda p: (0, 0)),
            pl.BlockSpec((rows, 2 * n), lambda p: (0, 0)),
        ],
        out_specs=pl.BlockSpec((rows, BRANCH_W), lambda p: (p, 0)),
        out_shape=jax.ShapeDtypeStruct((T, BRANCH_W), F32),
        compiler_params=_cparams("arbitrary"),
        name="hyena_lconv_direct",
    )(sig, gate_src, conv_w, conv_b, conv_w, conv_b, spec, bias, mf, mi)


def _dft_two_stage_mats():
    no, ni, half, n = FFT_NO, FFT_NI, FFT_HALF, FFT_N
    f64 = np.float64
    k1 = np.arange(no, dtype=f64)
    n_o = np.arange(half, dtype=f64)
    n_i = np.arange(ni, dtype=f64)
    ang = 2 * np.pi * (n_i[:, None, None] * k1[None, :, None] / n + k1[None, :, None] * n_o[None, None, :] / no)
    tr, ti = np.cos(ang), -np.sin(ang)
    m1 = np.concatenate([np.concatenate([tr, -ti], axis=2), np.concatenate([ti, tr], axis=2)], axis=1)
    k2 = np.arange(ni, dtype=f64)
    ang2 = 2 * np.pi * k2[:, None] * n_i[None, :] / ni
    f2r, f2i = np.cos(ang2), -np.sin(ang2)
    m2 = np.block([[f2r, -f2i], [f2i, f2r]])
    m2c = np.block([[f2r, f2i], [-f2i, f2r]])
    sr, si = np.transpose(tr, (0, 2, 1)) / n, -np.transpose(ti, (0, 2, 1)) / n
    m3 = np.concatenate([np.concatenate([sr, -si], axis=2), np.concatenate([si, sr], axis=2)], axis=1)
    return (m1.astype(np.float32), m2.astype(np.float32), m2c.astype(np.float32), m3.astype(np.float32))


def _dft_stage1_real_mat():
    no, ni, n = FFT_NO, FFT_NI, FFT_N
    k1 = np.arange(no, dtype=np.float64)
    n_o = np.arange(no, dtype=np.float64)
    n_i = np.arange(ni, dtype=np.float64)
    ang = 2 * np.pi * (n_i[:, None, None] * k1[None, :, None] / n + k1[None, :, None] * n_o[None, None, :] / no)
    return np.concatenate([np.cos(ang), -np.sin(ang)], axis=1).astype(np.float32)


def _store_stage1(w_ref, ni, out):
    w_ref[pl.ds(ni, FFT_NO, stride=2 * FFT_NI), :] = out[0:FFT_NO]
    w_ref[pl.ds(FFT_NI + ni, FFT_NO, stride=2 * FFT_NI), :] = out[FFT_NO:2 * FFT_NO]


def _fwd_stage1(za_ref, zb_ref, m1_ref, w_ref):
    def body(ni, carry):
        a = za_ref[pl.ds(ni, FFT_HALF, stride=FFT_NI), :]
        b = zb_ref[pl.ds(ni, FFT_HALF, stride=FFT_NI), :]
        out = jnp.dot(m1_ref[ni], jnp.concatenate([a, b], axis=0).astype(BF16), preferred_element_type=F32)
        _store_stage1(w_ref, ni, out)
        return carry

    lax.fori_loop(0, FFT_NI, body, 0, unroll=FFT_UNROLL)


def _spec_two_stage_kernel(h_ref, m1_ref, m2_ref, o_ref, w_ref):
    h = h_ref.at[0, 0]

    def stage1(ni, carry):
        a = h[pl.ds(ni, FFT_NO, stride=FFT_NI), :]
        _store_stage1(w_ref, ni, jnp.dot(m1_ref[ni], a.astype(BF16), preferred_element_type=F32))
        return carry

    lax.fori_loop(0, FFT_NI, stage1, 0, unroll=FFT_UNROLL)
    blk = 2 * FFT_NI

    cb = w_ref.shape[1]

    def stage2(kp, carry):
        rows = [pl.ds(pl.multiple_of((2 * kp + j) * blk, blk), blk) for j in range(2)]
        x = jnp.dot(m2_ref[...], jnp.concatenate([w_ref[r, :] for r in rows], axis=1).astype(BF16),
                    preferred_element_type=F32)
        for j in range(2):
            o_ref[0, 0, rows[j], :] = x[:, j * cb:(j + 1) * cb]
        return carry

    lax.fori_loop(0, FFT_NO // 2, stage2, 0, unroll=FFT_UNROLL)


def _spec_two_stage(h, m1_real, m2):
    cb = LCONV_CB
    return pl.pallas_call(
        _spec_two_stage_kernel,
        grid=(DEPTH, 2, BRANCH_W // cb),
        in_specs=[pl.BlockSpec((1, 1, FFT_N, cb), lambda l, o, c: (l, o, 0, c)),
                  pl.BlockSpec((FFT_NI, 2 * FFT_NO, FFT_NO), lambda l, o, c: (0, 0, 0)),
                  pl.BlockSpec((2 * FFT_NI, 2 * FFT_NI), lambda l, o, c: (0, 0))],
        out_specs=pl.BlockSpec((1, 1, 2 * FFT_N, cb), lambda l, o, c: (l, o, 0, c)),
        out_shape=jax.ShapeDtypeStruct((DEPTH, 2, 2 * FFT_N, BRANCH_W), F32),
        scratch_shapes=[pltpu.VMEM((2 * FFT_N, cb), F32)],
        compiler_params=_cparams("arbitrary", "arbitrary", "arbitrary"),
        name="hyena_spectrum_two_stage",
    )(h, m1_real, m2)


def _lconv_two_stage_kernel(s_ref, g_ref, cws_ref, cbs_ref, cwg_ref, cbg_ref, h_ref, bias_ref,
                            m1_ref, m2_ref, m2c_ref, m3_ref, o_ref, z_ref, w_ref, *, conv_sig):
    for b in range(2):
        sig = s_ref[b].astype(F32)
        if conv_sig:
            sig = _short_conv(sig, cws_ref, cbs_ref, DEC_SEQ)
        z_ref[b] = sig
    _fwd_stage1(z_ref.at[0], z_ref.at[1], m1_ref, w_ref)
    blk = 2 * FFT_NI

    cb = w_ref.shape[1]

    def mid(kp, carry):
        rows = [pl.ds(pl.multiple_of((2 * kp + j) * blk, blk), blk) for j in range(2)]
        x = jnp.dot(m2_ref[...], jnp.concatenate([w_ref[r, :] for r in rows], axis=1).astype(BF16),
                    preferred_element_type=F32)
        h = jnp.concatenate([h_ref[r, :] for r in rows], axis=1)
        xr, xi = x[0:FFT_NI], x[FFT_NI:blk]
        hr, hi = h[0:FFT_NI], h[FFT_NI:blk]
        y = jnp.concatenate([xr * hr - xi * hi, xr * hi + xi * hr], axis=0)
        c = jnp.dot(m2c_ref[...], y.astype(BF16), preferred_element_type=F32)
        for j in range(2):
            w_ref[rows[j], :] = c[:, j * cb:(j + 1) * cb]
        return carry

    lax.fori_loop(0, FFT_NO // 2, mid, 0, unroll=FFT_UNROLL)

    def last(ni, carry):
        cr = w_ref[pl.ds(ni, FFT_NO, stride=blk), :]
        ci = w_ref[pl.ds(FFT_NI + ni, FFT_NO, stride=blk), :]
        y = jnp.dot(m3_ref[ni], jnp.concatenate([cr, ci], axis=0).astype(BF16), preferred_element_type=F32)
        o_ref[0, pl.ds(ni, FFT_HALF, stride=FFT_NI), :] = y[0:FFT_HALF]
        o_ref[1, pl.ds(ni, FFT_HALF, stride=FFT_NI), :] = y[FFT_HALF:2 * FFT_HALF]
        return carry

    lax.fori_loop(0, FFT_NI, last, 0, unroll=FFT_UNROLL)
    for b in range(2):
        gate = _short_conv(g_ref[b].astype(F32), cwg_ref, cbg_ref, DEC_SEQ)
        sig = z_ref[b]
        o_ref[b] = gate * (o_ref[b] + sig * bias_ref[...])


def _lconv_two_stage(sig, sig_col, gate_src, gate_col, conv_w, conv_b, spec, l, order, bias, mats, conv_sig):
    cb = LCONV_CB
    ncb = BRANCH_W // cb
    m1, m2, m2c, m3 = mats
    const3 = lambda c, p: (0, 0, 0)
    const2 = lambda c, p: (0, 0)
    return pl.pallas_call(
        functools.partial(_lconv_two_stage_kernel, conv_sig=conv_sig),
        grid=(ncb, DEC_BATCH // 2),
        in_specs=[
            pl.BlockSpec((2, DEC_SEQ, cb), lambda c, p: (p, 0, sig_col * ncb + c)),
            pl.BlockSpec((2, DEC_SEQ, cb), lambda c, p: (p, 0, gate_col * ncb + c)),
            pl.BlockSpec((3, cb), lambda c, p: (0, c)),
            pl.BlockSpec((1, cb), lambda c, p: (0, c)),
            pl.BlockSpec((3, cb), lambda c, p: (0, gate_col * ncb + c)),
            pl.BlockSpec((1, cb), lambda c, p: (0, gate_col * ncb + c)),
            pl.BlockSpec((None, None, 2 * FFT_N, cb), lambda c, p: (l, order, 0, c)),
            pl.BlockSpec((1, cb), lambda c, p: (0, c)),
            pl.BlockSpec(m1.shape, const3),
            pl.BlockSpec(m2.shape, const2),
            pl.BlockSpec(m2c.shape, const2),
            pl.BlockSpec(m3.shape, const3),
        ],
        out_specs=pl.BlockSpec((2, DEC_SEQ, cb), lambda c, p: (p, 0, c)),
        out_shape=jax.ShapeDtypeStruct((DEC_BATCH, DEC_SEQ, BRANCH_W), F32),
        scratch_shapes=[pltpu.VMEM((2, DEC_SEQ, cb), F32), pltpu.VMEM((2 * FFT_N, cb), F32)],
        compiler_params=_cparams("arbitrary", "arbitrary"),
        name="hyena_lconv_two_stage",
    )(sig, gate_src, conv_w, conv_b, conv_w, conv_b, spec, bias, m1, m2, m2c, m3)


def kernel(x_prompt, x_sample, cache_na_k, cache_na_v, cache_da_k, cache_da_v, c, c_ctx, w_ada, b_ada, norm_mix,
           norm_ffn, w_in, hy_conv_w, hy_conv_b, hy_filt_w1, hy_filt_b1, hy_filt_w2, hy_filt_b2, hy_filt_w3,
           hy_filt_freq, hy_bias, na_rpb, da_lambda, da_subln, w_lift, w_out, w_ffn_in, w_ffn_out, norm_final):
    TP, TS = BATCH * SEQ, DEC_BATCH * DEC_SEQ
    xp = x_prompt.reshape(TP, D_MODEL)
    xs = x_sample.reshape(TS, D_MODEL)

    cc = jnp.concatenate([c_ctx[None, :], c, jnp.zeros((8 - 1 - DEC_BATCH, D_MODEL), F32)], axis=0)
    mod = _modulation(cc, w_ada, b_ada)
    mod_p = mod[:, 0:1].reshape(DEPTH, 1, 1, 6 * D_MODEL)
    mod_s = mod[:, 1:1 + DEC_BATCH].reshape(DEPTH, DEC_BATCH, 1, 6 * D_MODEL)

    w_mix = w_in[:, :, :MIX_W].astype(BF16)
    w_gate = w_in[:, :, MIX_W:].astype(BF16)
    w_lift_b = w_lift.astype(BF16)
    w_out_b = w_out.astype(BF16)
    w_ffn_in_b = w_ffn_in.astype(BF16)
    w_ffn_out_b = w_ffn_out.astype(BF16)
    g_mix = norm_mix.reshape(DEPTH, 1, D_MODEL)
    g_ffn = norm_ffn.reshape(DEPTH, 1, D_MODEL)
    g_fin = norm_final.reshape(1, D_MODEL)
    subln = da_subln.reshape(DEPTH, 1, DA_V_DIM)
    subln_col = da_subln.reshape(DEPTH, DA_V_DIM, 1)

    w1p = jnp.pad(hy_filt_w1, ((0, 0), (0, HY_FILT_HIDDEN - HY_POS_DIM), (0, 0)))
    b1 = hy_filt_b1.reshape(DEPTH, 1, HY_FILT_HIDDEN)
    b2 = hy_filt_b2.reshape(DEPTH, 1, HY_FILT_HIDDEN)
    fr = hy_filt_freq.reshape(DEPTH, 1, HY_FILT_HIDDEN)
    mf, mi = _dft_direct_mats()
    mats = _dft_two_stage_mats()
    h_p = _hyena_filters(SEQ, w1p, b1, hy_filt_w2, b2, hy_filt_w3, fr)
    h_s = _hyena_filters(DEC_SEQ, w1p, b1, hy_filt_w2, b2, hy_filt_w3, fr)
    spec_p = _spec_direct(h_p, jnp.asarray(_dft_real_mat()))
    mf_b, mi_b = jnp.asarray(mf, dtype=BF16), jnp.asarray(mi, dtype=BF16)
    mats_b = tuple(jnp.asarray(m, dtype=BF16) for m in mats)
    spec_s = _spec_two_stage(h_s, jnp.asarray(_dft_stage1_real_mat(), dtype=BF16), mats_b[1])
    conv_b = hy_conv_b.reshape(DEPTH, 1, 3 * BRANCH_W)

    na_bias = _na_bias_table(na_rpb)
    rope_tables = _rope_tables()
    ck_na = cache_na_k.reshape(DEC_BATCH, DEPTH, PAST_LEN, BRANCH_W)
    cv_na = cache_na_v.reshape(DEC_BATCH, DEPTH, PAST_LEN, BRANCH_W)
    ck_da = cache_da_k.reshape(DEC_BATCH, DEPTH, PAST_LEN, BRANCH_W)
    cv_da = cache_da_v.reshape(DEC_BATCH, DEPTH, PAST_LEN, BRANCH_W)

    caches = tuple(jnp.zeros((BATCH, DEPTH, SEQ, BRANCH_W), F32) for _ in CACHE_BLOCKS)
    for l in range(DEPTH):
        lam_init = 0.8 - 0.6 * math.exp(-0.3 * l)
        final = l == DEPTH - 1

        u, caches = _in_proj(xp, g_mix[l], mod_p[l], w_mix, l, TP, F32, caches=caches)
        z1 = _lconv_direct(u, 0, u, 1, hy_conv_w[l], conv_b[l], spec_p, l, 0, hy_bias[l, 0:1], mf_b, mi_b, True)
        y_hy = _lconv_direct(z1, 0, u, 2, hy_conv_w[l], conv_b[l], spec_p, l, 1, hy_bias[l, 1:2], mf_b, mi_b, False)
        y_na, y_da = _ctx_attention(u, da_lambda[l], subln_col[l], lam_init)
        xp = _merge_out(xp, g_mix[l], mod_p[l], y_hy, y_na, y_da, w_gate, w_lift_b, w_out_b, l, TP)
        xp = _ffn(xp, g_ffn[l], mod_p[l], w_ffn_in_b, w_ffn_out_b, g_fin, l, TP, final)

        u = _in_proj(xs, g_mix[l], mod_s[l], w_mix, l, DEC_SEQ, BF16)
        u3 = u.reshape(DEC_BATCH, DEC_SEQ, MIX_W)
        z1 = _lconv_two_stage(u3, 0, u3, 1, hy_conv_w[l], conv_b[l], spec_s, l, 0, hy_bias[l, 0:1], mats_b, True)
        y_hy = _lconv_two_stage(z1, 0, u3, 2, hy_conv_w[l], conv_b[l], spec_s, l, 1, hy_bias[l, 1:2], mats_b, False)
        y_hy = y_hy.reshape(TS, BRANCH_W)
        qn, kn, vn = _attn_prep(u, 3, ck_na[:, l], cv_na[:, l], NA_HEAD_DIM)
        y_na = _nbr_attention(qn, kn, vn, na_bias, l)
        q, kt, v = _attn_prep(u, 6, ck_da[:, l], cv_da[:, l], DA_HEAD_DIM, rope_tables)
        y_da = _diff_attention(q, kt, v, da_lambda[l], subln_col[l], lam_init)
        xs = _merge_out(xs, g_mix[l], mod_s[l], y_hy, y_na, y_da, w_gate, w_lift_b, w_out_b, l, DEC_SEQ)
        xs = _ffn(xs, g_ffn[l], mod_s[l], w_ffn_in_b, w_ffn_out_b, g_fin, l, DEC_SEQ, final)

    y_prompt = xp.reshape(BATCH, SEQ, D_MODEL)
    y_sample = xs.reshape(DEC_BATCH, DEC_SEQ, D_MODEL)
    heads = lambda a, d: a.reshape(BATCH, DEPTH, SEQ, BRANCH_W // d, d)
    return (y_prompt, y_sample, heads(caches[0], NA_HEAD_DIM), heads(caches[1], NA_HEAD_DIM),
            heads(caches[2], 2 * DA_HEAD_DIM), heads(caches[3], DA_V_DIM))
```

```python
import functools
import math

import numpy as np
import jax
import jax.numpy as jnp
from jax import lax
from jax.experimental import pallas as pl
from jax.experimental.pallas import tpu as pltpu

F32 = jnp.float32
BF16 = jnp.bfloat16
HIGHEST = lax.Precision.HIGHEST

D_MODEL = 1024
BATCH = 32
SEQ = 256
DEPTH = 4
DEC_BATCH = 4
DEC_SEQ = 4096
PAST_LEN = 256
GRID_W = 64
GRID_H = DEC_SEQ // GRID_W
BRANCH_W = 512
HY_POS_BANDS = 16
HY_POS_DIM = 1 + 2 * HY_POS_BANDS
HY_FILT_HIDDEN = 64
HY_DECAY_TARGET = 1e-2
HY_FAST_DECAY = 0.3
HY_SLOW_DECAY = 1.5
NA_HEADS = 8
NA_HEAD_DIM = 64
NA_WIN_ROWS = 8
NA_WIN_COLS = 16
DA_HEADS = 8
DA_HEAD_DIM = 32
DA_V_DIM = 64
D_FF = 2816
MIX_W = 9 * BRANCH_W
ROPE_BASE = 10000.0
EPS = 1e-6
NEG_INF = -1e30

VMEM_LIMIT_BYTES = 56 * 1024 * 1024
LANES = 128
MXU_DIM = 256

FFT_N = 2 * DEC_SEQ
FFT_NO = 64
FFT_NI = 128
FFT_HALF = FFT_NO // 2
FFT_UNROLL = 4
LCONV_CB = LANES


def _cparams(*sem):
    return pltpu.CompilerParams(dimension_semantics=sem, vmem_limit_bytes=VMEM_LIMIT_BYTES)


def _sigmoid(x):
    return 1.0 / (1.0 + jnp.exp(-x))


def _rms(x, g):
    return x * lax.rsqrt(jnp.mean(x * x, axis=-1, keepdims=True) + EPS) * g


def _modnorm(x, g, shift, scale):
    return _rms(x, g) * (1.0 + scale) + shift


def _bdot(a, b):
    return jnp.dot(a.astype(BF16), b.astype(BF16), preferred_element_type=F32)


def _mod_kernel(c_ref, w_ref, b_ref, o_ref):
    c = c_ref[...]
    s = c * _sigmoid(c)
    o_ref[0] = jnp.dot(s, w_ref[0], precision=HIGHEST, preferred_element_type=F32) + b_ref[0]


def _modulation(cc, w_ada, b_ada):
    nt = 6
    return pl.pallas_call(
        _mod_kernel,
        grid=(DEPTH, nt),
        in_specs=[
            pl.BlockSpec((8, D_MODEL), lambda l, j: (0, 0)),
            pl.BlockSpec((1, D_MODEL, D_MODEL), lambda l, j: (l, 0, j)),
            pl.BlockSpec((1, 1, D_MODEL), lambda l, j: (l, 0, j)),
        ],
        out_specs=pl.BlockSpec((1, 8, D_MODEL), lambda l, j: (l, 0, j)),
        out_shape=jax.ShapeDtypeStruct((DEPTH, 8, 6 * D_MODEL), F32),
        compiler_params=_cparams("arbitrary", "arbitrary"),
        name="modulation",
    )(cc, w_ada, b_ada.reshape(DEPTH, 1, 6 * D_MODEL))


IN_TM = 1024
IN_TN = 3 * BRANCH_W
CACHE_BLOCKS = (4, 5, 7, 8)


def _in_kernel(*refs, n_alias, cache_tiles):
    x_ref, g_ref, mod_ref, w_ref = refs[:4]
    o_ref = refs[4 + n_alias]
    cache_refs = refs[5 + n_alias:5 + n_alias + len(cache_tiles)]
    h_ref = refs[-1]
    j = pl.program_id(1)

    @pl.when(j == 0)
    def _():
        m = mod_ref[0]
        h = _modnorm(x_ref[...], g_ref[...], m[:, 0:D_MODEL], m[:, D_MODEL:2 * D_MODEL])
        h_ref[...] = h.astype(BF16)

    res = jnp.dot(h_ref[...], w_ref[...], preferred_element_type=F32)
    o_ref[...] = res.astype(o_ref.dtype)
    for (tile, off), c_ref in zip(cache_tiles, cache_refs):
        @pl.when(j == tile)
        def _(c_ref=c_ref, off=off):
            c_ref[...] = res[:, off:off + BRANCH_W].reshape(c_ref.shape)


def _in_proj(x, g, mod, w, l, rows_per_mod, out_dtype, caches=None):
    T = x.shape[0]
    tm, tn = IN_TM, IN_TN
    per = rows_per_mod // tm
    in_specs = [
        pl.BlockSpec((tm, D_MODEL), lambda i, j: (i, 0)),
        pl.BlockSpec((1, D_MODEL), lambda i, j: (0, 0)),
        pl.BlockSpec((1, 1, 6 * D_MODEL), lambda i, j: (i // per, 0, 0)),
        pl.BlockSpec((None, D_MODEL, tn), lambda i, j: (l, 0, j)),
    ]
    out_specs = [pl.BlockSpec((tm, tn), lambda i, j: (i, j))]
    out_shape = [jax.ShapeDtypeStruct((T, MIX_W), out_dtype)]
    args = [x, g, mod, w]
    cache_tiles, aliases = (), {}
    if caches is not None:
        seqs = tm // SEQ
        cache_tiles = tuple(divmod(c * BRANCH_W, tn) for c in CACHE_BLOCKS)
        out_specs += [pl.BlockSpec((seqs, 1, SEQ, BRANCH_W), lambda i, j: (i, l, 0, 0))] * len(CACHE_BLOCKS)
        out_shape += [jax.ShapeDtypeStruct(c.shape, c.dtype) for c in caches]
        in_specs += [pl.BlockSpec(memory_space=pl.ANY)] * len(caches)
        aliases = {4 + n: 1 + n for n in range(len(caches))}
        args += list(caches)
    n_alias = len(args) - 4
    outs = pl.pallas_call(
        functools.partial(_in_kernel, n_alias=n_alias, cache_tiles=cache_tiles),
        grid=(T // tm, MIX_W // tn),
        in_specs=in_specs,
        out_specs=out_specs,
        out_shape=out_shape,
        input_output_aliases=aliases,
        scratch_shapes=[pltpu.VMEM((tm, D_MODEL), BF16)],
        compiler_params=_cparams("arbitrary", "arbitrary"),
        name="in_proj",
    )(*args)
    return outs[0] if caches is None else (outs[0], tuple(outs[1:]))


def _mid_kernel(x_ref, g_ref, mod_ref, yh_ref, yn_ref, yd_ref, wg_ref, wl_ref, wo_ref, o_ref):
    m = mod_ref[0]
    x = x_ref[...]
    h = _modnorm(x, g_ref[...], m[:, 0:D_MODEL], m[:, D_MODEL:2 * D_MODEL]).astype(BF16)
    merged = None
    for br, y_ref in enumerate((yh_ref, yn_ref, yd_ref)):
        gate = _sigmoid(jnp.dot(h, wg_ref[:, br * D_MODEL:(br + 1) * D_MODEL], preferred_element_type=F32))
        lift = jnp.dot(y_ref[...].astype(BF16), wl_ref[br], preferred_element_type=F32)
        t = gate * lift
        merged = t if merged is None else merged + t
    o_ref[...] = x + m[:, 2 * D_MODEL:3 * D_MODEL] * _bdot(merged, wo_ref[...])


def _merge_out(x, g, mod, y_hy, y_na, y_da, w_gate, w_lift, w_out, l, rows_per_mod):
    T = x.shape[0]
    tm = 512
    per = rows_per_mod // tm
    row = lambda i: (i, 0)
    const2 = lambda i: (0, 0)
    return pl.pallas_call(
        _mid_kernel,
        grid=(T // tm,),
        in_specs=[
            pl.BlockSpec((tm, D_MODEL), row),
            pl.BlockSpec((1, D_MODEL), const2),
            pl.BlockSpec((1, 1, 6 * D_MODEL), lambda i: (i // per, 0, 0)),
            pl.BlockSpec((tm, BRANCH_W), row),
            pl.BlockSpec((tm, BRANCH_W), row),
            pl.BlockSpec((tm, BRANCH_W), row),
            pl.BlockSpec((None, D_MODEL, 3 * D_MODEL), lambda i: (l, 0, 0)),
            pl.BlockSpec((None, 3, BRANCH_W, D_MODEL), lambda i: (l, 0, 0, 0)),
            pl.BlockSpec((None, D_MODEL, D_MODEL), lambda i: (l, 0, 0)),
        ],
        out_specs=pl.BlockSpec((tm, D_MODEL), row),
        out_shape=jax.ShapeDtypeStruct((T, D_MODEL), F32),
        compiler_params=_cparams("arbitrary"),
        name="merge_out",
    )(x, g, mod, y_hy, y_na, y_da, w_gate, w_lift, w_out)


FFN_CHUNK = D_FF // 2


def _ffn_kernel(x_ref, g_ref, mod_ref, w1g_ref, w1u_ref, w2_ref, gf_ref, o_ref, h_ref, acc_ref, *, final):
    k = pl.program_id(1)

    @pl.when(k == 0)
    def _():
        m = mod_ref[0]
        h = _modnorm(x_ref[...], g_ref[...], m[:, 3 * D_MODEL:4 * D_MODEL], m[:, 4 * D_MODEL:5 * D_MODEL])
        h_ref[...] = h.astype(BF16)

    h = h_ref[...]
    a = jnp.dot(h, w1g_ref[...], preferred_element_type=F32)
    b = jnp.dot(h, w1u_ref[...], preferred_element_type=F32)
    part = _bdot(a * _sigmoid(a) * b, w2_ref[...])

    @pl.when(k == 0)
    def _():
        acc_ref[...] = part

    @pl.when(k == 1)
    def _():
        m = mod_ref[0]
        xn = x_ref[...] + m[:, 5 * D_MODEL:6 * D_MODEL] * (acc_ref[...] + part)
        if final:
            xn = _rms(xn, gf_ref[...])
        o_ref[...] = xn


def _ffn(x, g, mod, w_ffn_in, w_ffn_out, g_final, l, rows_per_mod, final):
    T = x.shape[0]
    tm = 512
    per = rows_per_mod // tm
    return pl.pallas_call(
        functools.partial(_ffn_kernel, final=final),
        grid=(T // tm, 2),
        in_specs=[
            pl.BlockSpec((tm, D_MODEL), lambda i, k: (i, 0)),
            pl.BlockSpec((1, D_MODEL), lambda i, k: (0, 0)),
            pl.BlockSpec((1, 1, 6 * D_MODEL), lambda i, k: (i // per, 0, 0)),
            pl.BlockSpec((None, D_MODEL, FFN_CHUNK), lambda i, k: (l, 0, k)),
            pl.BlockSpec((None, D_MODEL, FFN_CHUNK), lambda i, k: (l, 0, 2 + k)),
            pl.BlockSpec((None, FFN_CHUNK, D_MODEL), lambda i, k: (l, k, 0)),
            pl.BlockSpec((1, D_MODEL), lambda i, k: (0, 0)),
        ],
        out_specs=pl.BlockSpec((tm, D_MODEL), lambda i, k: (i, 0)),
        out_shape=jax.ShapeDtypeStruct((T, D_MODEL), F32),
        scratch_shapes=[pltpu.VMEM((tm, D_MODEL), BF16), pltpu.VMEM((tm, D_MODEL), F32)],
        compiler_params=_cparams("arbitrary", "arbitrary"),
        name="ffn",
    )(x, g, mod, w_ffn_in, w_ffn_in, w_ffn_out, g_final)


def _da_lambda(lam_ref, lam_init):
    lp = lam_ref[...]
    a = jnp.sum(lp[0:1] * lp[1:2], axis=1, keepdims=True)
    b = jnp.sum(lp[2:3] * lp[3:4], axis=1, keepdims=True)
    return jnp.exp(a) - jnp.exp(b) + lam_init


ATT_ONES_ROWS = 16
ATT_TQ = 256
ATT_KEYS = DEC_SEQ + PAST_LEN
LOG2E = math.log2(math.e)


def _masked_q_blocks(qt, d):
    row = lax.broadcasted_iota(jnp.int32, qt.shape, 0)
    zero = jnp.zeros_like(qt)
    return jnp.concatenate([jnp.where((row >= j * d) & (row < (j + 1) * d), qt, zero) for j in range(LANES // d)],
                           axis=1)


def _colmax(st):
    keys, n = st.shape
    return jnp.max(jnp.max(st.reshape(keys // MXU_DIM, MXU_DIM, n), axis=0), axis=0, keepdims=True)


def _ctx_attn_kernel(nq_ref, nk_ref, nv_ref, dq_ref, dk_ref, dv_ref, lam_ref, sub_ref, yn_ref, yd_ref, acc_ref,
                     *, lam_init):
    lam = _da_lambda(lam_ref, lam_init)
    ones = jnp.ones((ATT_ONES_ROWS, SEQ), BF16)

    def attend(q_ref, k_ref, v_ref, d, maps_per_head, finish):
        qt = (q_ref[...].astype(F32) * (d ** -0.5 * LOG2E)).T.astype(BF16)
        vt = v_ref[...].astype(F32).T.astype(BF16)
        kb = k_ref[...].astype(BF16)
        dv = NA_HEAD_DIM
        heads_per_group = LANES // (d * maps_per_head)
        w = maps_per_head * SEQ
        for g in range(BRANCH_W // LANES):
            lanes = slice(g * LANES, (g + 1) * LANES)
            st = jnp.dot(kb[:, lanes], _masked_q_blocks(qt[lanes], d), preferred_element_type=F32)
            pt = jnp.exp2(st - _colmax(st)).astype(BF16)
            for j in range(heads_per_group):
                h = g * heads_per_group + j
                ve = jnp.concatenate([vt[h * dv:(h + 1) * dv], ones], axis=0)
                oe = jnp.dot(ve, pt[:, j * w:(j + 1) * w], preferred_element_type=F32)
                os = [oe[0:dv, i * SEQ:(i + 1) * SEQ] / oe[dv:dv + 1, i * SEQ:(i + 1) * SEQ]
                      for i in range(maps_per_head)]
                acc_ref[h * dv:(h + 1) * dv, :] = finish(os)

    attend(nq_ref, nk_ref, nv_ref, NA_HEAD_DIM, 1, lambda os: os[0])
    yn_ref[...] = acc_ref[...].T.astype(yn_ref.dtype)

    def da_finish(os):
        ot = os[0] - lam * os[1]
        ot = ot * lax.rsqrt(jnp.mean(ot * ot, axis=0, keepdims=True) + EPS) * sub_ref[...]
        return ot * (1.0 - lam_init)

    attend(dq_ref, dk_ref, dv_ref, DA_HEAD_DIM, 2, da_finish)
    yd_ref[...] = acc_ref[...].T.astype(yd_ref.dtype)


def _ctx_attention(u, da_lambda, subln_col, lam_init):
    col = lambda j: pl.BlockSpec((SEQ, BRANCH_W), lambda b, j=j: (b, j))
    out = pl.BlockSpec((SEQ, BRANCH_W), lambda b: (b, 0))
    shape = jax.ShapeDtypeStruct((BATCH * SEQ, BRANCH_W), BF16)
    return pl.pallas_call(
        functools.partial(_ctx_attn_kernel, lam_init=lam_init),
        grid=(BATCH,),
        in_specs=[col(3), col(4), col(5), col(6), col(7), col(8),
                  pl.BlockSpec((4, DA_HEAD_DIM), lambda b: (0, 0)),
                  pl.BlockSpec((DA_V_DIM, 1), lambda b: (0, 0))],
        out_specs=[out, out],
        out_shape=[shape, shape],
        scratch_shapes=[pltpu.VMEM((BRANCH_W, SEQ), F32)],
        compiler_params=_cparams("arbitrary"),
        name="ctx_attention",
    )(u, u, u, u, u, u, da_lambda, subln_col)


def _rope(x, cos, sin_signed):
    n = x.shape[-1]
    lane = lax.broadcasted_iota(jnp.int32, x.shape, 1)
    partner = jnp.where(lane % 2 == 0, pltpu.roll(x, n - 1, axis=1), pltpu.roll(x, 1, axis=1))
    return x * cos + partner * sin_signed


def _attn_prep_kernel(q_ref, k_ref, v_ref, kc_ref, vc_ref, *refs, rope, scale):
    cos_ref, sin_ref = refs[:2] if rope else (None, None)
    qt_ref, ko_ref, vt_ref = refs[-3:]
    t = pl.program_id(1)
    dv = NA_HEAD_DIM

    def put_v(v):
        vt = v.astype(F32).T.astype(BF16)
        ones = jnp.ones((ATT_ONES_ROWS, ATT_TQ), BF16)
        for h in range(BRANCH_W // dv):
            vt_ref[0, h, 0:dv, :] = vt[h * dv:(h + 1) * dv]
            vt_ref[0, h, dv:dv + ATT_ONES_ROWS, :] = ones

    @pl.when(t < DEC_SEQ // ATT_TQ)
    def _():
        q = q_ref[...].astype(F32)
        k = k_ref[...].astype(F32)
        if rope:
            q = _rope(q, cos_ref[...], sin_ref[...])
            k = _rope(k, cos_ref[...], sin_ref[...])
        qt_ref[0] = (q * scale).T.astype(BF16)
        ko_ref[0] = k.astype(BF16)
        put_v(v_ref[...])

    @pl.when(t == DEC_SEQ // ATT_TQ)
    def _():
        ko_ref[0] = kc_ref[0].astype(BF16)
        put_v(vc_ref[0])


def _attn_prep(u, first_col, k_ctx, v_ctx, head_dim, rope_tables=None):
    rope = rope_tables is not None
    nt = DEC_SEQ // ATT_TQ
    last = nt - 1
    rowblk = lambda j: pl.BlockSpec((ATT_TQ, BRANCH_W), lambda b, t, j=j: (b * nt + jnp.minimum(t, last), j))
    tab = pl.BlockSpec((ATT_TQ, BRANCH_W), lambda b, t: (jnp.minimum(t, last), 0))
    ctx = pl.BlockSpec((1, PAST_LEN, BRANCH_W), lambda b, t: (b, 0, 0))
    heads = BRANCH_W // NA_HEAD_DIM
    vrows = NA_HEAD_DIM + ATT_ONES_ROWS
    return pl.pallas_call(
        functools.partial(_attn_prep_kernel, rope=rope, scale=head_dim ** -0.5 * LOG2E),
        grid=(DEC_BATCH, nt + 1),
        in_specs=[rowblk(first_col), rowblk(first_col + 1), rowblk(first_col + 2), ctx, ctx] + [tab, tab] * rope,
        out_specs=[
            pl.BlockSpec((1, BRANCH_W, ATT_TQ), lambda b, t: (b, 0, jnp.minimum(t, last))),
            pl.BlockSpec((1, ATT_TQ, BRANCH_W), lambda b, t: (b, t, 0)),
            pl.BlockSpec((1, heads, vrows, ATT_TQ), lambda b, t: (b, 0, 0, t)),
        ],
        out_shape=[
            jax.ShapeDtypeStruct((DEC_BATCH, BRANCH_W, DEC_SEQ), BF16),
            jax.ShapeDtypeStruct((DEC_BATCH, ATT_KEYS, BRANCH_W), BF16),
            jax.ShapeDtypeStruct((DEC_BATCH, heads, vrows, ATT_KEYS), BF16),
        ],
        compiler_params=_cparams("arbitrary", "arbitrary"),
        name="attn_prep",
    )(u, u, u, k_ctx, v_ctx, *(rope_tables or ()))


NA_ROWS = ATT_TQ // GRID_W
NA_UNION = 3 * NA_ROWS
NA_STEPS = GRID_H // NA_ROWS
NA_SLABS = NA_UNION // NA_ROWS
NA_VARIANT_OFFSET = (0, -NA_ROWS, -2 * NA_ROWS)


def _na_variant(s):
    return jnp.minimum(s, 1) + s // (NA_STEPS - 1)


def _na_window_block(s):
    return jnp.clip(s - 1, 0, NA_STEPS - NA_SLABS)


def _na_bias_kernel(rpb_ref, o_ref):
    kc = lax.broadcasted_iota(jnp.int32, (GRID_W, GRID_W), 0)
    qc = lax.broadcasted_iota(jnp.int32, (GRID_W, GRID_W), 1)
    dc = jnp.clip(kc - qc, -(NA_WIN_COLS - 1), NA_WIN_COLS - 1) + (NA_WIN_COLS - 1)
    c0 = jnp.clip(qc - NA_WIN_COLS // 2, 0, GRID_W - NA_WIN_COLS)
    col_ok = (kc >= c0) & (kc < c0 + NA_WIN_COLS)
    r = rpb_ref[0, 0] * LOG2E
    masked = jnp.full((GRID_W, GRID_W), NEG_INF, F32)
    tiles = []
    for dr in range(2 * NA_WIN_ROWS - 1):
        acc = jnp.zeros((GRID_W, GRID_W), F32)
        for d in range(2 * NA_WIN_COLS - 1):
            acc = jnp.where(dc == d, r[dr:dr + 1, d:d + 1], acc)
        tiles.append(jnp.where(col_ok, acc, masked))
    for v, off in enumerate(NA_VARIANT_OFFSET):
        for kr in range(NA_UNION):
            for rr in range(NA_ROWS):
                w0 = (0, rr, NA_UNION - NA_WIN_ROWS)[v]
                dr = kr + off - rr
                inside = w0 <= kr < w0 + NA_WIN_ROWS
                o_ref[0, v, 0, kr * GRID_W:(kr + 1) * GRID_W, rr * GRID_W:(rr + 1) * GRID_W] = (
                    tiles[dr + NA_WIN_ROWS - 1] if inside else masked)


def _na_bias_table(na_rpb):
    n_dr, n_dc = 2 * NA_WIN_ROWS - 1, 2 * NA_WIN_COLS - 1
    nv = len(NA_VARIANT_OFFSET)
    return pl.pallas_call(
        _na_bias_kernel,
        grid=(DEPTH, NA_HEADS),
        in_specs=[pl.BlockSpec((1, 1, n_dr, n_dc), lambda l, h: (l, h, 0, 0))],
        out_specs=pl.BlockSpec((1, nv, 1, NA_UNION * GRID_W, ATT_TQ), lambda l, h: (l, 0, h, 0, 0)),
        out_shape=jax.ShapeDtypeStruct((DEPTH, nv, NA_HEADS, NA_UNION * GRID_W, ATT_TQ), F32),
        compiler_params=_cparams("arbitrary", "arbitrary"),
        name="na_bias_table",
    )(na_rpb)


def _na_kernel(qt_ref, *refs):
    n = NA_SLABS + 1
    k_refs, vt_refs = refs[:n], refs[n:2 * n]
    bias_ref, o_ref, acc_ref = refs[2 * n:]
    dv = NA_HEAD_DIM
    heads_per_group = LANES // dv
    for g in range(BRANCH_W // LANES):
        lanes = slice(g * LANES, (g + 1) * LANES)
        qbd = _masked_q_blocks(qt_ref[0, lanes, :], dv)
        sts = []
        for j, k_ref in enumerate(k_refs):
            st = jnp.dot(k_ref[0, :, lanes], qbd, preferred_element_type=F32)
            if j < NA_SLABS:
                rows = slice(j * ATT_TQ, (j + 1) * ATT_TQ)
                st = st + jnp.concatenate(
                    [bias_ref[0, g * heads_per_group + hh, rows, :] for hh in range(heads_per_group)], axis=1)
            sts.append(st)
        mx = functools.reduce(jnp.maximum, [_colmax(st) for st in sts])
        pts = [jnp.exp2(st - mx).astype(BF16) for st in sts]
        for hh in range(heads_per_group):
            h = g * heads_per_group + hh
            oe = sum(jnp.dot(vt_ref[0, h], pt[:, hh * ATT_TQ:(hh + 1) * ATT_TQ], preferred_element_type=F32)
                     for vt_ref, pt in zip(vt_refs, pts))
            acc_ref[h * dv:(h + 1) * dv, :] = oe[0:dv] / oe[dv:dv + 1]
    o_ref[...] = acc_ref[...].T.astype(o_ref.dtype)


def _nbr_attention(qt, k, vt, bias, l):
    vrows = NA_HEAD_DIM + ATT_ONES_ROWS
    ctx_blk = DEC_SEQ // ATT_TQ
    k_specs = [pl.BlockSpec((1, ATT_TQ, BRANCH_W), lambda b, s, j=j: (b, _na_window_block(s) + j, 0))
               for j in range(NA_SLABS)]
    k_specs.append(pl.BlockSpec((1, ATT_TQ, BRANCH_W), lambda b, s: (b, ctx_blk, 0)))
    vt_specs = [pl.BlockSpec((1, NA_HEADS, vrows, ATT_TQ), lambda b, s, j=j: (b, 0, 0, _na_window_block(s) + j))
                for j in range(NA_SLABS)]
    vt_specs.append(pl.BlockSpec((1, NA_HEADS, vrows, ATT_TQ), lambda b, s: (b, 0, 0, ctx_blk)))
    n = NA_SLABS + 1
    return pl.pallas_call(
        _na_kernel,
        grid=(DEC_BATCH, NA_STEPS),
        in_specs=[pl.BlockSpec((1, BRANCH_W, ATT_TQ), lambda b, s: (b, 0, s))] + k_specs + vt_specs + [
            pl.BlockSpec((None, 1, NA_HEADS, NA_UNION * GRID_W, ATT_TQ), lambda b, s: (l, _na_variant(s), 0, 0, 0))],
        out_specs=pl.BlockSpec((ATT_TQ, BRANCH_W), lambda b, s: (b * NA_STEPS + s, 0)),
        out_shape=jax.ShapeDtypeStruct((DEC_BATCH * DEC_SEQ, BRANCH_W), BF16),
        scratch_shapes=[pltpu.VMEM((BRANCH_W, ATT_TQ), F32)],
        compiler_params=_cparams("arbitrary", "arbitrary"),
        name="nbr_attention",
    )(qt, *([k] * n), *([vt] * n), bias)


DA_TQ = ATT_TQ
DA_KEYS = ATT_KEYS
DA_ONES_ROWS = ATT_ONES_ROWS
DA_MAPS_PER_TILE = LANES // DA_HEAD_DIM


def _da_kernel(qt_ref, k_ref, vt_ref, lam_ref, sub_ref, o_ref, acc_ref, *, lam_init):
    lam = _da_lambda(lam_ref, lam_init)
    heads = DA_MAPS_PER_TILE // 2
    for g in range(BRANCH_W // LANES):
        lanes = slice(g * LANES, (g + 1) * LANES)
        st = jnp.dot(k_ref[0, :, lanes], _masked_q_blocks(qt_ref[0, lanes, :], DA_HEAD_DIM),
                     preferred_element_type=F32)
        pt = jnp.exp2(st - _colmax(st)).astype(BF16)
        for hh in range(heads):
            h = g * heads + hh
            oe = jnp.dot(vt_ref[0, h], pt[:, 2 * hh * DA_TQ:(2 * hh + 2) * DA_TQ], preferred_element_type=F32)
            os = [oe[0:DA_V_DIM, i * DA_TQ:(i + 1) * DA_TQ] / oe[DA_V_DIM:DA_V_DIM + 1, i * DA_TQ:(i + 1) * DA_TQ]
                  for i in range(2)]
            ot = os[0] - lam * os[1]
            ot = ot * lax.rsqrt(jnp.mean(ot * ot, axis=0, keepdims=True) + EPS) * sub_ref[...]
            acc_ref[h * DA_V_DIM:(h + 1) * DA_V_DIM, :] = ot * (1.0 - lam_init)
    o_ref[...] = acc_ref[...].T.astype(o_ref.dtype)


def _diff_attention(qt, k, vt, da_lambda, subln_col, lam_init):
    nt = DEC_SEQ // DA_TQ
    vrows = DA_V_DIM + DA_ONES_ROWS
    return pl.pallas_call(
        functools.partial(_da_kernel, lam_init=lam_init),
        grid=(DEC_BATCH, nt),
        in_specs=[
            pl.BlockSpec((1, BRANCH_W, DA_TQ), lambda b, t: (b, 0, t)),
            pl.BlockSpec((1, DA_KEYS, BRANCH_W), lambda b, t: (b, 0, 0)),
            pl.BlockSpec((1, DA_HEADS, vrows, DA_KEYS), lambda b, t: (b, 0, 0, 0)),
            pl.BlockSpec((4, DA_HEAD_DIM), lambda b, t: (0, 0)),
            pl.BlockSpec((DA_V_DIM, 1), lambda b, t: (0, 0)),
        ],
        out_specs=pl.BlockSpec((DA_TQ, BRANCH_W), lambda b, t: (b * nt + t, 0)),
        out_shape=jax.ShapeDtypeStruct((DEC_BATCH * DEC_SEQ, BRANCH_W), BF16),
        scratch_shapes=[pltpu.VMEM((BRANCH_W, DA_TQ), F32)],
        compiler_params=_cparams("arbitrary", "arbitrary"),
        name="diff_attention",
    )(qt, k, vt, da_lambda, subln_col)


def _rope_tables():
    pos = np.arange(DEC_SEQ)
    row = (pos // GRID_W).astype(np.float32)
    col = (pos % GRID_W).astype(np.float32)
    n_freq = DA_HEAD_DIM // 4
    inv = (np.float32(ROPE_BASE) ** (-np.arange(n_freq, dtype=np.float32) / n_freq)).astype(np.float32)
    ang = np.concatenate([row[:, None] * inv[None, :], col[:, None] * inv[None, :]], axis=-1)
    ang = ang.astype(np.float64)
    cos = np.repeat(np.cos(ang), 2, axis=-1)
    sin = np.repeat(np.sin(ang), 2, axis=-1)
    sign = np.where(np.arange(DA_HEAD_DIM) % 2 == 0, -1.0, 1.0)
    reps = BRANCH_W // DA_HEAD_DIM
    cos = np.tile(cos, (1, reps)).astype(np.float32)
    sin = np.tile(sin * sign[None, :], (1, reps)).astype(np.float32)
    return jnp.asarray(cos), jnp.asarray(sin)


def _filt_hidden_kernel(feat_ref, w1_ref, b1_ref, w2_ref, b2_ref, fr_ref, o_ref):
    fr = fr_ref[0]
    h = jnp.sin(fr * (jnp.dot(feat_ref[...], w1_ref[0], precision=HIGHEST, preferred_element_type=F32) + b1_ref[0]))
    o_ref[0] = jnp.sin(fr * (jnp.dot(h, w2_ref[0], precision=HIGHEST, preferred_element_type=F32) + b2_ref[0]))


def _filt_kernel(h_ref, w3f_ref, w3b_ref, dec_ref, o_ref):
    L = dec_ref.shape[0] // 2
    hf = jnp.dot(h_ref[0, 0:L], w3f_ref[0], precision=HIGHEST, preferred_element_type=F32) * dec_ref[0:L]
    hb = jnp.dot(h_ref[0, L:2 * L], w3b_ref[0], precision=HIGHEST, preferred_element_type=F32) * dec_ref[L:2 * L]
    row = lax.broadcasted_iota(jnp.int32, hb.shape, 0)
    hb = jnp.where(row == 0, 0.0, hb)
    nrm = jnp.sum(jnp.abs(hf), axis=0, keepdims=True) + jnp.sum(jnp.abs(hb), axis=0, keepdims=True)
    o_ref[0, 0, 0:L] = hf / nrm
    o_ref[0, 0, L:2 * L] = hb / nrm


def _circular_order(a):
    return np.concatenate([a, a[:1], a[1:][::-1]], axis=0)


def _hyena_pos_tables(L):
    f32 = np.float32
    pos = np.arange(L, dtype=f32)
    t = (pos / f32(L)).astype(f32)
    bands = np.linspace(1e-4, HY_POS_BANDS - 1, HY_POS_BANDS, dtype=f32)
    ang = (f32(2 * math.pi / L) * pos[:, None] * bands[None, :]).astype(np.float64)
    feats = np.zeros((L, HY_FILT_HIDDEN), f32)
    feats[:, 0] = t
    feats[:, 1:1 + HY_POS_BANDS] = np.cos(ang)
    feats[:, 1 + HY_POS_BANDS:HY_POS_DIM] = -np.sin(ang)
    deltas = np.linspace(math.log(HY_DECAY_TARGET) / HY_SLOW_DECAY,
                         math.log(HY_DECAY_TARGET) / HY_FAST_DECAY, BRANCH_W, dtype=f32)
    decay = np.exp((-t[:, None] * np.abs(deltas)[None, :]).astype(np.float64)).astype(f32)
    return jnp.asarray(_circular_order(feats)), jnp.asarray(_circular_order(decay))


def _hyena_filters(half, w1p, b1, w2, b2, w3, freq):
    feats, decay = _hyena_pos_tables(half)
    L = 2 * half
    cb = LANES
    ncb = BRANCH_W // cb
    small = lambda shape: pl.BlockSpec((1,) + shape, lambda l: (l, 0, 0))
    hidden = pl.pallas_call(
        _filt_hidden_kernel,
        grid=(DEPTH,),
        in_specs=[
            pl.BlockSpec((L, HY_FILT_HIDDEN), lambda l: (0, 0)),
            small((HY_FILT_HIDDEN, HY_FILT_HIDDEN)), small((1, HY_FILT_HIDDEN)),
            small((HY_FILT_HIDDEN, HY_FILT_HIDDEN)), small((1, HY_FILT_HIDDEN)),
            small((1, HY_FILT_HIDDEN)),
        ],
        out_specs=pl.BlockSpec((1, L, HY_FILT_HIDDEN), lambda l: (l, 0, 0)),
        out_shape=jax.ShapeDtypeStruct((DEPTH, L, HY_FILT_HIDDEN), F32),
        compiler_params=_cparams("arbitrary"),
        name=f"hyena_filter_hidden_{L}",
    )(feats, w1p, b1, w2, b2, freq)
    return pl.pallas_call(
        _filt_kernel,
        grid=(DEPTH, 2, ncb),
        in_specs=[
            pl.BlockSpec((1, L, HY_FILT_HIDDEN), lambda l, o, c: (l, 0, 0)),
            pl.BlockSpec((1, HY_FILT_HIDDEN, cb), lambda l, o, c: (l, 0, o * 2 * ncb + c)),
            pl.BlockSpec((1, HY_FILT_HIDDEN, cb), lambda l, o, c: (l, 0, o * 2 * ncb + ncb + c)),
            pl.BlockSpec((L, cb), lambda l, o, c: (0, c)),
        ],
        out_specs=pl.BlockSpec((1, 1, L, cb), lambda l, o, c: (l, o, 0, c)),
        out_shape=jax.ShapeDtypeStruct((DEPTH, 2, L, BRANCH_W), F32),
        compiler_params=_cparams("arbitrary", "arbitrary", "arbitrary"),
        name=f"hyena_filters_{L}",
    )(hidden, w3, w3, decay)


def _short_conv(u, w_ref, b_ref, seq_len):
    n = u.shape[0]
    t = lax.broadcasted_iota(jnp.int32, u.shape, 0) % seq_len
    prev = jnp.where(t == 0, 0.0, pltpu.roll(u, 1, axis=0))
    nxt = jnp.where(t == seq_len - 1, 0.0, pltpu.roll(u, n - 1, axis=0))
    return prev * w_ref[0:1, :] + u * w_ref[1:2, :] + nxt * w_ref[2:3, :] + b_ref[...]


def _dft_direct_mats():
    n, half = 2 * SEQ, SEQ
    k = np.arange(n)[:, None].astype(np.float64)
    t = np.arange(half)[None, :].astype(np.float64)
    ang = 2 * np.pi * k * t / n
    fr, fi = np.cos(ang), -np.sin(ang)
    mf = np.block([[fr, -fi], [fi, fr]])
    gr, gi = np.cos(ang).T / n, np.sin(ang).T / n
    mi = np.block([[gr, -gi], [gi, gr]])
    return mf.astype(np.float32), mi.astype(np.float32)


def _dft_real_mat():
    n = 2 * SEQ
    ang = 2 * np.pi * np.arange(n)[:, None].astype(np.float64) * np.arange(n)[None, :] / n
    return np.concatenate([np.cos(ang), -np.sin(ang)], axis=0).astype(np.float32)


def _spec_direct_kernel(h_ref, m_ref, o_ref):
    o_ref[0, 0] = jnp.dot(m_ref[...], h_ref[0, 0], precision=HIGHEST, preferred_element_type=F32)


def _spec_direct(h, m_real):
    n = 2 * SEQ
    return pl.pallas_call(
        _spec_direct_kernel,
        grid=(DEPTH, 2),
        in_specs=[pl.BlockSpec((1, 1, n, BRANCH_W), lambda l, o: (l, o, 0, 0)),
                  pl.BlockSpec((2 * n, n), lambda l, o: (0, 0))],
        out_specs=pl.BlockSpec((1, 1, 2 * n, BRANCH_W), lambda l, o: (l, o, 0, 0)),
        out_shape=jax.ShapeDtypeStruct((DEPTH, 2, 2 * n, BRANCH_W), F32),
        compiler_params=_cparams("arbitrary", "arbitrary"),
        name="hyena_spectrum_direct",
    )(h, m_real)


def _lconv_direct_kernel(s_ref, g_ref, cws_ref, cbs_ref, cwg_ref, cbg_ref, h_ref, bias_ref, mf_ref, mi_ref, o_ref,
                         *, conv_sig):
    n = 2 * SEQ
    sig = s_ref[...].astype(F32)
    if conv_sig:
        sig = _short_conv(sig, cws_ref, cbs_ref, SEQ)
    gate = _short_conv(g_ref[...].astype(F32), cwg_ref, cbg_ref, SEQ)
    z = jnp.dot(mf_ref[...], sig.astype(BF16), preferred_element_type=F32)
    zr, zi = z[0:n], z[n:2 * n]
    hr, hi = h_ref[0:n], h_ref[n:2 * n]
    y = jnp.concatenate([zr * hr - zi * hi, zr * hi + zi * hr], axis=0)
    y = jnp.dot(mi_ref[...], y.astype(BF16), preferred_element_type=F32)
    o_ref[...] = gate * (y + sig * bias_ref[...])


def _lconv_direct(sig, sig_col, gate_src, gate_col, conv_w, conv_b, spec, l, order, bias, mf, mi, conv_sig):
    n = 2 * SEQ
    rows = 2 * SEQ
    T = sig.shape[0]
    return pl.pallas_call(
        functools.partial(_lconv_direct_kernel, conv_sig=conv_sig),
        grid=(T // rows,),
        in_specs=[
            pl.BlockSpec((rows, BRANCH_W), lambda p: (p, sig_col)),
            pl.BlockSpec((rows, BRANCH_W), lambda p: (p, gate_col)),
            pl.BlockSpec((3, BRANCH_W), lambda p: (0, 0)),
            pl.BlockSpec((1, BRANCH_W), lambda p: (0, 0)),
            pl.BlockSpec((3, BRANCH_W), lambda p: (0, gate_col)),
            pl.BlockSpec((1, BRANCH_W), lambda p: (0, gate_col)),
            pl.BlockSpec((None, None, 2 * n, BRANCH_W), lambda p: (l, order, 0, 0)),
            pl.BlockSpec((1, BRANCH_W), lambda p: (0, 0)),
            pl.BlockSpec((2 * n, rows), lambda p: (0, 0)),
            pl.BlockSpec((rows, 2 * n), lambda p: (0, 0)),
        ],
        out_specs=pl.BlockSpec((rows, BRANCH_W), lambda p: (p, 0)),
        out_shape=jax.ShapeDtypeStruct((T, BRANCH_W), F32),
        compiler_params=_cparams("arbitrary"),
        name="hyena_lconv_direct",
    )(sig, gate_src, conv_w, conv_b, conv_w, conv_b, spec, bias, mf, mi)


def _dft_two_stage_mats():
    no, ni, half, n = FFT_NO, FFT_NI, FFT_HALF, FFT_N
    f64 = np.float64
    k1 = np.arange(no, dtype=f64)
    n_o = np.arange(half, dtype=f64)
    n_i = np.arange(ni, dtype=f64)
    ang = 2 * np.pi * (n_i[:, None, None] * k1[None, :, None] / n + k1[None, :, None] * n_o[None, None, :] / no)
    tr, ti = np.cos(ang), -np.sin(ang)
    m1 = np.concatenate([np.concatenate([tr, -ti], axis=2), np.concatenate([ti, tr], axis=2)], axis=1)
    k2 = np.arange(ni, dtype=f64)
    ang2 = 2 * np.pi * k2[:, None] * n_i[None, :] / ni
    f2r, f2i = np.cos(ang2), -np.sin(ang2)
    m2 = np.block([[f2r, -f2i], [f2i, f2r]])
    m2c = np.block([[f2r, f2i], [-f2i, f2r]])
    sr, si = np.transpose(tr, (0, 2, 1)) / n, -np.transpose(ti, (0, 2, 1)) / n
    m3 = np.concatenate([np.concatenate([sr, -si], axis=2), np.concatenate([si, sr], axis=2)], axis=1)
    return (m1.astype(np.float32), m2.astype(np.float32), m2c.astype(np.float32), m3.astype(np.float32))


def _dft_stage1_real_mat():
    no, ni, n = FFT_NO, FFT_NI, FFT_N
    k1 = np.arange(no, dtype=np.float64)
    n_o = np.arange(no, dtype=np.float64)
    n_i = np.arange(ni, dtype=np.float64)
    ang = 2 * np.pi * (n_i[:, None, None] * k1[None, :, None] / n + k1[None, :, None] * n_o[None, None, :] / no)
    return np.concatenate([np.cos(ang), -np.sin(ang)], axis=1).astype(np.float32)


def _store_stage1(w_ref, ni, out):
    w_ref[pl.ds(ni, FFT_NO, stride=2 * FFT_NI), :] = out[0:FFT_NO]
    w_ref[pl.ds(FFT_NI + ni, FFT_NO, stride=2 * FFT_NI), :] = out[FFT_NO:2 * FFT_NO]


def _fwd_stage1(za_ref, zb_ref, m1_ref, w_ref):
    def body(ni, carry):
        a = za_ref[pl.ds(ni, FFT_HALF, stride=FFT_NI), :]
        b = zb_ref[pl.ds(ni, FFT_HALF, stride=FFT_NI), :]
        out = jnp.dot(m1_ref[ni], jnp.concatenate([a, b], axis=0).astype(BF16), preferred_element_type=F32)
        _store_stage1(w_ref, ni, out)
        return carry

    lax.fori_loop(0, FFT_NI, body, 0, unroll=FFT_UNROLL)


def _spec_two_stage_kernel(h_ref, m1_ref, m2_ref, o_ref, w_ref):
    h = h_ref.at[0, 0]

    def stage1(ni, carry):
        a = h[pl.ds(ni, FFT_NO, stride=FFT_NI), :]
        _store_stage1(w_ref, ni, jnp.dot(m1_ref[ni], a.astype(BF16), preferred_element_type=F32))
        return carry

    lax.fori_loop(0, FFT_NI, stage1, 0, unroll=FFT_UNROLL)
    blk = 2 * FFT_NI

    cb = w_ref.shape[1]

    def stage2(kp, carry):
        rows = [pl.ds(pl.multiple_of((2 * kp + j) * blk, blk), blk) for j in range(2)]
        x = jnp.dot(m2_ref[...], jnp.concatenate([w_ref[r, :] for r in rows], axis=1).astype(BF16),
                    preferred_element_type=F32)
        for j in range(2):
            o_ref[0, 0, rows[j], :] = x[:, j * cb:(j + 1) * cb]
        return carry

    lax.fori_loop(0, FFT_NO // 2, stage2, 0, unroll=FFT_UNROLL)


def _spec_two_stage(h, m1_real, m2):
    cb = LCONV_CB
    return pl.pallas_call(
        _spec_two_stage_kernel,
        grid=(DEPTH, 2, BRANCH_W // cb),
        in_specs=[pl.BlockSpec((1, 1, FFT_N, cb), lambda l, o, c: (l, o, 0, c)),
                  pl.BlockSpec((FFT_NI, 2 * FFT_NO, FFT_NO), lambda l, o, c: (0, 0, 0)),
                  pl.BlockSpec((2 * FFT_NI, 2 * FFT_NI), lambda l, o, c: (0, 0))],
        out_specs=pl.BlockSpec((1, 1, 2 * FFT_N, cb), lambda l, o, c: (l, o, 0, c)),
        out_shape=jax.ShapeDtypeStruct((DEPTH, 2, 2 * FFT_N, BRANCH_W), F32),
        scratch_shapes=[pltpu.VMEM((2 * FFT_N, cb), F32)],
        compiler_params=_cparams("arbitrary", "arbitrary", "arbitrary"),
        name="hyena_spectrum_two_stage",
    )(h, m1_real, m2)


def _lconv_two_stage_kernel(s_ref, g_ref, cws_ref, cbs_ref, cwg_ref, cbg_ref, h_ref, bias_ref,
                            m1_ref, m2_ref, m2c_ref, m3_ref, o_ref, z_ref, w_ref, *, conv_sig):
    for b in range(2):
        sig = s_ref[b].astype(F32)
        if conv_sig:
            sig = _short_conv(sig, cws_ref, cbs_ref, DEC_SEQ)
        z_ref[b] = sig
    _fwd_stage1(z_ref.at[0], z_ref.at[1], m1_ref, w_ref)
    blk = 2 * FFT_NI

    cb = w_ref.shape[1]

    def mid(kp, carry):
        rows = [pl.ds(pl.multiple_of((2 * kp + j) * blk, blk), blk) for j in range(2)]
        x = jnp.dot(m2_ref[...], jnp.concatenate([w_ref[r, :] for r in rows], axis=1).astype(BF16),
                    preferred_element_type=F32)
        h = jnp.concatenate([h_ref[r, :] for r in rows], axis=1)
        xr, xi = x[0:FFT_NI], x[FFT_NI:blk]
        hr, hi = h[0:FFT_NI], h[FFT_NI:blk]
        y = jnp.concatenate([xr * hr - xi * hi, xr * hi + xi * hr], axis=0)
        c = jnp.dot(m2c_ref[...], y.astype(BF16), preferred_element_type=F32)
        for j in range(2):
            w_ref[rows[j], :] = c[:, j * cb:(j + 1) * cb]
        return carry

    lax.fori_loop(0, FFT_NO // 2, mid, 0, unroll=FFT_UNROLL)

    def last(ni, carry):
        cr = w_ref[pl.ds(ni, FFT_NO, stride=blk), :]
        ci = w_ref[pl.ds(FFT_NI + ni, FFT_NO, stride=blk), :]
        y = jnp.dot(m3_ref[ni], jnp.concatenate([cr, ci], axis=0).astype(BF16), preferred_element_type=F32)
        o_ref[0, pl.ds(ni, FFT_HALF, stride=FFT_NI), :] = y[0:FFT_HALF]
        o_ref[1, pl.ds(ni, FFT_HALF, stride=FFT_NI), :] = y[FFT_HALF:2 * FFT_HALF]
        return carry

    lax.fori_loop(0, FFT_NI, last, 0, unroll=FFT_UNROLL)
    for b in range(2):
        gate = _short_conv(g_ref[b].astype(F32), cwg_ref, cbg_ref, DEC_SEQ)
        sig = z_ref[b]
        o_ref[b] = gate * (o_ref[b] + sig * bias_ref[...])


def _lconv_two_stage(sig, sig_col, gate_src, gate_col, conv_w, conv_b, spec, l, order, bias, mats, conv_sig):
    cb = LCONV_CB
    ncb = BRANCH_W // cb
    m1, m2, m2c, m3 = mats
    const3 = lambda c, p: (0, 0, 0)
    const2 = lambda c, p: (0, 0)
    return pl.pallas_call(
        functools.partial(_lconv_two_stage_kernel, conv_sig=conv_sig),
        grid=(ncb, DEC_BATCH // 2),
        in_specs=[
            pl.BlockSpec((2, DEC_SEQ, cb), lambda c, p: (p, 0, sig_col * ncb + c)),
            pl.BlockSpec((2, DEC_SEQ, cb), lambda c, p: (p, 0, gate_col * ncb + c)),
            pl.BlockSpec((3, cb), lambda c, p: (0, c)),
            pl.BlockSpec((1, cb), lambda c, p: (0, c)),
            pl.BlockSpec((3, cb), lambda c, p: (0, gate_col * ncb + c)),
            pl.BlockSpec((1, cb), lambda c, p: (0, gate_col * ncb + c)),
            pl.BlockSpec((None, None, 2 * FFT_N, cb), lambda c, p: (l, order, 0, c)),
            pl.BlockSpec((1, cb), lambda c, p: (0, c)),
            pl.BlockSpec(m1.shape, const3),
            pl.BlockSpec(m2.shape, const2),
            pl.BlockSpec(m2c.shape, const2),
            pl.BlockSpec(m3.shape, const3),
        ],
        out_specs=pl.BlockSpec((2, DEC_SEQ, cb), lambda c, p: (p, 0, c)),
        out_shape=jax.ShapeDtypeStruct((DEC_BATCH, DEC_SEQ, BRANCH_W), F32),
        scratch_shapes=[pltpu.VMEM((2, DEC_SEQ, cb), F32), pltpu.VMEM((2 * FFT_N, cb), F32)],
        compiler_params=_cparams("arbitrary", "arbitrary"),
        name="hyena_lconv_two_stage",
    )(sig, gate_src, conv_w, conv_b, conv_w, conv_b, spec, bias, m1, m2, m2c, m3)


def kernel(x_prompt, x_sample, cache_na_k, cache_na_v, cache_da_k, cache_da_v, c, c_ctx, w_ada, b_ada, norm_mix,
           norm_ffn, w_in, hy_conv_w, hy_conv_b, hy_filt_w1, hy_filt_b1, hy_filt_w2, hy_filt_b2, hy_filt_w3,
           hy_filt_freq, hy_bias, na_rpb, da_lambda, da_subln, w_lift, w_out, w_ffn_in, w_ffn_out, norm_final):
    TP, TS = BATCH * SEQ, DEC_BATCH * DEC_SEQ
    xp = x_prompt.reshape(TP, D_MODEL)
    xs = x_sample.reshape(TS, D_MODEL)

    cc = jnp.concatenate([c_ctx[None, :], c, jnp.zeros((8 - 1 - DEC_BATCH, D_MODEL), F32)], axis=0)
    mod = _modulation(cc, w_ada, b_ada)
    mod_p = mod[:, 0:1].reshape(DEPTH, 1, 1, 6 * D_MODEL)
    mod_s = mod[:, 1:1 + DEC_BATCH].reshape(DEPTH, DEC_BATCH, 1, 6 * D_MODEL)

    w_mix = w_in[:, :, :MIX_W].astype(BF16)
    w_gate = w_in[:, :, MIX_W:].astype(BF16)
    w_lift_b = w_lift.astype(BF16)
    w_out_b = w_out.astype(BF16)
    w_ffn_in_b = w_ffn_in.astype(BF16)
    w_ffn_out_b = w_ffn_out.astype(BF16)
    g_mix = norm_mix.reshape(DEPTH, 1, D_MODEL)
    g_ffn = norm_ffn.reshape(DEPTH, 1, D_MODEL)
    g_fin = norm_final.reshape(1, D_MODEL)
    subln = da_subln.reshape(DEPTH, 1, DA_V_DIM)
    subln_col = da_subln.reshape(DEPTH, DA_V_DIM, 1)

    w1p = jnp.pad(hy_filt_w1, ((0, 0), (0, HY_FILT_HIDDEN - HY_POS_DIM), (0, 0)))
    b1 = hy_filt_b1.reshape(DEPTH, 1, HY_FILT_HIDDEN)
    b2 = hy_filt_b2.reshape(DEPTH, 1, HY_FILT_HIDDEN)
    fr = hy_filt_freq.reshape(DEPTH, 1, HY_FILT_HIDDEN)
    mf, mi = _dft_direct_mats()
    mats = _dft_two_stage_mats()
    h_p = _hyena_filters(SEQ, w1p, b1, hy_filt_w2, b2, hy_filt_w3, fr)
    h_s = _hyena_filters(DEC_SEQ, w1p, b1, hy_filt_w2, b2, hy_filt_w3, fr)
    spec_p = _spec_direct(h_p, jnp.asarray(_dft_real_mat()))
    mf_b, mi_b = jnp.asarray(mf, dtype=BF16), jnp.asarray(mi, dtype=BF16)
    mats_b = tuple(jnp.asarray(m, dtype=BF16) for m in mats)
    spec_s = _spec_two_stage(h_s, jnp.asarray(_dft_stage1_real_mat(), dtype=BF16), mats_b[1])
    conv_b = hy_conv_b.reshape(DEPTH, 1, 3 * BRANCH_W)

    na_bias = _na_bias_table(na_rpb)
    rope_tables = _rope_tables()
    ck_na = cache_na_k.reshape(DEC_BATCH, DEPTH, PAST_LEN, BRANCH_W)
    cv_na = cache_na_v.reshape(DEC_BATCH, DEPTH, PAST_LEN, BRANCH_W)
    ck_da = cache_da_k.reshape(DEC_BATCH, DEPTH, PAST_LEN, BRANCH_W)
    cv_da = cache_da_v.reshape(DEC_BATCH, DEPTH, PAST_LEN, BRANCH_W)

    caches = tuple(jnp.zeros((BATCH, DEPTH, SEQ, BRANCH_W), F32) for _ in CACHE_BLOCKS)
    for l in range(DEPTH):
        lam_init = 0.8 - 0.6 * math.exp(-0.3 * l)
        final = l == DEPTH - 1

        u, caches = _in_proj(xp, g_mix[l], mod_p[l], w_mix, l, TP, BF16, caches=caches)
        z1 = _lconv_direct(u, 0, u, 1, hy_conv_w[l], conv_b[l], spec_p, l, 0, hy_bias[l, 0:1], mf_b, mi_b, True)
        y_hy = _lconv_direct(z1, 0, u, 2, hy_conv_w[l], conv_b[l], spec_p, l, 1, hy_bias[l, 1:2], mf_b, mi_b, False)
        y_na, y_da = _ctx_attention(u, da_lambda[l], subln_col[l], lam_init)
        xp = _merge_out(xp, g_mix[l], mod_p[l], y_hy, y_na, y_da, w_gate, w_lift_b, w_out_b, l, TP)
        xp = _ffn(xp, g_ffn[l], mod_p[l], w_ffn_in_b, w_ffn_out_b, g_fin, l, TP, final)

        u = _in_proj(xs, g_mix[l], mod_s[l], w_mix, l, DEC_SEQ, BF16)
        u3 = u.reshape(DEC_BATCH, DEC_SEQ, MIX_W)
        z1 = _lconv_two_stage(u3, 0, u3, 1, hy_conv_w[l], conv_b[l], spec_s, l, 0, hy_bias[l, 0:1], mats_b, True)
        y_hy = _lconv_two_stage(z1, 0, u3, 2, hy_conv_w[l], conv_b[l], spec_s, l, 1, hy_bias[l, 1:2], mats_b, False)
        y_hy = y_hy.reshape(TS, BRANCH_W)
        qn, kn, vn = _attn_prep(u, 3, ck_na[:, l], cv_na[:, l], NA_HEAD_DIM)
        y_na = _nbr_attention(qn, kn, vn, na_bias, l)
        q, kt, v = _attn_prep(u, 6, ck_da[:, l], cv_da[:, l], DA_HEAD_DIM, rope_tables)
        y_da = _diff_attention(q, kt, v, da_lambda[l], subln_col[l], lam_init)
        xs = _merge_out(xs, g_mix[l], mod_s[l], y_hy, y_na, y_da, w_gate, w_lift_b, w_out_b, l, DEC_SEQ)
        xs = _ffn(xs, g_ffn[l], mod_s[l], w_ffn_in_b, w_ffn_out_b, g_fin, l, DEC_SEQ, final)

    y_prompt = xp.reshape(BATCH, SEQ, D_MODEL)
    y_sample = xs.reshape(DEC_BATCH, DEC_SEQ, D_MODEL)
    heads = lambda a, d: a.reshape(BATCH, DEPTH, SEQ, BRANCH_W // d, d)
    return (y_prompt, y_sample, heads(caches[0], NA_HEAD_DIM), heads(caches[1], NA_HEAD_DIM),
            heads(caches[2], 2 * DA_HEAD_DIM), heads(caches[3], DA_V_DIM))
```

```python
import functools
import math

import numpy as np
import jax
import jax.numpy as jnp
from jax import lax
from jax.experimental import pallas as pl
from jax.experimental.pallas import tpu as pltpu

F32 = jnp.float32
BF16 = jnp.bfloat16
HIGHEST = lax.Precision.HIGHEST

D_MODEL = 1024
BATCH = 32
SEQ = 256
DEPTH = 4
DEC_BATCH = 4
DEC_SEQ = 4096
PAST_LEN = 256
GRID_W = 64
GRID_H = DEC_SEQ // GRID_W
BRANCH_W = 512
HY_POS_BANDS = 16
HY_POS_DIM = 1 + 2 * HY_POS_BANDS
HY_FILT_HIDDEN = 64
HY_DECAY_TARGET = 1e-2
HY_FAST_DECAY = 0.3
HY_SLOW_DECAY = 1.5
NA_HEADS = 8
NA_HEAD_DIM = 64
NA_WIN_ROWS = 8
NA_WIN_COLS = 16
DA_HEADS = 8
DA_HEAD_DIM = 32
DA_V_DIM = 64
D_FF = 2816
MIX_W = 9 * BRANCH_W
ROPE_BASE = 10000.0
EPS = 1e-6
NEG_INF = -1e30

VMEM_LIMIT_BYTES = 56 * 1024 * 1024
LANES = 128
MXU_DIM = 256

FFT_N = 2 * DEC_SEQ
FFT_NO = 64
FFT_NI = 128
FFT_HALF = FFT_NO // 2
FFT_UNROLL = 8
FFT_MID_UNROLL = 16
LCONV_CB = LANES


def _cparams(*sem):
    return pltpu.CompilerParams(dimension_semantics=sem, vmem_limit_bytes=VMEM_LIMIT_BYTES)


def _sigmoid(x):
    return 1.0 / (1.0 + jnp.exp(-x))


def _rms(x, g):
    return x * lax.rsqrt(jnp.mean(x * x, axis=-1, keepdims=True) + EPS) * g


def _modnorm(x, g, shift, scale):
    return _rms(x, g) * (1.0 + scale) + shift


def _bdot(a, b):
    return jnp.dot(a.astype(BF16), b.astype(BF16), preferred_element_type=F32)


def _mod_kernel(c_ref, w_ref, b_ref, o_ref):
    c = c_ref[...]
    s = c * _sigmoid(c)
    o_ref[0] = jnp.dot(s, w_ref[0], precision=HIGHEST, preferred_element_type=F32) + b_ref[0]


def _modulation(cc, w_ada, b_ada):
    nt = 6
    return pl.pallas_call(
        _mod_kernel,
        grid=(DEPTH, nt),
        in_specs=[
            pl.BlockSpec((8, D_MODEL), lambda l, j: (0, 0)),
            pl.BlockSpec((1, D_MODEL, D_MODEL), lambda l, j: (l, 0, j)),
            pl.BlockSpec((1, 1, D_MODEL), lambda l, j: (l, 0, j)),
        ],
        out_specs=pl.BlockSpec((1, 8, D_MODEL), lambda l, j: (l, 0, j)),
        out_shape=jax.ShapeDtypeStruct((DEPTH, 8, 6 * D_MODEL), F32),
        compiler_params=_cparams("arbitrary", "arbitrary"),
        name="modulation",
    )(cc, w_ada, b_ada.reshape(DEPTH, 1, 6 * D_MODEL))


IN_TM = 1024
IN_TN = 3 * BRANCH_W
CACHE_BLOCKS = (4, 5, 7, 8)


def _in_kernel(*refs, n_alias, cache_tiles):
    x_ref, g_ref, mod_ref, w_ref = refs[:4]
    o_ref = refs[4 + n_alias]
    cache_refs = refs[5 + n_alias:5 + n_alias + len(cache_tiles)]
    h_ref = refs[-1]
    j = pl.program_id(1)

    @pl.when(j == 0)
    def _():
        m = mod_ref[0]
        h = _modnorm(x_ref[...], g_ref[...], m[:, 0:D_MODEL], m[:, D_MODEL:2 * D_MODEL])
        h_ref[...] = h.astype(BF16)

    res = jnp.dot(h_ref[...], w_ref[...], preferred_element_type=F32)
    o_ref[...] = res.astype(o_ref.dtype)
    for (tile, off), c_ref in zip(cache_tiles, cache_refs):
        @pl.when(j == tile)
        def _(c_ref=c_ref, off=off):
            c_ref[...] = res[:, off:off + BRANCH_W].reshape(c_ref.shape)


def _in_proj(x, g, mod, w, l, rows_per_mod, out_dtype, caches=None):
    T = x.shape[0]
    tm, tn = IN_TM, IN_TN
    per = rows_per_mod // tm
    in_specs = [
        pl.BlockSpec((tm, D_MODEL), lambda i, j: (i, 0)),
        pl.BlockSpec((1, D_MODEL), lambda i, j: (0, 0)),
        pl.BlockSpec((1, 1, 6 * D_MODEL), lambda i, j: (i // per, 0, 0)),
        pl.BlockSpec((None, D_MODEL, tn), lambda i, j: (l, 0, j)),
    ]
    out_specs = [pl.BlockSpec((tm, tn), lambda i, j: (i, j))]
    out_shape = [jax.ShapeDtypeStruct((T, MIX_W), out_dtype)]
    args = [x, g, mod, w]
    cache_tiles, aliases = (), {}
    if caches is not None:
        seqs = tm // SEQ
        cache_tiles = tuple(divmod(c * BRANCH_W, tn) for c in CACHE_BLOCKS)
        out_specs += [pl.BlockSpec((seqs, 1, SEQ, BRANCH_W), lambda i, j: (i, l, 0, 0))] * len(CACHE_BLOCKS)
        out_shape += [jax.ShapeDtypeStruct(c.shape, c.dtype) for c in caches]
        in_specs += [pl.BlockSpec(memory_space=pl.ANY)] * len(caches)
        aliases = {4 + n: 1 + n for n in range(len(caches))}
        args += list(caches)
    n_alias = len(args) - 4
    outs = pl.pallas_call(
        functools.partial(_in_kernel, n_alias=n_alias, cache_tiles=cache_tiles),
        grid=(T // tm, MIX_W // tn),
        in_specs=in_specs,
        out_specs=out_specs,
        out_shape=out_shape,
        input_output_aliases=aliases,
        scratch_shapes=[pltpu.VMEM((tm, D_MODEL), BF16)],
        compiler_params=_cparams("arbitrary", "arbitrary"),
        name="in_proj",
    )(*args)
    return outs[0] if caches is None else (outs[0], tuple(outs[1:]))


def _mid_kernel(x_ref, g_ref, mod_ref, yh_ref, yn_ref, yd_ref, wg_ref, wl_ref, wo_ref, o_ref):
    m = mod_ref[0]
    x = x_ref[...]
    h = _modnorm(x, g_ref[...], m[:, 0:D_MODEL], m[:, D_MODEL:2 * D_MODEL]).astype(BF16)
    merged = None
    for br, y_ref in enumerate((yh_ref, yn_ref, yd_ref)):
        gate = _sigmoid(jnp.dot(h, wg_ref[:, br * D_MODEL:(br + 1) * D_MODEL], preferred_element_type=F32))
        lift = jnp.dot(y_ref[...].astype(BF16), wl_ref[br], preferred_element_type=F32)
        t = gate * lift
        merged = t if merged is None else merged + t
    o_ref[...] = x + m[:, 2 * D_MODEL:3 * D_MODEL] * _bdot(merged, wo_ref[...])


def _merge_out(x, g, mod, y_hy, y_na, y_da, w_gate, w_lift, w_out, l, rows_per_mod):
    T = x.shape[0]
    tm = 512
    per = rows_per_mod // tm
    row = lambda i: (i, 0)
    const2 = lambda i: (0, 0)
    return pl.pallas_call(
        _mid_kernel,
        grid=(T // tm,),
        in_specs=[
            pl.BlockSpec((tm, D_MODEL), row),
            pl.BlockSpec((1, D_MODEL), const2),
            pl.BlockSpec((1, 1, 6 * D_MODEL), lambda i: (i // per, 0, 0)),
            pl.BlockSpec((tm, BRANCH_W), row),
            pl.BlockSpec((tm, BRANCH_W), row),
            pl.BlockSpec((tm, BRANCH_W), row),
            pl.BlockSpec((None, D_MODEL, 3 * D_MODEL), lambda i: (l, 0, 0)),
            pl.BlockSpec((None, 3, BRANCH_W, D_MODEL), lambda i: (l, 0, 0, 0)),
            pl.BlockSpec((None, D_MODEL, D_MODEL), lambda i: (l, 0, 0)),
        ],
        out_specs=pl.BlockSpec((tm, D_MODEL), row),
        out_shape=jax.ShapeDtypeStruct((T, D_MODEL), F32),
        compiler_params=_cparams("arbitrary"),
        name="merge_out",
    )(x, g, mod, y_hy, y_na, y_da, w_gate, w_lift, w_out)


FFN_CHUNK = D_FF // 2


def _ffn_kernel(x_ref, g_ref, mod_ref, w1g_ref, w1u_ref, w2_ref, gf_ref, o_ref, h_ref, acc_ref, *, final):
    k = pl.program_id(1)

    @pl.when(k == 0)
    def _():
        m = mod_ref[0]
        h = _modnorm(x_ref[...], g_ref[...], m[:, 3 * D_MODEL:4 * D_MODEL], m[:, 4 * D_MODEL:5 * D_MODEL])
        h_ref[...] = h.astype(BF16)

    h = h_ref[...]
    a = jnp.dot(h, w1g_ref[...], preferred_element_type=F32)
    b = jnp.dot(h, w1u_ref[...], preferred_element_type=F32)
    part = _bdot(a * _sigmoid(a) * b, w2_ref[...])

    @pl.when(k == 0)
    def _():
        acc_ref[...] = part

    @pl.when(k == 1)
    def _():
        m = mod_ref[0]
        xn = x_ref[...] + m[:, 5 * D_MODEL:6 * D_MODEL] * (acc_ref[...] + part)
        if final:
            xn = _rms(xn, gf_ref[...])
        o_ref[...] = xn


def _ffn(x, g, mod, w_ffn_in, w_ffn_out, g_final, l, rows_per_mod, final):
    T = x.shape[0]
    tm = 512
    per = rows_per_mod // tm
    return pl.pallas_call(
        functools.partial(_ffn_kernel, final=final),
        grid=(T // tm, 2),
        in_specs=[
            pl.BlockSpec((tm, D_MODEL), lambda i, k: (i, 0)),
            pl.BlockSpec((1, D_MODEL), lambda i, k: (0, 0)),
            pl.BlockSpec((1, 1, 6 * D_MODEL), lambda i, k: (i // per, 0, 0)),
            pl.BlockSpec((None, D_MODEL, FFN_CHUNK), lambda i, k: (l, 0, k)),
            pl.BlockSpec((None, D_MODEL, FFN_CHUNK), lambda i, k: (l, 0, 2 + k)),
            pl.BlockSpec((None, FFN_CHUNK, D_MODEL), lambda i, k: (l, k, 0)),
            pl.BlockSpec((1, D_MODEL), lambda i, k: (0, 0)),
        ],
        out_specs=pl.BlockSpec((tm, D_MODEL), lambda i, k: (i, 0)),
        out_shape=jax.ShapeDtypeStruct((T, D_MODEL), F32),
        scratch_shapes=[pltpu.VMEM((tm, D_MODEL), BF16), pltpu.VMEM((tm, D_MODEL), F32)],
        compiler_params=_cparams("arbitrary", "arbitrary"),
        name="ffn",
    )(x, g, mod, w_ffn_in, w_ffn_in, w_ffn_out, g_final)


def _da_lambda(lam_ref, lam_init):
    lp = lam_ref[...]
    a = jnp.sum(lp[0:1] * lp[1:2], axis=1, keepdims=True)
    b = jnp.sum(lp[2:3] * lp[3:4], axis=1, keepdims=True)
    return jnp.exp(a) - jnp.exp(b) + lam_init


ATT_ONES_ROWS = 16
ATT_TQ = 256
ATT_KEYS = DEC_SEQ + PAST_LEN
LOG2E = math.log2(math.e)


def _masked_q_blocks(qt, d):
    row = lax.broadcasted_iota(jnp.int32, qt.shape, 0)
    zero = jnp.zeros_like(qt)
    return jnp.concatenate([jnp.where((row >= j * d) & (row < (j + 1) * d), qt, zero) for j in range(LANES // d)],
                           axis=1)


def _colmax(st):
    keys, n = st.shape
    return jnp.max(jnp.max(st.reshape(keys // MXU_DIM, MXU_DIM, n), axis=0), axis=0, keepdims=True)


def _ctx_attn_kernel(nq_ref, nk_ref, nv_ref, dq_ref, dk_ref, dv_ref, lam_ref, sub_ref, yn_ref, yd_ref, acc_ref,
                     *, lam_init):
    lam = _da_lambda(lam_ref, lam_init)
    ones = jnp.ones((ATT_ONES_ROWS, SEQ), BF16)

    def attend(q_ref, k_ref, v_ref, d, maps_per_head, finish):
        qt = (q_ref[...].astype(F32) * (d ** -0.5 * LOG2E)).T.astype(BF16)
        vt = v_ref[...].astype(F32).T.astype(BF16)
        kb = k_ref[...].astype(BF16)
        dv = NA_HEAD_DIM
        heads_per_group = LANES // (d * maps_per_head)
        w = maps_per_head * SEQ
        for g in range(BRANCH_W // LANES):
            lanes = slice(g * LANES, (g + 1) * LANES)
            st = jnp.dot(kb[:, lanes], _masked_q_blocks(qt[lanes], d), preferred_element_type=F32)
            pt = jnp.exp2(st - _colmax(st)).astype(BF16)
            for j in range(heads_per_group):
                h = g * heads_per_group + j
                ve = jnp.concatenate([vt[h * dv:(h + 1) * dv], ones], axis=0)
                oe = jnp.dot(ve, pt[:, j * w:(j + 1) * w], preferred_element_type=F32)
                os = [oe[0:dv, i * SEQ:(i + 1) * SEQ] / oe[dv:dv + 1, i * SEQ:(i + 1) * SEQ]
                      for i in range(maps_per_head)]
                acc_ref[h * dv:(h + 1) * dv, :] = finish(os)

    attend(nq_ref, nk_ref, nv_ref, NA_HEAD_DIM, 1, lambda os: os[0])
    yn_ref[...] = acc_ref[...].T.astype(yn_ref.dtype)

    def da_finish(os):
        ot = os[0] - lam * os[1]
        ot = ot * lax.rsqrt(jnp.mean(ot * ot, axis=0, keepdims=True) + EPS) * sub_ref[...]
        return ot * (1.0 - lam_init)

    attend(dq_ref, dk_ref, dv_ref, DA_HEAD_DIM, 2, da_finish)
    yd_ref[...] = acc_ref[...].T.astype(yd_ref.dtype)


def _ctx_attention(u, da_lambda, subln_col, lam_init):
    col = lambda j: pl.BlockSpec((SEQ, BRANCH_W), lambda b, j=j: (b, j))
    out = pl.BlockSpec((SEQ, BRANCH_W), lambda b: (b, 0))
    shape = jax.ShapeDtypeStruct((BATCH * SEQ, BRANCH_W), BF16)
    return pl.pallas_call(
        functools.partial(_ctx_attn_kernel, lam_init=lam_init),
        grid=(BATCH,),
        in_specs=[col(3), col(4), col(5), col(6), col(7), col(8),
                  pl.BlockSpec((4, DA_HEAD_DIM), lambda b: (0, 0)),
                  pl.BlockSpec((DA_V_DIM, 1), lambda b: (0, 0))],
        out_specs=[out, out],
        out_shape=[shape, shape],
        scratch_shapes=[pltpu.VMEM((BRANCH_W, SEQ), F32)],
        compiler_params=_cparams("arbitrary"),
        name="ctx_attention",
    )(u, u, u, u, u, u, da_lambda, subln_col)


def _rope(x, cos, sin_signed):
    n = x.shape[-1]
    lane = lax.broadcasted_iota(jnp.int32, x.shape, 1)
    partner = jnp.where(lane % 2 == 0, pltpu.roll(x, n - 1, axis=1), pltpu.roll(x, 1, axis=1))
    return x * cos + partner * sin_signed


def _attn_prep_kernel(q_ref, k_ref, v_ref, kc_ref, vc_ref, *refs, rope, scale):
    cos_ref, sin_ref = refs[:2] if rope else (None, None)
    qt_ref, ko_ref, vt_ref = refs[-3:]
    t = pl.program_id(1)
    dv = NA_HEAD_DIM

    def put_v(v):
        vt = v.astype(F32).T.astype(BF16)
        ones = jnp.ones((ATT_ONES_ROWS, ATT_TQ), BF16)
        for h in range(BRANCH_W // dv):
            vt_ref[0, h, 0:dv, :] = vt[h * dv:(h + 1) * dv]
            vt_ref[0, h, dv:dv + ATT_ONES_ROWS, :] = ones

    @pl.when(t < DEC_SEQ // ATT_TQ)
    def _():
        q = q_ref[...].astype(F32)
        k = k_ref[...].astype(F32)
        if rope:
            q = _rope(q, cos_ref[...], sin_ref[...])
            k = _rope(k, cos_ref[...], sin_ref[...])
        qt_ref[0] = (q * scale).T.astype(BF16)
        ko_ref[0] = k.astype(BF16)
        put_v(v_ref[...])

    @pl.when(t == DEC_SEQ // ATT_TQ)
    def _():
        ko_ref[0] = kc_ref[0].astype(BF16)
        put_v(vc_ref[0])


def _attn_prep(u, first_col, k_ctx, v_ctx, head_dim, rope_tables=None):
    rope = rope_tables is not None
    nt = DEC_SEQ // ATT_TQ
    last = nt - 1
    rowblk = lambda j: pl.BlockSpec((ATT_TQ, BRANCH_W), lambda b, t, j=j: (b * nt + jnp.minimum(t, last), j))
    tab = pl.BlockSpec((ATT_TQ, BRANCH_W), lambda b, t: (jnp.minimum(t, last), 0))
    ctx = pl.BlockSpec((1, PAST_LEN, BRANCH_W), lambda b, t: (b, 0, 0))
    heads = BRANCH_W // NA_HEAD_DIM
    vrows = NA_HEAD_DIM + ATT_ONES_ROWS
    return pl.pallas_call(
        functools.partial(_attn_prep_kernel, rope=rope, scale=head_dim ** -0.5 * LOG2E),
        grid=(DEC_BATCH, nt + 1),
        in_specs=[rowblk(first_col), rowblk(first_col + 1), rowblk(first_col + 2), ctx, ctx] + [tab, tab] * rope,
        out_specs=[
            pl.BlockSpec((1, BRANCH_W, ATT_TQ), lambda b, t: (b, 0, jnp.minimum(t, last))),
            pl.BlockSpec((1, ATT_TQ, BRANCH_W), lambda b, t: (b, t, 0)),
            pl.BlockSpec((1, heads, vrows, ATT_TQ), lambda b, t: (b, 0, 0, t)),
        ],
        out_shape=[
            jax.ShapeDtypeStruct((DEC_BATCH, BRANCH_W, DEC_SEQ), BF16),
            jax.ShapeDtypeStruct((DEC_BATCH, ATT_KEYS, BRANCH_W), BF16),
            jax.ShapeDtypeStruct((DEC_BATCH, heads, vrows, ATT_KEYS), BF16),
        ],
        compiler_params=_cparams("arbitrary", "arbitrary"),
        name="attn_prep",
    )(u, u, u, k_ctx, v_ctx, *(rope_tables or ()))


NA_ROWS = ATT_TQ // GRID_W
NA_UNION = 3 * NA_ROWS
NA_STEPS = GRID_H // NA_ROWS
NA_SLABS = NA_UNION // NA_ROWS
NA_VARIANT_OFFSET = (0, -NA_ROWS, -2 * NA_ROWS)


def _na_variant(s):
    return jnp.minimum(s, 1) + s // (NA_STEPS - 1)


def _na_window_block(s):
    return jnp.clip(s - 1, 0, NA_STEPS - NA_SLABS)


def _na_bias_kernel(rpb_ref, o_ref):
    kc = lax.broadcasted_iota(jnp.int32, (GRID_W, GRID_W), 0)
    qc = lax.broadcasted_iota(jnp.int32, (GRID_W, GRID_W), 1)
    dc = jnp.clip(kc - qc, -(NA_WIN_COLS - 1), NA_WIN_COLS - 1) + (NA_WIN_COLS - 1)
    c0 = jnp.clip(qc - NA_WIN_COLS // 2, 0, GRID_W - NA_WIN_COLS)
    col_ok = (kc >= c0) & (kc < c0 + NA_WIN_COLS)
    r = rpb_ref[0, 0] * LOG2E
    masked = jnp.full((GRID_W, GRID_W), NEG_INF, F32)
    tiles = []
    for dr in range(2 * NA_WIN_ROWS - 1):
        acc = jnp.zeros((GRID_W, GRID_W), F32)
        for d in range(2 * NA_WIN_COLS - 1):
            acc = jnp.where(dc == d, r[dr:dr + 1, d:d + 1], acc)
        tiles.append(jnp.where(col_ok, acc, masked))
    for v, off in enumerate(NA_VARIANT_OFFSET):
        for kr in range(NA_UNION):
            for rr in range(NA_ROWS):
                w0 = (0, rr, NA_UNION - NA_WIN_ROWS)[v]
                dr = kr + off - rr
                inside = w0 <= kr < w0 + NA_WIN_ROWS
                o_ref[0, v, 0, kr * GRID_W:(kr + 1) * GRID_W, rr * GRID_W:(rr + 1) * GRID_W] = (
                    tiles[dr + NA_WIN_ROWS - 1] if inside else masked)


def _na_bias_table(na_rpb):
    n_dr, n_dc = 2 * NA_WIN_ROWS - 1, 2 * NA_WIN_COLS - 1
    nv = len(NA_VARIANT_OFFSET)
    return pl.pallas_call(
        _na_bias_kernel,
        grid=(DEPTH, NA_HEADS),
        in_specs=[pl.BlockSpec((1, 1, n_dr, n_dc), lambda l, h: (l, h, 0, 0))],
        out_specs=pl.BlockSpec((1, nv, 1, NA_UNION * GRID_W, ATT_TQ), lambda l, h: (l, 0, h, 0, 0)),
        out_shape=jax.ShapeDtypeStruct((DEPTH, nv, NA_HEADS, NA_UNION * GRID_W, ATT_TQ), F32),
        compiler_params=_cparams("arbitrary", "arbitrary"),
        name="na_bias_table",
    )(na_rpb)


def _na_kernel(qt_ref, *refs):
    n = NA_SLABS + 1
    k_refs, vt_refs = refs[:n], refs[n:2 * n]
    bias_ref, o_ref, acc_ref = refs[2 * n:]
    dv = NA_HEAD_DIM
    heads_per_group = LANES // dv
    for g in range(BRANCH_W // LANES):
        lanes = slice(g * LANES, (g + 1) * LANES)
        qbd = _masked_q_blocks(qt_ref[0, lanes, :], dv)
        sts = []
        for j, k_ref in enumerate(k_refs):
            st = jnp.dot(k_ref[0, :, lanes], qbd, preferred_element_type=F32)
            if j < NA_SLABS:
                rows = slice(j * ATT_TQ, (j + 1) * ATT_TQ)
                st = st + jnp.concatenate(
                    [bias_ref[0, g * heads_per_group + hh, rows, :] for hh in range(heads_per_group)], axis=1)
            sts.append(st)
        mx = functools.reduce(jnp.maximum, [_colmax(st) for st in sts])
        pts = [jnp.exp2(st - mx).astype(BF16) for st in sts]
        for hh in range(heads_per_group):
            h = g * heads_per_group + hh
            oe = sum(jnp.dot(vt_ref[0, h], pt[:, hh * ATT_TQ:(hh + 1) * ATT_TQ], preferred_element_type=F32)
                     for vt_ref, pt in zip(vt_refs, pts))
            acc_ref[h * dv:(h + 1) * dv, :] = oe[0:dv] / oe[dv:dv + 1]
    o_ref[...] = acc_ref[...].T.astype(o_ref.dtype)


def _nbr_attention(qt, k, vt, bias, l):
    vrows = NA_HEAD_DIM + ATT_ONES_ROWS
    ctx_blk = DEC_SEQ // ATT_TQ
    k_specs = [pl.BlockSpec((1, ATT_TQ, BRANCH_W), lambda b, s, j=j: (b, _na_window_block(s) + j, 0))
               for j in range(NA_SLABS)]
    k_specs.append(pl.BlockSpec((1, ATT_TQ, BRANCH_W), lambda b, s: (b, ctx_blk, 0)))
    vt_specs = [pl.BlockSpec((1, NA_HEADS, vrows, ATT_TQ), lambda b, s, j=j: (b, 0, 0, _na_window_block(s) + j))
                for j in range(NA_SLABS)]
    vt_specs.append(pl.BlockSpec((1, NA_HEADS, vrows, ATT_TQ), lambda b, s: (b, 0, 0, ctx_blk)))
    n = NA_SLABS + 1
    return pl.pallas_call(
        _na_kernel,
        grid=(DEC_BATCH, NA_STEPS),
        in_specs=[pl.BlockSpec((1, BRANCH_W, ATT_TQ), lambda b, s: (b, 0, s))] + k_specs + vt_specs + [
            pl.BlockSpec((None, 1, NA_HEADS, NA_UNION * GRID_W, ATT_TQ), lambda b, s: (l, _na_variant(s), 0, 0, 0))],
        out_specs=pl.BlockSpec((ATT_TQ, BRANCH_W), lambda b, s: (b * NA_STEPS + s, 0)),
        out_shape=jax.ShapeDtypeStruct((DEC_BATCH * DEC_SEQ, BRANCH_W), BF16),
        scratch_shapes=[pltpu.VMEM((BRANCH_W, ATT_TQ), F32)],
        compiler_params=_cparams("arbitrary", "arbitrary"),
        name="nbr_attention",
    )(qt, *([k] * n), *([vt] * n), bias)


DA_TQ = ATT_TQ
DA_KEYS = ATT_KEYS
DA_ONES_ROWS = ATT_ONES_ROWS
DA_MAPS_PER_TILE = LANES // DA_HEAD_DIM


def _da_kernel(qt_ref, k_ref, vt_ref, lam_ref, sub_ref, o_ref, acc_ref, *, lam_init):
    lam = _da_lambda(lam_ref, lam_init)
    heads = DA_MAPS_PER_TILE // 2
    for g in range(BRANCH_W // LANES):
        lanes = slice(g * LANES, (g + 1) * LANES)
        st = jnp.dot(k_ref[0, :, lanes], _masked_q_blocks(qt_ref[0, lanes, :], DA_HEAD_DIM),
                     preferred_element_type=F32)
        pt = jnp.exp2(st - _colmax(st)).astype(BF16)
        for hh in range(heads):
            h = g * heads + hh
            oe = jnp.dot(vt_ref[0, h], pt[:, 2 * hh * DA_TQ:(2 * hh + 2) * DA_TQ], preferred_element_type=F32)
            os = [oe[0:DA_V_DIM, i * DA_TQ:(i + 1) * DA_TQ] / oe[DA_V_DIM:DA_V_DIM + 1, i * DA_TQ:(i + 1) * DA_TQ]
                  for i in range(2)]
            ot = os[0] - lam * os[1]
            ot = ot * lax.rsqrt(jnp.mean(ot * ot, axis=0, keepdims=True) + EPS) * sub_ref[...]
            acc_ref[h * DA_V_DIM:(h + 1) * DA_V_DIM, :] = ot * (1.0 - lam_init)
    o_ref[...] = acc_ref[...].T.astype(o_ref.dtype)


def _diff_attention(qt, k, vt, da_lambda, subln_col, lam_init):
    nt = DEC_SEQ // DA_TQ
    vrows = DA_V_DIM + DA_ONES_ROWS
    return pl.pallas_call(
        functools.partial(_da_kernel, lam_init=lam_init),
        grid=(DEC_BATCH, nt),
        in_specs=[
            pl.BlockSpec((1, BRANCH_W, DA_TQ), lambda b, t: (b, 0, t)),
            pl.BlockSpec((1, DA_KEYS, BRANCH_W), lambda b, t: (b, 0, 0)),
            pl.BlockSpec((1, DA_HEADS, vrows, DA_KEYS), lambda b, t: (b, 0, 0, 0)),
            pl.BlockSpec((4, DA_HEAD_DIM), lambda b, t: (0, 0)),
            pl.BlockSpec((DA_V_DIM, 1), lambda b, t: (0, 0)),
        ],
        out_specs=pl.BlockSpec((DA_TQ, BRANCH_W), lambda b, t: (b * nt + t, 0)),
        out_shape=jax.ShapeDtypeStruct((DEC_BATCH * DEC_SEQ, BRANCH_W), BF16),
        scratch_shapes=[pltpu.VMEM((BRANCH_W, DA_TQ), F32)],
        compiler_params=_cparams("arbitrary", "arbitrary"),
        name="diff_attention",
    )(qt, k, vt, da_lambda, subln_col)


def _rope_tables():
    pos = np.arange(DEC_SEQ)
    row = (pos // GRID_W).astype(np.float32)
    col = (pos % GRID_W).astype(np.float32)
    n_freq = DA_HEAD_DIM // 4
    inv = (np.float32(ROPE_BASE) ** (-np.arange(n_freq, dtype=np.float32) / n_freq)).astype(np.float32)
    ang = np.concatenate([row[:, None] * inv[None, :], col[:, None] * inv[None, :]], axis=-1)
    ang = ang.astype(np.float64)
    cos = np.repeat(np.cos(ang), 2, axis=-1)
    sin = np.repeat(np.sin(ang), 2, axis=-1)
    sign = np.where(np.arange(DA_HEAD_DIM) % 2 == 0, -1.0, 1.0)
    reps = BRANCH_W // DA_HEAD_DIM
    cos = np.tile(cos, (1, reps)).astype(np.float32)
    sin = np.tile(sin * sign[None, :], (1, reps)).astype(np.float32)
    return jnp.asarray(cos), jnp.asarray(sin)


def _filt_hidden_kernel(feat_ref, w1_ref, b1_ref, w2_ref, b2_ref, fr_ref, o_ref):
    fr = fr_ref[0]
    h = jnp.sin(fr * (jnp.dot(feat_ref[...], w1_ref[0], precision=HIGHEST, preferred_element_type=F32) + b1_ref[0]))
    o_ref[0] = jnp.sin(fr * (jnp.dot(h, w2_ref[0], precision=HIGHEST, preferred_element_type=F32) + b2_ref[0]))


def _filt_kernel(h_ref, w3f_ref, w3b_ref, dec_ref, o_ref):
    L = dec_ref.shape[0] // 2
    hf = jnp.dot(h_ref[0, 0:L], w3f_ref[0], precision=HIGHEST, preferred_element_type=F32) * dec_ref[0:L]
    hb = jnp.dot(h_ref[0, L:2 * L], w3b_ref[0], precision=HIGHEST, preferred_element_type=F32) * dec_ref[L:2 * L]
    row = lax.broadcasted_iota(jnp.int32, hb.shape, 0)
    hb = jnp.where(row == 0, 0.0, hb)
    nrm = jnp.sum(jnp.abs(hf), axis=0, keepdims=True) + jnp.sum(jnp.abs(hb), axis=0, keepdims=True)
    o_ref[0, 0, 0:L] = hf / nrm
    o_ref[0, 0, L:2 * L] = hb / nrm


def _circular_order(a):
    return np.concatenate([a, a[:1], a[1:][::-1]], axis=0)


def _hyena_pos_tables(L):
    f32 = np.float32
    pos = np.arange(L, dtype=f32)
    t = (pos / f32(L)).astype(f32)
    bands = np.linspace(1e-4, HY_POS_BANDS - 1, HY_POS_BANDS, dtype=f32)
    ang = (f32(2 * math.pi / L) * pos[:, None] * bands[None, :]).astype(np.float64)
    feats = np.zeros((L, HY_FILT_HIDDEN), f32)
    feats[:, 0] = t
    feats[:, 1:1 + HY_POS_BANDS] = np.cos(ang)
    feats[:, 1 + HY_POS_BANDS:HY_POS_DIM] = -np.sin(ang)
    deltas = np.linspace(math.log(HY_DECAY_TARGET) / HY_SLOW_DECAY,
                         math.log(HY_DECAY_TARGET) / HY_FAST_DECAY, BRANCH_W, dtype=f32)
    decay = np.exp((-t[:, None] * np.abs(deltas)[None, :]).astype(np.float64)).astype(f32)
    return jnp.asarray(_circular_order(feats)), jnp.asarray(_circular_order(decay))


def _hyena_filters(half, w1p, b1, w2, b2, w3, freq):
    feats, decay = _hyena_pos_tables(half)
    L = 2 * half
    cb = LANES
    ncb = BRANCH_W // cb
    small = lambda shape: pl.BlockSpec((1,) + shape, lambda l: (l, 0, 0))
    hidden = pl.pallas_call(
        _filt_hidden_kernel,
        grid=(DEPTH,),
        in_specs=[
            pl.BlockSpec((L, HY_FILT_HIDDEN), lambda l: (0, 0)),
            small((HY_FILT_HIDDEN, HY_FILT_HIDDEN)), small((1, HY_FILT_HIDDEN)),
            small((HY_FILT_HIDDEN, HY_FILT_HIDDEN)), small((1, HY_FILT_HIDDEN)),
            small((1, HY_FILT_HIDDEN)),
        ],
        out_specs=pl.BlockSpec((1, L, HY_FILT_HIDDEN), lambda l: (l, 0, 0)),
        out_shape=jax.ShapeDtypeStruct((DEPTH, L, HY_FILT_HIDDEN), F32),
        compiler_params=_cparams("arbitrary"),
        name=f"hyena_filter_hidden_{L}",
    )(feats, w1p, b1, w2, b2, freq)
    return pl.pallas_call(
        _filt_kernel,
        grid=(DEPTH, 2, ncb),
        in_specs=[
            pl.BlockSpec((1, L, HY_FILT_HIDDEN), lambda l, o, c: (l, 0, 0)),
            pl.BlockSpec((1, HY_FILT_HIDDEN, cb), lambda l, o, c: (l, 0, o * 2 * ncb + c)),
            pl.BlockSpec((1, HY_FILT_HIDDEN, cb), lambda l, o, c: (l, 0, o * 2 * ncb + ncb + c)),
            pl.BlockSpec((L, cb), lambda l, o, c: (0, c)),
        ],
        out_specs=pl.BlockSpec((1, 1, L, cb), lambda l, o, c: (l, o, 0, c)),
        out_shape=jax.ShapeDtypeStruct((DEPTH, 2, L, BRANCH_W), F32),
        compiler_params=_cparams("arbitrary", "arbitrary", "arbitrary"),
        name=f"hyena_filters_{L}",
    )(hidden, w3, w3, decay)


def _short_conv(u, w_ref, b_ref, seq_len):
    n = u.shape[0]
    t = lax.broadcasted_iota(jnp.int32, u.shape, 0) % seq_len
    prev = jnp.where(t == 0, 0.0, pltpu.roll(u, 1, axis=0))
    nxt = jnp.where(t == seq_len - 1, 0.0, pltpu.roll(u, n - 1, axis=0))
    return prev * w_ref[0:1, :] + u * w_ref[1:2, :] + nxt * w_ref[2:3, :] + b_ref[...]


def _dft_direct_mats():
    n, half = 2 * SEQ, SEQ
    k = np.arange(n)[:, None].astype(np.float64)
    t = np.arange(half)[None, :].astype(np.float64)
    ang = 2 * np.pi * k * t / n
    fr, fi = np.cos(ang), -np.sin(ang)
    mf = np.block([[fr, -fi], [fi, fr]])
    gr, gi = np.cos(ang).T / n, np.sin(ang).T / n
    mi = np.block([[gr, -gi], [gi, gr]])
    return mf.astype(np.float32), mi.astype(np.float32)


def _dft_real_mat():
    n = 2 * SEQ
    ang = 2 * np.pi * np.arange(n)[:, None].astype(np.float64) * np.arange(n)[None, :] / n
    return np.concatenate([np.cos(ang), -np.sin(ang)], axis=0).astype(np.float32)


def _spec_direct_kernel(h_ref, m_ref, o_ref):
    o_ref[0, 0] = jnp.dot(m_ref[...], h_ref[0, 0], precision=HIGHEST, preferred_element_type=F32)


def _spec_direct(h, m_real):
    n = 2 * SEQ
    return pl.pallas_call(
        _spec_direct_kernel,
        grid=(DEPTH, 2),
        in_specs=[pl.BlockSpec((1, 1, n, BRANCH_W), lambda l, o: (l, o, 0, 0)),
                  pl.BlockSpec((2 * n, n), lambda l, o: (0, 0))],
        out_specs=pl.BlockSpec((1, 1, 2 * n, BRANCH_W), lambda l, o: (l, o, 0, 0)),
        out_shape=jax.ShapeDtypeStruct((DEPTH, 2, 2 * n, BRANCH_W), F32),
        compiler_params=_cparams("arbitrary", "arbitrary"),
        name="hyena_spectrum_direct",
    )(h, m_real)


def _lconv_direct_kernel(s_ref, g_ref, cws_ref, cbs_ref, cwg_ref, cbg_ref, h_ref, bias_ref, mf_ref, mi_ref, o_ref,
                         *, conv_sig):
    n = 2 * SEQ
    sig = s_ref[...].astype(F32)
    if conv_sig:
        sig = _short_conv(sig, cws_ref, cbs_ref, SEQ)
    gate = _short_conv(g_ref[...].astype(F32), cwg_ref, cbg_ref, SEQ)
    z = jnp.dot(mf_ref[...], sig.astype(BF16), preferred_element_type=F32)
    zr, zi = z[0:n], z[n:2 * n]
    hr, hi = h_ref[0:n], h_ref[n:2 * n]
    y = jnp.concatenate([zr * hr - zi * hi, zr * hi + zi * hr], axis=0)
    y = jnp.dot(mi_ref[...], y.astype(BF16), preferred_element_type=F32)
    o_ref[...] = gate * (y + sig * bias_ref[...])


def _lconv_direct(sig, sig_col, gate_src, gate_col, conv_w, conv_b, spec, l, order, bias, mf, mi, conv_sig):
    n = 2 * SEQ
    rows = 2 * SEQ
    T = sig.shape[0]
    return pl.pallas_call(
        functools.partial(_lconv_direct_kernel, conv_sig=conv_sig),
        grid=(T // rows,),
        in_specs=[
            pl.BlockSpec((rows, BRANCH_W), lambda p: (p, sig_col)),
            pl.BlockSpec((rows, BRANCH_W), lambda p: (p, gate_col)),
            pl.BlockSpec((3, BRANCH_W), lambda p: (0, 0)),
            pl.BlockSpec((1, BRANCH_W), lambda p: (0, 0)),
            pl.BlockSpec((3, BRANCH_W), lambda p: (0, gate_col)),
            pl.BlockSpec((1, BRANCH_W), lambda p: (0, gate_col)),
            pl.BlockSpec((None, None, 2 * n, BRANCH_W), lambda p: (l, order, 0, 0)),
            pl.BlockSpec((1, BRANCH_W), lambda p: (0, 0)),
            pl.BlockSpec((2 * n, rows), lambda p: (0, 0)),
            pl.BlockSpec((rows, 2 * n), lambda p: (0, 0)),
        ],
        out_specs=pl.BlockSpec((rows, BRANCH_W), lambda p: (p, 0)),
        out_shape=jax.ShapeDtypeStruct((T, BRANCH_W), F32),
        compiler_params=_cparams("arbitrary"),
        name="hyena_lconv_direct",
    )(sig, gate_src, conv_w, conv_b, conv_w, conv_b, spec, bias, mf, mi)


def _dft_two_stage_mats():
    no, ni, half, n = FFT_NO, FFT_NI, FFT_HALF, FFT_N
    f64 = np.float64
    k1 = np.arange(no, dtype=f64)
    n_o = np.arange(half, dtype=f64)
    n_i = np.arange(ni, dtype=f64)
    ang = 2 * np.pi * (n_i[:, None, None] * k1[None, :, None] / n + k1[None, :, None] * n_o[None, None, :] / no)
    tr, ti = np.cos(ang), -np.sin(ang)
    m1 = np.concatenate([np.concatenate([tr, -ti], axis=2), np.concatenate([ti, tr], axis=2)], axis=1)
    k2 = np.arange(ni, dtype=f64)
    ang2 = 2 * np.pi * k2[:, None] * n_i[None, :] / ni
    f2r, f2i = np.cos(ang2), -np.sin(ang2)
    m2 = np.block([[f2r, -f2i], [f2i, f2r]])
    m2c = np.block([[f2r, f2i], [-f2i, f2r]])
    sr, si = np.transpose(tr, (0, 2, 1)) / n, -np.transpose(ti, (0, 2, 1)) / n
    m3 = np.concatenate([np.concatenate([sr, -si], axis=2), np.concatenate([si, sr], axis=2)], axis=1)
    return (m1.astype(np.float32), m2.astype(np.float32), m2c.astype(np.float32), m3.astype(np.float32))


def _dft_stage1_real_mat():
    no, ni, n = FFT_NO, FFT_NI, FFT_N
    k1 = np.arange(no, dtype=np.float64)
    n_o = np.arange(no, dtype=np.float64)
    n_i = np.arange(ni, dtype=np.float64)
    ang = 2 * np.pi * (n_i[:, None, None] * k1[None, :, None] / n + k1[None, :, None] * n_o[None, None, :] / no)
    return np.concatenate([np.cos(ang), -np.sin(ang)], axis=1).astype(np.float32)


def _store_stage1(w_ref, ni, out):
    w_ref[pl.ds(ni, FFT_NO, stride=2 * FFT_NI), :] = out[0:FFT_NO]
    w_ref[pl.ds(FFT_NI + ni, FFT_NO, stride=2 * FFT_NI), :] = out[FFT_NO:2 * FFT_NO]


def _fwd_stage1(za_ref, zb_ref, m1_ref, w_ref):
    def body(ni, carry):
        a = za_ref[pl.ds(ni, FFT_HALF, stride=FFT_NI), :]
        b = zb_ref[pl.ds(ni, FFT_HALF, stride=FFT_NI), :]
        out = jnp.dot(m1_ref[ni], jnp.concatenate([a, b], axis=0).astype(BF16), preferred_element_type=F32)
        _store_stage1(w_ref, ni, out)
        return carry

    lax.fori_loop(0, FFT_NI, body, 0, unroll=FFT_UNROLL)


def _spec_two_stage_kernel(h_ref, m1_ref, m2_ref, o_ref, w_ref):
    h = h_ref.at[0, 0]

    def stage1(ni, carry):
        a = h[pl.ds(ni, FFT_NO, stride=FFT_NI), :]
        _store_stage1(w_ref, ni, jnp.dot(m1_ref[ni], a.astype(BF16), preferred_element_type=F32))
        return carry

    lax.fori_loop(0, FFT_NI, stage1, 0, unroll=FFT_UNROLL)
    blk = 2 * FFT_NI

    cb = w_ref.shape[1]

    def stage2(kp, carry):
        rows = [pl.ds(pl.multiple_of((2 * kp + j) * blk, blk), blk) for j in range(2)]
        x = jnp.dot(m2_ref[...], jnp.concatenate([w_ref[r, :] for r in rows], axis=1).astype(BF16),
                    preferred_element_type=F32)
        for j in range(2):
            o_ref[0, 0, rows[j], :] = x[:, j * cb:(j + 1) * cb]
        return carry

    lax.fori_loop(0, FFT_NO // 2, stage2, 0, unroll=FFT_MID_UNROLL)


def _spec_two_stage(h, m1_real, m2):
    cb = LCONV_CB
    return pl.pallas_call(
        _spec_two_stage_kernel,
        grid=(DEPTH, 2, BRANCH_W // cb),
        in_specs=[pl.BlockSpec((1, 1, FFT_N, cb), lambda l, o, c: (l, o, 0, c)),
                  pl.BlockSpec((FFT_NI, 2 * FFT_NO, FFT_NO), lambda l, o, c: (0, 0, 0)),
                  pl.BlockSpec((2 * FFT_NI, 2 * FFT_NI), lambda l, o, c: (0, 0))],
        out_specs=pl.BlockSpec((1, 1, 2 * FFT_N, cb), lambda l, o, c: (l, o, 0, c)),
        out_shape=jax.ShapeDtypeStruct((DEPTH, 2, 2 * FFT_N, BRANCH_W), F32),
        scratch_shapes=[pltpu.VMEM((2 * FFT_N, cb), F32)],
        compiler_params=_cparams("arbitrary", "arbitrary", "arbitrary"),
        name="hyena_spectrum_two_stage",
    )(h, m1_real, m2)


def _lconv_two_stage_kernel(s_ref, g_ref, cws_ref, cbs_ref, cwg_ref, cbg_ref, h_ref, bias_ref,
                            m1_ref, m2_ref, m2c_ref, m3_ref, o_ref, z_ref, w_ref, *, conv_sig):
    for b in range(2):
        sig = s_ref[b].astype(F32)
        if conv_sig:
            sig = _short_conv(sig, cws_ref, cbs_ref, DEC_SEQ)
        z_ref[b] = sig
    _fwd_stage1(z_ref.at[0], z_ref.at[1], m1_ref, w_ref)
    blk = 2 * FFT_NI

    cb = w_ref.shape[1]

    def mid(kp, carry):
        rows = [pl.ds(pl.multiple_of((2 * kp + j) * blk, blk), blk) for j in range(2)]
        x = jnp.dot(m2_ref[...], jnp.concatenate([w_ref[r, :] for r in rows], axis=1).astype(BF16),
                    preferred_element_type=F32)
        h = jnp.concatenate([h_ref[r, :] for r in rows], axis=1)
        xr, xi = x[0:FFT_NI], x[FFT_NI:blk]
        hr, hi = h[0:FFT_NI], h[FFT_NI:blk]
        y = jnp.concatenate([xr * hr - xi * hi, xr * hi + xi * hr], axis=0)
        c = jnp.dot(m2c_ref[...], y.astype(BF16), preferred_element_type=F32)
        for j in range(2):
            w_ref[rows[j], :] = c[:, j * cb:(j + 1) * cb]
        return carry

    lax.fori_loop(0, FFT_NO // 2, mid, 0, unroll=FFT_MID_UNROLL)

    def last(ni, carry):
        cr = w_ref[pl.ds(ni, FFT_NO, stride=blk), :]
        ci = w_ref[pl.ds(FFT_NI + ni, FFT_NO, stride=blk), :]
        y = jnp.dot(m3_ref[ni], jnp.concatenate([cr, ci], axis=0).astype(BF16), preferred_element_type=F32)
        o_ref[0, pl.ds(ni, FFT_HALF, stride=FFT_NI), :] = y[0:FFT_HALF]
        o_ref[1, pl.ds(ni, FFT_HALF, stride=FFT_NI), :] = y[FFT_HALF:2 * FFT_HALF]
        return carry

    lax.fori_loop(0, FFT_NI, last, 0, unroll=FFT_UNROLL)
    for b in range(2):
        gate = _short_conv(g_ref[b].astype(F32), cwg_ref, cbg_ref, DEC_SEQ)
        sig = z_ref[b]
        o_ref[b] = gate * (o_ref[b] + sig * bias_ref[...])


def _lconv_two_stage(sig, sig_col, gate_src, gate_col, conv_w, conv_b, spec, l, order, bias, mats, conv_sig):
    cb = LCONV_CB
    ncb = BRANCH_W // cb
    m1, m2, m2c, m3 = mats
    const3 = lambda c, p: (0, 0, 0)
    const2 = lambda c, p: (0, 0)
    return pl.pallas_call(
        functools.partial(_lconv_two_stage_kernel, conv_sig=conv_sig),
        grid=(ncb, DEC_BATCH // 2),
        in_specs=[
            pl.BlockSpec((2, DEC_SEQ, cb), lambda c, p: (p, 0, sig_col * ncb + c)),
            pl.BlockSpec((2, DEC_SEQ, cb), lambda c, p: (p, 0, gate_col * ncb + c)),
            pl.BlockSpec((3, cb), lambda c, p: (0, c)),
            pl.BlockSpec((1, cb), lambda c, p: (0, c)),
            pl.BlockSpec((3, cb), lambda c, p: (0, gate_col * ncb + c)),
            pl.BlockSpec((1, cb), lambda c, p: (0, gate_col * ncb + c)),
            pl.BlockSpec((None, None, 2 * FFT_N, cb), lambda c, p: (l, order, 0, c)),
            pl.BlockSpec((1, cb), lambda c, p: (0, c)),
            pl.BlockSpec(m1.shape, const3),
            pl.BlockSpec(m2.shape, const2),
            pl.BlockSpec(m2c.shape, const2),
            pl.BlockSpec(m3.shape, const3),
        ],
        out_specs=pl.BlockSpec((2, DEC_SEQ, cb), lambda c, p: (p, 0, c)),
        out_shape=jax.ShapeDtypeStruct((DEC_BATCH, DEC_SEQ, BRANCH_W), F32),
        scratch_shapes=[pltpu.VMEM((2, DEC_SEQ, cb), F32), pltpu.VMEM((2 * FFT_N, cb), F32)],
        compiler_params=_cparams("arbitrary", "arbitrary"),
        name="hyena_lconv_two_stage",
    )(sig, gate_src, conv_w, conv_b, conv_w, conv_b, spec, bias, m1, m2, m2c, m3)


def kernel(x_prompt, x_sample, cache_na_k, cache_na_v, cache_da_k, cache_da_v, c, c_ctx, w_ada, b_ada, norm_mix,
           norm_ffn, w_in, hy_conv_w, hy_conv_b, hy_filt_w1, hy_filt_b1, hy_filt_w2, hy_filt_b2, hy_filt_w3,
           hy_filt_freq, hy_bias, na_rpb, da_lambda, da_subln, w_lift, w_out, w_ffn_in, w_ffn_out, norm_final):
    TP, TS = BATCH * SEQ, DEC_BATCH * DEC_SEQ
    xp = x_prompt.reshape(TP, D_MODEL)
    xs = x_sample.reshape(TS, D_MODEL)

    cc = jnp.concatenate([c_ctx[None, :], c, jnp.zeros((8 - 1 - DEC_BATCH, D_MODEL), F32)], axis=0)
    mod = _modulation(cc, w_ada, b_ada)
    mod_p = mod[:, 0:1].reshape(DEPTH, 1, 1, 6 * D_MODEL)
    mod_s = mod[:, 1:1 + DEC_BATCH].reshape(DEPTH, DEC_BATCH, 1, 6 * D_MODEL)

    w_mix = w_in[:, :, :MIX_W].astype(BF16)
    w_gate = w_in[:, :, MIX_W:].astype(BF16)
    w_lift_b = w_lift.astype(BF16)
    w_out_b = w_out.astype(BF16)
    w_ffn_in_b = w_ffn_in.astype(BF16)
    w_ffn_out_b = w_ffn_out.astype(BF16)
    g_mix = norm_mix.reshape(DEPTH, 1, D_MODEL)
    g_ffn = norm_ffn.reshape(DEPTH, 1, D_MODEL)
    g_fin = norm_final.reshape(1, D_MODEL)
    subln = da_subln.reshape(DEPTH, 1, DA_V_DIM)
    subln_col = da_subln.reshape(DEPTH, DA_V_DIM, 1)

    w1p = jnp.pad(hy_filt_w1, ((0, 0), (0, HY_FILT_HIDDEN - HY_POS_DIM), (0, 0)))
    b1 = hy_filt_b1.reshape(DEPTH, 1, HY_FILT_HIDDEN)
    b2 = hy_filt_b2.reshape(DEPTH, 1, HY_FILT_HIDDEN)
    fr = hy_filt_freq.reshape(DEPTH, 1, HY_FILT_HIDDEN)
    mf, mi = _dft_direct_mats()
    mats = _dft_two_stage_mats()
    h_p = _hyena_filters(SEQ, w1p, b1, hy_filt_w2, b2, hy_filt_w3, fr)
    h_s = _hyena_filters(DEC_SEQ, w1p, b1, hy_filt_w2, b2, hy_filt_w3, fr)
    spec_p = _spec_direct(h_p, jnp.asarray(_dft_real_mat()))
    mf_b, mi_b = jnp.asarray(mf, dtype=BF16), jnp.asarray(mi, dtype=BF16)
    mats_b = tuple(jnp.asarray(m, dtype=BF16) for m in mats)
    spec_s = _spec_two_stage(h_s, jnp.asarray(_dft_stage1_real_mat(), dtype=BF16), mats_b[1])
    conv_b = hy_conv_b.reshape(DEPTH, 1, 3 * BRANCH_W)

    na_bias = _na_bias_table(na_rpb)
    rope_tables = _rope_tables()
    ck_na = cache_na_k.reshape(DEC_BATCH, DEPTH, PAST_LEN, BRANCH_W)
    cv_na = cache_na_v.reshape(DEC_BATCH, DEPTH, PAST_LEN, BRANCH_W)
    ck_da = cache_da_k.reshape(DEC_BATCH, DEPTH, PAST_LEN, BRANCH_W)
    cv_da = cache_da_v.reshape(DEC_BATCH, DEPTH, PAST_LEN, BRANCH_W)

    caches = tuple(jnp.zeros((BATCH, DEPTH, SEQ, BRANCH_W), F32) for _ in CACHE_BLOCKS)
    for l in range(DEPTH):
        lam_init = 0.8 - 0.6 * math.exp(-0.3 * l)
        final = l == DEPTH - 1

        u, caches = _in_proj(xp, g_mix[l], mod_p[l], w_mix, l, TP, BF16, caches=caches)
        z1 = _lconv_direct(u, 0, u, 1, hy_conv_w[l], conv_b[l], spec_p, l, 0, hy_bias[l, 0:1], mf_b, mi_b, True)
        y_hy = _lconv_direct(z1, 0, u, 2, hy_conv_w[l], conv_b[l], spec_p, l, 1, hy_bias[l, 1:2], mf_b, mi_b, False)
        y_na, y_da = _ctx_attention(u, da_lambda[l], subln_col[l], lam_init)
        xp = _merge_out(xp, g_mix[l], mod_p[l], y_hy, y_na, y_da, w_gate, w_lift_b, w_out_b, l, TP)
        xp = _ffn(xp, g_ffn[l], mod_p[l], w_ffn_in_b, w_ffn_out_b, g_fin, l, TP, final)

        u = _in_proj(xs, g_mix[l], mod_s[l], w_mix, l, DEC_SEQ, BF16)
        u3 = u.reshape(DEC_BATCH, DEC_SEQ, MIX_W)
        z1 = _lconv_two_stage(u3, 0, u3, 1, hy_conv_w[l], conv_b[l], spec_s, l, 0, hy_bias[l, 0:1], mats_b, True)
        y_hy = _lconv_two_stage(z1, 0, u3, 2, hy_conv_w[l], conv_b[l], spec_s, l, 1, hy_bias[l, 1:2], mats_b, False)
        y_hy = y_hy.reshape(TS, BRANCH_W)
        qn, kn, vn = _attn_prep(u, 3, ck_na[:, l], cv_na[:, l], NA_HEAD_DIM)
        y_na = _nbr_attention(qn, kn, vn, na_bias, l)
        q, kt, v = _attn_prep(u, 6, ck_da[:, l], cv_da[:, l], DA_HEAD_DIM, rope_tables)
        y_da = _diff_attention(q, kt, v, da_lambda[l], subln_col[l], lam_init)
        xs = _merge_out(xs, g_mix[l], mod_s[l], y_hy, y_na, y_da, w_gate, w_lift_b, w_out_b, l, DEC_SEQ)
        xs = _ffn(xs, g_ffn[l], mod_s[l], w_ffn_in_b, w_ffn_out_b, g_fin, l, DEC_SEQ, final)

    y_prompt = xp.reshape(BATCH, SEQ, D_MODEL)
    y_sample = xs.reshape(DEC_BATCH, DEC_SEQ, D_MODEL)
    heads = lambda a, d: a.reshape(BATCH, DEPTH, SEQ, BRANCH_W // d, d)
    return (y_prompt, y_sample, heads(caches[0], NA_HEAD_DIM), heads(caches[1], NA_HEAD_DIM),
            heads(caches[2], 2 * DA_HEAD_DIM), heads(caches[3], DA_V_DIM))
```

```python
import functools
import math

import numpy as np
import jax
import jax.numpy as jnp
from jax import lax
from jax.experimental import pallas as pl
from jax.experimental.pallas import tpu as pltpu

F32 = jnp.float32
BF16 = jnp.bfloat16
HIGHEST = lax.Precision.HIGHEST

D_MODEL = 1024
BATCH = 32
SEQ = 256
DEPTH = 4
DEC_BATCH = 4
DEC_SEQ = 4096
PAST_LEN = 256
GRID_W = 64
GRID_H = DEC_SEQ // GRID_W
BRANCH_W = 512
HY_POS_BANDS = 16
HY_POS_DIM = 1 + 2 * HY_POS_BANDS
HY_FILT_HIDDEN = 64
HY_DECAY_TARGET = 1e-2
HY_FAST_DECAY = 0.3
HY_SLOW_DECAY = 1.5
NA_HEADS = 8
NA_HEAD_DIM = 64
NA_WIN_ROWS = 8
NA_WIN_COLS = 16
DA_HEADS = 8
DA_HEAD_DIM = 32
DA_V_DIM = 64
D_FF = 2816
MIX_W = 9 * BRANCH_W
ROPE_BASE = 10000.0
EPS = 1e-6
NEG_INF = -1e30

VMEM_LIMIT_BYTES = 56 * 1024 * 1024
LANES = 128
MXU_DIM = 256

FFT_N = 2 * DEC_SEQ
FFT_NO = 64
FFT_NI = 128
FFT_HALF = FFT_NO // 2
FFT_UNROLL = 8
FFT_MID_UNROLL = 16
LCONV_CB = LANES


def _cparams(*sem):
    return pltpu.CompilerParams(dimension_semantics=sem, vmem_limit_bytes=VMEM_LIMIT_BYTES)


def _sigmoid(x):
    return 1.0 / (1.0 + jnp.exp(-x))


def _rms(x, g):
    return x * lax.rsqrt(jnp.mean(x * x, axis=-1, keepdims=True) + EPS) * g


def _modnorm(x, g, shift, scale):
    return _rms(x, g) * (1.0 + scale) + shift


def _bdot(a, b):
    return jnp.dot(a.astype(BF16), b.astype(BF16), preferred_element_type=F32)


def _mod_kernel(c_ref, w_ref, b_ref, o_ref):
    c = c_ref[...]
    s = c * _sigmoid(c)
    o_ref[0] = jnp.dot(s, w_ref[0], precision=HIGHEST, preferred_element_type=F32) + b_ref[0]


def _modulation(cc, w_ada, b_ada):
    nt = 6
    return pl.pallas_call(
        _mod_kernel,
        grid=(DEPTH, nt),
        in_specs=[
            pl.BlockSpec((8, D_MODEL), lambda l, j: (0, 0)),
            pl.BlockSpec((1, D_MODEL, D_MODEL), lambda l, j: (l, 0, j)),
            pl.BlockSpec((1, 1, D_MODEL), lambda l, j: (l, 0, j)),
        ],
        out_specs=pl.BlockSpec((1, 8, D_MODEL), lambda l, j: (l, 0, j)),
        out_shape=jax.ShapeDtypeStruct((DEPTH, 8, 6 * D_MODEL), F32),
        compiler_params=_cparams("arbitrary", "arbitrary"),
        name="modulation",
    )(cc, w_ada, b_ada.reshape(DEPTH, 1, 6 * D_MODEL))


IN_TM = 1024
IN_TN = 3 * BRANCH_W
CACHE_BLOCKS = (4, 5, 7, 8)


def _in_kernel(*refs, n_alias, cache_tiles):
    x_ref, g_ref, mod_ref, w_ref = refs[:4]
    o_ref = refs[4 + n_alias]
    cache_refs = refs[5 + n_alias:5 + n_alias + len(cache_tiles)]
    h_ref = refs[-1]
    j = pl.program_id(1)

    @pl.when(j == 0)
    def _():
        m = mod_ref[0]
        h = _modnorm(x_ref[...], g_ref[...], m[:, 0:D_MODEL], m[:, D_MODEL:2 * D_MODEL])
        h_ref[...] = h.astype(BF16)

    res = jnp.dot(h_ref[...], w_ref[...], preferred_element_type=F32)
    o_ref[...] = res.astype(o_ref.dtype)
    for (tile, off), c_ref in zip(cache_tiles, cache_refs):
        @pl.when(j == tile)
        def _(c_ref=c_ref, off=off):
            c_ref[...] = res[:, off:off + BRANCH_W].reshape(c_ref.shape)


def _in_proj(x, g, mod, w, l, rows_per_mod, out_dtype, caches=None):
    T = x.shape[0]
    tm, tn = IN_TM, IN_TN
    per = rows_per_mod // tm
    in_specs = [
        pl.BlockSpec((tm, D_MODEL), lambda i, j: (i, 0)),
        pl.BlockSpec((1, D_MODEL), lambda i, j: (0, 0)),
        pl.BlockSpec((1, 1, 6 * D_MODEL), lambda i, j: (i // per, 0, 0)),
        pl.BlockSpec((None, D_MODEL, tn), lambda i, j: (l, 0, j)),
    ]
    out_specs = [pl.BlockSpec((tm, tn), lambda i, j: (i, j))]
    out_shape = [jax.ShapeDtypeStruct((T, MIX_W), out_dtype)]
    args = [x, g, mod, w]
    cache_tiles, aliases = (), {}
    if caches is not None:
        seqs = tm // SEQ
        cache_tiles = tuple(divmod(c * BRANCH_W, tn) for c in CACHE_BLOCKS)
        out_specs += [pl.BlockSpec((seqs, 1, SEQ, BRANCH_W), lambda i, j: (i, l, 0, 0))] * len(CACHE_BLOCKS)
        out_shape += [jax.ShapeDtypeStruct(c.shape, c.dtype) for c in caches]
        in_specs += [pl.BlockSpec(memory_space=pl.ANY)] * len(caches)
        aliases = {4 + n: 1 + n for n in range(len(caches))}
        args += list(caches)
    n_alias = len(args) - 4
    outs = pl.pallas_call(
        functools.partial(_in_kernel, n_alias=n_alias, cache_tiles=cache_tiles),
        grid=(T // tm, MIX_W // tn),
        in_specs=in_specs,
        out_specs=out_specs,
        out_shape=out_shape,
        input_output_aliases=aliases,
        scratch_shapes=[pltpu.VMEM((tm, D_MODEL), BF16)],
        compiler_params=_cparams("arbitrary", "arbitrary"),
        name="in_proj",
    )(*args)
    return outs[0] if caches is None else (outs[0], tuple(outs[1:]))


def _mid_kernel(x_ref, g_ref, mod_ref, yh_ref, yn_ref, yd_ref, wg_ref, wl_ref, wo_ref, o_ref):
    m = mod_ref[0]
    x = x_ref[...]
    h = _modnorm(x, g_ref[...], m[:, 0:D_MODEL], m[:, D_MODEL:2 * D_MODEL]).astype(BF16)
    merged = None
    for br, y_ref in enumerate((yh_ref, yn_ref, yd_ref)):
        gate = _sigmoid(jnp.dot(h, wg_ref[:, br * D_MODEL:(br + 1) * D_MODEL], preferred_element_type=F32))
        lift = jnp.dot(y_ref[...].astype(BF16), wl_ref[br], preferred_element_type=F32)
        t = gate * lift
        merged = t if merged is None else merged + t
    o_ref[...] = x + m[:, 2 * D_MODEL:3 * D_MODEL] * _bdot(merged, wo_ref[...])


def _merge_out(x, g, mod, y_hy, y_na, y_da, w_gate, w_lift, w_out, l, rows_per_mod):
    T = x.shape[0]
    tm = 512
    per = rows_per_mod // tm
    row = lambda i: (i, 0)
    const2 = lambda i: (0, 0)
    return pl.pallas_call(
        _mid_kernel,
        grid=(T // tm,),
        in_specs=[
            pl.BlockSpec((tm, D_MODEL), row),
            pl.BlockSpec((1, D_MODEL), const2),
            pl.BlockSpec((1, 1, 6 * D_MODEL), lambda i: (i // per, 0, 0)),
            pl.BlockSpec((tm, BRANCH_W), row),
            pl.BlockSpec((tm, BRANCH_W), row),
            pl.BlockSpec((tm, BRANCH_W), row),
            pl.BlockSpec((None, D_MODEL, 3 * D_MODEL), lambda i: (l, 0, 0)),
            pl.BlockSpec((None, 3, BRANCH_W, D_MODEL), lambda i: (l, 0, 0, 0)),
            pl.BlockSpec((None, D_MODEL, D_MODEL), lambda i: (l, 0, 0)),
        ],
        out_specs=pl.BlockSpec((tm, D_MODEL), row),
        out_shape=jax.ShapeDtypeStruct((T, D_MODEL), F32),
        compiler_params=_cparams("arbitrary"),
        name="merge_out",
    )(x, g, mod, y_hy, y_na, y_da, w_gate, w_lift, w_out)


def _ffn_kernel(x_ref, g_ref, mod_ref, w1_ref, w2_ref, gf_ref, o_ref, *, final):
    m = mod_ref[0]
    x = x_ref[...]
    h = _modnorm(x, g_ref[...], m[:, 3 * D_MODEL:4 * D_MODEL], m[:, 4 * D_MODEL:5 * D_MODEL]).astype(BF16)
    a = jnp.dot(h, w1_ref[:, 0:D_FF], preferred_element_type=F32)
    b = jnp.dot(h, w1_ref[:, D_FF:2 * D_FF], preferred_element_type=F32)
    xn = x + m[:, 5 * D_MODEL:6 * D_MODEL] * _bdot(a * _sigmoid(a) * b, w2_ref[...])
    if final:
        xn = _rms(xn, gf_ref[...])
    o_ref[...] = xn


def _ffn(x, g, mod, w_ffn_in, w_ffn_out, g_final, l, rows_per_mod, final):
    T = x.shape[0]
    tm = 512
    per = rows_per_mod // tm
    resident = pl.Buffered(1)
    return pl.pallas_call(
        functools.partial(_ffn_kernel, final=final),
        grid=(T // tm,),
        in_specs=[
            pl.BlockSpec((tm, D_MODEL), lambda i: (i, 0)),
            pl.BlockSpec((1, D_MODEL), lambda i: (0, 0)),
            pl.BlockSpec((1, 1, 6 * D_MODEL), lambda i: (i // per, 0, 0)),
            pl.BlockSpec((None, D_MODEL, 2 * D_FF), lambda i: (l, 0, 0), pipeline_mode=resident),
            pl.BlockSpec((None, D_FF, D_MODEL), lambda i: (l, 0, 0), pipeline_mode=resident),
            pl.BlockSpec((1, D_MODEL), lambda i: (0, 0)),
        ],
        out_specs=pl.BlockSpec((tm, D_MODEL), lambda i: (i, 0)),
        out_shape=jax.ShapeDtypeStruct((T, D_MODEL), F32),
        compiler_params=_cparams("arbitrary"),
        name="ffn",
    )(x, g, mod, w_ffn_in, w_ffn_out, g_final)


def _da_lambda(lam_ref, lam_init):
    lp = lam_ref[...]
    a = jnp.sum(lp[0:1] * lp[1:2], axis=1, keepdims=True)
    b = jnp.sum(lp[2:3] * lp[3:4], axis=1, keepdims=True)
    return jnp.exp(a) - jnp.exp(b) + lam_init


ATT_ONES_ROWS = 16
ATT_TQ = 256
ATT_KEYS = DEC_SEQ + PAST_LEN
LOG2E = math.log2(math.e)


def _masked_q_blocks(qt, d):
    row = lax.broadcasted_iota(jnp.int32, qt.shape, 0)
    zero = jnp.zeros_like(qt)
    return jnp.concatenate([jnp.where((row >= j * d) & (row < (j + 1) * d), qt, zero) for j in range(LANES // d)],
                           axis=1)


def _colmax(st):
    keys, n = st.shape
    return jnp.max(jnp.max(st.reshape(keys // MXU_DIM, MXU_DIM, n), axis=0), axis=0, keepdims=True)


def _ctx_attn_kernel(nq_ref, nk_ref, nv_ref, dq_ref, dk_ref, dv_ref, lam_ref, sub_ref, yn_ref, yd_ref, acc_ref,
                     *, lam_init):
    lam = _da_lambda(lam_ref, lam_init)
    ones = jnp.ones((ATT_ONES_ROWS, SEQ), BF16)

    def attend(q_ref, k_ref, v_ref, d, maps_per_head, finish):
        qt = (q_ref[...].astype(F32) * (d ** -0.5 * LOG2E)).T.astype(BF16)
        vt = v_ref[...].astype(F32).T.astype(BF16)
        kb = k_ref[...].astype(BF16)
        dv = NA_HEAD_DIM
        heads_per_group = LANES // (d * maps_per_head)
        w = maps_per_head * SEQ
        for g in range(BRANCH_W // LANES):
            lanes = slice(g * LANES, (g + 1) * LANES)
            st = jnp.dot(kb[:, lanes], _masked_q_blocks(qt[lanes], d), preferred_element_type=F32)
            pt = jnp.exp2(st - _colmax(st)).astype(BF16)
            for j in range(heads_per_group):
                h = g * heads_per_group + j
                ve = jnp.concatenate([vt[h * dv:(h + 1) * dv], ones], axis=0)
                oe = jnp.dot(ve, pt[:, j * w:(j + 1) * w], preferred_element_type=F32)
                os = [oe[0:dv, i * SEQ:(i + 1) * SEQ] / oe[dv:dv + 1, i * SEQ:(i + 1) * SEQ]
                      for i in range(maps_per_head)]
                acc_ref[h * dv:(h + 1) * dv, :] = finish(os)

    attend(nq_ref, nk_ref, nv_ref, NA_HEAD_DIM, 1, lambda os: os[0])
    yn_ref[...] = acc_ref[...].T.astype(yn_ref.dtype)

    def da_finish(os):
        ot = os[0] - lam * os[1]
        ot = ot * lax.rsqrt(jnp.mean(ot * ot, axis=0, keepdims=True) + EPS) * sub_ref[...]
        return ot * (1.0 - lam_init)

    attend(dq_ref, dk_ref, dv_ref, DA_HEAD_DIM, 2, da_finish)
    yd_ref[...] = acc_ref[...].T.astype(yd_ref.dtype)


def _ctx_attention(u, da_lambda, subln_col, lam_init):
    col = lambda j: pl.BlockSpec((SEQ, BRANCH_W), lambda b, j=j: (b, j))
    out = pl.BlockSpec((SEQ, BRANCH_W), lambda b: (b, 0))
    shape = jax.ShapeDtypeStruct((BATCH * SEQ, BRANCH_W), BF16)
    return pl.pallas_call(
        functools.partial(_ctx_attn_kernel, lam_init=lam_init),
        grid=(BATCH,),
        in_specs=[col(3), col(4), col(5), col(6), col(7), col(8),
                  pl.BlockSpec((4, DA_HEAD_DIM), lambda b: (0, 0)),
                  pl.BlockSpec((DA_V_DIM, 1), lambda b: (0, 0))],
        out_specs=[out, out],
        out_shape=[shape, shape],
        scratch_shapes=[pltpu.VMEM((BRANCH_W, SEQ), F32)],
        compiler_params=_cparams("arbitrary"),
        name="ctx_attention",
    )(u, u, u, u, u, u, da_lambda, subln_col)


def _rope(x, cos, sin_signed):
    n = x.shape[-1]
    lane = lax.broadcasted_iota(jnp.int32, x.shape, 1)
    partner = jnp.where(lane % 2 == 0, pltpu.roll(x, n - 1, axis=1), pltpu.roll(x, 1, axis=1))
    return x * cos + partner * sin_signed


def _attn_prep_kernel(q_ref, k_ref, v_ref, kc_ref, vc_ref, *refs, rope, scale):
    cos_ref, sin_ref = refs[:2] if rope else (None, None)
    qt_ref, ko_ref, vt_ref = refs[-3:]
    t = pl.program_id(1)
    dv = NA_HEAD_DIM

    def put_v(v):
        vt = v.astype(F32).T.astype(BF16)
        ones = jnp.ones((ATT_ONES_ROWS, ATT_TQ), BF16)
        for h in range(BRANCH_W // dv):
            vt_ref[0, h, 0:dv, :] = vt[h * dv:(h + 1) * dv]
            vt_ref[0, h, dv:dv + ATT_ONES_ROWS, :] = ones

    @pl.when(t < DEC_SEQ // ATT_TQ)
    def _():
        q = q_ref[...].astype(F32)
        k = k_ref[...].astype(F32)
        if rope:
            q = _rope(q, cos_ref[...], sin_ref[...])
            k = _rope(k, cos_ref[...], sin_ref[...])
        qt_ref[0] = (q * scale).T.astype(BF16)
        ko_ref[0] = k.astype(BF16)
        put_v(v_ref[...])

    @pl.when(t == DEC_SEQ // ATT_TQ)
    def _():
        ko_ref[0] = kc_ref[0].astype(BF16)
        put_v(vc_ref[0])


def _attn_prep(u, first_col, k_ctx, v_ctx, head_dim, rope_tables=None):
    rope = rope_tables is not None
    nt = DEC_SEQ // ATT_TQ
    last = nt - 1
    rowblk = lambda j: pl.BlockSpec((ATT_TQ, BRANCH_W), lambda b, t, j=j: (b * nt + jnp.minimum(t, last), j))
    tab = pl.BlockSpec((ATT_TQ, BRANCH_W), lambda b, t: (jnp.minimum(t, last), 0))
    ctx = pl.BlockSpec((1, PAST_LEN, BRANCH_W), lambda b, t: (b, 0, 0))
    heads = BRANCH_W // NA_HEAD_DIM
    vrows = NA_HEAD_DIM + ATT_ONES_ROWS
    return pl.pallas_call(
        functools.partial(_attn_prep_kernel, rope=rope, scale=head_dim ** -0.5 * LOG2E),
        grid=(DEC_BATCH, nt + 1),
        in_specs=[rowblk(first_col), rowblk(first_col + 1), rowblk(first_col + 2), ctx, ctx] + [tab, tab] * rope,
        out_specs=[
            pl.BlockSpec((1, BRANCH_W, ATT_TQ), lambda b, t: (b, 0, jnp.minimum(t, last))),
            pl.BlockSpec((1, ATT_TQ, BRANCH_W), lambda b, t: (b, t, 0)),
            pl.BlockSpec((1, heads, vrows, ATT_TQ), lambda b, t: (b, 0, 0, t)),
        ],
        out_shape=[
            jax.ShapeDtypeStruct((DEC_BATCH, BRANCH_W, DEC_SEQ), BF16),
            jax.ShapeDtypeStruct((DEC_BATCH, ATT_KEYS, BRANCH_W), BF16),
            jax.ShapeDtypeStruct((DEC_BATCH, heads, vrows, ATT_KEYS), BF16),
        ],
        compiler_params=_cparams("arbitrary", "arbitrary"),
        name="attn_prep",
    )(u, u, u, k_ctx, v_ctx, *(rope_tables or ()))


NA_ROWS = ATT_TQ // GRID_W
NA_UNION = 3 * NA_ROWS
NA_STEPS = GRID_H // NA_ROWS
NA_SLABS = NA_UNION // NA_ROWS
NA_VARIANT_OFFSET = (0, -NA_ROWS, -2 * NA_ROWS)


def _na_variant(s):
    return jnp.minimum(s, 1) + s // (NA_STEPS - 1)


def _na_window_block(s):
    return jnp.clip(s - 1, 0, NA_STEPS - NA_SLABS)


def _na_bias_kernel(rpb_ref, o_ref):
    kc = lax.broadcasted_iota(jnp.int32, (GRID_W, GRID_W), 0)
    qc = lax.broadcasted_iota(jnp.int32, (GRID_W, GRID_W), 1)
    dc = jnp.clip(kc - qc, -(NA_WIN_COLS - 1), NA_WIN_COLS - 1) + (NA_WIN_COLS - 1)
    c0 = jnp.clip(qc - NA_WIN_COLS // 2, 0, GRID_W - NA_WIN_COLS)
    col_ok = (kc >= c0) & (kc < c0 + NA_WIN_COLS)
    r = rpb_ref[0, 0] * LOG2E
    masked = jnp.full((GRID_W, GRID_W), NEG_INF, F32)
    tiles = []
    for dr in range(2 * NA_WIN_ROWS - 1):
        acc = jnp.zeros((GRID_W, GRID_W), F32)
        for d in range(2 * NA_WIN_COLS - 1):
            acc = jnp.where(dc == d, r[dr:dr + 1, d:d + 1], acc)
        tiles.append(jnp.where(col_ok, acc, masked))
    for v, off in enumerate(NA_VARIANT_OFFSET):
        for kr in range(NA_UNION):
            for rr in range(NA_ROWS):
                w0 = (0, rr, NA_UNION - NA_WIN_ROWS)[v]
                dr = kr + off - rr
                inside = w0 <= kr < w0 + NA_WIN_ROWS
                o_ref[0, v, 0, kr * GRID_W:(kr + 1) * GRID_W, rr * GRID_W:(rr + 1) * GRID_W] = (
                    tiles[dr + NA_WIN_ROWS - 1] if inside else masked)


def _na_bias_table(na_rpb):
    n_dr, n_dc = 2 * NA_WIN_ROWS - 1, 2 * NA_WIN_COLS - 1
    nv = len(NA_VARIANT_OFFSET)
    return pl.pallas_call(
        _na_bias_kernel,
        grid=(DEPTH, NA_HEADS),
        in_specs=[pl.BlockSpec((1, 1, n_dr, n_dc), lambda l, h: (l, h, 0, 0))],
        out_specs=pl.BlockSpec((1, nv, 1, NA_UNION * GRID_W, ATT_TQ), lambda l, h: (l, 0, h, 0, 0)),
        out_shape=jax.ShapeDtypeStruct((DEPTH, nv, NA_HEADS, NA_UNION * GRID_W, ATT_TQ), F32),
        compiler_params=_cparams("arbitrary", "arbitrary"),
        name="na_bias_table",
    )(na_rpb)


def _na_kernel(qt_ref, *refs):
    n = NA_SLABS + 1
    k_refs, vt_refs = refs[:n], refs[n:2 * n]
    bias_ref, o_ref, acc_ref = refs[2 * n:]
    dv = NA_HEAD_DIM
    heads_per_group = LANES // dv
    for g in range(BRANCH_W // LANES):
        lanes = slice(g * LANES, (g + 1) * LANES)
        qbd = _masked_q_blocks(qt_ref[0, lanes, :], dv)
        sts = []
        for j, k_ref in enumerate(k_refs):
            st = jnp.dot(k_ref[0, :, lanes], qbd, preferred_element_type=F32)
            if j < NA_SLABS:
                rows = slice(j * ATT_TQ, (j + 1) * ATT_TQ)
                st = st + jnp.concatenate(
                    [bias_ref[0, g * heads_per_group + hh, rows, :] for hh in range(heads_per_group)], axis=1)
            sts.append(st)
        mx = functools.reduce(jnp.maximum, [_colmax(st) for st in sts])
        pts = [jnp.exp2(st - mx).astype(BF16) for st in sts]
        for hh in range(heads_per_group):
            h = g * heads_per_group + hh
            oe = sum(jnp.dot(vt_ref[0, h], pt[:, hh * ATT_TQ:(hh + 1) * ATT_TQ], preferred_element_type=F32)
                     for vt_ref, pt in zip(vt_refs, pts))
            acc_ref[h * dv:(h + 1) * dv, :] = oe[0:dv] / oe[dv:dv + 1]
    o_ref[...] = acc_ref[...].T.astype(o_ref.dtype)


def _nbr_attention(qt, k, vt, bias, l):
    vrows = NA_HEAD_DIM + ATT_ONES_ROWS
    ctx_blk = DEC_SEQ // ATT_TQ
    k_specs = [pl.BlockSpec((1, ATT_TQ, BRANCH_W), lambda b, s, j=j: (b, _na_window_block(s) + j, 0))
               for j in range(NA_SLABS)]
    k_specs.append(pl.BlockSpec((1, ATT_TQ, BRANCH_W), lambda b, s: (b, ctx_blk, 0)))
    vt_specs = [pl.BlockSpec((1, NA_HEADS, vrows, ATT_TQ), lambda b, s, j=j: (b, 0, 0, _na_window_block(s) + j))
                for j in range(NA_SLABS)]
    vt_specs.append(pl.BlockSpec((1, NA_HEADS, vrows, ATT_TQ), lambda b, s: (b, 0, 0, ctx_blk)))
    n = NA_SLABS + 1
    return pl.pallas_call(
        _na_kernel,
        grid=(DEC_BATCH, NA_STEPS),
        in_specs=[pl.BlockSpec((1, BRANCH_W, ATT_TQ), lambda b, s: (b, 0, s))] + k_specs + vt_specs + [
            pl.BlockSpec((None, 1, NA_HEADS, NA_UNION * GRID_W, ATT_TQ), lambda b, s: (l, _na_variant(s), 0, 0, 0))],
        out_specs=pl.BlockSpec((ATT_TQ, BRANCH_W), lambda b, s: (b * NA_STEPS + s, 0)),
        out_shape=jax.ShapeDtypeStruct((DEC_BATCH * DEC_SEQ, BRANCH_W), BF16),
        scratch_shapes=[pltpu.VMEM((BRANCH_W, ATT_TQ), F32)],
        compiler_params=_cparams("arbitrary", "arbitrary"),
        name="nbr_attention",
    )(qt, *([k] * n), *([vt] * n), bias)


DA_TQ = ATT_TQ
DA_KEYS = ATT_KEYS
DA_ONES_ROWS = ATT_ONES_ROWS
DA_MAPS_PER_TILE = LANES // DA_HEAD_DIM


def _da_kernel(qt_ref, k_ref, vt_ref, lam_ref, sub_ref, o_ref, acc_ref, *, lam_init):
    lam = _da_lambda(lam_ref, lam_init)
    heads = DA_MAPS_PER_TILE // 2
    for g in range(BRANCH_W // LANES):
        lanes = slice(g * LANES, (g + 1) * LANES)
        st = jnp.dot(k_ref[0, :, lanes], _masked_q_blocks(qt_ref[0, lanes, :], DA_HEAD_DIM),
                     preferred_element_type=F32)
        pt = jnp.exp2(st - _colmax(st)).astype(BF16)
        for hh in range(heads):
            h = g * heads + hh
            oe = jnp.dot(vt_ref[0, h], pt[:, 2 * hh * DA_TQ:(2 * hh + 2) * DA_TQ], preferred_element_type=F32)
            os = [oe[0:DA_V_DIM, i * DA_TQ:(i + 1) * DA_TQ] / oe[DA_V_DIM:DA_V_DIM + 1, i * DA_TQ:(i + 1) * DA_TQ]
                  for i in range(2)]
            ot = os[0] - lam * os[1]
            ot = ot * lax.rsqrt(jnp.mean(ot * ot, axis=0, keepdims=True) + EPS) * sub_ref[...]
            acc_ref[h * DA_V_DIM:(h + 1) * DA_V_DIM, :] = ot * (1.0 - lam_init)
    o_ref[...] = acc_ref[...].T.astype(o_ref.dtype)


def _diff_attention(qt, k, vt, da_lambda, subln_col, lam_init):
    nt = DEC_SEQ // DA_TQ
    vrows = DA_V_DIM + DA_ONES_ROWS
    return pl.pallas_call(
        functools.partial(_da_kernel, lam_init=lam_init),
        grid=(DEC_BATCH, nt),
        in_specs=[
            pl.BlockSpec((1, BRANCH_W, DA_TQ), lambda b, t: (b, 0, t)),
            pl.BlockSpec((1, DA_KEYS, BRANCH_W), lambda b, t: (b, 0, 0)),
            pl.BlockSpec((1, DA_HEADS, vrows, DA_KEYS), lambda b, t: (b, 0, 0, 0)),
            pl.BlockSpec((4, DA_HEAD_DIM), lambda b, t: (0, 0)),
            pl.BlockSpec((DA_V_DIM, 1), lambda b, t: (0, 0)),
        ],
        out_specs=pl.BlockSpec((DA_TQ, BRANCH_W), lambda b, t: (b * nt + t, 0)),
        out_shape=jax.ShapeDtypeStruct((DEC_BATCH * DEC_SEQ, BRANCH_W), BF16),
        scratch_shapes=[pltpu.VMEM((BRANCH_W, DA_TQ), F32)],
        compiler_params=_cparams("arbitrary", "arbitrary"),
        name="diff_attention",
    )(qt, k, vt, da_lambda, subln_col)


def _rope_tables():
    pos = np.arange(DEC_SEQ)
    row = (pos // GRID_W).astype(np.float32)
    col = (pos % GRID_W).astype(np.float32)
    n_freq = DA_HEAD_DIM // 4
    inv = (np.float32(ROPE_BASE) ** (-np.arange(n_freq, dtype=np.float32) / n_freq)).astype(np.float32)
    ang = np.concatenate([row[:, None] * inv[None, :], col[:, None] * inv[None, :]], axis=-1)
    ang = ang.astype(np.float64)
    cos = np.repeat(np.cos(ang), 2, axis=-1)
    sin = np.repeat(np.sin(ang), 2, axis=-1)
    sign = np.where(np.arange(DA_HEAD_DIM) % 2 == 0, -1.0, 1.0)
    reps = BRANCH_W // DA_HEAD_DIM
    cos = np.tile(cos, (1, reps)).astype(np.float32)
    sin = np.tile(sin * sign[None, :], (1, reps)).astype(np.float32)
    return jnp.asarray(cos), jnp.asarray(sin)


def _filt_hidden_kernel(feat_ref, w1_ref, b1_ref, w2_ref, b2_ref, fr_ref, o_ref):
    fr = fr_ref[0]
    h = jnp.sin(fr * (jnp.dot(feat_ref[...], w1_ref[0], precision=HIGHEST, preferred_element_type=F32) + b1_ref[0]))
    o_ref[0] = jnp.sin(fr * (jnp.dot(h, w2_ref[0], precision=HIGHEST, preferred_element_type=F32) + b2_ref[0]))


def _filt_kernel(h_ref, w3f_ref, w3b_ref, dec_ref, o_ref):
    L = dec_ref.shape[0] // 2
    hf = jnp.dot(h_ref[0, 0:L], w3f_ref[0], precision=HIGHEST, preferred_element_type=F32) * dec_ref[0:L]
    hb = jnp.dot(h_ref[0, L:2 * L], w3b_ref[0], precision=HIGHEST, preferred_element_type=F32) * dec_ref[L:2 * L]
    row = lax.broadcasted_iota(jnp.int32, hb.shape, 0)
    hb = jnp.where(row == 0, 0.0, hb)
    nrm = jnp.sum(jnp.abs(hf), axis=0, keepdims=True) + jnp.sum(jnp.abs(hb), axis=0, keepdims=True)
    o_ref[0, 0, 0:L] = hf / nrm
    o_ref[0, 0, L:2 * L] = hb / nrm


def _circular_order(a):
    return np.concatenate([a, a[:1], a[1:][::-1]], axis=0)


def _hyena_pos_tables(L):
    f32 = np.float32
    pos = np.arange(L, dtype=f32)
    t = (pos / f32(L)).astype(f32)
    bands = np.linspace(1e-4, HY_POS_BANDS - 1, HY_POS_BANDS, dtype=f32)
    ang = (f32(2 * math.pi / L) * pos[:, None] * bands[None, :]).astype(np.float64)
    feats = np.zeros((L, HY_FILT_HIDDEN), f32)
    feats[:, 0] = t
    feats[:, 1:1 + HY_POS_BANDS] = np.cos(ang)
    feats[:, 1 + HY_POS_BANDS:HY_POS_DIM] = -np.sin(ang)
    deltas = np.linspace(math.log(HY_DECAY_TARGET) / HY_SLOW_DECAY,
                         math.log(HY_DECAY_TARGET) / HY_FAST_DECAY, BRANCH_W, dtype=f32)
    decay = np.exp((-t[:, None] * np.abs(deltas)[None, :]).astype(np.float64)).astype(f32)
    return jnp.asarray(_circular_order(feats)), jnp.asarray(_circular_order(decay))


def _hyena_filters(half, w1p, b1, w2, b2, w3, freq):
    feats, decay = _hyena_pos_tables(half)
    L = 2 * half
    cb = LANES
    ncb = BRANCH_W // cb
    small = lambda shape: pl.BlockSpec((1,) + shape, lambda l: (l, 0, 0))
    hidden = pl.pallas_call(
        _filt_hidden_kernel,
        grid=(DEPTH,),
        in_specs=[
            pl.BlockSpec((L, HY_FILT_HIDDEN), lambda l: (0, 0)),
            small((HY_FILT_HIDDEN, HY_FILT_HIDDEN)), small((1, HY_FILT_HIDDEN)),
            small((HY_FILT_HIDDEN, HY_FILT_HIDDEN)), small((1, HY_FILT_HIDDEN)),
            small((1, HY_FILT_HIDDEN)),
        ],
        out_specs=pl.BlockSpec((1, L, HY_FILT_HIDDEN), lambda l: (l, 0, 0)),
        out_shape=jax.ShapeDtypeStruct((DEPTH, L, HY_FILT_HIDDEN), F32),
        compiler_params=_cparams("arbitrary"),
        name=f"hyena_filter_hidden_{L}",
    )(feats, w1p, b1, w2, b2, freq)
    return pl.pallas_call(
        _filt_kernel,
        grid=(DEPTH, 2, ncb),
        in_specs=[
            pl.BlockSpec((1, L, HY_FILT_HIDDEN), lambda l, o, c: (l, 0, 0)),
            pl.BlockSpec((1, HY_FILT_HIDDEN, cb), lambda l, o, c: (l, 0, o * 2 * ncb + c)),
            pl.BlockSpec((1, HY_FILT_HIDDEN, cb), lambda l, o, c: (l, 0, o * 2 * ncb + ncb + c)),
            pl.BlockSpec((L, cb), lambda l, o, c: (0, c)),
        ],
        out_specs=pl.BlockSpec((1, 1, L, cb), lambda l, o, c: (l, o, 0, c)),
        out_shape=jax.ShapeDtypeStruct((DEPTH, 2, L, BRANCH_W), F32),
        compiler_params=_cparams("arbitrary", "arbitrary", "arbitrary"),
        name=f"hyena_filters_{L}",
    )(hidden, w3, w3, decay)


def _short_conv(u, w_ref, b_ref, seq_len):
    n = u.shape[0]
    t = lax.broadcasted_iota(jnp.int32, u.shape, 0) % seq_len
    prev = jnp.where(t == 0, 0.0, pltpu.roll(u, 1, axis=0))
    nxt = jnp.where(t == seq_len - 1, 0.0, pltpu.roll(u, n - 1, axis=0))
    return prev * w_ref[0:1, :] + u * w_ref[1:2, :] + nxt * w_ref[2:3, :] + b_ref[...]


def _dft_direct_mats():
    n, half = 2 * SEQ, SEQ
    k = np.arange(n)[:, None].astype(np.float64)
    t = np.arange(half)[None, :].astype(np.float64)
    ang = 2 * np.pi * k * t / n
    fr, fi = np.cos(ang), -np.sin(ang)
    mf = np.block([[fr, -fi], [fi, fr]])
    gr, gi = np.cos(ang).T / n, np.sin(ang).T / n
    mi = np.block([[gr, -gi], [gi, gr]])
    return mf.astype(np.float32), mi.astype(np.float32)


def _dft_real_mat():
    n = 2 * SEQ
    ang = 2 * np.pi * np.arange(n)[:, None].astype(np.float64) * np.arange(n)[None, :] / n
    return np.concatenate([np.cos(ang), -np.sin(ang)], axis=0).astype(np.float32)


def _spec_direct_kernel(h_ref, m_ref, o_ref):
    o_ref[0, 0] = jnp.dot(m_ref[...], h_ref[0, 0], precision=HIGHEST, preferred_element_type=F32)


def _spec_direct(h, m_real):
    n = 2 * SEQ
    return pl.pallas_call(
        _spec_direct_kernel,
        grid=(DEPTH, 2),
        in_specs=[pl.BlockSpec((1, 1, n, BRANCH_W), lambda l, o: (l, o, 0, 0)),
                  pl.BlockSpec((2 * n, n), lambda l, o: (0, 0))],
        out_specs=pl.BlockSpec((1, 1, 2 * n, BRANCH_W), lambda l, o: (l, o, 0, 0)),
        out_shape=jax.ShapeDtypeStruct((DEPTH, 2, 2 * n, BRANCH_W), F32),
        compiler_params=_cparams("arbitrary", "arbitrary"),
        name="hyena_spectrum_direct",
    )(h, m_real)


def _lconv_direct_kernel(s_ref, g_ref, cws_ref, cbs_ref, cwg_ref, cbg_ref, h_ref, bias_ref, mf_ref, mi_ref, o_ref,
                         *, conv_sig):
    n = 2 * SEQ
    sig = s_ref[...].astype(F32)
    if conv_sig:
        sig = _short_conv(sig, cws_ref, cbs_ref, SEQ)
    gate = _short_conv(g_ref[...].astype(F32), cwg_ref, cbg_ref, SEQ)
    z = jnp.dot(mf_ref[...], sig.astype(BF16), preferred_element_type=F32)
    zr, zi = z[0:n], z[n:2 * n]
    hr, hi = h_ref[0:n], h_ref[n:2 * n]
    y = jnp.concatenate([zr * hr - zi * hi, zr * hi + zi * hr], axis=0)
    y = jnp.dot(mi_ref[...], y.astype(BF16), preferred_element_type=F32)
    o_ref[...] = gate * (y + sig * bias_ref[...])


def _lconv_direct(sig, sig_col, gate_src, gate_col, conv_w, conv_b, spec, l, order, bias, mf, mi, conv_sig):
    n = 2 * SEQ
    rows = 2 * SEQ
    T = sig.shape[0]
    return pl.pallas_call(
        functools.partial(_lconv_direct_kernel, conv_sig=conv_sig),
        grid=(T // rows,),
        in_specs=[
            pl.BlockSpec((rows, BRANCH_W), lambda p: (p, sig_col)),
            pl.BlockSpec((rows, BRANCH_W), lambda p: (p, gate_col)),
            pl.BlockSpec((3, BRANCH_W), lambda p: (0, 0)),
            pl.BlockSpec((1, BRANCH_W), lambda p: (0, 0)),
            pl.BlockSpec((3, BRANCH_W), lambda p: (0, gate_col)),
            pl.BlockSpec((1, BRANCH_W), lambda p: (0, gate_col)),
            pl.BlockSpec((None, None, 2 * n, BRANCH_W), lambda p: (l, order, 0, 0)),
            pl.BlockSpec((1, BRANCH_W), lambda p: (0, 0)),
            pl.BlockSpec((2 * n, rows), lambda p: (0, 0)),
            pl.BlockSpec((rows, 2 * n), lambda p: (0, 0)),
        ],
        out_specs=pl.BlockSpec((rows, BRANCH_W), lambda p: (p, 0)),
        out_shape=jax.ShapeDtypeStruct((T, BRANCH_W), F32),
        compiler_params=_cparams("arbitrary"),
        name="hyena_lconv_direct",
    )(sig, gate_src, conv_w, conv_b, conv_w, conv_b, spec, bias, mf, mi)


def _dft_two_stage_mats():
    no, ni, half, n = FFT_NO, FFT_NI, FFT_HALF, FFT_N
    f64 = np.float64
    k1 = np.arange(no, dtype=f64)
    n_o = np.arange(half, dtype=f64)
    n_i = np.arange(ni, dtype=f64)
    ang = 2 * np.pi * (n_i[:, None, None] * k1[None, :, None] / n + k1[None, :, None] * n_o[None, None, :] / no)
    tr, ti = np.cos(ang), -np.sin(ang)
    m1 = np.concatenate([np.concatenate([tr, -ti], axis=2), np.concatenate([ti, tr], axis=2)], axis=1)
    k2 = np.arange(ni, dtype=f64)
    ang2 = 2 * np.pi * k2[:, None] * n_i[None, :] / ni
    f2r, f2i = np.cos(ang2), -np.sin(ang2)
    m2 = np.block([[f2r, -f2i], [f2i, f2r]])
    m2c = np.block([[f2r, f2i], [-f2i, f2r]])
    sr, si = np.transpose(tr, (0, 2, 1)) / n, -np.transpose(ti, (0, 2, 1)) / n
    m3 = np.concatenate([np.concatenate([sr, -si], axis=2), np.concatenate([si, sr], axis=2)], axis=1)
    return (m1.astype(np.float32), m2.astype(np.float32), m2c.astype(np.float32), m3.astype(np.float32))


def _dft_stage1_real_mat():
    no, ni, n = FFT_NO, FFT_NI, FFT_N
    k1 = np.arange(no, dtype=np.float64)
    n_o = np.arange(no, dtype=np.float64)
    n_i = np.arange(ni, dtype=np.float64)
    ang = 2 * np.pi * (n_i[:, None, None] * k1[None, :, None] / n + k1[None, :, None] * n_o[None, None, :] / no)
    return np.concatenate([np.cos(ang), -np.sin(ang)], axis=1).astype(np.float32)


def _store_stage1(w_ref, ni, out):
    w_ref[pl.ds(ni, FFT_NO, stride=2 * FFT_NI), :] = out[0:FFT_NO]
    w_ref[pl.ds(FFT_NI + ni, FFT_NO, stride=2 * FFT_NI), :] = out[FFT_NO:2 * FFT_NO]


def _fwd_stage1(za_ref, zb_ref, m1_ref, w_ref):
    def body(ni, carry):
        a = za_ref[pl.ds(ni, FFT_HALF, stride=FFT_NI), :]
        b = zb_ref[pl.ds(ni, FFT_HALF, stride=FFT_NI), :]
        out = jnp.dot(m1_ref[ni], jnp.concatenate([a, b], axis=0).astype(BF16), preferred_element_type=F32)
        _store_stage1(w_ref, ni, out)
        return carry

    lax.fori_loop(0, FFT_NI, body, 0, unroll=FFT_UNROLL)


def _spec_two_stage_kernel(h_ref, m1_ref, m2_ref, o_ref, w_ref):
    h = h_ref.at[0, 0]

    def stage1(ni, carry):
        a = h[pl.ds(ni, FFT_NO, stride=FFT_NI), :]
        _store_stage1(w_ref, ni, jnp.dot(m1_ref[ni], a.astype(BF16), preferred_element_type=F32))
        return carry

    lax.fori_loop(0, FFT_NI, stage1, 0, unroll=FFT_UNROLL)
    blk = 2 * FFT_NI

    cb = w_ref.shape[1]

    def stage2(kp, carry):
        rows = [pl.ds(pl.multiple_of((2 * kp + j) * blk, blk), blk) for j in range(2)]
        x = jnp.dot(m2_ref[...], jnp.concatenate([w_ref[r, :] for r in rows], axis=1).astype(BF16),
                    preferred_element_type=F32)
        for j in range(2):
            o_ref[0, 0, rows[j], :] = x[:, j * cb:(j + 1) * cb]
        return carry

    lax.fori_loop(0, FFT_NO // 2, stage2, 0, unroll=FFT_MID_UNROLL)


def _spec_two_stage(h, m1_real, m2):
    cb = LCONV_CB
    return pl.pallas_call(
        _spec_two_stage_kernel,
        grid=(DEPTH, 2, BRANCH_W // cb),
        in_specs=[pl.BlockSpec((1, 1, FFT_N, cb), lambda l, o, c: (l, o, 0, c)),
                  pl.BlockSpec((FFT_NI, 2 * FFT_NO, FFT_NO), lambda l, o, c: (0, 0, 0)),
                  pl.BlockSpec((2 * FFT_NI, 2 * FFT_NI), lambda l, o, c: (0, 0))],
        out_specs=pl.BlockSpec((1, 1, 2 * FFT_N, cb), lambda l, o, c: (l, o, 0, c)),
        out_shape=jax.ShapeDtypeStruct((DEPTH, 2, 2 * FFT_N, BRANCH_W), F32),
        scratch_shapes=[pltpu.VMEM((2 * FFT_N, cb), F32)],
        compiler_params=_cparams("arbitrary", "arbitrary", "arbitrary"),
        name="hyena_spectrum_two_stage",
    )(h, m1_real, m2)


def _lconv_two_stage_kernel(s_ref, g_ref, cws_ref, cbs_ref, cwg_ref, cbg_ref, h_ref, bias_ref,
                            m1_ref, m2_ref, m2c_ref, m3_ref, o_ref, z_ref, w_ref, *, conv_sig):
    for b in range(2):
        sig = s_ref[b].astype(F32)
        if conv_sig:
            sig = _short_conv(sig, cws_ref, cbs_ref, DEC_SEQ)
        z_ref[b] = sig
    _fwd_stage1(z_ref.at[0], z_ref.at[1], m1_ref, w_ref)
    blk = 2 * FFT_NI

    cb = w_ref.shape[1]

    def mid(kp, carry):
        rows = [pl.ds(pl.multiple_of((2 * kp + j) * blk, blk), blk) for j in range(2)]
        x = jnp.dot(m2_ref[...], jnp.concatenate([w_ref[r, :] for r in rows], axis=1).astype(BF16),
                    preferred_element_type=F32)
        h = jnp.concatenate([h_ref[r, :] for r in rows], axis=1)
        xr, xi = x[0:FFT_NI], x[FFT_NI:blk]
        hr, hi = h[0:FFT_NI], h[FFT_NI:blk]
        y = jnp.concatenate([xr * hr - xi * hi, xr * hi + xi * hr], axis=0)
        c = jnp.dot(m2c_ref[...], y.astype(BF16), preferred_element_type=F32)
        for j in range(2):
            w_ref[rows[j], :] = c[:, j * cb:(j + 1) * cb]
        return carry

    lax.fori_loop(0, FFT_NO // 2, mid, 0, unroll=FFT_MID_UNROLL)

    def last(ni, carry):
        cr = w_ref[pl.ds(ni, FFT_NO, stride=blk), :]
        ci = w_ref[pl.ds(FFT_NI + ni, FFT_NO, stride=blk), :]
        y = jnp.dot(m3_ref[ni], jnp.concatenate([cr, ci], axis=0).astype(BF16), preferred_element_type=F32)
        o_ref[0, pl.ds(ni, FFT_HALF, stride=FFT_NI), :] = y[0:FFT_HALF]
        o_ref[1, pl.ds(ni, FFT_HALF, stride=FFT_NI), :] = y[FFT_HALF:2 * FFT_HALF]
        return carry

    lax.fori_loop(0, FFT_NI, last, 0, unroll=FFT_UNROLL)
    for b in range(2):
        gate = _short_conv(g_ref[b].astype(F32), cwg_ref, cbg_ref, DEC_SEQ)
        sig = z_ref[b]
        o_ref[b] = gate * (o_ref[b] + sig * bias_ref[...])


def _lconv_two_stage(sig, sig_col, gate_src, gate_col, conv_w, conv_b, spec, l, order, bias, mats, conv_sig):
    cb = LCONV_CB
    ncb = BRANCH_W // cb
    m1, m2, m2c, m3 = mats
    const3 = lambda c, p: (0, 0, 0)
    const2 = lambda c, p: (0, 0)
    return pl.pallas_call(
        functools.partial(_lconv_two_stage_kernel, conv_sig=conv_sig),
        grid=(ncb, DEC_BATCH // 2),
        in_specs=[
            pl.BlockSpec((2, DEC_SEQ, cb), lambda c, p: (p, 0, sig_col * ncb + c)),
            pl.BlockSpec((2, DEC_SEQ, cb), lambda c, p: (p, 0, gate_col * ncb + c)),
            pl.BlockSpec((3, cb), lambda c, p: (0, c)),
            pl.BlockSpec((1, cb), lambda c, p: (0, c)),
            pl.BlockSpec((3, cb), lambda c, p: (0, gate_col * ncb + c)),
            pl.BlockSpec((1, cb), lambda c, p: (0, gate_col * ncb + c)),
            pl.BlockSpec((None, None, 2 * FFT_N, cb), lambda c, p: (l, order, 0, c)),
            pl.BlockSpec((1, cb), lambda c, p: (0, c)),
            pl.BlockSpec(m1.shape, const3),
            pl.BlockSpec(m2.shape, const2),
            pl.BlockSpec(m2c.shape, const2),
            pl.BlockSpec(m3.shape, const3),
        ],
        out_specs=pl.BlockSpec((2, DEC_SEQ, cb), lambda c, p: (p, 0, c)),
        out_shape=jax.ShapeDtypeStruct((DEC_BATCH, DEC_SEQ, BRANCH_W), F32),
        scratch_shapes=[pltpu.VMEM((2, DEC_SEQ, cb), F32), pltpu.VMEM((2 * FFT_N, cb), F32)],
        compiler_params=_cparams("arbitrary", "arbitrary"),
        name="hyena_lconv_two_stage",
    )(sig, gate_src, conv_w, conv_b, conv_w, conv_b, spec, bias, m1, m2, m2c, m3)


def kernel(x_prompt, x_sample, cache_na_k, cache_na_v, cache_da_k, cache_da_v, c, c_ctx, w_ada, b_ada, norm_mix,
           norm_ffn, w_in, hy_conv_w, hy_conv_b, hy_filt_w1, hy_filt_b1, hy_filt_w2, hy_filt_b2, hy_filt_w3,
           hy_filt_freq, hy_bias, na_rpb, da_lambda, da_subln, w_lift, w_out, w_ffn_in, w_ffn_out, norm_final):
    TP, TS = BATCH * SEQ, DEC_BATCH * DEC_SEQ
    xp = x_prompt.reshape(TP, D_MODEL)
    xs = x_sample.reshape(TS, D_MODEL)

    cc = jnp.concatenate([c_ctx[None, :], c, jnp.zeros((8 - 1 - DEC_BATCH, D_MODEL), F32)], axis=0)
    mod = _modulation(cc, w_ada, b_ada)
    mod_p = mod[:, 0:1].reshape(DEPTH, 1, 1, 6 * D_MODEL)
    mod_s = mod[:, 1:1 + DEC_BATCH].reshape(DEPTH, DEC_BATCH, 1, 6 * D_MODEL)

    w_mix = w_in[:, :, :MIX_W].astype(BF16)
    w_gate = w_in[:, :, MIX_W:].astype(BF16)
    w_lift_b = w_lift.astype(BF16)
    w_out_b = w_out.astype(BF16)
    w_ffn_in_b = w_ffn_in.astype(BF16)
    w_ffn_out_b = w_ffn_out.astype(BF16)
    g_mix = norm_mix.reshape(DEPTH, 1, D_MODEL)
    g_ffn = norm_ffn.reshape(DEPTH, 1, D_MODEL)
    g_fin = norm_final.reshape(1, D_MODEL)
    subln = da_subln.reshape(DEPTH, 1, DA_V_DIM)
    subln_col = da_subln.reshape(DEPTH, DA_V_DIM, 1)

    w1p = jnp.pad(hy_filt_w1, ((0, 0), (0, HY_FILT_HIDDEN - HY_POS_DIM), (0, 0)))
    b1 = hy_filt_b1.reshape(DEPTH, 1, HY_FILT_HIDDEN)
    b2 = hy_filt_b2.reshape(DEPTH, 1, HY_FILT_HIDDEN)
    fr = hy_filt_freq.reshape(DEPTH, 1, HY_FILT_HIDDEN)
    mf, mi = _dft_direct_mats()
    mats = _dft_two_stage_mats()
    h_p = _hyena_filters(SEQ, w1p, b1, hy_filt_w2, b2, hy_filt_w3, fr)
    h_s = _hyena_filters(DEC_SEQ, w1p, b1, hy_filt_w2, b2, hy_filt_w3, fr)
    spec_p = _spec_direct(h_p, jnp.asarray(_dft_real_mat()))
    mf_b, mi_b = jnp.asarray(mf, dtype=BF16), jnp.asarray(mi, dtype=BF16)
    mats_b = tuple(jnp.asarray(m, dtype=BF16) for m in mats)
    spec_s = _spec_two_stage(h_s, jnp.asarray(_dft_stage1_real_mat(), dtype=BF16), mats_b[1])
    conv_b = hy_conv_b.reshape(DEPTH, 1, 3 * BRANCH_W)

    na_bias = _na_bias_table(na_rpb)
    rope_tables = _rope_tables()
    ck_na = cache_na_k.reshape(DEC_BATCH, DEPTH, PAST_LEN, BRANCH_W)
    cv_na = cache_na_v.reshape(DEC_BATCH, DEPTH, PAST_LEN, BRANCH_W)
    ck_da = cache_da_k.reshape(DEC_BATCH, DEPTH, PAST_LEN, BRANCH_W)
    cv_da = cache_da_v.reshape(DEC_BATCH, DEPTH, PAST_LEN, BRANCH_W)

    caches = tuple(jnp.zeros((BATCH, DEPTH, SEQ, BRANCH_W), F32) for _ in CACHE_BLOCKS)
    for l in range(DEPTH):
        lam_init = 0.8 - 0.6 * math.exp(-0.3 * l)
        final = l == DEPTH - 1

        u, caches = _in_proj(xp, g_mix[l], mod_p[l], w_mix, l, TP, BF16, caches=caches)
        z1 = _lconv_direct(u, 0, u, 1, hy_conv_w[l], conv_b[l], spec_p, l, 0, hy_bias[l, 0:1], mf_b, mi_b, True)
        y_hy = _lconv_direct(z1, 0, u, 2, hy_conv_w[l], conv_b[l], spec_p, l, 1, hy_bias[l, 1:2], mf_b, mi_b, False)
        y_na, y_da = _ctx_attention(u, da_lambda[l], subln_col[l], lam_init)
        xp = _merge_out(xp, g_mix[l], mod_p[l], y_hy, y_na, y_da, w_gate, w_lift_b, w_out_b, l, TP)
        xp = _ffn(xp, g_ffn[l], mod_p[l], w_ffn_in_b, w_ffn_out_b, g_fin, l, TP, final)

        u = _in_proj(xs, g_mix[l], mod_s[l], w_mix, l, DEC_SEQ, BF16)
        u3 = u.reshape(DEC_BATCH, DEC_SEQ, MIX_W)
        z1 = _lconv_two_stage(u3, 0, u3, 1, hy_conv_w[l], conv_b[l], spec_s, l, 0, hy_bias[l, 0:1], mats_b, True)
        y_hy = _lconv_two_stage(z1, 0, u3, 2, hy_conv_w[l], conv_b[l], spec_s, l, 1, hy_bias[l, 1:2], mats_b, False)
        y_hy = y_hy.reshape(TS, BRANCH_W)
        qn, kn, vn = _attn_prep(u, 3, ck_na[:, l], cv_na[:, l], NA_HEAD_DIM)
        y_na = _nbr_attention(qn, kn, vn, na_bias, l)
        q, kt, v = _attn_prep(u, 6, ck_da[:, l], cv_da[:, l], DA_HEAD_DIM, rope_tables)
        y_da = _diff_attention(q, kt, v, da_lambda[l], subln_col[l], lam_init)
        xs = _merge_out(xs, g_mix[l], mod_s[l], y_hy, y_na, y_da, w_gate, w_lift_b, w_out_b, l, DEC_SEQ)
        xs = _ffn(xs, g_ffn[l], mod_s[l], w_ffn_in_b, w_ffn_out_b, g_fin, l, DEC_SEQ, final)

    y_prompt = xp.reshape(BATCH, SEQ, D_MODEL)
    y_sample = xs.reshape(DEC_BATCH, DEC_SEQ, D_MODEL)
    heads = lambda a, d: a.reshape(BATCH, DEPTH, SEQ, BRANCH_W // d, d)
    return (y_prompt, y_sample, heads(caches[0], NA_HEAD_DIM), heads(caches[1], NA_HEAD_DIM),
            heads(caches[2], 2 * DA_HEAD_DIM), heads(caches[3], DA_V_DIM))
```

```python
import functools
import math

import numpy as np
import jax
import jax.numpy as jnp
from jax import lax
from jax.experimental import pallas as pl
from jax.experimental.pallas import tpu as pltpu

F32 = jnp.float32
BF16 = jnp.bfloat16
HIGHEST = lax.Precision.HIGHEST

D_MODEL = 1024
BATCH = 32
SEQ = 256
DEPTH = 4
DEC_BATCH = 4
DEC_SEQ = 4096
PAST_LEN = 256
GRID_W = 64
GRID_H = DEC_SEQ // GRID_W
BRANCH_W = 512
HY_POS_BANDS = 16
HY_POS_DIM = 1 + 2 * HY_POS_BANDS
HY_FILT_HIDDEN = 64
HY_DECAY_TARGET = 1e-2
HY_FAST_DECAY = 0.3
HY_SLOW_DECAY = 1.5
NA_HEADS = 8
NA_HEAD_DIM = 64
NA_WIN_ROWS = 8
NA_WIN_COLS = 16
DA_HEADS = 8
DA_HEAD_DIM = 32
DA_V_DIM = 64
D_FF = 2816
MIX_W = 9 * BRANCH_W
ROPE_BASE = 10000.0
EPS = 1e-6
NEG_INF = -1e30

VMEM_LIMIT_BYTES = 56 * 1024 * 1024
LANES = 128
MXU_DIM = 256

FFT_N = 2 * DEC_SEQ
FFT_NO = 64
FFT_NI = 128
FFT_HALF = FFT_NO // 2
FFT_UNROLL = 8
FFT_MID_UNROLL = 16
LCONV_CB = LANES


def _cparams(*sem):
    return pltpu.CompilerParams(dimension_semantics=sem, vmem_limit_bytes=VMEM_LIMIT_BYTES)


def _sigmoid(x):
    return 1.0 / (1.0 + jnp.exp(-x))


def _rms(x, g):
    return x * lax.rsqrt(jnp.mean(x * x, axis=-1, keepdims=True) + EPS) * g


def _modnorm(x, g, shift, scale):
    return _rms(x, g) * (1.0 + scale) + shift


def _bdot(a, b):
    return jnp.dot(a.astype(BF16), b.astype(BF16), preferred_element_type=F32)


def _mod_kernel(c_ref, w_ref, b_ref, o_ref):
    c = c_ref[...]
    s = c * _sigmoid(c)
    o_ref[0] = jnp.dot(s, w_ref[0], precision=HIGHEST, preferred_element_type=F32) + b_ref[0]


def _modulation(cc, w_ada, b_ada):
    nt = 6
    return pl.pallas_call(
        _mod_kernel,
        grid=(DEPTH, nt),
        in_specs=[
            pl.BlockSpec((8, D_MODEL), lambda l, j: (0, 0)),
            pl.BlockSpec((1, D_MODEL, D_MODEL), lambda l, j: (l, 0, j)),
            pl.BlockSpec((1, 1, D_MODEL), lambda l, j: (l, 0, j)),
        ],
        out_specs=pl.BlockSpec((1, 8, D_MODEL), lambda l, j: (l, 0, j)),
        out_shape=jax.ShapeDtypeStruct((DEPTH, 8, 6 * D_MODEL), F32),
        compiler_params=_cparams("arbitrary", "arbitrary"),
        name="modulation",
    )(cc, w_ada, b_ada.reshape(DEPTH, 1, 6 * D_MODEL))


IN_TM = 512
CACHE_BLOCKS = (4, 5, 7, 8)


def _in_kernel(*refs, n_cache):
    x_ref, g_ref, mod_ref, w_ref = refs[:4]
    o_ref = refs[4 + n_cache]
    cache_refs = refs[5 + n_cache:]
    m = mod_ref[0]
    h = _modnorm(x_ref[...], g_ref[...], m[:, 0:D_MODEL], m[:, D_MODEL:2 * D_MODEL]).astype(BF16)
    res = jnp.dot(h, w_ref[...], preferred_element_type=F32)
    o_ref[...] = res.astype(o_ref.dtype)
    for c, c_ref in zip(CACHE_BLOCKS, cache_refs):
        c_ref[...] = res[:, c * BRANCH_W:(c + 1) * BRANCH_W].reshape(c_ref.shape)


def _in_proj(x, g, mod, w, l, rows_per_mod, out_dtype, caches=None):
    T = x.shape[0]
    tm = IN_TM
    per = rows_per_mod // tm
    in_specs = [
        pl.BlockSpec((tm, D_MODEL), lambda i: (i, 0)),
        pl.BlockSpec((1, D_MODEL), lambda i: (0, 0)),
        pl.BlockSpec((1, 1, 6 * D_MODEL), lambda i: (i // per, 0, 0)),
        pl.BlockSpec((None, D_MODEL, MIX_W), lambda i: (l, 0, 0), pipeline_mode=pl.Buffered(1)),
    ]
    out_specs = [pl.BlockSpec((tm, MIX_W), lambda i: (i, 0))]
    out_shape = [jax.ShapeDtypeStruct((T, MIX_W), out_dtype)]
    args = [x, g, mod, w]
    aliases = {}
    if caches is not None:
        out_specs += [pl.BlockSpec((tm // SEQ, 1, SEQ, BRANCH_W), lambda i: (i, l, 0, 0))] * len(caches)
        out_shape += [jax.ShapeDtypeStruct(c.shape, c.dtype) for c in caches]
        in_specs += [pl.BlockSpec(memory_space=pl.ANY)] * len(caches)
        aliases = {4 + n: 1 + n for n in range(len(caches))}
        args += list(caches)
    outs = pl.pallas_call(
        functools.partial(_in_kernel, n_cache=len(args) - 4),
        grid=(T // tm,),
        in_specs=in_specs,
        out_specs=out_specs,
        out_shape=out_shape,
        input_output_aliases=aliases,
        compiler_params=_cparams("arbitrary"),
        name="in_proj",
    )(*args)
    return outs[0] if caches is None else (outs[0], tuple(outs[1:]))


def _mid_kernel(x_ref, g_ref, mod_ref, yh_ref, yn_ref, yd_ref, wg_ref, wl_ref, wo_ref, o_ref):
    m = mod_ref[0]
    x = x_ref[...]
    h = _modnorm(x, g_ref[...], m[:, 0:D_MODEL], m[:, D_MODEL:2 * D_MODEL]).astype(BF16)
    merged = None
    for br, y_ref in enumerate((yh_ref, yn_ref, yd_ref)):
        gate = _sigmoid(jnp.dot(h, wg_ref[:, br * D_MODEL:(br + 1) * D_MODEL], preferred_element_type=F32))
        lift = jnp.dot(y_ref[...].astype(BF16), wl_ref[br], preferred_element_type=F32)
        t = gate * lift
        merged = t if merged is None else merged + t
    o_ref[...] = x + m[:, 2 * D_MODEL:3 * D_MODEL] * _bdot(merged, wo_ref[...])


def _merge_out(x, g, mod, y_hy, y_na, y_da, w_gate, w_lift, w_out, l, rows_per_mod):
    T = x.shape[0]
    tm = 512
    per = rows_per_mod // tm
    row = lambda i: (i, 0)
    const2 = lambda i: (0, 0)
    return pl.pallas_call(
        _mid_kernel,
        grid=(T // tm,),
        in_specs=[
            pl.BlockSpec((tm, D_MODEL), row),
            pl.BlockSpec((1, D_MODEL), const2),
            pl.BlockSpec((1, 1, 6 * D_MODEL), lambda i: (i // per, 0, 0)),
            pl.BlockSpec((tm, BRANCH_W), row),
            pl.BlockSpec((tm, BRANCH_W), row),
            pl.BlockSpec((tm, BRANCH_W), row),
            pl.BlockSpec((None, D_MODEL, 3 * D_MODEL), lambda i: (l, 0, 0)),
            pl.BlockSpec((None, 3, BRANCH_W, D_MODEL), lambda i: (l, 0, 0, 0)),
            pl.BlockSpec((None, D_MODEL, D_MODEL), lambda i: (l, 0, 0)),
        ],
        out_specs=pl.BlockSpec((tm, D_MODEL), row),
        out_shape=jax.ShapeDtypeStruct((T, D_MODEL), F32),
        compiler_params=_cparams("arbitrary"),
        name="merge_out",
    )(x, g, mod, y_hy, y_na, y_da, w_gate, w_lift, w_out)


def _ffn_kernel(x_ref, g_ref, mod_ref, w1_ref, w2_ref, gf_ref, o_ref, *, final):
    m = mod_ref[0]
    x = x_ref[...]
    h = _modnorm(x, g_ref[...], m[:, 3 * D_MODEL:4 * D_MODEL], m[:, 4 * D_MODEL:5 * D_MODEL]).astype(BF16)
    a = jnp.dot(h, w1_ref[:, 0:D_FF], preferred_element_type=F32)
    b = jnp.dot(h, w1_ref[:, D_FF:2 * D_FF], preferred_element_type=F32)
    xn = x + m[:, 5 * D_MODEL:6 * D_MODEL] * _bdot(a * _sigmoid(a) * b, w2_ref[...])
    if final:
        xn = _rms(xn, gf_ref[...])
    o_ref[...] = xn


def _ffn(x, g, mod, w_ffn_in, w_ffn_out, g_final, l, rows_per_mod, final):
    T = x.shape[0]
    tm = 512
    per = rows_per_mod // tm
    resident = pl.Buffered(1)
    return pl.pallas_call(
        functools.partial(_ffn_kernel, final=final),
        grid=(T // tm,),
        in_specs=[
            pl.BlockSpec((tm, D_MODEL), lambda i: (i, 0)),
            pl.BlockSpec((1, D_MODEL), lambda i: (0, 0)),
            pl.BlockSpec((1, 1, 6 * D_MODEL), lambda i: (i // per, 0, 0)),
            pl.BlockSpec((None, D_MODEL, 2 * D_FF), lambda i: (l, 0, 0), pipeline_mode=resident),
            pl.BlockSpec((None, D_FF, D_MODEL), lambda i: (l, 0, 0), pipeline_mode=resident),
            pl.BlockSpec((1, D_MODEL), lambda i: (0, 0)),
        ],
        out_specs=pl.BlockSpec((tm, D_MODEL), lambda i: (i, 0)),
        out_shape=jax.ShapeDtypeStruct((T, D_MODEL), F32),
        compiler_params=_cparams("arbitrary"),
        name="ffn",
    )(x, g, mod, w_ffn_in, w_ffn_out, g_final)


def _da_lambda(lam_ref, lam_init):
    lp = lam_ref[...]
    a = jnp.sum(lp[0:1] * lp[1:2], axis=1, keepdims=True)
    b = jnp.sum(lp[2:3] * lp[3:4], axis=1, keepdims=True)
    return jnp.exp(a) - jnp.exp(b) + lam_init


ATT_ONES_ROWS = 16
ATT_TQ = 256
ATT_KEYS = DEC_SEQ + PAST_LEN
LOG2E = math.log2(math.e)


def _masked_q_blocks(qt, d):
    row = lax.broadcasted_iota(jnp.int32, qt.shape, 0)
    zero = jnp.zeros_like(qt)
    return jnp.concatenate([jnp.where((row >= j * d) & (row < (j + 1) * d), qt, zero) for j in range(LANES // d)],
                           axis=1)


def _colmax(st):
    keys, n = st.shape
    return jnp.max(jnp.max(st.reshape(keys // MXU_DIM, MXU_DIM, n), axis=0), axis=0, keepdims=True)


def _ctx_attn_kernel(nq_ref, nk_ref, nv_ref, dq_ref, dk_ref, dv_ref, lam_ref, sub_ref, yn_ref, yd_ref, acc_ref,
                     *, lam_init):
    lam = _da_lambda(lam_ref, lam_init)
    ones = jnp.ones((ATT_ONES_ROWS, SEQ), BF16)

    def attend(q_ref, k_ref, v_ref, d, maps_per_head, finish):
        qt = (q_ref[...].astype(F32) * (d ** -0.5 * LOG2E)).T.astype(BF16)
        vt = v_ref[...].astype(F32).T.astype(BF16)
        kb = k_ref[...].astype(BF16)
        dv = NA_HEAD_DIM
        heads_per_group = LANES // (d * maps_per_head)
        w = maps_per_head * SEQ
        for g in range(BRANCH_W // LANES):
            lanes = slice(g * LANES, (g + 1) * LANES)
            st = jnp.dot(kb[:, lanes], _masked_q_blocks(qt[lanes], d), preferred_element_type=F32)
            pt = jnp.exp2(st - _colmax(st)).astype(BF16)
            for j in range(heads_per_group):
                h = g * heads_per_group + j
                ve = jnp.concatenate([vt[h * dv:(h + 1) * dv], ones], axis=0)
                oe = jnp.dot(ve, pt[:, j * w:(j + 1) * w], preferred_element_type=F32)
                os = [oe[0:dv, i * SEQ:(i + 1) * SEQ] / oe[dv:dv + 1, i * SEQ:(i + 1) * SEQ]
                      for i in range(maps_per_head)]
                acc_ref[h * dv:(h + 1) * dv, :] = finish(os)

    attend(nq_ref, nk_ref, nv_ref, NA_HEAD_DIM, 1, lambda os: os[0])
    yn_ref[...] = acc_ref[...].T.astype(yn_ref.dtype)

    def da_finish(os):
        ot = os[0] - lam * os[1]
        ot = ot * lax.rsqrt(jnp.mean(ot * ot, axis=0, keepdims=True) + EPS) * sub_ref[...]
        return ot * (1.0 - lam_init)

    attend(dq_ref, dk_ref, dv_ref, DA_HEAD_DIM, 2, da_finish)
    yd_ref[...] = acc_ref[...].T.astype(yd_ref.dtype)


def _ctx_attention(u, da_lambda, subln_col, lam_init):
    col = lambda j: pl.BlockSpec((SEQ, BRANCH_W), lambda b, j=j: (b, j))
    out = pl.BlockSpec((SEQ, BRANCH_W), lambda b: (b, 0))
    shape = jax.ShapeDtypeStruct((BATCH * SEQ, BRANCH_W), BF16)
    return pl.pallas_call(
        functools.partial(_ctx_attn_kernel, lam_init=lam_init),
        grid=(BATCH,),
        in_specs=[col(3), col(4), col(5), col(6), col(7), col(8),
                  pl.BlockSpec((4, DA_HEAD_DIM), lambda b: (0, 0)),
                  pl.BlockSpec((DA_V_DIM, 1), lambda b: (0, 0))],
        out_specs=[out, out],
        out_shape=[shape, shape],
        scratch_shapes=[pltpu.VMEM((BRANCH_W, SEQ), F32)],
        compiler_params=_cparams("arbitrary"),
        name="ctx_attention",
    )(u, u, u, u, u, u, da_lambda, subln_col)


def _rope(x, cos, sin_signed):
    n = x.shape[-1]
    lane = lax.broadcasted_iota(jnp.int32, x.shape, 1)
    partner = jnp.where(lane % 2 == 0, pltpu.roll(x, n - 1, axis=1), pltpu.roll(x, 1, axis=1))
    return x * cos + partner * sin_signed


def _attn_prep_kernel(q_ref, k_ref, v_ref, kc_ref, vc_ref, *refs, rope, scale):
    cos_ref, sin_ref = refs[:2] if rope else (None, None)
    qt_ref, ko_ref, vt_ref = refs[-3:]
    t = pl.program_id(1)
    dv = NA_HEAD_DIM

    def put_v(v):
        vt = v.astype(F32).T.astype(BF16)
        ones = jnp.ones((ATT_ONES_ROWS, ATT_TQ), BF16)
        for h in range(BRANCH_W // dv):
            vt_ref[0, h, 0:dv, :] = vt[h * dv:(h + 1) * dv]
            vt_ref[0, h, dv:dv + ATT_ONES_ROWS, :] = ones

    @pl.when(t < DEC_SEQ // ATT_TQ)
    def _():
        q = q_ref[...].astype(F32)
        k = k_ref[...].astype(F32)
        if rope:
            q = _rope(q, cos_ref[...], sin_ref[...])
            k = _rope(k, cos_ref[...], sin_ref[...])
        qt_ref[0] = (q * scale).T.astype(BF16)
        ko_ref[0] = k.astype(BF16)
        put_v(v_ref[...])

    @pl.when(t == DEC_SEQ // ATT_TQ)
    def _():
        ko_ref[0] = kc_ref[0].astype(BF16)
        put_v(vc_ref[0])


def _attn_prep(u, first_col, k_ctx, v_ctx, head_dim, rope_tables=None):
    rope = rope_tables is not None
    nt = DEC_SEQ // ATT_TQ
    last = nt - 1
    rowblk = lambda j: pl.BlockSpec((ATT_TQ, BRANCH_W), lambda b, t, j=j: (b * nt + jnp.minimum(t, last), j))
    tab = pl.BlockSpec((ATT_TQ, BRANCH_W), lambda b, t: (jnp.minimum(t, last), 0))
    ctx = pl.BlockSpec((1, PAST_LEN, BRANCH_W), lambda b, t: (b, 0, 0))
    heads = BRANCH_W // NA_HEAD_DIM
    vrows = NA_HEAD_DIM + ATT_ONES_ROWS
    return pl.pallas_call(
        functools.partial(_attn_prep_kernel, rope=rope, scale=head_dim ** -0.5 * LOG2E),
        grid=(DEC_BATCH, nt + 1),
        in_specs=[rowblk(first_col), rowblk(first_col + 1), rowblk(first_col + 2), ctx, ctx] + [tab, tab] * rope,
        out_specs=[
            pl.BlockSpec((1, BRANCH_W, ATT_TQ), lambda b, t: (b, 0, jnp.minimum(t, last))),
            pl.BlockSpec((1, ATT_TQ, BRANCH_W), lambda b, t: (b, t, 0)),
            pl.BlockSpec((1, heads, vrows, ATT_TQ), lambda b, t: (b, 0, 0, t)),
        ],
        out_shape=[
            jax.ShapeDtypeStruct((DEC_BATCH, BRANCH_W, DEC_SEQ), BF16),
            jax.ShapeDtypeStruct((DEC_BATCH, ATT_KEYS, BRANCH_W), BF16),
            jax.ShapeDtypeStruct((DEC_BATCH, heads, vrows, ATT_KEYS), BF16),
        ],
        compiler_params=_cparams("arbitrary", "arbitrary"),
        name="attn_prep",
    )(u, u, u, k_ctx, v_ctx, *(rope_tables or ()))


NA_ROWS = ATT_TQ // GRID_W
NA_UNION = 3 * NA_ROWS
NA_STEPS = GRID_H // NA_ROWS
NA_SLABS = NA_UNION // NA_ROWS
NA_VARIANT_OFFSET = (0, -NA_ROWS, -2 * NA_ROWS)


def _na_variant(s):
    return jnp.minimum(s, 1) + s // (NA_STEPS - 1)


def _na_window_block(s):
    return jnp.clip(s - 1, 0, NA_STEPS - NA_SLABS)


def _na_bias_kernel(rpb_ref, o_ref):
    kc = lax.broadcasted_iota(jnp.int32, (GRID_W, GRID_W), 0)
    qc = lax.broadcasted_iota(jnp.int32, (GRID_W, GRID_W), 1)
    dc = jnp.clip(kc - qc, -(NA_WIN_COLS - 1), NA_WIN_COLS - 1) + (NA_WIN_COLS - 1)
    c0 = jnp.clip(qc - NA_WIN_COLS // 2, 0, GRID_W - NA_WIN_COLS)
    col_ok = (kc >= c0) & (kc < c0 + NA_WIN_COLS)
    r = rpb_ref[0, 0] * LOG2E
    masked = jnp.full((GRID_W, GRID_W), NEG_INF, F32)
    tiles = []
    for dr in range(2 * NA_WIN_ROWS - 1):
        acc = jnp.zeros((GRID_W, GRID_W), F32)
        for d in range(2 * NA_WIN_COLS - 1):
            acc = jnp.where(dc == d, r[dr:dr + 1, d:d + 1], acc)
        tiles.append(jnp.where(col_ok, acc, masked))
    for v, off in enumerate(NA_VARIANT_OFFSET):
        for kr in range(NA_UNION):
            for rr in range(NA_ROWS):
                w0 = (0, rr, NA_UNION - NA_WIN_ROWS)[v]
                dr = kr + off - rr
                inside = w0 <= kr < w0 + NA_WIN_ROWS
                o_ref[0, v, 0, kr * GRID_W:(kr + 1) * GRID_W, rr * GRID_W:(rr + 1) * GRID_W] = (
                    tiles[dr + NA_WIN_ROWS - 1] if inside else masked)


def _na_bias_table(na_rpb):
    n_dr, n_dc = 2 * NA_WIN_ROWS - 1, 2 * NA_WIN_COLS - 1
    nv = len(NA_VARIANT_OFFSET)
    return pl.pallas_call(
        _na_bias_kernel,
        grid=(DEPTH, NA_HEADS),
        in_specs=[pl.BlockSpec((1, 1, n_dr, n_dc), lambda l, h: (l, h, 0, 0))],
        out_specs=pl.BlockSpec((1, nv, 1, NA_UNION * GRID_W, ATT_TQ), lambda l, h: (l, 0, h, 0, 0)),
        out_shape=jax.ShapeDtypeStruct((DEPTH, nv, NA_HEADS, NA_UNION * GRID_W, ATT_TQ), F32),
        compiler_params=_cparams("arbitrary", "arbitrary"),
        name="na_bias_table",
    )(na_rpb)


def _na_kernel(qt_ref, *refs):
    n = NA_SLABS + 1
    k_refs, vt_refs = refs[:n], refs[n:2 * n]
    bias_ref, o_ref, acc_ref = refs[2 * n:]
    dv = NA_HEAD_DIM
    heads_per_group = LANES // dv
    for g in range(BRANCH_W // LANES):
        lanes = slice(g * LANES, (g + 1) * LANES)
        qbd = _masked_q_blocks(qt_ref[0, lanes, :], dv)
        sts = []
        for j, k_ref in enumerate(k_refs):
            st = jnp.dot(k_ref[0, :, lanes], qbd, preferred_element_type=F32)
            if j < NA_SLABS:
                rows = slice(j * ATT_TQ, (j + 1) * ATT_TQ)
                st = st + jnp.concatenate(
                    [bias_ref[0, g * heads_per_group + hh, rows, :] for hh in range(heads_per_group)], axis=1)
            sts.append(st)
        mx = functools.reduce(jnp.maximum, [_colmax(st) for st in sts])
        pts = [jnp.exp2(st - mx).astype(BF16) for st in sts]
        for hh in range(heads_per_group):
            h = g * heads_per_group + hh
            oe = sum(jnp.dot(vt_ref[0, h], pt[:, hh * ATT_TQ:(hh + 1) * ATT_TQ], preferred_element_type=F32)
                     for vt_ref, pt in zip(vt_refs, pts))
            acc_ref[h * dv:(h + 1) * dv, :] = oe[0:dv] / oe[dv:dv + 1]
    o_ref[...] = acc_ref[...].T.astype(o_ref.dtype)


def _nbr_attention(qt, k, vt, bias, l):
    vrows = NA_HEAD_DIM + ATT_ONES_ROWS
    ctx_blk = DEC_SEQ // ATT_TQ
    k_specs = [pl.BlockSpec((1, ATT_TQ, BRANCH_W), lambda b, s, j=j: (b, _na_window_block(s) + j, 0))
               for j in range(NA_SLABS)]
    k_specs.append(pl.BlockSpec((1, ATT_TQ, BRANCH_W), lambda b, s: (b, ctx_blk, 0)))
    vt_specs = [pl.BlockSpec((1, NA_HEADS, vrows, ATT_TQ), lambda b, s, j=j: (b, 0, 0, _na_window_block(s) + j))
                for j in range(NA_SLABS)]
    vt_specs.append(pl.BlockSpec((1, NA_HEADS, vrows, ATT_TQ), lambda b, s: (b, 0, 0, ctx_blk)))
    n = NA_SLABS + 1
    return pl.pallas_call(
        _na_kernel,
        grid=(DEC_BATCH, NA_STEPS),
        in_specs=[pl.BlockSpec((1, BRANCH_W, ATT_TQ), lambda b, s: (b, 0, s))] + k_specs + vt_specs + [
            pl.BlockSpec((None, 1, NA_HEADS, NA_UNION * GRID_W, ATT_TQ), lambda b, s: (l, _na_variant(s), 0, 0, 0))],
        out_specs=pl.BlockSpec((ATT_TQ, BRANCH_W), lambda b, s: (b * NA_STEPS + s, 0)),
        out_shape=jax.ShapeDtypeStruct((DEC_BATCH * DEC_SEQ, BRANCH_W), BF16),
        scratch_shapes=[pltpu.VMEM((BRANCH_W, ATT_TQ), F32)],
        compiler_params=_cparams("arbitrary", "arbitrary"),
        name="nbr_attention",
    )(qt, *([k] * n), *([vt] * n), bias)


DA_TQ = ATT_TQ
DA_KEYS = ATT_KEYS
DA_ONES_ROWS = ATT_ONES_ROWS
DA_MAPS_PER_TILE = LANES // DA_HEAD_DIM


def _da_kernel(qt_ref, k_ref, vt_ref, lam_ref, sub_ref, o_ref, acc_ref, *, lam_init):
    lam = _da_lambda(lam_ref, lam_init)
    heads = DA_MAPS_PER_TILE // 2
    for g in range(BRANCH_W // LANES):
        lanes = slice(g * LANES, (g + 1) * LANES)
        st = jnp.dot(k_ref[0, :, lanes], _masked_q_blocks(qt_ref[0, lanes, :], DA_HEAD_DIM),
                     preferred_element_type=F32)
        pt = jnp.exp2(st - _colmax(st)).astype(BF16)
        for hh in range(heads):
            h = g * heads + hh
            oe = jnp.dot(vt_ref[0, h], pt[:, 2 * hh * DA_TQ:(2 * hh + 2) * DA_TQ], preferred_element_type=F32)
            os = [oe[0:DA_V_DIM, i * DA_TQ:(i + 1) * DA_TQ] / oe[DA_V_DIM:DA_V_DIM + 1, i * DA_TQ:(i + 1) * DA_TQ]
                  for i in range(2)]
            ot = os[0] - lam * os[1]
            ot = ot * lax.rsqrt(jnp.mean(ot * ot, axis=0, keepdims=True) + EPS) * sub_ref[...]
            acc_ref[h * DA_V_DIM:(h + 1) * DA_V_DIM, :] = ot * (1.0 - lam_init)
    o_ref[...] = acc_ref[...].T.astype(o_ref.dtype)


def _diff_attention(qt, k, vt, da_lambda, subln_col, lam_init):
    nt = DEC_SEQ // DA_TQ
    vrows = DA_V_DIM + DA_ONES_ROWS
    return pl.pallas_call(
        functools.partial(_da_kernel, lam_init=lam_init),
        grid=(DEC_BATCH, nt),
        in_specs=[
            pl.BlockSpec((1, BRANCH_W, DA_TQ), lambda b, t: (b, 0, t)),
            pl.BlockSpec((1, DA_KEYS, BRANCH_W), lambda b, t: (b, 0, 0)),
            pl.BlockSpec((1, DA_HEADS, vrows, DA_KEYS), lambda b, t: (b, 0, 0, 0)),
            pl.BlockSpec((4, DA_HEAD_DIM), lambda b, t: (0, 0)),
            pl.BlockSpec((DA_V_DIM, 1), lambda b, t: (0, 0)),
        ],
        out_specs=pl.BlockSpec((DA_TQ, BRANCH_W), lambda b, t: (b * nt + t, 0)),
        out_shape=jax.ShapeDtypeStruct((DEC_BATCH * DEC_SEQ, BRANCH_W), BF16),
        scratch_shapes=[pltpu.VMEM((BRANCH_W, DA_TQ), F32)],
        compiler_params=_cparams("arbitrary", "arbitrary"),
        name="diff_attention",
    )(qt, k, vt, da_lambda, subln_col)


def _rope_tables():
    pos = np.arange(DEC_SEQ)
    row = (pos // GRID_W).astype(np.float32)
    col = (pos % GRID_W).astype(np.float32)
    n_freq = DA_HEAD_DIM // 4
    inv = (np.float32(ROPE_BASE) ** (-np.arange(n_freq, dtype=np.float32) / n_freq)).astype(np.float32)
    ang = np.concatenate([row[:, None] * inv[None, :], col[:, None] * inv[None, :]], axis=-1)
    ang = ang.astype(np.float64)
    cos = np.repeat(np.cos(ang), 2, axis=-1)
    sin = np.repeat(np.sin(ang), 2, axis=-1)
    sign = np.where(np.arange(DA_HEAD_DIM) % 2 == 0, -1.0, 1.0)
    reps = BRANCH_W // DA_HEAD_DIM
    cos = np.tile(cos, (1, reps)).astype(np.float32)
    sin = np.tile(sin * sign[None, :], (1, reps)).astype(np.float32)
    return jnp.asarray(cos), jnp.asarray(sin)


def _filt_hidden_kernel(feat_ref, w1_ref, b1_ref, w2_ref, b2_ref, fr_ref, o_ref):
    fr = fr_ref[0]
    h = jnp.sin(fr * (jnp.dot(feat_ref[...], w1_ref[0], precision=HIGHEST, preferred_element_type=F32) + b1_ref[0]))
    o_ref[0] = jnp.sin(fr * (jnp.dot(h, w2_ref[0], precision=HIGHEST, preferred_element_type=F32) + b2_ref[0]))


def _filt_kernel(h_ref, w3f_ref, w3b_ref, dec_ref, o_ref):
    L = dec_ref.shape[0] // 2
    hf = jnp.dot(h_ref[0, 0:L], w3f_ref[0], precision=HIGHEST, preferred_element_type=F32) * dec_ref[0:L]
    hb = jnp.dot(h_ref[0, L:2 * L], w3b_ref[0], precision=HIGHEST, preferred_element_type=F32) * dec_ref[L:2 * L]
    row = lax.broadcasted_iota(jnp.int32, hb.shape, 0)
    hb = jnp.where(row == 0, 0.0, hb)
    nrm = jnp.sum(jnp.abs(hf), axis=0, keepdims=True) + jnp.sum(jnp.abs(hb), axis=0, keepdims=True)
    o_ref[0, 0, 0:L] = hf / nrm
    o_ref[0, 0, L:2 * L] = hb / nrm


def _circular_order(a):
    return np.concatenate([a, a[:1], a[1:][::-1]], axis=0)


def _hyena_pos_tables(L):
    f32 = np.float32
    pos = np.arange(L, dtype=f32)
    t = (pos / f32(L)).astype(f32)
    bands = np.linspace(1e-4, HY_POS_BANDS - 1, HY_POS_BANDS, dtype=f32)
    ang = (f32(2 * math.pi / L) * pos[:, None] * bands[None, :]).astype(np.float64)
    feats = np.zeros((L, HY_FILT_HIDDEN), f32)
    feats[:, 0] = t
    feats[:, 1:1 + HY_POS_BANDS] = np.cos(ang)
    feats[:, 1 + HY_POS_BANDS:HY_POS_DIM] = -np.sin(ang)
    deltas = np.linspace(math.log(HY_DECAY_TARGET) / HY_SLOW_DECAY,
                         math.log(HY_DECAY_TARGET) / HY_FAST_DECAY, BRANCH_W, dtype=f32)
    decay = np.exp((-t[:, None] * np.abs(deltas)[None, :]).astype(np.float64)).astype(f32)
    return jnp.asarray(_circular_order(feats)), jnp.asarray(_circular_order(decay))


def _hyena_filters(half, w1p, b1, w2, b2, w3, freq):
    feats, decay = _hyena_pos_tables(half)
    L = 2 * half
    cb = LANES
    ncb = BRANCH_W // cb
    small = lambda shape: pl.BlockSpec((1,) + shape, lambda l: (l, 0, 0))
    hidden = pl.pallas_call(
        _filt_hidden_kernel,
        grid=(DEPTH,),
        in_specs=[
            pl.BlockSpec((L, HY_FILT_HIDDEN), lambda l: (0, 0)),
            small((HY_FILT_HIDDEN, HY_FILT_HIDDEN)), small((1, HY_FILT_HIDDEN)),
            small((HY_FILT_HIDDEN, HY_FILT_HIDDEN)), small((1, HY_FILT_HIDDEN)),
            small((1, HY_FILT_HIDDEN)),
        ],
        out_specs=pl.BlockSpec((1, L, HY_FILT_HIDDEN), lambda l: (l, 0, 0)),
        out_shape=jax.ShapeDtypeStruct((DEPTH, L, HY_FILT_HIDDEN), F32),
        compiler_params=_cparams("arbitrary"),
        name=f"hyena_filter_hidden_{L}",
    )(feats, w1p, b1, w2, b2, freq)
    return pl.pallas_call(
        _filt_kernel,
        grid=(DEPTH, 2, ncb),
        in_specs=[
            pl.BlockSpec((1, L, HY_FILT_HIDDEN), lambda l, o, c: (l, 0, 0)),
            pl.BlockSpec((1, HY_FILT_HIDDEN, cb), lambda l, o, c: (l, 0, o * 2 * ncb + c)),
            pl.BlockSpec((1, HY_FILT_HIDDEN, cb), lambda l, o, c: (l, 0, o * 2 * ncb + ncb + c)),
            pl.BlockSpec((L, cb), lambda l, o, c: (0, c)),
        ],
        out_specs=pl.BlockSpec((1, 1, L, cb), lambda l, o, c: (l, o, 0, c)),
        out_shape=jax.ShapeDtypeStruct((DEPTH, 2, L, BRANCH_W), F32),
        compiler_params=_cparams("arbitrary", "arbitrary", "arbitrary"),
        name=f"hyena_filters_{L}",
    )(hidden, w3, w3, decay)


def _short_conv(u, w_ref, b_ref, seq_len):
    n = u.shape[0]
    t = lax.broadcasted_iota(jnp.int32, u.shape, 0) % seq_len
    prev = jnp.where(t == 0, 0.0, pltpu.roll(u, 1, axis=0))
    nxt = jnp.where(t == seq_len - 1, 0.0, pltpu.roll(u, n - 1, axis=0))
    return prev * w_ref[0:1, :] + u * w_ref[1:2, :] + nxt * w_ref[2:3, :] + b_ref[...]


def _dft_direct_mats():
    n, half = 2 * SEQ, SEQ
    k = np.arange(n)[:, None].astype(np.float64)
    t = np.arange(half)[None, :].astype(np.float64)
    ang = 2 * np.pi * k * t / n
    fr, fi = np.cos(ang), -np.sin(ang)
    mf = np.block([[fr, -fi], [fi, fr]])
    gr, gi = np.cos(ang).T / n, np.sin(ang).T / n
    mi = np.block([[gr, -gi], [gi, gr]])
    return mf.astype(np.float32), mi.astype(np.float32)


def _dft_real_mat():
    n = 2 * SEQ
    ang = 2 * np.pi * np.arange(n)[:, None].astype(np.float64) * np.arange(n)[None, :] / n
    return np.concatenate([np.cos(ang), -np.sin(ang)], axis=0).astype(np.float32)


def _spec_direct_kernel(h_ref, m_ref, o_ref):
    o_ref[0, 0] = jnp.dot(m_ref[...], h_ref[0, 0], precision=HIGHEST, preferred_element_type=F32)


def _spec_direct(h, m_real):
    n = 2 * SEQ
    return pl.pallas_call(
        _spec_direct_kernel,
        grid=(DEPTH, 2),
        in_specs=[pl.BlockSpec((1, 1, n, BRANCH_W), lambda l, o: (l, o, 0, 0)),
                  pl.BlockSpec((2 * n, n), lambda l, o: (0, 0))],
        out_specs=pl.BlockSpec((1, 1, 2 * n, BRANCH_W), lambda l, o: (l, o, 0, 0)),
        out_shape=jax.ShapeDtypeStruct((DEPTH, 2, 2 * n, BRANCH_W), F32),
        compiler_params=_cparams("arbitrary", "arbitrary"),
        name="hyena_spectrum_direct",
    )(h, m_real)


def _lconv_direct_kernel(s_ref, g_ref, cws_ref, cbs_ref, cwg_ref, cbg_ref, h_ref, bias_ref, mf_ref, mi_ref, o_ref,
                         *, conv_sig):
    n = 2 * SEQ
    sig = s_ref[...].astype(F32)
    if conv_sig:
        sig = _short_conv(sig, cws_ref, cbs_ref, SEQ)
    gate = _short_conv(g_ref[...].astype(F32), cwg_ref, cbg_ref, SEQ)
    z = jnp.dot(mf_ref[...], sig.astype(BF16), preferred_element_type=F32)
    zr, zi = z[0:n], z[n:2 * n]
    hr, hi = h_ref[0:n], h_ref[n:2 * n]
    y = jnp.concatenate([zr * hr - zi * hi, zr * hi + zi * hr], axis=0)
    y = jnp.dot(mi_ref[...], y.astype(BF16), preferred_element_type=F32)
    o_ref[...] = gate * (y + sig * bias_ref[...])


def _lconv_direct(sig, sig_col, gate_src, gate_col, conv_w, conv_b, spec, l, order, bias, mf, mi, conv_sig):
    n = 2 * SEQ
    rows = 2 * SEQ
    T = sig.shape[0]
    return pl.pallas_call(
        functools.partial(_lconv_direct_kernel, conv_sig=conv_sig),
        grid=(T // rows,),
        in_specs=[
            pl.BlockSpec((rows, BRANCH_W), lambda p: (p, sig_col)),
            pl.BlockSpec((rows, BRANCH_W), lambda p: (p, gate_col)),
            pl.BlockSpec((3, BRANCH_W), lambda p: (0, 0)),
            pl.BlockSpec((1, BRANCH_W), lambda p: (0, 0)),
            pl.BlockSpec((3, BRANCH_W), lambda p: (0, gate_col)),
            pl.BlockSpec((1, BRANCH_W), lambda p: (0, gate_col)),
            pl.BlockSpec((None, None, 2 * n, BRANCH_W), lambda p: (l, order, 0, 0)),
            pl.BlockSpec((1, BRANCH_W), lambda p: (0, 0)),
            pl.BlockSpec((2 * n, rows), lambda p: (0, 0)),
            pl.BlockSpec((rows, 2 * n), lambda p: (0, 0)),
        ],
        out_specs=pl.BlockSpec((rows, BRANCH_W), lambda p: (p, 0)),
        out_shape=jax.ShapeDtypeStruct((T, BRANCH_W), F32),
        compiler_params=_cparams("arbitrary"),
        name="hyena_lconv_direct",
    )(sig, gate_src, conv_w, conv_b, conv_w, conv_b, spec, bias, mf, mi)


def _dft_two_stage_mats():
    no, ni, half, n = FFT_NO, FFT_NI, FFT_HALF, FFT_N
    f64 = np.float64
    k1 = np.arange(no, dtype=f64)
    n_o = np.arange(half, dtype=f64)
    n_i = np.arange(ni, dtype=f64)
    ang = 2 * np.pi * (n_i[:, None, None] * k1[None, :, None] / n + k1[None, :, None] * n_o[None, None, :] / no)
    tr, ti = np.cos(ang), -np.sin(ang)
    m1 = np.concatenate([np.concatenate([tr, -ti], axis=2), np.concatenate([ti, tr], axis=2)], axis=1)
    k2 = np.arange(ni, dtype=f64)
    ang2 = 2 * np.pi * k2[:, None] * n_i[None, :] / ni
    f2r, f2i = np.cos(ang2), -np.sin(ang2)
    m2 = np.block([[f2r, -f2i], [f2i, f2r]])
    m2c = np.block([[f2r, f2i], [-f2i, f2r]])
    sr, si = np.transpose(tr, (0, 2, 1)) / n, -np.transpose(ti, (0, 2, 1)) / n
    m3 = np.concatenate([np.concatenate([sr, -si], axis=2), np.concatenate([si, sr], axis=2)], axis=1)
    return (m1.astype(np.float32), m2.astype(np.float32), m2c.astype(np.float32), m3.astype(np.float32))


def _dft_stage1_real_mat():
    no, ni, n = FFT_NO, FFT_NI, FFT_N
    k1 = np.arange(no, dtype=np.float64)
    n_o = np.arange(no, dtype=np.float64)
    n_i = np.arange(ni, dtype=np.float64)
    ang = 2 * np.pi * (n_i[:, None, None] * k1[None, :, None] / n + k1[None, :, None] * n_o[None, None, :] / no)
    return np.concatenate([np.cos(ang), -np.sin(ang)], axis=1).astype(np.float32)


def _store_stage1(w_ref, ni, out):
    w_ref[pl.ds(ni, FFT_NO, stride=2 * FFT_NI), :] = out[0:FFT_NO]
    w_ref[pl.ds(FFT_NI + ni, FFT_NO, stride=2 * FFT_NI), :] = out[FFT_NO:2 * FFT_NO]


def _fwd_stage1(za_ref, zb_ref, m1_ref, w_ref):
    def body(ni, carry):
        a = za_ref[pl.ds(ni, FFT_HALF, stride=FFT_NI), :]
        b = zb_ref[pl.ds(ni, FFT_HALF, stride=FFT_NI), :]
        out = jnp.dot(m1_ref[ni], jnp.concatenate([a, b], axis=0).astype(BF16), preferred_element_type=F32)
        _store_stage1(w_ref, ni, out)
        return carry

    lax.fori_loop(0, FFT_NI, body, 0, unroll=FFT_UNROLL)


def _spec_two_stage_kernel(h_ref, m1_ref, m2_ref, o_ref, w_ref):
    h = h_ref.at[0, 0]

    def stage1(ni, carry):
        a = h[pl.ds(ni, FFT_NO, stride=FFT_NI), :]
        _store_stage1(w_ref, ni, jnp.dot(m1_ref[ni], a.astype(BF16), preferred_element_type=F32))
        return carry

    lax.fori_loop(0, FFT_NI, stage1, 0, unroll=FFT_UNROLL)
    blk = 2 * FFT_NI

    cb = w_ref.shape[1]

    def stage2(kp, carry):
        rows = [pl.ds(pl.multiple_of((2 * kp + j) * blk, blk), blk) for j in range(2)]
        x = jnp.dot(m2_ref[...], jnp.concatenate([w_ref[r, :] for r in rows], axis=1).astype(BF16),
                    preferred_element_type=F32)
        for j in range(2):
            o_ref[0, 0, rows[j], :] = x[:, j * cb:(j + 1) * cb]
        return carry

    lax.fori_loop(0, FFT_NO // 2, stage2, 0, unroll=FFT_MID_UNROLL)


def _spec_two_stage(h, m1_real, m2):
    cb = LCONV_CB
    return pl.pallas_call(
        _spec_two_stage_kernel,
        grid=(DEPTH, 2, BRANCH_W // cb),
        in_specs=[pl.BlockSpec((1, 1, FFT_N, cb), lambda l, o, c: (l, o, 0, c)),
                  pl.BlockSpec((FFT_NI, 2 * FFT_NO, FFT_NO), lambda l, o, c: (0, 0, 0)),
                  pl.BlockSpec((2 * FFT_NI, 2 * FFT_NI), lambda l, o, c: (0, 0))],
        out_specs=pl.BlockSpec((1, 1, 2 * FFT_N, cb), lambda l, o, c: (l, o, 0, c)),
        out_shape=jax.ShapeDtypeStruct((DEPTH, 2, 2 * FFT_N, BRANCH_W), F32),
        scratch_shapes=[pltpu.VMEM((2 * FFT_N, cb), F32)],
        compiler_params=_cparams("arbitrary", "arbitrary", "arbitrary"),
        name="hyena_spectrum_two_stage",
    )(h, m1_real, m2)


def _lconv_two_stage_kernel(s_ref, g_ref, cws_ref, cbs_ref, cwg_ref, cbg_ref, h_ref, bias_ref,
                            m1_ref, m2_ref, m2c_ref, m3_ref, o_ref, z_ref, w_ref, *, conv_sig):
    for b in range(2):
        sig = s_ref[b].astype(F32)
        if conv_sig:
            sig = _short_conv(sig, cws_ref, cbs_ref, DEC_SEQ)
        z_ref[b] = sig
    _fwd_stage1(z_ref.at[0], z_ref.at[1], m1_ref, w_ref)
    blk = 2 * FFT_NI

    cb = w_ref.shape[1]

    def mid(kp, carry):
        rows = [pl.ds(pl.multiple_of((2 * kp + j) * blk, blk), blk) for j in range(2)]
        x = jnp.dot(m2_ref[...], jnp.concatenate([w_ref[r, :] for r in rows], axis=1).astype(BF16),
                    preferred_element_type=F32)
        h = jnp.concatenate([h_ref[r, :] for r in rows], axis=1)
        xr, xi = x[0:FFT_NI], x[FFT_NI:blk]
        hr, hi = h[0:FFT_NI], h[FFT_NI:blk]
        y = jnp.concatenate([xr * hr - xi * hi, xr * hi + xi * hr], axis=0)
        c = jnp.dot(m2c_ref[...], y.astype(BF16), preferred_element_type=F32)
        for j in range(2):
            w_ref[rows[j], :] = c[:, j * cb:(j + 1) * cb]
        return carry

    lax.fori_loop(0, FFT_NO // 2, mid, 0, unroll=FFT_MID_UNROLL)

    def last(ni, carry):
        cr = w_ref[pl.ds(ni, FFT_NO, stride=blk), :]
        ci = w_ref[pl.ds(FFT_NI + ni, FFT_NO, stride=blk), :]
        y = jnp.dot(m3_ref[ni], jnp.concatenate([cr, ci], axis=0).astype(BF16), preferred_element_type=F32)
        o_ref[0, pl.ds(ni, FFT_HALF, stride=FFT_NI), :] = y[0:FFT_HALF]
        o_ref[1, pl.ds(ni, FFT_HALF, stride=FFT_NI), :] = y[FFT_HALF:2 * FFT_HALF]
        return carry

    lax.fori_loop(0, FFT_NI, last, 0, unroll=FFT_UNROLL)
    for b in range(2):
        gate = _short_conv(g_ref[b].astype(F32), cwg_ref, cbg_ref, DEC_SEQ)
        sig = z_ref[b]
        o_ref[b] = gate * (o_ref[b] + sig * bias_ref[...])


def _lconv_two_stage(sig, sig_col, gate_src, gate_col, conv_w, conv_b, spec, l, order, bias, mats, conv_sig):
    cb = LCONV_CB
    ncb = BRANCH_W // cb
    m1, m2, m2c, m3 = mats
    const3 = lambda c, p: (0, 0, 0)
    const2 = lambda c, p: (0, 0)
    return pl.pallas_call(
        functools.partial(_lconv_two_stage_kernel, conv_sig=conv_sig),
        grid=(ncb, DEC_BATCH // 2),
        in_specs=[
            pl.BlockSpec((2, DEC_SEQ, cb), lambda c, p: (p, 0, sig_col * ncb + c)),
            pl.BlockSpec((2, DEC_SEQ, cb), lambda c, p: (p, 0, gate_col * ncb + c)),
            pl.BlockSpec((3, cb), lambda c, p: (0, c)),
            pl.BlockSpec((1, cb), lambda c, p: (0, c)),
            pl.BlockSpec((3, cb), lambda c, p: (0, gate_col * ncb + c)),
            pl.BlockSpec((1, cb), lambda c, p: (0, gate_col * ncb + c)),
            pl.BlockSpec((None, None, 2 * FFT_N, cb), lambda c, p: (l, order, 0, c)),
            pl.BlockSpec((1, cb), lambda c, p: (0, c)),
            pl.BlockSpec(m1.shape, const3),
            pl.BlockSpec(m2.shape, const2),
            pl.BlockSpec(m2c.shape, const2),
            pl.BlockSpec(m3.shape, const3),
        ],
        out_specs=pl.BlockSpec((2, DEC_SEQ, cb), lambda c, p: (p, 0, c)),
        out_shape=jax.ShapeDtypeStruct((DEC_BATCH, DEC_SEQ, BRANCH_W), F32),
        scratch_shapes=[pltpu.VMEM((2, DEC_SEQ, cb), F32), pltpu.VMEM((2 * FFT_N, cb), F32)],
        compiler_params=_cparams("arbitrary", "arbitrary"),
        name="hyena_lconv_two_stage",
    )(sig, gate_src, conv_w, conv_b, conv_w, conv_b, spec, bias, m1, m2, m2c, m3)


def kernel(x_prompt, x_sample, cache_na_k, cache_na_v, cache_da_k, cache_da_v, c, c_ctx, w_ada, b_ada, norm_mix,
           norm_ffn, w_in, hy_conv_w, hy_conv_b, hy_filt_w1, hy_filt_b1, hy_filt_w2, hy_filt_b2, hy_filt_w3,
           hy_filt_freq, hy_bias, na_rpb, da_lambda, da_subln, w_lift, w_out, w_ffn_in, w_ffn_out, norm_final):
    TP, TS = BATCH * SEQ, DEC_BATCH * DEC_SEQ
    xp = x_prompt.reshape(TP, D_MODEL)
    xs = x_sample.reshape(TS, D_MODEL)

    cc = jnp.concatenate([c_ctx[None, :], c, jnp.zeros((8 - 1 - DEC_BATCH, D_MODEL), F32)], axis=0)
    mod = _modulation(cc, w_ada, b_ada)
    mod_p = mod[:, 0:1].reshape(DEPTH, 1, 1, 6 * D_MODEL)
    mod_s = mod[:, 1:1 + DEC_BATCH].reshape(DEPTH, DEC_BATCH, 1, 6 * D_MODEL)

    w_mix = w_in[:, :, :MIX_W].astype(BF16)
    w_gate = w_in[:, :, MIX_W:].astype(BF16)
    w_lift_b = w_lift.astype(BF16)
    w_out_b = w_out.astype(BF16)
    w_ffn_in_b = w_ffn_in.astype(BF16)
    w_ffn_out_b = w_ffn_out.astype(BF16)
    g_mix = norm_mix.reshape(DEPTH, 1, D_MODEL)
    g_ffn = norm_ffn.reshape(DEPTH, 1, D_MODEL)
    g_fin = norm_final.reshape(1, D_MODEL)
    subln = da_subln.reshape(DEPTH, 1, DA_V_DIM)
    subln_col = da_subln.reshape(DEPTH, DA_V_DIM, 1)

    w1p = jnp.pad(hy_filt_w1, ((0, 0), (0, HY_FILT_HIDDEN - HY_POS_DIM), (0, 0)))
    b1 = hy_filt_b1.reshape(DEPTH, 1, HY_FILT_HIDDEN)
    b2 = hy_filt_b2.reshape(DEPTH, 1, HY_FILT_HIDDEN)
    fr = hy_filt_freq.reshape(DEPTH, 1, HY_FILT_HIDDEN)
    mf, mi = _dft_direct_mats()
    mats = _dft_two_stage_mats()
    h_p = _hyena_filters(SEQ, w1p, b1, hy_filt_w2, b2, hy_filt_w3, fr)
    h_s = _hyena_filters(DEC_SEQ, w1p, b1, hy_filt_w2, b2, hy_filt_w3, fr)
    spec_p = _spec_direct(h_p, jnp.asarray(_dft_real_mat()))
    mf_b, mi_b = jnp.asarray(mf, dtype=BF16), jnp.asarray(mi, dtype=BF16)
    mats_b = tuple(jnp.asarray(m, dtype=BF16) for m in mats)
    spec_s = _spec_two_stage(h_s, jnp.asarray(_dft_stage1_real_mat(), dtype=BF16), mats_b[1])
    conv_b = hy_conv_b.reshape(DEPTH, 1, 3 * BRANCH_W)

    na_bias = _na_bias_table(na_rpb)
    rope_tables = _rope_tables()
    ck_na = cache_na_k.reshape(DEC_BATCH, DEPTH, PAST_LEN, BRANCH_W)
    cv_na = cache_na_v.reshape(DEC_BATCH, DEPTH, PAST_LEN, BRANCH_W)
    ck_da = cache_da_k.reshape(DEC_BATCH, DEPTH, PAST_LEN, BRANCH_W)
    cv_da = cache_da_v.reshape(DEC_BATCH, DEPTH, PAST_LEN, BRANCH_W)

    caches = tuple(jnp.zeros((BATCH, DEPTH, SEQ, BRANCH_W), F32) for _ in CACHE_BLOCKS)
    for l in range(DEPTH):
        lam_init = 0.8 - 0.6 * math.exp(-0.3 * l)
        final = l == DEPTH - 1

        u, caches = _in_proj(xp, g_mix[l], mod_p[l], w_mix, l, TP, BF16, caches=caches)
        z1 = _lconv_direct(u, 0, u, 1, hy_conv_w[l], conv_b[l], spec_p, l, 0, hy_bias[l, 0:1], mf_b, mi_b, True)
        y_hy = _lconv_direct(z1, 0, u, 2, hy_conv_w[l], conv_b[l], spec_p, l, 1, hy_bias[l, 1:2], mf_b, mi_b, False)
        y_na, y_da = _ctx_attention(u, da_lambda[l], subln_col[l], lam_init)
        xp = _merge_out(xp, g_mix[l], mod_p[l], y_hy, y_na, y_da, w_gate, w_lift_b, w_out_b, l, TP)
        xp = _ffn(xp, g_ffn[l], mod_p[l], w_ffn_in_b, w_ffn_out_b, g_fin, l, TP, final)

        u = _in_proj(xs, g_mix[l], mod_s[l], w_mix, l, DEC_SEQ, BF16)
        u3 = u.reshape(DEC_BATCH, DEC_SEQ, MIX_W)
        z1 = _lconv_two_stage(u3, 0, u3, 1, hy_conv_w[l], conv_b[l], spec_s, l, 0, hy_bias[l, 0:1], mats_b, True)
        y_hy = _lconv_two_stage(z1, 0, u3, 2, hy_conv_w[l], conv_b[l], spec_s, l, 1, hy_bias[l, 1:2], mats_b, False)
        y_hy = y_hy.reshape(TS, BRANCH_W)
        qn, kn, vn = _attn_prep(u, 3, ck_na[:, l], cv_na[:, l], NA_HEAD_DIM)
        y_na = _nbr_attention(qn, kn, vn, na_bias, l)
        q, kt, v = _attn_prep(u, 6, ck_da[:, l], cv_da[:, l], DA_HEAD_DIM, rope_tables)
        y_da = _diff_attention(q, kt, v, da_lambda[l], subln_col[l], lam_init)
        xs = _merge_out(xs, g_mix[l], mod_s[l], y_hy, y_na, y_da, w_gate, w_lift_b, w_out_b, l, DEC_SEQ)
        xs = _ffn(xs, g_ffn[l], mod_s[l], w_ffn_in_b, w_ffn_out_b, g_fin, l, DEC_SEQ, final)

    y_prompt = xp.reshape(BATCH, SEQ, D_MODEL)
    y_sample = xs.reshape(DEC_BATCH, DEC_SEQ, D_MODEL)
    heads = lambda a, d: a.reshape(BATCH, DEPTH, SEQ, BRANCH_W // d, d)
    return (y_prompt, y_sample, heads(caches[0], NA_HEAD_DIM), heads(caches[1], NA_HEAD_DIM),
            heads(caches[2], 2 * DA_HEAD_DIM), heads(caches[3], DA_V_DIM))
```

```python
import functools
import math

import numpy as np
import jax
import jax.numpy as jnp
from jax import lax
from jax.experimental import pallas as pl
from jax.experimental.pallas import tpu as pltpu

F32 = jnp.float32
BF16 = jnp.bfloat16
HIGHEST = lax.Precision.HIGHEST

D_MODEL = 1024
BATCH = 32
SEQ = 256
DEPTH = 4
DEC_BATCH = 4
DEC_SEQ = 4096
PAST_LEN = 256
GRID_W = 64
GRID_H = DEC_SEQ // GRID_W
BRANCH_W = 512
HY_POS_BANDS = 16
HY_POS_DIM = 1 + 2 * HY_POS_BANDS
HY_FILT_HIDDEN = 64
HY_DECAY_TARGET = 1e-2
HY_FAST_DECAY = 0.3
HY_SLOW_DECAY = 1.5
NA_HEADS = 8
NA_HEAD_DIM = 64
NA_WIN_ROWS = 8
NA_WIN_COLS = 16
DA_HEADS = 8
DA_HEAD_DIM = 32
DA_V_DIM = 64
D_FF = 2816
MIX_W = 9 * BRANCH_W
ROPE_BASE = 10000.0
EPS = 1e-6
NEG_INF = -1e30

VMEM_LIMIT_BYTES = 56 * 1024 * 1024
LANES = 128
MXU_DIM = 256

FFT_N = 2 * DEC_SEQ
FFT_NO = 64
FFT_NI = 128
FFT_HALF = FFT_NO // 2
FFT_UNROLL = 8
FFT_MID_UNROLL = 16
LCONV_CB = LANES


def _cparams(*sem):
    return pltpu.CompilerParams(dimension_semantics=sem, vmem_limit_bytes=VMEM_LIMIT_BYTES)


def _sigmoid(x):
    return 1.0 / (1.0 + jnp.exp(-x))


def _rms(x, g):
    return x * lax.rsqrt(jnp.mean(x * x, axis=-1, keepdims=True) + EPS) * g


def _modnorm(x, g, shift, scale):
    return _rms(x, g) * (1.0 + scale) + shift


def _bdot(a, b):
    return jnp.dot(a.astype(BF16), b.astype(BF16), preferred_element_type=F32)


def _mod_kernel(c_ref, w_ref, b_ref, o_ref):
    c = c_ref[...]
    s = c * _sigmoid(c)
    o_ref[0] = jnp.dot(s, w_ref[0], precision=HIGHEST, preferred_element_type=F32) + b_ref[0]


def _modulation(cc, w_ada, b_ada):
    nt = 6
    return pl.pallas_call(
        _mod_kernel,
        grid=(DEPTH, nt),
        in_specs=[
            pl.BlockSpec((8, D_MODEL), lambda l, j: (0, 0)),
            pl.BlockSpec((1, D_MODEL, D_MODEL), lambda l, j: (l, 0, j)),
            pl.BlockSpec((1, 1, D_MODEL), lambda l, j: (l, 0, j)),
        ],
        out_specs=pl.BlockSpec((1, 8, D_MODEL), lambda l, j: (l, 0, j)),
        out_shape=jax.ShapeDtypeStruct((DEPTH, 8, 6 * D_MODEL), F32),
        compiler_params=_cparams("arbitrary", "arbitrary"),
        name="modulation",
    )(cc, w_ada, b_ada.reshape(DEPTH, 1, 6 * D_MODEL))


IN_TM = 512
CACHE_BLOCKS = (4, 5, 7, 8)


def _in_kernel(*refs, n_cache):
    x_ref, g_ref, mod_ref, w_ref = refs[:4]
    o_ref = refs[4 + n_cache]
    cache_refs = refs[5 + n_cache:]
    m = mod_ref[0]
    h = _modnorm(x_ref[...], g_ref[...], m[:, 0:D_MODEL], m[:, D_MODEL:2 * D_MODEL]).astype(BF16)
    res = jnp.dot(h, w_ref[...], preferred_element_type=F32)
    o_ref[...] = res.astype(o_ref.dtype)
    for c, c_ref in zip(CACHE_BLOCKS, cache_refs):
        c_ref[...] = res[:, c * BRANCH_W:(c + 1) * BRANCH_W].reshape(c_ref.shape)


def _in_proj(x, g, mod, w, l, rows_per_mod, out_dtype, caches=None):
    T = x.shape[0]
    tm = IN_TM
    per = rows_per_mod // tm
    in_specs = [
        pl.BlockSpec((tm, D_MODEL), lambda i: (i, 0)),
        pl.BlockSpec((1, D_MODEL), lambda i: (0, 0)),
        pl.BlockSpec((1, 1, 6 * D_MODEL), lambda i: (i // per, 0, 0)),
        pl.BlockSpec((None, D_MODEL, MIX_W), lambda i: (l, 0, 0), pipeline_mode=pl.Buffered(1)),
    ]
    out_specs = [pl.BlockSpec((tm, MIX_W), lambda i: (i, 0))]
    out_shape = [jax.ShapeDtypeStruct((T, MIX_W), out_dtype)]
    args = [x, g, mod, w]
    aliases = {}
    if caches is not None:
        out_specs += [pl.BlockSpec((tm // SEQ, 1, SEQ, BRANCH_W), lambda i: (i, l, 0, 0))] * len(caches)
        out_shape += [jax.ShapeDtypeStruct(c.shape, c.dtype) for c in caches]
        in_specs += [pl.BlockSpec(memory_space=pl.ANY)] * len(caches)
        aliases = {4 + n: 1 + n for n in range(len(caches))}
        args += list(caches)
    outs = pl.pallas_call(
        functools.partial(_in_kernel, n_cache=len(args) - 4),
        grid=(T // tm,),
        in_specs=in_specs,
        out_specs=out_specs,
        out_shape=out_shape,
        input_output_aliases=aliases,
        compiler_params=_cparams("arbitrary"),
        name="in_proj",
    )(*args)
    return outs[0] if caches is None else (outs[0], tuple(outs[1:]))


def _mid_kernel(x_ref, g_ref, mod_ref, yh_ref, yn_ref, yd_ref, wg_ref, wl_ref, wo_ref, o_ref):
    m = mod_ref[0]
    x = x_ref[...]
    h = _modnorm(x, g_ref[...], m[:, 0:D_MODEL], m[:, D_MODEL:2 * D_MODEL]).astype(BF16)
    merged = None
    for br, y_ref in enumerate((yh_ref, yn_ref, yd_ref)):
        gate = _sigmoid(jnp.dot(h, wg_ref[:, br * D_MODEL:(br + 1) * D_MODEL], preferred_element_type=F32))
        lift = jnp.dot(y_ref[...].astype(BF16), wl_ref[br], preferred_element_type=F32)
        t = gate * lift
        merged = t if merged is None else merged + t
    o_ref[...] = x + m[:, 2 * D_MODEL:3 * D_MODEL] * _bdot(merged, wo_ref[...])


def _merge_out(x, g, mod, y_hy, y_na, y_da, w_gate, w_lift, w_out, l, rows_per_mod):
    T = x.shape[0]
    tm = 512
    per = rows_per_mod // tm
    row = lambda i: (i, 0)
    const2 = lambda i: (0, 0)
    return pl.pallas_call(
        _mid_kernel,
        grid=(T // tm,),
        in_specs=[
            pl.BlockSpec((tm, D_MODEL), row),
            pl.BlockSpec((1, D_MODEL), const2),
            pl.BlockSpec((1, 1, 6 * D_MODEL), lambda i: (i // per, 0, 0)),
            pl.BlockSpec((tm, BRANCH_W), row),
            pl.BlockSpec((tm, BRANCH_W), row),
            pl.BlockSpec((tm, BRANCH_W), row),
            pl.BlockSpec((None, D_MODEL, 3 * D_MODEL), lambda i: (l, 0, 0)),
            pl.BlockSpec((None, 3, BRANCH_W, D_MODEL), lambda i: (l, 0, 0, 0)),
            pl.BlockSpec((None, D_MODEL, D_MODEL), lambda i: (l, 0, 0)),
        ],
        out_specs=pl.BlockSpec((tm, D_MODEL), row),
        out_shape=jax.ShapeDtypeStruct((T, D_MODEL), F32),
        compiler_params=_cparams("arbitrary"),
        name="merge_out",
    )(x, g, mod, y_hy, y_na, y_da, w_gate, w_lift, w_out)


def _ffn_kernel(x_ref, g_ref, mod_ref, w1_ref, w2_ref, gf_ref, o_ref, *, final):
    m = mod_ref[0]
    x = x_ref[...]
    h = _modnorm(x, g_ref[...], m[:, 3 * D_MODEL:4 * D_MODEL], m[:, 4 * D_MODEL:5 * D_MODEL]).astype(BF16)
    a = jnp.dot(h, w1_ref[:, 0:D_FF], preferred_element_type=F32)
    b = jnp.dot(h, w1_ref[:, D_FF:2 * D_FF], preferred_element_type=F32)
    xn = x + m[:, 5 * D_MODEL:6 * D_MODEL] * _bdot(a * _sigmoid(a) * b, w2_ref[...])
    if final:
        xn = _rms(xn, gf_ref[...])
    o_ref[...] = xn


def _ffn(x, g, mod, w_ffn_in, w_ffn_out, g_final, l, rows_per_mod, final):
    T = x.shape[0]
    tm = 512
    per = rows_per_mod // tm
    resident = pl.Buffered(1)
    return pl.pallas_call(
        functools.partial(_ffn_kernel, final=final),
        grid=(T // tm,),
        in_specs=[
            pl.BlockSpec((tm, D_MODEL), lambda i: (i, 0)),
            pl.BlockSpec((1, D_MODEL), lambda i: (0, 0)),
            pl.BlockSpec((1, 1, 6 * D_MODEL), lambda i: (i // per, 0, 0)),
            pl.BlockSpec((None, D_MODEL, 2 * D_FF), lambda i: (l, 0, 0), pipeline_mode=resident),
            pl.BlockSpec((None, D_FF, D_MODEL), lambda i: (l, 0, 0), pipeline_mode=resident),
            pl.BlockSpec((1, D_MODEL), lambda i: (0, 0)),
        ],
        out_specs=pl.BlockSpec((tm, D_MODEL), lambda i: (i, 0)),
        out_shape=jax.ShapeDtypeStruct((T, D_MODEL), F32),
        compiler_params=_cparams("arbitrary"),
        name="ffn",
    )(x, g, mod, w_ffn_in, w_ffn_out, g_final)


def _da_lambda(lam_ref, lam_init):
    lp = lam_ref[...]
    a = jnp.sum(lp[0:1] * lp[1:2], axis=1, keepdims=True)
    b = jnp.sum(lp[2:3] * lp[3:4], axis=1, keepdims=True)
    return jnp.exp(a) - jnp.exp(b) + lam_init


ATT_ONES_ROWS = 16
ATT_TQ = 256
ATT_PREP_T = 512
ATT_KEYS = DEC_SEQ + PAST_LEN
LOG2E = math.log2(math.e)


def _masked_q_blocks(qt, d):
    row = lax.broadcasted_iota(jnp.int32, qt.shape, 0)
    zero = jnp.zeros_like(qt)
    return jnp.concatenate([jnp.where((row >= j * d) & (row < (j + 1) * d), qt, zero) for j in range(LANES // d)],
                           axis=1)


def _colmax(st):
    keys, n = st.shape
    return jnp.max(jnp.max(st.reshape(keys // MXU_DIM, MXU_DIM, n), axis=0), axis=0, keepdims=True)


def _ctx_attn_kernel(nq_ref, nk_ref, nv_ref, dq_ref, dk_ref, dv_ref, lam_ref, sub_ref, yn_ref, yd_ref, acc_ref,
                     *, lam_init):
    lam = _da_lambda(lam_ref, lam_init)
    ones = jnp.ones((ATT_ONES_ROWS, SEQ), BF16)

    def attend(q_ref, k_ref, v_ref, d, maps_per_head, finish):
        qt = (q_ref[...].astype(F32) * (d ** -0.5 * LOG2E)).T.astype(BF16)
        vt = v_ref[...].astype(F32).T.astype(BF16)
        kb = k_ref[...].astype(BF16)
        dv = NA_HEAD_DIM
        heads_per_group = LANES // (d * maps_per_head)
        w = maps_per_head * SEQ
        for g in range(BRANCH_W // LANES):
            lanes = slice(g * LANES, (g + 1) * LANES)
            st = jnp.dot(kb[:, lanes], _masked_q_blocks(qt[lanes], d), preferred_element_type=F32)
            pt = jnp.exp2(st - _colmax(st)).astype(BF16)
            for j in range(heads_per_group):
                h = g * heads_per_group + j
                ve = jnp.concatenate([vt[h * dv:(h + 1) * dv], ones], axis=0)
                oe = jnp.dot(ve, pt[:, j * w:(j + 1) * w], preferred_element_type=F32)
                os = [oe[0:dv, i * SEQ:(i + 1) * SEQ] / oe[dv:dv + 1, i * SEQ:(i + 1) * SEQ]
                      for i in range(maps_per_head)]
                acc_ref[h * dv:(h + 1) * dv, :] = finish(os)

    attend(nq_ref, nk_ref, nv_ref, NA_HEAD_DIM, 1, lambda os: os[0])
    yn_ref[...] = acc_ref[...].T.astype(yn_ref.dtype)

    def da_finish(os):
        ot = os[0] - lam * os[1]
        ot = ot * lax.rsqrt(jnp.mean(ot * ot, axis=0, keepdims=True) + EPS) * sub_ref[...]
        return ot * (1.0 - lam_init)

    attend(dq_ref, dk_ref, dv_ref, DA_HEAD_DIM, 2, da_finish)
    yd_ref[...] = acc_ref[...].T.astype(yd_ref.dtype)


def _ctx_attention(u, da_lambda, subln_col, lam_init):
    col = lambda j: pl.BlockSpec((SEQ, BRANCH_W), lambda b, j=j: (b, j))
    out = pl.BlockSpec((SEQ, BRANCH_W), lambda b: (b, 0))
    shape = jax.ShapeDtypeStruct((BATCH * SEQ, BRANCH_W), BF16)
    return pl.pallas_call(
        functools.partial(_ctx_attn_kernel, lam_init=lam_init),
        grid=(BATCH,),
        in_specs=[col(3), col(4), col(5), col(6), col(7), col(8),
                  pl.BlockSpec((4, DA_HEAD_DIM), lambda b: (0, 0)),
                  pl.BlockSpec((DA_V_DIM, 1), lambda b: (0, 0))],
        out_specs=[out, out],
        out_shape=[shape, shape],
        scratch_shapes=[pltpu.VMEM((BRANCH_W, SEQ), F32)],
        compiler_params=_cparams("arbitrary"),
        name="ctx_attention",
    )(u, u, u, u, u, u, da_lambda, subln_col)


def _rope(x, cos, sin_signed):
    n = x.shape[-1]
    lane = lax.broadcasted_iota(jnp.int32, x.shape, 1)
    partner = jnp.where(lane % 2 == 0, pltpu.roll(x, n - 1, axis=1), pltpu.roll(x, 1, axis=1))
    return x * cos + partner * sin_signed


def _attn_prep_kernel(q_ref, k_ref, v_ref, kc_ref, vc_ref, *refs, rope, scale):
    cos_ref, sin_ref = refs[:2] if rope else (None, None)
    qt_ref, ko_ref, vt_ref = refs[-3:]
    t = pl.program_id(1)
    dv = NA_HEAD_DIM

    def put_v(v):
        n = v.shape[0]
        vt = v.astype(F32).T.astype(BF16)
        ones = jnp.ones((ATT_ONES_ROWS, n), BF16)
        for h in range(BRANCH_W // dv):
            vt_ref[0, h, 0:dv, 0:n] = vt[h * dv:(h + 1) * dv]
            vt_ref[0, h, dv:dv + ATT_ONES_ROWS, 0:n] = ones

    @pl.when(t < DEC_SEQ // ATT_PREP_T)
    def _():
        q = q_ref[...].astype(F32)
        k = k_ref[...].astype(F32)
        if rope:
            q = _rope(q, cos_ref[...], sin_ref[...])
            k = _rope(k, cos_ref[...], sin_ref[...])
        qt_ref[0] = (q * scale).T.astype(BF16)
        ko_ref[0] = k.astype(BF16)
        put_v(v_ref[...])

    @pl.when(t == DEC_SEQ // ATT_PREP_T)
    def _():
        ko_ref[0, 0:PAST_LEN, :] = kc_ref[0].astype(BF16)
        put_v(vc_ref[0])


def _attn_prep(u, first_col, k_ctx, v_ctx, head_dim, rope_tables=None):
    rope = rope_tables is not None
    tile = ATT_PREP_T
    nt = DEC_SEQ // tile
    last = nt - 1
    rowblk = lambda j: pl.BlockSpec((tile, BRANCH_W), lambda b, t, j=j: (b * nt + jnp.minimum(t, last), j))
    tab = pl.BlockSpec((tile, BRANCH_W), lambda b, t: (jnp.minimum(t, last), 0))
    ctx = pl.BlockSpec((1, PAST_LEN, BRANCH_W), lambda b, t: (b, 0, 0))
    heads = BRANCH_W // NA_HEAD_DIM
    vrows = NA_HEAD_DIM + ATT_ONES_ROWS
    return pl.pallas_call(
        functools.partial(_attn_prep_kernel, rope=rope, scale=head_dim ** -0.5 * LOG2E),
        grid=(DEC_BATCH, nt + 1),
        in_specs=[rowblk(first_col), rowblk(first_col + 1), rowblk(first_col + 2), ctx, ctx] + [tab, tab] * rope,
        out_specs=[
            pl.BlockSpec((1, BRANCH_W, tile), lambda b, t: (b, 0, jnp.minimum(t, last))),
            pl.BlockSpec((1, tile, BRANCH_W), lambda b, t: (b, t, 0)),
            pl.BlockSpec((1, heads, vrows, tile), lambda b, t: (b, 0, 0, t)),
        ],
        out_shape=[
            jax.ShapeDtypeStruct((DEC_BATCH, BRANCH_W, DEC_SEQ), BF16),
            jax.ShapeDtypeStruct((DEC_BATCH, ATT_KEYS, BRANCH_W), BF16),
            jax.ShapeDtypeStruct((DEC_BATCH, heads, vrows, ATT_KEYS), BF16),
        ],
        compiler_params=_cparams("arbitrary", "arbitrary"),
        name="attn_prep",
    )(u, u, u, k_ctx, v_ctx, *(rope_tables or ()))


NA_ROWS = ATT_TQ // GRID_W
NA_UNION = 3 * NA_ROWS
NA_STEPS = GRID_H // NA_ROWS
NA_SLABS = NA_UNION // NA_ROWS
NA_VARIANT_OFFSET = (0, -NA_ROWS, -2 * NA_ROWS)


def _na_variant(s):
    return jnp.minimum(s, 1) + s // (NA_STEPS - 1)


def _na_window_block(s):
    return jnp.clip(s - 1, 0, NA_STEPS - NA_SLABS)


def _na_bias_kernel(rpb_ref, o_ref):
    kc = lax.broadcasted_iota(jnp.int32, (GRID_W, GRID_W), 0)
    qc = lax.broadcasted_iota(jnp.int32, (GRID_W, GRID_W), 1)
    dc = jnp.clip(kc - qc, -(NA_WIN_COLS - 1), NA_WIN_COLS - 1) + (NA_WIN_COLS - 1)
    c0 = jnp.clip(qc - NA_WIN_COLS // 2, 0, GRID_W - NA_WIN_COLS)
    col_ok = (kc >= c0) & (kc < c0 + NA_WIN_COLS)
    r = rpb_ref[0, 0] * LOG2E
    masked = jnp.full((GRID_W, GRID_W), NEG_INF, F32)
    tiles = []
    for dr in range(2 * NA_WIN_ROWS - 1):
        acc = jnp.zeros((GRID_W, GRID_W), F32)
        for d in range(2 * NA_WIN_COLS - 1):
            acc = jnp.where(dc == d, r[dr:dr + 1, d:d + 1], acc)
        tiles.append(jnp.where(col_ok, acc, masked))
    for v, off in enumerate(NA_VARIANT_OFFSET):
        for kr in range(NA_UNION):
            for rr in range(NA_ROWS):
                w0 = (0, rr, NA_UNION - NA_WIN_ROWS)[v]
                dr = kr + off - rr
                inside = w0 <= kr < w0 + NA_WIN_ROWS
                o_ref[0, v, 0, kr * GRID_W:(kr + 1) * GRID_W, rr * GRID_W:(rr + 1) * GRID_W] = (
                    tiles[dr + NA_WIN_ROWS - 1] if inside else masked)


def _na_bias_table(na_rpb):
    n_dr, n_dc = 2 * NA_WIN_ROWS - 1, 2 * NA_WIN_COLS - 1
    nv = len(NA_VARIANT_OFFSET)
    return pl.pallas_call(
        _na_bias_kernel,
        grid=(DEPTH, NA_HEADS),
        in_specs=[pl.BlockSpec((1, 1, n_dr, n_dc), lambda l, h: (l, h, 0, 0))],
        out_specs=pl.BlockSpec((1, nv, 1, NA_UNION * GRID_W, ATT_TQ), lambda l, h: (l, 0, h, 0, 0)),
        out_shape=jax.ShapeDtypeStruct((DEPTH, nv, NA_HEADS, NA_UNION * GRID_W, ATT_TQ), F32),
        compiler_params=_cparams("arbitrary", "arbitrary"),
        name="na_bias_table",
    )(na_rpb)


def _na_kernel(qt_ref, *refs):
    n = NA_SLABS + 1
    k_refs, vt_refs = refs[:n], refs[n:2 * n]
    bias_ref, o_ref, acc_ref = refs[2 * n:]
    dv = NA_HEAD_DIM
    heads_per_group = LANES // dv
    for g in range(BRANCH_W // LANES):
        lanes = slice(g * LANES, (g + 1) * LANES)
        qbd = _masked_q_blocks(qt_ref[0, lanes, :], dv)
        sts = []
        for j, k_ref in enumerate(k_refs):
            st = jnp.dot(k_ref[0, :, lanes], qbd, preferred_element_type=F32)
            if j < NA_SLABS:
                rows = slice(j * ATT_TQ, (j + 1) * ATT_TQ)
                st = st + jnp.concatenate(
                    [bias_ref[0, g * heads_per_group + hh, rows, :] for hh in range(heads_per_group)], axis=1)
            sts.append(st)
        mx = functools.reduce(jnp.maximum, [_colmax(st) for st in sts])
        pts = [jnp.exp2(st - mx).astype(BF16) for st in sts]
        for hh in range(heads_per_group):
            h = g * heads_per_group + hh
            oe = sum(jnp.dot(vt_ref[0, h], pt[:, hh * ATT_TQ:(hh + 1) * ATT_TQ], preferred_element_type=F32)
                     for vt_ref, pt in zip(vt_refs, pts))
            acc_ref[h * dv:(h + 1) * dv, :] = oe[0:dv] / oe[dv:dv + 1]
    o_ref[...] = acc_ref[...].T.astype(o_ref.dtype)


def _nbr_attention(qt, k, vt, bias, l):
    vrows = NA_HEAD_DIM + ATT_ONES_ROWS
    ctx_blk = DEC_SEQ // ATT_TQ
    k_specs = [pl.BlockSpec((1, ATT_TQ, BRANCH_W), lambda b, s, j=j: (b, _na_window_block(s) + j, 0))
               for j in range(NA_SLABS)]
    k_specs.append(pl.BlockSpec((1, ATT_TQ, BRANCH_W), lambda b, s: (b, ctx_blk, 0)))
    vt_specs = [pl.BlockSpec((1, NA_HEADS, vrows, ATT_TQ), lambda b, s, j=j: (b, 0, 0, _na_window_block(s) + j))
                for j in range(NA_SLABS)]
    vt_specs.append(pl.BlockSpec((1, NA_HEADS, vrows, ATT_TQ), lambda b, s: (b, 0, 0, ctx_blk)))
    n = NA_SLABS + 1
    return pl.pallas_call(
        _na_kernel,
        grid=(DEC_BATCH, NA_STEPS),
        in_specs=[pl.BlockSpec((1, BRANCH_W, ATT_TQ), lambda b, s: (b, 0, s))] + k_specs + vt_specs + [
            pl.BlockSpec((None, 1, NA_HEADS, NA_UNION * GRID_W, ATT_TQ), lambda b, s: (l, _na_variant(s), 0, 0, 0))],
        out_specs=pl.BlockSpec((ATT_TQ, BRANCH_W), lambda b, s: (b * NA_STEPS + s, 0)),
        out_shape=jax.ShapeDtypeStruct((DEC_BATCH * DEC_SEQ, BRANCH_W), BF16),
        scratch_shapes=[pltpu.VMEM((BRANCH_W, ATT_TQ), F32)],
        compiler_params=_cparams("arbitrary", "arbitrary"),
        name="nbr_attention",
    )(qt, *([k] * n), *([vt] * n), bias)


DA_TQ = ATT_TQ
DA_KEYS = ATT_KEYS
DA_ONES_ROWS = ATT_ONES_ROWS
DA_MAPS_PER_TILE = LANES // DA_HEAD_DIM


def _da_kernel(qt_ref, k_ref, vt_ref, lam_ref, sub_ref, o_ref, acc_ref, *, lam_init):
    lam = _da_lambda(lam_ref, lam_init)
    heads = DA_MAPS_PER_TILE // 2
    for g in range(BRANCH_W // LANES):
        lanes = slice(g * LANES, (g + 1) * LANES)
        st = jnp.dot(k_ref[0, :, lanes], _masked_q_blocks(qt_ref[0, lanes, :], DA_HEAD_DIM),
                     preferred_element_type=F32)
        pt = jnp.exp2(st - _colmax(st)).astype(BF16)
        for hh in range(heads):
            h = g * heads + hh
            oe = jnp.dot(vt_ref[0, h], pt[:, 2 * hh * DA_TQ:(2 * hh + 2) * DA_TQ], preferred_element_type=F32)
            os = [oe[0:DA_V_DIM, i * DA_TQ:(i + 1) * DA_TQ] / oe[DA_V_DIM:DA_V_DIM + 1, i * DA_TQ:(i + 1) * DA_TQ]
                  for i in range(2)]
            ot = os[0] - lam * os[1]
            ot = ot * lax.rsqrt(jnp.mean(ot * ot, axis=0, keepdims=True) + EPS) * sub_ref[...]
            acc_ref[h * DA_V_DIM:(h + 1) * DA_V_DIM, :] = ot * (1.0 - lam_init)
    o_ref[...] = acc_ref[...].T.astype(o_ref.dtype)


def _diff_attention(qt, k, vt, da_lambda, subln_col, lam_init):
    nt = DEC_SEQ // DA_TQ
    vrows = DA_V_DIM + DA_ONES_ROWS
    return pl.pallas_call(
        functools.partial(_da_kernel, lam_init=lam_init),
        grid=(DEC_BATCH, nt),
        in_specs=[
            pl.BlockSpec((1, BRANCH_W, DA_TQ), lambda b, t: (b, 0, t)),
            pl.BlockSpec((1, DA_KEYS, BRANCH_W), lambda b, t: (b, 0, 0)),
            pl.BlockSpec((1, DA_HEADS, vrows, DA_KEYS), lambda b, t: (b, 0, 0, 0)),
            pl.BlockSpec((4, DA_HEAD_DIM), lambda b, t: (0, 0)),
            pl.BlockSpec((DA_V_DIM, 1), lambda b, t: (0, 0)),
        ],
        out_specs=pl.BlockSpec((DA_TQ, BRANCH_W), lambda b, t: (b * nt + t, 0)),
        out_shape=jax.ShapeDtypeStruct((DEC_BATCH * DEC_SEQ, BRANCH_W), BF16),
        scratch_shapes=[pltpu.VMEM((BRANCH_W, DA_TQ), F32)],
        compiler_params=_cparams("arbitrary", "arbitrary"),
        name="diff_attention",
    )(qt, k, vt, da_lambda, subln_col)


def _rope_tables():
    pos = np.arange(DEC_SEQ)
    row = (pos // GRID_W).astype(np.float32)
    col = (pos % GRID_W).astype(np.float32)
    n_freq = DA_HEAD_DIM // 4
    inv = (np.float32(ROPE_BASE) ** (-np.arange(n_freq, dtype=np.float32) / n_freq)).astype(np.float32)
    ang = np.concatenate([row[:, None] * inv[None, :], col[:, None] * inv[None, :]], axis=-1)
    ang = ang.astype(np.float64)
    cos = np.repeat(np.cos(ang), 2, axis=-1)
    sin = np.repeat(np.sin(ang), 2, axis=-1)
    sign = np.where(np.arange(DA_HEAD_DIM) % 2 == 0, -1.0, 1.0)
    reps = BRANCH_W // DA_HEAD_DIM
    cos = np.tile(cos, (1, reps)).astype(np.float32)
    sin = np.tile(sin * sign[None, :], (1, reps)).astype(np.float32)
    return jnp.asarray(cos), jnp.asarray(sin)


def _filt_hidden_kernel(feat_ref, w1_ref, b1_ref, w2_ref, b2_ref, fr_ref, o_ref):
    fr = fr_ref[0]
    h = jnp.sin(fr * (jnp.dot(feat_ref[...], w1_ref[0], precision=HIGHEST, preferred_element_type=F32) + b1_ref[0]))
    o_ref[0] = jnp.sin(fr * (jnp.dot(h, w2_ref[0], precision=HIGHEST, preferred_element_type=F32) + b2_ref[0]))


def _filt_kernel(h_ref, w3f_ref, w3b_ref, dec_ref, o_ref):
    L = dec_ref.shape[0] // 2
    hf = jnp.dot(h_ref[0, 0:L], w3f_ref[0], precision=HIGHEST, preferred_element_type=F32) * dec_ref[0:L]
    hb = jnp.dot(h_ref[0, L:2 * L], w3b_ref[0], precision=HIGHEST, preferred_element_type=F32) * dec_ref[L:2 * L]
    row = lax.broadcasted_iota(jnp.int32, hb.shape, 0)
    hb = jnp.where(row == 0, 0.0, hb)
    nrm = jnp.sum(jnp.abs(hf), axis=0, keepdims=True) + jnp.sum(jnp.abs(hb), axis=0, keepdims=True)
    o_ref[0, 0, 0:L] = hf / nrm
    o_ref[0, 0, L:2 * L] = hb / nrm


def _circular_order(a):
    return np.concatenate([a, a[:1], a[1:][::-1]], axis=0)


def _hyena_pos_tables(L):
    f32 = np.float32
    pos = np.arange(L, dtype=f32)
    t = (pos / f32(L)).astype(f32)
    bands = np.linspace(1e-4, HY_POS_BANDS - 1, HY_POS_BANDS, dtype=f32)
    ang = (f32(2 * math.pi / L) * pos[:, None] * bands[None, :]).astype(np.float64)
    feats = np.zeros((L, HY_FILT_HIDDEN), f32)
    feats[:, 0] = t
    feats[:, 1:1 + HY_POS_BANDS] = np.cos(ang)
    feats[:, 1 + HY_POS_BANDS:HY_POS_DIM] = -np.sin(ang)
    deltas = np.linspace(math.log(HY_DECAY_TARGET) / HY_SLOW_DECAY,
                         math.log(HY_DECAY_TARGET) / HY_FAST_DECAY, BRANCH_W, dtype=f32)
    decay = np.exp((-t[:, None] * np.abs(deltas)[None, :]).astype(np.float64)).astype(f32)
    return jnp.asarray(_circular_order(feats)), jnp.asarray(_circular_order(decay))


def _hyena_filters(half, w1p, b1, w2, b2, w3, freq):
    feats, decay = _hyena_pos_tables(half)
    L = 2 * half
    cb = LANES
    ncb = BRANCH_W // cb
    small = lambda shape: pl.BlockSpec((1,) + shape, lambda l: (l, 0, 0))
    hidden = pl.pallas_call(
        _filt_hidden_kernel,
        grid=(DEPTH,),
        in_specs=[
            pl.BlockSpec((L, HY_FILT_HIDDEN), lambda l: (0, 0)),
            small((HY_FILT_HIDDEN, HY_FILT_HIDDEN)), small((1, HY_FILT_HIDDEN)),
            small((HY_FILT_HIDDEN, HY_FILT_HIDDEN)), small((1, HY_FILT_HIDDEN)),
            small((1, HY_FILT_HIDDEN)),
        ],
        out_specs=pl.BlockSpec((1, L, HY_FILT_HIDDEN), lambda l: (l, 0, 0)),
        out_shape=jax.ShapeDtypeStruct((DEPTH, L, HY_FILT_HIDDEN), F32),
        compiler_params=_cparams("arbitrary"),
        name=f"hyena_filter_hidden_{L}",
    )(feats, w1p, b1, w2, b2, freq)
    return pl.pallas_call(
        _filt_kernel,
        grid=(DEPTH, 2, ncb),
        in_specs=[
            pl.BlockSpec((1, L, HY_FILT_HIDDEN), lambda l, o, c: (l, 0, 0)),
            pl.BlockSpec((1, HY_FILT_HIDDEN, cb), lambda l, o, c: (l, 0, o * 2 * ncb + c)),
            pl.BlockSpec((1, HY_FILT_HIDDEN, cb), lambda l, o, c: (l, 0, o * 2 * ncb + ncb + c)),
            pl.BlockSpec((L, cb), lambda l, o, c: (0, c)),
        ],
        out_specs=pl.BlockSpec((1, 1, L, cb), lambda l, o, c: (l, o, 0, c)),
        out_shape=jax.ShapeDtypeStruct((DEPTH, 2, L, BRANCH_W), F32),
        compiler_params=_cparams("arbitrary", "arbitrary", "arbitrary"),
        name=f"hyena_filters_{L}",
    )(hidden, w3, w3, decay)


def _short_conv(u, w_ref, b_ref, seq_len):
    n = u.shape[0]
    t = lax.broadcasted_iota(jnp.int32, u.shape, 0) % seq_len
    prev = jnp.where(t == 0, 0.0, pltpu.roll(u, 1, axis=0))
    nxt = jnp.where(t == seq_len - 1, 0.0, pltpu.roll(u, n - 1, axis=0))
    return prev * w_ref[0:1, :] + u * w_ref[1:2, :] + nxt * w_ref[2:3, :] + b_ref[...]


def _dft_direct_mats():
    n, half = 2 * SEQ, SEQ
    k = np.arange(n)[:, None].astype(np.float64)
    t = np.arange(half)[None, :].astype(np.float64)
    ang = 2 * np.pi * k * t / n
    fr, fi = np.cos(ang), -np.sin(ang)
    mf = np.block([[fr, -fi], [fi, fr]])
    gr, gi = np.cos(ang).T / n, np.sin(ang).T / n
    mi = np.block([[gr, -gi], [gi, gr]])
    return mf.astype(np.float32), mi.astype(np.float32)


def _dft_real_mat():
    n = 2 * SEQ
    ang = 2 * np.pi * np.arange(n)[:, None].astype(np.float64) * np.arange(n)[None, :] / n
    return np.concatenate([np.cos(ang), -np.sin(ang)], axis=0).astype(np.float32)


def _spec_direct_kernel(h_ref, m_ref, o_ref):
    o_ref[0, 0] = jnp.dot(m_ref[...], h_ref[0, 0], precision=HIGHEST, preferred_element_type=F32)


def _spec_direct(h, m_real):
    n = 2 * SEQ
    return pl.pallas_call(
        _spec_direct_kernel,
        grid=(DEPTH, 2),
        in_specs=[pl.BlockSpec((1, 1, n, BRANCH_W), lambda l, o: (l, o, 0, 0)),
                  pl.BlockSpec((2 * n, n), lambda l, o: (0, 0))],
        out_specs=pl.BlockSpec((1, 1, 2 * n, BRANCH_W), lambda l, o: (l, o, 0, 0)),
        out_shape=jax.ShapeDtypeStruct((DEPTH, 2, 2 * n, BRANCH_W), F32),
        compiler_params=_cparams("arbitrary", "arbitrary"),
        name="hyena_spectrum_direct",
    )(h, m_real)


def _lconv_direct_kernel(s_ref, g_ref, cws_ref, cbs_ref, cwg_ref, cbg_ref, h_ref, bias_ref, mf_ref, mi_ref, o_ref,
                         *, conv_sig):
    n = 2 * SEQ
    sig = s_ref[...].astype(F32)
    if conv_sig:
        sig = _short_conv(sig, cws_ref, cbs_ref, SEQ)
    gate = _short_conv(g_ref[...].astype(F32), cwg_ref, cbg_ref, SEQ)
    z = jnp.dot(mf_ref[...], sig.astype(BF16), preferred_element_type=F32)
    zr, zi = z[0:n], z[n:2 * n]
    hr, hi = h_ref[0:n], h_ref[n:2 * n]
    y = jnp.concatenate([zr * hr - zi * hi, zr * hi + zi * hr], axis=0)
    y = jnp.dot(mi_ref[...], y.astype(BF16), preferred_element_type=F32)
    o_ref[...] = gate * (y + sig * bias_ref[...])


def _lconv_direct(sig, sig_col, gate_src, gate_col, conv_w, conv_b, spec, l, order, bias, mf, mi, conv_sig):
    n = 2 * SEQ
    rows = 2 * SEQ
    T = sig.shape[0]
    return pl.pallas_call(
        functools.partial(_lconv_direct_kernel, conv_sig=conv_sig),
        grid=(T // rows,),
        in_specs=[
            pl.BlockSpec((rows, BRANCH_W), lambda p: (p, sig_col)),
            pl.BlockSpec((rows, BRANCH_W), lambda p: (p, gate_col)),
            pl.BlockSpec((3, BRANCH_W), lambda p: (0, 0)),
            pl.BlockSpec((1, BRANCH_W), lambda p: (0, 0)),
            pl.BlockSpec((3, BRANCH_W), lambda p: (0, gate_col)),
            pl.BlockSpec((1, BRANCH_W), lambda p: (0, gate_col)),
            pl.BlockSpec((None, None, 2 * n, BRANCH_W), lambda p: (l, order, 0, 0)),
            pl.BlockSpec((1, BRANCH_W), lambda p: (0, 0)),
            pl.BlockSpec((2 * n, rows), lambda p: (0, 0)),
            pl.BlockSpec((rows, 2 * n), lambda p: (0, 0)),
        ],
        out_specs=pl.BlockSpec((rows, BRANCH_W), lambda p: (p, 0)),
        out_shape=jax.ShapeDtypeStruct((T, BRANCH_W), F32),
        compiler_params=_cparams("arbitrary"),
        name="hyena_lconv_direct",
    )(sig, gate_src, conv_w, conv_b, conv_w, conv_b, spec, bias, mf, mi)


def _dft_two_stage_mats():
    no, ni, half, n = FFT_NO, FFT_NI, FFT_HALF, FFT_N
    f64 = np.float64
    k1 = np.arange(no, dtype=f64)
    n_o = np.arange(half, dtype=f64)
    n_i = np.arange(ni, dtype=f64)
    ang = 2 * np.pi * (n_i[:, None, None] * k1[None, :, None] / n + k1[None, :, None] * n_o[None, None, :] / no)
    tr, ti = np.cos(ang), -np.sin(ang)
    m1 = np.concatenate([np.concatenate([tr, -ti], axis=2), np.concatenate([ti, tr], axis=2)], axis=1)
    k2 = np.arange(ni, dtype=f64)
    ang2 = 2 * np.pi * k2[:, None] * n_i[None, :] / ni
    f2r, f2i = np.cos(ang2), -np.sin(ang2)
    m2 = np.block([[f2r, -f2i], [f2i, f2r]])
    m2c = np.block([[f2r, f2i], [-f2i, f2r]])
    sr, si = np.transpose(tr, (0, 2, 1)) / n, -np.transpose(ti, (0, 2, 1)) / n
    m3 = np.concatenate([np.concatenate([sr, -si], axis=2), np.concatenate([si, sr], axis=2)], axis=1)
    return (m1.astype(np.float32), m2.astype(np.float32), m2c.astype(np.float32), m3.astype(np.float32))


def _dft_stage1_real_mat():
    no, ni, n = FFT_NO, FFT_NI, FFT_N
    k1 = np.arange(no, dtype=np.float64)
    n_o = np.arange(no, dtype=np.float64)
    n_i = np.arange(ni, dtype=np.float64)
    ang = 2 * np.pi * (n_i[:, None, None] * k1[None, :, None] / n + k1[None, :, None] * n_o[None, None, :] / no)
    return np.concatenate([np.cos(ang), -np.sin(ang)], axis=1).astype(np.float32)


def _store_stage1(w_ref, ni, out):
    w_ref[pl.ds(ni, FFT_NO, stride=2 * FFT_NI), :] = out[0:FFT_NO]
    w_ref[pl.ds(FFT_NI + ni, FFT_NO, stride=2 * FFT_NI), :] = out[FFT_NO:2 * FFT_NO]


def _fwd_stage1(za_ref, zb_ref, m1_ref, w_ref):
    def body(ni, carry):
        a = za_ref[pl.ds(ni, FFT_HALF, stride=FFT_NI), :]
        b = zb_ref[pl.ds(ni, FFT_HALF, stride=FFT_NI), :]
        out = jnp.dot(m1_ref[ni], jnp.concatenate([a, b], axis=0).astype(BF16), preferred_element_type=F32)
        _store_stage1(w_ref, ni, out)
        return carry

    lax.fori_loop(0, FFT_NI, body, 0, unroll=FFT_UNROLL)


def _spec_two_stage_kernel(h_ref, m1_ref, m2_ref, o_ref, w_ref):
    h = h_ref.at[0, 0]

    def stage1(ni, carry):
        a = h[pl.ds(ni, FFT_NO, stride=FFT_NI), :]
        _store_stage1(w_ref, ni, jnp.dot(m1_ref[ni], a.astype(BF16), preferred_element_type=F32))
        return carry

    lax.fori_loop(0, FFT_NI, stage1, 0, unroll=FFT_UNROLL)
    blk = 2 * FFT_NI

    cb = w_ref.shape[1]

    def stage2(kp, carry):
        rows = [pl.ds(pl.multiple_of((2 * kp + j) * blk, blk), blk) for j in range(2)]
        x = jnp.dot(m2_ref[...], jnp.concatenate([w_ref[r, :] for r in rows], axis=1).astype(BF16),
                    preferred_element_type=F32)
        for j in range(2):
            o_ref[0, 0, rows[j], :] = x[:, j * cb:(j + 1) * cb]
        return carry

    lax.fori_loop(0, FFT_NO // 2, stage2, 0, unroll=FFT_MID_UNROLL)


def _spec_two_stage(h, m1_real, m2):
    cb = LCONV_CB
    return pl.pallas_call(
        _spec_two_stage_kernel,
        grid=(DEPTH, 2, BRANCH_W // cb),
        in_specs=[pl.BlockSpec((1, 1, FFT_N, cb), lambda l, o, c: (l, o, 0, c)),
                  pl.BlockSpec((FFT_NI, 2 * FFT_NO, FFT_NO), lambda l, o, c: (0, 0, 0)),
                  pl.BlockSpec((2 * FFT_NI, 2 * FFT_NI), lambda l, o, c: (0, 0))],
        out_specs=pl.BlockSpec((1, 1, 2 * FFT_N, cb), lambda l, o, c: (l, o, 0, c)),
        out_shape=jax.ShapeDtypeStruct((DEPTH, 2, 2 * FFT_N, BRANCH_W), F32),
        scratch_shapes=[pltpu.VMEM((2 * FFT_N, cb), F32)],
        compiler_params=_cparams("arbitrary", "arbitrary", "arbitrary"),
        name="hyena_spectrum_two_stage",
    )(h, m1_real, m2)


def _lconv_two_stage_kernel(s_ref, g_ref, cws_ref, cbs_ref, cwg_ref, cbg_ref, h_ref, bias_ref,
                            m1_ref, m2_ref, m2c_ref, m3_ref, o_ref, z_ref, w_ref, *, conv_sig):
    for b in range(2):
        sig = s_ref[b].astype(F32)
        if conv_sig:
            sig = _short_conv(sig, cws_ref, cbs_ref, DEC_SEQ)
        z_ref[b] = sig
    _fwd_stage1(z_ref.at[0], z_ref.at[1], m1_ref, w_ref)
    blk = 2 * FFT_NI

    cb = w_ref.shape[1]

    def mid(kp, carry):
        rows = [pl.ds(pl.multiple_of((2 * kp + j) * blk, blk), blk) for j in range(2)]
        x = jnp.dot(m2_ref[...], jnp.concatenate([w_ref[r, :] for r in rows], axis=1).astype(BF16),
                    preferred_element_type=F32)
        h = jnp.concatenate([h_ref[r, :] for r in rows], axis=1)
        xr, xi = x[0:FFT_NI], x[FFT_NI:blk]
        hr, hi = h[0:FFT_NI], h[FFT_NI:blk]
        y = jnp.concatenate([xr * hr - xi * hi, xr * hi + xi * hr], axis=0)
        c = jnp.dot(m2c_ref[...], y.astype(BF16), preferred_element_type=F32)
        for j in range(2):
            w_ref[rows[j], :] = c[:, j * cb:(j + 1) * cb]
        return carry

    lax.fori_loop(0, FFT_NO // 2, mid, 0, unroll=FFT_MID_UNROLL)

    def last(ni, carry):
        cr = w_ref[pl.ds(ni, FFT_NO, stride=blk), :]
        ci = w_ref[pl.ds(FFT_NI + ni, FFT_NO, stride=blk), :]
        y = jnp.dot(m3_ref[ni], jnp.concatenate([cr, ci], axis=0).astype(BF16), preferred_element_type=F32)
        o_ref[0, pl.ds(ni, FFT_HALF, stride=FFT_NI), :] = y[0:FFT_HALF]
        o_ref[1, pl.ds(ni, FFT_HALF, stride=FFT_NI), :] = y[FFT_HALF:2 * FFT_HALF]
        return carry

    lax.fori_loop(0, FFT_NI, last, 0, unroll=FFT_UNROLL)
    for b in range(2):
        gate = _short_conv(g_ref[b].astype(F32), cwg_ref, cbg_ref, DEC_SEQ)
        sig = z_ref[b]
        o_ref[b] = gate * (o_ref[b] + sig * bias_ref[...])


def _lconv_two_stage(sig, sig_col, gate_src, gate_col, conv_w, conv_b, spec, l, order, bias, mats, conv_sig):
    cb = LCONV_CB
    ncb = BRANCH_W // cb
    m1, m2, m2c, m3 = mats
    const3 = lambda c, p: (0, 0, 0)
    const2 = lambda c, p: (0, 0)
    return pl.pallas_call(
        functools.partial(_lconv_two_stage_kernel, conv_sig=conv_sig),
        grid=(ncb, DEC_BATCH // 2),
        in_specs=[
            pl.BlockSpec((2, DEC_SEQ, cb), lambda c, p: (p, 0, sig_col * ncb + c)),
            pl.BlockSpec((2, DEC_SEQ, cb), lambda c, p: (p, 0, gate_col * ncb + c)),
            pl.BlockSpec((3, cb), lambda c, p: (0, c)),
            pl.BlockSpec((1, cb), lambda c, p: (0, c)),
            pl.BlockSpec((3, cb), lambda c, p: (0, gate_col * ncb + c)),
            pl.BlockSpec((1, cb), lambda c, p: (0, gate_col * ncb + c)),
            pl.BlockSpec((None, None, 2 * FFT_N, cb), lambda c, p: (l, order, 0, c)),
            pl.BlockSpec((1, cb), lambda c, p: (0, c)),
            pl.BlockSpec(m1.shape, const3),
            pl.BlockSpec(m2.shape, const2),
            pl.BlockSpec(m2c.shape, const2),
            pl.BlockSpec(m3.shape, const3),
        ],
        out_specs=pl.BlockSpec((2, DEC_SEQ, cb), lambda c, p: (p, 0, c)),
        out_shape=jax.ShapeDtypeStruct((DEC_BATCH, DEC_SEQ, BRANCH_W), F32),
        scratch_shapes=[pltpu.VMEM((2, DEC_SEQ, cb), F32), pltpu.VMEM((2 * FFT_N, cb), F32)],
        compiler_params=_cparams("arbitrary", "arbitrary"),
        name="hyena_lconv_two_stage",
    )(sig, gate_src, conv_w, conv_b, conv_w, conv_b, spec, bias, m1, m2, m2c, m3)


def kernel(x_prompt, x_sample, cache_na_k, cache_na_v, cache_da_k, cache_da_v, c, c_ctx, w_ada, b_ada, norm_mix,
           norm_ffn, w_in, hy_conv_w, hy_conv_b, hy_filt_w1, hy_filt_b1, hy_filt_w2, hy_filt_b2, hy_filt_w3,
           hy_filt_freq, hy_bias, na_rpb, da_lambda, da_subln, w_lift, w_out, w_ffn_in, w_ffn_out, norm_final):
    TP, TS = BATCH * SEQ, DEC_BATCH * DEC_SEQ
    xp = x_prompt.reshape(TP, D_MODEL)
    xs = x_sample.reshape(TS, D_MODEL)

    cc = jnp.concatenate([c_ctx[None, :], c, jnp.zeros((8 - 1 - DEC_BATCH, D_MODEL), F32)], axis=0)
    mod = _modulation(cc, w_ada, b_ada)
    mod_p = mod[:, 0:1].reshape(DEPTH, 1, 1, 6 * D_MODEL)
    mod_s = mod[:, 1:1 + DEC_BATCH].reshape(DEPTH, DEC_BATCH, 1, 6 * D_MODEL)

    w_mix = w_in[:, :, :MIX_W].astype(BF16)
    w_gate = w_in[:, :, MIX_W:].astype(BF16)
    w_lift_b = w_lift.astype(BF16)
    w_out_b = w_out.astype(BF16)
    w_ffn_in_b = w_ffn_in.astype(BF16)
    w_ffn_out_b = w_ffn_out.astype(BF16)
    g_mix = norm_mix.reshape(DEPTH, 1, D_MODEL)
    g_ffn = norm_ffn.reshape(DEPTH, 1, D_MODEL)
    g_fin = norm_final.reshape(1, D_MODEL)
    subln = da_subln.reshape(DEPTH, 1, DA_V_DIM)
    subln_col = da_subln.reshape(DEPTH, DA_V_DIM, 1)

    w1p = jnp.pad(hy_filt_w1, ((0, 0), (0, HY_FILT_HIDDEN - HY_POS_DIM), (0, 0)))
    b1 = hy_filt_b1.reshape(DEPTH, 1, HY_FILT_HIDDEN)
    b2 = hy_filt_b2.reshape(DEPTH, 1, HY_FILT_HIDDEN)
    fr = hy_filt_freq.reshape(DEPTH, 1, HY_FILT_HIDDEN)
    mf, mi = _dft_direct_mats()
    mats = _dft_two_stage_mats()
    h_p = _hyena_filters(SEQ, w1p, b1, hy_filt_w2, b2, hy_filt_w3, fr)
    h_s = _hyena_filters(DEC_SEQ, w1p, b1, hy_filt_w2, b2, hy_filt_w3, fr)
    spec_p = _spec_direct(h_p, jnp.asarray(_dft_real_mat()))
    mf_b, mi_b = jnp.asarray(mf, dtype=BF16), jnp.asarray(mi, dtype=BF16)
    mats_b = tuple(jnp.asarray(m, dtype=BF16) for m in mats)
    spec_s = _spec_two_stage(h_s, jnp.asarray(_dft_stage1_real_mat(), dtype=BF16), mats_b[1])
    conv_b = hy_conv_b.reshape(DEPTH, 1, 3 * BRANCH_W)

    na_bias = _na_bias_table(na_rpb)
    rope_tables = _rope_tables()
    ck_na = cache_na_k.reshape(DEC_BATCH, DEPTH, PAST_LEN, BRANCH_W)
    cv_na = cache_na_v.reshape(DEC_BATCH, DEPTH, PAST_LEN, BRANCH_W)
    ck_da = cache_da_k.reshape(DEC_BATCH, DEPTH, PAST_LEN, BRANCH_W)
    cv_da = cache_da_v.reshape(DEC_BATCH, DEPTH, PAST_LEN, BRANCH_W)

    caches = tuple(jnp.zeros((BATCH, DEPTH, SEQ, BRANCH_W), F32) for _ in CACHE_BLOCKS)
    for l in range(DEPTH):
        lam_init = 0.8 - 0.6 * math.exp(-0.3 * l)
        final = l == DEPTH - 1

        u, caches = _in_proj(xp, g_mix[l], mod_p[l], w_mix, l, TP, BF16, caches=caches)
        z1 = _lconv_direct(u, 0, u, 1, hy_conv_w[l], conv_b[l], spec_p, l, 0, hy_bias[l, 0:1], mf_b, mi_b, True)
        y_hy = _lconv_direct(z1, 0, u, 2, hy_conv_w[l], conv_b[l], spec_p, l, 1, hy_bias[l, 1:2], mf_b, mi_b, False)
        y_na, y_da = _ctx_attention(u, da_lambda[l], subln_col[l], lam_init)
        xp = _merge_out(xp, g_mix[l], mod_p[l], y_hy, y_na, y_da, w_gate, w_lift_b, w_out_b, l, TP)
        xp = _ffn(xp, g_ffn[l], mod_p[l], w_ffn_in_b, w_ffn_out_b, g_fin, l, TP, final)

        u = _in_proj(xs, g_mix[l], mod_s[l], w_mix, l, DEC_SEQ, BF16)
        u3 = u.reshape(DEC_BATCH, DEC_SEQ, MIX_W)
        z1 = _lconv_two_stage(u3, 0, u3, 1, hy_conv_w[l], conv_b[l], spec_s, l, 0, hy_bias[l, 0:1], mats_b, True)
        y_hy = _lconv_two_stage(z1, 0, u3, 2, hy_conv_w[l], conv_b[l], spec_s, l, 1, hy_bias[l, 1:2], mats_b, False)
        y_hy = y_hy.reshape(TS, BRANCH_W)
        qn, kn, vn = _attn_prep(u, 3, ck_na[:, l], cv_na[:, l], NA_HEAD_DIM)
        y_na = _nbr_attention(qn, kn, vn, na_bias, l)
        q, kt, v = _attn_prep(u, 6, ck_da[:, l], cv_da[:, l], DA_HEAD_DIM, rope_tables)
        y_da = _diff_attention(q, kt, v, da_lambda[l], subln_col[l], lam_init)
        xs = _merge_out(xs, g_mix[l], mod_s[l], y_hy, y_na, y_da, w_gate, w_lift_b, w_out_b, l, DEC_SEQ)
        xs = _ffn(xs, g_ffn[l], mod_s[l], w_ffn_in_b, w_ffn_out_b, g_fin, l, DEC_SEQ, final)

    y_prompt = xp.reshape(BATCH, SEQ, D_MODEL)
    y_sample = xs.reshape(DEC_BATCH, DEC_SEQ, D_MODEL)
    heads = lambda a, d: a.reshape(BATCH, DEPTH, SEQ, BRANCH_W // d, d)
    return (y_prompt, y_sample, heads(caches[0], NA_HEAD_DIM), heads(caches[1], NA_HEAD_DIM),
            heads(caches[2], 2 * DA_HEAD_DIM), heads(caches[3], DA_V_DIM))
```

```python
import functools
import math

import numpy as np
import jax
import jax.numpy as jnp
from jax import lax
from jax.experimental import pallas as pl
from jax.experimental.pallas import tpu as pltpu

F32 = jnp.float32
BF16 = jnp.bfloat16
HIGHEST = lax.Precision.HIGHEST

D_MODEL = 1024
BATCH = 32
SEQ = 256
DEPTH = 4
DEC_BATCH = 4
DEC_SEQ = 4096
PAST_LEN = 256
GRID_W = 64
GRID_H = DEC_SEQ // GRID_W
BRANCH_W = 512
HY_POS_BANDS = 16
HY_POS_DIM = 1 + 2 * HY_POS_BANDS
HY_FILT_HIDDEN = 64
HY_DECAY_TARGET = 1e-2
HY_FAST_DECAY = 0.3
HY_SLOW_DECAY = 1.5
NA_HEADS = 8
NA_HEAD_DIM = 64
NA_WIN_ROWS = 8
NA_WIN_COLS = 16
DA_HEADS = 8
DA_HEAD_DIM = 32
DA_V_DIM = 64
D_FF = 2816
MIX_W = 9 * BRANCH_W
ROPE_BASE = 10000.0
EPS = 1e-6
NEG_INF = -1e30

VMEM_LIMIT_BYTES = 56 * 1024 * 1024
LANES = 128
MXU_DIM = 256

FFT_N = 2 * DEC_SEQ
FFT_NO = 64
FFT_NI = 128
FFT_HALF = FFT_NO // 2
FFT_UNROLL = 8
FFT_MID_UNROLL = 16
LCONV_CB = LANES


def _cparams(*sem):
    return pltpu.CompilerParams(dimension_semantics=sem, vmem_limit_bytes=VMEM_LIMIT_BYTES)


def _sigmoid(x):
    return 1.0 / (1.0 + jnp.exp(-x))


def _rms(x, g):
    return x * lax.rsqrt(jnp.mean(x * x, axis=-1, keepdims=True) + EPS) * g


def _modnorm(x, g, shift, scale):
    return _rms(x, g) * (1.0 + scale) + shift


def _bdot(a, b):
    return jnp.dot(a.astype(BF16), b.astype(BF16), preferred_element_type=F32)


def _mod_kernel(c_ref, w_ref, b_ref, o_ref):
    c = c_ref[...]
    s = c * _sigmoid(c)
    o_ref[0] = jnp.dot(s, w_ref[0], precision=HIGHEST, preferred_element_type=F32) + b_ref[0]


def _modulation(cc, w_ada, b_ada):
    nt = 6
    return pl.pallas_call(
        _mod_kernel,
        grid=(DEPTH, nt),
        in_specs=[
            pl.BlockSpec((8, D_MODEL), lambda l, j: (0, 0)),
            pl.BlockSpec((1, D_MODEL, D_MODEL), lambda l, j: (l, 0, j)),
            pl.BlockSpec((1, 1, D_MODEL), lambda l, j: (l, 0, j)),
        ],
        out_specs=pl.BlockSpec((1, 8, D_MODEL), lambda l, j: (l, 0, j)),
        out_shape=jax.ShapeDtypeStruct((DEPTH, 8, 6 * D_MODEL), F32),
        compiler_params=_cparams("arbitrary", "arbitrary"),
        name="modulation",
    )(cc, w_ada, b_ada.reshape(DEPTH, 1, 6 * D_MODEL))


IN_TM = 512
CACHE_BLOCKS = (4, 5, 7, 8)


def _in_kernel(*refs, n_cache):
    x_ref, g_ref, mod_ref, w_ref = refs[:4]
    o_ref = refs[4 + n_cache]
    cache_refs = refs[5 + n_cache:]
    m = mod_ref[0]
    h = _modnorm(x_ref[...], g_ref[...], m[:, 0:D_MODEL], m[:, D_MODEL:2 * D_MODEL]).astype(BF16)
    res = jnp.dot(h, w_ref[...], preferred_element_type=F32)
    o_ref[...] = res.astype(o_ref.dtype)
    for c, c_ref in zip(CACHE_BLOCKS, cache_refs):
        c_ref[...] = res[:, c * BRANCH_W:(c + 1) * BRANCH_W].reshape(c_ref.shape)


def _in_proj(x, g, mod, w, l, rows_per_mod, out_dtype, caches=None):
    T = x.shape[0]
    tm = IN_TM
    per = rows_per_mod // tm
    in_specs = [
        pl.BlockSpec((tm, D_MODEL), lambda i: (i, 0)),
        pl.BlockSpec((1, D_MODEL), lambda i: (0, 0)),
        pl.BlockSpec((1, 1, 6 * D_MODEL), lambda i: (i // per, 0, 0)),
        pl.BlockSpec((None, D_MODEL, MIX_W), lambda i: (l, 0, 0), pipeline_mode=pl.Buffered(1)),
    ]
    out_specs = [pl.BlockSpec((tm, MIX_W), lambda i: (i, 0))]
    out_shape = [jax.ShapeDtypeStruct((T, MIX_W), out_dtype)]
    args = [x, g, mod, w]
    aliases = {}
    if caches is not None:
        out_specs += [pl.BlockSpec((tm // SEQ, 1, SEQ, BRANCH_W), lambda i: (i, l, 0, 0))] * len(caches)
        out_shape += [jax.ShapeDtypeStruct(c.shape, c.dtype) for c in caches]
        in_specs += [pl.BlockSpec(memory_space=pl.ANY)] * len(caches)
        aliases = {4 + n: 1 + n for n in range(len(caches))}
        args += list(caches)
    outs = pl.pallas_call(
        functools.partial(_in_kernel, n_cache=len(args) - 4),
        grid=(T // tm,),
        in_specs=in_specs,
        out_specs=out_specs,
        out_shape=out_shape,
        input_output_aliases=aliases,
        compiler_params=_cparams("arbitrary"),
        name="in_proj",
    )(*args)
    return outs[0] if caches is None else (outs[0], tuple(outs[1:]))


def _mid_kernel(x_ref, g_ref, mod_ref, yh_ref, yn_ref, yd_ref, wg_ref, wl_ref, wo_ref, o_ref):
    m = mod_ref[0]
    x = x_ref[...]
    h = _modnorm(x, g_ref[...], m[:, 0:D_MODEL], m[:, D_MODEL:2 * D_MODEL]).astype(BF16)
    merged = None
    for br, y_ref in enumerate((yh_ref, yn_ref, yd_ref)):
        gate = _sigmoid(jnp.dot(h, wg_ref[:, br * D_MODEL:(br + 1) * D_MODEL], preferred_element_type=F32))
        lift = jnp.dot(y_ref[...].astype(BF16), wl_ref[br], preferred_element_type=F32)
        t = gate * lift
        merged = t if merged is None else merged + t
    o_ref[...] = x + m[:, 2 * D_MODEL:3 * D_MODEL] * _bdot(merged, wo_ref[...])


def _merge_out(x, g, mod, y_hy, y_na, y_da, w_gate, w_lift, w_out, l, rows_per_mod):
    T = x.shape[0]
    tm = 512
    per = rows_per_mod // tm
    row = lambda i: (i, 0)
    const2 = lambda i: (0, 0)
    return pl.pallas_call(
        _mid_kernel,
        grid=(T // tm,),
        in_specs=[
            pl.BlockSpec((tm, D_MODEL), row),
            pl.BlockSpec((1, D_MODEL), const2),
            pl.BlockSpec((1, 1, 6 * D_MODEL), lambda i: (i // per, 0, 0)),
            pl.BlockSpec((tm, BRANCH_W), row),
            pl.BlockSpec((tm, BRANCH_W), row),
            pl.BlockSpec((tm, BRANCH_W), row),
            pl.BlockSpec((None, D_MODEL, 3 * D_MODEL), lambda i: (l, 0, 0)),
            pl.BlockSpec((None, 3, BRANCH_W, D_MODEL), lambda i: (l, 0, 0, 0)),
            pl.BlockSpec((None, D_MODEL, D_MODEL), lambda i: (l, 0, 0)),
        ],
        out_specs=pl.BlockSpec((tm, D_MODEL), row),
        out_shape=jax.ShapeDtypeStruct((T, D_MODEL), F32),
        compiler_params=_cparams("arbitrary"),
        name="merge_out",
    )(x, g, mod, y_hy, y_na, y_da, w_gate, w_lift, w_out)


def _ffn_kernel(x_ref, g_ref, mod_ref, w1_ref, w2_ref, gf_ref, o_ref, *, final):
    m = mod_ref[0]
    x = x_ref[...]
    h = _modnorm(x, g_ref[...], m[:, 3 * D_MODEL:4 * D_MODEL], m[:, 4 * D_MODEL:5 * D_MODEL]).astype(BF16)
    a = jnp.dot(h, w1_ref[:, 0:D_FF], preferred_element_type=F32)
    b = jnp.dot(h, w1_ref[:, D_FF:2 * D_FF], preferred_element_type=F32)
    xn = x + m[:, 5 * D_MODEL:6 * D_MODEL] * _bdot(a * _sigmoid(a) * b, w2_ref[...])
    if final:
        xn = _rms(xn, gf_ref[...])
    o_ref[...] = xn


def _ffn(x, g, mod, w_ffn_in, w_ffn_out, g_final, l, rows_per_mod, final):
    T = x.shape[0]
    tm = 512
    per = rows_per_mod // tm
    resident = pl.Buffered(1)
    return pl.pallas_call(
        functools.partial(_ffn_kernel, final=final),
        grid=(T // tm,),
        in_specs=[
            pl.BlockSpec((tm, D_MODEL), lambda i: (i, 0)),
            pl.BlockSpec((1, D_MODEL), lambda i: (0, 0)),
            pl.BlockSpec((1, 1, 6 * D_MODEL), lambda i: (i // per, 0, 0)),
            pl.BlockSpec((None, D_MODEL, 2 * D_FF), lambda i: (l, 0, 0), pipeline_mode=resident),
            pl.BlockSpec((None, D_FF, D_MODEL), lambda i: (l, 0, 0), pipeline_mode=resident),
            pl.BlockSpec((1, D_MODEL), lambda i: (0, 0)),
        ],
        out_specs=pl.BlockSpec((tm, D_MODEL), lambda i: (i, 0)),
        out_shape=jax.ShapeDtypeStruct((T, D_MODEL), F32),
        compiler_params=_cparams("arbitrary"),
        name="ffn",
    )(x, g, mod, w_ffn_in, w_ffn_out, g_final)


def _da_lambda(lam_ref, lam_init):
    lp = lam_ref[...]
    a = jnp.sum(lp[0:1] * lp[1:2], axis=1, keepdims=True)
    b = jnp.sum(lp[2:3] * lp[3:4], axis=1, keepdims=True)
    return jnp.exp(a) - jnp.exp(b) + lam_init


ATT_ONES_ROWS = 16
ATT_TQ = 256
ATT_PREP_T = 512
ATT_KEYS = DEC_SEQ + PAST_LEN
LOG2E = math.log2(math.e)


def _masked_q_blocks(qt, d):
    row = lax.broadcasted_iota(jnp.int32, qt.shape, 0)
    zero = jnp.zeros_like(qt)
    return jnp.concatenate([jnp.where((row >= j * d) & (row < (j + 1) * d), qt, zero) for j in range(LANES // d)],
                           axis=1)


def _colmax(st):
    keys, n = st.shape
    return jnp.max(jnp.max(st.reshape(keys // MXU_DIM, MXU_DIM, n), axis=0), axis=0, keepdims=True)


def _ctx_attn_kernel(nq_ref, nk_ref, nv_ref, dq_ref, dk_ref, dv_ref, lam_ref, sub_ref, yn_ref, yd_ref, acc_ref,
                     *, lam_init):
    lam = _da_lambda(lam_ref, lam_init)
    ones = jnp.ones((ATT_ONES_ROWS, SEQ), BF16)

    def attend(q_ref, k_ref, v_ref, d, maps_per_head, finish):
        qt = (q_ref[...].astype(F32) * (d ** -0.5 * LOG2E)).T.astype(BF16)
        vt = v_ref[...].astype(F32).T.astype(BF16)
        kb = k_ref[...].astype(BF16)
        dv = NA_HEAD_DIM
        heads_per_group = LANES // (d * maps_per_head)
        w = maps_per_head * SEQ
        for g in range(BRANCH_W // LANES):
            lanes = slice(g * LANES, (g + 1) * LANES)
            st = jnp.dot(kb[:, lanes], _masked_q_blocks(qt[lanes], d), preferred_element_type=F32)
            pt = jnp.exp2(st - _colmax(st)).astype(BF16)
            for j in range(heads_per_group):
                h = g * heads_per_group + j
                ve = jnp.concatenate([vt[h * dv:(h + 1) * dv], ones], axis=0)
                oe = jnp.dot(ve, pt[:, j * w:(j + 1) * w], preferred_element_type=F32)
                os = [oe[0:dv, i * SEQ:(i + 1) * SEQ] / oe[dv:dv + 1, i * SEQ:(i + 1) * SEQ]
                      for i in range(maps_per_head)]
                acc_ref[h * dv:(h + 1) * dv, :] = finish(os)

    attend(nq_ref, nk_ref, nv_ref, NA_HEAD_DIM, 1, lambda os: os[0])
    yn_ref[...] = acc_ref[...].T.astype(yn_ref.dtype)

    def da_finish(os):
        ot = os[0] - lam * os[1]
        ot = ot * lax.rsqrt(jnp.mean(ot * ot, axis=0, keepdims=True) + EPS) * sub_ref[...]
        return ot * (1.0 - lam_init)

    attend(dq_ref, dk_ref, dv_ref, DA_HEAD_DIM, 2, da_finish)
    yd_ref[...] = acc_ref[...].T.astype(yd_ref.dtype)


def _ctx_attention(u, da_lambda, subln_col, lam_init):
    col = lambda j: pl.BlockSpec((SEQ, BRANCH_W), lambda b, j=j: (b, j))
    out = pl.BlockSpec((SEQ, BRANCH_W), lambda b: (b, 0))
    shape = jax.ShapeDtypeStruct((BATCH * SEQ, BRANCH_W), BF16)
    return pl.pallas_call(
        functools.partial(_ctx_attn_kernel, lam_init=lam_init),
        grid=(BATCH,),
        in_specs=[col(3), col(4), col(5), col(6), col(7), col(8),
                  pl.BlockSpec((4, DA_HEAD_DIM), lambda b: (0, 0)),
                  pl.BlockSpec((DA_V_DIM, 1), lambda b: (0, 0))],
        out_specs=[out, out],
        out_shape=[shape, shape],
        scratch_shapes=[pltpu.VMEM((BRANCH_W, SEQ), F32)],
        compiler_params=_cparams("arbitrary"),
        name="ctx_attention",
    )(u, u, u, u, u, u, da_lambda, subln_col)


def _rope(x, cos, sin_signed):
    n = x.shape[-1]
    lane = lax.broadcasted_iota(jnp.int32, x.shape, 1)
    partner = jnp.where(lane % 2 == 0, pltpu.roll(x, n - 1, axis=1), pltpu.roll(x, 1, axis=1))
    return x * cos + partner * sin_signed


def _attn_prep_kernel(q_ref, k_ref, v_ref, kc_ref, vc_ref, *refs, rope, scale):
    cos_ref, sin_ref = refs[:2] if rope else (None, None)
    qt_ref, ko_ref, vt_ref = refs[-3:]
    t = pl.program_id(1)
    dv = NA_HEAD_DIM

    def put_v(v):
        n = v.shape[0]
        vt = v.astype(F32).T.astype(BF16)
        ones = jnp.ones((ATT_ONES_ROWS, n), BF16)
        for h in range(BRANCH_W // dv):
            vt_ref[0, h, 0:dv, 0:n] = vt[h * dv:(h + 1) * dv]
            vt_ref[0, h, dv:dv + ATT_ONES_ROWS, 0:n] = ones

    @pl.when(t < DEC_SEQ // ATT_PREP_T)
    def _():
        q = q_ref[...].astype(F32)
        k = k_ref[...].astype(F32)
        if rope:
            q = _rope(q, cos_ref[...], sin_ref[...])
            k = _rope(k, cos_ref[...], sin_ref[...])
        qt_ref[0] = (q * scale).T.astype(BF16)
        ko_ref[0] = k.astype(BF16)
        put_v(v_ref[...])

    @pl.when(t == DEC_SEQ // ATT_PREP_T)
    def _():
        ko_ref[0, 0:PAST_LEN, :] = kc_ref[0].astype(BF16)
        put_v(vc_ref[0])


def _attn_prep(u, first_col, k_ctx, v_ctx, head_dim, rope_tables=None):
    rope = rope_tables is not None
    tile = ATT_PREP_T
    nt = DEC_SEQ // tile
    last = nt - 1
    rowblk = lambda j: pl.BlockSpec((tile, BRANCH_W), lambda b, t, j=j: (b * nt + jnp.minimum(t, last), j))
    tab = pl.BlockSpec((tile, BRANCH_W), lambda b, t: (jnp.minimum(t, last), 0))
    ctx = pl.BlockSpec((1, PAST_LEN, BRANCH_W), lambda b, t: (b, 0, 0))
    heads = BRANCH_W // NA_HEAD_DIM
    vrows = NA_HEAD_DIM + ATT_ONES_ROWS
    return pl.pallas_call(
        functools.partial(_attn_prep_kernel, rope=rope, scale=head_dim ** -0.5 * LOG2E),
        grid=(DEC_BATCH, nt + 1),
        in_specs=[rowblk(first_col), rowblk(first_col + 1), rowblk(first_col + 2), ctx, ctx] + [tab, tab] * rope,
        out_specs=[
            pl.BlockSpec((1, BRANCH_W, tile), lambda b, t: (b, 0, jnp.minimum(t, last))),
            pl.BlockSpec((1, tile, BRANCH_W), lambda b, t: (b, t, 0)),
            pl.BlockSpec((1, heads, vrows, tile), lambda b, t: (b, 0, 0, t)),
        ],
        out_shape=[
            jax.ShapeDtypeStruct((DEC_BATCH, BRANCH_W, DEC_SEQ), BF16),
            jax.ShapeDtypeStruct((DEC_BATCH, ATT_KEYS, BRANCH_W), BF16),
            jax.ShapeDtypeStruct((DEC_BATCH, heads, vrows, ATT_KEYS), BF16),
        ],
        compiler_params=_cparams("arbitrary", "arbitrary"),
        name="attn_prep",
    )(u, u, u, k_ctx, v_ctx, *(rope_tables or ()))


NA_ROWS = ATT_TQ // GRID_W
NA_UNION = 3 * NA_ROWS
NA_STEPS = GRID_H // NA_ROWS
NA_SLABS = NA_UNION // NA_ROWS
NA_VARIANT_OFFSET = (0, -NA_ROWS, -2 * NA_ROWS)


def _na_variant(s):
    return jnp.minimum(s, 1) + s // (NA_STEPS - 1)


def _na_window_block(s):
    return jnp.clip(s - 1, 0, NA_STEPS - NA_SLABS)


def _na_bias_kernel(rpb_ref, o_ref):
    kc = lax.broadcasted_iota(jnp.int32, (GRID_W, GRID_W), 0)
    qc = lax.broadcasted_iota(jnp.int32, (GRID_W, GRID_W), 1)
    dc = jnp.clip(kc - qc, -(NA_WIN_COLS - 1), NA_WIN_COLS - 1) + (NA_WIN_COLS - 1)
    c0 = jnp.clip(qc - NA_WIN_COLS // 2, 0, GRID_W - NA_WIN_COLS)
    col_ok = (kc >= c0) & (kc < c0 + NA_WIN_COLS)
    r = rpb_ref[0, 0] * LOG2E
    masked = jnp.full((GRID_W, GRID_W), NEG_INF, F32)
    tiles = []
    for dr in range(2 * NA_WIN_ROWS - 1):
        acc = jnp.zeros((GRID_W, GRID_W), F32)
        for d in range(2 * NA_WIN_COLS - 1):
            acc = jnp.where(dc == d, r[dr:dr + 1, d:d + 1], acc)
        tiles.append(jnp.where(col_ok, acc, masked))
    for v, off in enumerate(NA_VARIANT_OFFSET):
        for kr in range(NA_UNION):
            for rr in range(NA_ROWS):
                w0 = (0, rr, NA_UNION - NA_WIN_ROWS)[v]
                dr = kr + off - rr
                inside = w0 <= kr < w0 + NA_WIN_ROWS
                o_ref[0, v, 0, kr * GRID_W:(kr + 1) * GRID_W, rr * GRID_W:(rr + 1) * GRID_W] = (
                    tiles[dr + NA_WIN_ROWS - 1] if inside else masked)


def _na_bias_table(na_rpb):
    n_dr, n_dc = 2 * NA_WIN_ROWS - 1, 2 * NA_WIN_COLS - 1
    nv = len(NA_VARIANT_OFFSET)
    return pl.pallas_call(
        _na_bias_kernel,
        grid=(DEPTH, NA_HEADS),
        in_specs=[pl.BlockSpec((1, 1, n_dr, n_dc), lambda l, h: (l, h, 0, 0))],
        out_specs=pl.BlockSpec((1, nv, 1, NA_UNION * GRID_W, ATT_TQ), lambda l, h: (l, 0, h, 0, 0)),
        out_shape=jax.ShapeDtypeStruct((DEPTH, nv, NA_HEADS, NA_UNION * GRID_W, ATT_TQ), F32),
        compiler_params=_cparams("arbitrary", "arbitrary"),
        name="na_bias_table",
    )(na_rpb)


def _na_kernel(qt_ref, *refs):
    n = NA_SLABS + 1
    k_refs, vt_refs = refs[:n], refs[n:2 * n]
    bias_ref, o_ref, acc_ref = refs[2 * n:]
    dv = NA_HEAD_DIM
    heads_per_group = LANES // dv
    for g in range(BRANCH_W // LANES):
        lanes = slice(g * LANES, (g + 1) * LANES)
        qbd = _masked_q_blocks(qt_ref[0, lanes, :], dv)
        keys = jnp.concatenate([k_ref[0, :, lanes] for k_ref in k_refs], axis=0)
        st = jnp.dot(keys, qbd, preferred_element_type=F32)
        n_win = NA_SLABS * ATT_TQ
        st_win = st[0:n_win] + jnp.concatenate(
            [bias_ref[0, g * heads_per_group + hh] for hh in range(heads_per_group)], axis=1)
        st_ctx = st[n_win:]
        mx = jnp.maximum(_colmax(st_win), _colmax(st_ctx))
        pt = jnp.concatenate([jnp.exp2(st_win - mx), jnp.exp2(st_ctx - mx)], axis=0).astype(BF16)
        for hh in range(heads_per_group):
            h = g * heads_per_group + hh
            ve = jnp.concatenate([vt_ref[0, h] for vt_ref in vt_refs], axis=1)
            oe = jnp.dot(ve, pt[:, hh * ATT_TQ:(hh + 1) * ATT_TQ], preferred_element_type=F32)
            acc_ref[h * dv:(h + 1) * dv, :] = oe[0:dv] / oe[dv:dv + 1]
    o_ref[...] = acc_ref[...].T.astype(o_ref.dtype)


def _nbr_attention(qt, k, vt, bias, l):
    vrows = NA_HEAD_DIM + ATT_ONES_ROWS
    ctx_blk = DEC_SEQ // ATT_TQ
    k_specs = [pl.BlockSpec((1, ATT_TQ, BRANCH_W), lambda b, s, j=j: (b, _na_window_block(s) + j, 0))
               for j in range(NA_SLABS)]
    k_specs.append(pl.BlockSpec((1, ATT_TQ, BRANCH_W), lambda b, s: (b, ctx_blk, 0)))
    vt_specs = [pl.BlockSpec((1, NA_HEADS, vrows, ATT_TQ), lambda b, s, j=j: (b, 0, 0, _na_window_block(s) + j))
                for j in range(NA_SLABS)]
    vt_specs.append(pl.BlockSpec((1, NA_HEADS, vrows, ATT_TQ), lambda b, s: (b, 0, 0, ctx_blk)))
    n = NA_SLABS + 1
    return pl.pallas_call(
        _na_kernel,
        grid=(DEC_BATCH, NA_STEPS),
        in_specs=[pl.BlockSpec((1, BRANCH_W, ATT_TQ), lambda b, s: (b, 0, s))] + k_specs + vt_specs + [
            pl.BlockSpec((None, 1, NA_HEADS, NA_UNION * GRID_W, ATT_TQ), lambda b, s: (l, _na_variant(s), 0, 0, 0))],
        out_specs=pl.BlockSpec((ATT_TQ, BRANCH_W), lambda b, s: (b * NA_STEPS + s, 0)),
        out_shape=jax.ShapeDtypeStruct((DEC_BATCH * DEC_SEQ, BRANCH_W), BF16),
        scratch_shapes=[pltpu.VMEM((BRANCH_W, ATT_TQ), F32)],
        compiler_params=_cparams("arbitrary", "arbitrary"),
        name="nbr_attention",
    )(qt, *([k] * n), *([vt] * n), bias)


DA_TQ = ATT_TQ
DA_KEYS = ATT_KEYS
DA_ONES_ROWS = ATT_ONES_ROWS
DA_MAPS_PER_TILE = LANES // DA_HEAD_DIM


DA_MIN_DENOM = 2.0 ** -64


def _da_kernel(qt_ref, k_ref, vt_ref, lam_ref, sub_ref, o_ref, acc_ref, kn_ref, *, lam_init):
    lam = _da_lambda(lam_ref, lam_init)
    heads = DA_MAPS_PER_TILE // 2
    w = 2 * DA_TQ
    r = lax.broadcasted_iota(jnp.int32, (LANES, LANES), 0) // DA_HEAD_DIM
    c = lax.broadcasted_iota(jnp.int32, (LANES, LANES), 1) // DA_HEAD_DIM
    same_map = (r == c).astype(F32)

    @pl.when(pl.program_id(1) == 0)
    def _():
        for g in range(BRANCH_W // LANES):
            lanes = slice(g * LANES, (g + 1) * LANES)
            kf = k_ref[0, :, lanes].astype(F32)
            n2 = jnp.dot(kf * kf, same_map, precision=HIGHEST, preferred_element_type=F32)
            kn_ref[:, lanes] = jnp.max(n2, axis=0, keepdims=True)

    jrow = lax.broadcasted_iota(jnp.int32, (8, LANES), 0)
    dmap = lax.broadcasted_iota(jnp.int32, (8, LANES), 1) // DA_HEAD_DIM
    for g in range(BRANCH_W // LANES):
        lanes = slice(g * LANES, (g + 1) * LANES)
        qg = qt_ref[0, lanes, :]
        qbd = _masked_q_blocks(qg, DA_HEAD_DIM)
        qf = qg.astype(F32)
        b2 = jnp.dot(jnp.where(jrow == dmap, kn_ref[:, lanes], 0.0), qf * qf, precision=HIGHEST,
                     preferred_element_type=F32)
        bound = jnp.concatenate([jnp.sqrt(b2[j:j + 1]) for j in range(DA_MAPS_PER_TILE)], axis=1) * 1.001 + 1e-3

        def attend(carry, g=g, lanes=lanes, qbd=qbd):
            it, shift, _ = carry
            st = jnp.dot(k_ref[0, :, lanes], qbd, preferred_element_type=F32)
            pt = jnp.exp2(st - shift).astype(BF16)
            low = jnp.float32(jnp.inf)
            for hh in range(heads):
                h = g * heads + hh
                oe = jnp.dot(vt_ref[0, h], pt[:, hh * w:(hh + 1) * w], preferred_element_type=F32)
                den = oe[DA_V_DIM:DA_V_DIM + 1]
                low = jnp.minimum(low, jnp.min(den))
                os = [oe[0:DA_V_DIM, i * DA_TQ:(i + 1) * DA_TQ] / den[:, i * DA_TQ:(i + 1) * DA_TQ] for i in range(2)]
                ot = os[0] - lam * os[1]
                ot = ot * lax.rsqrt(jnp.mean(ot * ot, axis=0, keepdims=True) + EPS) * sub_ref[...]
                acc_ref[h * DA_V_DIM:(h + 1) * DA_V_DIM, :] = ot * (1.0 - lam_init)
            return it + 1, _colmax(st), low

        def again(carry):
            it, _, low = carry
            return (it == 0) | ((it == 1) & jnp.logical_not(low >= DA_MIN_DENOM))

        lax.while_loop(again, attend, (jnp.int32(0), bound, jnp.float32(0.0)))
    o_ref[...] = acc_ref[...].T.astype(o_ref.dtype)


def _diff_attention(qt, k, vt, da_lambda, subln_col, lam_init):
    nt = DEC_SEQ // DA_TQ
    vrows = DA_V_DIM + DA_ONES_ROWS
    return pl.pallas_call(
        functools.partial(_da_kernel, lam_init=lam_init),
        grid=(DEC_BATCH, nt),
        in_specs=[
            pl.BlockSpec((1, BRANCH_W, DA_TQ), lambda b, t: (b, 0, t)),
            pl.BlockSpec((1, DA_KEYS, BRANCH_W), lambda b, t: (b, 0, 0)),
            pl.BlockSpec((1, DA_HEADS, vrows, DA_KEYS), lambda b, t: (b, 0, 0, 0)),
            pl.BlockSpec((4, DA_HEAD_DIM), lambda b, t: (0, 0)),
            pl.BlockSpec((DA_V_DIM, 1), lambda b, t: (0, 0)),
        ],
        out_specs=pl.BlockSpec((DA_TQ, BRANCH_W), lambda b, t: (b * nt + t, 0)),
        out_shape=jax.ShapeDtypeStruct((DEC_BATCH * DEC_SEQ, BRANCH_W), BF16),
        scratch_shapes=[pltpu.VMEM((BRANCH_W, DA_TQ), F32), pltpu.VMEM((1, BRANCH_W), F32)],
        compiler_params=_cparams("arbitrary", "arbitrary"),
        name="diff_attention",
    )(qt, k, vt, da_lambda, subln_col)


def _rope_tables():
    pos = np.arange(DEC_SEQ)
    row = (pos // GRID_W).astype(np.float32)
    col = (pos % GRID_W).astype(np.float32)
    n_freq = DA_HEAD_DIM // 4
    inv = (np.float32(ROPE_BASE) ** (-np.arange(n_freq, dtype=np.float32) / n_freq)).astype(np.float32)
    ang = np.concatenate([row[:, None] * inv[None, :], col[:, None] * inv[None, :]], axis=-1)
    ang = ang.astype(np.float64)
    cos = np.repeat(np.cos(ang), 2, axis=-1)
    sin = np.repeat(np.sin(ang), 2, axis=-1)
    sign = np.where(np.arange(DA_HEAD_DIM) % 2 == 0, -1.0, 1.0)
    reps = BRANCH_W // DA_HEAD_DIM
    cos = np.tile(cos, (1, reps)).astype(np.float32)
    sin = np.tile(sin * sign[None, :], (1, reps)).astype(np.float32)
    return jnp.asarray(cos), jnp.asarray(sin)


def _filt_hidden_kernel(feat_ref, w1_ref, b1_ref, w2_ref, b2_ref, fr_ref, o_ref):
    fr = fr_ref[0]
    h = jnp.sin(fr * (jnp.dot(feat_ref[...], w1_ref[0], precision=HIGHEST, preferred_element_type=F32) + b1_ref[0]))
    o_ref[0] = jnp.sin(fr * (jnp.dot(h, w2_ref[0], precision=HIGHEST, preferred_element_type=F32) + b2_ref[0]))


def _filt_kernel(h_ref, w3f_ref, w3b_ref, dec_ref, o_ref):
    L = dec_ref.shape[0] // 2
    hf = jnp.dot(h_ref[0, 0:L], w3f_ref[0], precision=HIGHEST, preferred_element_type=F32) * dec_ref[0:L]
    hb = jnp.dot(h_ref[0, L:2 * L], w3b_ref[0], precision=HIGHEST, preferred_element_type=F32) * dec_ref[L:2 * L]
    row = lax.broadcasted_iota(jnp.int32, hb.shape, 0)
    hb = jnp.where(row == 0, 0.0, hb)
    nrm = jnp.sum(jnp.abs(hf), axis=0, keepdims=True) + jnp.sum(jnp.abs(hb), axis=0, keepdims=True)
    o_ref[0, 0, 0:L] = hf / nrm
    o_ref[0, 0, L:2 * L] = hb / nrm


def _circular_order(a):
    return np.concatenate([a, a[:1], a[1:][::-1]], axis=0)


def _hyena_pos_tables(L):
    f32 = np.float32
    pos = np.arange(L, dtype=f32)
    t = (pos / f32(L)).astype(f32)
    bands = np.linspace(1e-4, HY_POS_BANDS - 1, HY_POS_BANDS, dtype=f32)
    ang = (f32(2 * math.pi / L) * pos[:, None] * bands[None, :]).astype(np.float64)
    feats = np.zeros((L, HY_FILT_HIDDEN), f32)
    feats[:, 0] = t
    feats[:, 1:1 + HY_POS_BANDS] = np.cos(ang)
    feats[:, 1 + HY_POS_BANDS:HY_POS_DIM] = -np.sin(ang)
    deltas = np.linspace(math.log(HY_DECAY_TARGET) / HY_SLOW_DECAY,
                         math.log(HY_DECAY_TARGET) / HY_FAST_DECAY, BRANCH_W, dtype=f32)
    decay = np.exp((-t[:, None] * np.abs(deltas)[None, :]).astype(np.float64)).astype(f32)
    return jnp.asarray(_circular_order(feats)), jnp.asarray(_circular_order(decay))


def _hyena_filters(half, w1p, b1, w2, b2, w3, freq):
    feats, decay = _hyena_pos_tables(half)
    L = 2 * half
    cb = LANES
    ncb = BRANCH_W // cb
    small = lambda shape: pl.BlockSpec((1,) + shape, lambda l: (l, 0, 0))
    hidden = pl.pallas_call(
        _filt_hidden_kernel,
        grid=(DEPTH,),
        in_specs=[
            pl.BlockSpec((L, HY_FILT_HIDDEN), lambda l: (0, 0)),
            small((HY_FILT_HIDDEN, HY_FILT_HIDDEN)), small((1, HY_FILT_HIDDEN)),
            small((HY_FILT_HIDDEN, HY_FILT_HIDDEN)), small((1, HY_FILT_HIDDEN)),
            small((1, HY_FILT_HIDDEN)),
        ],
        out_specs=pl.BlockSpec((1, L, HY_FILT_HIDDEN), lambda l: (l, 0, 0)),
        out_shape=jax.ShapeDtypeStruct((DEPTH, L, HY_FILT_HIDDEN), F32),
        compiler_params=_cparams("arbitrary"),
        name=f"hyena_filter_hidden_{L}",
    )(feats, w1p, b1, w2, b2, freq)
    return pl.pallas_call(
        _filt_kernel,
        grid=(DEPTH, 2, ncb),
        in_specs=[
            pl.BlockSpec((1, L, HY_FILT_HIDDEN), lambda l, o, c: (l, 0, 0)),
            pl.BlockSpec((1, HY_FILT_HIDDEN, cb), lambda l, o, c: (l, 0, o * 2 * ncb + c)),
            pl.BlockSpec((1, HY_FILT_HIDDEN, cb), lambda l, o, c: (l, 0, o * 2 * ncb + ncb + c)),
            pl.BlockSpec((L, cb), lambda l, o, c: (0, c)),
        ],
        out_specs=pl.BlockSpec((1, 1, L, cb), lambda l, o, c: (l, o, 0, c)),
        out_shape=jax.ShapeDtypeStruct((DEPTH, 2, L, BRANCH_W), F32),
        compiler_params=_cparams("arbitrary", "arbitrary", "arbitrary"),
        name=f"hyena_filters_{L}",
    )(hidden, w3, w3, decay)


def _short_conv(u, w_ref, b_ref, seq_len):
    n = u.shape[0]
    t = lax.broadcasted_iota(jnp.int32, u.shape, 0) % seq_len
    prev = jnp.where(t == 0, 0.0, pltpu.roll(u, 1, axis=0))
    nxt = jnp.where(t == seq_len - 1, 0.0, pltpu.roll(u, n - 1, axis=0))
    return prev * w_ref[0:1, :] + u * w_ref[1:2, :] + nxt * w_ref[2:3, :] + b_ref[...]


def _dft_direct_mats():
    n, half = 2 * SEQ, SEQ
    k = np.arange(n)[:, None].astype(np.float64)
    t = np.arange(half)[None, :].astype(np.float64)
    ang = 2 * np.pi * k * t / n
    fr, fi = np.cos(ang), -np.sin(ang)
    mf = np.block([[fr, -fi], [fi, fr]])
    gr, gi = np.cos(ang).T / n, np.sin(ang).T / n
    mi = np.block([[gr, -gi], [gi, gr]])
    return mf.astype(np.float32), mi.astype(np.float32)


def _dft_real_mat():
    n = 2 * SEQ
    ang = 2 * np.pi * np.arange(n)[:, None].astype(np.float64) * np.arange(n)[None, :] / n
    return np.concatenate([np.cos(ang), -np.sin(ang)], axis=0).astype(np.float32)


def _spec_direct_kernel(h_ref, m_ref, o_ref):
    o_ref[0, 0] = jnp.dot(m_ref[...], h_ref[0, 0], precision=HIGHEST, preferred_element_type=F32)


def _spec_direct(h, m_real):
    n = 2 * SEQ
    return pl.pallas_call(
        _spec_direct_kernel,
        grid=(DEPTH, 2),
        in_specs=[pl.BlockSpec((1, 1, n, BRANCH_W), lambda l, o: (l, o, 0, 0)),
                  pl.BlockSpec((2 * n, n), lambda l, o: (0, 0))],
        out_specs=pl.BlockSpec((1, 1, 2 * n, BRANCH_W), lambda l, o: (l, o, 0, 0)),
        out_shape=jax.ShapeDtypeStruct((DEPTH, 2, 2 * n, BRANCH_W), F32),
        compiler_params=_cparams("arbitrary", "arbitrary"),
        name="hyena_spectrum_direct",
    )(h, m_real)


def _lconv_direct_kernel(s_ref, g_ref, cws_ref, cbs_ref, cwg_ref, cbg_ref, h_ref, bias_ref, mf_ref, mi_ref, o_ref,
                         *, conv_sig):
    n = 2 * SEQ
    sig = s_ref[...].astype(F32)
    if conv_sig:
        sig = _short_conv(sig, cws_ref, cbs_ref, SEQ)
    gate = _short_conv(g_ref[...].astype(F32), cwg_ref, cbg_ref, SEQ)
    z = jnp.dot(mf_ref[...], sig.astype(BF16), preferred_element_type=F32)
    zr, zi = z[0:n], z[n:2 * n]
    hr, hi = h_ref[0:n], h_ref[n:2 * n]
    y = jnp.concatenate([zr * hr - zi * hi, zr * hi + zi * hr], axis=0)
    y = jnp.dot(mi_ref[...], y.astype(BF16), preferred_element_type=F32)
    o_ref[...] = gate * (y + sig * bias_ref[...])


def _lconv_direct(sig, sig_col, gate_src, gate_col, conv_w, conv_b, spec, l, order, bias, mf, mi, conv_sig):
    n = 2 * SEQ
    rows = 2 * SEQ
    T = sig.shape[0]
    return pl.pallas_call(
        functools.partial(_lconv_direct_kernel, conv_sig=conv_sig),
        grid=(T // rows,),
        in_specs=[
            pl.BlockSpec((rows, BRANCH_W), lambda p: (p, sig_col)),
            pl.BlockSpec((rows, BRANCH_W), lambda p: (p, gate_col)),
            pl.BlockSpec((3, BRANCH_W), lambda p: (0, 0)),
            pl.BlockSpec((1, BRANCH_W), lambda p: (0, 0)),
            pl.BlockSpec((3, BRANCH_W), lambda p: (0, gate_col)),
            pl.BlockSpec((1, BRANCH_W), lambda p: (0, gate_col)),
            pl.BlockSpec((None, None, 2 * n, BRANCH_W), lambda p: (l, order, 0, 0)),
            pl.BlockSpec((1, BRANCH_W), lambda p: (0, 0)),
            pl.BlockSpec((2 * n, rows), lambda p: (0, 0)),
            pl.BlockSpec((rows, 2 * n), lambda p: (0, 0)),
        ],
        out_specs=pl.BlockSpec((rows, BRANCH_W), lambda p: (p, 0)),
        out_shape=jax.ShapeDtypeStruct((T, BRANCH_W), F32),
        compiler_params=_cparams("arbitrary"),
        name="hyena_lconv_direct",
    )(sig, gate_src, conv_w, conv_b, conv_w, conv_b, spec, bias, mf, mi)


def _dft_two_stage_mats():
    no, ni, half, n = FFT_NO, FFT_NI, FFT_HALF, FFT_N
    f64 = np.float64
    k1 = np.arange(no, dtype=f64)
    n_o = np.arange(half, dtype=f64)
    n_i = np.arange(ni, dtype=f64)
    ang = 2 * np.pi * (n_i[:, None, None] * k1[None, :, None] / n + k1[None, :, None] * n_o[None, None, :] / no)
    tr, ti = np.cos(ang), -np.sin(ang)
    m1 = np.concatenate([np.concatenate([tr, -ti], axis=2), np.concatenate([ti, tr], axis=2)], axis=1)
    k2 = np.arange(ni, dtype=f64)
    ang2 = 2 * np.pi * k2[:, None] * n_i[None, :] / ni
    f2r, f2i = np.cos(ang2), -np.sin(ang2)
    m2 = np.block([[f2r, -f2i], [f2i, f2r]])
    m2c = np.block([[f2r, f2i], [-f2i, f2r]])
    sr, si = np.transpose(tr, (0, 2, 1)) / n, -np.transpose(ti, (0, 2, 1)) / n
    m3 = np.concatenate([np.concatenate([sr, -si], axis=2), np.concatenate([si, sr], axis=2)], axis=1)
    return (m1.astype(np.float32), m2.astype(np.float32), m2c.astype(np.float32), m3.astype(np.float32))


def _dft_stage1_real_mat():
    no, ni, n = FFT_NO, FFT_NI, FFT_N
    k1 = np.arange(no, dtype=np.float64)
    n_o = np.arange(no, dtype=np.float64)
    n_i = np.arange(ni, dtype=np.float64)
    ang = 2 * np.pi * (n_i[:, None, None] * k1[None, :, None] / n + k1[None, :, None] * n_o[None, None, :] / no)
    return np.concatenate([np.cos(ang), -np.sin(ang)], axis=1).astype(np.float32)


def _store_stage1(w_ref, ni, out):
    w_ref[pl.ds(ni, FFT_NO, stride=2 * FFT_NI), :] = out[0:FFT_NO]
    w_ref[pl.ds(FFT_NI + ni, FFT_NO, stride=2 * FFT_NI), :] = out[FFT_NO:2 * FFT_NO]


def _fwd_stage1(za_ref, zb_ref, m1_ref, w_ref):
    def body(ni, carry):
        a = za_ref[pl.ds(ni, FFT_HALF, stride=FFT_NI), :]
        b = zb_ref[pl.ds(ni, FFT_HALF, stride=FFT_NI), :]
        out = jnp.dot(m1_ref[ni], jnp.concatenate([a, b], axis=0).astype(BF16), preferred_element_type=F32)
        _store_stage1(w_ref, ni, out)
        return carry

    lax.fori_loop(0, FFT_NI, body, 0, unroll=FFT_UNROLL)


def _spec_two_stage_kernel(h_ref, m1_ref, m2_ref, o_ref, w_ref):
    h = h_ref.at[0, 0]

    def stage1(ni, carry):
        a = h[pl.ds(ni, FFT_NO, stride=FFT_NI), :]
        _store_stage1(w_ref, ni, jnp.dot(m1_ref[ni], a.astype(BF16), preferred_element_type=F32))
        return carry

    lax.fori_loop(0, FFT_NI, stage1, 0, unroll=FFT_UNROLL)
    blk = 2 * FFT_NI

    cb = w_ref.shape[1]

    def stage2(kp, carry):
        rows = [pl.ds(pl.multiple_of((2 * kp + j) * blk, blk), blk) for j in range(2)]
        x = jnp.dot(m2_ref[...], jnp.concatenate([w_ref[r, :] for r in rows], axis=1).astype(BF16),
                    preferred_element_type=F32)
        for j in range(2):
            o_ref[0, 0, rows[j], :] = x[:, j * cb:(j + 1) * cb]
        return carry

    lax.fori_loop(0, FFT_NO // 2, stage2, 0, unroll=FFT_MID_UNROLL)


def _spec_two_stage(h, m1_real, m2):
    cb = LCONV_CB
    return pl.pallas_call(
        _spec_two_stage_kernel,
        grid=(DEPTH, 2, BRANCH_W // cb),
        in_specs=[pl.BlockSpec((1, 1, FFT_N, cb), lambda l, o, c: (l, o, 0, c)),
                  pl.BlockSpec((FFT_NI, 2 * FFT_NO, FFT_NO), lambda l, o, c: (0, 0, 0)),
                  pl.BlockSpec((2 * FFT_NI, 2 * FFT_NI), lambda l, o, c: (0, 0))],
        out_specs=pl.BlockSpec((1, 1, 2 * FFT_N, cb), lambda l, o, c: (l, o, 0, c)),
        out_shape=jax.ShapeDtypeStruct((DEPTH, 2, 2 * FFT_N, BRANCH_W), F32),
        scratch_shapes=[pltpu.VMEM((2 * FFT_N, cb), F32)],
        compiler_params=_cparams("arbitrary", "arbitrary", "arbitrary"),
        name="hyena_spectrum_two_stage",
    )(h, m1_real, m2)


def _lconv_two_stage_kernel(s_ref, g_ref, cws_ref, cbs_ref, cwg_ref, cbg_ref, h_ref, bias_ref,
                            m1_ref, m2_ref, m2c_ref, m3_ref, o_ref, z_ref, w_ref, *, conv_sig):
    for b in range(2):
        sig = s_ref[b].astype(F32)
        if conv_sig:
            sig = _short_conv(sig, cws_ref, cbs_ref, DEC_SEQ)
        z_ref[b] = sig
    _fwd_stage1(z_ref.at[0], z_ref.at[1], m1_ref, w_ref)
    blk = 2 * FFT_NI

    cb = w_ref.shape[1]

    def mid(kp, carry):
        rows = [pl.ds(pl.multiple_of((2 * kp + j) * blk, blk), blk) for j in range(2)]
        x = jnp.dot(m2_ref[...], jnp.concatenate([w_ref[r, :] for r in rows], axis=1).astype(BF16),
                    preferred_element_type=F32)
        h = jnp.concatenate([h_ref[r, :] for r in rows], axis=1)
        xr, xi = x[0:FFT_NI], x[FFT_NI:blk]
        hr, hi = h[0:FFT_NI], h[FFT_NI:blk]
        y = jnp.concatenate([xr * hr - xi * hi, xr * hi + xi * hr], axis=0)
        c = jnp.dot(m2c_ref[...], y.astype(BF16), preferred_element_type=F32)
        for j in range(2):
            w_ref[rows[j], :] = c[:, j * cb:(j + 1) * cb]
        return carry

    lax.fori_loop(0, FFT_NO // 2, mid, 0, unroll=FFT_MID_UNROLL)

    def last(ni, carry):
        cr = w_ref[pl.ds(ni, FFT_NO, stride=blk), :]
        ci = w_ref[pl.ds(FFT_NI + ni, FFT_NO, stride=blk), :]
        y = jnp.dot(m3_ref[ni], jnp.concatenate([cr, ci], axis=0).astype(BF16), preferred_element_type=F32)
        o_ref[0, pl.ds(ni, FFT_HALF, stride=FFT_NI), :] = y[0:FFT_HALF]
        o_ref[1, pl.ds(ni, FFT_HALF, stride=FFT_NI), :] = y[FFT_HALF:2 * FFT_HALF]
        return carry

    lax.fori_loop(0, FFT_NI, last, 0, unroll=FFT_UNROLL)
    for b in range(2):
        gate = _short_conv(g_ref[b].astype(F32), cwg_ref, cbg_ref, DEC_SEQ)
        sig = z_ref[b]
        o_ref[b] = gate * (o_ref[b] + sig * bias_ref[...])


def _lconv_two_stage(sig, sig_col, gate_src, gate_col, conv_w, conv_b, spec, l, order, bias, mats, conv_sig):
    cb = LCONV_CB
    ncb = BRANCH_W // cb
    m1, m2, m2c, m3 = mats
    const3 = lambda c, p: (0, 0, 0)
    const2 = lambda c, p: (0, 0)
    return pl.pallas_call(
        functools.partial(_lconv_two_stage_kernel, conv_sig=conv_sig),
        grid=(ncb, DEC_BATCH // 2),
        in_specs=[
            pl.BlockSpec((2, DEC_SEQ, cb), lambda c, p: (p, 0, sig_col * ncb + c)),
            pl.BlockSpec((2, DEC_SEQ, cb), lambda c, p: (p, 0, gate_col * ncb + c)),
            pl.BlockSpec((3, cb), lambda c, p: (0, c)),
            pl.BlockSpec((1, cb), lambda c, p: (0, c)),
            pl.BlockSpec((3, cb), lambda c, p: (0, gate_col * ncb + c)),
            pl.BlockSpec((1, cb), lambda c, p: (0, gate_col * ncb + c)),
            pl.BlockSpec((None, None, 2 * FFT_N, cb), lambda c, p: (l, order, 0, c)),
            pl.BlockSpec((1, cb), lambda c, p: (0, c)),
            pl.BlockSpec(m1.shape, const3),
            pl.BlockSpec(m2.shape, const2),
            pl.BlockSpec(m2c.shape, const2),
            pl.BlockSpec(m3.shape, const3),
        ],
        out_specs=pl.BlockSpec((2, DEC_SEQ, cb), lambda c, p: (p, 0, c)),
        out_shape=jax.ShapeDtypeStruct((DEC_BATCH, DEC_SEQ, BRANCH_W), F32),
        scratch_shapes=[pltpu.VMEM((2, DEC_SEQ, cb), F32), pltpu.VMEM((2 * FFT_N, cb), F32)],
        compiler_params=_cparams("arbitrary", "arbitrary"),
        name="hyena_lconv_two_stage",
    )(sig, gate_src, conv_w, conv_b, conv_w, conv_b, spec, bias, m1, m2, m2c, m3)


def kernel(x_prompt, x_sample, cache_na_k, cache_na_v, cache_da_k, cache_da_v, c, c_ctx, w_ada, b_ada, norm_mix,
           norm_ffn, w_in, hy_conv_w, hy_conv_b, hy_filt_w1, hy_filt_b1, hy_filt_w2, hy_filt_b2, hy_filt_w3,
           hy_filt_freq, hy_bias, na_rpb, da_lambda, da_subln, w_lift, w_out, w_ffn_in, w_ffn_out, norm_final):
    TP, TS = BATCH * SEQ, DEC_BATCH * DEC_SEQ
    xp = x_prompt.reshape(TP, D_MODEL)
    xs = x_sample.reshape(TS, D_MODEL)

    cc = jnp.concatenate([c_ctx[None, :], c, jnp.zeros((8 - 1 - DEC_BATCH, D_MODEL), F32)], axis=0)
    mod = _modulation(cc, w_ada, b_ada)
    mod_p = mod[:, 0:1].reshape(DEPTH, 1, 1, 6 * D_MODEL)
    mod_s = mod[:, 1:1 + DEC_BATCH].reshape(DEPTH, DEC_BATCH, 1, 6 * D_MODEL)

    w_mix = w_in[:, :, :MIX_W].astype(BF16)
    w_gate = w_in[:, :, MIX_W:].astype(BF16)
    w_lift_b = w_lift.astype(BF16)
    w_out_b = w_out.astype(BF16)
    w_ffn_in_b = w_ffn_in.astype(BF16)
    w_ffn_out_b = w_ffn_out.astype(BF16)
    g_mix = norm_mix.reshape(DEPTH, 1, D_MODEL)
    g_ffn = norm_ffn.reshape(DEPTH, 1, D_MODEL)
    g_fin = norm_final.reshape(1, D_MODEL)
    subln = da_subln.reshape(DEPTH, 1, DA_V_DIM)
    subln_col = da_subln.reshape(DEPTH, DA_V_DIM, 1)

    w1p = jnp.pad(hy_filt_w1, ((0, 0), (0, HY_FILT_HIDDEN - HY_POS_DIM), (0, 0)))
    b1 = hy_filt_b1.reshape(DEPTH, 1, HY_FILT_HIDDEN)
    b2 = hy_filt_b2.reshape(DEPTH, 1, HY_FILT_HIDDEN)
    fr = hy_filt_freq.reshape(DEPTH, 1, HY_FILT_HIDDEN)
    mf, mi = _dft_direct_mats()
    mats = _dft_two_stage_mats()
    h_p = _hyena_filters(SEQ, w1p, b1, hy_filt_w2, b2, hy_filt_w3, fr)
    h_s = _hyena_filters(DEC_SEQ, w1p, b1, hy_filt_w2, b2, hy_filt_w3, fr)
    spec_p = _spec_direct(h_p, jnp.asarray(_dft_real_mat()))
    mf_b, mi_b = jnp.asarray(mf, dtype=BF16), jnp.asarray(mi, dtype=BF16)
    mats_b = tuple(jnp.asarray(m, dtype=BF16) for m in mats)
    spec_s = _spec_two_stage(h_s, jnp.asarray(_dft_stage1_real_mat(), dtype=BF16), mats_b[1])
    conv_b = hy_conv_b.reshape(DEPTH, 1, 3 * BRANCH_W)

    na_bias = _na_bias_table(na_rpb)
    rope_tables = _rope_tables()
    ck_na = cache_na_k.reshape(DEC_BATCH, DEPTH, PAST_LEN, BRANCH_W)
    cv_na = cache_na_v.reshape(DEC_BATCH, DEPTH, PAST_LEN, BRANCH_W)
    ck_da = cache_da_k.reshape(DEC_BATCH, DEPTH, PAST_LEN, BRANCH_W)
    cv_da = cache_da_v.reshape(DEC_BATCH, DEPTH, PAST_LEN, BRANCH_W)

    caches = tuple(jnp.zeros((BATCH, DEPTH, SEQ, BRANCH_W), F32) for _ in CACHE_BLOCKS)
    for l in range(DEPTH):
        lam_init = 0.8 - 0.6 * math.exp(-0.3 * l)
        final = l == DEPTH - 1

        u, caches = _in_proj(xp, g_mix[l], mod_p[l], w_mix, l, TP, BF16, caches=caches)
        z1 = _lconv_direct(u, 0, u, 1, hy_conv_w[l], conv_b[l], spec_p, l, 0, hy_bias[l, 0:1], mf_b, mi_b, True)
        y_hy = _lconv_direct(z1, 0, u, 2, hy_conv_w[l], conv_b[l], spec_p, l, 1, hy_bias[l, 1:2], mf_b, mi_b, False)
        y_na, y_da = _ctx_attention(u, da_lambda[l], subln_col[l], lam_init)
        xp = _merge_out(xp, g_mix[l], mod_p[l], y_hy, y_na, y_da, w_gate, w_lift_b, w_out_b, l, TP)
        xp = _ffn(xp, g_ffn[l], mod_p[l], w_ffn_in_b, w_ffn_out_b, g_fin, l, TP, final)

        u = _in_proj(xs, g_mix[l], mod_s[l], w_mix, l, DEC_SEQ, BF16)
        u3 = u.reshape(DEC_BATCH, DEC_SEQ, MIX_W)
        z1 = _lconv_two_stage(u3, 0, u3, 1, hy_conv_w[l], conv_b[l], spec_s, l, 0, hy_bias[l, 0:1], mats_b, True)
        y_hy = _lconv_two_stage(z1, 0, u3, 2, hy_conv_w[l], conv_b[l], spec_s, l, 1, hy_bias[l, 1:2], mats_b, False)
        y_hy = y_hy.reshape(TS, BRANCH_W)
        qn, kn, vn = _attn_prep(u, 3, ck_na[:, l], cv_na[:, l], NA_HEAD_DIM)
        y_na = _nbr_attention(qn, kn, vn, na_bias, l)
        q, kt, v = _attn_prep(u, 6, ck_da[:, l], cv_da[:, l], DA_HEAD_DIM, rope_tables)
        y_da = _diff_attention(q, kt, v, da_lambda[l], subln_col[l], lam_init)
        xs = _merge_out(xs, g_mix[l], mod_s[l], y_hy, y_na, y_da, w_gate, w_lift_b, w_out_b, l, DEC_SEQ)
        xs = _ffn(xs, g_ffn[l], mod_s[l], w_ffn_in_b, w_ffn_out_b, g_fin, l, DEC_SEQ, final)

    y_prompt = xp.reshape(BATCH, SEQ, D_MODEL)
    y_sample = xs.reshape(DEC_BATCH, DEC_SEQ, D_MODEL)
    heads = lambda a, d: a.reshape(BATCH, DEPTH, SEQ, BRANCH_W // d, d)
    return (y_prompt, y_sample, heads(caches[0], NA_HEAD_DIM), heads(caches[1], NA_HEAD_DIM),
            heads(caches[2], 2 * DA_HEAD_DIM), heads(caches[3], DA_V_DIM))
```

```python
import functools
import math

import numpy as np
import jax
import jax.numpy as jnp
from jax import lax
from jax.experimental import pallas as pl
from jax.experimental.pallas import tpu as pltpu

F32 = jnp.float32
BF16 = jnp.bfloat16
HIGHEST = lax.Precision.HIGHEST

D_MODEL = 1024
BATCH = 32
SEQ = 256
DEPTH = 4
DEC_BATCH = 4
DEC_SEQ = 4096
PAST_LEN = 256
GRID_W = 64
GRID_H = DEC_SEQ // GRID_W
BRANCH_W = 512
HY_POS_BANDS = 16
HY_POS_DIM = 1 + 2 * HY_POS_BANDS
HY_FILT_HIDDEN = 64
HY_DECAY_TARGET = 1e-2
HY_FAST_DECAY = 0.3
HY_SLOW_DECAY = 1.5
NA_HEADS = 8
NA_HEAD_DIM = 64
NA_WIN_ROWS = 8
NA_WIN_COLS = 16
DA_HEADS = 8
DA_HEAD_DIM = 32
DA_V_DIM = 64
D_FF = 2816
MIX_W = 9 * BRANCH_W
ROPE_BASE = 10000.0
EPS = 1e-6
NEG_INF = -1e30

VMEM_LIMIT_BYTES = 56 * 1024 * 1024
LANES = 128
MXU_DIM = 256

FFT_N = 2 * DEC_SEQ
FFT_NO = 64
FFT_NI = 128
FFT_HALF = FFT_NO // 2
FFT_UNROLL = 8
FFT_MID_UNROLL = 16
LCONV_CB = LANES


def _cparams(*sem):
    return pltpu.CompilerParams(dimension_semantics=sem, vmem_limit_bytes=VMEM_LIMIT_BYTES)


def _sigmoid(x):
    return 1.0 / (1.0 + jnp.exp(-x))


def _rms(x, g):
    return x * lax.rsqrt(jnp.mean(x * x, axis=-1, keepdims=True) + EPS) * g


def _modnorm(x, g, shift, scale):
    return _rms(x, g) * (1.0 + scale) + shift


def _bdot(a, b):
    return jnp.dot(a.astype(BF16), b.astype(BF16), preferred_element_type=F32)


def _mod_kernel(c_ref, w_ref, b_ref, o_ref):
    c = c_ref[...]
    s = c * _sigmoid(c)
    o_ref[0] = jnp.dot(s, w_ref[0], precision=HIGHEST, preferred_element_type=F32) + b_ref[0]


def _modulation(cc, w_ada, b_ada):
    nt = 6
    return pl.pallas_call(
        _mod_kernel,
        grid=(DEPTH, nt),
        in_specs=[
            pl.BlockSpec((8, D_MODEL), lambda l, j: (0, 0)),
            pl.BlockSpec((1, D_MODEL, D_MODEL), lambda l, j: (l, 0, j)),
            pl.BlockSpec((1, 1, D_MODEL), lambda l, j: (l, 0, j)),
        ],
        out_specs=pl.BlockSpec((1, 8, D_MODEL), lambda l, j: (l, 0, j)),
        out_shape=jax.ShapeDtypeStruct((DEPTH, 8, 6 * D_MODEL), F32),
        compiler_params=_cparams("arbitrary", "arbitrary"),
        name="modulation",
    )(cc, w_ada, b_ada.reshape(DEPTH, 1, 6 * D_MODEL))


IN_TM = 512
CACHE_BLOCKS = (4, 5, 7, 8)


def _in_kernel(*refs, n_cache):
    x_ref, g_ref, mod_ref, w_ref = refs[:4]
    o_ref = refs[4 + n_cache]
    cache_refs = refs[5 + n_cache:]
    m = mod_ref[0]
    h = _modnorm(x_ref[...], g_ref[...], m[:, 0:D_MODEL], m[:, D_MODEL:2 * D_MODEL]).astype(BF16)
    res = jnp.dot(h, w_ref[...], preferred_element_type=F32)
    o_ref[...] = res.astype(o_ref.dtype)
    for c, c_ref in zip(CACHE_BLOCKS, cache_refs):
        c_ref[...] = res[:, c * BRANCH_W:(c + 1) * BRANCH_W].reshape(c_ref.shape)


def _in_proj(x, g, mod, w, l, rows_per_mod, out_dtype, caches=None):
    T = x.shape[0]
    tm = IN_TM
    per = rows_per_mod // tm
    in_specs = [
        pl.BlockSpec((tm, D_MODEL), lambda i: (i, 0)),
        pl.BlockSpec((1, D_MODEL), lambda i: (0, 0)),
        pl.BlockSpec((1, 1, 6 * D_MODEL), lambda i: (i // per, 0, 0)),
        pl.BlockSpec((None, D_MODEL, MIX_W), lambda i: (l, 0, 0), pipeline_mode=pl.Buffered(1)),
    ]
    out_specs = [pl.BlockSpec((tm, MIX_W), lambda i: (i, 0))]
    out_shape = [jax.ShapeDtypeStruct((T, MIX_W), out_dtype)]
    args = [x, g, mod, w]
    aliases = {}
    if caches is not None:
        out_specs += [pl.BlockSpec((tm // SEQ, 1, SEQ, BRANCH_W), lambda i: (i, l, 0, 0))] * len(caches)
        out_shape += [jax.ShapeDtypeStruct(c.shape, c.dtype) for c in caches]
        in_specs += [pl.BlockSpec(memory_space=pl.ANY)] * len(caches)
        aliases = {4 + n: 1 + n for n in range(len(caches))}
        args += list(caches)
    outs = pl.pallas_call(
        functools.partial(_in_kernel, n_cache=len(args) - 4),
        grid=(T // tm,),
        in_specs=in_specs,
        out_specs=out_specs,
        out_shape=out_shape,
        input_output_aliases=aliases,
        compiler_params=_cparams("arbitrary"),
        name="in_proj",
    )(*args)
    return outs[0] if caches is None else (outs[0], tuple(outs[1:]))


def _mid_kernel(x_ref, g_ref, mod_ref, yh_ref, yn_ref, yd_ref, wg_ref, wl_ref, wo_ref, o_ref):
    m = mod_ref[0]
    x = x_ref[...]
    h = _modnorm(x, g_ref[...], m[:, 0:D_MODEL], m[:, D_MODEL:2 * D_MODEL]).astype(BF16)
    merged = None
    for br, y_ref in enumerate((yh_ref, yn_ref, yd_ref)):
        gate = _sigmoid(jnp.dot(h, wg_ref[:, br * D_MODEL:(br + 1) * D_MODEL], preferred_element_type=F32))
        lift = jnp.dot(y_ref[...].astype(BF16), wl_ref[br], preferred_element_type=F32)
        t = gate * lift
        merged = t if merged is None else merged + t
    o_ref[...] = x + m[:, 2 * D_MODEL:3 * D_MODEL] * _bdot(merged, wo_ref[...])


def _merge_out(x, g, mod, y_hy, y_na, y_da, w_gate, w_lift, w_out, l, rows_per_mod):
    T = x.shape[0]
    tm = 512
    per = rows_per_mod // tm
    row = lambda i: (i, 0)
    const2 = lambda i: (0, 0)
    return pl.pallas_call(
        _mid_kernel,
        grid=(T // tm,),
        in_specs=[
            pl.BlockSpec((tm, D_MODEL), row),
            pl.BlockSpec((1, D_MODEL), const2),
            pl.BlockSpec((1, 1, 6 * D_MODEL), lambda i: (i // per, 0, 0)),
            pl.BlockSpec((tm, BRANCH_W), row),
            pl.BlockSpec((tm, BRANCH_W), row),
            pl.BlockSpec((tm, BRANCH_W), row),
            pl.BlockSpec((None, D_MODEL, 3 * D_MODEL), lambda i: (l, 0, 0)),
            pl.BlockSpec((None, 3, BRANCH_W, D_MODEL), lambda i: (l, 0, 0, 0)),
            pl.BlockSpec((None, D_MODEL, D_MODEL), lambda i: (l, 0, 0)),
        ],
        out_specs=pl.BlockSpec((tm, D_MODEL), row),
        out_shape=jax.ShapeDtypeStruct((T, D_MODEL), F32),
        compiler_params=_cparams("arbitrary"),
        name="merge_out",
    )(x, g, mod, y_hy, y_na, y_da, w_gate, w_lift, w_out)


def _ffn_kernel(x_ref, g_ref, mod_ref, w1_ref, w2_ref, gf_ref, o_ref, *, final):
    m = mod_ref[0]
    x = x_ref[...]
    h = _modnorm(x, g_ref[...], m[:, 3 * D_MODEL:4 * D_MODEL], m[:, 4 * D_MODEL:5 * D_MODEL]).astype(BF16)
    a = jnp.dot(h, w1_ref[:, 0:D_FF], preferred_element_type=F32)
    b = jnp.dot(h, w1_ref[:, D_FF:2 * D_FF], preferred_element_type=F32)
    xn = x + m[:, 5 * D_MODEL:6 * D_MODEL] * _bdot(a * _sigmoid(a) * b, w2_ref[...])
    if final:
        xn = _rms(xn, gf_ref[...])
    o_ref[...] = xn


def _ffn(x, g, mod, w_ffn_in, w_ffn_out, g_final, l, rows_per_mod, final):
    T = x.shape[0]
    tm = 512
    per = rows_per_mod // tm
    resident = pl.Buffered(1)
    return pl.pallas_call(
        functools.partial(_ffn_kernel, final=final),
        grid=(T // tm,),
        in_specs=[
            pl.BlockSpec((tm, D_MODEL), lambda i: (i, 0)),
            pl.BlockSpec((1, D_MODEL), lambda i: (0, 0)),
            pl.BlockSpec((1, 1, 6 * D_MODEL), lambda i: (i // per, 0, 0)),
            pl.BlockSpec((None, D_MODEL, 2 * D_FF), lambda i: (l, 0, 0), pipeline_mode=resident),
            pl.BlockSpec((None, D_FF, D_MODEL), lambda i: (l, 0, 0), pipeline_mode=resident),
            pl.BlockSpec((1, D_MODEL), lambda i: (0, 0)),
        ],
        out_specs=pl.BlockSpec((tm, D_MODEL), lambda i: (i, 0)),
        out_shape=jax.ShapeDtypeStruct((T, D_MODEL), F32),
        compiler_params=_cparams("arbitrary"),
        name="ffn",
    )(x, g, mod, w_ffn_in, w_ffn_out, g_final)


def _da_lambda(lam_ref, lam_init):
    lp = lam_ref[...]
    a = jnp.sum(lp[0:1] * lp[1:2], axis=1, keepdims=True)
    b = jnp.sum(lp[2:3] * lp[3:4], axis=1, keepdims=True)
    return jnp.exp(a) - jnp.exp(b) + lam_init


ATT_ONES_ROWS = 16
ATT_TQ = 256
ATT_PREP_T = 512
ATT_KEYS = DEC_SEQ + PAST_LEN
ATT_MIN_DENOM = 2.0 ** -64
LOG2E = math.log2(math.e)


def _masked_q_blocks(qt, d):
    row = lax.broadcasted_iota(jnp.int32, qt.shape, 0)
    zero = jnp.zeros_like(qt)
    return jnp.concatenate([jnp.where((row >= j * d) & (row < (j + 1) * d), qt, zero) for j in range(LANES // d)],
                           axis=1)


def _colmax(st):
    keys, n = st.shape
    return jnp.max(jnp.max(st.reshape(keys // MXU_DIM, MXU_DIM, n), axis=0), axis=0, keepdims=True)


def _ctx_attn_kernel(nq_ref, nk_ref, nv_ref, dq_ref, dk_ref, dv_ref, lam_ref, sub_ref, yn_ref, yd_ref, acc_ref,
                     *, lam_init):
    lam = _da_lambda(lam_ref, lam_init)
    ones = jnp.ones((ATT_ONES_ROWS, SEQ), BF16)

    def attend(q_ref, k_ref, v_ref, d, maps_per_head, finish):
        qt = (q_ref[...].astype(F32) * (d ** -0.5 * LOG2E)).T.astype(BF16)
        vt = v_ref[...].astype(F32).T.astype(BF16)
        kb = k_ref[...].astype(BF16)
        dv = NA_HEAD_DIM
        heads_per_group = LANES // (d * maps_per_head)
        w = maps_per_head * SEQ
        for g in range(BRANCH_W // LANES):
            lanes = slice(g * LANES, (g + 1) * LANES)
            st = jnp.dot(kb[:, lanes], _masked_q_blocks(qt[lanes], d), preferred_element_type=F32)
            pt = jnp.exp2(st - _colmax(st)).astype(BF16)
            for j in range(heads_per_group):
                h = g * heads_per_group + j
                ve = jnp.concatenate([vt[h * dv:(h + 1) * dv], ones], axis=0)
                oe = jnp.dot(ve, pt[:, j * w:(j + 1) * w], preferred_element_type=F32)
                os = [oe[0:dv, i * SEQ:(i + 1) * SEQ] / oe[dv:dv + 1, i * SEQ:(i + 1) * SEQ]
                      for i in range(maps_per_head)]
                acc_ref[h * dv:(h + 1) * dv, :] = finish(os)

    attend(nq_ref, nk_ref, nv_ref, NA_HEAD_DIM, 1, lambda os: os[0])
    yn_ref[...] = acc_ref[...].T.astype(yn_ref.dtype)

    def da_finish(os):
        ot = os[0] - lam * os[1]
        ot = ot * lax.rsqrt(jnp.mean(ot * ot, axis=0, keepdims=True) + EPS) * sub_ref[...]
        return ot * (1.0 - lam_init)

    attend(dq_ref, dk_ref, dv_ref, DA_HEAD_DIM, 2, da_finish)
    yd_ref[...] = acc_ref[...].T.astype(yd_ref.dtype)


def _ctx_attention(u, da_lambda, subln_col, lam_init):
    col = lambda j: pl.BlockSpec((SEQ, BRANCH_W), lambda b, j=j: (b, j))
    out = pl.BlockSpec((SEQ, BRANCH_W), lambda b: (b, 0))
    shape = jax.ShapeDtypeStruct((BATCH * SEQ, BRANCH_W), BF16)
    return pl.pallas_call(
        functools.partial(_ctx_attn_kernel, lam_init=lam_init),
        grid=(BATCH,),
        in_specs=[col(3), col(4), col(5), col(6), col(7), col(8),
                  pl.BlockSpec((4, DA_HEAD_DIM), lambda b: (0, 0)),
                  pl.BlockSpec((DA_V_DIM, 1), lambda b: (0, 0))],
        out_specs=[out, out],
        out_shape=[shape, shape],
        scratch_shapes=[pltpu.VMEM((BRANCH_W, SEQ), F32)],
        compiler_params=_cparams("arbitrary"),
        name="ctx_attention",
    )(u, u, u, u, u, u, da_lambda, subln_col)


def _rope(x, cos, sin_signed):
    n = x.shape[-1]
    lane = lax.broadcasted_iota(jnp.int32, x.shape, 1)
    partner = jnp.where(lane % 2 == 0, pltpu.roll(x, n - 1, axis=1), pltpu.roll(x, 1, axis=1))
    return x * cos + partner * sin_signed


def _attn_prep_kernel(q_ref, k_ref, v_ref, kc_ref, vc_ref, *refs, rope, scale):
    cos_ref, sin_ref = refs[:2] if rope else (None, None)
    qt_ref, ko_ref, vt_ref = refs[-3:]
    t = pl.program_id(1)
    dv = NA_HEAD_DIM

    def put_v(v):
        n = v.shape[0]
        vt = v.astype(F32).T.astype(BF16)
        ones = jnp.ones((ATT_ONES_ROWS, n), BF16)
        for h in range(BRANCH_W // dv):
            vt_ref[0, h, 0:dv, 0:n] = vt[h * dv:(h + 1) * dv]
            vt_ref[0, h, dv:dv + ATT_ONES_ROWS, 0:n] = ones

    @pl.when(t < DEC_SEQ // ATT_PREP_T)
    def _():
        q = q_ref[...].astype(F32)
        k = k_ref[...].astype(F32)
        if rope:
            q = _rope(q, cos_ref[...], sin_ref[...])
            k = _rope(k, cos_ref[...], sin_ref[...])
        qt_ref[0] = (q * scale).T.astype(BF16)
        ko_ref[0] = k.astype(BF16)
        put_v(v_ref[...])

    @pl.when(t == DEC_SEQ // ATT_PREP_T)
    def _():
        ko_ref[0, 0:PAST_LEN, :] = kc_ref[0].astype(BF16)
        put_v(vc_ref[0])


def _attn_prep(u, first_col, k_ctx, v_ctx, head_dim, rope_tables=None):
    rope = rope_tables is not None
    tile = ATT_PREP_T
    nt = DEC_SEQ // tile
    last = nt - 1
    rowblk = lambda j: pl.BlockSpec((tile, BRANCH_W), lambda b, t, j=j: (b * nt + jnp.minimum(t, last), j))
    tab = pl.BlockSpec((tile, BRANCH_W), lambda b, t: (jnp.minimum(t, last), 0))
    ctx = pl.BlockSpec((1, PAST_LEN, BRANCH_W), lambda b, t: (b, 0, 0))
    heads = BRANCH_W // NA_HEAD_DIM
    vrows = NA_HEAD_DIM + ATT_ONES_ROWS
    return pl.pallas_call(
        functools.partial(_attn_prep_kernel, rope=rope, scale=head_dim ** -0.5 * LOG2E),
        grid=(DEC_BATCH, nt + 1),
        in_specs=[rowblk(first_col), rowblk(first_col + 1), rowblk(first_col + 2), ctx, ctx] + [tab, tab] * rope,
        out_specs=[
            pl.BlockSpec((1, BRANCH_W, tile), lambda b, t: (b, 0, jnp.minimum(t, last))),
            pl.BlockSpec((1, tile, BRANCH_W), lambda b, t: (b, t, 0)),
            pl.BlockSpec((1, heads, vrows, tile), lambda b, t: (b, 0, 0, t)),
        ],
        out_shape=[
            jax.ShapeDtypeStruct((DEC_BATCH, BRANCH_W, DEC_SEQ), BF16),
            jax.ShapeDtypeStruct((DEC_BATCH, ATT_KEYS, BRANCH_W), BF16),
            jax.ShapeDtypeStruct((DEC_BATCH, heads, vrows, ATT_KEYS), BF16),
        ],
        compiler_params=_cparams("arbitrary", "arbitrary"),
        name="attn_prep",
    )(u, u, u, k_ctx, v_ctx, *(rope_tables or ()))


NA_ROWS = ATT_TQ // GRID_W
NA_UNION = 3 * NA_ROWS
NA_STEPS = GRID_H // NA_ROWS
NA_SLABS = NA_UNION // NA_ROWS
NA_VARIANT_OFFSET = (0, -NA_ROWS, -2 * NA_ROWS)


def _na_variant(s):
    return jnp.minimum(s, 1) + s // (NA_STEPS - 1)


def _na_window_block(s):
    return jnp.clip(s - 1, 0, NA_STEPS - NA_SLABS)


def _na_bias_kernel(rpb_ref, o_ref):
    kc = lax.broadcasted_iota(jnp.int32, (GRID_W, GRID_W), 0)
    qc = lax.broadcasted_iota(jnp.int32, (GRID_W, GRID_W), 1)
    dc = jnp.clip(kc - qc, -(NA_WIN_COLS - 1), NA_WIN_COLS - 1) + (NA_WIN_COLS - 1)
    c0 = jnp.clip(qc - NA_WIN_COLS // 2, 0, GRID_W - NA_WIN_COLS)
    col_ok = (kc >= c0) & (kc < c0 + NA_WIN_COLS)
    r = rpb_ref[0, 0] * LOG2E
    masked = jnp.full((GRID_W, GRID_W), NEG_INF, F32)
    tiles = []
    for dr in range(2 * NA_WIN_ROWS - 1):
        acc = jnp.zeros((GRID_W, GRID_W), F32)
        for d in range(2 * NA_WIN_COLS - 1):
            acc = jnp.where(dc == d, r[dr:dr + 1, d:d + 1], acc)
        tiles.append(jnp.where(col_ok, acc, masked))
    for v, off in enumerate(NA_VARIANT_OFFSET):
        for kr in range(NA_UNION):
            for rr in range(NA_ROWS):
                w0 = (0, rr, NA_UNION - NA_WIN_ROWS)[v]
                dr = kr + off - rr
                inside = w0 <= kr < w0 + NA_WIN_ROWS
                o_ref[0, v, 0, kr * GRID_W:(kr + 1) * GRID_W, rr * GRID_W:(rr + 1) * GRID_W] = (
                    tiles[dr + NA_WIN_ROWS - 1] if inside else masked)


def _na_bias_table(na_rpb):
    n_dr, n_dc = 2 * NA_WIN_ROWS - 1, 2 * NA_WIN_COLS - 1
    nv = len(NA_VARIANT_OFFSET)
    return pl.pallas_call(
        _na_bias_kernel,
        grid=(DEPTH, NA_HEADS),
        in_specs=[pl.BlockSpec((1, 1, n_dr, n_dc), lambda l, h: (l, h, 0, 0))],
        out_specs=pl.BlockSpec((1, nv, 1, NA_UNION * GRID_W, ATT_TQ), lambda l, h: (l, 0, h, 0, 0)),
        out_shape=jax.ShapeDtypeStruct((DEPTH, nv, NA_HEADS, NA_UNION * GRID_W, ATT_TQ), F32),
        compiler_params=_cparams("arbitrary", "arbitrary"),
        name="na_bias_table",
    )(na_rpb)


def _na_kernel(qt_ref, *refs):
    n = NA_SLABS + 1
    k_refs, vt_refs = refs[:n], refs[n:2 * n]
    bias_ref, o_ref, acc_ref = refs[2 * n:]
    dv = NA_HEAD_DIM
    heads_per_group = LANES // dv
    for g in range(BRANCH_W // LANES):
        lanes = slice(g * LANES, (g + 1) * LANES)
        qbd = _masked_q_blocks(qt_ref[0, lanes, :], dv)
        keys = jnp.concatenate([k_ref[0, :, lanes] for k_ref in k_refs], axis=0)
        st = jnp.dot(keys, qbd, preferred_element_type=F32)
        n_win = NA_SLABS * ATT_TQ
        st_win = st[0:n_win] + jnp.concatenate(
            [bias_ref[0, g * heads_per_group + hh] for hh in range(heads_per_group)], axis=1)
        st_ctx = st[n_win:]
        mx = jnp.maximum(_colmax(st_win), _colmax(st_ctx))
        pt = jnp.concatenate([jnp.exp2(st_win - mx), jnp.exp2(st_ctx - mx)], axis=0).astype(BF16)
        for hh in range(heads_per_group):
            h = g * heads_per_group + hh
            ve = jnp.concatenate([vt_ref[0, h] for vt_ref in vt_refs], axis=1)
            oe = jnp.dot(ve, pt[:, hh * ATT_TQ:(hh + 1) * ATT_TQ], preferred_element_type=F32)
            acc_ref[h * dv:(h + 1) * dv, :] = oe[0:dv] / oe[dv:dv + 1]
    o_ref[...] = acc_ref[...].T.astype(o_ref.dtype)


def _nbr_attention(qt, k, vt, bias, l):
    vrows = NA_HEAD_DIM + ATT_ONES_ROWS
    ctx_blk = DEC_SEQ // ATT_TQ
    k_specs = [pl.BlockSpec((1, ATT_TQ, BRANCH_W), lambda b, s, j=j: (b, _na_window_block(s) + j, 0))
               for j in range(NA_SLABS)]
    k_specs.append(pl.BlockSpec((1, ATT_TQ, BRANCH_W), lambda b, s: (b, ctx_blk, 0)))
    vt_specs = [pl.BlockSpec((1, NA_HEADS, vrows, ATT_TQ), lambda b, s, j=j: (b, 0, 0, _na_window_block(s) + j))
                for j in range(NA_SLABS)]
    vt_specs.append(pl.BlockSpec((1, NA_HEADS, vrows, ATT_TQ), lambda b, s: (b, 0, 0, ctx_blk)))
    n = NA_SLABS + 1
    return pl.pallas_call(
        _na_kernel,
        grid=(DEC_BATCH, NA_STEPS),
        in_specs=[pl.BlockSpec((1, BRANCH_W, ATT_TQ), lambda b, s: (b, 0, s))] + k_specs + vt_specs + [
            pl.BlockSpec((None, 1, NA_HEADS, NA_UNION * GRID_W, ATT_TQ), lambda b, s: (l, _na_variant(s), 0, 0, 0))],
        out_specs=pl.BlockSpec((ATT_TQ, BRANCH_W), lambda b, s: (b * NA_STEPS + s, 0)),
        out_shape=jax.ShapeDtypeStruct((DEC_BATCH * DEC_SEQ, BRANCH_W), BF16),
        scratch_shapes=[pltpu.VMEM((BRANCH_W, ATT_TQ), F32)],
        compiler_params=_cparams("arbitrary", "arbitrary"),
        name="nbr_attention",
    )(qt, *([k] * n), *([vt] * n), bias)


DA_TQ = ATT_TQ
DA_KEYS = ATT_KEYS
DA_ONES_ROWS = ATT_ONES_ROWS
DA_MAPS_PER_TILE = LANES // DA_HEAD_DIM


def _da_kernel(qt_ref, k_ref, vt_ref, lam_ref, sub_ref, o_ref, acc_ref, kn_ref, *, lam_init):
    lam = _da_lambda(lam_ref, lam_init)
    heads = DA_MAPS_PER_TILE // 2
    w = 2 * DA_TQ
    r = lax.broadcasted_iota(jnp.int32, (LANES, LANES), 0) // DA_HEAD_DIM
    c = lax.broadcasted_iota(jnp.int32, (LANES, LANES), 1) // DA_HEAD_DIM
    same_map = (r == c).astype(F32)

    @pl.when(pl.program_id(1) == 0)
    def _():
        for g in range(BRANCH_W // LANES):
            lanes = slice(g * LANES, (g + 1) * LANES)
            kf = k_ref[0, :, lanes].astype(F32)
            n2 = _bdot(kf * kf, same_map)
            kn_ref[:, lanes] = jnp.max(n2, axis=0, keepdims=True)

    jrow = lax.broadcasted_iota(jnp.int32, (8, LANES), 0)
    dmap = lax.broadcasted_iota(jnp.int32, (8, LANES), 1) // DA_HEAD_DIM
    for g in range(BRANCH_W // LANES):
        lanes = slice(g * LANES, (g + 1) * LANES)
        qg = qt_ref[0, lanes, :]
        qbd = _masked_q_blocks(qg, DA_HEAD_DIM)
        qf = qg.astype(F32)
        b2 = _bdot(jnp.where(jrow == dmap, kn_ref[:, lanes], 0.0), qf * qf)
        bound = jnp.concatenate([jnp.sqrt(b2[j:j + 1]) for j in range(DA_MAPS_PER_TILE)], axis=1) * 1.01 + 1e-3

        def attend(carry, g=g, lanes=lanes, qbd=qbd):
            it, shift, _ = carry
            st = jnp.dot(k_ref[0, :, lanes], qbd, preferred_element_type=F32)
            pt = jnp.exp2(st - shift).astype(BF16)
            low = jnp.float32(jnp.inf)
            for hh in range(heads):
                h = g * heads + hh
                oe = jnp.dot(vt_ref[0, h], pt[:, hh * w:(hh + 1) * w], preferred_element_type=F32)
                den = oe[DA_V_DIM:DA_V_DIM + 1]
                low = jnp.minimum(low, jnp.min(den))
                os = [oe[0:DA_V_DIM, i * DA_TQ:(i + 1) * DA_TQ] / den[:, i * DA_TQ:(i + 1) * DA_TQ] for i in range(2)]
                ot = os[0] - lam * os[1]
                ot = ot * lax.rsqrt(jnp.mean(ot * ot, axis=0, keepdims=True) + EPS) * sub_ref[...]
                acc_ref[h * DA_V_DIM:(h + 1) * DA_V_DIM, :] = ot * (1.0 - lam_init)
            return it + 1, _colmax(st), low

        def again(carry):
            it, _, low = carry
            return (it == 0) | ((it == 1) & jnp.logical_not(low >= ATT_MIN_DENOM))

        lax.while_loop(again, attend, (jnp.int32(0), bound, jnp.float32(0.0)))
    o_ref[...] = acc_ref[...].T.astype(o_ref.dtype)


def _diff_attention(qt, k, vt, da_lambda, subln_col, lam_init):
    nt = DEC_SEQ // DA_TQ
    vrows = DA_V_DIM + DA_ONES_ROWS
    return pl.pallas_call(
        functools.partial(_da_kernel, lam_init=lam_init),
        grid=(DEC_BATCH, nt),
        in_specs=[
            pl.BlockSpec((1, BRANCH_W, DA_TQ), lambda b, t: (b, 0, t)),
            pl.BlockSpec((1, DA_KEYS, BRANCH_W), lambda b, t: (b, 0, 0)),
            pl.BlockSpec((1, DA_HEADS, vrows, DA_KEYS), lambda b, t: (b, 0, 0, 0)),
            pl.BlockSpec((4, DA_HEAD_DIM), lambda b, t: (0, 0)),
            pl.BlockSpec((DA_V_DIM, 1), lambda b, t: (0, 0)),
        ],
        out_specs=pl.BlockSpec((DA_TQ, BRANCH_W), lambda b, t: (b * nt + t, 0)),
        out_shape=jax.ShapeDtypeStruct((DEC_BATCH * DEC_SEQ, BRANCH_W), BF16),
        scratch_shapes=[pltpu.VMEM((BRANCH_W, DA_TQ), F32), pltpu.VMEM((1, BRANCH_W), F32)],
        compiler_params=_cparams("arbitrary", "arbitrary"),
        name="diff_attention",
    )(qt, k, vt, da_lambda, subln_col)


def _rope_tables():
    pos = np.arange(DEC_SEQ)
    row = (pos // GRID_W).astype(np.float32)
    col = (pos % GRID_W).astype(np.float32)
    n_freq = DA_HEAD_DIM // 4
    inv = (np.float32(ROPE_BASE) ** (-np.arange(n_freq, dtype=np.float32) / n_freq)).astype(np.float32)
    ang = np.concatenate([row[:, None] * inv[None, :], col[:, None] * inv[None, :]], axis=-1)
    ang = ang.astype(np.float64)
    cos = np.repeat(np.cos(ang), 2, axis=-1)
    sin = np.repeat(np.sin(ang), 2, axis=-1)
    sign = np.where(np.arange(DA_HEAD_DIM) % 2 == 0, -1.0, 1.0)
    reps = BRANCH_W // DA_HEAD_DIM
    cos = np.tile(cos, (1, reps)).astype(np.float32)
    sin = np.tile(sin * sign[None, :], (1, reps)).astype(np.float32)
    return jnp.asarray(cos), jnp.asarray(sin)


def _filt_hidden_kernel(feat_ref, w1_ref, b1_ref, w2_ref, b2_ref, fr_ref, o_ref):
    fr = fr_ref[0]
    h = jnp.sin(fr * (jnp.dot(feat_ref[...], w1_ref[0], precision=HIGHEST, preferred_element_type=F32) + b1_ref[0]))
    o_ref[0] = jnp.sin(fr * (jnp.dot(h, w2_ref[0], precision=HIGHEST, preferred_element_type=F32) + b2_ref[0]))


def _filt_kernel(h_ref, w3f_ref, w3b_ref, dec_ref, o_ref):
    L = dec_ref.shape[0] // 2
    hf = jnp.dot(h_ref[0, 0:L], w3f_ref[0], precision=HIGHEST, preferred_element_type=F32) * dec_ref[0:L]
    hb = jnp.dot(h_ref[0, L:2 * L], w3b_ref[0], precision=HIGHEST, preferred_element_type=F32) * dec_ref[L:2 * L]
    row = lax.broadcasted_iota(jnp.int32, hb.shape, 0)
    hb = jnp.where(row == 0, 0.0, hb)
    nrm = jnp.sum(jnp.abs(hf), axis=0, keepdims=True) + jnp.sum(jnp.abs(hb), axis=0, keepdims=True)
    o_ref[0, 0, 0:L] = hf / nrm
    o_ref[0, 0, L:2 * L] = hb / nrm


def _circular_order(a):
    return np.concatenate([a, a[:1], a[1:][::-1]], axis=0)


def _hyena_pos_tables(L):
    f32 = np.float32
    pos = np.arange(L, dtype=f32)
    t = (pos / f32(L)).astype(f32)
    bands = np.linspace(1e-4, HY_POS_BANDS - 1, HY_POS_BANDS, dtype=f32)
    ang = (f32(2 * math.pi / L) * pos[:, None] * bands[None, :]).astype(np.float64)
    feats = np.zeros((L, HY_FILT_HIDDEN), f32)
    feats[:, 0] = t
    feats[:, 1:1 + HY_POS_BANDS] = np.cos(ang)
    feats[:, 1 + HY_POS_BANDS:HY_POS_DIM] = -np.sin(ang)
    deltas = np.linspace(math.log(HY_DECAY_TARGET) / HY_SLOW_DECAY,
                         math.log(HY_DECAY_TARGET) / HY_FAST_DECAY, BRANCH_W, dtype=f32)
    decay = np.exp((-t[:, None] * np.abs(deltas)[None, :]).astype(np.float64)).astype(f32)
    return jnp.asarray(_circular_order(feats)), jnp.asarray(_circular_order(decay))


def _hyena_filters(half, w1p, b1, w2, b2, w3, freq):
    feats, decay = _hyena_pos_tables(half)
    L = 2 * half
    cb = LANES
    ncb = BRANCH_W // cb
    small = lambda shape: pl.BlockSpec((1,) + shape, lambda l: (l, 0, 0))
    hidden = pl.pallas_call(
        _filt_hidden_kernel,
        grid=(DEPTH,),
        in_specs=[
            pl.BlockSpec((L, HY_FILT_HIDDEN), lambda l: (0, 0)),
            small((HY_FILT_HIDDEN, HY_FILT_HIDDEN)), small((1, HY_FILT_HIDDEN)),
            small((HY_FILT_HIDDEN, HY_FILT_HIDDEN)), small((1, HY_FILT_HIDDEN)),
            small((1, HY_FILT_HIDDEN)),
        ],
        out_specs=pl.BlockSpec((1, L, HY_FILT_HIDDEN), lambda l: (l, 0, 0)),
        out_shape=jax.ShapeDtypeStruct((DEPTH, L, HY_FILT_HIDDEN), F32),
        compiler_params=_cparams("arbitrary"),
        name=f"hyena_filter_hidden_{L}",
    )(feats, w1p, b1, w2, b2, freq)
    return pl.pallas_call(
        _filt_kernel,
        grid=(DEPTH, 2, ncb),
        in_specs=[
            pl.BlockSpec((1, L, HY_FILT_HIDDEN), lambda l, o, c: (l, 0, 0)),
            pl.BlockSpec((1, HY_FILT_HIDDEN, cb), lambda l, o, c: (l, 0, o * 2 * ncb + c)),
            pl.BlockSpec((1, HY_FILT_HIDDEN, cb), lambda l, o, c: (l, 0, o * 2 * ncb + ncb + c)),
            pl.BlockSpec((L, cb), lambda l, o, c: (0, c)),
        ],
        out_specs=pl.BlockSpec((1, 1, L, cb), lambda l, o, c: (l, o, 0, c)),
        out_shape=jax.ShapeDtypeStruct((DEPTH, 2, L, BRANCH_W), F32),
        compiler_params=_cparams("arbitrary", "arbitrary", "arbitrary"),
        name=f"hyena_filters_{L}",
    )(hidden, w3, w3, decay)


def _short_conv(u, w_ref, b_ref, seq_len):
    n = u.shape[0]
    t = lax.broadcasted_iota(jnp.int32, u.shape, 0) % seq_len
    prev = jnp.where(t == 0, 0.0, pltpu.roll(u, 1, axis=0))
    nxt = jnp.where(t == seq_len - 1, 0.0, pltpu.roll(u, n - 1, axis=0))
    return prev * w_ref[0:1, :] + u * w_ref[1:2, :] + nxt * w_ref[2:3, :] + b_ref[...]


def _dft_direct_mats():
    n, half = 2 * SEQ, SEQ
    k = np.arange(n)[:, None].astype(np.float64)
    t = np.arange(half)[None, :].astype(np.float64)
    ang = 2 * np.pi * k * t / n
    fr, fi = np.cos(ang), -np.sin(ang)
    mf = np.block([[fr, -fi], [fi, fr]])
    gr, gi = np.cos(ang).T / n, np.sin(ang).T / n
    mi = np.block([[gr, -gi], [gi, gr]])
    return mf.astype(np.float32), mi.astype(np.float32)


def _dft_real_mat():
    n = 2 * SEQ
    ang = 2 * np.pi * np.arange(n)[:, None].astype(np.float64) * np.arange(n)[None, :] / n
    return np.concatenate([np.cos(ang), -np.sin(ang)], axis=0).astype(np.float32)


def _spec_direct_kernel(h_ref, m_ref, o_ref):
    o_ref[0, 0] = jnp.dot(m_ref[...], h_ref[0, 0], precision=HIGHEST, preferred_element_type=F32)


def _spec_direct(h, m_real):
    n = 2 * SEQ
    return pl.pallas_call(
        _spec_direct_kernel,
        grid=(DEPTH, 2),
        in_specs=[pl.BlockSpec((1, 1, n, BRANCH_W), lambda l, o: (l, o, 0, 0)),
                  pl.BlockSpec((2 * n, n), lambda l, o: (0, 0))],
        out_specs=pl.BlockSpec((1, 1, 2 * n, BRANCH_W), lambda l, o: (l, o, 0, 0)),
        out_shape=jax.ShapeDtypeStruct((DEPTH, 2, 2 * n, BRANCH_W), F32),
        compiler_params=_cparams("arbitrary", "arbitrary"),
        name="hyena_spectrum_direct",
    )(h, m_real)


def _lconv_direct_kernel(s_ref, g_ref, cws_ref, cbs_ref, cwg_ref, cbg_ref, h_ref, bias_ref, mf_ref, mi_ref, o_ref,
                         *, conv_sig):
    n = 2 * SEQ
    sig = s_ref[...].astype(F32)
    if conv_sig:
        sig = _short_conv(sig, cws_ref, cbs_ref, SEQ)
    gate = _short_conv(g_ref[...].astype(F32), cwg_ref, cbg_ref, SEQ)
    z = jnp.dot(mf_ref[...], sig.astype(BF16), preferred_element_type=F32)
    zr, zi = z[0:n], z[n:2 * n]
    hr, hi = h_ref[0:n], h_ref[n:2 * n]
    y = jnp.concatenate([zr * hr - zi * hi, zr * hi + zi * hr], axis=0)
    y = jnp.dot(mi_ref[...], y.astype(BF16), preferred_element_type=F32)
    o_ref[...] = gate * (y + sig * bias_ref[...])


def _lconv_direct(sig, sig_col, gate_src, gate_col, conv_w, conv_b, spec, l, order, bias, mf, mi, conv_sig):
    n = 2 * SEQ
    rows = 2 * SEQ
    T = sig.shape[0]
    return pl.pallas_call(
        functools.partial(_lconv_direct_kernel, conv_sig=conv_sig),
        grid=(T // rows,),
        in_specs=[
            pl.BlockSpec((rows, BRANCH_W), lambda p: (p, sig_col)),
            pl.BlockSpec((rows, BRANCH_W), lambda p: (p, gate_col)),
            pl.BlockSpec((3, BRANCH_W), lambda p: (0, 0)),
            pl.BlockSpec((1, BRANCH_W), lambda p: (0, 0)),
            pl.BlockSpec((3, BRANCH_W), lambda p: (0, gate_col)),
            pl.BlockSpec((1, BRANCH_W), lambda p: (0, gate_col)),
            pl.BlockSpec((None, None, 2 * n, BRANCH_W), lambda p: (l, order, 0, 0)),
            pl.BlockSpec((1, BRANCH_W), lambda p: (0, 0)),
            pl.BlockSpec((2 * n, rows), lambda p: (0, 0)),
            pl.BlockSpec((rows, 2 * n), lambda p: (0, 0)),
        ],
        out_specs=pl.BlockSpec((rows, BRANCH_W), lambda p: (p, 0)),
        out_shape=jax.ShapeDtypeStruct((T, BRANCH_W), F32),
        compiler_params=_cparams("arbitrary"),
        name="hyena_lconv_direct",
    )(sig, gate_src, conv_w, conv_b, conv_w, conv_b, spec, bias, mf, mi)


def _dft_two_stage_mats():
    no, ni, half, n = FFT_NO, FFT_NI, FFT_HALF, FFT_N
    f64 = np.float64
    k1 = np.arange(no, dtype=f64)
    n_o = np.arange(half, dtype=f64)
    n_i = np.arange(ni, dtype=f64)
    ang = 2 * np.pi * (n_i[:, None, None] * k1[None, :, None] / n + k1[None, :, None] * n_o[None, None, :] / no)
    tr, ti = np.cos(ang), -np.sin(ang)
    m1 = np.concatenate([np.concatenate([tr, -ti], axis=2), np.concatenate([ti, tr], axis=2)], axis=1)
    k2 = np.arange(ni, dtype=f64)
    ang2 = 2 * np.pi * k2[:, None] * n_i[None, :] / ni
    f2r, f2i = np.cos(ang2), -np.sin(ang2)
    m2 = np.block([[f2r, -f2i], [f2i, f2r]])
    m2c = np.block([[f2r, f2i], [-f2i, f2r]])
    sr, si = np.transpose(tr, (0, 2, 1)) / n, -np.transpose(ti, (0, 2, 1)) / n
    m3 = np.concatenate([np.concatenate([sr, -si], axis=2), np.concatenate([si, sr], axis=2)], axis=1)
    return (m1.astype(np.float32), m2.astype(np.float32), m2c.astype(np.float32), m3.astype(np.float32))


def _dft_stage1_real_mat():
    no, ni, n = FFT_NO, FFT_NI, FFT_N
    k1 = np.arange(no, dtype=np.float64)
    n_o = np.arange(no, dtype=np.float64)
    n_i = np.arange(ni, dtype=np.float64)
    ang = 2 * np.pi * (n_i[:, None, None] * k1[None, :, None] / n + k1[None, :, None] * n_o[None, None, :] / no)
    return np.concatenate([np.cos(ang), -np.sin(ang)], axis=1).astype(np.float32)


def _store_stage1(w_ref, ni, out):
    w_ref[pl.ds(ni, FFT_NO, stride=2 * FFT_NI), :] = out[0:FFT_NO]
    w_ref[pl.ds(FFT_NI + ni, FFT_NO, stride=2 * FFT_NI), :] = out[FFT_NO:2 * FFT_NO]


def _fwd_stage1(za_ref, zb_ref, m1_ref, w_ref):
    def body(ni, carry):
        a = za_ref[pl.ds(ni, FFT_HALF, stride=FFT_NI), :]
        b = zb_ref[pl.ds(ni, FFT_HALF, stride=FFT_NI), :]
        out = jnp.dot(m1_ref[ni], jnp.concatenate([a, b], axis=0).astype(BF16), preferred_element_type=F32)
        _store_stage1(w_ref, ni, out)
        return carry

    lax.fori_loop(0, FFT_NI, body, 0, unroll=FFT_UNROLL)


def _spec_two_stage_kernel(h_ref, m1_ref, m2_ref, o_ref, w_ref):
    h = h_ref.at[0, 0]

    def stage1(ni, carry):
        a = h[pl.ds(ni, FFT_NO, stride=FFT_NI), :]
        _store_stage1(w_ref, ni, jnp.dot(m1_ref[ni], a.astype(BF16), preferred_element_type=F32))
        return carry

    lax.fori_loop(0, FFT_NI, stage1, 0, unroll=FFT_UNROLL)
    blk = 2 * FFT_NI

    cb = w_ref.shape[1]

    def stage2(kp, carry):
        rows = [pl.ds(pl.multiple_of((2 * kp + j) * blk, blk), blk) for j in range(2)]
        x = jnp.dot(m2_ref[...], jnp.concatenate([w_ref[r, :] for r in rows], axis=1).astype(BF16),
                    preferred_element_type=F32)
        for j in range(2):
            o_ref[0, 0, rows[j], :] = x[:, j * cb:(j + 1) * cb]
        return carry

    lax.fori_loop(0, FFT_NO // 2, stage2, 0, unroll=FFT_MID_UNROLL)


def _spec_two_stage(h, m1_real, m2):
    cb = LCONV_CB
    return pl.pallas_call(
        _spec_two_stage_kernel,
        grid=(DEPTH, 2, BRANCH_W // cb),
        in_specs=[pl.BlockSpec((1, 1, FFT_N, cb), lambda l, o, c: (l, o, 0, c)),
                  pl.BlockSpec((FFT_NI, 2 * FFT_NO, FFT_NO), lambda l, o, c: (0, 0, 0)),
                  pl.BlockSpec((2 * FFT_NI, 2 * FFT_NI), lambda l, o, c: (0, 0))],
        out_specs=pl.BlockSpec((1, 1, 2 * FFT_N, cb), lambda l, o, c: (l, o, 0, c)),
        out_shape=jax.ShapeDtypeStruct((DEPTH, 2, 2 * FFT_N, BRANCH_W), F32),
        scratch_shapes=[pltpu.VMEM((2 * FFT_N, cb), F32)],
        compiler_params=_cparams("arbitrary", "arbitrary", "arbitrary"),
        name="hyena_spectrum_two_stage",
    )(h, m1_real, m2)


def _lconv_two_stage_kernel(s_ref, g_ref, cws_ref, cbs_ref, cwg_ref, cbg_ref, h_ref, bias_ref,
                            m1_ref, m2_ref, m2c_ref, m3_ref, o_ref, z_ref, w_ref, *, conv_sig):
    for b in range(2):
        sig = s_ref[b].astype(F32)
        if conv_sig:
            sig = _short_conv(sig, cws_ref, cbs_ref, DEC_SEQ)
        z_ref[b] = sig
    _fwd_stage1(z_ref.at[0], z_ref.at[1], m1_ref, w_ref)
    blk = 2 * FFT_NI

    cb = w_ref.shape[1]

    def mid(kp, carry):
        rows = [pl.ds(pl.multiple_of((2 * kp + j) * blk, blk), blk) for j in range(2)]
        x = jnp.dot(m2_ref[...], jnp.concatenate([w_ref[r, :] for r in rows], axis=1).astype(BF16),
                    preferred_element_type=F32)
        h = jnp.concatenate([h_ref[r, :] for r in rows], axis=1)
        xr, xi = x[0:FFT_NI], x[FFT_NI:blk]
        hr, hi = h[0:FFT_NI], h[FFT_NI:blk]
        y = jnp.concatenate([xr * hr - xi * hi, xr * hi + xi * hr], axis=0)
        c = jnp.dot(m2c_ref[...], y.astype(BF16), preferred_element_type=F32)
        for j in range(2):
            w_ref[rows[j], :] = c[:, j * cb:(j + 1) * cb]
        return carry

    lax.fori_loop(0, FFT_NO // 2, mid, 0, unroll=FFT_MID_UNROLL)

    def last(ni, carry):
        cr = w_ref[pl.ds(ni, FFT_NO, stride=blk), :]
        ci = w_ref[pl.ds(FFT_NI + ni, FFT_NO, stride=blk), :]
        y = jnp.dot(m3_ref[ni], jnp.concatenate([cr, ci], axis=0).astype(BF16), preferred_element_type=F32)
        o_ref[0, pl.ds(ni, FFT_HALF, stride=FFT_NI), :] = y[0:FFT_HALF]
        o_ref[1, pl.ds(ni, FFT_HALF, stride=FFT_NI), :] = y[FFT_HALF:2 * FFT_HALF]
        return carry

    lax.fori_loop(0, FFT_NI, last, 0, unroll=FFT_UNROLL)
    for b in range(2):
        gate = _short_conv(g_ref[b].astype(F32), cwg_ref, cbg_ref, DEC_SEQ)
        sig = z_ref[b]
        o_ref[b] = gate * (o_ref[b] + sig * bias_ref[...])


def _lconv_two_stage(sig, sig_col, gate_src, gate_col, conv_w, conv_b, spec, l, order, bias, mats, conv_sig):
    cb = LCONV_CB
    ncb = BRANCH_W // cb
    m1, m2, m2c, m3 = mats
    const3 = lambda c, p: (0, 0, 0)
    const2 = lambda c, p: (0, 0)
    return pl.pallas_call(
        functools.partial(_lconv_two_stage_kernel, conv_sig=conv_sig),
        grid=(ncb, DEC_BATCH // 2),
        in_specs=[
            pl.BlockSpec((2, DEC_SEQ, cb), lambda c, p: (p, 0, sig_col * ncb + c)),
            pl.BlockSpec((2, DEC_SEQ, cb), lambda c, p: (p, 0, gate_col * ncb + c)),
            pl.BlockSpec((3, cb), lambda c, p: (0, c)),
            pl.BlockSpec((1, cb), lambda c, p: (0, c)),
            pl.BlockSpec((3, cb), lambda c, p: (0, gate_col * ncb + c)),
            pl.BlockSpec((1, cb), lambda c, p: (0, gate_col * ncb + c)),
            pl.BlockSpec((None, None, 2 * FFT_N, cb), lambda c, p: (l, order, 0, c)),
            pl.BlockSpec((1, cb), lambda c, p: (0, c)),
            pl.BlockSpec(m1.shape, const3),
            pl.BlockSpec(m2.shape, const2),
            pl.BlockSpec(m2c.shape, const2),
            pl.BlockSpec(m3.shape, const3),
        ],
        out_specs=pl.BlockSpec((2, DEC_SEQ, cb), lambda c, p: (p, 0, c)),
        out_shape=jax.ShapeDtypeStruct((DEC_BATCH, DEC_SEQ, BRANCH_W), F32),
        scratch_shapes=[pltpu.VMEM((2, DEC_SEQ, cb), F32), pltpu.VMEM((2 * FFT_N, cb), F32)],
        compiler_params=_cparams("arbitrary", "arbitrary"),
        name="hyena_lconv_two_stage",
    )(sig, gate_src, conv_w, conv_b, conv_w, conv_b, spec, bias, m1, m2, m2c, m3)


def kernel(x_prompt, x_sample, cache_na_k, cache_na_v, cache_da_k, cache_da_v, c, c_ctx, w_ada, b_ada, norm_mix,
           norm_ffn, w_in, hy_conv_w, hy_conv_b, hy_filt_w1, hy_filt_b1, hy_filt_w2, hy_filt_b2, hy_filt_w3,
           hy_filt_freq, hy_bias, na_rpb, da_lambda, da_subln, w_lift, w_out, w_ffn_in, w_ffn_out, norm_final):
    TP, TS = BATCH * SEQ, DEC_BATCH * DEC_SEQ
    xp = x_prompt.reshape(TP, D_MODEL)
    xs = x_sample.reshape(TS, D_MODEL)

    cc = jnp.concatenate([c_ctx[None, :], c, jnp.zeros((8 - 1 - DEC_BATCH, D_MODEL), F32)], axis=0)
    mod = _modulation(cc, w_ada, b_ada)
    mod_p = mod[:, 0:1].reshape(DEPTH, 1, 1, 6 * D_MODEL)
    mod_s = mod[:, 1:1 + DEC_BATCH].reshape(DEPTH, DEC_BATCH, 1, 6 * D_MODEL)

    w_mix = w_in[:, :, :MIX_W].astype(BF16)
    w_gate = w_in[:, :, MIX_W:].astype(BF16)
    w_lift_b = w_lift.astype(BF16)
    w_out_b = w_out.astype(BF16)
    w_ffn_in_b = w_ffn_in.astype(BF16)
    w_ffn_out_b = w_ffn_out.astype(BF16)
    g_mix = norm_mix.reshape(DEPTH, 1, D_MODEL)
    g_ffn = norm_ffn.reshape(DEPTH, 1, D_MODEL)
    g_fin = norm_final.reshape(1, D_MODEL)
    subln = da_subln.reshape(DEPTH, 1, DA_V_DIM)
    subln_col = da_subln.reshape(DEPTH, DA_V_DIM, 1)

    w1p = jnp.pad(hy_filt_w1, ((0, 0), (0, HY_FILT_HIDDEN - HY_POS_DIM), (0, 0)))
    b1 = hy_filt_b1.reshape(DEPTH, 1, HY_FILT_HIDDEN)
    b2 = hy_filt_b2.reshape(DEPTH, 1, HY_FILT_HIDDEN)
    fr = hy_filt_freq.reshape(DEPTH, 1, HY_FILT_HIDDEN)
    mf, mi = _dft_direct_mats()
    mats = _dft_two_stage_mats()
    h_p = _hyena_filters(SEQ, w1p, b1, hy_filt_w2, b2, hy_filt_w3, fr)
    h_s = _hyena_filters(DEC_SEQ, w1p, b1, hy_filt_w2, b2, hy_filt_w3, fr)
    spec_p = _spec_direct(h_p, jnp.asarray(_dft_real_mat()))
    mf_b, mi_b = jnp.asarray(mf, dtype=BF16), jnp.asarray(mi, dtype=BF16)
    mats_b = tuple(jnp.asarray(m, dtype=BF16) for m in mats)
    spec_s = _spec_two_stage(h_s, jnp.asarray(_dft_stage1_real_mat(), dtype=BF16), mats_b[1])
    conv_b = hy_conv_b.reshape(DEPTH, 1, 3 * BRANCH_W)

    na_bias = _na_bias_table(na_rpb)
    rope_tables = _rope_tables()
    ck_na = cache_na_k.reshape(DEC_BATCH, DEPTH, PAST_LEN, BRANCH_W)
    cv_na = cache_na_v.reshape(DEC_BATCH, DEPTH, PAST_LEN, BRANCH_W)
    ck_da = cache_da_k.reshape(DEC_BATCH, DEPTH, PAST_LEN, BRANCH_W)
    cv_da = cache_da_v.reshape(DEC_BATCH, DEPTH, PAST_LEN, BRANCH_W)

    caches = tuple(jnp.zeros((BATCH, DEPTH, SEQ, BRANCH_W), F32) for _ in CACHE_BLOCKS)
    for l in range(DEPTH):
        lam_init = 0.8 - 0.6 * math.exp(-0.3 * l)
        final = l == DEPTH - 1

        u, caches = _in_proj(xp, g_mix[l], mod_p[l], w_mix, l, TP, BF16, caches=caches)
        z1 = _lconv_direct(u, 0, u, 1, hy_conv_w[l], conv_b[l], spec_p, l, 0, hy_bias[l, 0:1], mf_b, mi_b, True)
        y_hy = _lconv_direct(z1, 0, u, 2, hy_conv_w[l], conv_b[l], spec_p, l, 1, hy_bias[l, 1:2], mf_b, mi_b, False)
        y_na, y_da = _ctx_attention(u, da_lambda[l], subln_col[l], lam_init)
        xp = _merge_out(xp, g_mix[l], mod_p[l], y_hy, y_na, y_da, w_gate, w_lift_b, w_out_b, l, TP)
        xp = _ffn(xp, g_ffn[l], mod_p[l], w_ffn_in_b, w_ffn_out_b, g_fin, l, TP, final)

        u = _in_proj(xs, g_mix[l], mod_s[l], w_mix, l, DEC_SEQ, BF16)
        u3 = u.reshape(DEC_BATCH, DEC_SEQ, MIX_W)
        z1 = _lconv_two_stage(u3, 0, u3, 1, hy_conv_w[l], conv_b[l], spec_s, l, 0, hy_bias[l, 0:1], mats_b, True)
        y_hy = _lconv_two_stage(z1, 0, u3, 2, hy_conv_w[l], conv_b[l], spec_s, l, 1, hy_bias[l, 1:2], mats_b, False)
        y_hy = y_hy.reshape(TS, BRANCH_W)
        qn, kn, vn = _attn_prep(u, 3, ck_na[:, l], cv_na[:, l], NA_HEAD_DIM)
        y_na = _nbr_attention(qn, kn, vn, na_bias, l)
        q, kt, v = _attn_prep(u, 6, ck_da[:, l], cv_da[:, l], DA_HEAD_DIM, rope_tables)
        y_da = _diff_attention(q, kt, v, da_lambda[l], subln_col[l], lam_init)
        xs = _merge_out(xs, g_mix[l], mod_s[l], y_hy, y_na, y_da, w_gate, w_lift_b, w_out_b, l, DEC_SEQ)
        xs = _ffn(xs, g_ffn[l], mod_s[l], w_ffn_in_b, w_ffn_out_b, g_fin, l, DEC_SEQ, final)

    y_prompt = xp.reshape(BATCH, SEQ, D_MODEL)
    y_sample = xs.reshape(DEC_BATCH, DEC_SEQ, D_MODEL)
    heads = lambda a, d: a.reshape(BATCH, DEPTH, SEQ, BRANCH_W // d, d)
    return (y_prompt, y_sample, heads(caches[0], NA_HEAD_DIM), heads(caches[1], NA_HEAD_DIM),
            heads(caches[2], 2 * DA_HEAD_DIM), heads(caches[3], DA_V_DIM))
```

```python
import functools
import math

import numpy as np
import jax
import jax.numpy as jnp
from jax import lax
from jax.experimental import pallas as pl
from jax.experimental.pallas import tpu as pltpu

F32 = jnp.float32
BF16 = jnp.bfloat16
HIGHEST = lax.Precision.HIGHEST

D_MODEL = 1024
BATCH = 32
SEQ = 256
DEPTH = 4
DEC_BATCH = 4
DEC_SEQ = 4096
PAST_LEN = 256
GRID_W = 64
GRID_H = DEC_SEQ // GRID_W
BRANCH_W = 512
HY_POS_BANDS = 16
HY_POS_DIM = 1 + 2 * HY_POS_BANDS
HY_FILT_HIDDEN = 64
HY_DECAY_TARGET = 1e-2
HY_FAST_DECAY = 0.3
HY_SLOW_DECAY = 1.5
NA_HEADS = 8
NA_HEAD_DIM = 64
NA_WIN_ROWS = 8
NA_WIN_COLS = 16
DA_HEADS = 8
DA_HEAD_DIM = 32
DA_V_DIM = 64
D_FF = 2816
MIX_W = 9 * BRANCH_W
ROPE_BASE = 10000.0
EPS = 1e-6
NEG_INF = -1e30

VMEM_LIMIT_BYTES = 56 * 1024 * 1024
LANES = 128
MXU_DIM = 256

FFT_N = 2 * DEC_SEQ
FFT_NO = 64
FFT_NI = 128
FFT_HALF = FFT_NO // 2
FFT_UNROLL = 8
FFT_MID_UNROLL = 16
LCONV_CB = LANES


def _cparams(*sem):
    return pltpu.CompilerParams(dimension_semantics=sem, vmem_limit_bytes=VMEM_LIMIT_BYTES)


def _sigmoid(x):
    return 1.0 / (1.0 + jnp.exp(-x))


def _rms(x, g):
    return x * lax.rsqrt(jnp.mean(x * x, axis=-1, keepdims=True) + EPS) * g


def _modnorm(x, g, shift, scale):
    return _rms(x, g) * (1.0 + scale) + shift


def _bdot(a, b):
    return jnp.dot(a.astype(BF16), b.astype(BF16), preferred_element_type=F32)


def _mod_kernel(c_ref, w_ref, b_ref, o_ref):
    c = c_ref[...]
    s = c * _sigmoid(c)
    o_ref[0] = jnp.dot(s, w_ref[0], precision=HIGHEST, preferred_element_type=F32) + b_ref[0]


def _modulation(cc, w_ada, b_ada):
    nt = 6
    return pl.pallas_call(
        _mod_kernel,
        grid=(DEPTH, nt),
        in_specs=[
            pl.BlockSpec((8, D_MODEL), lambda l, j: (0, 0)),
            pl.BlockSpec((1, D_MODEL, D_MODEL), lambda l, j: (l, 0, j)),
            pl.BlockSpec((1, 1, D_MODEL), lambda l, j: (l, 0, j)),
        ],
        out_specs=pl.BlockSpec((1, 8, D_MODEL), lambda l, j: (l, 0, j)),
        out_shape=jax.ShapeDtypeStruct((DEPTH, 8, 6 * D_MODEL), F32),
        compiler_params=_cparams("arbitrary", "arbitrary"),
        name="modulation",
    )(cc, w_ada, b_ada.reshape(DEPTH, 1, 6 * D_MODEL))


IN_TM = 512
CACHE_BLOCKS = (4, 5, 7, 8)


def _in_kernel(*refs, n_cache):
    x_ref, g_ref, mod_ref, w_ref = refs[:4]
    o_ref = refs[4 + n_cache]
    cache_refs = refs[5 + n_cache:]
    m = mod_ref[0]
    h = _modnorm(x_ref[...], g_ref[...], m[:, 0:D_MODEL], m[:, D_MODEL:2 * D_MODEL]).astype(BF16)
    res = jnp.dot(h, w_ref[...], preferred_element_type=F32)
    o_ref[...] = res.astype(o_ref.dtype)
    for c, c_ref in zip(CACHE_BLOCKS, cache_refs):
        c_ref[...] = res[:, c * BRANCH_W:(c + 1) * BRANCH_W].reshape(c_ref.shape)


def _in_proj(x, g, mod, w, l, rows_per_mod, out_dtype, caches=None):
    T = x.shape[0]
    tm = IN_TM
    per = rows_per_mod // tm
    in_specs = [
        pl.BlockSpec((tm, D_MODEL), lambda i: (i, 0)),
        pl.BlockSpec((1, D_MODEL), lambda i: (0, 0)),
        pl.BlockSpec((1, 1, 6 * D_MODEL), lambda i: (i // per, 0, 0)),
        pl.BlockSpec((None, D_MODEL, MIX_W), lambda i: (l, 0, 0), pipeline_mode=pl.Buffered(1)),
    ]
    out_specs = [pl.BlockSpec((tm, MIX_W), lambda i: (i, 0))]
    out_shape = [jax.ShapeDtypeStruct((T, MIX_W), out_dtype)]
    args = [x, g, mod, w]
    aliases = {}
    if caches is not None:
        out_specs += [pl.BlockSpec((tm // SEQ, 1, SEQ, BRANCH_W), lambda i: (i, l, 0, 0))] * len(caches)
        out_shape += [jax.ShapeDtypeStruct(c.shape, c.dtype) for c in caches]
        in_specs += [pl.BlockSpec(memory_space=pl.ANY)] * len(caches)
        aliases = {4 + n: 1 + n for n in range(len(caches))}
        args += list(caches)
    outs = pl.pallas_call(
        functools.partial(_in_kernel, n_cache=len(args) - 4),
        grid=(T // tm,),
        in_specs=in_specs,
        out_specs=out_specs,
        out_shape=out_shape,
        input_output_aliases=aliases,
        compiler_params=_cparams("arbitrary"),
        name="in_proj",
    )(*args)
    return outs[0] if caches is None else (outs[0], tuple(outs[1:]))


def _mid_kernel(x_ref, g_ref, mod_ref, yh_ref, yn_ref, yd_ref, wg_ref, wl_ref, wo_ref, o_ref):
    m = mod_ref[0]
    x = x_ref[...]
    h = _modnorm(x, g_ref[...], m[:, 0:D_MODEL], m[:, D_MODEL:2 * D_MODEL]).astype(BF16)
    merged = None
    for br, y_ref in enumerate((yh_ref, yn_ref, yd_ref)):
        gate = _sigmoid(jnp.dot(h, wg_ref[:, br * D_MODEL:(br + 1) * D_MODEL], preferred_element_type=F32))
        lift = jnp.dot(y_ref[...].astype(BF16), wl_ref[br], preferred_element_type=F32)
        t = gate * lift
        merged = t if merged is None else merged + t
    o_ref[...] = x + m[:, 2 * D_MODEL:3 * D_MODEL] * _bdot(merged, wo_ref[...])


def _merge_out(x, g, mod, y_hy, y_na, y_da, w_gate, w_lift, w_out, l, rows_per_mod):
    T = x.shape[0]
    tm = 512
    per = rows_per_mod // tm
    row = lambda i: (i, 0)
    const2 = lambda i: (0, 0)
    return pl.pallas_call(
        _mid_kernel,
        grid=(T // tm,),
        in_specs=[
            pl.BlockSpec((tm, D_MODEL), row),
            pl.BlockSpec((1, D_MODEL), const2),
            pl.BlockSpec((1, 1, 6 * D_MODEL), lambda i: (i // per, 0, 0)),
            pl.BlockSpec((tm, BRANCH_W), row),
            pl.BlockSpec((tm, BRANCH_W), row),
            pl.BlockSpec((tm, BRANCH_W), row),
            pl.BlockSpec((None, D_MODEL, 3 * D_MODEL), lambda i: (l, 0, 0)),
            pl.BlockSpec((None, 3, BRANCH_W, D_MODEL), lambda i: (l, 0, 0, 0)),
            pl.BlockSpec((None, D_MODEL, D_MODEL), lambda i: (l, 0, 0)),
        ],
        out_specs=pl.BlockSpec((tm, D_MODEL), row),
        out_shape=jax.ShapeDtypeStruct((T, D_MODEL), F32),
        compiler_params=_cparams("arbitrary"),
        name="merge_out",
    )(x, g, mod, y_hy, y_na, y_da, w_gate, w_lift, w_out)


def _ffn_kernel(x_ref, g_ref, mod_ref, w1_ref, w2_ref, gf_ref, o_ref, *, final):
    m = mod_ref[0]
    x = x_ref[...]
    h = _modnorm(x, g_ref[...], m[:, 3 * D_MODEL:4 * D_MODEL], m[:, 4 * D_MODEL:5 * D_MODEL]).astype(BF16)
    a = jnp.dot(h, w1_ref[:, 0:D_FF], preferred_element_type=F32)
    b = jnp.dot(h, w1_ref[:, D_FF:2 * D_FF], preferred_element_type=F32)
    xn = x + m[:, 5 * D_MODEL:6 * D_MODEL] * _bdot(a * _sigmoid(a) * b, w2_ref[...])
    if final:
        xn = _rms(xn, gf_ref[...])
    o_ref[...] = xn


def _ffn(x, g, mod, w_ffn_in, w_ffn_out, g_final, l, rows_per_mod, final):
    T = x.shape[0]
    tm = 512
    per = rows_per_mod // tm
    resident = pl.Buffered(1)
    return pl.pallas_call(
        functools.partial(_ffn_kernel, final=final),
        grid=(T // tm,),
        in_specs=[
            pl.BlockSpec((tm, D_MODEL), lambda i: (i, 0)),
            pl.BlockSpec((1, D_MODEL), lambda i: (0, 0)),
            pl.BlockSpec((1, 1, 6 * D_MODEL), lambda i: (i // per, 0, 0)),
            pl.BlockSpec((None, D_MODEL, 2 * D_FF), lambda i: (l, 0, 0), pipeline_mode=resident),
            pl.BlockSpec((None, D_FF, D_MODEL), lambda i: (l, 0, 0), pipeline_mode=resident),
            pl.BlockSpec((1, D_MODEL), lambda i: (0, 0)),
        ],
        out_specs=pl.BlockSpec((tm, D_MODEL), lambda i: (i, 0)),
        out_shape=jax.ShapeDtypeStruct((T, D_MODEL), F32),
        compiler_params=_cparams("arbitrary"),
        name="ffn",
    )(x, g, mod, w_ffn_in, w_ffn_out, g_final)


def _da_lambda(lam_ref, lam_init):
    lp = lam_ref[...]
    a = jnp.sum(lp[0:1] * lp[1:2], axis=1, keepdims=True)
    b = jnp.sum(lp[2:3] * lp[3:4], axis=1, keepdims=True)
    return jnp.exp(a) - jnp.exp(b) + lam_init


ATT_ONES_ROWS = 16
ATT_TQ = 256
ATT_PREP_T = 512
ATT_KEYS = DEC_SEQ + PAST_LEN
ATT_MIN_DENOM = 2.0 ** -64
LOG2E = math.log2(math.e)


def _masked_q_blocks(qt, d):
    row = lax.broadcasted_iota(jnp.int32, qt.shape, 0)
    zero = jnp.zeros_like(qt)
    return jnp.concatenate([jnp.where((row >= j * d) & (row < (j + 1) * d), qt, zero) for j in range(LANES // d)],
                           axis=1)


def _colmax(st):
    keys, n = st.shape
    return jnp.max(jnp.max(st.reshape(keys // MXU_DIM, MXU_DIM, n), axis=0), axis=0, keepdims=True)


def _ctx_attn_kernel(nq_ref, nk_ref, nv_ref, dq_ref, dk_ref, dv_ref, lam_ref, sub_ref, yn_ref, yd_ref, acc_ref,
                     *, lam_init):
    lam = _da_lambda(lam_ref, lam_init)
    ones = jnp.ones((ATT_ONES_ROWS, SEQ), BF16)

    def attend(q_ref, k_ref, v_ref, d, maps_per_head, finish):
        qt = (q_ref[...].astype(F32) * (d ** -0.5 * LOG2E)).T.astype(BF16)
        vt = v_ref[...].astype(F32).T.astype(BF16)
        kb = k_ref[...].astype(BF16)
        dv = NA_HEAD_DIM
        heads_per_group = LANES // (d * maps_per_head)
        w = maps_per_head * SEQ
        for g in range(BRANCH_W // LANES):
            lanes = slice(g * LANES, (g + 1) * LANES)
            st = jnp.dot(kb[:, lanes], _masked_q_blocks(qt[lanes], d), preferred_element_type=F32)
            pt = jnp.exp2(st - _colmax(st)).astype(BF16)
            for j in range(heads_per_group):
                h = g * heads_per_group + j
                ve = jnp.concatenate([vt[h * dv:(h + 1) * dv], ones], axis=0)
                oe = jnp.dot(ve, pt[:, j * w:(j + 1) * w], preferred_element_type=F32)
                os = [oe[0:dv, i * SEQ:(i + 1) * SEQ] / oe[dv:dv + 1, i * SEQ:(i + 1) * SEQ]
                      for i in range(maps_per_head)]
                acc_ref[h * dv:(h + 1) * dv, :] = finish(os)

    attend(nq_ref, nk_ref, nv_ref, NA_HEAD_DIM, 1, lambda os: os[0])
    yn_ref[...] = acc_ref[...].T.astype(yn_ref.dtype)

    def da_finish(os):
        ot = os[0] - lam * os[1]
        ot = ot * lax.rsqrt(jnp.mean(ot * ot, axis=0, keepdims=True) + EPS) * sub_ref[...]
        return ot * (1.0 - lam_init)

    attend(dq_ref, dk_ref, dv_ref, DA_HEAD_DIM, 2, da_finish)
    yd_ref[...] = acc_ref[...].T.astype(yd_ref.dtype)


def _ctx_attention(u, da_lambda, subln_col, lam_init):
    col = lambda j: pl.BlockSpec((SEQ, BRANCH_W), lambda b, j=j: (b, j))
    out = pl.BlockSpec((SEQ, BRANCH_W), lambda b: (b, 0))
    shape = jax.ShapeDtypeStruct((BATCH * SEQ, BRANCH_W), BF16)
    return pl.pallas_call(
        functools.partial(_ctx_attn_kernel, lam_init=lam_init),
        grid=(BATCH,),
        in_specs=[col(3), col(4), col(5), col(6), col(7), col(8),
                  pl.BlockSpec((4, DA_HEAD_DIM), lambda b: (0, 0)),
                  pl.BlockSpec((DA_V_DIM, 1), lambda b: (0, 0))],
        out_specs=[out, out],
        out_shape=[shape, shape],
        scratch_shapes=[pltpu.VMEM((BRANCH_W, SEQ), F32)],
        compiler_params=_cparams("arbitrary"),
        name="ctx_attention",
    )(u, u, u, u, u, u, da_lambda, subln_col)


def _rope(x, cos, sin_signed):
    n = x.shape[-1]
    lane = lax.broadcasted_iota(jnp.int32, x.shape, 1)
    partner = jnp.where(lane % 2 == 0, pltpu.roll(x, n - 1, axis=1), pltpu.roll(x, 1, axis=1))
    return x * cos + partner * sin_signed


def _attn_prep_kernel(q_ref, k_ref, v_ref, kc_ref, vc_ref, *refs, rope, scale):
    cos_ref, sin_ref = refs[:2] if rope else (None, None)
    qt_ref, ko_ref, vt_ref = refs[-3:]
    t = pl.program_id(1)
    dv = NA_HEAD_DIM

    def put_v(v):
        n = v.shape[0]
        vt = v.astype(F32).T.astype(BF16)
        ones = jnp.ones((ATT_ONES_ROWS, n), BF16)
        for h in range(BRANCH_W // dv):
            vt_ref[0, h, 0:dv, 0:n] = vt[h * dv:(h + 1) * dv]
            vt_ref[0, h, dv:dv + ATT_ONES_ROWS, 0:n] = ones

    @pl.when(t < DEC_SEQ // ATT_PREP_T)
    def _():
        q = q_ref[...].astype(F32)
        k = k_ref[...].astype(F32)
        if rope:
            q = _rope(q, cos_ref[...], sin_ref[...])
            k = _rope(k, cos_ref[...], sin_ref[...])
        qt_ref[0] = (q * scale).T.astype(BF16)
        ko_ref[0] = k.astype(BF16)
        put_v(v_ref[...])

    @pl.when(t == DEC_SEQ // ATT_PREP_T)
    def _():
        ko_ref[0, 0:PAST_LEN, :] = kc_ref[0].astype(BF16)
        put_v(vc_ref[0])


def _attn_prep(u, first_col, k_ctx, v_ctx, head_dim, rope_tables=None):
    rope = rope_tables is not None
    tile = ATT_PREP_T
    nt = DEC_SEQ // tile
    last = nt - 1
    rowblk = lambda j: pl.BlockSpec((tile, BRANCH_W), lambda b, t, j=j: (b * nt + jnp.minimum(t, last), j))
    tab = pl.BlockSpec((tile, BRANCH_W), lambda b, t: (jnp.minimum(t, last), 0))
    ctx = pl.BlockSpec((1, PAST_LEN, BRANCH_W), lambda b, t: (b, 0, 0))
    heads = BRANCH_W // NA_HEAD_DIM
    vrows = NA_HEAD_DIM + ATT_ONES_ROWS
    return pl.pallas_call(
        functools.partial(_attn_prep_kernel, rope=rope, scale=head_dim ** -0.5 * LOG2E),
        grid=(DEC_BATCH, nt + 1),
        in_specs=[rowblk(first_col), rowblk(first_col + 1), rowblk(first_col + 2), ctx, ctx] + [tab, tab] * rope,
        out_specs=[
            pl.BlockSpec((1, BRANCH_W, tile), lambda b, t: (b, 0, jnp.minimum(t, last))),
            pl.BlockSpec((1, tile, BRANCH_W), lambda b, t: (b, t, 0)),
            pl.BlockSpec((1, heads, vrows, tile), lambda b, t: (b, 0, 0, t)),
        ],
        out_shape=[
            jax.ShapeDtypeStruct((DEC_BATCH, BRANCH_W, DEC_SEQ), BF16),
            jax.ShapeDtypeStruct((DEC_BATCH, ATT_KEYS, BRANCH_W), BF16),
            jax.ShapeDtypeStruct((DEC_BATCH, heads, vrows, ATT_KEYS), BF16),
        ],
        compiler_params=_cparams("arbitrary", "arbitrary"),
        name="attn_prep",
    )(u, u, u, k_ctx, v_ctx, *(rope_tables or ()))


NA_ROWS = ATT_TQ // GRID_W
NA_UNION = 3 * NA_ROWS
NA_STEPS = GRID_H // NA_ROWS
NA_SLABS = NA_UNION // NA_ROWS
NA_VARIANT_OFFSET = (0, -NA_ROWS, -2 * NA_ROWS)


def _na_variant(s):
    return jnp.minimum(s, 1) + s // (NA_STEPS - 1)


def _na_window_block(s):
    return jnp.clip(s - 1, 0, NA_STEPS - NA_SLABS)


def _na_bias_kernel(rpb_ref, o_ref):
    kc = lax.broadcasted_iota(jnp.int32, (GRID_W, GRID_W), 0)
    qc = lax.broadcasted_iota(jnp.int32, (GRID_W, GRID_W), 1)
    dc = jnp.clip(kc - qc, -(NA_WIN_COLS - 1), NA_WIN_COLS - 1) + (NA_WIN_COLS - 1)
    c0 = jnp.clip(qc - NA_WIN_COLS // 2, 0, GRID_W - NA_WIN_COLS)
    col_ok = (kc >= c0) & (kc < c0 + NA_WIN_COLS)
    r = rpb_ref[0, 0] * LOG2E
    masked = jnp.full((GRID_W, GRID_W), NEG_INF, F32)
    tiles = []
    for dr in range(2 * NA_WIN_ROWS - 1):
        acc = jnp.zeros((GRID_W, GRID_W), F32)
        for d in range(2 * NA_WIN_COLS - 1):
            acc = jnp.where(dc == d, r[dr:dr + 1, d:d + 1], acc)
        tiles.append(jnp.where(col_ok, acc, masked))
    for v, off in enumerate(NA_VARIANT_OFFSET):
        for kr in range(NA_UNION):
            for rr in range(NA_ROWS):
                w0 = (0, rr, NA_UNION - NA_WIN_ROWS)[v]
                dr = kr + off - rr
                inside = w0 <= kr < w0 + NA_WIN_ROWS
                o_ref[0, v, 0, kr * GRID_W:(kr + 1) * GRID_W, rr * GRID_W:(rr + 1) * GRID_W] = (
                    tiles[dr + NA_WIN_ROWS - 1] if inside else masked)


def _na_bias_table(na_rpb):
    n_dr, n_dc = 2 * NA_WIN_ROWS - 1, 2 * NA_WIN_COLS - 1
    nv = len(NA_VARIANT_OFFSET)
    return pl.pallas_call(
        _na_bias_kernel,
        grid=(DEPTH, NA_HEADS),
        in_specs=[pl.BlockSpec((1, 1, n_dr, n_dc), lambda l, h: (l, h, 0, 0))],
        out_specs=pl.BlockSpec((1, nv, 1, NA_UNION * GRID_W, ATT_TQ), lambda l, h: (l, 0, h, 0, 0)),
        out_shape=jax.ShapeDtypeStruct((DEPTH, nv, NA_HEADS, NA_UNION * GRID_W, ATT_TQ), F32),
        compiler_params=_cparams("arbitrary", "arbitrary"),
        name="na_bias_table",
    )(na_rpb)


def _na_kernel(qt_ref, *refs):
    n = NA_SLABS + 1
    k_refs, vt_refs = refs[:n], refs[n:2 * n]
    bias_ref, o_ref, acc_ref = refs[2 * n:]
    dv = NA_HEAD_DIM
    heads_per_group = LANES // dv
    for g in range(BRANCH_W // LANES):
        lanes = slice(g * LANES, (g + 1) * LANES)
        qbd = _masked_q_blocks(qt_ref[0, lanes, :], dv)
        keys = jnp.concatenate([k_ref[0, :, lanes] for k_ref in k_refs], axis=0)
        st = jnp.dot(keys, qbd, preferred_element_type=F32)
        n_win = NA_SLABS * ATT_TQ
        st_win = st[0:n_win] + jnp.concatenate(
            [bias_ref[0, g * heads_per_group + hh] for hh in range(heads_per_group)], axis=1)
        st_ctx = st[n_win:]
        mx = jnp.maximum(_colmax(st_win), _colmax(st_ctx))
        pt = jnp.concatenate([jnp.exp2(st_win - mx), jnp.exp2(st_ctx - mx)], axis=0).astype(BF16)
        for hh in range(heads_per_group):
            h = g * heads_per_group + hh
            ve = jnp.concatenate([vt_ref[0, h] for vt_ref in vt_refs], axis=1)
            oe = jnp.dot(ve, pt[:, hh * ATT_TQ:(hh + 1) * ATT_TQ], preferred_element_type=F32)
            acc_ref[h * dv:(h + 1) * dv, :] = oe[0:dv] / oe[dv:dv + 1]
    o_ref[...] = acc_ref[...].T.astype(o_ref.dtype)


def _nbr_attention(qt, k, vt, bias, l):
    vrows = NA_HEAD_DIM + ATT_ONES_ROWS
    ctx_blk = DEC_SEQ // ATT_TQ
    k_specs = [pl.BlockSpec((1, ATT_TQ, BRANCH_W), lambda b, s, j=j: (b, _na_window_block(s) + j, 0))
               for j in range(NA_SLABS)]
    k_specs.append(pl.BlockSpec((1, ATT_TQ, BRANCH_W), lambda b, s: (b, ctx_blk, 0)))
    vt_specs = [pl.BlockSpec((1, NA_HEADS, vrows, ATT_TQ), lambda b, s, j=j: (b, 0, 0, _na_window_block(s) + j))
                for j in range(NA_SLABS)]
    vt_specs.append(pl.BlockSpec((1, NA_HEADS, vrows, ATT_TQ), lambda b, s: (b, 0, 0, ctx_blk)))
    n = NA_SLABS + 1
    return pl.pallas_call(
        _na_kernel,
        grid=(DEC_BATCH, NA_STEPS),
        in_specs=[pl.BlockSpec((1, BRANCH_W, ATT_TQ), lambda b, s: (b, 0, s))] + k_specs + vt_specs + [
            pl.BlockSpec((None, 1, NA_HEADS, NA_UNION * GRID_W, ATT_TQ), lambda b, s: (l, _na_variant(s), 0, 0, 0))],
        out_specs=pl.BlockSpec((ATT_TQ, BRANCH_W), lambda b, s: (b * NA_STEPS + s, 0)),
        out_shape=jax.ShapeDtypeStruct((DEC_BATCH * DEC_SEQ, BRANCH_W), BF16),
        scratch_shapes=[pltpu.VMEM((BRANCH_W, ATT_TQ), F32)],
        compiler_params=_cparams("arbitrary", "arbitrary"),
        name="nbr_attention",
    )(qt, *([k] * n), *([vt] * n), bias)


DA_TQ = ATT_TQ
DA_KEYS = ATT_KEYS
DA_ONES_ROWS = ATT_ONES_ROWS
DA_MAPS_PER_TILE = LANES // DA_HEAD_DIM


def _da_kernel(qt_ref, k_ref, vt_ref, lam_ref, sub_ref, o_ref, acc_ref, kn_ref, *, lam_init):
    lam = _da_lambda(lam_ref, lam_init)
    heads = DA_MAPS_PER_TILE // 2
    w = 2 * DA_TQ
    r = lax.broadcasted_iota(jnp.int32, (LANES, LANES), 0) // DA_HEAD_DIM
    c = lax.broadcasted_iota(jnp.int32, (LANES, LANES), 1) // DA_HEAD_DIM
    same_map = (r == c).astype(F32)

    @pl.when(pl.program_id(1) == 0)
    def _():
        for g in range(BRANCH_W // LANES):
            lanes = slice(g * LANES, (g + 1) * LANES)
            kf = k_ref[0, :, lanes].astype(F32)
            n2 = _bdot(kf * kf, same_map)
            kn_ref[:, lanes] = jnp.max(n2, axis=0, keepdims=True)

    jrow = lax.broadcasted_iota(jnp.int32, (8, LANES), 0)
    dmap = lax.broadcasted_iota(jnp.int32, (8, LANES), 1) // DA_HEAD_DIM
    for g in range(BRANCH_W // LANES):
        lanes = slice(g * LANES, (g + 1) * LANES)
        qg = qt_ref[0, lanes, :]
        qbd = _masked_q_blocks(qg, DA_HEAD_DIM)
        qf = qg.astype(F32)
        b2 = _bdot(jnp.where(jrow == dmap, kn_ref[:, lanes], 0.0), qf * qf)
        bound = jnp.concatenate([jnp.sqrt(b2[j:j + 1]) for j in range(DA_MAPS_PER_TILE)], axis=1) * 1.01 + 1e-3

        def attend(carry, g=g, lanes=lanes, qbd=qbd):
            it, shift, _ = carry
            st = jnp.dot(k_ref[0, :, lanes], qbd, preferred_element_type=F32)
            pt = jnp.exp2(st - shift).astype(BF16)
            low = jnp.float32(jnp.inf)
            for hh in range(heads):
                h = g * heads + hh
                oe = jnp.dot(vt_ref[0, h], pt[:, hh * w:(hh + 1) * w], preferred_element_type=F32)
                den = oe[DA_V_DIM:DA_V_DIM + 1]
                low = jnp.minimum(low, jnp.min(den))
                os = [oe[0:DA_V_DIM, i * DA_TQ:(i + 1) * DA_TQ] / den[:, i * DA_TQ:(i + 1) * DA_TQ] for i in range(2)]
                ot = os[0] - lam * os[1]
                ot = ot * lax.rsqrt(jnp.mean(ot * ot, axis=0, keepdims=True) + EPS) * sub_ref[...]
                acc_ref[h * DA_V_DIM:(h + 1) * DA_V_DIM, :] = ot * (1.0 - lam_init)
            return it + 1, _colmax(st), low

        def again(carry):
            it, _, low = carry
            return (it == 0) | ((it == 1) & jnp.logical_not(low >= ATT_MIN_DENOM))

        lax.while_loop(again, attend, (jnp.int32(0), bound, jnp.float32(0.0)))
    o_ref[...] = acc_ref[...].T.astype(o_ref.dtype)


def _diff_attention(qt, k, vt, da_lambda, subln_col, lam_init):
    nt = DEC_SEQ // DA_TQ
    vrows = DA_V_DIM + DA_ONES_ROWS
    return pl.pallas_call(
        functools.partial(_da_kernel, lam_init=lam_init),
        grid=(DEC_BATCH, nt),
        in_specs=[
            pl.BlockSpec((1, BRANCH_W, DA_TQ), lambda b, t: (b, 0, t)),
            pl.BlockSpec((1, DA_KEYS, BRANCH_W), lambda b, t: (b, 0, 0)),
            pl.BlockSpec((1, DA_HEADS, vrows, DA_KEYS), lambda b, t: (b, 0, 0, 0)),
            pl.BlockSpec((4, DA_HEAD_DIM), lambda b, t: (0, 0)),
            pl.BlockSpec((DA_V_DIM, 1), lambda b, t: (0, 0)),
        ],
        out_specs=pl.BlockSpec((DA_TQ, BRANCH_W), lambda b, t: (b * nt + t, 0)),
        out_shape=jax.ShapeDtypeStruct((DEC_BATCH * DEC_SEQ, BRANCH_W), BF16),
        scratch_shapes=[pltpu.VMEM((BRANCH_W, DA_TQ), F32), pltpu.VMEM((1, BRANCH_W), F32)],
        compiler_params=_cparams("arbitrary", "arbitrary"),
        name="diff_attention",
    )(qt, k, vt, da_lambda, subln_col)


def _rope_tables():
    pos = np.arange(DEC_SEQ)
    row = (pos // GRID_W).astype(np.float32)
    col = (pos % GRID_W).astype(np.float32)
    n_freq = DA_HEAD_DIM // 4
    inv = (np.float32(ROPE_BASE) ** (-np.arange(n_freq, dtype=np.float32) / n_freq)).astype(np.float32)
    ang = np.concatenate([row[:, None] * inv[None, :], col[:, None] * inv[None, :]], axis=-1)
    ang = ang.astype(np.float64)
    cos = np.repeat(np.cos(ang), 2, axis=-1)
    sin = np.repeat(np.sin(ang), 2, axis=-1)
    sign = np.where(np.arange(DA_HEAD_DIM) % 2 == 0, -1.0, 1.0)
    reps = BRANCH_W // DA_HEAD_DIM
    cos = np.tile(cos, (1, reps)).astype(np.float32)
    sin = np.tile(sin * sign[None, :], (1, reps)).astype(np.float32)
    return jnp.asarray(cos), jnp.asarray(sin)


def _filt_hidden_kernel(feat_ref, w1_ref, b1_ref, w2_ref, b2_ref, fr_ref, o_ref):
    fr = fr_ref[0]
    h = jnp.sin(fr * (jnp.dot(feat_ref[...], w1_ref[0], precision=HIGHEST, preferred_element_type=F32) + b1_ref[0]))
    o_ref[0] = jnp.sin(fr * (jnp.dot(h, w2_ref[0], precision=HIGHEST, preferred_element_type=F32) + b2_ref[0]))


def _filt_kernel(h_ref, w3f_ref, w3b_ref, dec_ref, o_ref):
    L = dec_ref.shape[0] // 2
    hf = jnp.dot(h_ref[0, 0:L], w3f_ref[0], precision=HIGHEST, preferred_element_type=F32) * dec_ref[0:L]
    hb = jnp.dot(h_ref[0, L:2 * L], w3b_ref[0], precision=HIGHEST, preferred_element_type=F32) * dec_ref[L:2 * L]
    row = lax.broadcasted_iota(jnp.int32, hb.shape, 0)
    hb = jnp.where(row == 0, 0.0, hb)
    nrm = jnp.sum(jnp.abs(hf), axis=0, keepdims=True) + jnp.sum(jnp.abs(hb), axis=0, keepdims=True)
    o_ref[0, 0, 0:L] = hf / nrm
    o_ref[0, 0, L:2 * L] = hb / nrm


def _circular_order(a):
    return np.concatenate([a, a[:1], a[1:][::-1]], axis=0)


def _hyena_pos_tables(L):
    f32 = np.float32
    pos = np.arange(L, dtype=f32)
    t = (pos / f32(L)).astype(f32)
    bands = np.linspace(1e-4, HY_POS_BANDS - 1, HY_POS_BANDS, dtype=f32)
    ang = (f32(2 * math.pi / L) * pos[:, None] * bands[None, :]).astype(np.float64)
    feats = np.zeros((L, HY_FILT_HIDDEN), f32)
    feats[:, 0] = t
    feats[:, 1:1 + HY_POS_BANDS] = np.cos(ang)
    feats[:, 1 + HY_POS_BANDS:HY_POS_DIM] = -np.sin(ang)
    deltas = np.linspace(math.log(HY_DECAY_TARGET) / HY_SLOW_DECAY,
                         math.log(HY_DECAY_TARGET) / HY_FAST_DECAY, BRANCH_W, dtype=f32)
    decay = np.exp((-t[:, None] * np.abs(deltas)[None, :]).astype(np.float64)).astype(f32)
    return jnp.asarray(_circular_order(feats)), jnp.asarray(_circular_order(decay))


def _hyena_filters(half, w1p, b1, w2, b2, w3, freq):
    feats, decay = _hyena_pos_tables(half)
    L = 2 * half
    cb = LANES
    ncb = BRANCH_W // cb
    small = lambda shape: pl.BlockSpec((1,) + shape, lambda l: (l, 0, 0))
    hidden = pl.pallas_call(
        _filt_hidden_kernel,
        grid=(DEPTH,),
        in_specs=[
            pl.BlockSpec((L, HY_FILT_HIDDEN), lambda l: (0, 0)),
            small((HY_FILT_HIDDEN, HY_FILT_HIDDEN)), small((1, HY_FILT_HIDDEN)),
            small((HY_FILT_HIDDEN, HY_FILT_HIDDEN)), small((1, HY_FILT_HIDDEN)),
            small((1, HY_FILT_HIDDEN)),
        ],
        out_specs=pl.BlockSpec((1, L, HY_FILT_HIDDEN), lambda l: (l, 0, 0)),
        out_shape=jax.ShapeDtypeStruct((DEPTH, L, HY_FILT_HIDDEN), F32),
        compiler_params=_cparams("arbitrary"),
        name=f"hyena_filter_hidden_{L}",
    )(feats, w1p, b1, w2, b2, freq)
    return pl.pallas_call(
        _filt_kernel,
        grid=(DEPTH, 2, ncb),
        in_specs=[
            pl.BlockSpec((1, L, HY_FILT_HIDDEN), lambda l, o, c: (l, 0, 0)),
            pl.BlockSpec((1, HY_FILT_HIDDEN, cb), lambda l, o, c: (l, 0, o * 2 * ncb + c)),
            pl.BlockSpec((1, HY_FILT_HIDDEN, cb), lambda l, o, c: (l, 0, o * 2 * ncb + ncb + c)),
            pl.BlockSpec((L, cb), lambda l, o, c: (0, c)),
        ],
        out_specs=pl.BlockSpec((1, 1, L, cb), lambda l, o, c: (l, o, 0, c)),
        out_shape=jax.ShapeDtypeStruct((DEPTH, 2, L, BRANCH_W), F32),
        compiler_params=_cparams("arbitrary", "arbitrary", "arbitrary"),
        name=f"hyena_filters_{L}",
    )(hidden, w3, w3, decay)


def _short_conv(u, w_ref, b_ref, seq_len):
    n = u.shape[0]
    t = lax.broadcasted_iota(jnp.int32, u.shape, 0) % seq_len
    prev = jnp.where(t == 0, 0.0, pltpu.roll(u, 1, axis=0))
    nxt = jnp.where(t == seq_len - 1, 0.0, pltpu.roll(u, n - 1, axis=0))
    return prev * w_ref[0:1, :] + u * w_ref[1:2, :] + nxt * w_ref[2:3, :] + b_ref[...]


def _dft_direct_mats():
    n, half = 2 * SEQ, SEQ
    k = np.arange(n)[:, None].astype(np.float64)
    t = np.arange(half)[None, :].astype(np.float64)
    ang = 2 * np.pi * k * t / n
    fr, fi = np.cos(ang), -np.sin(ang)
    mf = np.block([[fr, -fi], [fi, fr]])
    gr, gi = np.cos(ang).T / n, np.sin(ang).T / n
    mi = np.block([[gr, -gi], [gi, gr]])
    return mf.astype(np.float32), mi.astype(np.float32)


def _dft_real_mat():
    n = 2 * SEQ
    ang = 2 * np.pi * np.arange(n)[:, None].astype(np.float64) * np.arange(n)[None, :] / n
    return np.concatenate([np.cos(ang), -np.sin(ang)], axis=0).astype(np.float32)


def _spec_direct_kernel(h_ref, m_ref, o_ref):
    o_ref[0, 0] = jnp.dot(m_ref[...], h_ref[0, 0], precision=HIGHEST, preferred_element_type=F32)


def _spec_direct(h, m_real):
    n = 2 * SEQ
    return pl.pallas_call(
        _spec_direct_kernel,
        grid=(DEPTH, 2),
        in_specs=[pl.BlockSpec((1, 1, n, BRANCH_W), lambda l, o: (l, o, 0, 0)),
                  pl.BlockSpec((2 * n, n), lambda l, o: (0, 0))],
        out_specs=pl.BlockSpec((1, 1, 2 * n, BRANCH_W), lambda l, o: (l, o, 0, 0)),
        out_shape=jax.ShapeDtypeStruct((DEPTH, 2, 2 * n, BRANCH_W), F32),
        compiler_params=_cparams("arbitrary", "arbitrary"),
        name="hyena_spectrum_direct",
    )(h, m_real)


def _hyena_direct_kernel(u_ref, cw_ref, cb_ref, h_ref, bias_ref, mf_ref, mi_ref, o_ref):
    n = 2 * SEQ
    c = _short_conv(u_ref[...].astype(F32), cw_ref, cb_ref, SEQ)
    z = c[:, 0:BRANCH_W]
    for order in range(2):
        gate = c[:, (order + 1) * BRANCH_W:(order + 2) * BRANCH_W]
        f = jnp.dot(mf_ref[...], z.astype(BF16), preferred_element_type=F32)
        fr, fi = f[0:n], f[n:2 * n]
        hr, hi = h_ref[order, 0:n], h_ref[order, n:2 * n]
        y = jnp.concatenate([fr * hr - fi * hi, fr * hi + fi * hr], axis=0)
        y = jnp.dot(mi_ref[...], y.astype(BF16), preferred_element_type=F32)
        z = gate * (y + z * bias_ref[order:order + 1, :])
    o_ref[...] = z.astype(o_ref.dtype)


def _hyena_direct(u, conv_w, conv_b, spec, l, bias, mf, mi):
    n = 2 * SEQ
    rows = 2 * SEQ
    T = u.shape[0]
    return pl.pallas_call(
        _hyena_direct_kernel,
        grid=(T // rows,),
        in_specs=[
            pl.BlockSpec((rows, 3 * BRANCH_W), lambda p: (p, 0)),
            pl.BlockSpec((3, 3 * BRANCH_W), lambda p: (0, 0)),
            pl.BlockSpec((1, 3 * BRANCH_W), lambda p: (0, 0)),
            pl.BlockSpec((None, 2, 2 * n, BRANCH_W), lambda p: (l, 0, 0, 0)),
            pl.BlockSpec((2, BRANCH_W), lambda p: (0, 0)),
            pl.BlockSpec((2 * n, rows), lambda p: (0, 0)),
            pl.BlockSpec((rows, 2 * n), lambda p: (0, 0)),
        ],
        out_specs=pl.BlockSpec((rows, BRANCH_W), lambda p: (p, 0)),
        out_shape=jax.ShapeDtypeStruct((T, BRANCH_W), BF16),
        compiler_params=_cparams("arbitrary"),
        name="hyena_direct",
    )(u, conv_w, conv_b, spec, bias, mf, mi)


def _dft_two_stage_mats():
    no, ni, half, n = FFT_NO, FFT_NI, FFT_HALF, FFT_N
    f64 = np.float64
    k1 = np.arange(no, dtype=f64)
    n_o = np.arange(half, dtype=f64)
    n_i = np.arange(ni, dtype=f64)
    ang = 2 * np.pi * (n_i[:, None, None] * k1[None, :, None] / n + k1[None, :, None] * n_o[None, None, :] / no)
    tr, ti = np.cos(ang), -np.sin(ang)
    m1 = np.concatenate([np.concatenate([tr, -ti], axis=2), np.concatenate([ti, tr], axis=2)], axis=1)
    k2 = np.arange(ni, dtype=f64)
    ang2 = 2 * np.pi * k2[:, None] * n_i[None, :] / ni
    f2r, f2i = np.cos(ang2), -np.sin(ang2)
    m2 = np.block([[f2r, -f2i], [f2i, f2r]])
    m2c = np.block([[f2r, f2i], [-f2i, f2r]])
    sr, si = np.transpose(tr, (0, 2, 1)) / n, -np.transpose(ti, (0, 2, 1)) / n
    m3 = np.concatenate([np.concatenate([sr, -si], axis=2), np.concatenate([si, sr], axis=2)], axis=1)
    return (m1.astype(np.float32), m2.astype(np.float32), m2c.astype(np.float32), m3.astype(np.float32))


def _dft_stage1_real_mat():
    no, ni, n = FFT_NO, FFT_NI, FFT_N
    k1 = np.arange(no, dtype=np.float64)
    n_o = np.arange(no, dtype=np.float64)
    n_i = np.arange(ni, dtype=np.float64)
    ang = 2 * np.pi * (n_i[:, None, None] * k1[None, :, None] / n + k1[None, :, None] * n_o[None, None, :] / no)
    return np.concatenate([np.cos(ang), -np.sin(ang)], axis=1).astype(np.float32)


def _store_stage1(w_ref, ni, out):
    w_ref[pl.ds(ni, FFT_NO, stride=2 * FFT_NI), :] = out[0:FFT_NO]
    w_ref[pl.ds(FFT_NI + ni, FFT_NO, stride=2 * FFT_NI), :] = out[FFT_NO:2 * FFT_NO]


def _fwd_stage1(za_ref, zb_ref, m1_ref, w_ref):
    def body(ni, carry):
        a = za_ref[pl.ds(ni, FFT_HALF, stride=FFT_NI), :]
        b = zb_ref[pl.ds(ni, FFT_HALF, stride=FFT_NI), :]
        out = jnp.dot(m1_ref[ni], jnp.concatenate([a, b], axis=0).astype(BF16), preferred_element_type=F32)
        _store_stage1(w_ref, ni, out)
        return carry

    lax.fori_loop(0, FFT_NI, body, 0, unroll=FFT_UNROLL)


def _spec_two_stage_kernel(h_ref, m1_ref, m2_ref, o_ref, w_ref):
    h = h_ref.at[0, 0]

    def stage1(ni, carry):
        a = h[pl.ds(ni, FFT_NO, stride=FFT_NI), :]
        _store_stage1(w_ref, ni, jnp.dot(m1_ref[ni], a.astype(BF16), preferred_element_type=F32))
        return carry

    lax.fori_loop(0, FFT_NI, stage1, 0, unroll=FFT_UNROLL)
    blk = 2 * FFT_NI

    cb = w_ref.shape[1]

    def stage2(kp, carry):
        rows = [pl.ds(pl.multiple_of((2 * kp + j) * blk, blk), blk) for j in range(2)]
        x = jnp.dot(m2_ref[...], jnp.concatenate([w_ref[r, :] for r in rows], axis=1).astype(BF16),
                    preferred_element_type=F32)
        for j in range(2):
            o_ref[0, 0, rows[j], :] = x[:, j * cb:(j + 1) * cb]
        return carry

    lax.fori_loop(0, FFT_NO // 2, stage2, 0, unroll=FFT_MID_UNROLL)


def _spec_two_stage(h, m1_real, m2):
    cb = LCONV_CB
    return pl.pallas_call(
        _spec_two_stage_kernel,
        grid=(DEPTH, 2, BRANCH_W // cb),
        in_specs=[pl.BlockSpec((1, 1, FFT_N, cb), lambda l, o, c: (l, o, 0, c)),
                  pl.BlockSpec((FFT_NI, 2 * FFT_NO, FFT_NO), lambda l, o, c: (0, 0, 0)),
                  pl.BlockSpec((2 * FFT_NI, 2 * FFT_NI), lambda l, o, c: (0, 0))],
        out_specs=pl.BlockSpec((1, 1, 2 * FFT_N, cb), lambda l, o, c: (l, o, 0, c)),
        out_shape=jax.ShapeDtypeStruct((DEPTH, 2, 2 * FFT_N, BRANCH_W), F32),
        scratch_shapes=[pltpu.VMEM((2 * FFT_N, cb), F32)],
        compiler_params=_cparams("arbitrary", "arbitrary", "arbitrary"),
        name="hyena_spectrum_two_stage",
    )(h, m1_real, m2)


def _lconv_two_stage_kernel(s_ref, g_ref, cws_ref, cbs_ref, cwg_ref, cbg_ref, h_ref, bias_ref,
                            m1_ref, m2_ref, m2c_ref, m3_ref, o_ref, z_ref, w_ref, *, conv_sig):
    for b in range(2):
        sig = s_ref[b].astype(F32)
        if conv_sig:
            sig = _short_conv(sig, cws_ref, cbs_ref, DEC_SEQ)
        z_ref[b] = sig
    _fwd_stage1(z_ref.at[0], z_ref.at[1], m1_ref, w_ref)
    blk = 2 * FFT_NI

    cb = w_ref.shape[1]

    def mid(kp, carry):
        rows = [pl.ds(pl.multiple_of((2 * kp + j) * blk, blk), blk) for j in range(2)]
        x = jnp.dot(m2_ref[...], jnp.concatenate([w_ref[r, :] for r in rows], axis=1).astype(BF16),
                    preferred_element_type=F32)
        h = jnp.concatenate([h_ref[r, :] for r in rows], axis=1)
        xr, xi = x[0:FFT_NI], x[FFT_NI:blk]
        hr, hi = h[0:FFT_NI], h[FFT_NI:blk]
        y = jnp.concatenate([xr * hr - xi * hi, xr * hi + xi * hr], axis=0)
        c = jnp.dot(m2c_ref[...], y.astype(BF16), preferred_element_type=F32)
        for j in range(2):
            w_ref[rows[j], :] = c[:, j * cb:(j + 1) * cb]
        return carry

    lax.fori_loop(0, FFT_NO // 2, mid, 0, unroll=FFT_MID_UNROLL)

    def last(ni, carry):
        cr = w_ref[pl.ds(ni, FFT_NO, stride=blk), :]
        ci = w_ref[pl.ds(FFT_NI + ni, FFT_NO, stride=blk), :]
        y = jnp.dot(m3_ref[ni], jnp.concatenate([cr, ci], axis=0).astype(BF16), preferred_element_type=F32)
        o_ref[0, pl.ds(ni, FFT_HALF, stride=FFT_NI), :] = y[0:FFT_HALF]
        o_ref[1, pl.ds(ni, FFT_HALF, stride=FFT_NI), :] = y[FFT_HALF:2 * FFT_HALF]
        return carry

    lax.fori_loop(0, FFT_NI, last, 0, unroll=FFT_UNROLL)
    for b in range(2):
        gate = _short_conv(g_ref[b].astype(F32), cwg_ref, cbg_ref, DEC_SEQ)
        sig = z_ref[b]
        o_ref[b] = gate * (o_ref[b] + sig * bias_ref[...])


def _lconv_two_stage(sig, sig_col, gate_src, gate_col, conv_w, conv_b, spec, l, order, bias, mats, conv_sig):
    cb = LCONV_CB
    ncb = BRANCH_W // cb
    m1, m2, m2c, m3 = mats
    const3 = lambda c, p: (0, 0, 0)
    const2 = lambda c, p: (0, 0)
    return pl.pallas_call(
        functools.partial(_lconv_two_stage_kernel, conv_sig=conv_sig),
        grid=(ncb, DEC_BATCH // 2),
        in_specs=[
            pl.BlockSpec((2, DEC_SEQ, cb), lambda c, p: (p, 0, sig_col * ncb + c)),
            pl.BlockSpec((2, DEC_SEQ, cb), lambda c, p: (p, 0, gate_col * ncb + c)),
            pl.BlockSpec((3, cb), lambda c, p: (0, c)),
            pl.BlockSpec((1, cb), lambda c, p: (0, c)),
            pl.BlockSpec((3, cb), lambda c, p: (0, gate_col * ncb + c)),
            pl.BlockSpec((1, cb), lambda c, p: (0, gate_col * ncb + c)),
            pl.BlockSpec((None, None, 2 * FFT_N, cb), lambda c, p: (l, order, 0, c)),
            pl.BlockSpec((1, cb), lambda c, p: (0, c)),
            pl.BlockSpec(m1.shape, const3),
            pl.BlockSpec(m2.shape, const2),
            pl.BlockSpec(m2c.shape, const2),
            pl.BlockSpec(m3.shape, const3),
        ],
        out_specs=pl.BlockSpec((2, DEC_SEQ, cb), lambda c, p: (p, 0, c)),
        out_shape=jax.ShapeDtypeStruct((DEC_BATCH, DEC_SEQ, BRANCH_W), F32),
        scratch_shapes=[pltpu.VMEM((2, DEC_SEQ, cb), F32), pltpu.VMEM((2 * FFT_N, cb), F32)],
        compiler_params=_cparams("arbitrary", "arbitrary"),
        name="hyena_lconv_two_stage",
    )(sig, gate_src, conv_w, conv_b, conv_w, conv_b, spec, bias, m1, m2, m2c, m3)


def kernel(x_prompt, x_sample, cache_na_k, cache_na_v, cache_da_k, cache_da_v, c, c_ctx, w_ada, b_ada, norm_mix,
           norm_ffn, w_in, hy_conv_w, hy_conv_b, hy_filt_w1, hy_filt_b1, hy_filt_w2, hy_filt_b2, hy_filt_w3,
           hy_filt_freq, hy_bias, na_rpb, da_lambda, da_subln, w_lift, w_out, w_ffn_in, w_ffn_out, norm_final):
    TP, TS = BATCH * SEQ, DEC_BATCH * DEC_SEQ
    xp = x_prompt.reshape(TP, D_MODEL)
    xs = x_sample.reshape(TS, D_MODEL)

    cc = jnp.concatenate([c_ctx[None, :], c, jnp.zeros((8 - 1 - DEC_BATCH, D_MODEL), F32)], axis=0)
    mod = _modulation(cc, w_ada, b_ada)
    mod_p = mod[:, 0:1].reshape(DEPTH, 1, 1, 6 * D_MODEL)
    mod_s = mod[:, 1:1 + DEC_BATCH].reshape(DEPTH, DEC_BATCH, 1, 6 * D_MODEL)

    w_mix = w_in[:, :, :MIX_W].astype(BF16)
    w_gate = w_in[:, :, MIX_W:].astype(BF16)
    w_lift_b = w_lift.astype(BF16)
    w_out_b = w_out.astype(BF16)
    w_ffn_in_b = w_ffn_in.astype(BF16)
    w_ffn_out_b = w_ffn_out.astype(BF16)
    g_mix = norm_mix.reshape(DEPTH, 1, D_MODEL)
    g_ffn = norm_ffn.reshape(DEPTH, 1, D_MODEL)
    g_fin = norm_final.reshape(1, D_MODEL)
    subln = da_subln.reshape(DEPTH, 1, DA_V_DIM)
    subln_col = da_subln.reshape(DEPTH, DA_V_DIM, 1)

    w1p = jnp.pad(hy_filt_w1, ((0, 0), (0, HY_FILT_HIDDEN - HY_POS_DIM), (0, 0)))
    b1 = hy_filt_b1.reshape(DEPTH, 1, HY_FILT_HIDDEN)
    b2 = hy_filt_b2.reshape(DEPTH, 1, HY_FILT_HIDDEN)
    fr = hy_filt_freq.reshape(DEPTH, 1, HY_FILT_HIDDEN)
    mf, mi = _dft_direct_mats()
    mats = _dft_two_stage_mats()
    h_p = _hyena_filters(SEQ, w1p, b1, hy_filt_w2, b2, hy_filt_w3, fr)
    h_s = _hyena_filters(DEC_SEQ, w1p, b1, hy_filt_w2, b2, hy_filt_w3, fr)
    spec_p = _spec_direct(h_p, jnp.asarray(_dft_real_mat()))
    mf_b, mi_b = jnp.asarray(mf, dtype=BF16), jnp.asarray(mi, dtype=BF16)
    mats_b = tuple(jnp.asarray(m, dtype=BF16) for m in mats)
    spec_s = _spec_two_stage(h_s, jnp.asarray(_dft_stage1_real_mat(), dtype=BF16), mats_b[1])
    conv_b = hy_conv_b.reshape(DEPTH, 1, 3 * BRANCH_W)

    na_bias = _na_bias_table(na_rpb)
    rope_tables = _rope_tables()
    ck_na = cache_na_k.reshape(DEC_BATCH, DEPTH, PAST_LEN, BRANCH_W)
    cv_na = cache_na_v.reshape(DEC_BATCH, DEPTH, PAST_LEN, BRANCH_W)
    ck_da = cache_da_k.reshape(DEC_BATCH, DEPTH, PAST_LEN, BRANCH_W)
    cv_da = cache_da_v.reshape(DEC_BATCH, DEPTH, PAST_LEN, BRANCH_W)

    caches = tuple(jnp.zeros((BATCH, DEPTH, SEQ, BRANCH_W), F32) for _ in CACHE_BLOCKS)
    for l in range(DEPTH):
        lam_init = 0.8 - 0.6 * math.exp(-0.3 * l)
        final = l == DEPTH - 1

        u, caches = _in_proj(xp, g_mix[l], mod_p[l], w_mix, l, TP, BF16, caches=caches)
        y_hy = _hyena_direct(u, hy_conv_w[l], conv_b[l], spec_p, l, hy_bias[l], mf_b, mi_b)
        y_na, y_da = _ctx_attention(u, da_lambda[l], subln_col[l], lam_init)
        xp = _merge_out(xp, g_mix[l], mod_p[l], y_hy, y_na, y_da, w_gate, w_lift_b, w_out_b, l, TP)
        xp = _ffn(xp, g_ffn[l], mod_p[l], w_ffn_in_b, w_ffn_out_b, g_fin, l, TP, final)

        u = _in_proj(xs, g_mix[l], mod_s[l], w_mix, l, DEC_SEQ, BF16)
        u3 = u.reshape(DEC_BATCH, DEC_SEQ, MIX_W)
        z1 = _lconv_two_stage(u3, 0, u3, 1, hy_conv_w[l], conv_b[l], spec_s, l, 0, hy_bias[l, 0:1], mats_b, True)
        y_hy = _lconv_two_stage(z1, 0, u3, 2, hy_conv_w[l], conv_b[l], spec_s, l, 1, hy_bias[l, 1:2], mats_b, False)
        y_hy = y_hy.reshape(TS, BRANCH_W)
        qn, kn, vn = _attn_prep(u, 3, ck_na[:, l], cv_na[:, l], NA_HEAD_DIM)
        y_na = _nbr_attention(qn, kn, vn, na_bias, l)
        q, kt, v = _attn_prep(u, 6, ck_da[:, l], cv_da[:, l], DA_HEAD_DIM, rope_tables)
        y_da = _diff_attention(q, kt, v, da_lambda[l], subln_col[l], lam_init)
        xs = _merge_out(xs, g_mix[l], mod_s[l], y_hy, y_na, y_da, w_gate, w_lift_b, w_out_b, l, DEC_SEQ)
        xs = _ffn(xs, g_ffn[l], mod_s[l], w_ffn_in_b, w_ffn_out_b, g_fin, l, DEC_SEQ, final)

    y_prompt = xp.reshape(BATCH, SEQ, D_MODEL)
    y_sample = xs.reshape(DEC_BATCH, DEC_SEQ, D_MODEL)
    heads = lambda a, d: a.reshape(BATCH, DEPTH, SEQ, BRANCH_W // d, d)
    return (y_prompt, y_sample, heads(caches[0], NA_HEAD_DIM), heads(caches[1], NA_HEAD_DIM),
            heads(caches[2], 2 * DA_HEAD_DIM), heads(caches[3], DA_V_DIM))
```

```python
import functools
import math

import numpy as np
import jax
import jax.numpy as jnp
from jax import lax
from jax.experimental import pallas as pl
from jax.experimental.pallas import tpu as pltpu

F32 = jnp.float32
BF16 = jnp.bfloat16
HIGHEST = lax.Precision.HIGHEST

D_MODEL = 1024
BATCH = 32
SEQ = 256
DEPTH = 4
DEC_BATCH = 4
DEC_SEQ = 4096
PAST_LEN = 256
GRID_W = 64
GRID_H = DEC_SEQ // GRID_W
BRANCH_W = 512
HY_POS_BANDS = 16
HY_POS_DIM = 1 + 2 * HY_POS_BANDS
HY_FILT_HIDDEN = 64
HY_DECAY_TARGET = 1e-2
HY_FAST_DECAY = 0.3
HY_SLOW_DECAY = 1.5
NA_HEADS = 8
NA_HEAD_DIM = 64
NA_WIN_ROWS = 8
NA_WIN_COLS = 16
DA_HEADS = 8
DA_HEAD_DIM = 32
DA_V_DIM = 64
D_FF = 2816
MIX_W = 9 * BRANCH_W
ROPE_BASE = 10000.0
EPS = 1e-6
NEG_INF = -1e30

VMEM_LIMIT_BYTES = 56 * 1024 * 1024
LANES = 128
MXU_DIM = 256

FFT_N = 2 * DEC_SEQ
FFT_NO = 64
FFT_NI = 128
FFT_HALF = FFT_NO // 2
FFT_UNROLL = 8
FFT_MID_UNROLL = 16
LCONV_CB = LANES


def _cparams(*sem):
    return pltpu.CompilerParams(dimension_semantics=sem, vmem_limit_bytes=VMEM_LIMIT_BYTES)


def _sigmoid(x):
    return 1.0 / (1.0 + jnp.exp(-x))


def _rms(x, g):
    return x * lax.rsqrt(jnp.mean(x * x, axis=-1, keepdims=True) + EPS) * g


def _modnorm(x, g, shift, scale):
    return _rms(x, g) * (1.0 + scale) + shift


def _bdot(a, b):
    return jnp.dot(a.astype(BF16), b.astype(BF16), preferred_element_type=F32)


def _mod_kernel(c_ref, w_ref, b_ref, o_ref):
    c = c_ref[...]
    s = c * _sigmoid(c)
    o_ref[0] = jnp.dot(s, w_ref[0], precision=HIGHEST, preferred_element_type=F32) + b_ref[0]


def _modulation(cc, w_ada, b_ada):
    nt = 6
    return pl.pallas_call(
        _mod_kernel,
        grid=(DEPTH, nt),
        in_specs=[
            pl.BlockSpec((8, D_MODEL), lambda l, j: (0, 0)),
            pl.BlockSpec((1, D_MODEL, D_MODEL), lambda l, j: (l, 0, j)),
            pl.BlockSpec((1, 1, D_MODEL), lambda l, j: (l, 0, j)),
        ],
        out_specs=pl.BlockSpec((1, 8, D_MODEL), lambda l, j: (l, 0, j)),
        out_shape=jax.ShapeDtypeStruct((DEPTH, 8, 6 * D_MODEL), F32),
        compiler_params=_cparams("arbitrary", "arbitrary"),
        name="modulation",
    )(cc, w_ada, b_ada.reshape(DEPTH, 1, 6 * D_MODEL))


IN_TM = 512
CACHE_BLOCKS = (4, 5, 7, 8)


def _in_kernel(*refs, n_cache):
    x_ref, g_ref, mod_ref, w_ref = refs[:4]
    o_ref = refs[4 + n_cache]
    cache_refs = refs[5 + n_cache:]
    m = mod_ref[0]
    h = _modnorm(x_ref[...], g_ref[...], m[:, 0:D_MODEL], m[:, D_MODEL:2 * D_MODEL]).astype(BF16)
    res = jnp.dot(h, w_ref[...], preferred_element_type=F32)
    o_ref[...] = res.astype(o_ref.dtype)
    for c, c_ref in zip(CACHE_BLOCKS, cache_refs):
        c_ref[...] = res[:, c * BRANCH_W:(c + 1) * BRANCH_W].reshape(c_ref.shape)


def _in_proj(x, g, mod, w, l, rows_per_mod, out_dtype, caches=None):
    T = x.shape[0]
    tm = IN_TM
    per = rows_per_mod // tm
    in_specs = [
        pl.BlockSpec((tm, D_MODEL), lambda i: (i, 0)),
        pl.BlockSpec((1, D_MODEL), lambda i: (0, 0)),
        pl.BlockSpec((1, 1, 6 * D_MODEL), lambda i: (i // per, 0, 0)),
        pl.BlockSpec((None, D_MODEL, MIX_W), lambda i: (l, 0, 0), pipeline_mode=pl.Buffered(1)),
    ]
    out_specs = [pl.BlockSpec((tm, MIX_W), lambda i: (i, 0))]
    out_shape = [jax.ShapeDtypeStruct((T, MIX_W), out_dtype)]
    args = [x, g, mod, w]
    aliases = {}
    if caches is not None:
        out_specs += [pl.BlockSpec((tm // SEQ, 1, SEQ, BRANCH_W), lambda i: (i, l, 0, 0))] * len(caches)
        out_shape += [jax.ShapeDtypeStruct(c.shape, c.dtype) for c in caches]
        in_specs += [pl.BlockSpec(memory_space=pl.ANY)] * len(caches)
        aliases = {4 + n: 1 + n for n in range(len(caches))}
        args += list(caches)
    outs = pl.pallas_call(
        functools.partial(_in_kernel, n_cache=len(args) - 4),
        grid=(T // tm,),
        in_specs=in_specs,
        out_specs=out_specs,
        out_shape=out_shape,
        input_output_aliases=aliases,
        compiler_params=_cparams("arbitrary"),
        name="in_proj",
    )(*args)
    return outs[0] if caches is None else (outs[0], tuple(outs[1:]))


def _mid_kernel(x_ref, g_ref, mod_ref, yh_ref, yn_ref, yd_ref, wg_ref, wl_ref, wo_ref, o_ref):
    m = mod_ref[0]
    x = x_ref[...]
    h = _modnorm(x, g_ref[...], m[:, 0:D_MODEL], m[:, D_MODEL:2 * D_MODEL]).astype(BF16)
    merged = None
    for br, y_ref in enumerate((yh_ref, yn_ref, yd_ref)):
        gate = _sigmoid(jnp.dot(h, wg_ref[:, br * D_MODEL:(br + 1) * D_MODEL], preferred_element_type=F32))
        lift = jnp.dot(y_ref[...].astype(BF16), wl_ref[br], preferred_element_type=F32)
        t = gate * lift
        merged = t if merged is None else merged + t
    o_ref[...] = x + m[:, 2 * D_MODEL:3 * D_MODEL] * _bdot(merged, wo_ref[...])


def _merge_out(x, g, mod, y_hy, y_na, y_da, w_gate, w_lift, w_out, l, rows_per_mod):
    T = x.shape[0]
    tm = 512
    per = rows_per_mod // tm
    row = lambda i: (i, 0)
    const2 = lambda i: (0, 0)
    return pl.pallas_call(
        _mid_kernel,
        grid=(T // tm,),
        in_specs=[
            pl.BlockSpec((tm, D_MODEL), row),
            pl.BlockSpec((1, D_MODEL), const2),
            pl.BlockSpec((1, 1, 6 * D_MODEL), lambda i: (i // per, 0, 0)),
            pl.BlockSpec((tm, BRANCH_W), row),
            pl.BlockSpec((tm, BRANCH_W), row),
            pl.BlockSpec((tm, BRANCH_W), row),
            pl.BlockSpec((None, D_MODEL, 3 * D_MODEL), lambda i: (l, 0, 0)),
            pl.BlockSpec((None, 3, BRANCH_W, D_MODEL), lambda i: (l, 0, 0, 0)),
            pl.BlockSpec((None, D_MODEL, D_MODEL), lambda i: (l, 0, 0)),
        ],
        out_specs=pl.BlockSpec((tm, D_MODEL), row),
        out_shape=jax.ShapeDtypeStruct((T, D_MODEL), F32),
        compiler_params=_cparams("arbitrary"),
        name="merge_out",
    )(x, g, mod, y_hy, y_na, y_da, w_gate, w_lift, w_out)


def _ffn_kernel(x_ref, g_ref, mod_ref, w1_ref, w2_ref, gf_ref, o_ref, *, final):
    m = mod_ref[0]
    x = x_ref[...]
    h = _modnorm(x, g_ref[...], m[:, 3 * D_MODEL:4 * D_MODEL], m[:, 4 * D_MODEL:5 * D_MODEL]).astype(BF16)
    a = jnp.dot(h, w1_ref[:, 0:D_FF], preferred_element_type=F32)
    b = jnp.dot(h, w1_ref[:, D_FF:2 * D_FF], preferred_element_type=F32)
    xn = x + m[:, 5 * D_MODEL:6 * D_MODEL] * _bdot(a * _sigmoid(a) * b, w2_ref[...])
    if final:
        xn = _rms(xn, gf_ref[...])
    o_ref[...] = xn


def _ffn(x, g, mod, w_ffn_in, w_ffn_out, g_final, l, rows_per_mod, final):
    T = x.shape[0]
    tm = 512
    per = rows_per_mod // tm
    resident = pl.Buffered(1)
    return pl.pallas_call(
        functools.partial(_ffn_kernel, final=final),
        grid=(T // tm,),
        in_specs=[
            pl.BlockSpec((tm, D_MODEL), lambda i: (i, 0)),
            pl.BlockSpec((1, D_MODEL), lambda i: (0, 0)),
            pl.BlockSpec((1, 1, 6 * D_MODEL), lambda i: (i // per, 0, 0)),
            pl.BlockSpec((None, D_MODEL, 2 * D_FF), lambda i: (l, 0, 0), pipeline_mode=resident),
            pl.BlockSpec((None, D_FF, D_MODEL), lambda i: (l, 0, 0), pipeline_mode=resident),
            pl.BlockSpec((1, D_MODEL), lambda i: (0, 0)),
        ],
        out_specs=pl.BlockSpec((tm, D_MODEL), lambda i: (i, 0)),
        out_shape=jax.ShapeDtypeStruct((T, D_MODEL), F32),
        compiler_params=_cparams("arbitrary"),
        name="ffn",
    )(x, g, mod, w_ffn_in, w_ffn_out, g_final)


def _da_lambda(lam_ref, lam_init):
    lp = lam_ref[...]
    a = jnp.sum(lp[0:1] * lp[1:2], axis=1, keepdims=True)
    b = jnp.sum(lp[2:3] * lp[3:4], axis=1, keepdims=True)
    return jnp.exp(a) - jnp.exp(b) + lam_init


ATT_ONES_ROWS = 16
ATT_TQ = 256
ATT_PREP_T = 512
ATT_KEYS = DEC_SEQ + PAST_LEN
ATT_MIN_DENOM = 2.0 ** -64
LOG2E = math.log2(math.e)


def _masked_q_blocks(qt, d):
    row = lax.broadcasted_iota(jnp.int32, qt.shape, 0)
    zero = jnp.zeros_like(qt)
    return jnp.concatenate([jnp.where((row >= j * d) & (row < (j + 1) * d), qt, zero) for j in range(LANES // d)],
                           axis=1)


def _colmax(st):
    keys, n = st.shape
    return jnp.max(jnp.max(st.reshape(keys // MXU_DIM, MXU_DIM, n), axis=0), axis=0, keepdims=True)


def _ctx_attn_kernel(nq_ref, nk_ref, nv_ref, dq_ref, dk_ref, dv_ref, lam_ref, sub_ref, yn_ref, yd_ref, acc_ref,
                     *, lam_init):
    lam = _da_lambda(lam_ref, lam_init)
    ones = jnp.ones((ATT_ONES_ROWS, SEQ), BF16)

    def attend(q_ref, k_ref, v_ref, d, maps_per_head, finish):
        qt = (q_ref[...].astype(F32) * (d ** -0.5 * LOG2E)).T.astype(BF16)
        vt = v_ref[...].astype(F32).T.astype(BF16)
        kb = k_ref[...].astype(BF16)
        dv = NA_HEAD_DIM
        heads_per_group = LANES // (d * maps_per_head)
        w = maps_per_head * SEQ
        for g in range(BRANCH_W // LANES):
            lanes = slice(g * LANES, (g + 1) * LANES)
            st = jnp.dot(kb[:, lanes], _masked_q_blocks(qt[lanes], d), preferred_element_type=F32)
            pt = jnp.exp2(st - _colmax(st)).astype(BF16)
            for j in range(heads_per_group):
                h = g * heads_per_group + j
                ve = jnp.concatenate([vt[h * dv:(h + 1) * dv], ones], axis=0)
                oe = jnp.dot(ve, pt[:, j * w:(j + 1) * w], preferred_element_type=F32)
                os = [oe[0:dv, i * SEQ:(i + 1) * SEQ] / oe[dv:dv + 1, i * SEQ:(i + 1) * SEQ]
                      for i in range(maps_per_head)]
                acc_ref[h * dv:(h + 1) * dv, :] = finish(os)

    attend(nq_ref, nk_ref, nv_ref, NA_HEAD_DIM, 1, lambda os: os[0])
    yn_ref[...] = acc_ref[...].T.astype(yn_ref.dtype)

    def da_finish(os):
        ot = os[0] - lam * os[1]
        ot = ot * lax.rsqrt(jnp.mean(ot * ot, axis=0, keepdims=True) + EPS) * sub_ref[...]
        return ot * (1.0 - lam_init)

    attend(dq_ref, dk_ref, dv_ref, DA_HEAD_DIM, 2, da_finish)
    yd_ref[...] = acc_ref[...].T.astype(yd_ref.dtype)


def _ctx_attention(u, da_lambda, subln_col, lam_init):
    col = lambda j: pl.BlockSpec((SEQ, BRANCH_W), lambda b, j=j: (b, j))
    out = pl.BlockSpec((SEQ, BRANCH_W), lambda b: (b, 0))
    shape = jax.ShapeDtypeStruct((BATCH * SEQ, BRANCH_W), BF16)
    return pl.pallas_call(
        functools.partial(_ctx_attn_kernel, lam_init=lam_init),
        grid=(BATCH,),
        in_specs=[col(3), col(4), col(5), col(6), col(7), col(8),
                  pl.BlockSpec((4, DA_HEAD_DIM), lambda b: (0, 0)),
                  pl.BlockSpec((DA_V_DIM, 1), lambda b: (0, 0))],
        out_specs=[out, out],
        out_shape=[shape, shape],
        scratch_shapes=[pltpu.VMEM((BRANCH_W, SEQ), F32)],
        compiler_params=_cparams("arbitrary"),
        name="ctx_attention",
    )(u, u, u, u, u, u, da_lambda, subln_col)


def _rope(x, cos, sin_signed):
    n = x.shape[-1]
    lane = lax.broadcasted_iota(jnp.int32, x.shape, 1)
    partner = jnp.where(lane % 2 == 0, pltpu.roll(x, n - 1, axis=1), pltpu.roll(x, 1, axis=1))
    return x * cos + partner * sin_signed


def _attn_prep_kernel(q_ref, k_ref, v_ref, kc_ref, vc_ref, *refs, rope, scale):
    cos_ref, sin_ref = refs[:2] if rope else (None, None)
    qt_ref, ko_ref, vt_ref = refs[-3:]
    t = pl.program_id(1)
    dv = NA_HEAD_DIM

    def put_v(v):
        n = v.shape[0]
        vt = v.astype(F32).T.astype(BF16)
        ones = jnp.ones((ATT_ONES_ROWS, n), BF16)
        for h in range(BRANCH_W // dv):
            vt_ref[0, h, 0:dv, 0:n] = vt[h * dv:(h + 1) * dv]
            vt_ref[0, h, dv:dv + ATT_ONES_ROWS, 0:n] = ones

    @pl.when(t < DEC_SEQ // ATT_PREP_T)
    def _():
        q = q_ref[...].astype(F32)
        k = k_ref[...].astype(F32)
        if rope:
            q = _rope(q, cos_ref[...], sin_ref[...])
            k = _rope(k, cos_ref[...], sin_ref[...])
        qt_ref[0] = (q * scale).T.astype(BF16)
        ko_ref[0] = k.astype(BF16)
        put_v(v_ref[...])

    @pl.when(t == DEC_SEQ // ATT_PREP_T)
    def _():
        ko_ref[0, 0:PAST_LEN, :] = kc_ref[0].astype(BF16)
        put_v(vc_ref[0])


def _attn_prep(u, first_col, k_ctx, v_ctx, head_dim, rope_tables=None):
    rope = rope_tables is not None
    tile = ATT_PREP_T
    nt = DEC_SEQ // tile
    last = nt - 1
    rowblk = lambda j: pl.BlockSpec((tile, BRANCH_W), lambda b, t, j=j: (b * nt + jnp.minimum(t, last), j))
    tab = pl.BlockSpec((tile, BRANCH_W), lambda b, t: (jnp.minimum(t, last), 0))
    ctx = pl.BlockSpec((1, PAST_LEN, BRANCH_W), lambda b, t: (b, 0, 0))
    heads = BRANCH_W // NA_HEAD_DIM
    vrows = NA_HEAD_DIM + ATT_ONES_ROWS
    return pl.pallas_call(
        functools.partial(_attn_prep_kernel, rope=rope, scale=head_dim ** -0.5 * LOG2E),
        grid=(DEC_BATCH, nt + 1),
        in_specs=[rowblk(first_col), rowblk(first_col + 1), rowblk(first_col + 2), ctx, ctx] + [tab, tab] * rope,
        out_specs=[
            pl.BlockSpec((1, BRANCH_W, tile), lambda b, t: (b, 0, jnp.minimum(t, last))),
            pl.BlockSpec((1, tile, BRANCH_W), lambda b, t: (b, t, 0)),
            pl.BlockSpec((1, heads, vrows, tile), lambda b, t: (b, 0, 0, t)),
        ],
        out_shape=[
            jax.ShapeDtypeStruct((DEC_BATCH, BRANCH_W, DEC_SEQ), BF16),
            jax.ShapeDtypeStruct((DEC_BATCH, ATT_KEYS, BRANCH_W), BF16),
            jax.ShapeDtypeStruct((DEC_BATCH, heads, vrows, ATT_KEYS), BF16),
        ],
        compiler_params=_cparams("arbitrary", "arbitrary"),
        name="attn_prep",
    )(u, u, u, k_ctx, v_ctx, *(rope_tables or ()))


NA_ROWS = ATT_TQ // GRID_W
NA_UNION = 3 * NA_ROWS
NA_STEPS = GRID_H // NA_ROWS
NA_SLABS = NA_UNION // NA_ROWS
NA_VARIANT_OFFSET = (0, -NA_ROWS, -2 * NA_ROWS)


def _na_variant(s):
    return jnp.minimum(s, 1) + s // (NA_STEPS - 1)


def _na_window_block(s):
    return jnp.clip(s - 1, 0, NA_STEPS - NA_SLABS)


def _na_bias_kernel(rpb_ref, o_ref):
    kc = lax.broadcasted_iota(jnp.int32, (GRID_W, GRID_W), 0)
    qc = lax.broadcasted_iota(jnp.int32, (GRID_W, GRID_W), 1)
    dc = jnp.clip(kc - qc, -(NA_WIN_COLS - 1), NA_WIN_COLS - 1) + (NA_WIN_COLS - 1)
    c0 = jnp.clip(qc - NA_WIN_COLS // 2, 0, GRID_W - NA_WIN_COLS)
    col_ok = (kc >= c0) & (kc < c0 + NA_WIN_COLS)
    r = rpb_ref[0, 0] * LOG2E
    masked = jnp.full((GRID_W, GRID_W), NEG_INF, F32)
    tiles = []
    for dr in range(2 * NA_WIN_ROWS - 1):
        acc = jnp.zeros((GRID_W, GRID_W), F32)
        for d in range(2 * NA_WIN_COLS - 1):
            acc = jnp.where(dc == d, r[dr:dr + 1, d:d + 1], acc)
        tiles.append(jnp.where(col_ok, acc, masked))
    for v, off in enumerate(NA_VARIANT_OFFSET):
        for kr in range(NA_UNION):
            for rr in range(NA_ROWS):
                w0 = (0, rr, NA_UNION - NA_WIN_ROWS)[v]
                dr = kr + off - rr
                inside = w0 <= kr < w0 + NA_WIN_ROWS
                o_ref[0, v, 0, kr * GRID_W:(kr + 1) * GRID_W, rr * GRID_W:(rr + 1) * GRID_W] = (
                    tiles[dr + NA_WIN_ROWS - 1] if inside else masked)


def _na_bias_table(na_rpb):
    n_dr, n_dc = 2 * NA_WIN_ROWS - 1, 2 * NA_WIN_COLS - 1
    nv = len(NA_VARIANT_OFFSET)
    return pl.pallas_call(
        _na_bias_kernel,
        grid=(DEPTH, NA_HEADS),
        in_specs=[pl.BlockSpec((1, 1, n_dr, n_dc), lambda l, h: (l, h, 0, 0))],
        out_specs=pl.BlockSpec((1, nv, 1, NA_UNION * GRID_W, ATT_TQ), lambda l, h: (l, 0, h, 0, 0)),
        out_shape=jax.ShapeDtypeStruct((DEPTH, nv, NA_HEADS, NA_UNION * GRID_W, ATT_TQ), F32),
        compiler_params=_cparams("arbitrary", "arbitrary"),
        name="na_bias_table",
    )(na_rpb)


def _na_kernel(qt_ref, *refs):
    n = NA_SLABS + 1
    k_refs, vt_refs = refs[:n], refs[n:2 * n]
    bias_ref, o_ref, acc_ref = refs[2 * n:]
    dv = NA_HEAD_DIM
    heads_per_group = LANES // dv
    for g in range(BRANCH_W // LANES):
        lanes = slice(g * LANES, (g + 1) * LANES)
        qbd = _masked_q_blocks(qt_ref[0, lanes, :], dv)
        keys = jnp.concatenate([k_ref[0, :, lanes] for k_ref in k_refs], axis=0)
        st = jnp.dot(keys, qbd, preferred_element_type=F32)
        n_win = NA_SLABS * ATT_TQ
        st_win = st[0:n_win] + jnp.concatenate(
            [bias_ref[0, g * heads_per_group + hh] for hh in range(heads_per_group)], axis=1)
        st_ctx = st[n_win:]
        mx = jnp.maximum(_colmax(st_win), _colmax(st_ctx))
        pt = jnp.concatenate([jnp.exp2(st_win - mx), jnp.exp2(st_ctx - mx)], axis=0).astype(BF16)
        for hh in range(heads_per_group):
            h = g * heads_per_group + hh
            ve = jnp.concatenate([vt_ref[0, h] for vt_ref in vt_refs], axis=1)
            oe = jnp.dot(ve, pt[:, hh * ATT_TQ:(hh + 1) * ATT_TQ], preferred_element_type=F32)
            acc_ref[h * dv:(h + 1) * dv, :] = oe[0:dv] / oe[dv:dv + 1]
    o_ref[...] = acc_ref[...].T.astype(o_ref.dtype)


def _nbr_attention(qt, k, vt, bias, l):
    vrows = NA_HEAD_DIM + ATT_ONES_ROWS
    ctx_blk = DEC_SEQ // ATT_TQ
    k_specs = [pl.BlockSpec((1, ATT_TQ, BRANCH_W), lambda b, s, j=j: (b, _na_window_block(s) + j, 0))
               for j in range(NA_SLABS)]
    k_specs.append(pl.BlockSpec((1, ATT_TQ, BRANCH_W), lambda b, s: (b, ctx_blk, 0)))
    vt_specs = [pl.BlockSpec((1, NA_HEADS, vrows, ATT_TQ), lambda b, s, j=j: (b, 0, 0, _na_window_block(s) + j))
                for j in range(NA_SLABS)]
    vt_specs.append(pl.BlockSpec((1, NA_HEADS, vrows, ATT_TQ), lambda b, s: (b, 0, 0, ctx_blk)))
    n = NA_SLABS + 1
    return pl.pallas_call(
        _na_kernel,
        grid=(DEC_BATCH, NA_STEPS),
        in_specs=[pl.BlockSpec((1, BRANCH_W, ATT_TQ), lambda b, s: (b, 0, s))] + k_specs + vt_specs + [
            pl.BlockSpec((None, 1, NA_HEADS, NA_UNION * GRID_W, ATT_TQ), lambda b, s: (l, _na_variant(s), 0, 0, 0))],
        out_specs=pl.BlockSpec((ATT_TQ, BRANCH_W), lambda b, s: (b * NA_STEPS + s, 0)),
        out_shape=jax.ShapeDtypeStruct((DEC_BATCH * DEC_SEQ, BRANCH_W), BF16),
        scratch_shapes=[pltpu.VMEM((BRANCH_W, ATT_TQ), F32)],
        compiler_params=_cparams("arbitrary", "arbitrary"),
        name="nbr_attention",
    )(qt, *([k] * n), *([vt] * n), bias)


DA_TQ = ATT_TQ
DA_KEYS = ATT_KEYS
DA_ONES_ROWS = ATT_ONES_ROWS
DA_MAPS_PER_TILE = LANES // DA_HEAD_DIM


def _da_kernel(qt_ref, k_ref, vt_ref, lam_ref, sub_ref, o_ref, acc_ref, kn_ref, *, lam_init):
    lam = _da_lambda(lam_ref, lam_init)
    heads = DA_MAPS_PER_TILE // 2
    w = 2 * DA_TQ
    r = lax.broadcasted_iota(jnp.int32, (LANES, LANES), 0) // DA_HEAD_DIM
    c = lax.broadcasted_iota(jnp.int32, (LANES, LANES), 1) // DA_HEAD_DIM
    same_map = (r == c).astype(F32)

    @pl.when(pl.program_id(1) == 0)
    def _():
        for g in range(BRANCH_W // LANES):
            lanes = slice(g * LANES, (g + 1) * LANES)
            kf = k_ref[0, :, lanes].astype(F32)
            n2 = _bdot(kf * kf, same_map)
            kn_ref[:, lanes] = jnp.max(n2, axis=0, keepdims=True)

    jrow = lax.broadcasted_iota(jnp.int32, (8, LANES), 0)
    dmap = lax.broadcasted_iota(jnp.int32, (8, LANES), 1) // DA_HEAD_DIM
    for g in range(BRANCH_W // LANES):
        lanes = slice(g * LANES, (g + 1) * LANES)
        qg = qt_ref[0, lanes, :]
        qbd = _masked_q_blocks(qg, DA_HEAD_DIM)
        qf = qg.astype(F32)
        b2 = _bdot(jnp.where(jrow == dmap, kn_ref[:, lanes], 0.0), qf * qf)
        bound = jnp.concatenate([jnp.sqrt(b2[j:j + 1]) for j in range(DA_MAPS_PER_TILE)], axis=1) * 1.01 + 1e-3

        def attend(carry, g=g, lanes=lanes, qbd=qbd):
            it, shift, _ = carry
            st = jnp.dot(k_ref[0, :, lanes], qbd, preferred_element_type=F32)
            pt = jnp.exp2(st - shift).astype(BF16)
            low = jnp.float32(jnp.inf)
            for hh in range(heads):
                h = g * heads + hh
                oe = jnp.dot(vt_ref[0, h], pt[:, hh * w:(hh + 1) * w], preferred_element_type=F32)
                den = oe[DA_V_DIM:DA_V_DIM + 1]
                low = jnp.minimum(low, jnp.min(den))
                os = [oe[0:DA_V_DIM, i * DA_TQ:(i + 1) * DA_TQ] / den[:, i * DA_TQ:(i + 1) * DA_TQ] for i in range(2)]
                ot = os[0] - lam * os[1]
                ot = ot * lax.rsqrt(jnp.mean(ot * ot, axis=0, keepdims=True) + EPS) * sub_ref[...]
                acc_ref[h * DA_V_DIM:(h + 1) * DA_V_DIM, :] = ot * (1.0 - lam_init)
            return it + 1, _colmax(st), low

        def again(carry):
            it, _, low = carry
            return (it == 0) | ((it == 1) & jnp.logical_not(low >= ATT_MIN_DENOM))

        lax.while_loop(again, attend, (jnp.int32(0), bound, jnp.float32(0.0)))
    o_ref[...] = acc_ref[...].T.astype(o_ref.dtype)


def _diff_attention(qt, k, vt, da_lambda, subln_col, lam_init):
    nt = DEC_SEQ // DA_TQ
    vrows = DA_V_DIM + DA_ONES_ROWS
    return pl.pallas_call(
        functools.partial(_da_kernel, lam_init=lam_init),
        grid=(DEC_BATCH, nt),
        in_specs=[
            pl.BlockSpec((1, BRANCH_W, DA_TQ), lambda b, t: (b, 0, t)),
            pl.BlockSpec((1, DA_KEYS, BRANCH_W), lambda b, t: (b, 0, 0)),
            pl.BlockSpec((1, DA_HEADS, vrows, DA_KEYS), lambda b, t: (b, 0, 0, 0)),
            pl.BlockSpec((4, DA_HEAD_DIM), lambda b, t: (0, 0)),
            pl.BlockSpec((DA_V_DIM, 1), lambda b, t: (0, 0)),
        ],
        out_specs=pl.BlockSpec((DA_TQ, BRANCH_W), lambda b, t: (b * nt + t, 0)),
        out_shape=jax.ShapeDtypeStruct((DEC_BATCH * DEC_SEQ, BRANCH_W), BF16),
        scratch_shapes=[pltpu.VMEM((BRANCH_W, DA_TQ), F32), pltpu.VMEM((1, BRANCH_W), F32)],
        compiler_params=_cparams("arbitrary", "arbitrary"),
        name="diff_attention",
    )(qt, k, vt, da_lambda, subln_col)


def _rope_tables():
    pos = np.arange(DEC_SEQ)
    row = (pos // GRID_W).astype(np.float32)
    col = (pos % GRID_W).astype(np.float32)
    n_freq = DA_HEAD_DIM // 4
    inv = (np.float32(ROPE_BASE) ** (-np.arange(n_freq, dtype=np.float32) / n_freq)).astype(np.float32)
    ang = np.concatenate([row[:, None] * inv[None, :], col[:, None] * inv[None, :]], axis=-1)
    ang = ang.astype(np.float64)
    cos = np.repeat(np.cos(ang), 2, axis=-1)
    sin = np.repeat(np.sin(ang), 2, axis=-1)
    sign = np.where(np.arange(DA_HEAD_DIM) % 2 == 0, -1.0, 1.0)
    reps = BRANCH_W // DA_HEAD_DIM
    cos = np.tile(cos, (1, reps)).astype(np.float32)
    sin = np.tile(sin * sign[None, :], (1, reps)).astype(np.float32)
    return jnp.asarray(cos), jnp.asarray(sin)


def _filt_hidden_kernel(feat_ref, w1_ref, b1_ref, w2_ref, b2_ref, fr_ref, o_ref):
    half = feat_ref.shape[0]
    fr = fr_ref[0]
    h = jnp.sin(fr * (jnp.dot(feat_ref[...], w1_ref[0], precision=HIGHEST, preferred_element_type=F32) + b1_ref[0]))
    h = jnp.sin(fr * (jnp.dot(h, w2_ref[0], precision=HIGHEST, preferred_element_type=F32) + b2_ref[0]))
    o_ref[0, 0:half] = h
    blk = MXU_DIM
    r = lax.broadcasted_iota(jnp.int32, (blk, blk), 0)
    c = lax.broadcasted_iota(jnp.int32, (blk, blk), 1)
    exchange = (r + c == blk - 1).astype(F32)
    nb = half // blk
    rev = jnp.concatenate(
        [jnp.dot(exchange, h[(nb - 1 - b) * blk:(nb - b) * blk], precision=HIGHEST, preferred_element_type=F32)
         for b in range(nb)], axis=0)
    o_ref[0, half:2 * half] = pltpu.roll(rev, 1, axis=0)


def _filt_kernel(h_ref, w3f_ref, w3b_ref, dec_ref, o_ref):
    L = dec_ref.shape[0] // 2
    hf = jnp.dot(h_ref[0, 0:L], w3f_ref[0], precision=HIGHEST, preferred_element_type=F32) * dec_ref[0:L]
    hb = jnp.dot(h_ref[0, L:2 * L], w3b_ref[0], precision=HIGHEST, preferred_element_type=F32) * dec_ref[L:2 * L]
    row = lax.broadcasted_iota(jnp.int32, hb.shape, 0)
    hb = jnp.where(row == 0, 0.0, hb)
    nrm = jnp.sum(jnp.abs(hf), axis=0, keepdims=True) + jnp.sum(jnp.abs(hb), axis=0, keepdims=True)
    o_ref[0, 0, 0:L] = hf / nrm
    o_ref[0, 0, L:2 * L] = hb / nrm


def _circular_order(a):
    return np.concatenate([a, a[:1], a[1:][::-1]], axis=0)


def _hyena_pos_tables(L):
    f32 = np.float32
    pos = np.arange(L, dtype=f32)
    t = (pos / f32(L)).astype(f32)
    bands = np.linspace(1e-4, HY_POS_BANDS - 1, HY_POS_BANDS, dtype=f32)
    ang = (f32(2 * math.pi / L) * pos[:, None] * bands[None, :]).astype(np.float64)
    feats = np.zeros((L, HY_FILT_HIDDEN), f32)
    feats[:, 0] = t
    feats[:, 1:1 + HY_POS_BANDS] = np.cos(ang)
    feats[:, 1 + HY_POS_BANDS:HY_POS_DIM] = -np.sin(ang)
    deltas = np.linspace(math.log(HY_DECAY_TARGET) / HY_SLOW_DECAY,
                         math.log(HY_DECAY_TARGET) / HY_FAST_DECAY, BRANCH_W, dtype=f32)
    decay = np.exp((-t[:, None] * np.abs(deltas)[None, :]).astype(np.float64)).astype(f32)
    return jnp.asarray(feats), jnp.asarray(_circular_order(decay))


def _hyena_filters(half, w1p, b1, w2, b2, w3, freq):
    feats, decay = _hyena_pos_tables(half)
    L = 2 * half
    cb = LANES
    ncb = BRANCH_W // cb
    small = lambda shape: pl.BlockSpec((1,) + shape, lambda l: (l, 0, 0))
    hidden = pl.pallas_call(
        _filt_hidden_kernel,
        grid=(DEPTH,),
        in_specs=[
            pl.BlockSpec((half, HY_FILT_HIDDEN), lambda l: (0, 0)),
            small((HY_FILT_HIDDEN, HY_FILT_HIDDEN)), small((1, HY_FILT_HIDDEN)),
            small((HY_FILT_HIDDEN, HY_FILT_HIDDEN)), small((1, HY_FILT_HIDDEN)),
            small((1, HY_FILT_HIDDEN)),
        ],
        out_specs=pl.BlockSpec((1, L, HY_FILT_HIDDEN), lambda l: (l, 0, 0)),
        out_shape=jax.ShapeDtypeStruct((DEPTH, L, HY_FILT_HIDDEN), F32),
        compiler_params=_cparams("arbitrary"),
        name=f"hyena_filter_hidden_{L}",
    )(feats, w1p, b1, w2, b2, freq)
    return pl.pallas_call(
        _filt_kernel,
        grid=(DEPTH, 2, ncb),
        in_specs=[
            pl.BlockSpec((1, L, HY_FILT_HIDDEN), lambda l, o, c: (l, 0, 0)),
            pl.BlockSpec((1, HY_FILT_HIDDEN, cb), lambda l, o, c: (l, 0, o * 2 * ncb + c)),
            pl.BlockSpec((1, HY_FILT_HIDDEN, cb), lambda l, o, c: (l, 0, o * 2 * ncb + ncb + c)),
            pl.BlockSpec((L, cb), lambda l, o, c: (0, c)),
        ],
        out_specs=pl.BlockSpec((1, 1, L, cb), lambda l, o, c: (l, o, 0, c)),
        out_shape=jax.ShapeDtypeStruct((DEPTH, 2, L, BRANCH_W), F32),
        compiler_params=_cparams("arbitrary", "arbitrary", "arbitrary"),
        name=f"hyena_filters_{L}",
    )(hidden, w3, w3, decay)


def _short_conv(u, w_ref, b_ref, seq_len):
    n = u.shape[0]
    t = lax.broadcasted_iota(jnp.int32, u.shape, 0) % seq_len
    prev = jnp.where(t == 0, 0.0, pltpu.roll(u, 1, axis=0))
    nxt = jnp.where(t == seq_len - 1, 0.0, pltpu.roll(u, n - 1, axis=0))
    return prev * w_ref[0:1, :] + u * w_ref[1:2, :] + nxt * w_ref[2:3, :] + b_ref[...]


def _dft_direct_mats():
    n, half = 2 * SEQ, SEQ
    k = np.arange(n)[:, None].astype(np.float64)
    t = np.arange(half)[None, :].astype(np.float64)
    ang = 2 * np.pi * k * t / n
    fr, fi = np.cos(ang), -np.sin(ang)
    mf = np.block([[fr, -fi], [fi, fr]])
    gr, gi = np.cos(ang).T / n, np.sin(ang).T / n
    mi = np.block([[gr, -gi], [gi, gr]])
    return mf.astype(np.float32), mi.astype(np.float32)


def _dft_real_mat():
    n = 2 * SEQ
    ang = 2 * np.pi * np.arange(n)[:, None].astype(np.float64) * np.arange(n)[None, :] / n
    return np.concatenate([np.cos(ang), -np.sin(ang)], axis=0).astype(np.float32)


def _spec_direct_kernel(h_ref, m_ref, o_ref):
    o_ref[0, 0] = jnp.dot(m_ref[...], h_ref[0, 0], precision=HIGHEST, preferred_element_type=F32)


def _spec_direct(h, m_real):
    n = 2 * SEQ
    return pl.pallas_call(
        _spec_direct_kernel,
        grid=(DEPTH, 2),
        in_specs=[pl.BlockSpec((1, 1, n, BRANCH_W), lambda l, o: (l, o, 0, 0)),
                  pl.BlockSpec((2 * n, n), lambda l, o: (0, 0))],
        out_specs=pl.BlockSpec((1, 1, 2 * n, BRANCH_W), lambda l, o: (l, o, 0, 0)),
        out_shape=jax.ShapeDtypeStruct((DEPTH, 2, 2 * n, BRANCH_W), F32),
        compiler_params=_cparams("arbitrary", "arbitrary"),
        name="hyena_spectrum_direct",
    )(h, m_real)


def _hyena_direct_kernel(u_ref, cw_ref, cb_ref, h_ref, bias_ref, mf_ref, mi_ref, o_ref):
    n = 2 * SEQ
    c = _short_conv(u_ref[...].astype(F32), cw_ref, cb_ref, SEQ)
    z = c[:, 0:BRANCH_W]
    for order in range(2):
        gate = c[:, (order + 1) * BRANCH_W:(order + 2) * BRANCH_W]
        f = jnp.dot(mf_ref[...], z.astype(BF16), preferred_element_type=F32)
        fr, fi = f[0:n], f[n:2 * n]
        hr, hi = h_ref[order, 0:n], h_ref[order, n:2 * n]
        y = jnp.concatenate([fr * hr - fi * hi, fr * hi + fi * hr], axis=0)
        y = jnp.dot(mi_ref[...], y.astype(BF16), preferred_element_type=F32)
        z = gate * (y + z * bias_ref[order:order + 1, :])
    o_ref[...] = z.astype(o_ref.dtype)


def _hyena_direct(u, conv_w, conv_b, spec, l, bias, mf, mi):
    n = 2 * SEQ
    rows = 2 * SEQ
    T = u.shape[0]
    return pl.pallas_call(
        _hyena_direct_kernel,
        grid=(T // rows,),
        in_specs=[
            pl.BlockSpec((rows, 3 * BRANCH_W), lambda p: (p, 0)),
            pl.BlockSpec((3, 3 * BRANCH_W), lambda p: (0, 0)),
            pl.BlockSpec((1, 3 * BRANCH_W), lambda p: (0, 0)),
            pl.BlockSpec((None, 2, 2 * n, BRANCH_W), lambda p: (l, 0, 0, 0)),
            pl.BlockSpec((2, BRANCH_W), lambda p: (0, 0)),
            pl.BlockSpec((2 * n, rows), lambda p: (0, 0)),
            pl.BlockSpec((rows, 2 * n), lambda p: (0, 0)),
        ],
        out_specs=pl.BlockSpec((rows, BRANCH_W), lambda p: (p, 0)),
        out_shape=jax.ShapeDtypeStruct((T, BRANCH_W), BF16),
        compiler_params=_cparams("arbitrary"),
        name="hyena_direct",
    )(u, conv_w, conv_b, spec, bias, mf, mi)


def _dft_two_stage_mats():
    no, ni, half, n = FFT_NO, FFT_NI, FFT_HALF, FFT_N
    f64 = np.float64
    k1 = np.arange(no, dtype=f64)
    n_o = np.arange(half, dtype=f64)
    n_i = np.arange(ni, dtype=f64)
    ang = 2 * np.pi * (n_i[:, None, None] * k1[None, :, None] / n + k1[None, :, None] * n_o[None, None, :] / no)
    tr, ti = np.cos(ang), -np.sin(ang)
    m1 = np.concatenate([np.concatenate([tr, -ti], axis=2), np.concatenate([ti, tr], axis=2)], axis=1)
    k2 = np.arange(ni, dtype=f64)
    ang2 = 2 * np.pi * k2[:, None] * n_i[None, :] / ni
    f2r, f2i = np.cos(ang2), -np.sin(ang2)
    m2 = np.block([[f2r, -f2i], [f2i, f2r]])
    m2c = np.block([[f2r, f2i], [-f2i, f2r]])
    sr, si = np.transpose(tr, (0, 2, 1)) / n, -np.transpose(ti, (0, 2, 1)) / n
    m3 = np.concatenate([np.concatenate([sr, -si], axis=2), np.concatenate([si, sr], axis=2)], axis=1)
    return (m1.astype(np.float32), m2.astype(np.float32), m2c.astype(np.float32), m3.astype(np.float32))


def _dft_stage1_real_mat():
    no, ni, n = FFT_NO, FFT_NI, FFT_N
    k1 = np.arange(no, dtype=np.float64)
    n_o = np.arange(no, dtype=np.float64)
    n_i = np.arange(ni, dtype=np.float64)
    ang = 2 * np.pi * (n_i[:, None, None] * k1[None, :, None] / n + k1[None, :, None] * n_o[None, None, :] / no)
    return np.concatenate([np.cos(ang), -np.sin(ang)], axis=1).astype(np.float32)


def _store_stage1(w_ref, ni, out):
    w_ref[pl.ds(ni, FFT_NO, stride=2 * FFT_NI), :] = out[0:FFT_NO]
    w_ref[pl.ds(FFT_NI + ni, FFT_NO, stride=2 * FFT_NI), :] = out[FFT_NO:2 * FFT_NO]


def _fwd_stage1(za_ref, zb_ref, m1_ref, w_ref):
    def body(ni, carry):
        a = za_ref[pl.ds(ni, FFT_HALF, stride=FFT_NI), :]
        b = zb_ref[pl.ds(ni, FFT_HALF, stride=FFT_NI), :]
        out = jnp.dot(m1_ref[ni], jnp.concatenate([a, b], axis=0).astype(BF16), preferred_element_type=F32)
        _store_stage1(w_ref, ni, out)
        return carry

    lax.fori_loop(0, FFT_NI, body, 0, unroll=FFT_UNROLL)


def _spec_two_stage_kernel(h_ref, m1_ref, m2_ref, o_ref, w_ref):
    h = h_ref.at[0, 0]

    def stage1(ni, carry):
        a = h[pl.ds(ni, FFT_NO, stride=FFT_NI), :]
        _store_stage1(w_ref, ni, jnp.dot(m1_ref[ni], a.astype(BF16), preferred_element_type=F32))
        return carry

    lax.fori_loop(0, FFT_NI, stage1, 0, unroll=FFT_UNROLL)
    blk = 2 * FFT_NI

    cb = w_ref.shape[1]

    def stage2(kp, carry):
        rows = [pl.ds(pl.multiple_of((2 * kp + j) * blk, blk), blk) for j in range(2)]
        x = jnp.dot(m2_ref[...], jnp.concatenate([w_ref[r, :] for r in rows], axis=1).astype(BF16),
                    preferred_element_type=F32)
        for j in range(2):
            o_ref[0, 0, rows[j], :] = x[:, j * cb:(j + 1) * cb]
        return carry

    lax.fori_loop(0, FFT_NO // 2, stage2, 0, unroll=FFT_MID_UNROLL)


def _spec_two_stage(h, m1_real, m2):
    cb = LCONV_CB
    return pl.pallas_call(
        _spec_two_stage_kernel,
        grid=(DEPTH, 2, BRANCH_W // cb),
        in_specs=[pl.BlockSpec((1, 1, FFT_N, cb), lambda l, o, c: (l, o, 0, c)),
                  pl.BlockSpec((FFT_NI, 2 * FFT_NO, FFT_NO), lambda l, o, c: (0, 0, 0)),
                  pl.BlockSpec((2 * FFT_NI, 2 * FFT_NI), lambda l, o, c: (0, 0))],
        out_specs=pl.BlockSpec((1, 1, 2 * FFT_N, cb), lambda l, o, c: (l, o, 0, c)),
        out_shape=jax.ShapeDtypeStruct((DEPTH, 2, 2 * FFT_N, BRANCH_W), F32),
        scratch_shapes=[pltpu.VMEM((2 * FFT_N, cb), F32)],
        compiler_params=_cparams("arbitrary", "arbitrary", "arbitrary"),
        name="hyena_spectrum_two_stage",
    )(h, m1_real, m2)


def _lconv_two_stage_kernel(s_ref, g_ref, cws_ref, cbs_ref, cwg_ref, cbg_ref, h_ref, bias_ref,
                            m1_ref, m2_ref, m2c_ref, m3_ref, o_ref, z_ref, w_ref, *, conv_sig):
    for b in range(2):
        sig = s_ref[b].astype(F32)
        if conv_sig:
            sig = _short_conv(sig, cws_ref, cbs_ref, DEC_SEQ)
        z_ref[b] = sig
    _fwd_stage1(z_ref.at[0], z_ref.at[1], m1_ref, w_ref)
    blk = 2 * FFT_NI

    cb = w_ref.shape[1]

    def mid(kp, carry):
        rows = [pl.ds(pl.multiple_of((2 * kp + j) * blk, blk), blk) for j in range(2)]
        x = jnp.dot(m2_ref[...], jnp.concatenate([w_ref[r, :] for r in rows], axis=1).astype(BF16),
                    preferred_element_type=F32)
        h = jnp.concatenate([h_ref[r, :] for r in rows], axis=1)
        xr, xi = x[0:FFT_NI], x[FFT_NI:blk]
        hr, hi = h[0:FFT_NI], h[FFT_NI:blk]
        y = jnp.concatenate([xr * hr - xi * hi, xr * hi + xi * hr], axis=0)
        c = jnp.dot(m2c_ref[...], y.astype(BF16), preferred_element_type=F32)
        for j in range(2):
            w_ref[rows[j], :] = c[:, j * cb:(j + 1) * cb]
        return carry

    lax.fori_loop(0, FFT_NO // 2, mid, 0, unroll=FFT_MID_UNROLL)

    def last(ni, carry):
        cr = w_ref[pl.ds(ni, FFT_NO, stride=blk), :]
        ci = w_ref[pl.ds(FFT_NI + ni, FFT_NO, stride=blk), :]
        y = jnp.dot(m3_ref[ni], jnp.concatenate([cr, ci], axis=0).astype(BF16), preferred_element_type=F32)
        o_ref[0, pl.ds(ni, FFT_HALF, stride=FFT_NI), :] = y[0:FFT_HALF]
        o_ref[1, pl.ds(ni, FFT_HALF, stride=FFT_NI), :] = y[FFT_HALF:2 * FFT_HALF]
        return carry

    lax.fori_loop(0, FFT_NI, last, 0, unroll=FFT_UNROLL)
    for b in range(2):
        gate = _short_conv(g_ref[b].astype(F32), cwg_ref, cbg_ref, DEC_SEQ)
        sig = z_ref[b]
        o_ref[b] = gate * (o_ref[b] + sig * bias_ref[...])


def _lconv_two_stage(sig, sig_col, gate_src, gate_col, conv_w, conv_b, spec, l, order, bias, mats, conv_sig):
    cb = LCONV_CB
    ncb = BRANCH_W // cb
    m1, m2, m2c, m3 = mats
    const3 = lambda c, p: (0, 0, 0)
    const2 = lambda c, p: (0, 0)
    return pl.pallas_call(
        functools.partial(_lconv_two_stage_kernel, conv_sig=conv_sig),
        grid=(ncb, DEC_BATCH // 2),
        in_specs=[
            pl.BlockSpec((2, DEC_SEQ, cb), lambda c, p: (p, 0, sig_col * ncb + c)),
            pl.BlockSpec((2, DEC_SEQ, cb), lambda c, p: (p, 0, gate_col * ncb + c)),
            pl.BlockSpec((3, cb), lambda c, p: (0, c)),
            pl.BlockSpec((1, cb), lambda c, p: (0, c)),
            pl.BlockSpec((3, cb), lambda c, p: (0, gate_col * ncb + c)),
            pl.BlockSpec((1, cb), lambda c, p: (0, gate_col * ncb + c)),
            pl.BlockSpec((None, None, 2 * FFT_N, cb), lambda c, p: (l, order, 0, c)),
            pl.BlockSpec((1, cb), lambda c, p: (0, c)),
            pl.BlockSpec(m1.shape, const3),
            pl.BlockSpec(m2.shape, const2),
            pl.BlockSpec(m2c.shape, const2),
            pl.BlockSpec(m3.shape, const3),
        ],
        out_specs=pl.BlockSpec((2, DEC_SEQ, cb), lambda c, p: (p, 0, c)),
        out_shape=jax.ShapeDtypeStruct((DEC_BATCH, DEC_SEQ, BRANCH_W), F32),
        scratch_shapes=[pltpu.VMEM((2, DEC_SEQ, cb), F32), pltpu.VMEM((2 * FFT_N, cb), F32)],
        compiler_params=_cparams("arbitrary", "arbitrary"),
        name="hyena_lconv_two_stage",
    )(sig, gate_src, conv_w, conv_b, conv_w, conv_b, spec, bias, m1, m2, m2c, m3)


def kernel(x_prompt, x_sample, cache_na_k, cache_na_v, cache_da_k, cache_da_v, c, c_ctx, w_ada, b_ada, norm_mix,
           norm_ffn, w_in, hy_conv_w, hy_conv_b, hy_filt_w1, hy_filt_b1, hy_filt_w2, hy_filt_b2, hy_filt_w3,
           hy_filt_freq, hy_bias, na_rpb, da_lambda, da_subln, w_lift, w_out, w_ffn_in, w_ffn_out, norm_final):
    TP, TS = BATCH * SEQ, DEC_BATCH * DEC_SEQ
    xp = x_prompt.reshape(TP, D_MODEL)
    xs = x_sample.reshape(TS, D_MODEL)

    cc = jnp.concatenate([c_ctx[None, :], c, jnp.zeros((8 - 1 - DEC_BATCH, D_MODEL), F32)], axis=0)
    mod = _modulation(cc, w_ada, b_ada)
    mod_p = mod[:, 0:1].reshape(DEPTH, 1, 1, 6 * D_MODEL)
    mod_s = mod[:, 1:1 + DEC_BATCH].reshape(DEPTH, DEC_BATCH, 1, 6 * D_MODEL)

    w_mix = w_in[:, :, :MIX_W].astype(BF16)
    w_gate = w_in[:, :, MIX_W:].astype(BF16)
    w_lift_b = w_lift.astype(BF16)
    w_out_b = w_out.astype(BF16)
    w_ffn_in_b = w_ffn_in.astype(BF16)
    w_ffn_out_b = w_ffn_out.astype(BF16)
    g_mix = norm_mix.reshape(DEPTH, 1, D_MODEL)
    g_ffn = norm_ffn.reshape(DEPTH, 1, D_MODEL)
    g_fin = norm_final.reshape(1, D_MODEL)
    subln = da_subln.reshape(DEPTH, 1, DA_V_DIM)
    subln_col = da_subln.reshape(DEPTH, DA_V_DIM, 1)

    w1p = jnp.pad(hy_filt_w1, ((0, 0), (0, HY_FILT_HIDDEN - HY_POS_DIM), (0, 0)))
    b1 = hy_filt_b1.reshape(DEPTH, 1, HY_FILT_HIDDEN)
    b2 = hy_filt_b2.reshape(DEPTH, 1, HY_FILT_HIDDEN)
    fr = hy_filt_freq.reshape(DEPTH, 1, HY_FILT_HIDDEN)
    mf, mi = _dft_direct_mats()
    mats = _dft_two_stage_mats()
    h_p = _hyena_filters(SEQ, w1p, b1, hy_filt_w2, b2, hy_filt_w3, fr)
    h_s = _hyena_filters(DEC_SEQ, w1p, b1, hy_filt_w2, b2, hy_filt_w3, fr)
    spec_p = _spec_direct(h_p, jnp.asarray(_dft_real_mat()))
    mf_b, mi_b = jnp.asarray(mf, dtype=BF16), jnp.asarray(mi, dtype=BF16)
    mats_b = tuple(jnp.asarray(m, dtype=BF16) for m in mats)
    spec_s = _spec_two_stage(h_s, jnp.asarray(_dft_stage1_real_mat(), dtype=BF16), mats_b[1])
    conv_b = hy_conv_b.reshape(DEPTH, 1, 3 * BRANCH_W)

    na_bias = _na_bias_table(na_rpb)
    rope_tables = _rope_tables()
    ck_na = cache_na_k.reshape(DEC_BATCH, DEPTH, PAST_LEN, BRANCH_W)
    cv_na = cache_na_v.reshape(DEC_BATCH, DEPTH, PAST_LEN, BRANCH_W)
    ck_da = cache_da_k.reshape(DEC_BATCH, DEPTH, PAST_LEN, BRANCH_W)
    cv_da = cache_da_v.reshape(DEC_BATCH, DEPTH, PAST_LEN, BRANCH_W)

    caches = tuple(jnp.zeros((BATCH, DEPTH, SEQ, BRANCH_W), F32) for _ in CACHE_BLOCKS)
    for l in range(DEPTH):
        lam_init = 0.8 - 0.6 * math.exp(-0.3 * l)
        final = l == DEPTH - 1

        u, caches = _in_proj(xp, g_mix[l], mod_p[l], w_mix, l, TP, BF16, caches=caches)
        y_hy = _hyena_direct(u, hy_conv_w[l], conv_b[l], spec_p, l, hy_bias[l], mf_b, mi_b)
        y_na, y_da = _ctx_attention(u, da_lambda[l], subln_col[l], lam_init)
        xp = _merge_out(xp, g_mix[l], mod_p[l], y_hy, y_na, y_da, w_gate, w_lift_b, w_out_b, l, TP)
        xp = _ffn(xp, g_ffn[l], mod_p[l], w_ffn_in_b, w_ffn_out_b, g_fin, l, TP, final)

        u = _in_proj(xs, g_mix[l], mod_s[l], w_mix, l, DEC_SEQ, BF16)
        u3 = u.reshape(DEC_BATCH, DEC_SEQ, MIX_W)
        z1 = _lconv_two_stage(u3, 0, u3, 1, hy_conv_w[l], conv_b[l], spec_s, l, 0, hy_bias[l, 0:1], mats_b, True)
        y_hy = _lconv_two_stage(z1, 0, u3, 2, hy_conv_w[l], conv_b[l], spec_s, l, 1, hy_bias[l, 1:2], mats_b, False)
        y_hy = y_hy.reshape(TS, BRANCH_W)
        qn, kn, vn = _attn_prep(u, 3, ck_na[:, l], cv_na[:, l], NA_HEAD_DIM)
        y_na = _nbr_attention(qn, kn, vn, na_bias, l)
        q, kt, v = _attn_prep(u, 6, ck_da[:, l], cv_da[:, l], DA_HEAD_DIM, rope_tables)
        y_da = _diff_attention(q, kt, v, da_lambda[l], subln_col[l], lam_init)
        xs = _merge_out(xs, g_mix[l], mod_s[l], y_hy, y_na, y_da, w_gate, w_lift_b, w_out_b, l, DEC_SEQ)
        xs = _ffn(xs, g_ffn[l], mod_s[l], w_ffn_in_b, w_ffn_out_b, g_fin, l, DEC_SEQ, final)

    y_prompt = xp.reshape(BATCH, SEQ, D_MODEL)
    y_sample = xs.reshape(DEC_BATCH, DEC_SEQ, D_MODEL)
    heads = lambda a, d: a.reshape(BATCH, DEPTH, SEQ, BRANCH_W // d, d)
    return (y_prompt, y_sample, heads(caches[0], NA_HEAD_DIM), heads(caches[1], NA_HEAD_DIM),
            heads(caches[2], 2 * DA_HEAD_DIM), heads(caches[3], DA_V_DIM))
```

```python
import functools
import math

import numpy as np
import jax
import jax.numpy as jnp
from jax import lax
from jax.experimental import pallas as pl
from jax.experimental.pallas import tpu as pltpu

F32 = jnp.float32
BF16 = jnp.bfloat16
HIGHEST = lax.Precision.HIGHEST

D_MODEL = 1024
BATCH = 32
SEQ = 256
DEPTH = 4
DEC_BATCH = 4
DEC_SEQ = 4096
PAST_LEN = 256
GRID_W = 64
GRID_H = DEC_SEQ // GRID_W
BRANCH_W = 512
HY_POS_BANDS = 16
HY_POS_DIM = 1 + 2 * HY_POS_BANDS
HY_FILT_HIDDEN = 64
HY_DECAY_TARGET = 1e-2
HY_FAST_DECAY = 0.3
HY_SLOW_DECAY = 1.5
NA_HEADS = 8
NA_HEAD_DIM = 64
NA_WIN_ROWS = 8
NA_WIN_COLS = 16
DA_HEADS = 8
DA_HEAD_DIM = 32
DA_V_DIM = 64
D_FF = 2816
MIX_W = 9 * BRANCH_W
ROPE_BASE = 10000.0
EPS = 1e-6
NEG_INF = -1e30

VMEM_LIMIT_BYTES = 56 * 1024 * 1024
LANES = 128
MXU_DIM = 256

FFT_N = 2 * DEC_SEQ
FFT_NO = 64
FFT_NI = 128
FFT_HALF = FFT_NO // 2
FFT_UNROLL = 8
FFT_MID_UNROLL = 16
LCONV_CB = LANES


def _cparams(*sem):
    return pltpu.CompilerParams(dimension_semantics=sem, vmem_limit_bytes=VMEM_LIMIT_BYTES)


def _sigmoid(x):
    return 1.0 / (1.0 + jnp.exp(-x))


def _rms(x, g):
    return x * lax.rsqrt(jnp.mean(x * x, axis=-1, keepdims=True) + EPS) * g


def _modnorm(x, g, shift, scale):
    return _rms(x, g) * (1.0 + scale) + shift


def _bdot(a, b):
    return jnp.dot(a.astype(BF16), b.astype(BF16), preferred_element_type=F32)


def _mod_kernel(c_ref, w_ref, b_ref, o_ref):
    c = c_ref[...]
    s = c * _sigmoid(c)
    o_ref[0] = jnp.dot(s, w_ref[0], precision=HIGHEST, preferred_element_type=F32) + b_ref[0]


def _modulation(cc, w_ada, b_ada):
    nt = 6
    return pl.pallas_call(
        _mod_kernel,
        grid=(DEPTH, nt),
        in_specs=[
            pl.BlockSpec((8, D_MODEL), lambda l, j: (0, 0)),
            pl.BlockSpec((1, D_MODEL, D_MODEL), lambda l, j: (l, 0, j)),
            pl.BlockSpec((1, 1, D_MODEL), lambda l, j: (l, 0, j)),
        ],
        out_specs=pl.BlockSpec((1, 8, D_MODEL), lambda l, j: (l, 0, j)),
        out_shape=jax.ShapeDtypeStruct((DEPTH, 8, 6 * D_MODEL), F32),
        compiler_params=_cparams("arbitrary", "arbitrary"),
        name="modulation",
    )(cc, w_ada, b_ada.reshape(DEPTH, 1, 6 * D_MODEL))


IN_TM = 512
CACHE_BLOCKS = (4, 5, 7, 8)


def _in_kernel(*refs, n_cache):
    x_ref, g_ref, mod_ref, w_ref = refs[:4]
    o_ref = refs[4 + n_cache]
    cache_refs = refs[5 + n_cache:]
    m = mod_ref[0]
    h = _modnorm(x_ref[...], g_ref[...], m[:, 0:D_MODEL], m[:, D_MODEL:2 * D_MODEL]).astype(BF16)
    res = jnp.dot(h, w_ref[...], preferred_element_type=F32)
    o_ref[...] = res.astype(o_ref.dtype)
    for c, c_ref in zip(CACHE_BLOCKS, cache_refs):
        c_ref[...] = res[:, c * BRANCH_W:(c + 1) * BRANCH_W].reshape(c_ref.shape)


def _in_proj(x, g, mod, w, l, rows_per_mod, out_dtype, caches=None):
    T = x.shape[0]
    tm = IN_TM
    per = rows_per_mod // tm
    in_specs = [
        pl.BlockSpec((tm, D_MODEL), lambda i: (i, 0)),
        pl.BlockSpec((1, D_MODEL), lambda i: (0, 0)),
        pl.BlockSpec((1, 1, 6 * D_MODEL), lambda i: (i // per, 0, 0)),
        pl.BlockSpec((None, D_MODEL, MIX_W), lambda i: (l, 0, 0), pipeline_mode=pl.Buffered(1)),
    ]
    out_specs = [pl.BlockSpec((tm, MIX_W), lambda i: (i, 0))]
    out_shape = [jax.ShapeDtypeStruct((T, MIX_W), out_dtype)]
    args = [x, g, mod, w]
    aliases = {}
    if caches is not None:
        out_specs += [pl.BlockSpec((tm // SEQ, 1, SEQ, BRANCH_W), lambda i: (i, l, 0, 0))] * len(caches)
        out_shape += [jax.ShapeDtypeStruct(c.shape, c.dtype) for c in caches]
        in_specs += [pl.BlockSpec(memory_space=pl.ANY)] * len(caches)
        aliases = {4 + n: 1 + n for n in range(len(caches))}
        args += list(caches)
    outs = pl.pallas_call(
        functools.partial(_in_kernel, n_cache=len(args) - 4),
        grid=(T // tm,),
        in_specs=in_specs,
        out_specs=out_specs,
        out_shape=out_shape,
        input_output_aliases=aliases,
        compiler_params=_cparams("arbitrary"),
        name="in_proj",
    )(*args)
    return outs[0] if caches is None else (outs[0], tuple(outs[1:]))


def _mid_kernel(x_ref, g_ref, mod_ref, yh_ref, yn_ref, yd_ref, wg_ref, wl_ref, wo_ref, o_ref):
    m = mod_ref[0]
    x = x_ref[...]
    h = _modnorm(x, g_ref[...], m[:, 0:D_MODEL], m[:, D_MODEL:2 * D_MODEL]).astype(BF16)
    merged = None
    for br, y_ref in enumerate((yh_ref, yn_ref, yd_ref)):
        gate = _sigmoid(jnp.dot(h, wg_ref[:, br * D_MODEL:(br + 1) * D_MODEL], preferred_element_type=F32))
        lift = jnp.dot(y_ref[...].astype(BF16), wl_ref[br], preferred_element_type=F32)
        t = gate * lift
        merged = t if merged is None else merged + t
    o_ref[...] = x + m[:, 2 * D_MODEL:3 * D_MODEL] * _bdot(merged, wo_ref[...])


def _merge_out(x, g, mod, y_hy, y_na, y_da, w_gate, w_lift, w_out, l, rows_per_mod):
    T = x.shape[0]
    tm = 512
    per = rows_per_mod // tm
    row = lambda i: (i, 0)
    const2 = lambda i: (0, 0)
    return pl.pallas_call(
        _mid_kernel,
        grid=(T // tm,),
        in_specs=[
            pl.BlockSpec((tm, D_MODEL), row),
            pl.BlockSpec((1, D_MODEL), const2),
            pl.BlockSpec((1, 1, 6 * D_MODEL), lambda i: (i // per, 0, 0)),
            pl.BlockSpec((tm, BRANCH_W), row),
            pl.BlockSpec((tm, BRANCH_W), row),
            pl.BlockSpec((tm, BRANCH_W), row),
            pl.BlockSpec((None, D_MODEL, 3 * D_MODEL), lambda i: (l, 0, 0)),
            pl.BlockSpec((None, 3, BRANCH_W, D_MODEL), lambda i: (l, 0, 0, 0)),
            pl.BlockSpec((None, D_MODEL, D_MODEL), lambda i: (l, 0, 0)),
        ],
        out_specs=pl.BlockSpec((tm, D_MODEL), row),
        out_shape=jax.ShapeDtypeStruct((T, D_MODEL), F32),
        compiler_params=_cparams("arbitrary"),
        name="merge_out",
    )(x, g, mod, y_hy, y_na, y_da, w_gate, w_lift, w_out)


def _ffn_kernel(x_ref, g_ref, mod_ref, w1_ref, w2_ref, gf_ref, o_ref, *, final):
    m = mod_ref[0]
    x = x_ref[...]
    h = _modnorm(x, g_ref[...], m[:, 3 * D_MODEL:4 * D_MODEL], m[:, 4 * D_MODEL:5 * D_MODEL]).astype(BF16)
    a = jnp.dot(h, w1_ref[:, 0:D_FF], preferred_element_type=F32)
    b = jnp.dot(h, w1_ref[:, D_FF:2 * D_FF], preferred_element_type=F32)
    xn = x + m[:, 5 * D_MODEL:6 * D_MODEL] * _bdot(a * _sigmoid(a) * b, w2_ref[...])
    if final:
        xn = _rms(xn, gf_ref[...])
    o_ref[...] = xn


def _ffn(x, g, mod, w_ffn_in, w_ffn_out, g_final, l, rows_per_mod, final):
    T = x.shape[0]
    tm = 512
    per = rows_per_mod // tm
    resident = pl.Buffered(1)
    return pl.pallas_call(
        functools.partial(_ffn_kernel, final=final),
        grid=(T // tm,),
        in_specs=[
            pl.BlockSpec((tm, D_MODEL), lambda i: (i, 0)),
            pl.BlockSpec((1, D_MODEL), lambda i: (0, 0)),
            pl.BlockSpec((1, 1, 6 * D_MODEL), lambda i: (i // per, 0, 0)),
            pl.BlockSpec((None, D_MODEL, 2 * D_FF), lambda i: (l, 0, 0), pipeline_mode=resident),
            pl.BlockSpec((None, D_FF, D_MODEL), lambda i: (l, 0, 0), pipeline_mode=resident),
            pl.BlockSpec((1, D_MODEL), lambda i: (0, 0)),
        ],
        out_specs=pl.BlockSpec((tm, D_MODEL), lambda i: (i, 0)),
        out_shape=jax.ShapeDtypeStruct((T, D_MODEL), F32),
        compiler_params=_cparams("arbitrary"),
        name="ffn",
    )(x, g, mod, w_ffn_in, w_ffn_out, g_final)


def _da_lambda(lam_ref, lam_init):
    lp = lam_ref[...]
    a = jnp.sum(lp[0:1] * lp[1:2], axis=1, keepdims=True)
    b = jnp.sum(lp[2:3] * lp[3:4], axis=1, keepdims=True)
    return jnp.exp(a) - jnp.exp(b) + lam_init


ATT_ONES_ROWS = 16
ATT_TQ = 256
ATT_PREP_T = 512
ATT_KEYS = DEC_SEQ + PAST_LEN
ATT_MIN_DENOM = 2.0 ** -64
LOG2E = math.log2(math.e)


def _masked_q_blocks(qt, d):
    row = lax.broadcasted_iota(jnp.int32, qt.shape, 0)
    zero = jnp.zeros_like(qt)
    return jnp.concatenate([jnp.where((row >= j * d) & (row < (j + 1) * d), qt, zero) for j in range(LANES // d)],
                           axis=1)


def _colmax(st):
    keys, n = st.shape
    return jnp.max(jnp.max(st.reshape(keys // MXU_DIM, MXU_DIM, n), axis=0), axis=0, keepdims=True)


def _ctx_attn_kernel(nq_ref, nk_ref, nv_ref, dq_ref, dk_ref, dv_ref, lam_ref, sub_ref, yn_ref, yd_ref, acc_ref,
                     *, lam_init):
    lam = _da_lambda(lam_ref, lam_init)
    ones = jnp.ones((ATT_ONES_ROWS, SEQ), BF16)

    def attend(q_ref, k_ref, v_ref, d, maps_per_head, finish):
        qt = (q_ref[...].astype(F32) * (d ** -0.5 * LOG2E)).T.astype(BF16)
        vt = v_ref[...].astype(F32).T.astype(BF16)
        kb = k_ref[...].astype(BF16)
        dv = NA_HEAD_DIM
        heads_per_group = LANES // (d * maps_per_head)
        w = maps_per_head * SEQ
        for g in range(BRANCH_W // LANES):
            lanes = slice(g * LANES, (g + 1) * LANES)
            st = jnp.dot(kb[:, lanes], _masked_q_blocks(qt[lanes], d), preferred_element_type=F32)
            pt = jnp.exp2(st - _colmax(st)).astype(BF16)
            for j in range(heads_per_group):
                h = g * heads_per_group + j
                ve = jnp.concatenate([vt[h * dv:(h + 1) * dv], ones], axis=0)
                oe = jnp.dot(ve, pt[:, j * w:(j + 1) * w], preferred_element_type=F32)
                os = [oe[0:dv, i * SEQ:(i + 1) * SEQ] / oe[dv:dv + 1, i * SEQ:(i + 1) * SEQ]
                      for i in range(maps_per_head)]
                acc_ref[h * dv:(h + 1) * dv, :] = finish(os)

    attend(nq_ref, nk_ref, nv_ref, NA_HEAD_DIM, 1, lambda os: os[0])
    yn_ref[...] = acc_ref[...].T.astype(yn_ref.dtype)

    def da_finish(os):
        ot = os[0] - lam * os[1]
        ot = ot * lax.rsqrt(jnp.mean(ot * ot, axis=0, keepdims=True) + EPS) * sub_ref[...]
        return ot * (1.0 - lam_init)

    attend(dq_ref, dk_ref, dv_ref, DA_HEAD_DIM, 2, da_finish)
    yd_ref[...] = acc_ref[...].T.astype(yd_ref.dtype)


def _ctx_attention(u, da_lambda, subln_col, lam_init):
    col = lambda j: pl.BlockSpec((SEQ, BRANCH_W), lambda b, j=j: (b, j))
    out = pl.BlockSpec((SEQ, BRANCH_W), lambda b: (b, 0))
    shape = jax.ShapeDtypeStruct((BATCH * SEQ, BRANCH_W), BF16)
    return pl.pallas_call(
        functools.partial(_ctx_attn_kernel, lam_init=lam_init),
        grid=(BATCH,),
        in_specs=[col(3), col(4), col(5), col(6), col(7), col(8),
                  pl.BlockSpec((4, DA_HEAD_DIM), lambda b: (0, 0)),
                  pl.BlockSpec((DA_V_DIM, 1), lambda b: (0, 0))],
        out_specs=[out, out],
        out_shape=[shape, shape],
        scratch_shapes=[pltpu.VMEM((BRANCH_W, SEQ), F32)],
        compiler_params=_cparams("arbitrary"),
        name="ctx_attention",
    )(u, u, u, u, u, u, da_lambda, subln_col)


def _rope(x, cos, sin_signed):
    n = x.shape[-1]
    lane = lax.broadcasted_iota(jnp.int32, x.shape, 1)
    partner = jnp.where(lane % 2 == 0, pltpu.roll(x, n - 1, axis=1), pltpu.roll(x, 1, axis=1))
    return x * cos + partner * sin_signed


def _attn_prep_kernel(q_ref, k_ref, v_ref, kc_ref, vc_ref, *refs, rope, scale):
    cos_ref, sin_ref = refs[:2] if rope else (None, None)
    qt_ref, ko_ref, vt_ref = refs[-3:]
    t = pl.program_id(1)
    dv = NA_HEAD_DIM

    def put_v(v):
        n = v.shape[0]
        vt = v.astype(F32).T.astype(BF16)
        ones = jnp.ones((ATT_ONES_ROWS, n), BF16)
        for h in range(BRANCH_W // dv):
            vt_ref[0, h, 0:dv, 0:n] = vt[h * dv:(h + 1) * dv]
            vt_ref[0, h, dv:dv + ATT_ONES_ROWS, 0:n] = ones

    @pl.when(t < DEC_SEQ // ATT_PREP_T)
    def _():
        q = q_ref[...].astype(F32)
        k = k_ref[...].astype(F32)
        if rope:
            cos = jnp.concatenate([cos_ref[...]] * (BRANCH_W // LANES), axis=1)
            sin = jnp.concatenate([sin_ref[...]] * (BRANCH_W // LANES), axis=1)
            q = _rope(q, cos, sin)
            k = _rope(k, cos, sin)
        qt_ref[0] = (q * scale).T.astype(BF16)
        ko_ref[0] = k.astype(BF16)
        put_v(v_ref[...])

    @pl.when(t == DEC_SEQ // ATT_PREP_T)
    def _():
        ko_ref[0, 0:PAST_LEN, :] = kc_ref[0].astype(BF16)
        put_v(vc_ref[0])


def _attn_prep(u, first_col, k_ctx, v_ctx, head_dim, rope_tables=None):
    rope = rope_tables is not None
    tile = ATT_PREP_T
    nt = DEC_SEQ // tile
    last = nt - 1
    rowblk = lambda j: pl.BlockSpec((tile, BRANCH_W), lambda b, t, j=j: (b * nt + jnp.minimum(t, last), j))
    tab = pl.BlockSpec((tile, LANES), lambda b, t: (jnp.minimum(t, last), 0))
    ctx = pl.BlockSpec((1, PAST_LEN, BRANCH_W), lambda b, t: (b, 0, 0))
    heads = BRANCH_W // NA_HEAD_DIM
    vrows = NA_HEAD_DIM + ATT_ONES_ROWS
    return pl.pallas_call(
        functools.partial(_attn_prep_kernel, rope=rope, scale=head_dim ** -0.5 * LOG2E),
        grid=(DEC_BATCH, nt + 1),
        in_specs=[rowblk(first_col), rowblk(first_col + 1), rowblk(first_col + 2), ctx, ctx] + [tab, tab] * rope,
        out_specs=[
            pl.BlockSpec((1, BRANCH_W, tile), lambda b, t: (b, 0, jnp.minimum(t, last))),
            pl.BlockSpec((1, tile, BRANCH_W), lambda b, t: (b, t, 0)),
            pl.BlockSpec((1, heads, vrows, tile), lambda b, t: (b, 0, 0, t)),
        ],
        out_shape=[
            jax.ShapeDtypeStruct((DEC_BATCH, BRANCH_W, DEC_SEQ), BF16),
            jax.ShapeDtypeStruct((DEC_BATCH, ATT_KEYS, BRANCH_W), BF16),
            jax.ShapeDtypeStruct((DEC_BATCH, heads, vrows, ATT_KEYS), BF16),
        ],
        compiler_params=_cparams("arbitrary", "arbitrary"),
        name="attn_prep",
    )(u, u, u, k_ctx, v_ctx, *(rope_tables or ()))


NA_ROWS = ATT_TQ // GRID_W
NA_UNION = 3 * NA_ROWS
NA_STEPS = GRID_H // NA_ROWS
NA_SLABS = NA_UNION // NA_ROWS
NA_VARIANT_OFFSET = (0, -NA_ROWS, -2 * NA_ROWS)


def _na_variant(s):
    return jnp.minimum(s, 1) + s // (NA_STEPS - 1)


def _na_window_block(s):
    return jnp.clip(s - 1, 0, NA_STEPS - NA_SLABS)


def _na_bias_kernel(rpb_ref, o_ref):
    kc = lax.broadcasted_iota(jnp.int32, (GRID_W, GRID_W), 0)
    qc = lax.broadcasted_iota(jnp.int32, (GRID_W, GRID_W), 1)
    dc = jnp.clip(kc - qc, -(NA_WIN_COLS - 1), NA_WIN_COLS - 1) + (NA_WIN_COLS - 1)
    c0 = jnp.clip(qc - NA_WIN_COLS // 2, 0, GRID_W - NA_WIN_COLS)
    col_ok = (kc >= c0) & (kc < c0 + NA_WIN_COLS)
    r = rpb_ref[0, 0] * LOG2E
    masked = jnp.full((GRID_W, GRID_W), NEG_INF, F32)
    tiles = []
    for dr in range(2 * NA_WIN_ROWS - 1):
        acc = jnp.zeros((GRID_W, GRID_W), F32)
        for d in range(2 * NA_WIN_COLS - 1):
            acc = jnp.where(dc == d, r[dr:dr + 1, d:d + 1], acc)
        tiles.append(jnp.where(col_ok, acc, masked))
    for v, off in enumerate(NA_VARIANT_OFFSET):
        for kr in range(NA_UNION):
            for rr in range(NA_ROWS):
                w0 = (0, rr, NA_UNION - NA_WIN_ROWS)[v]
                dr = kr + off - rr
                inside = w0 <= kr < w0 + NA_WIN_ROWS
                o_ref[0, v, 0, kr * GRID_W:(kr + 1) * GRID_W, rr * GRID_W:(rr + 1) * GRID_W] = (
                    tiles[dr + NA_WIN_ROWS - 1] if inside else masked)


def _na_bias_table(na_rpb):
    n_dr, n_dc = 2 * NA_WIN_ROWS - 1, 2 * NA_WIN_COLS - 1
    nv = len(NA_VARIANT_OFFSET)
    return pl.pallas_call(
        _na_bias_kernel,
        grid=(DEPTH, NA_HEADS),
        in_specs=[pl.BlockSpec((1, 1, n_dr, n_dc), lambda l, h: (l, h, 0, 0))],
        out_specs=pl.BlockSpec((1, nv, 1, NA_UNION * GRID_W, ATT_TQ), lambda l, h: (l, 0, h, 0, 0)),
        out_shape=jax.ShapeDtypeStruct((DEPTH, nv, NA_HEADS, NA_UNION * GRID_W, ATT_TQ), F32),
        compiler_params=_cparams("arbitrary", "arbitrary"),
        name="na_bias_table",
    )(na_rpb)


def _na_kernel(qt_ref, *refs):
    n = NA_SLABS + 1
    k_refs, vt_refs = refs[:n], refs[n:2 * n]
    bias_ref, o_ref, acc_ref = refs[2 * n:]
    dv = NA_HEAD_DIM
    heads_per_group = LANES // dv
    for g in range(BRANCH_W // LANES):
        lanes = slice(g * LANES, (g + 1) * LANES)
        qbd = _masked_q_blocks(qt_ref[0, lanes, :], dv)
        keys = jnp.concatenate([k_ref[0, :, lanes] for k_ref in k_refs], axis=0)
        st = jnp.dot(keys, qbd, preferred_element_type=F32)
        n_win = NA_SLABS * ATT_TQ
        st_win = st[0:n_win] + jnp.concatenate(
            [bias_ref[0, g * heads_per_group + hh] for hh in range(heads_per_group)], axis=1)
        st_ctx = st[n_win:]
        mx = jnp.maximum(_colmax(st_win), _colmax(st_ctx))
        pt = jnp.concatenate([jnp.exp2(st_win - mx), jnp.exp2(st_ctx - mx)], axis=0).astype(BF16)
        for hh in range(heads_per_group):
            h = g * heads_per_group + hh
            ve = jnp.concatenate([vt_ref[0, h] for vt_ref in vt_refs], axis=1)
            oe = jnp.dot(ve, pt[:, hh * ATT_TQ:(hh + 1) * ATT_TQ], preferred_element_type=F32)
            acc_ref[h * dv:(h + 1) * dv, :] = oe[0:dv] / oe[dv:dv + 1]
    o_ref[...] = acc_ref[...].T.astype(o_ref.dtype)


def _nbr_attention(qt, k, vt, bias, l):
    vrows = NA_HEAD_DIM + ATT_ONES_ROWS
    ctx_blk = DEC_SEQ // ATT_TQ
    k_specs = [pl.BlockSpec((1, ATT_TQ, BRANCH_W), lambda b, s, j=j: (b, _na_window_block(s) + j, 0))
               for j in range(NA_SLABS)]
    k_specs.append(pl.BlockSpec((1, ATT_TQ, BRANCH_W), lambda b, s: (b, ctx_blk, 0)))
    vt_specs = [pl.BlockSpec((1, NA_HEADS, vrows, ATT_TQ), lambda b, s, j=j: (b, 0, 0, _na_window_block(s) + j))
                for j in range(NA_SLABS)]
    vt_specs.append(pl.BlockSpec((1, NA_HEADS, vrows, ATT_TQ), lambda b, s: (b, 0, 0, ctx_blk)))
    n = NA_SLABS + 1
    return pl.pallas_call(
        _na_kernel,
        grid=(DEC_BATCH, NA_STEPS),
        in_specs=[pl.BlockSpec((1, BRANCH_W, ATT_TQ), lambda b, s: (b, 0, s))] + k_specs + vt_specs + [
            pl.BlockSpec((None, 1, NA_HEADS, NA_UNION * GRID_W, ATT_TQ), lambda b, s: (l, _na_variant(s), 0, 0, 0))],
        out_specs=pl.BlockSpec((ATT_TQ, BRANCH_W), lambda b, s: (b * NA_STEPS + s, 0)),
        out_shape=jax.ShapeDtypeStruct((DEC_BATCH * DEC_SEQ, BRANCH_W), BF16),
        scratch_shapes=[pltpu.VMEM((BRANCH_W, ATT_TQ), F32)],
        compiler_params=_cparams("arbitrary", "arbitrary"),
        name="nbr_attention",
    )(qt, *([k] * n), *([vt] * n), bias)


DA_TQ = ATT_TQ
DA_KEYS = ATT_KEYS
DA_ONES_ROWS = ATT_ONES_ROWS
DA_MAPS_PER_TILE = LANES // DA_HEAD_DIM


def _da_kernel(qt_ref, k_ref, vt_ref, lam_ref, sub_ref, o_ref, acc_ref, kn_ref, *, lam_init):
    lam = _da_lambda(lam_ref, lam_init)
    heads = DA_MAPS_PER_TILE // 2
    w = 2 * DA_TQ
    r = lax.broadcasted_iota(jnp.int32, (LANES, LANES), 0) // DA_HEAD_DIM
    c = lax.broadcasted_iota(jnp.int32, (LANES, LANES), 1) // DA_HEAD_DIM
    same_map = (r == c).astype(F32)

    @pl.when(pl.program_id(1) == 0)
    def _():
        for g in range(BRANCH_W // LANES):
            lanes = slice(g * LANES, (g + 1) * LANES)
            kf = k_ref[0, :, lanes].astype(F32)
            n2 = _bdot(kf * kf, same_map)
            kn_ref[:, lanes] = jnp.max(n2, axis=0, keepdims=True)

    jrow = lax.broadcasted_iota(jnp.int32, (8, LANES), 0)
    dmap = lax.broadcasted_iota(jnp.int32, (8, LANES), 1) // DA_HEAD_DIM
    for g in range(BRANCH_W // LANES):
        lanes = slice(g * LANES, (g + 1) * LANES)
        qg = qt_ref[0, lanes, :]
        qbd = _masked_q_blocks(qg, DA_HEAD_DIM)
        qf = qg.astype(F32)
        b2 = _bdot(jnp.where(jrow == dmap, kn_ref[:, lanes], 0.0), qf * qf)
        bound = jnp.concatenate([jnp.sqrt(b2[j:j + 1]) for j in range(DA_MAPS_PER_TILE)], axis=1) * 1.01 + 1e-3

        def attend(carry, g=g, lanes=lanes, qbd=qbd):
            it, shift, _ = carry
            st = jnp.dot(k_ref[0, :, lanes], qbd, preferred_element_type=F32)
            pt = jnp.exp2(st - shift).astype(BF16)
            low = jnp.float32(jnp.inf)
            for hh in range(heads):
                h = g * heads + hh
                oe = jnp.dot(vt_ref[0, h], pt[:, hh * w:(hh + 1) * w], preferred_element_type=F32)
                den = oe[DA_V_DIM:DA_V_DIM + 1]
                low = jnp.minimum(low, jnp.min(den))
                os = [oe[0:DA_V_DIM, i * DA_TQ:(i + 1) * DA_TQ] / den[:, i * DA_TQ:(i + 1) * DA_TQ] for i in range(2)]
                ot = os[0] - lam * os[1]
                ot = ot * lax.rsqrt(jnp.mean(ot * ot, axis=0, keepdims=True) + EPS) * sub_ref[...]
                acc_ref[h * DA_V_DIM:(h + 1) * DA_V_DIM, :] = ot * (1.0 - lam_init)
            return it + 1, _colmax(st), low

        def again(carry):
            it, _, low = carry
            return (it == 0) | ((it == 1) & jnp.logical_not(low >= ATT_MIN_DENOM))

        lax.while_loop(again, attend, (jnp.int32(0), bound, jnp.float32(0.0)))
    o_ref[...] = acc_ref[...].T.astype(o_ref.dtype)


def _diff_attention(qt, k, vt, da_lambda, subln_col, lam_init):
    nt = DEC_SEQ // DA_TQ
    vrows = DA_V_DIM + DA_ONES_ROWS
    return pl.pallas_call(
        functools.partial(_da_kernel, lam_init=lam_init),
        grid=(DEC_BATCH, nt),
        in_specs=[
            pl.BlockSpec((1, BRANCH_W, DA_TQ), lambda b, t: (b, 0, t)),
            pl.BlockSpec((1, DA_KEYS, BRANCH_W), lambda b, t: (b, 0, 0)),
            pl.BlockSpec((1, DA_HEADS, vrows, DA_KEYS), lambda b, t: (b, 0, 0, 0)),
            pl.BlockSpec((4, DA_HEAD_DIM), lambda b, t: (0, 0)),
            pl.BlockSpec((DA_V_DIM, 1), lambda b, t: (0, 0)),
        ],
        out_specs=pl.BlockSpec((DA_TQ, BRANCH_W), lambda b, t: (b * nt + t, 0)),
        out_shape=jax.ShapeDtypeStruct((DEC_BATCH * DEC_SEQ, BRANCH_W), BF16),
        scratch_shapes=[pltpu.VMEM((BRANCH_W, DA_TQ), F32), pltpu.VMEM((1, BRANCH_W), F32)],
        compiler_params=_cparams("arbitrary", "arbitrary"),
        name="diff_attention",
    )(qt, k, vt, da_lambda, subln_col)


def _rope_tables():
    pos = np.arange(DEC_SEQ)
    row = (pos // GRID_W).astype(np.float32)
    col = (pos % GRID_W).astype(np.float32)
    n_freq = DA_HEAD_DIM // 4
    inv = (np.float32(ROPE_BASE) ** (-np.arange(n_freq, dtype=np.float32) / n_freq)).astype(np.float32)
    ang = np.concatenate([row[:, None] * inv[None, :], col[:, None] * inv[None, :]], axis=-1)
    ang = ang.astype(np.float64)
    cos = np.repeat(np.cos(ang), 2, axis=-1)
    sin = np.repeat(np.sin(ang), 2, axis=-1)
    sign = np.where(np.arange(DA_HEAD_DIM) % 2 == 0, -1.0, 1.0)
    reps = LANES // DA_HEAD_DIM
    cos = np.tile(cos, (1, reps)).astype(np.float32)
    sin = np.tile(sin * sign[None, :], (1, reps)).astype(np.float32)
    return jnp.asarray(cos), jnp.asarray(sin)


def _filt_hidden_kernel(feat_ref, w1_ref, b1_ref, w2_ref, b2_ref, fr_ref, o_ref):
    half = feat_ref.shape[0]
    fr = fr_ref[0]
    h = jnp.sin(fr * (jnp.dot(feat_ref[...], w1_ref[0], precision=HIGHEST, preferred_element_type=F32) + b1_ref[0]))
    h = jnp.sin(fr * (jnp.dot(h, w2_ref[0], precision=HIGHEST, preferred_element_type=F32) + b2_ref[0]))
    o_ref[0, 0:half] = h
    blk = MXU_DIM
    r = lax.broadcasted_iota(jnp.int32, (blk, blk), 0)
    c = lax.broadcasted_iota(jnp.int32, (blk, blk), 1)
    exchange = (r + c == blk - 1).astype(F32)
    nb = half // blk
    rev = jnp.concatenate(
        [jnp.dot(exchange, h[(nb - 1 - b) * blk:(nb - b) * blk], precision=HIGHEST, preferred_element_type=F32)
         for b in range(nb)], axis=0)
    o_ref[0, half:2 * half] = pltpu.roll(rev, 1, axis=0)


def _filt_kernel(h_ref, w3f_ref, w3b_ref, dec_ref, o_ref):
    L = dec_ref.shape[0] // 2
    hf = jnp.dot(h_ref[0, 0:L], w3f_ref[0], precision=HIGHEST, preferred_element_type=F32) * dec_ref[0:L]
    hb = jnp.dot(h_ref[0, L:2 * L], w3b_ref[0], precision=HIGHEST, preferred_element_type=F32) * dec_ref[L:2 * L]
    row = lax.broadcasted_iota(jnp.int32, hb.shape, 0)
    hb = jnp.where(row == 0, 0.0, hb)
    nrm = jnp.sum(jnp.abs(hf), axis=0, keepdims=True) + jnp.sum(jnp.abs(hb), axis=0, keepdims=True)
    o_ref[0, 0, 0:L] = hf / nrm
    o_ref[0, 0, L:2 * L] = hb / nrm


def _circular_order(a):
    return np.concatenate([a, a[:1], a[1:][::-1]], axis=0)


def _hyena_pos_tables(L):
    f32 = np.float32
    pos = np.arange(L, dtype=f32)
    t = (pos / f32(L)).astype(f32)
    bands = np.linspace(1e-4, HY_POS_BANDS - 1, HY_POS_BANDS, dtype=f32)
    ang = (f32(2 * math.pi / L) * pos[:, None] * bands[None, :]).astype(np.float64)
    feats = np.zeros((L, HY_FILT_HIDDEN), f32)
    feats[:, 0] = t
    feats[:, 1:1 + HY_POS_BANDS] = np.cos(ang)
    feats[:, 1 + HY_POS_BANDS:HY_POS_DIM] = -np.sin(ang)
    deltas = np.linspace(math.log(HY_DECAY_TARGET) / HY_SLOW_DECAY,
                         math.log(HY_DECAY_TARGET) / HY_FAST_DECAY, BRANCH_W, dtype=f32)
    decay = np.exp((-t[:, None] * np.abs(deltas)[None, :]).astype(np.float64)).astype(f32)
    return jnp.asarray(feats), jnp.asarray(_circular_order(decay))


def _hyena_filters(half, w1p, b1, w2, b2, w3, freq):
    feats, decay = _hyena_pos_tables(half)
    L = 2 * half
    cb = LANES
    ncb = BRANCH_W // cb
    small = lambda shape: pl.BlockSpec((1,) + shape, lambda l: (l, 0, 0))
    hidden = pl.pallas_call(
        _filt_hidden_kernel,
        grid=(DEPTH,),
        in_specs=[
            pl.BlockSpec((half, HY_FILT_HIDDEN), lambda l: (0, 0)),
            small((HY_FILT_HIDDEN, HY_FILT_HIDDEN)), small((1, HY_FILT_HIDDEN)),
            small((HY_FILT_HIDDEN, HY_FILT_HIDDEN)), small((1, HY_FILT_HIDDEN)),
            small((1, HY_FILT_HIDDEN)),
        ],
        out_specs=pl.BlockSpec((1, L, HY_FILT_HIDDEN), lambda l: (l, 0, 0)),
        out_shape=jax.ShapeDtypeStruct((DEPTH, L, HY_FILT_HIDDEN), F32),
        compiler_params=_cparams("arbitrary"),
        name=f"hyena_filter_hidden_{L}",
    )(feats, w1p, b1, w2, b2, freq)
    return pl.pallas_call(
        _filt_kernel,
        grid=(DEPTH, 2, ncb),
        in_specs=[
            pl.BlockSpec((1, L, HY_FILT_HIDDEN), lambda l, o, c: (l, 0, 0)),
            pl.BlockSpec((1, HY_FILT_HIDDEN, cb), lambda l, o, c: (l, 0, o * 2 * ncb + c)),
            pl.BlockSpec((1, HY_FILT_HIDDEN, cb), lambda l, o, c: (l, 0, o * 2 * ncb + ncb + c)),
            pl.BlockSpec((L, cb), lambda l, o, c: (0, c)),
        ],
        out_specs=pl.BlockSpec((1, 1, L, cb), lambda l, o, c: (l, o, 0, c)),
        out_shape=jax.ShapeDtypeStruct((DEPTH, 2, L, BRANCH_W), F32),
        compiler_params=_cparams("arbitrary", "arbitrary", "arbitrary"),
        name=f"hyena_filters_{L}",
    )(hidden, w3, w3, decay)


def _short_conv(u, w_ref, b_ref, seq_len):
    n = u.shape[0]
    t = lax.broadcasted_iota(jnp.int32, u.shape, 0) % seq_len
    prev = jnp.where(t == 0, 0.0, pltpu.roll(u, 1, axis=0))
    nxt = jnp.where(t == seq_len - 1, 0.0, pltpu.roll(u, n - 1, axis=0))
    return prev * w_ref[0:1, :] + u * w_ref[1:2, :] + nxt * w_ref[2:3, :] + b_ref[...]


def _dft_direct_mats():
    n, half = 2 * SEQ, SEQ
    k = np.arange(n)[:, None].astype(np.float64)
    t = np.arange(half)[None, :].astype(np.float64)
    ang = 2 * np.pi * k * t / n
    fr, fi = np.cos(ang), -np.sin(ang)
    mf = np.block([[fr, -fi], [fi, fr]])
    gr, gi = np.cos(ang).T / n, np.sin(ang).T / n
    mi = np.block([[gr, -gi], [gi, gr]])
    return mf.astype(np.float32), mi.astype(np.float32)


def _dft_real_mat():
    n = 2 * SEQ
    ang = 2 * np.pi * np.arange(n)[:, None].astype(np.float64) * np.arange(n)[None, :] / n
    return np.concatenate([np.cos(ang), -np.sin(ang)], axis=0).astype(np.float32)


def _spec_direct_kernel(h_ref, m_ref, o_ref):
    o_ref[0, 0] = jnp.dot(m_ref[...], h_ref[0, 0], precision=HIGHEST, preferred_element_type=F32)


def _spec_direct(h, m_real):
    n = 2 * SEQ
    return pl.pallas_call(
        _spec_direct_kernel,
        grid=(DEPTH, 2),
        in_specs=[pl.BlockSpec((1, 1, n, BRANCH_W), lambda l, o: (l, o, 0, 0)),
                  pl.BlockSpec((2 * n, n), lambda l, o: (0, 0))],
        out_specs=pl.BlockSpec((1, 1, 2 * n, BRANCH_W), lambda l, o: (l, o, 0, 0)),
        out_shape=jax.ShapeDtypeStruct((DEPTH, 2, 2 * n, BRANCH_W), F32),
        compiler_params=_cparams("arbitrary", "arbitrary"),
        name="hyena_spectrum_direct",
    )(h, m_real)


def _hyena_direct_kernel(u_ref, cw_ref, cb_ref, h_ref, bias_ref, mf_ref, mi_ref, o_ref):
    n = 2 * SEQ
    c = _short_conv(u_ref[...].astype(F32), cw_ref, cb_ref, SEQ)
    z = c[:, 0:BRANCH_W]
    for order in range(2):
        gate = c[:, (order + 1) * BRANCH_W:(order + 2) * BRANCH_W]
        f = jnp.dot(mf_ref[...], z.astype(BF16), preferred_element_type=F32)
        fr, fi = f[0:n], f[n:2 * n]
        hr, hi = h_ref[order, 0:n], h_ref[order, n:2 * n]
        y = jnp.concatenate([fr * hr - fi * hi, fr * hi + fi * hr], axis=0)
        y = jnp.dot(mi_ref[...], y.astype(BF16), preferred_element_type=F32)
        z = gate * (y + z * bias_ref[order:order + 1, :])
    o_ref[...] = z.astype(o_ref.dtype)


def _hyena_direct(u, conv_w, conv_b, spec, l, bias, mf, mi):
    n = 2 * SEQ
    rows = 2 * SEQ
    T = u.shape[0]
    return pl.pallas_call(
        _hyena_direct_kernel,
        grid=(T // rows,),
        in_specs=[
            pl.BlockSpec((rows, 3 * BRANCH_W), lambda p: (p, 0)),
            pl.BlockSpec((3, 3 * BRANCH_W), lambda p: (0, 0)),
            pl.BlockSpec((1, 3 * BRANCH_W), lambda p: (0, 0)),
            pl.BlockSpec((None, 2, 2 * n, BRANCH_W), lambda p: (l, 0, 0, 0)),
            pl.BlockSpec((2, BRANCH_W), lambda p: (0, 0)),
            pl.BlockSpec((2 * n, rows), lambda p: (0, 0)),
            pl.BlockSpec((rows, 2 * n), lambda p: (0, 0)),
        ],
        out_specs=pl.BlockSpec((rows, BRANCH_W), lambda p: (p, 0)),
        out_shape=jax.ShapeDtypeStruct((T, BRANCH_W), BF16),
        compiler_params=_cparams("arbitrary"),
        name="hyena_direct",
    )(u, conv_w, conv_b, spec, bias, mf, mi)


def _dft_two_stage_mats():
    no, ni, half, n = FFT_NO, FFT_NI, FFT_HALF, FFT_N
    f64 = np.float64
    k1 = np.arange(no, dtype=f64)
    n_o = np.arange(half, dtype=f64)
    n_i = np.arange(ni, dtype=f64)
    ang = 2 * np.pi * (n_i[:, None, None] * k1[None, :, None] / n + k1[None, :, None] * n_o[None, None, :] / no)
    tr, ti = np.cos(ang), -np.sin(ang)
    m1 = np.concatenate([np.concatenate([tr, -ti], axis=2), np.concatenate([ti, tr], axis=2)], axis=1)
    k2 = np.arange(ni, dtype=f64)
    ang2 = 2 * np.pi * k2[:, None] * n_i[None, :] / ni
    f2r, f2i = np.cos(ang2), -np.sin(ang2)
    m2 = np.block([[f2r, -f2i], [f2i, f2r]])
    m2c = np.block([[f2r, f2i], [-f2i, f2r]])
    sr, si = np.transpose(tr, (0, 2, 1)) / n, -np.transpose(ti, (0, 2, 1)) / n
    m3 = np.concatenate([np.concatenate([sr, -si], axis=2), np.concatenate([si, sr], axis=2)], axis=1)
    return (m1.astype(np.float32), m2.astype(np.float32), m2c.astype(np.float32), m3.astype(np.float32))


def _dft_stage1_real_mat():
    no, ni, n = FFT_NO, FFT_NI, FFT_N
    k1 = np.arange(no, dtype=np.float64)
    n_o = np.arange(no, dtype=np.float64)
    n_i = np.arange(ni, dtype=np.float64)
    ang = 2 * np.pi * (n_i[:, None, None] * k1[None, :, None] / n + k1[None, :, None] * n_o[None, None, :] / no)
    return np.concatenate([np.cos(ang), -np.sin(ang)], axis=1).astype(np.float32)


def _store_stage1(w_ref, ni, out):
    w_ref[pl.ds(ni, FFT_NO, stride=2 * FFT_NI), :] = out[0:FFT_NO]
    w_ref[pl.ds(FFT_NI + ni, FFT_NO, stride=2 * FFT_NI), :] = out[FFT_NO:2 * FFT_NO]


def _fwd_stage1(za_ref, zb_ref, m1_ref, w_ref):
    def body(ni, carry):
        a = za_ref[pl.ds(ni, FFT_HALF, stride=FFT_NI), :]
        b = zb_ref[pl.ds(ni, FFT_HALF, stride=FFT_NI), :]
        out = jnp.dot(m1_ref[ni], jnp.concatenate([a, b], axis=0).astype(BF16), preferred_element_type=F32)
        _store_stage1(w_ref, ni, out)
        return carry

    lax.fori_loop(0, FFT_NI, body, 0, unroll=FFT_UNROLL)


def _spec_two_stage_kernel(h_ref, m1_ref, m2_ref, o_ref, w_ref):
    h = h_ref.at[0, 0]

    def stage1(ni, carry):
        a = h[pl.ds(ni, FFT_NO, stride=FFT_NI), :]
        _store_stage1(w_ref, ni, jnp.dot(m1_ref[ni], a.astype(BF16), preferred_element_type=F32))
        return carry

    lax.fori_loop(0, FFT_NI, stage1, 0, unroll=FFT_UNROLL)
    blk = 2 * FFT_NI

    cb = w_ref.shape[1]

    def stage2(kp, carry):
        rows = [pl.ds(pl.multiple_of((2 * kp + j) * blk, blk), blk) for j in range(2)]
        x = jnp.dot(m2_ref[...], jnp.concatenate([w_ref[r, :] for r in rows], axis=1).astype(BF16),
                    preferred_element_type=F32)
        for j in range(2):
            o_ref[0, 0, rows[j], :] = x[:, j * cb:(j + 1) * cb]
        return carry

    lax.fori_loop(0, FFT_NO // 2, stage2, 0, unroll=FFT_MID_UNROLL)


def _spec_two_stage(h, m1_real, m2):
    cb = LCONV_CB
    return pl.pallas_call(
        _spec_two_stage_kernel,
        grid=(DEPTH, 2, BRANCH_W // cb),
        in_specs=[pl.BlockSpec((1, 1, FFT_N, cb), lambda l, o, c: (l, o, 0, c)),
                  pl.BlockSpec((FFT_NI, 2 * FFT_NO, FFT_NO), lambda l, o, c: (0, 0, 0)),
                  pl.BlockSpec((2 * FFT_NI, 2 * FFT_NI), lambda l, o, c: (0, 0))],
        out_specs=pl.BlockSpec((1, 1, 2 * FFT_N, cb), lambda l, o, c: (l, o, 0, c)),
        out_shape=jax.ShapeDtypeStruct((DEPTH, 2, 2 * FFT_N, BRANCH_W), F32),
        scratch_shapes=[pltpu.VMEM((2 * FFT_N, cb), F32)],
        compiler_params=_cparams("arbitrary", "arbitrary", "arbitrary"),
        name="hyena_spectrum_two_stage",
    )(h, m1_real, m2)


def _lconv_two_stage_kernel(s_ref, g_ref, cws_ref, cbs_ref, cwg_ref, cbg_ref, h_ref, bias_ref,
                            m1_ref, m2_ref, m2c_ref, m3_ref, o_ref, z_ref, w_ref, *, conv_sig):
    for b in range(2):
        sig = s_ref[b].astype(F32)
        if conv_sig:
            sig = _short_conv(sig, cws_ref, cbs_ref, DEC_SEQ)
        z_ref[b] = sig
    _fwd_stage1(z_ref.at[0], z_ref.at[1], m1_ref, w_ref)
    blk = 2 * FFT_NI

    cb = w_ref.shape[1]

    def mid(kp, carry):
        rows = [pl.ds(pl.multiple_of((2 * kp + j) * blk, blk), blk) for j in range(2)]
        x = jnp.dot(m2_ref[...], jnp.concatenate([w_ref[r, :] for r in rows], axis=1).astype(BF16),
                    preferred_element_type=F32)
        h = jnp.concatenate([h_ref[r, :] for r in rows], axis=1)
        xr, xi = x[0:FFT_NI], x[FFT_NI:blk]
        hr, hi = h[0:FFT_NI], h[FFT_NI:blk]
        y = jnp.concatenate([xr * hr - xi * hi, xr * hi + xi * hr], axis=0)
        c = jnp.dot(m2c_ref[...], y.astype(BF16), preferred_element_type=F32)
        for j in range(2):
            w_ref[rows[j], :] = c[:, j * cb:(j + 1) * cb]
        return carry

    lax.fori_loop(0, FFT_NO // 2, mid, 0, unroll=FFT_MID_UNROLL)

    def last(ni, carry):
        cr = w_ref[pl.ds(ni, FFT_NO, stride=blk), :]
        ci = w_ref[pl.ds(FFT_NI + ni, FFT_NO, stride=blk), :]
        y = jnp.dot(m3_ref[ni], jnp.concatenate([cr, ci], axis=0).astype(BF16), preferred_element_type=F32)
        o_ref[0, pl.ds(ni, FFT_HALF, stride=FFT_NI), :] = y[0:FFT_HALF]
        o_ref[1, pl.ds(ni, FFT_HALF, stride=FFT_NI), :] = y[FFT_HALF:2 * FFT_HALF]
        return carry

    lax.fori_loop(0, FFT_NI, last, 0, unroll=FFT_UNROLL)
    for b in range(2):
        gate = _short_conv(g_ref[b].astype(F32), cwg_ref, cbg_ref, DEC_SEQ)
        sig = z_ref[b]
        o_ref[b] = gate * (o_ref[b] + sig * bias_ref[...])


def _lconv_two_stage(sig, sig_col, gate_src, gate_col, conv_w, conv_b, spec, l, order, bias, mats, conv_sig):
    cb = LCONV_CB
    ncb = BRANCH_W // cb
    m1, m2, m2c, m3 = mats
    const3 = lambda c, p: (0, 0, 0)
    const2 = lambda c, p: (0, 0)
    return pl.pallas_call(
        functools.partial(_lconv_two_stage_kernel, conv_sig=conv_sig),
        grid=(ncb, DEC_BATCH // 2),
        in_specs=[
            pl.BlockSpec((2, DEC_SEQ, cb), lambda c, p: (p, 0, sig_col * ncb + c)),
            pl.BlockSpec((2, DEC_SEQ, cb), lambda c, p: (p, 0, gate_col * ncb + c)),
            pl.BlockSpec((3, cb), lambda c, p: (0, c)),
            pl.BlockSpec((1, cb), lambda c, p: (0, c)),
            pl.BlockSpec((3, cb), lambda c, p: (0, gate_col * ncb + c)),
            pl.BlockSpec((1, cb), lambda c, p: (0, gate_col * ncb + c)),
            pl.BlockSpec((None, None, 2 * FFT_N, cb), lambda c, p: (l, order, 0, c)),
            pl.BlockSpec((1, cb), lambda c, p: (0, c)),
            pl.BlockSpec(m1.shape, const3),
            pl.BlockSpec(m2.shape, const2),
            pl.BlockSpec(m2c.shape, const2),
            pl.BlockSpec(m3.shape, const3),
        ],
        out_specs=pl.BlockSpec((2, DEC_SEQ, cb), lambda c, p: (p, 0, c)),
        out_shape=jax.ShapeDtypeStruct((DEC_BATCH, DEC_SEQ, BRANCH_W), F32),
        scratch_shapes=[pltpu.VMEM((2, DEC_SEQ, cb), F32), pltpu.VMEM((2 * FFT_N, cb), F32)],
        compiler_params=_cparams("arbitrary", "arbitrary"),
        name="hyena_lconv_two_stage",
    )(sig, gate_src, conv_w, conv_b, conv_w, conv_b, spec, bias, m1, m2, m2c, m3)


def kernel(x_prompt, x_sample, cache_na_k, cache_na_v, cache_da_k, cache_da_v, c, c_ctx, w_ada, b_ada, norm_mix,
           norm_ffn, w_in, hy_conv_w, hy_conv_b, hy_filt_w1, hy_filt_b1, hy_filt_w2, hy_filt_b2, hy_filt_w3,
           hy_filt_freq, hy_bias, na_rpb, da_lambda, da_subln, w_lift, w_out, w_ffn_in, w_ffn_out, norm_final):
    TP, TS = BATCH * SEQ, DEC_BATCH * DEC_SEQ
    xp = x_prompt.reshape(TP, D_MODEL)
    xs = x_sample.reshape(TS, D_MODEL)

    cc = jnp.concatenate([c_ctx[None, :], c, jnp.zeros((8 - 1 - DEC_BATCH, D_MODEL), F32)], axis=0)
    mod = _modulation(cc, w_ada, b_ada)
    mod_p = mod[:, 0:1].reshape(DEPTH, 1, 1, 6 * D_MODEL)
    mod_s = mod[:, 1:1 + DEC_BATCH].reshape(DEPTH, DEC_BATCH, 1, 6 * D_MODEL)

    w_mix = w_in[:, :, :MIX_W].astype(BF16)
    w_gate = w_in[:, :, MIX_W:].astype(BF16)
    w_lift_b = w_lift.astype(BF16)
    w_out_b = w_out.astype(BF16)
    w_ffn_in_b = w_ffn_in.astype(BF16)
    w_ffn_out_b = w_ffn_out.astype(BF16)
    g_mix = norm_mix.reshape(DEPTH, 1, D_MODEL)
    g_ffn = norm_ffn.reshape(DEPTH, 1, D_MODEL)
    g_fin = norm_final.reshape(1, D_MODEL)
    subln = da_subln.reshape(DEPTH, 1, DA_V_DIM)
    subln_col = da_subln.reshape(DEPTH, DA_V_DIM, 1)

    w1p = jnp.pad(hy_filt_w1, ((0, 0), (0, HY_FILT_HIDDEN - HY_POS_DIM), (0, 0)))
    b1 = hy_filt_b1.reshape(DEPTH, 1, HY_FILT_HIDDEN)
    b2 = hy_filt_b2.reshape(DEPTH, 1, HY_FILT_HIDDEN)
    fr = hy_filt_freq.reshape(DEPTH, 1, HY_FILT_HIDDEN)
    mf, mi = _dft_direct_mats()
    mats = _dft_two_stage_mats()
    h_p = _hyena_filters(SEQ, w1p, b1, hy_filt_w2, b2, hy_filt_w3, fr)
    h_s = _hyena_filters(DEC_SEQ, w1p, b1, hy_filt_w2, b2, hy_filt_w3, fr)
    spec_p = _spec_direct(h_p, jnp.asarray(_dft_real_mat()))
    mf_b, mi_b = jnp.asarray(mf, dtype=BF16), jnp.asarray(mi, dtype=BF16)
    mats_b = tuple(jnp.asarray(m, dtype=BF16) for m in mats)
    spec_s = _spec_two_stage(h_s, jnp.asarray(_dft_stage1_real_mat(), dtype=BF16), mats_b[1])
    conv_b = hy_conv_b.reshape(DEPTH, 1, 3 * BRANCH_W)

    na_bias = _na_bias_table(na_rpb)
    rope_tables = _rope_tables()
    ck_na = cache_na_k.reshape(DEC_BATCH, DEPTH, PAST_LEN, BRANCH_W)
    cv_na = cache_na_v.reshape(DEC_BATCH, DEPTH, PAST_LEN, BRANCH_W)
    ck_da = cache_da_k.reshape(DEC_BATCH, DEPTH, PAST_LEN, BRANCH_W)
    cv_da = cache_da_v.reshape(DEC_BATCH, DEPTH, PAST_LEN, BRANCH_W)

    caches = tuple(jnp.zeros((BATCH, DEPTH, SEQ, BRANCH_W), F32) for _ in CACHE_BLOCKS)
    for l in range(DEPTH):
        lam_init = 0.8 - 0.6 * math.exp(-0.3 * l)
        final = l == DEPTH - 1

        u, caches = _in_proj(xp, g_mix[l], mod_p[l], w_mix, l, TP, BF16, caches=caches)
        y_hy = _hyena_direct(u, hy_conv_w[l], conv_b[l], spec_p, l, hy_bias[l], mf_b, mi_b)
        y_na, y_da = _ctx_attention(u, da_lambda[l], subln_col[l], lam_init)
        xp = _merge_out(xp, g_mix[l], mod_p[l], y_hy, y_na, y_da, w_gate, w_lift_b, w_out_b, l, TP)
        xp = _ffn(xp, g_ffn[l], mod_p[l], w_ffn_in_b, w_ffn_out_b, g_fin, l, TP, final)

        u = _in_proj(xs, g_mix[l], mod_s[l], w_mix, l, DEC_SEQ, BF16)
        u3 = u.reshape(DEC_BATCH, DEC_SEQ, MIX_W)
        z1 = _lconv_two_stage(u3, 0, u3, 1, hy_conv_w[l], conv_b[l], spec_s, l, 0, hy_bias[l, 0:1], mats_b, True)
        y_hy = _lconv_two_stage(z1, 0, u3, 2, hy_conv_w[l], conv_b[l], spec_s, l, 1, hy_bias[l, 1:2], mats_b, False)
        y_hy = y_hy.reshape(TS, BRANCH_W)
        qn, kn, vn = _attn_prep(u, 3, ck_na[:, l], cv_na[:, l], NA_HEAD_DIM)
        y_na = _nbr_attention(qn, kn, vn, na_bias, l)
        q, kt, v = _attn_prep(u, 6, ck_da[:, l], cv_da[:, l], DA_HEAD_DIM, rope_tables)
        y_da = _diff_attention(q, kt, v, da_lambda[l], subln_col[l], lam_init)
        xs = _merge_out(xs, g_mix[l], mod_s[l], y_hy, y_na, y_da, w_gate, w_lift_b, w_out_b, l, DEC_SEQ)
        xs = _ffn(xs, g_ffn[l], mod_s[l], w_ffn_in_b, w_ffn_out_b, g_fin, l, DEC_SEQ, final)

    y_prompt = xp.reshape(BATCH, SEQ, D_MODEL)
    y_sample = xs.reshape(DEC_BATCH, DEC_SEQ, D_MODEL)
    heads = lambda a, d: a.reshape(BATCH, DEPTH, SEQ, BRANCH_W // d, d)
    return (y_prompt, y_sample, heads(caches[0], NA_HEAD_DIM), heads(caches[1], NA_HEAD_DIM),
            heads(caches[2], 2 * DA_HEAD_DIM), heads(caches[3], DA_V_DIM))
```

```python
import functools
import math

import numpy as np
import jax
import jax.numpy as jnp
from jax import lax
from jax.experimental import pallas as pl
from jax.experimental.pallas import tpu as pltpu

F32 = jnp.float32
BF16 = jnp.bfloat16
HIGHEST = lax.Precision.HIGHEST

D_MODEL = 1024
BATCH = 32
SEQ = 256
DEPTH = 4
DEC_BATCH = 4
DEC_SEQ = 4096
PAST_LEN = 256
GRID_W = 64
GRID_H = DEC_SEQ // GRID_W
BRANCH_W = 512
HY_POS_BANDS = 16
HY_POS_DIM = 1 + 2 * HY_POS_BANDS
HY_FILT_HIDDEN = 64
HY_DECAY_TARGET = 1e-2
HY_FAST_DECAY = 0.3
HY_SLOW_DECAY = 1.5
NA_HEADS = 8
NA_HEAD_DIM = 64
NA_WIN_ROWS = 8
NA_WIN_COLS = 16
DA_HEADS = 8
DA_HEAD_DIM = 32
DA_V_DIM = 64
D_FF = 2816
MIX_W = 9 * BRANCH_W
ROPE_BASE = 10000.0
EPS = 1e-6
NEG_INF = -1e30

VMEM_LIMIT_BYTES = 56 * 1024 * 1024
LANES = 128
MXU_DIM = 256

FFT_N = 2 * DEC_SEQ
FFT_NO = 64
FFT_NI = 128
FFT_HALF = FFT_NO // 2
FFT_UNROLL = 8
FFT_MID_UNROLL = 16
FFT_W_PITCH = 2 * FFT_NI + 8
LCONV_CB = LANES


def _cparams(*sem):
    return pltpu.CompilerParams(dimension_semantics=sem, vmem_limit_bytes=VMEM_LIMIT_BYTES)


def _sigmoid(x):
    return 1.0 / (1.0 + jnp.exp(-x))


def _rms(x, g):
    return x * lax.rsqrt(jnp.mean(x * x, axis=-1, keepdims=True) + EPS) * g


def _modnorm(x, g, shift, scale):
    return _rms(x, g) * (1.0 + scale) + shift


def _bdot(a, b):
    return jnp.dot(a.astype(BF16), b.astype(BF16), preferred_element_type=F32)


def _mod_kernel(c_ref, w_ref, b_ref, o_ref):
    c = c_ref[...]
    s = c * _sigmoid(c)
    o_ref[0] = jnp.dot(s, w_ref[0], precision=HIGHEST, preferred_element_type=F32) + b_ref[0]


def _modulation(cc, w_ada, b_ada):
    nt = 6
    return pl.pallas_call(
        _mod_kernel,
        grid=(DEPTH, nt),
        in_specs=[
            pl.BlockSpec((8, D_MODEL), lambda l, j: (0, 0)),
            pl.BlockSpec((1, D_MODEL, D_MODEL), lambda l, j: (l, 0, j)),
            pl.BlockSpec((1, 1, D_MODEL), lambda l, j: (l, 0, j)),
        ],
        out_specs=pl.BlockSpec((1, 8, D_MODEL), lambda l, j: (l, 0, j)),
        out_shape=jax.ShapeDtypeStruct((DEPTH, 8, 6 * D_MODEL), F32),
        compiler_params=_cparams("arbitrary", "arbitrary"),
        name="modulation",
    )(cc, w_ada, b_ada.reshape(DEPTH, 1, 6 * D_MODEL))


IN_TM = 512
CACHE_BLOCKS = (4, 5, 7, 8)


def _in_kernel(*refs, n_cache):
    x_ref, g_ref, mod_ref, w_ref = refs[:4]
    o_ref = refs[4 + n_cache]
    cache_refs = refs[5 + n_cache:]
    m = mod_ref[0]
    h = _modnorm(x_ref[...], g_ref[...], m[:, 0:D_MODEL], m[:, D_MODEL:2 * D_MODEL]).astype(BF16)
    res = jnp.dot(h, w_ref[...], preferred_element_type=F32)
    o_ref[...] = res.astype(o_ref.dtype)
    for c, c_ref in zip(CACHE_BLOCKS, cache_refs):
        c_ref[...] = res[:, c * BRANCH_W:(c + 1) * BRANCH_W].reshape(c_ref.shape)


def _in_proj(x, g, mod, w, l, rows_per_mod, out_dtype, caches=None):
    T = x.shape[0]
    tm = IN_TM
    per = rows_per_mod // tm
    in_specs = [
        pl.BlockSpec((tm, D_MODEL), lambda i: (i, 0)),
        pl.BlockSpec((1, D_MODEL), lambda i: (0, 0)),
        pl.BlockSpec((1, 1, 6 * D_MODEL), lambda i: (i // per, 0, 0)),
        pl.BlockSpec((None, D_MODEL, MIX_W), lambda i: (l, 0, 0), pipeline_mode=pl.Buffered(1)),
    ]
    out_specs = [pl.BlockSpec((tm, MIX_W), lambda i: (i, 0))]
    out_shape = [jax.ShapeDtypeStruct((T, MIX_W), out_dtype)]
    args = [x, g, mod, w]
    aliases = {}
    if caches is not None:
        out_specs += [pl.BlockSpec((tm // SEQ, 1, SEQ, BRANCH_W), lambda i: (i, l, 0, 0))] * len(caches)
        out_shape += [jax.ShapeDtypeStruct(c.shape, c.dtype) for c in caches]
        in_specs += [pl.BlockSpec(memory_space=pl.ANY)] * len(caches)
        aliases = {4 + n: 1 + n for n in range(len(caches))}
        args += list(caches)
    outs = pl.pallas_call(
        functools.partial(_in_kernel, n_cache=len(args) - 4),
        grid=(T // tm,),
        in_specs=in_specs,
        out_specs=out_specs,
        out_shape=out_shape,
        input_output_aliases=aliases,
        compiler_params=_cparams("arbitrary"),
        name="in_proj",
    )(*args)
    return outs[0] if caches is None else (outs[0], tuple(outs[1:]))


def _mid_kernel(x_ref, g_ref, mod_ref, yh_ref, yn_ref, yd_ref, wg_ref, wl_ref, wo_ref, o_ref):
    m = mod_ref[0]
    x = x_ref[...]
    h = _modnorm(x, g_ref[...], m[:, 0:D_MODEL], m[:, D_MODEL:2 * D_MODEL]).astype(BF16)
    merged = None
    for br, y_ref in enumerate((yh_ref, yn_ref, yd_ref)):
        gate = _sigmoid(jnp.dot(h, wg_ref[:, br * D_MODEL:(br + 1) * D_MODEL], preferred_element_type=F32))
        lift = jnp.dot(y_ref[...].astype(BF16), wl_ref[br], preferred_element_type=F32)
        t = gate * lift
        merged = t if merged is None else merged + t
    o_ref[...] = x + m[:, 2 * D_MODEL:3 * D_MODEL] * _bdot(merged, wo_ref[...])


def _merge_out(x, g, mod, y_hy, y_na, y_da, w_gate, w_lift, w_out, l, rows_per_mod):
    T = x.shape[0]
    tm = 512
    per = rows_per_mod // tm
    row = lambda i: (i, 0)
    const2 = lambda i: (0, 0)
    return pl.pallas_call(
        _mid_kernel,
        grid=(T // tm,),
        in_specs=[
            pl.BlockSpec((tm, D_MODEL), row),
            pl.BlockSpec((1, D_MODEL), const2),
            pl.BlockSpec((1, 1, 6 * D_MODEL), lambda i: (i // per, 0, 0)),
            pl.BlockSpec((tm, BRANCH_W), row),
            pl.BlockSpec((tm, BRANCH_W), row),
            pl.BlockSpec((tm, BRANCH_W), row),
            pl.BlockSpec((None, D_MODEL, 3 * D_MODEL), lambda i: (l, 0, 0)),
            pl.BlockSpec((None, 3, BRANCH_W, D_MODEL), lambda i: (l, 0, 0, 0)),
            pl.BlockSpec((None, D_MODEL, D_MODEL), lambda i: (l, 0, 0)),
        ],
        out_specs=pl.BlockSpec((tm, D_MODEL), row),
        out_shape=jax.ShapeDtypeStruct((T, D_MODEL), F32),
        compiler_params=_cparams("arbitrary"),
        name="merge_out",
    )(x, g, mod, y_hy, y_na, y_da, w_gate, w_lift, w_out)


def _ffn_kernel(x_ref, g_ref, mod_ref, w1_ref, w2_ref, gf_ref, o_ref, *, final):
    m = mod_ref[0]
    x = x_ref[...]
    h = _modnorm(x, g_ref[...], m[:, 3 * D_MODEL:4 * D_MODEL], m[:, 4 * D_MODEL:5 * D_MODEL]).astype(BF16)
    a = jnp.dot(h, w1_ref[:, 0:D_FF], preferred_element_type=F32)
    b = jnp.dot(h, w1_ref[:, D_FF:2 * D_FF], preferred_element_type=F32)
    xn = x + m[:, 5 * D_MODEL:6 * D_MODEL] * _bdot(a * _sigmoid(a) * b, w2_ref[...])
    if final:
        xn = _rms(xn, gf_ref[...])
    o_ref[...] = xn


def _ffn(x, g, mod, w_ffn_in, w_ffn_out, g_final, l, rows_per_mod, final):
    T = x.shape[0]
    tm = 512
    per = rows_per_mod // tm
    resident = pl.Buffered(1)
    return pl.pallas_call(
        functools.partial(_ffn_kernel, final=final),
        grid=(T // tm,),
        in_specs=[
            pl.BlockSpec((tm, D_MODEL), lambda i: (i, 0)),
            pl.BlockSpec((1, D_MODEL), lambda i: (0, 0)),
            pl.BlockSpec((1, 1, 6 * D_MODEL), lambda i: (i // per, 0, 0)),
            pl.BlockSpec((None, D_MODEL, 2 * D_FF), lambda i: (l, 0, 0), pipeline_mode=resident),
            pl.BlockSpec((None, D_FF, D_MODEL), lambda i: (l, 0, 0), pipeline_mode=resident),
            pl.BlockSpec((1, D_MODEL), lambda i: (0, 0)),
        ],
        out_specs=pl.BlockSpec((tm, D_MODEL), lambda i: (i, 0)),
        out_shape=jax.ShapeDtypeStruct((T, D_MODEL), F32),
        compiler_params=_cparams("arbitrary"),
        name="ffn",
    )(x, g, mod, w_ffn_in, w_ffn_out, g_final)


def _da_lambda(lam_ref, lam_init):
    lp = lam_ref[...]
    a = jnp.sum(lp[0:1] * lp[1:2], axis=1, keepdims=True)
    b = jnp.sum(lp[2:3] * lp[3:4], axis=1, keepdims=True)
    return jnp.exp(a) - jnp.exp(b) + lam_init


ATT_ONES_ROWS = 16
ATT_TQ = 256
ATT_PREP_T = 512
ATT_KEYS = DEC_SEQ + PAST_LEN
ATT_MIN_DENOM = 2.0 ** -64
LOG2E = math.log2(math.e)


def _masked_q_blocks(qt, d):
    row = lax.broadcasted_iota(jnp.int32, qt.shape, 0)
    zero = jnp.zeros_like(qt)
    return jnp.concatenate([jnp.where((row >= j * d) & (row < (j + 1) * d), qt, zero) for j in range(LANES // d)],
                           axis=1)


def _colmax(st):
    keys, n = st.shape
    return jnp.max(jnp.max(st.reshape(keys // MXU_DIM, MXU_DIM, n), axis=0), axis=0, keepdims=True)


def _ctx_attn_kernel(nq_ref, nk_ref, nv_ref, dq_ref, dk_ref, dv_ref, lam_ref, sub_ref, yn_ref, yd_ref, acc_ref,
                     *, lam_init):
    lam = _da_lambda(lam_ref, lam_init)
    ones = jnp.ones((ATT_ONES_ROWS, SEQ), BF16)

    def attend(q_ref, k_ref, v_ref, d, maps_per_head, finish):
        qt = (q_ref[...].astype(F32) * (d ** -0.5 * LOG2E)).T.astype(BF16)
        vt = v_ref[...].astype(F32).T.astype(BF16)
        kb = k_ref[...].astype(BF16)
        dv = NA_HEAD_DIM
        heads_per_group = LANES // (d * maps_per_head)
        w = maps_per_head * SEQ
        for g in range(BRANCH_W // LANES):
            lanes = slice(g * LANES, (g + 1) * LANES)
            st = jnp.dot(kb[:, lanes], _masked_q_blocks(qt[lanes], d), preferred_element_type=F32)
            pt = jnp.exp2(st - _colmax(st)).astype(BF16)
            for j in range(heads_per_group):
                h = g * heads_per_group + j
                ve = jnp.concatenate([vt[h * dv:(h + 1) * dv], ones], axis=0)
                oe = jnp.dot(ve, pt[:, j * w:(j + 1) * w], preferred_element_type=F32)
                os = [oe[0:dv, i * SEQ:(i + 1) * SEQ] / oe[dv:dv + 1, i * SEQ:(i + 1) * SEQ]
                      for i in range(maps_per_head)]
                acc_ref[h * dv:(h + 1) * dv, :] = finish(os)

    attend(nq_ref, nk_ref, nv_ref, NA_HEAD_DIM, 1, lambda os: os[0])
    yn_ref[...] = acc_ref[...].T.astype(yn_ref.dtype)

    def da_finish(os):
        ot = os[0] - lam * os[1]
        ot = ot * lax.rsqrt(jnp.mean(ot * ot, axis=0, keepdims=True) + EPS) * sub_ref[...]
        return ot * (1.0 - lam_init)

    attend(dq_ref, dk_ref, dv_ref, DA_HEAD_DIM, 2, da_finish)
    yd_ref[...] = acc_ref[...].T.astype(yd_ref.dtype)


def _ctx_attention(u, da_lambda, subln_col, lam_init):
    col = lambda j: pl.BlockSpec((SEQ, BRANCH_W), lambda b, j=j: (b, j))
    out = pl.BlockSpec((SEQ, BRANCH_W), lambda b: (b, 0))
    shape = jax.ShapeDtypeStruct((BATCH * SEQ, BRANCH_W), BF16)
    return pl.pallas_call(
        functools.partial(_ctx_attn_kernel, lam_init=lam_init),
        grid=(BATCH,),
        in_specs=[col(3), col(4), col(5), col(6), col(7), col(8),
                  pl.BlockSpec((4, DA_HEAD_DIM), lambda b: (0, 0)),
                  pl.BlockSpec((DA_V_DIM, 1), lambda b: (0, 0))],
        out_specs=[out, out],
        out_shape=[shape, shape],
        scratch_shapes=[pltpu.VMEM((BRANCH_W, SEQ), F32)],
        compiler_params=_cparams("arbitrary"),
        name="ctx_attention",
    )(u, u, u, u, u, u, da_lambda, subln_col)


def _rope(x, cos, sin_signed):
    n = x.shape[-1]
    lane = lax.broadcasted_iota(jnp.int32, x.shape, 1)
    partner = jnp.where(lane % 2 == 0, pltpu.roll(x, n - 1, axis=1), pltpu.roll(x, 1, axis=1))
    return x * cos + partner * sin_signed


def _attn_prep_kernel(q_ref, k_ref, v_ref, kc_ref, vc_ref, *refs, rope, scale):
    cos_ref, sin_ref = refs[:2] if rope else (None, None)
    qt_ref, ko_ref, vt_ref = refs[-3:]
    t = pl.program_id(1)
    dv = NA_HEAD_DIM

    def put_v(v):
        n = v.shape[0]
        vt = v.astype(F32).T.astype(BF16)
        ones = jnp.ones((ATT_ONES_ROWS, n), BF16)
        for h in range(BRANCH_W // dv):
            vt_ref[0, h, 0:dv, 0:n] = vt[h * dv:(h + 1) * dv]
            vt_ref[0, h, dv:dv + ATT_ONES_ROWS, 0:n] = ones

    @pl.when(t < DEC_SEQ // ATT_PREP_T)
    def _():
        q = q_ref[...].astype(F32)
        k = k_ref[...].astype(F32)
        if rope:
            cos = jnp.concatenate([cos_ref[...]] * (BRANCH_W // LANES), axis=1)
            sin = jnp.concatenate([sin_ref[...]] * (BRANCH_W // LANES), axis=1)
            q = _rope(q, cos, sin)
            k = _rope(k, cos, sin)
        qt_ref[0] = (q * scale).T.astype(BF16)
        ko_ref[0] = k.astype(BF16)
        put_v(v_ref[...])

    @pl.when(t == DEC_SEQ // ATT_PREP_T)
    def _():
        ko_ref[0, 0:PAST_LEN, :] = kc_ref[0].astype(BF16)
        put_v(vc_ref[0])


def _attn_prep(u, first_col, k_ctx, v_ctx, head_dim, rope_tables=None):
    rope = rope_tables is not None
    tile = ATT_PREP_T
    nt = DEC_SEQ // tile
    last = nt - 1
    rowblk = lambda j: pl.BlockSpec((tile, BRANCH_W), lambda b, t, j=j: (b * nt + jnp.minimum(t, last), j))
    tab = pl.BlockSpec((tile, LANES), lambda b, t: (jnp.minimum(t, last), 0))
    ctx = pl.BlockSpec((1, PAST_LEN, BRANCH_W), lambda b, t: (b, 0, 0))
    heads = BRANCH_W // NA_HEAD_DIM
    vrows = NA_HEAD_DIM + ATT_ONES_ROWS
    return pl.pallas_call(
        functools.partial(_attn_prep_kernel, rope=rope, scale=head_dim ** -0.5 * LOG2E),
        grid=(DEC_BATCH, nt + 1),
        in_specs=[rowblk(first_col), rowblk(first_col + 1), rowblk(first_col + 2), ctx, ctx] + [tab, tab] * rope,
        out_specs=[
            pl.BlockSpec((1, BRANCH_W, tile), lambda b, t: (b, 0, jnp.minimum(t, last))),
            pl.BlockSpec((1, tile, BRANCH_W), lambda b, t: (b, t, 0)),
            pl.BlockSpec((1, heads, vrows, tile), lambda b, t: (b, 0, 0, t)),
        ],
        out_shape=[
            jax.ShapeDtypeStruct((DEC_BATCH, BRANCH_W, DEC_SEQ), BF16),
            jax.ShapeDtypeStruct((DEC_BATCH, ATT_KEYS, BRANCH_W), BF16),
            jax.ShapeDtypeStruct((DEC_BATCH, heads, vrows, ATT_KEYS), BF16),
        ],
        compiler_params=_cparams("arbitrary", "arbitrary"),
        name="attn_prep",
    )(u, u, u, k_ctx, v_ctx, *(rope_tables or ()))


NA_ROWS = ATT_TQ // GRID_W
NA_UNION = 3 * NA_ROWS
NA_STEPS = GRID_H // NA_ROWS
NA_SLABS = NA_UNION // NA_ROWS
NA_VARIANT_OFFSET = (0, -NA_ROWS, -2 * NA_ROWS)


def _na_variant(s):
    return jnp.minimum(s, 1) + s // (NA_STEPS - 1)


def _na_window_block(s):
    return jnp.clip(s - 1, 0, NA_STEPS - NA_SLABS)


def _na_bias_kernel(rpb_ref, o_ref):
    kc = lax.broadcasted_iota(jnp.int32, (GRID_W, GRID_W), 0)
    qc = lax.broadcasted_iota(jnp.int32, (GRID_W, GRID_W), 1)
    dc = jnp.clip(kc - qc, -(NA_WIN_COLS - 1), NA_WIN_COLS - 1) + (NA_WIN_COLS - 1)
    c0 = jnp.clip(qc - NA_WIN_COLS // 2, 0, GRID_W - NA_WIN_COLS)
    col_ok = (kc >= c0) & (kc < c0 + NA_WIN_COLS)
    r = rpb_ref[0, 0] * LOG2E
    masked = jnp.full((GRID_W, GRID_W), NEG_INF, F32)
    tiles = []
    for dr in range(2 * NA_WIN_ROWS - 1):
        acc = jnp.zeros((GRID_W, GRID_W), F32)
        for d in range(2 * NA_WIN_COLS - 1):
            acc = jnp.where(dc == d, r[dr:dr + 1, d:d + 1], acc)
        tiles.append(jnp.where(col_ok, acc, masked))
    for v, off in enumerate(NA_VARIANT_OFFSET):
        for kr in range(NA_UNION):
            for rr in range(NA_ROWS):
                w0 = (0, rr, NA_UNION - NA_WIN_ROWS)[v]
                dr = kr + off - rr
                inside = w0 <= kr < w0 + NA_WIN_ROWS
                o_ref[0, v, 0, kr * GRID_W:(kr + 1) * GRID_W, rr * GRID_W:(rr + 1) * GRID_W] = (
                    tiles[dr + NA_WIN_ROWS - 1] if inside else masked)


def _na_bias_table(na_rpb):
    n_dr, n_dc = 2 * NA_WIN_ROWS - 1, 2 * NA_WIN_COLS - 1
    nv = len(NA_VARIANT_OFFSET)
    return pl.pallas_call(
        _na_bias_kernel,
        grid=(DEPTH, NA_HEADS),
        in_specs=[pl.BlockSpec((1, 1, n_dr, n_dc), lambda l, h: (l, h, 0, 0))],
        out_specs=pl.BlockSpec((1, nv, 1, NA_UNION * GRID_W, ATT_TQ), lambda l, h: (l, 0, h, 0, 0)),
        out_shape=jax.ShapeDtypeStruct((DEPTH, nv, NA_HEADS, NA_UNION * GRID_W, ATT_TQ), F32),
        compiler_params=_cparams("arbitrary", "arbitrary"),
        name="na_bias_table",
    )(na_rpb)


def _na_kernel(qt_ref, *refs):
    n = NA_SLABS + 1
    k_refs, vt_refs = refs[:n], refs[n:2 * n]
    bias_ref, o_ref, acc_ref = refs[2 * n:]
    dv = NA_HEAD_DIM
    heads_per_group = LANES // dv
    for g in range(BRANCH_W // LANES):
        lanes = slice(g * LANES, (g + 1) * LANES)
        qbd = _masked_q_blocks(qt_ref[0, lanes, :], dv)
        keys = jnp.concatenate([k_ref[0, :, lanes] for k_ref in k_refs], axis=0)
        st = jnp.dot(keys, qbd, preferred_element_type=F32)
        n_win = NA_SLABS * ATT_TQ
        st_win = st[0:n_win] + jnp.concatenate(
            [bias_ref[0, g * heads_per_group + hh] for hh in range(heads_per_group)], axis=1)
        st_ctx = st[n_win:]
        mx = jnp.maximum(_colmax(st_win), _colmax(st_ctx))
        pt = jnp.concatenate([jnp.exp2(st_win - mx), jnp.exp2(st_ctx - mx)], axis=0).astype(BF16)
        for hh in range(heads_per_group):
            h = g * heads_per_group + hh
            ve = jnp.concatenate([vt_ref[0, h] for vt_ref in vt_refs], axis=1)
            oe = jnp.dot(ve, pt[:, hh * ATT_TQ:(hh + 1) * ATT_TQ], preferred_element_type=F32)
            acc_ref[h * dv:(h + 1) * dv, :] = oe[0:dv] / oe[dv:dv + 1]
    o_ref[...] = acc_ref[...].T.astype(o_ref.dtype)


def _nbr_attention(qt, k, vt, bias, l):
    vrows = NA_HEAD_DIM + ATT_ONES_ROWS
    ctx_blk = DEC_SEQ // ATT_TQ
    k_specs = [pl.BlockSpec((1, ATT_TQ, BRANCH_W), lambda b, s, j=j: (b, _na_window_block(s) + j, 0))
               for j in range(NA_SLABS)]
    k_specs.append(pl.BlockSpec((1, ATT_TQ, BRANCH_W), lambda b, s: (b, ctx_blk, 0)))
    vt_specs = [pl.BlockSpec((1, NA_HEADS, vrows, ATT_TQ), lambda b, s, j=j: (b, 0, 0, _na_window_block(s) + j))
                for j in range(NA_SLABS)]
    vt_specs.append(pl.BlockSpec((1, NA_HEADS, vrows, ATT_TQ), lambda b, s: (b, 0, 0, ctx_blk)))
    n = NA_SLABS + 1
    return pl.pallas_call(
        _na_kernel,
        grid=(DEC_BATCH, NA_STEPS),
        in_specs=[pl.BlockSpec((1, BRANCH_W, ATT_TQ), lambda b, s: (b, 0, s))] + k_specs + vt_specs + [
            pl.BlockSpec((None, 1, NA_HEADS, NA_UNION * GRID_W, ATT_TQ), lambda b, s: (l, _na_variant(s), 0, 0, 0))],
        out_specs=pl.BlockSpec((ATT_TQ, BRANCH_W), lambda b, s: (b * NA_STEPS + s, 0)),
        out_shape=jax.ShapeDtypeStruct((DEC_BATCH * DEC_SEQ, BRANCH_W), BF16),
        scratch_shapes=[pltpu.VMEM((BRANCH_W, ATT_TQ), F32)],
        compiler_params=_cparams("arbitrary", "arbitrary"),
        name="nbr_attention",
    )(qt, *([k] * n), *([vt] * n), bias)


DA_TQ = ATT_TQ
DA_KEYS = ATT_KEYS
DA_ONES_ROWS = ATT_ONES_ROWS
DA_MAPS_PER_TILE = LANES // DA_HEAD_DIM


def _da_kernel(qt_ref, k_ref, vt_ref, lam_ref, sub_ref, o_ref, acc_ref, kn_ref, *, lam_init):
    lam = _da_lambda(lam_ref, lam_init)
    heads = DA_MAPS_PER_TILE // 2
    w = 2 * DA_TQ
    r = lax.broadcasted_iota(jnp.int32, (LANES, LANES), 0) // DA_HEAD_DIM
    c = lax.broadcasted_iota(jnp.int32, (LANES, LANES), 1) // DA_HEAD_DIM
    same_map = (r == c).astype(F32)

    @pl.when(pl.program_id(1) == 0)
    def _():
        for g in range(BRANCH_W // LANES):
            lanes = slice(g * LANES, (g + 1) * LANES)
            kf = k_ref[0, :, lanes].astype(F32)
            n2 = _bdot(kf * kf, same_map)
            kn_ref[:, lanes] = jnp.max(n2, axis=0, keepdims=True)

    jrow = lax.broadcasted_iota(jnp.int32, (8, LANES), 0)
    dmap = lax.broadcasted_iota(jnp.int32, (8, LANES), 1) // DA_HEAD_DIM
    for g in range(BRANCH_W // LANES):
        lanes = slice(g * LANES, (g + 1) * LANES)
        qg = qt_ref[0, lanes, :]
        qbd = _masked_q_blocks(qg, DA_HEAD_DIM)
        qf = qg.astype(F32)
        b2 = _bdot(jnp.where(jrow == dmap, kn_ref[:, lanes], 0.0), qf * qf)
        bound = jnp.concatenate([jnp.sqrt(b2[j:j + 1]) for j in range(DA_MAPS_PER_TILE)], axis=1) * 1.01 + 1e-3

        def attend(carry, g=g, lanes=lanes, qbd=qbd):
            it, shift, _ = carry
            st = jnp.dot(k_ref[0, :, lanes], qbd, preferred_element_type=F32)
            pt = jnp.exp2(st - shift).astype(BF16)
            low = jnp.float32(jnp.inf)
            for hh in range(heads):
                h = g * heads + hh
                oe = jnp.dot(vt_ref[0, h], pt[:, hh * w:(hh + 1) * w], preferred_element_type=F32)
                den = oe[DA_V_DIM:DA_V_DIM + 1]
                low = jnp.minimum(low, jnp.min(den))
                os = [oe[0:DA_V_DIM, i * DA_TQ:(i + 1) * DA_TQ] / den[:, i * DA_TQ:(i + 1) * DA_TQ] for i in range(2)]
                ot = os[0] - lam * os[1]
                ot = ot * lax.rsqrt(jnp.mean(ot * ot, axis=0, keepdims=True) + EPS) * sub_ref[...]
                acc_ref[h * DA_V_DIM:(h + 1) * DA_V_DIM, :] = ot * (1.0 - lam_init)
            return it + 1, _colmax(st), low

        def again(carry):
            it, _, low = carry
            return (it == 0) | ((it == 1) & jnp.logical_not(low >= ATT_MIN_DENOM))

        lax.while_loop(again, attend, (jnp.int32(0), bound, jnp.float32(0.0)))
    o_ref[...] = acc_ref[...].T.astype(o_ref.dtype)


def _diff_attention(qt, k, vt, da_lambda, subln_col, lam_init):
    nt = DEC_SEQ // DA_TQ
    vrows = DA_V_DIM + DA_ONES_ROWS
    return pl.pallas_call(
        functools.partial(_da_kernel, lam_init=lam_init),
        grid=(DEC_BATCH, nt),
        in_specs=[
            pl.BlockSpec((1, BRANCH_W, DA_TQ), lambda b, t: (b, 0, t)),
            pl.BlockSpec((1, DA_KEYS, BRANCH_W), lambda b, t: (b, 0, 0)),
            pl.BlockSpec((1, DA_HEADS, vrows, DA_KEYS), lambda b, t: (b, 0, 0, 0)),
            pl.BlockSpec((4, DA_HEAD_DIM), lambda b, t: (0, 0)),
            pl.BlockSpec((DA_V_DIM, 1), lambda b, t: (0, 0)),
        ],
        out_specs=pl.BlockSpec((DA_TQ, BRANCH_W), lambda b, t: (b * nt + t, 0)),
        out_shape=jax.ShapeDtypeStruct((DEC_BATCH * DEC_SEQ, BRANCH_W), BF16),
        scratch_shapes=[pltpu.VMEM((BRANCH_W, DA_TQ), F32), pltpu.VMEM((1, BRANCH_W), F32)],
        compiler_params=_cparams("arbitrary", "arbitrary"),
        name="diff_attention",
    )(qt, k, vt, da_lambda, subln_col)


def _rope_tables():
    pos = np.arange(DEC_SEQ)
    row = (pos // GRID_W).astype(np.float32)
    col = (pos % GRID_W).astype(np.float32)
    n_freq = DA_HEAD_DIM // 4
    inv = (np.float32(ROPE_BASE) ** (-np.arange(n_freq, dtype=np.float32) / n_freq)).astype(np.float32)
    ang = np.concatenate([row[:, None] * inv[None, :], col[:, None] * inv[None, :]], axis=-1)
    ang = ang.astype(np.float64)
    cos = np.repeat(np.cos(ang), 2, axis=-1)
    sin = np.repeat(np.sin(ang), 2, axis=-1)
    sign = np.where(np.arange(DA_HEAD_DIM) % 2 == 0, -1.0, 1.0)
    reps = LANES // DA_HEAD_DIM
    cos = np.tile(cos, (1, reps)).astype(np.float32)
    sin = np.tile(sin * sign[None, :], (1, reps)).astype(np.float32)
    return jnp.asarray(cos), jnp.asarray(sin)


def _filt_hidden_kernel(feat_ref, w1_ref, b1_ref, w2_ref, b2_ref, fr_ref, o_ref):
    half = feat_ref.shape[0]
    fr = fr_ref[0]
    h = jnp.sin(fr * (jnp.dot(feat_ref[...], w1_ref[0], precision=HIGHEST, preferred_element_type=F32) + b1_ref[0]))
    h = jnp.sin(fr * (jnp.dot(h, w2_ref[0], precision=HIGHEST, preferred_element_type=F32) + b2_ref[0]))
    o_ref[0, 0:half] = h
    blk = MXU_DIM
    r = lax.broadcasted_iota(jnp.int32, (blk, blk), 0)
    c = lax.broadcasted_iota(jnp.int32, (blk, blk), 1)
    exchange = (r + c == blk - 1).astype(F32)
    nb = half // blk
    rev = jnp.concatenate(
        [jnp.dot(exchange, h[(nb - 1 - b) * blk:(nb - b) * blk], precision=HIGHEST, preferred_element_type=F32)
         for b in range(nb)], axis=0)
    o_ref[0, half:2 * half] = pltpu.roll(rev, 1, axis=0)


def _filt_kernel(h_ref, w3f_ref, w3b_ref, dec_ref, o_ref):
    L = dec_ref.shape[0] // 2
    hf = jnp.dot(h_ref[0, 0:L], w3f_ref[0], precision=HIGHEST, preferred_element_type=F32) * dec_ref[0:L]
    hb = jnp.dot(h_ref[0, L:2 * L], w3b_ref[0], precision=HIGHEST, preferred_element_type=F32) * dec_ref[L:2 * L]
    row = lax.broadcasted_iota(jnp.int32, hb.shape, 0)
    hb = jnp.where(row == 0, 0.0, hb)
    nrm = jnp.sum(jnp.abs(hf), axis=0, keepdims=True) + jnp.sum(jnp.abs(hb), axis=0, keepdims=True)
    o_ref[0, 0, 0:L] = hf / nrm
    o_ref[0, 0, L:2 * L] = hb / nrm


def _circular_order(a):
    return np.concatenate([a, a[:1], a[1:][::-1]], axis=0)


def _hyena_pos_tables(L):
    f32 = np.float32
    pos = np.arange(L, dtype=f32)
    t = (pos / f32(L)).astype(f32)
    bands = np.linspace(1e-4, HY_POS_BANDS - 1, HY_POS_BANDS, dtype=f32)
    ang = (f32(2 * math.pi / L) * pos[:, None] * bands[None, :]).astype(np.float64)
    feats = np.zeros((L, HY_FILT_HIDDEN), f32)
    feats[:, 0] = t
    feats[:, 1:1 + HY_POS_BANDS] = np.cos(ang)
    feats[:, 1 + HY_POS_BANDS:HY_POS_DIM] = -np.sin(ang)
    deltas = np.linspace(math.log(HY_DECAY_TARGET) / HY_SLOW_DECAY,
                         math.log(HY_DECAY_TARGET) / HY_FAST_DECAY, BRANCH_W, dtype=f32)
    decay = np.exp((-t[:, None] * np.abs(deltas)[None, :]).astype(np.float64)).astype(f32)
    return jnp.asarray(feats), jnp.asarray(_circular_order(decay))


def _hyena_filters(half, w1p, b1, w2, b2, w3, freq):
    feats, decay = _hyena_pos_tables(half)
    L = 2 * half
    cb = LANES
    ncb = BRANCH_W // cb
    small = lambda shape: pl.BlockSpec((1,) + shape, lambda l: (l, 0, 0))
    hidden = pl.pallas_call(
        _filt_hidden_kernel,
        grid=(DEPTH,),
        in_specs=[
            pl.BlockSpec((half, HY_FILT_HIDDEN), lambda l: (0, 0)),
            small((HY_FILT_HIDDEN, HY_FILT_HIDDEN)), small((1, HY_FILT_HIDDEN)),
            small((HY_FILT_HIDDEN, HY_FILT_HIDDEN)), small((1, HY_FILT_HIDDEN)),
            small((1, HY_FILT_HIDDEN)),
        ],
        out_specs=pl.BlockSpec((1, L, HY_FILT_HIDDEN), lambda l: (l, 0, 0)),
        out_shape=jax.ShapeDtypeStruct((DEPTH, L, HY_FILT_HIDDEN), F32),
        compiler_params=_cparams("arbitrary"),
        name=f"hyena_filter_hidden_{L}",
    )(feats, w1p, b1, w2, b2, freq)
    return pl.pallas_call(
        _filt_kernel,
        grid=(DEPTH, 2, ncb),
        in_specs=[
            pl.BlockSpec((1, L, HY_FILT_HIDDEN), lambda l, o, c: (l, 0, 0)),
            pl.BlockSpec((1, HY_FILT_HIDDEN, cb), lambda l, o, c: (l, 0, o * 2 * ncb + c)),
            pl.BlockSpec((1, HY_FILT_HIDDEN, cb), lambda l, o, c: (l, 0, o * 2 * ncb + ncb + c)),
            pl.BlockSpec((L, cb), lambda l, o, c: (0, c)),
        ],
        out_specs=pl.BlockSpec((1, 1, L, cb), lambda l, o, c: (l, o, 0, c)),
        out_shape=jax.ShapeDtypeStruct((DEPTH, 2, L, BRANCH_W), F32),
        compiler_params=_cparams("arbitrary", "arbitrary", "arbitrary"),
        name=f"hyena_filters_{L}",
    )(hidden, w3, w3, decay)


def _short_conv(u, w_ref, b_ref, seq_len):
    n = u.shape[0]
    t = lax.broadcasted_iota(jnp.int32, u.shape, 0) % seq_len
    prev = jnp.where(t == 0, 0.0, pltpu.roll(u, 1, axis=0))
    nxt = jnp.where(t == seq_len - 1, 0.0, pltpu.roll(u, n - 1, axis=0))
    return prev * w_ref[0:1, :] + u * w_ref[1:2, :] + nxt * w_ref[2:3, :] + b_ref[...]


def _dft_direct_mats():
    n, half = 2 * SEQ, SEQ
    k = np.arange(n)[:, None].astype(np.float64)
    t = np.arange(half)[None, :].astype(np.float64)
    ang = 2 * np.pi * k * t / n
    fr, fi = np.cos(ang), -np.sin(ang)
    mf = np.block([[fr, -fi], [fi, fr]])
    gr, gi = np.cos(ang).T / n, np.sin(ang).T / n
    mi = np.block([[gr, -gi], [gi, gr]])
    return mf.astype(np.float32), mi.astype(np.float32)


def _dft_real_mat():
    n = 2 * SEQ
    ang = 2 * np.pi * np.arange(n)[:, None].astype(np.float64) * np.arange(n)[None, :] / n
    return np.concatenate([np.cos(ang), -np.sin(ang)], axis=0).astype(np.float32)


def _spec_direct_kernel(h_ref, m_ref, o_ref):
    o_ref[0, 0] = jnp.dot(m_ref[...], h_ref[0, 0], precision=HIGHEST, preferred_element_type=F32)


def _spec_direct(h, m_real):
    n = 2 * SEQ
    return pl.pallas_call(
        _spec_direct_kernel,
        grid=(DEPTH, 2),
        in_specs=[pl.BlockSpec((1, 1, n, BRANCH_W), lambda l, o: (l, o, 0, 0)),
                  pl.BlockSpec((2 * n, n), lambda l, o: (0, 0))],
        out_specs=pl.BlockSpec((1, 1, 2 * n, BRANCH_W), lambda l, o: (l, o, 0, 0)),
        out_shape=jax.ShapeDtypeStruct((DEPTH, 2, 2 * n, BRANCH_W), F32),
        compiler_params=_cparams("arbitrary", "arbitrary"),
        name="hyena_spectrum_direct",
    )(h, m_real)


def _hyena_direct_kernel(u_ref, cw_ref, cb_ref, h_ref, bias_ref, mf_ref, mi_ref, o_ref):
    n = 2 * SEQ
    c = _short_conv(u_ref[...].astype(F32), cw_ref, cb_ref, SEQ)
    z = c[:, 0:BRANCH_W]
    for order in range(2):
        gate = c[:, (order + 1) * BRANCH_W:(order + 2) * BRANCH_W]
        f = jnp.dot(mf_ref[...], z.astype(BF16), preferred_element_type=F32)
        fr, fi = f[0:n], f[n:2 * n]
        hr, hi = h_ref[order, 0:n], h_ref[order, n:2 * n]
        y = jnp.concatenate([fr * hr - fi * hi, fr * hi + fi * hr], axis=0)
        y = jnp.dot(mi_ref[...], y.astype(BF16), preferred_element_type=F32)
        z = gate * (y + z * bias_ref[order:order + 1, :])
    o_ref[...] = z.astype(o_ref.dtype)


def _hyena_direct(u, conv_w, conv_b, spec, l, bias, mf, mi):
    n = 2 * SEQ
    rows = 2 * SEQ
    T = u.shape[0]
    return pl.pallas_call(
        _hyena_direct_kernel,
        grid=(T // rows,),
        in_specs=[
            pl.BlockSpec((rows, 3 * BRANCH_W), lambda p: (p, 0)),
            pl.BlockSpec((3, 3 * BRANCH_W), lambda p: (0, 0)),
            pl.BlockSpec((1, 3 * BRANCH_W), lambda p: (0, 0)),
            pl.BlockSpec((None, 2, 2 * n, BRANCH_W), lambda p: (l, 0, 0, 0)),
            pl.BlockSpec((2, BRANCH_W), lambda p: (0, 0)),
            pl.BlockSpec((2 * n, rows), lambda p: (0, 0)),
            pl.BlockSpec((rows, 2 * n), lambda p: (0, 0)),
        ],
        out_specs=pl.BlockSpec((rows, BRANCH_W), lambda p: (p, 0)),
        out_shape=jax.ShapeDtypeStruct((T, BRANCH_W), BF16),
        compiler_params=_cparams("arbitrary"),
        name="hyena_direct",
    )(u, conv_w, conv_b, spec, bias, mf, mi)


def _dft_two_stage_mats():
    no, ni, half, n = FFT_NO, FFT_NI, FFT_HALF, FFT_N
    f64 = np.float64
    k1 = np.arange(no, dtype=f64)
    n_o = np.arange(half, dtype=f64)
    n_i = np.arange(ni, dtype=f64)
    ang = 2 * np.pi * (n_i[:, None, None] * k1[None, :, None] / n + k1[None, :, None] * n_o[None, None, :] / no)
    tr, ti = np.cos(ang), -np.sin(ang)
    m1 = np.concatenate([np.concatenate([tr, -ti], axis=2), np.concatenate([ti, tr], axis=2)], axis=1)
    k2 = np.arange(ni, dtype=f64)
    ang2 = 2 * np.pi * k2[:, None] * n_i[None, :] / ni
    f2r, f2i = np.cos(ang2), -np.sin(ang2)
    m2 = np.block([[f2r, -f2i], [f2i, f2r]])
    m2c = np.block([[f2r, f2i], [-f2i, f2r]])
    sr, si = np.transpose(tr, (0, 2, 1)) / n, -np.transpose(ti, (0, 2, 1)) / n
    m3 = np.concatenate([np.concatenate([sr, -si], axis=2), np.concatenate([si, sr], axis=2)], axis=1)
    return (m1.astype(np.float32), m2.astype(np.float32), m2c.astype(np.float32), m3.astype(np.float32))


def _dft_stage1_real_mat():
    no, ni, n = FFT_NO, FFT_NI, FFT_N
    k1 = np.arange(no, dtype=np.float64)
    n_o = np.arange(no, dtype=np.float64)
    n_i = np.arange(ni, dtype=np.float64)
    ang = 2 * np.pi * (n_i[:, None, None] * k1[None, :, None] / n + k1[None, :, None] * n_o[None, None, :] / no)
    return np.concatenate([np.cos(ang), -np.sin(ang)], axis=1).astype(np.float32)


def _store_stage1(w_ref, ni, out):
    w_ref[pl.ds(ni, FFT_NO, stride=FFT_W_PITCH), :] = out[0:FFT_NO]
    w_ref[pl.ds(FFT_NI + ni, FFT_NO, stride=FFT_W_PITCH), :] = out[FFT_NO:2 * FFT_NO]


def _fwd_stage1(za_ref, zb_ref, m1_ref, w_ref):
    def body(ni, carry):
        a = za_ref[pl.ds(ni, FFT_HALF, stride=FFT_NI), :]
        b = zb_ref[pl.ds(ni, FFT_HALF, stride=FFT_NI), :]
        out = jnp.dot(m1_ref[ni], jnp.concatenate([a, b], axis=0).astype(BF16), preferred_element_type=F32)
        _store_stage1(w_ref, ni, out)
        return carry

    lax.fori_loop(0, FFT_NI, body, 0, unroll=FFT_UNROLL)


def _spec_two_stage_kernel(h_ref, m1_ref, m2_ref, o_ref, w_ref):
    h = h_ref.at[0, 0]

    def stage1(ni, carry):
        a = h[pl.ds(ni, FFT_NO, stride=FFT_NI), :]
        _store_stage1(w_ref, ni, jnp.dot(m1_ref[ni], a.astype(BF16), preferred_element_type=F32))
        return carry

    lax.fori_loop(0, FFT_NI, stage1, 0, unroll=FFT_UNROLL)
    blk = 2 * FFT_NI

    cb = w_ref.shape[1]

    def stage2(kp, carry):
        rows = [pl.ds(pl.multiple_of((2 * kp + j) * blk, blk), blk) for j in range(2)]
        wrows = [pl.ds(pl.multiple_of((2 * kp + j) * FFT_W_PITCH, 8), blk) for j in range(2)]
        x = jnp.dot(m2_ref[...], jnp.concatenate([w_ref[r, :] for r in wrows], axis=1).astype(BF16),
                    preferred_element_type=F32)
        for j in range(2):
            o_ref[0, 0, rows[j], :] = x[:, j * cb:(j + 1) * cb]
        return carry

    lax.fori_loop(0, FFT_NO // 2, stage2, 0, unroll=FFT_MID_UNROLL)


def _spec_two_stage(h, m1_real, m2):
    cb = LCONV_CB
    return pl.pallas_call(
        _spec_two_stage_kernel,
        grid=(DEPTH, 2, BRANCH_W // cb),
        in_specs=[pl.BlockSpec((1, 1, FFT_N, cb), lambda l, o, c: (l, o, 0, c)),
                  pl.BlockSpec((FFT_NI, 2 * FFT_NO, FFT_NO), lambda l, o, c: (0, 0, 0)),
                  pl.BlockSpec((2 * FFT_NI, 2 * FFT_NI), lambda l, o, c: (0, 0))],
        out_specs=pl.BlockSpec((1, 1, 2 * FFT_N, cb), lambda l, o, c: (l, o, 0, c)),
        out_shape=jax.ShapeDtypeStruct((DEPTH, 2, 2 * FFT_N, BRANCH_W), F32),
        scratch_shapes=[pltpu.VMEM((FFT_NO * FFT_W_PITCH, cb), F32)],
        compiler_params=_cparams("arbitrary", "arbitrary", "arbitrary"),
        name="hyena_spectrum_two_stage",
    )(h, m1_real, m2)


def _lconv_two_stage_kernel(s_ref, g_ref, cws_ref, cbs_ref, cwg_ref, cbg_ref, h_ref, bias_ref,
                            m1_ref, m2_ref, m2c_ref, m3_ref, o_ref, z_ref, w_ref, *, conv_sig):
    for b in range(2):
        sig = s_ref[b].astype(F32)
        if conv_sig:
            sig = _short_conv(sig, cws_ref, cbs_ref, DEC_SEQ)
        z_ref[b] = sig
    _fwd_stage1(z_ref.at[0], z_ref.at[1], m1_ref, w_ref)
    blk = 2 * FFT_NI

    cb = w_ref.shape[1]

    def mid(kp, carry):
        rows = [pl.ds(pl.multiple_of((2 * kp + j) * blk, blk), blk) for j in range(2)]
        wrows = [pl.ds(pl.multiple_of((2 * kp + j) * FFT_W_PITCH, 8), blk) for j in range(2)]
        x = jnp.dot(m2_ref[...], jnp.concatenate([w_ref[r, :] for r in wrows], axis=1).astype(BF16),
                    preferred_element_type=F32)
        h = jnp.concatenate([h_ref[r, :] for r in rows], axis=1)
        xr, xi = x[0:FFT_NI], x[FFT_NI:blk]
        hr, hi = h[0:FFT_NI], h[FFT_NI:blk]
        y = jnp.concatenate([xr * hr - xi * hi, xr * hi + xi * hr], axis=0)
        c = jnp.dot(m2c_ref[...], y.astype(BF16), preferred_element_type=F32)
        for j in range(2):
            w_ref[wrows[j], :] = c[:, j * cb:(j + 1) * cb]
        return carry

    lax.fori_loop(0, FFT_NO // 2, mid, 0, unroll=FFT_MID_UNROLL)

    def last(ni, carry):
        cr = w_ref[pl.ds(ni, FFT_NO, stride=FFT_W_PITCH), :]
        ci = w_ref[pl.ds(FFT_NI + ni, FFT_NO, stride=FFT_W_PITCH), :]
        y = jnp.dot(m3_ref[ni], jnp.concatenate([cr, ci], axis=0).astype(BF16), preferred_element_type=F32)
        o_ref[0, pl.ds(ni, FFT_HALF, stride=FFT_NI), :] = y[0:FFT_HALF]
        o_ref[1, pl.ds(ni, FFT_HALF, stride=FFT_NI), :] = y[FFT_HALF:2 * FFT_HALF]
        return carry

    lax.fori_loop(0, FFT_NI, last, 0, unroll=FFT_UNROLL)
    for b in range(2):
        gate = _short_conv(g_ref[b].astype(F32), cwg_ref, cbg_ref, DEC_SEQ)
        sig = z_ref[b]
        o_ref[b] = gate * (o_ref[b] + sig * bias_ref[...])


def _lconv_two_stage(sig, sig_col, gate_src, gate_col, conv_w, conv_b, spec, l, order, bias, mats, conv_sig):
    cb = LCONV_CB
    ncb = BRANCH_W // cb
    m1, m2, m2c, m3 = mats
    const3 = lambda c, p: (0, 0, 0)
    const2 = lambda c, p: (0, 0)
    return pl.pallas_call(
        functools.partial(_lconv_two_stage_kernel, conv_sig=conv_sig),
        grid=(ncb, DEC_BATCH // 2),
        in_specs=[
            pl.BlockSpec((2, DEC_SEQ, cb), lambda c, p: (p, 0, sig_col * ncb + c)),
            pl.BlockSpec((2, DEC_SEQ, cb), lambda c, p: (p, 0, gate_col * ncb + c)),
            pl.BlockSpec((3, cb), lambda c, p: (0, c)),
            pl.BlockSpec((1, cb), lambda c, p: (0, c)),
            pl.BlockSpec((3, cb), lambda c, p: (0, gate_col * ncb + c)),
            pl.BlockSpec((1, cb), lambda c, p: (0, gate_col * ncb + c)),
            pl.BlockSpec((None, None, 2 * FFT_N, cb), lambda c, p: (l, order, 0, c)),
            pl.BlockSpec((1, cb), lambda c, p: (0, c)),
            pl.BlockSpec(m1.shape, const3),
            pl.BlockSpec(m2.shape, const2),
            pl.BlockSpec(m2c.shape, const2),
            pl.BlockSpec(m3.shape, const3),
        ],
        out_specs=pl.BlockSpec((2, DEC_SEQ, cb), lambda c, p: (p, 0, c)),
        out_shape=jax.ShapeDtypeStruct((DEC_BATCH, DEC_SEQ, BRANCH_W), F32),
        scratch_shapes=[pltpu.VMEM((2, DEC_SEQ, cb), F32), pltpu.VMEM((FFT_NO * FFT_W_PITCH, cb), F32)],
        compiler_params=_cparams("arbitrary", "arbitrary"),
        name="hyena_lconv_two_stage",
    )(sig, gate_src, conv_w, conv_b, conv_w, conv_b, spec, bias, m1, m2, m2c, m3)


def kernel(x_prompt, x_sample, cache_na_k, cache_na_v, cache_da_k, cache_da_v, c, c_ctx, w_ada, b_ada, norm_mix,
           norm_ffn, w_in, hy_conv_w, hy_conv_b, hy_filt_w1, hy_filt_b1, hy_filt_w2, hy_filt_b2, hy_filt_w3,
           hy_filt_freq, hy_bias, na_rpb, da_lambda, da_subln, w_lift, w_out, w_ffn_in, w_ffn_out, norm_final):
    TP, TS = BATCH * SEQ, DEC_BATCH * DEC_SEQ
    xp = x_prompt.reshape(TP, D_MODEL)
    xs = x_sample.reshape(TS, D_MODEL)

    cc = jnp.concatenate([c_ctx[None, :], c, jnp.zeros((8 - 1 - DEC_BATCH, D_MODEL), F32)], axis=0)
    mod = _modulation(cc, w_ada, b_ada)
    mod_p = mod[:, 0:1].reshape(DEPTH, 1, 1, 6 * D_MODEL)
    mod_s = mod[:, 1:1 + DEC_BATCH].reshape(DEPTH, DEC_BATCH, 1, 6 * D_MODEL)

    w_mix = w_in[:, :, :MIX_W].astype(BF16)
    w_gate = w_in[:, :, MIX_W:].astype(BF16)
    w_lift_b = w_lift.astype(BF16)
    w_out_b = w_out.astype(BF16)
    w_ffn_in_b = w_ffn_in.astype(BF16)
    w_ffn_out_b = w_ffn_out.astype(BF16)
    g_mix = norm_mix.reshape(DEPTH, 1, D_MODEL)
    g_ffn = norm_ffn.reshape(DEPTH, 1, D_MODEL)
    g_fin = norm_final.reshape(1, D_MODEL)
    subln = da_subln.reshape(DEPTH, 1, DA_V_DIM)
    subln_col = da_subln.reshape(DEPTH, DA_V_DIM, 1)

    w1p = jnp.pad(hy_filt_w1, ((0, 0), (0, HY_FILT_HIDDEN - HY_POS_DIM), (0, 0)))
    b1 = hy_filt_b1.reshape(DEPTH, 1, HY_FILT_HIDDEN)
    b2 = hy_filt_b2.reshape(DEPTH, 1, HY_FILT_HIDDEN)
    fr = hy_filt_freq.reshape(DEPTH, 1, HY_FILT_HIDDEN)
    mf, mi = _dft_direct_mats()
    mats = _dft_two_stage_mats()
    h_p = _hyena_filters(SEQ, w1p, b1, hy_filt_w2, b2, hy_filt_w3, fr)
    h_s = _hyena_filters(DEC_SEQ, w1p, b1, hy_filt_w2, b2, hy_filt_w3, fr)
    spec_p = _spec_direct(h_p, jnp.asarray(_dft_real_mat()))
    mf_b, mi_b = jnp.asarray(mf, dtype=BF16), jnp.asarray(mi, dtype=BF16)
    mats_b = tuple(jnp.asarray(m, dtype=BF16) for m in mats)
    spec_s = _spec_two_stage(h_s, jnp.asarray(_dft_stage1_real_mat(), dtype=BF16), mats_b[1])
    conv_b = hy_conv_b.reshape(DEPTH, 1, 3 * BRANCH_W)

    na_bias = _na_bias_table(na_rpb)
    rope_tables = _rope_tables()
    ck_na = cache_na_k.reshape(DEC_BATCH, DEPTH, PAST_LEN, BRANCH_W)
    cv_na = cache_na_v.reshape(DEC_BATCH, DEPTH, PAST_LEN, BRANCH_W)
    ck_da = cache_da_k.reshape(DEC_BATCH, DEPTH, PAST_LEN, BRANCH_W)
    cv_da = cache_da_v.reshape(DEC_BATCH, DEPTH, PAST_LEN, BRANCH_W)

    caches = tuple(jnp.zeros((BATCH, DEPTH, SEQ, BRANCH_W), F32) for _ in CACHE_BLOCKS)
    for l in range(DEPTH):
        lam_init = 0.8 - 0.6 * math.exp(-0.3 * l)
        final = l == DEPTH - 1

        u, caches = _in_proj(xp, g_mix[l], mod_p[l], w_mix, l, TP, BF16, caches=caches)
        y_hy = _hyena_direct(u, hy_conv_w[l], conv_b[l], spec_p, l, hy_bias[l], mf_b, mi_b)
        y_na, y_da = _ctx_attention(u, da_lambda[l], subln_col[l], lam_init)
        xp = _merge_out(xp, g_mix[l], mod_p[l], y_hy, y_na, y_da, w_gate, w_lift_b, w_out_b, l, TP)
        xp = _ffn(xp, g_ffn[l], mod_p[l], w_ffn_in_b, w_ffn_out_b, g_fin, l, TP, final)

        u = _in_proj(xs, g_mix[l], mod_s[l], w_mix, l, DEC_SEQ, BF16)
        u3 = u.reshape(DEC_BATCH, DEC_SEQ, MIX_W)
        z1 = _lconv_two_stage(u3, 0, u3, 1, hy_conv_w[l], conv_b[l], spec_s, l, 0, hy_bias[l, 0:1], mats_b, True)
        y_hy = _lconv_two_stage(z1, 0, u3, 2, hy_conv_w[l], conv_b[l], spec_s, l, 1, hy_bias[l, 1:2], mats_b, False)
        y_hy = y_hy.reshape(TS, BRANCH_W)
        qn, kn, vn = _attn_prep(u, 3, ck_na[:, l], cv_na[:, l], NA_HEAD_DIM)
        y_na = _nbr_attention(qn, kn, vn, na_bias, l)
        q, kt, v = _attn_prep(u, 6, ck_da[:, l], cv_da[:, l], DA_HEAD_DIM, rope_tables)
        y_da = _diff_attention(q, kt, v, da_lambda[l], subln_col[l], lam_init)
        xs = _merge_out(xs, g_mix[l], mod_s[l], y_hy, y_na, y_da, w_gate, w_lift_b, w_out_b, l, DEC_SEQ)
        xs = _ffn(xs, g_ffn[l], mod_s[l], w_ffn_in_b, w_ffn_out_b, g_fin, l, DEC_SEQ, final)

    y_prompt = xp.reshape(BATCH, SEQ, D_MODEL)
    y_sample = xs.reshape(DEC_BATCH, DEC_SEQ, D_MODEL)
    heads = lambda a, d: a.reshape(BATCH, DEPTH, SEQ, BRANCH_W // d, d)
    return (y_prompt, y_sample, heads(caches[0], NA_HEAD_DIM), heads(caches[1], NA_HEAD_DIM),
            heads(caches[2], 2 * DA_HEAD_DIM), heads(caches[3], DA_V_DIM))
```

```python
import functools
import math

import numpy as np
import jax
import jax.numpy as jnp
from jax import lax
from jax.experimental import pallas as pl
from jax.experimental.pallas import tpu as pltpu

F32 = jnp.float32
BF16 = jnp.bfloat16
HIGHEST = lax.Precision.HIGHEST

D_MODEL = 1024
BATCH = 32
SEQ = 256
DEPTH = 4
DEC_BATCH = 4
DEC_SEQ = 4096
PAST_LEN = 256
GRID_W = 64
GRID_H = DEC_SEQ // GRID_W
BRANCH_W = 512
HY_POS_BANDS = 16
HY_POS_DIM = 1 + 2 * HY_POS_BANDS
HY_FILT_HIDDEN = 64
HY_DECAY_TARGET = 1e-2
HY_FAST_DECAY = 0.3
HY_SLOW_DECAY = 1.5
NA_HEADS = 8
NA_HEAD_DIM = 64
NA_WIN_ROWS = 8
NA_WIN_COLS = 16
DA_HEADS = 8
DA_HEAD_DIM = 32
DA_V_DIM = 64
D_FF = 2816
MIX_W = 9 * BRANCH_W
ROPE_BASE = 10000.0
EPS = 1e-6
NEG_INF = -1e30

VMEM_LIMIT_BYTES = 56 * 1024 * 1024
LANES = 128
MXU_DIM = 256

FFT_N = 2 * DEC_SEQ
FFT_NO = 64
FFT_NI = 128
FFT_HALF = FFT_NO // 2
FFT_UNROLL = 8
FFT_MID_UNROLL = 16
FFT_W_PITCH = 2 * FFT_NI + 8
FFT_Z_PITCH = FFT_NI + 8
LCONV_CB = LANES


def _cparams(*sem):
    return pltpu.CompilerParams(dimension_semantics=sem, vmem_limit_bytes=VMEM_LIMIT_BYTES)


def _sigmoid(x):
    return 1.0 / (1.0 + jnp.exp(-x))


def _rms(x, g):
    return x * lax.rsqrt(jnp.mean(x * x, axis=-1, keepdims=True) + EPS) * g


def _modnorm(x, g, shift, scale):
    return _rms(x, g) * (1.0 + scale) + shift


def _bdot(a, b):
    return jnp.dot(a.astype(BF16), b.astype(BF16), preferred_element_type=F32)


def _mod_kernel(c_ref, w_ref, b_ref, o_ref):
    c = c_ref[...]
    s = c * _sigmoid(c)
    o_ref[0] = jnp.dot(s, w_ref[0], precision=HIGHEST, preferred_element_type=F32) + b_ref[0]


def _modulation(cc, w_ada, b_ada):
    nt = 6
    return pl.pallas_call(
        _mod_kernel,
        grid=(DEPTH, nt),
        in_specs=[
            pl.BlockSpec((8, D_MODEL), lambda l, j: (0, 0)),
            pl.BlockSpec((1, D_MODEL, D_MODEL), lambda l, j: (l, 0, j)),
            pl.BlockSpec((1, 1, D_MODEL), lambda l, j: (l, 0, j)),
        ],
        out_specs=pl.BlockSpec((1, 8, D_MODEL), lambda l, j: (l, 0, j)),
        out_shape=jax.ShapeDtypeStruct((DEPTH, 8, 6 * D_MODEL), F32),
        compiler_params=_cparams("arbitrary", "arbitrary"),
        name="modulation",
    )(cc, w_ada, b_ada.reshape(DEPTH, 1, 6 * D_MODEL))


IN_TM = 512
CACHE_BLOCKS = (4, 5, 7, 8)


def _in_kernel(*refs, n_cache):
    x_ref, g_ref, mod_ref, w_ref = refs[:4]
    o_ref = refs[4 + n_cache]
    cache_refs = refs[5 + n_cache:]
    m = mod_ref[0]
    h = _modnorm(x_ref[...], g_ref[...], m[:, 0:D_MODEL], m[:, D_MODEL:2 * D_MODEL]).astype(BF16)
    res = jnp.dot(h, w_ref[...], preferred_element_type=F32)
    o_ref[...] = res.astype(o_ref.dtype)
    for c, c_ref in zip(CACHE_BLOCKS, cache_refs):
        c_ref[...] = res[:, c * BRANCH_W:(c + 1) * BRANCH_W].reshape(c_ref.shape)


def _in_proj(x, g, mod, w, l, rows_per_mod, out_dtype, caches=None):
    T = x.shape[0]
    tm = IN_TM
    per = rows_per_mod // tm
    in_specs = [
        pl.BlockSpec((tm, D_MODEL), lambda i: (i, 0)),
        pl.BlockSpec((1, D_MODEL), lambda i: (0, 0)),
        pl.BlockSpec((1, 1, 6 * D_MODEL), lambda i: (i // per, 0, 0)),
        pl.BlockSpec((None, D_MODEL, MIX_W), lambda i: (l, 0, 0), pipeline_mode=pl.Buffered(1)),
    ]
    out_specs = [pl.BlockSpec((tm, MIX_W), lambda i: (i, 0))]
    out_shape = [jax.ShapeDtypeStruct((T, MIX_W), out_dtype)]
    args = [x, g, mod, w]
    aliases = {}
    if caches is not None:
        out_specs += [pl.BlockSpec((tm // SEQ, 1, SEQ, BRANCH_W), lambda i: (i, l, 0, 0))] * len(caches)
        out_shape += [jax.ShapeDtypeStruct(c.shape, c.dtype) for c in caches]
        in_specs += [pl.BlockSpec(memory_space=pl.ANY)] * len(caches)
        aliases = {4 + n: 1 + n for n in range(len(caches))}
        args += list(caches)
    outs = pl.pallas_call(
        functools.partial(_in_kernel, n_cache=len(args) - 4),
        grid=(T // tm,),
        in_specs=in_specs,
        out_specs=out_specs,
        out_shape=out_shape,
        input_output_aliases=aliases,
        compiler_params=_cparams("arbitrary"),
        name="in_proj",
    )(*args)
    return outs[0] if caches is None else (outs[0], tuple(outs[1:]))


def _mid_kernel(x_ref, g_ref, mod_ref, yh_ref, yn_ref, yd_ref, wg_ref, wl_ref, wo_ref, o_ref):
    m = mod_ref[0]
    x = x_ref[...]
    h = _modnorm(x, g_ref[...], m[:, 0:D_MODEL], m[:, D_MODEL:2 * D_MODEL]).astype(BF16)
    merged = None
    for br, y_ref in enumerate((yh_ref, yn_ref, yd_ref)):
        gate = _sigmoid(jnp.dot(h, wg_ref[:, br * D_MODEL:(br + 1) * D_MODEL], preferred_element_type=F32))
        lift = jnp.dot(y_ref[...].astype(BF16), wl_ref[br], preferred_element_type=F32)
        t = gate * lift
        merged = t if merged is None else merged + t
    o_ref[...] = x + m[:, 2 * D_MODEL:3 * D_MODEL] * _bdot(merged, wo_ref[...])


def _merge_out(x, g, mod, y_hy, y_na, y_da, w_gate, w_lift, w_out, l, rows_per_mod):
    T = x.shape[0]
    tm = 512
    per = rows_per_mod // tm
    row = lambda i: (i, 0)
    const2 = lambda i: (0, 0)
    return pl.pallas_call(
        _mid_kernel,
        grid=(T // tm,),
        in_specs=[
            pl.BlockSpec((tm, D_MODEL), row),
            pl.BlockSpec((1, D_MODEL), const2),
            pl.BlockSpec((1, 1, 6 * D_MODEL), lambda i: (i // per, 0, 0)),
            pl.BlockSpec((tm, BRANCH_W), row),
            pl.BlockSpec((tm, BRANCH_W), row),
            pl.BlockSpec((tm, BRANCH_W), row),
            pl.BlockSpec((None, D_MODEL, 3 * D_MODEL), lambda i: (l, 0, 0)),
            pl.BlockSpec((None, 3, BRANCH_W, D_MODEL), lambda i: (l, 0, 0, 0)),
            pl.BlockSpec((None, D_MODEL, D_MODEL), lambda i: (l, 0, 0)),
        ],
        out_specs=pl.BlockSpec((tm, D_MODEL), row),
        out_shape=jax.ShapeDtypeStruct((T, D_MODEL), F32),
        compiler_params=_cparams("arbitrary"),
        name="merge_out",
    )(x, g, mod, y_hy, y_na, y_da, w_gate, w_lift, w_out)


def _ffn_kernel(x_ref, g_ref, mod_ref, w1_ref, w2_ref, gf_ref, o_ref, *, final):
    m = mod_ref[0]
    x = x_ref[...]
    h = _modnorm(x, g_ref[...], m[:, 3 * D_MODEL:4 * D_MODEL], m[:, 4 * D_MODEL:5 * D_MODEL]).astype(BF16)
    a = jnp.dot(h, w1_ref[:, 0:D_FF], preferred_element_type=F32)
    b = jnp.dot(h, w1_ref[:, D_FF:2 * D_FF], preferred_element_type=F32)
    xn = x + m[:, 5 * D_MODEL:6 * D_MODEL] * _bdot(a * _sigmoid(a) * b, w2_ref[...])
    if final:
        xn = _rms(xn, gf_ref[...])
    o_ref[...] = xn


def _ffn(x, g, mod, w_ffn_in, w_ffn_out, g_final, l, rows_per_mod, final):
    T = x.shape[0]
    tm = 512
    per = rows_per_mod // tm
    resident = pl.Buffered(1)
    return pl.pallas_call(
        functools.partial(_ffn_kernel, final=final),
        grid=(T // tm,),
        in_specs=[
            pl.BlockSpec((tm, D_MODEL), lambda i: (i, 0)),
            pl.BlockSpec((1, D_MODEL), lambda i: (0, 0)),
            pl.BlockSpec((1, 1, 6 * D_MODEL), lambda i: (i // per, 0, 0)),
            pl.BlockSpec((None, D_MODEL, 2 * D_FF), lambda i: (l, 0, 0), pipeline_mode=resident),
            pl.BlockSpec((None, D_FF, D_MODEL), lambda i: (l, 0, 0), pipeline_mode=resident),
            pl.BlockSpec((1, D_MODEL), lambda i: (0, 0)),
        ],
        out_specs=pl.BlockSpec((tm, D_MODEL), lambda i: (i, 0)),
        out_shape=jax.ShapeDtypeStruct((T, D_MODEL), F32),
        compiler_params=_cparams("arbitrary"),
        name="ffn",
    )(x, g, mod, w_ffn_in, w_ffn_out, g_final)


def _da_lambda(lam_ref, lam_init):
    lp = lam_ref[...]
    a = jnp.sum(lp[0:1] * lp[1:2], axis=1, keepdims=True)
    b = jnp.sum(lp[2:3] * lp[3:4], axis=1, keepdims=True)
    return jnp.exp(a) - jnp.exp(b) + lam_init


ATT_ONES_ROWS = 16
ATT_TQ = 256
ATT_PREP_T = 512
ATT_KEYS = DEC_SEQ + PAST_LEN
ATT_MIN_DENOM = 2.0 ** -64
LOG2E = math.log2(math.e)


def _masked_q_blocks(qt, d):
    row = lax.broadcasted_iota(jnp.int32, qt.shape, 0)
    zero = jnp.zeros_like(qt)
    return jnp.concatenate([jnp.where((row >= j * d) & (row < (j + 1) * d), qt, zero) for j in range(LANES // d)],
                           axis=1)


def _colmax(st):
    keys, n = st.shape
    return jnp.max(jnp.max(st.reshape(keys // MXU_DIM, MXU_DIM, n), axis=0), axis=0, keepdims=True)


def _ctx_attn_kernel(nq_ref, nk_ref, nv_ref, dq_ref, dk_ref, dv_ref, lam_ref, sub_ref, yn_ref, yd_ref, acc_ref,
                     *, lam_init):
    lam = _da_lambda(lam_ref, lam_init)
    ones = jnp.ones((ATT_ONES_ROWS, SEQ), BF16)

    def attend(q_ref, k_ref, v_ref, d, maps_per_head, finish):
        qt = (q_ref[...].astype(F32) * (d ** -0.5 * LOG2E)).T.astype(BF16)
        vt = v_ref[...].astype(F32).T.astype(BF16)
        kb = k_ref[...].astype(BF16)
        dv = NA_HEAD_DIM
        heads_per_group = LANES // (d * maps_per_head)
        w = maps_per_head * SEQ
        for g in range(BRANCH_W // LANES):
            lanes = slice(g * LANES, (g + 1) * LANES)
            st = jnp.dot(kb[:, lanes], _masked_q_blocks(qt[lanes], d), preferred_element_type=F32)
            pt = jnp.exp2(st - _colmax(st)).astype(BF16)
            for j in range(heads_per_group):
                h = g * heads_per_group + j
                ve = jnp.concatenate([vt[h * dv:(h + 1) * dv], ones], axis=0)
                oe = jnp.dot(ve, pt[:, j * w:(j + 1) * w], preferred_element_type=F32)
                os = [oe[0:dv, i * SEQ:(i + 1) * SEQ] / oe[dv:dv + 1, i * SEQ:(i + 1) * SEQ]
                      for i in range(maps_per_head)]
                acc_ref[h * dv:(h + 1) * dv, :] = finish(os)

    attend(nq_ref, nk_ref, nv_ref, NA_HEAD_DIM, 1, lambda os: os[0])
    yn_ref[...] = acc_ref[...].T.astype(yn_ref.dtype)

    def da_finish(os):
        ot = os[0] - lam * os[1]
        ot = ot * lax.rsqrt(jnp.mean(ot * ot, axis=0, keepdims=True) + EPS) * sub_ref[...]
        return ot * (1.0 - lam_init)

    attend(dq_ref, dk_ref, dv_ref, DA_HEAD_DIM, 2, da_finish)
    yd_ref[...] = acc_ref[...].T.astype(yd_ref.dtype)


def _ctx_attention(u, da_lambda, subln_col, lam_init):
    col = lambda j: pl.BlockSpec((SEQ, BRANCH_W), lambda b, j=j: (b, j))
    out = pl.BlockSpec((SEQ, BRANCH_W), lambda b: (b, 0))
    shape = jax.ShapeDtypeStruct((BATCH * SEQ, BRANCH_W), BF16)
    return pl.pallas_call(
        functools.partial(_ctx_attn_kernel, lam_init=lam_init),
        grid=(BATCH,),
        in_specs=[col(3), col(4), col(5), col(6), col(7), col(8),
                  pl.BlockSpec((4, DA_HEAD_DIM), lambda b: (0, 0)),
                  pl.BlockSpec((DA_V_DIM, 1), lambda b: (0, 0))],
        out_specs=[out, out],
        out_shape=[shape, shape],
        scratch_shapes=[pltpu.VMEM((BRANCH_W, SEQ), F32)],
        compiler_params=_cparams("arbitrary"),
        name="ctx_attention",
    )(u, u, u, u, u, u, da_lambda, subln_col)


def _rope(x, cos, sin_signed):
    n = x.shape[-1]
    lane = lax.broadcasted_iota(jnp.int32, x.shape, 1)
    partner = jnp.where(lane % 2 == 0, pltpu.roll(x, n - 1, axis=1), pltpu.roll(x, 1, axis=1))
    return x * cos + partner * sin_signed


def _attn_prep_kernel(q_ref, k_ref, v_ref, kc_ref, vc_ref, *refs, rope, scale):
    cos_ref, sin_ref = refs[:2] if rope else (None, None)
    qt_ref, ko_ref, vt_ref = refs[-3:]
    t = pl.program_id(1)
    dv = NA_HEAD_DIM

    def put_v(v):
        n = v.shape[0]
        vt = v.astype(F32).T.astype(BF16)
        ones = jnp.ones((ATT_ONES_ROWS, n), BF16)
        for h in range(BRANCH_W // dv):
            vt_ref[0, h, 0:dv, 0:n] = vt[h * dv:(h + 1) * dv]
            vt_ref[0, h, dv:dv + ATT_ONES_ROWS, 0:n] = ones

    @pl.when(t < DEC_SEQ // ATT_PREP_T)
    def _():
        q = q_ref[...].astype(F32)
        k = k_ref[...].astype(F32)
        if rope:
            cos = jnp.concatenate([cos_ref[...]] * (BRANCH_W // LANES), axis=1)
            sin = jnp.concatenate([sin_ref[...]] * (BRANCH_W // LANES), axis=1)
            q = _rope(q, cos, sin)
            k = _rope(k, cos, sin)
        qt_ref[0] = (q * scale).T.astype(BF16)
        ko_ref[0] = k.astype(BF16)
        put_v(v_ref[...])

    @pl.when(t == DEC_SEQ // ATT_PREP_T)
    def _():
        ko_ref[0, 0:PAST_LEN, :] = kc_ref[0].astype(BF16)
        put_v(vc_ref[0])


def _attn_prep(u, first_col, k_ctx, v_ctx, head_dim, rope_tables=None):
    rope = rope_tables is not None
    tile = ATT_PREP_T
    nt = DEC_SEQ // tile
    last = nt - 1
    rowblk = lambda j: pl.BlockSpec((tile, BRANCH_W), lambda b, t, j=j: (b * nt + jnp.minimum(t, last), j))
    tab = pl.BlockSpec((tile, LANES), lambda b, t: (jnp.minimum(t, last), 0))
    ctx = pl.BlockSpec((1, PAST_LEN, BRANCH_W), lambda b, t: (b, 0, 0))
    heads = BRANCH_W // NA_HEAD_DIM
    vrows = NA_HEAD_DIM + ATT_ONES_ROWS
    return pl.pallas_call(
        functools.partial(_attn_prep_kernel, rope=rope, scale=head_dim ** -0.5 * LOG2E),
        grid=(DEC_BATCH, nt + 1),
        in_specs=[rowblk(first_col), rowblk(first_col + 1), rowblk(first_col + 2), ctx, ctx] + [tab, tab] * rope,
        out_specs=[
            pl.BlockSpec((1, BRANCH_W, tile), lambda b, t: (b, 0, jnp.minimum(t, last))),
            pl.BlockSpec((1, tile, BRANCH_W), lambda b, t: (b, t, 0)),
            pl.BlockSpec((1, heads, vrows, tile), lambda b, t: (b, 0, 0, t)),
        ],
        out_shape=[
            jax.ShapeDtypeStruct((DEC_BATCH, BRANCH_W, DEC_SEQ), BF16),
            jax.ShapeDtypeStruct((DEC_BATCH, ATT_KEYS, BRANCH_W), BF16),
            jax.ShapeDtypeStruct((DEC_BATCH, heads, vrows, ATT_KEYS), BF16),
        ],
        compiler_params=_cparams("arbitrary", "arbitrary"),
        name="attn_prep",
    )(u, u, u, k_ctx, v_ctx, *(rope_tables or ()))


NA_ROWS = ATT_TQ // GRID_W
NA_UNION = 3 * NA_ROWS
NA_STEPS = GRID_H // NA_ROWS
NA_SLABS = NA_UNION // NA_ROWS
NA_VARIANT_OFFSET = (0, -NA_ROWS, -2 * NA_ROWS)


def _na_variant(s):
    return jnp.minimum(s, 1) + s // (NA_STEPS - 1)


def _na_window_block(s):
    return jnp.clip(s - 1, 0, NA_STEPS - NA_SLABS)


def _na_bias_kernel(rpb_ref, o_ref):
    kc = lax.broadcasted_iota(jnp.int32, (GRID_W, GRID_W), 0)
    qc = lax.broadcasted_iota(jnp.int32, (GRID_W, GRID_W), 1)
    dc = jnp.clip(kc - qc, -(NA_WIN_COLS - 1), NA_WIN_COLS - 1) + (NA_WIN_COLS - 1)
    c0 = jnp.clip(qc - NA_WIN_COLS // 2, 0, GRID_W - NA_WIN_COLS)
    col_ok = (kc >= c0) & (kc < c0 + NA_WIN_COLS)
    r = rpb_ref[0, 0] * LOG2E
    masked = jnp.full((GRID_W, GRID_W), NEG_INF, F32)
    tiles = []
    for dr in range(2 * NA_WIN_ROWS - 1):
        acc = jnp.zeros((GRID_W, GRID_W), F32)
        for d in range(2 * NA_WIN_COLS - 1):
            acc = jnp.where(dc == d, r[dr:dr + 1, d:d + 1], acc)
        tiles.append(jnp.where(col_ok, acc, masked))
    for v, off in enumerate(NA_VARIANT_OFFSET):
        for kr in range(NA_UNION):
            for rr in range(NA_ROWS):
                w0 = (0, rr, NA_UNION - NA_WIN_ROWS)[v]
                dr = kr + off - rr
                inside = w0 <= kr < w0 + NA_WIN_ROWS
                o_ref[0, v, 0, kr * GRID_W:(kr + 1) * GRID_W, rr * GRID_W:(rr + 1) * GRID_W] = (
                    tiles[dr + NA_WIN_ROWS - 1] if inside else masked)


def _na_bias_table(na_rpb):
    n_dr, n_dc = 2 * NA_WIN_ROWS - 1, 2 * NA_WIN_COLS - 1
    nv = len(NA_VARIANT_OFFSET)
    return pl.pallas_call(
        _na_bias_kernel,
        grid=(DEPTH, NA_HEADS),
        in_specs=[pl.BlockSpec((1, 1, n_dr, n_dc), lambda l, h: (l, h, 0, 0))],
        out_specs=pl.BlockSpec((1, nv, 1, NA_UNION * GRID_W, ATT_TQ), lambda l, h: (l, 0, h, 0, 0)),
        out_shape=jax.ShapeDtypeStruct((DEPTH, nv, NA_HEADS, NA_UNION * GRID_W, ATT_TQ), F32),
        compiler_params=_cparams("arbitrary", "arbitrary"),
        name="na_bias_table",
    )(na_rpb)


def _na_kernel(qt_ref, *refs):
    n = NA_SLABS + 1
    k_refs, vt_refs = refs[:n], refs[n:2 * n]
    bias_ref, o_ref, acc_ref = refs[2 * n:]
    dv = NA_HEAD_DIM
    heads_per_group = LANES // dv
    for g in range(BRANCH_W // LANES):
        lanes = slice(g * LANES, (g + 1) * LANES)
        qbd = _masked_q_blocks(qt_ref[0, lanes, :], dv)
        keys = jnp.concatenate([k_ref[0, :, lanes] for k_ref in k_refs], axis=0)
        st = jnp.dot(keys, qbd, preferred_element_type=F32)
        n_win = NA_SLABS * ATT_TQ
        st_win = st[0:n_win] + jnp.concatenate(
            [bias_ref[0, g * heads_per_group + hh] for hh in range(heads_per_group)], axis=1)
        st_ctx = st[n_win:]
        mx = jnp.maximum(_colmax(st_win), _colmax(st_ctx))
        pt = jnp.concatenate([jnp.exp2(st_win - mx), jnp.exp2(st_ctx - mx)], axis=0).astype(BF16)
        for hh in range(heads_per_group):
            h = g * heads_per_group + hh
            ve = jnp.concatenate([vt_ref[0, h] for vt_ref in vt_refs], axis=1)
            oe = jnp.dot(ve, pt[:, hh * ATT_TQ:(hh + 1) * ATT_TQ], preferred_element_type=F32)
            acc_ref[h * dv:(h + 1) * dv, :] = oe[0:dv] / oe[dv:dv + 1]
    o_ref[...] = acc_ref[...].T.astype(o_ref.dtype)


def _nbr_attention(qt, k, vt, bias, l):
    vrows = NA_HEAD_DIM + ATT_ONES_ROWS
    ctx_blk = DEC_SEQ // ATT_TQ
    k_specs = [pl.BlockSpec((1, ATT_TQ, BRANCH_W), lambda b, s, j=j: (b, _na_window_block(s) + j, 0))
               for j in range(NA_SLABS)]
    k_specs.append(pl.BlockSpec((1, ATT_TQ, BRANCH_W), lambda b, s: (b, ctx_blk, 0)))
    vt_specs = [pl.BlockSpec((1, NA_HEADS, vrows, ATT_TQ), lambda b, s, j=j: (b, 0, 0, _na_window_block(s) + j))
                for j in range(NA_SLABS)]
    vt_specs.append(pl.BlockSpec((1, NA_HEADS, vrows, ATT_TQ), lambda b, s: (b, 0, 0, ctx_blk)))
    n = NA_SLABS + 1
    return pl.pallas_call(
        _na_kernel,
        grid=(DEC_BATCH, NA_STEPS),
        in_specs=[pl.BlockSpec((1, BRANCH_W, ATT_TQ), lambda b, s: (b, 0, s))] + k_specs + vt_specs + [
            pl.BlockSpec((None, 1, NA_HEADS, NA_UNION * GRID_W, ATT_TQ), lambda b, s: (l, _na_variant(s), 0, 0, 0))],
        out_specs=pl.BlockSpec((ATT_TQ, BRANCH_W), lambda b, s: (b * NA_STEPS + s, 0)),
        out_shape=jax.ShapeDtypeStruct((DEC_BATCH * DEC_SEQ, BRANCH_W), BF16),
        scratch_shapes=[pltpu.VMEM((BRANCH_W, ATT_TQ), F32)],
        compiler_params=_cparams("arbitrary", "arbitrary"),
        name="nbr_attention",
    )(qt, *([k] * n), *([vt] * n), bias)


DA_TQ = ATT_TQ
DA_KEYS = ATT_KEYS
DA_ONES_ROWS = ATT_ONES_ROWS
DA_MAPS_PER_TILE = LANES // DA_HEAD_DIM


def _da_kernel(qt_ref, k_ref, vt_ref, lam_ref, sub_ref, o_ref, acc_ref, kn_ref, *, lam_init):
    lam = _da_lambda(lam_ref, lam_init)
    heads = DA_MAPS_PER_TILE // 2
    w = 2 * DA_TQ
    r = lax.broadcasted_iota(jnp.int32, (LANES, LANES), 0) // DA_HEAD_DIM
    c = lax.broadcasted_iota(jnp.int32, (LANES, LANES), 1) // DA_HEAD_DIM
    same_map = (r == c).astype(F32)

    @pl.when(pl.program_id(1) == 0)
    def _():
        for g in range(BRANCH_W // LANES):
            lanes = slice(g * LANES, (g + 1) * LANES)
            kf = k_ref[0, :, lanes].astype(F32)
            n2 = _bdot(kf * kf, same_map)
            kn_ref[:, lanes] = jnp.max(n2, axis=0, keepdims=True)

    jrow = lax.broadcasted_iota(jnp.int32, (8, LANES), 0)
    dmap = lax.broadcasted_iota(jnp.int32, (8, LANES), 1) // DA_HEAD_DIM
    for g in range(BRANCH_W // LANES):
        lanes = slice(g * LANES, (g + 1) * LANES)
        qg = qt_ref[0, lanes, :]
        qbd = _masked_q_blocks(qg, DA_HEAD_DIM)
        qf = qg.astype(F32)
        b2 = _bdot(jnp.where(jrow == dmap, kn_ref[:, lanes], 0.0), qf * qf)
        bound = jnp.concatenate([jnp.sqrt(b2[j:j + 1]) for j in range(DA_MAPS_PER_TILE)], axis=1) * 1.01 + 1e-3

        def attend(carry, g=g, lanes=lanes, qbd=qbd):
            it, shift, _ = carry
            st = jnp.dot(k_ref[0, :, lanes], qbd, preferred_element_type=F32)
            pt = jnp.exp2(st - shift).astype(BF16)
            low = jnp.float32(jnp.inf)
            for hh in range(heads):
                h = g * heads + hh
                oe = jnp.dot(vt_ref[0, h], pt[:, hh * w:(hh + 1) * w], preferred_element_type=F32)
                den = oe[DA_V_DIM:DA_V_DIM + 1]
                low = jnp.minimum(low, jnp.min(den))
                os = [oe[0:DA_V_DIM, i * DA_TQ:(i + 1) * DA_TQ] / den[:, i * DA_TQ:(i + 1) * DA_TQ] for i in range(2)]
                ot = os[0] - lam * os[1]
                ot = ot * lax.rsqrt(jnp.mean(ot * ot, axis=0, keepdims=True) + EPS) * sub_ref[...]
                acc_ref[h * DA_V_DIM:(h + 1) * DA_V_DIM, :] = ot * (1.0 - lam_init)
            return it + 1, _colmax(st), low

        def again(carry):
            it, _, low = carry
            return (it == 0) | ((it == 1) & jnp.logical_not(low >= ATT_MIN_DENOM))

        lax.while_loop(again, attend, (jnp.int32(0), bound, jnp.float32(0.0)))
    o_ref[...] = acc_ref[...].T.astype(o_ref.dtype)


def _diff_attention(qt, k, vt, da_lambda, subln_col, lam_init):
    nt = DEC_SEQ // DA_TQ
    vrows = DA_V_DIM + DA_ONES_ROWS
    return pl.pallas_call(
        functools.partial(_da_kernel, lam_init=lam_init),
        grid=(DEC_BATCH, nt),
        in_specs=[
            pl.BlockSpec((1, BRANCH_W, DA_TQ), lambda b, t: (b, 0, t)),
            pl.BlockSpec((1, DA_KEYS, BRANCH_W), lambda b, t: (b, 0, 0)),
            pl.BlockSpec((1, DA_HEADS, vrows, DA_KEYS), lambda b, t: (b, 0, 0, 0)),
            pl.BlockSpec((4, DA_HEAD_DIM), lambda b, t: (0, 0)),
            pl.BlockSpec((DA_V_DIM, 1), lambda b, t: (0, 0)),
        ],
        out_specs=pl.BlockSpec((DA_TQ, BRANCH_W), lambda b, t: (b * nt + t, 0)),
        out_shape=jax.ShapeDtypeStruct((DEC_BATCH * DEC_SEQ, BRANCH_W), BF16),
        scratch_shapes=[pltpu.VMEM((BRANCH_W, DA_TQ), F32), pltpu.VMEM((1, BRANCH_W), F32)],
        compiler_params=_cparams("arbitrary", "arbitrary"),
        name="diff_attention",
    )(qt, k, vt, da_lambda, subln_col)


def _rope_tables():
    pos = np.arange(DEC_SEQ)
    row = (pos // GRID_W).astype(np.float32)
    col = (pos % GRID_W).astype(np.float32)
    n_freq = DA_HEAD_DIM // 4
    inv = (np.float32(ROPE_BASE) ** (-np.arange(n_freq, dtype=np.float32) / n_freq)).astype(np.float32)
    ang = np.concatenate([row[:, None] * inv[None, :], col[:, None] * inv[None, :]], axis=-1)
    ang = ang.astype(np.float64)
    cos = np.repeat(np.cos(ang), 2, axis=-1)
    sin = np.repeat(np.sin(ang), 2, axis=-1)
    sign = np.where(np.arange(DA_HEAD_DIM) % 2 == 0, -1.0, 1.0)
    reps = LANES // DA_HEAD_DIM
    cos = np.tile(cos, (1, reps)).astype(np.float32)
    sin = np.tile(sin * sign[None, :], (1, reps)).astype(np.float32)
    return jnp.asarray(cos), jnp.asarray(sin)


def _filt_hidden_kernel(feat_ref, w1_ref, b1_ref, w2_ref, b2_ref, fr_ref, o_ref):
    half = feat_ref.shape[0]
    fr = fr_ref[0]
    h = jnp.sin(fr * (jnp.dot(feat_ref[...], w1_ref[0], precision=HIGHEST, preferred_element_type=F32) + b1_ref[0]))
    h = jnp.sin(fr * (jnp.dot(h, w2_ref[0], precision=HIGHEST, preferred_element_type=F32) + b2_ref[0]))
    o_ref[0, 0:half] = h
    blk = MXU_DIM
    r = lax.broadcasted_iota(jnp.int32, (blk, blk), 0)
    c = lax.broadcasted_iota(jnp.int32, (blk, blk), 1)
    exchange = (r + c == blk - 1).astype(F32)
    nb = half // blk
    rev = jnp.concatenate(
        [jnp.dot(exchange, h[(nb - 1 - b) * blk:(nb - b) * blk], precision=HIGHEST, preferred_element_type=F32)
         for b in range(nb)], axis=0)
    o_ref[0, half:2 * half] = pltpu.roll(rev, 1, axis=0)


def _filt_kernel(h_ref, w3f_ref, w3b_ref, dec_ref, o_ref):
    L = dec_ref.shape[0] // 2
    hf = jnp.dot(h_ref[0, 0:L], w3f_ref[0], precision=HIGHEST, preferred_element_type=F32) * dec_ref[0:L]
    hb = jnp.dot(h_ref[0, L:2 * L], w3b_ref[0], precision=HIGHEST, preferred_element_type=F32) * dec_ref[L:2 * L]
    row = lax.broadcasted_iota(jnp.int32, hb.shape, 0)
    hb = jnp.where(row == 0, 0.0, hb)
    nrm = jnp.sum(jnp.abs(hf), axis=0, keepdims=True) + jnp.sum(jnp.abs(hb), axis=0, keepdims=True)
    o_ref[0, 0, 0:L] = hf / nrm
    o_ref[0, 0, L:2 * L] = hb / nrm


def _circular_order(a):
    return np.concatenate([a, a[:1], a[1:][::-1]], axis=0)


def _hyena_pos_tables(L):
    f32 = np.float32
    pos = np.arange(L, dtype=f32)
    t = (pos / f32(L)).astype(f32)
    bands = np.linspace(1e-4, HY_POS_BANDS - 1, HY_POS_BANDS, dtype=f32)
    ang = (f32(2 * math.pi / L) * pos[:, None] * bands[None, :]).astype(np.float64)
    feats = np.zeros((L, HY_FILT_HIDDEN), f32)
    feats[:, 0] = t
    feats[:, 1:1 + HY_POS_BANDS] = np.cos(ang)
    feats[:, 1 + HY_POS_BANDS:HY_POS_DIM] = -np.sin(ang)
    deltas = np.linspace(math.log(HY_DECAY_TARGET) / HY_SLOW_DECAY,
                         math.log(HY_DECAY_TARGET) / HY_FAST_DECAY, BRANCH_W, dtype=f32)
    decay = np.exp((-t[:, None] * np.abs(deltas)[None, :]).astype(np.float64)).astype(f32)
    return jnp.asarray(feats), jnp.asarray(_circular_order(decay))


def _hyena_filters(half, w1p, b1, w2, b2, w3, freq):
    feats, decay = _hyena_pos_tables(half)
    L = 2 * half
    cb = LANES
    ncb = BRANCH_W // cb
    small = lambda shape: pl.BlockSpec((1,) + shape, lambda l: (l, 0, 0))
    hidden = pl.pallas_call(
        _filt_hidden_kernel,
        grid=(DEPTH,),
        in_specs=[
            pl.BlockSpec((half, HY_FILT_HIDDEN), lambda l: (0, 0)),
            small((HY_FILT_HIDDEN, HY_FILT_HIDDEN)), small((1, HY_FILT_HIDDEN)),
            small((HY_FILT_HIDDEN, HY_FILT_HIDDEN)), small((1, HY_FILT_HIDDEN)),
            small((1, HY_FILT_HIDDEN)),
        ],
        out_specs=pl.BlockSpec((1, L, HY_FILT_HIDDEN), lambda l: (l, 0, 0)),
        out_shape=jax.ShapeDtypeStruct((DEPTH, L, HY_FILT_HIDDEN), F32),
        compiler_params=_cparams("arbitrary"),
        name=f"hyena_filter_hidden_{L}",
    )(feats, w1p, b1, w2, b2, freq)
    return pl.pallas_call(
        _filt_kernel,
        grid=(DEPTH, 2, ncb),
        in_specs=[
            pl.BlockSpec((1, L, HY_FILT_HIDDEN), lambda l, o, c: (l, 0, 0)),
            pl.BlockSpec((1, HY_FILT_HIDDEN, cb), lambda l, o, c: (l, 0, o * 2 * ncb + c)),
            pl.BlockSpec((1, HY_FILT_HIDDEN, cb), lambda l, o, c: (l, 0, o * 2 * ncb + ncb + c)),
            pl.BlockSpec((L, cb), lambda l, o, c: (0, c)),
        ],
        out_specs=pl.BlockSpec((1, 1, L, cb), lambda l, o, c: (l, o, 0, c)),
        out_shape=jax.ShapeDtypeStruct((DEPTH, 2, L, BRANCH_W), F32),
        compiler_params=_cparams("arbitrary", "arbitrary", "arbitrary"),
        name=f"hyena_filters_{L}",
    )(hidden, w3, w3, decay)


def _short_conv(u, w_ref, b_ref, seq_len):
    n = u.shape[0]
    t = lax.broadcasted_iota(jnp.int32, u.shape, 0) % seq_len
    prev = jnp.where(t == 0, 0.0, pltpu.roll(u, 1, axis=0))
    nxt = jnp.where(t == seq_len - 1, 0.0, pltpu.roll(u, n - 1, axis=0))
    return prev * w_ref[0:1, :] + u * w_ref[1:2, :] + nxt * w_ref[2:3, :] + b_ref[...]


def _dft_direct_mats():
    n, half = 2 * SEQ, SEQ
    k = np.arange(n)[:, None].astype(np.float64)
    t = np.arange(half)[None, :].astype(np.float64)
    ang = 2 * np.pi * k * t / n
    fr, fi = np.cos(ang), -np.sin(ang)
    mf = np.block([[fr, -fi], [fi, fr]])
    gr, gi = np.cos(ang).T / n, np.sin(ang).T / n
    mi = np.block([[gr, -gi], [gi, gr]])
    return mf.astype(np.float32), mi.astype(np.float32)


def _dft_real_mat():
    n = 2 * SEQ
    ang = 2 * np.pi * np.arange(n)[:, None].astype(np.float64) * np.arange(n)[None, :] / n
    return np.concatenate([np.cos(ang), -np.sin(ang)], axis=0).astype(np.float32)


def _spec_direct_kernel(h_ref, m_ref, o_ref):
    o_ref[0, 0] = jnp.dot(m_ref[...], h_ref[0, 0], precision=HIGHEST, preferred_element_type=F32)


def _spec_direct(h, m_real):
    n = 2 * SEQ
    return pl.pallas_call(
        _spec_direct_kernel,
        grid=(DEPTH, 2),
        in_specs=[pl.BlockSpec((1, 1, n, BRANCH_W), lambda l, o: (l, o, 0, 0)),
                  pl.BlockSpec((2 * n, n), lambda l, o: (0, 0))],
        out_specs=pl.BlockSpec((1, 1, 2 * n, BRANCH_W), lambda l, o: (l, o, 0, 0)),
        out_shape=jax.ShapeDtypeStruct((DEPTH, 2, 2 * n, BRANCH_W), F32),
        compiler_params=_cparams("arbitrary", "arbitrary"),
        name="hyena_spectrum_direct",
    )(h, m_real)


def _hyena_direct_kernel(u_ref, cw_ref, cb_ref, h_ref, bias_ref, mf_ref, mi_ref, o_ref):
    n = 2 * SEQ
    c = _short_conv(u_ref[...].astype(F32), cw_ref, cb_ref, SEQ)
    z = c[:, 0:BRANCH_W]
    for order in range(2):
        gate = c[:, (order + 1) * BRANCH_W:(order + 2) * BRANCH_W]
        f = jnp.dot(mf_ref[...], z.astype(BF16), preferred_element_type=F32)
        fr, fi = f[0:n], f[n:2 * n]
        hr, hi = h_ref[order, 0:n], h_ref[order, n:2 * n]
        y = jnp.concatenate([fr * hr - fi * hi, fr * hi + fi * hr], axis=0)
        y = jnp.dot(mi_ref[...], y.astype(BF16), preferred_element_type=F32)
        z = gate * (y + z * bias_ref[order:order + 1, :])
    o_ref[...] = z.astype(o_ref.dtype)


def _hyena_direct(u, conv_w, conv_b, spec, l, bias, mf, mi):
    n = 2 * SEQ
    rows = 2 * SEQ
    T = u.shape[0]
    return pl.pallas_call(
        _hyena_direct_kernel,
        grid=(T // rows,),
        in_specs=[
            pl.BlockSpec((rows, 3 * BRANCH_W), lambda p: (p, 0)),
            pl.BlockSpec((3, 3 * BRANCH_W), lambda p: (0, 0)),
            pl.BlockSpec((1, 3 * BRANCH_W), lambda p: (0, 0)),
            pl.BlockSpec((None, 2, 2 * n, BRANCH_W), lambda p: (l, 0, 0, 0)),
            pl.BlockSpec((2, BRANCH_W), lambda p: (0, 0)),
            pl.BlockSpec((2 * n, rows), lambda p: (0, 0)),
            pl.BlockSpec((rows, 2 * n), lambda p: (0, 0)),
        ],
        out_specs=pl.BlockSpec((rows, BRANCH_W), lambda p: (p, 0)),
        out_shape=jax.ShapeDtypeStruct((T, BRANCH_W), BF16),
        compiler_params=_cparams("arbitrary"),
        name="hyena_direct",
    )(u, conv_w, conv_b, spec, bias, mf, mi)


def _dft_two_stage_mats():
    no, ni, half, n = FFT_NO, FFT_NI, FFT_HALF, FFT_N
    f64 = np.float64
    k1 = np.arange(no, dtype=f64)
    n_o = np.arange(half, dtype=f64)
    n_i = np.arange(ni, dtype=f64)
    ang = 2 * np.pi * (n_i[:, None, None] * k1[None, :, None] / n + k1[None, :, None] * n_o[None, None, :] / no)
    tr, ti = np.cos(ang), -np.sin(ang)
    m1 = np.concatenate([np.concatenate([tr, -ti], axis=2), np.concatenate([ti, tr], axis=2)], axis=1)
    k2 = np.arange(ni, dtype=f64)
    ang2 = 2 * np.pi * k2[:, None] * n_i[None, :] / ni
    f2r, f2i = np.cos(ang2), -np.sin(ang2)
    m2 = np.block([[f2r, -f2i], [f2i, f2r]])
    m2c = np.block([[f2r, f2i], [-f2i, f2r]])
    sr, si = np.transpose(tr, (0, 2, 1)) / n, -np.transpose(ti, (0, 2, 1)) / n
    m3 = np.concatenate([np.concatenate([sr, -si], axis=2), np.concatenate([si, sr], axis=2)], axis=1)
    return (m1.astype(np.float32), m2.astype(np.float32), m2c.astype(np.float32), m3.astype(np.float32))


def _dft_stage1_real_mat():
    no, ni, n = FFT_NO, FFT_NI, FFT_N
    k1 = np.arange(no, dtype=np.float64)
    n_o = np.arange(no, dtype=np.float64)
    n_i = np.arange(ni, dtype=np.float64)
    ang = 2 * np.pi * (n_i[:, None, None] * k1[None, :, None] / n + k1[None, :, None] * n_o[None, None, :] / no)
    return np.concatenate([np.cos(ang), -np.sin(ang)], axis=1).astype(np.float32)


def _store_stage1(w_ref, ni, out):
    w_ref[pl.ds(ni, FFT_NO, stride=FFT_W_PITCH), :] = out[0:FFT_NO]
    w_ref[pl.ds(FFT_NI + ni, FFT_NO, stride=FFT_W_PITCH), :] = out[FFT_NO:2 * FFT_NO]


def _fwd_stage1(za_ref, zb_ref, m1_ref, w_ref):
    def body(ni, carry):
        a = za_ref[pl.ds(ni, FFT_HALF, stride=FFT_Z_PITCH), :]
        b = zb_ref[pl.ds(ni, FFT_HALF, stride=FFT_Z_PITCH), :]
        out = jnp.dot(m1_ref[ni], jnp.concatenate([a, b], axis=0).astype(BF16), preferred_element_type=F32)
        _store_stage1(w_ref, ni, out)
        return carry

    lax.fori_loop(0, FFT_NI, body, 0, unroll=FFT_UNROLL)


def _spec_two_stage_kernel(h_ref, m1_ref, m2_ref, o_ref, w_ref, hp_ref):
    for no in range(FFT_NO):
        hp_ref[no * FFT_Z_PITCH:no * FFT_Z_PITCH + FFT_NI, :] = h_ref[0, 0, no * FFT_NI:(no + 1) * FFT_NI, :]

    def stage1(ni, carry):
        a = hp_ref[pl.ds(ni, FFT_NO, stride=FFT_Z_PITCH), :]
        _store_stage1(w_ref, ni, jnp.dot(m1_ref[ni], a.astype(BF16), preferred_element_type=F32))
        return carry

    lax.fori_loop(0, FFT_NI, stage1, 0, unroll=FFT_UNROLL)
    blk = 2 * FFT_NI

    cb = w_ref.shape[1]

    def stage2(kp, carry):
        rows = [pl.ds(pl.multiple_of((2 * kp + j) * blk, blk), blk) for j in range(2)]
        wrows = [pl.ds(pl.multiple_of((2 * kp + j) * FFT_W_PITCH, 8), blk) for j in range(2)]
        x = jnp.dot(m2_ref[...], jnp.concatenate([w_ref[r, :] for r in wrows], axis=1).astype(BF16),
                    preferred_element_type=F32)
        for j in range(2):
            o_ref[0, 0, rows[j], :] = x[:, j * cb:(j + 1) * cb]
        return carry

    lax.fori_loop(0, FFT_NO // 2, stage2, 0, unroll=FFT_MID_UNROLL)


def _spec_two_stage(h, m1_real, m2):
    cb = LCONV_CB
    return pl.pallas_call(
        _spec_two_stage_kernel,
        grid=(DEPTH, 2, BRANCH_W // cb),
        in_specs=[pl.BlockSpec((1, 1, FFT_N, cb), lambda l, o, c: (l, o, 0, c)),
                  pl.BlockSpec((FFT_NI, 2 * FFT_NO, FFT_NO), lambda l, o, c: (0, 0, 0)),
                  pl.BlockSpec((2 * FFT_NI, 2 * FFT_NI), lambda l, o, c: (0, 0))],
        out_specs=pl.BlockSpec((1, 1, 2 * FFT_N, cb), lambda l, o, c: (l, o, 0, c)),
        out_shape=jax.ShapeDtypeStruct((DEPTH, 2, 2 * FFT_N, BRANCH_W), F32),
        scratch_shapes=[pltpu.VMEM((FFT_NO * FFT_W_PITCH, cb), F32), pltpu.VMEM((FFT_NO * FFT_Z_PITCH, cb), F32)],
        compiler_params=_cparams("arbitrary", "arbitrary", "arbitrary"),
        name="hyena_spectrum_two_stage",
    )(h, m1_real, m2)


def _lconv_two_stage_kernel(s_ref, g_ref, cws_ref, cbs_ref, cwg_ref, cbg_ref, h_ref, bias_ref,
                            m1_ref, m2_ref, m2c_ref, m3_ref, o_ref, z_ref, w_ref, y_ref, *, conv_sig):
    def chunk(no):
        return slice(no * FFT_NI, (no + 1) * FFT_NI), slice(no * FFT_Z_PITCH, no * FFT_Z_PITCH + FFT_NI)

    for b in range(2):
        sig = s_ref[b].astype(F32)
        if conv_sig:
            sig = _short_conv(sig, cws_ref, cbs_ref, DEC_SEQ)
        for no in range(FFT_HALF):
            rows, prow = chunk(no)
            z_ref[b, prow, :] = sig[rows]
    _fwd_stage1(z_ref.at[0], z_ref.at[1], m1_ref, w_ref)
    blk = 2 * FFT_NI

    cb = w_ref.shape[1]

    def mid(kp, carry):
        rows = [pl.ds(pl.multiple_of((2 * kp + j) * blk, blk), blk) for j in range(2)]
        wrows = [pl.ds(pl.multiple_of((2 * kp + j) * FFT_W_PITCH, 8), blk) for j in range(2)]
        x = jnp.dot(m2_ref[...], jnp.concatenate([w_ref[r, :] for r in wrows], axis=1).astype(BF16),
                    preferred_element_type=F32)
        h = jnp.concatenate([h_ref[r, :] for r in rows], axis=1)
        xr, xi = x[0:FFT_NI], x[FFT_NI:blk]
        hr, hi = h[0:FFT_NI], h[FFT_NI:blk]
        y = jnp.concatenate([xr * hr - xi * hi, xr * hi + xi * hr], axis=0)
        c = jnp.dot(m2c_ref[...], y.astype(BF16), preferred_element_type=F32)
        for j in range(2):
            w_ref[wrows[j], :] = c[:, j * cb:(j + 1) * cb]
        return carry

    lax.fori_loop(0, FFT_NO // 2, mid, 0, unroll=FFT_MID_UNROLL)

    def last(ni, carry):
        cr = w_ref[pl.ds(ni, FFT_NO, stride=FFT_W_PITCH), :]
        ci = w_ref[pl.ds(FFT_NI + ni, FFT_NO, stride=FFT_W_PITCH), :]
        y = jnp.dot(m3_ref[ni], jnp.concatenate([cr, ci], axis=0).astype(BF16), preferred_element_type=F32)
        y_ref[0, pl.ds(ni, FFT_HALF, stride=FFT_Z_PITCH), :] = y[0:FFT_HALF]
        y_ref[1, pl.ds(ni, FFT_HALF, stride=FFT_Z_PITCH), :] = y[FFT_HALF:2 * FFT_HALF]
        return carry

    lax.fori_loop(0, FFT_NI, last, 0, unroll=FFT_UNROLL)
    for b in range(2):
        gate = _short_conv(g_ref[b].astype(F32), cwg_ref, cbg_ref, DEC_SEQ)
        for no in range(FFT_HALF):
            rows, prow = chunk(no)
            o_ref[b, rows, :] = (gate[rows] * (y_ref[b, prow, :] + z_ref[b, prow, :] * bias_ref[...])).astype(o_ref.dtype)


def _lconv_two_stage(sig, sig_col, gate_src, gate_col, conv_w, conv_b, spec, l, order, bias, mats, conv_sig):
    cb = LCONV_CB
    ncb = BRANCH_W // cb
    m1, m2, m2c, m3 = mats
    const3 = lambda c, p: (0, 0, 0)
    const2 = lambda c, p: (0, 0)
    return pl.pallas_call(
        functools.partial(_lconv_two_stage_kernel, conv_sig=conv_sig),
        grid=(ncb, DEC_BATCH // 2),
        in_specs=[
            pl.BlockSpec((2, DEC_SEQ, cb), lambda c, p: (p, 0, sig_col * ncb + c)),
            pl.BlockSpec((2, DEC_SEQ, cb), lambda c, p: (p, 0, gate_col * ncb + c)),
            pl.BlockSpec((3, cb), lambda c, p: (0, c)),
            pl.BlockSpec((1, cb), lambda c, p: (0, c)),
            pl.BlockSpec((3, cb), lambda c, p: (0, gate_col * ncb + c)),
            pl.BlockSpec((1, cb), lambda c, p: (0, gate_col * ncb + c)),
            pl.BlockSpec((None, None, 2 * FFT_N, cb), lambda c, p: (l, order, 0, c)),
            pl.BlockSpec((1, cb), lambda c, p: (0, c)),
            pl.BlockSpec(m1.shape, const3, pipeline_mode=pl.Buffered(1)),
            pl.BlockSpec(m2.shape, const2, pipeline_mode=pl.Buffered(1)),
            pl.BlockSpec(m2c.shape, const2, pipeline_mode=pl.Buffered(1)),
            pl.BlockSpec(m3.shape, const3, pipeline_mode=pl.Buffered(1)),
        ],
        out_specs=pl.BlockSpec((2, DEC_SEQ, cb), lambda c, p: (p, 0, c)),
        out_shape=jax.ShapeDtypeStruct((DEC_BATCH, DEC_SEQ, BRANCH_W), BF16),
        scratch_shapes=[pltpu.VMEM((2, FFT_HALF * FFT_Z_PITCH, cb), F32),
                        pltpu.VMEM((FFT_NO * FFT_W_PITCH, cb), F32),
                        pltpu.VMEM((2, FFT_HALF * FFT_Z_PITCH, cb), F32)],
        compiler_params=_cparams("arbitrary", "arbitrary"),
        name="hyena_lconv_two_stage",
    )(sig, gate_src, conv_w, conv_b, conv_w, conv_b, spec, bias, m1, m2, m2c, m3)


def kernel(x_prompt, x_sample, cache_na_k, cache_na_v, cache_da_k, cache_da_v, c, c_ctx, w_ada, b_ada, norm_mix,
           norm_ffn, w_in, hy_conv_w, hy_conv_b, hy_filt_w1, hy_filt_b1, hy_filt_w2, hy_filt_b2, hy_filt_w3,
           hy_filt_freq, hy_bias, na_rpb, da_lambda, da_subln, w_lift, w_out, w_ffn_in, w_ffn_out, norm_final):
    TP, TS = BATCH * SEQ, DEC_BATCH * DEC_SEQ
    xp = x_prompt.reshape(TP, D_MODEL)
    xs = x_sample.reshape(TS, D_MODEL)

    cc = jnp.concatenate([c_ctx[None, :], c, jnp.zeros((8 - 1 - DEC_BATCH, D_MODEL), F32)], axis=0)
    mod = _modulation(cc, w_ada, b_ada)
    mod_p = mod[:, 0:1].reshape(DEPTH, 1, 1, 6 * D_MODEL)
    mod_s = mod[:, 1:1 + DEC_BATCH].reshape(DEPTH, DEC_BATCH, 1, 6 * D_MODEL)

    w_mix = w_in[:, :, :MIX_W].astype(BF16)
    w_gate = w_in[:, :, MIX_W:].astype(BF16)
    w_lift_b = w_lift.astype(BF16)
    w_out_b = w_out.astype(BF16)
    w_ffn_in_b = w_ffn_in.astype(BF16)
    w_ffn_out_b = w_ffn_out.astype(BF16)
    g_mix = norm_mix.reshape(DEPTH, 1, D_MODEL)
    g_ffn = norm_ffn.reshape(DEPTH, 1, D_MODEL)
    g_fin = norm_final.reshape(1, D_MODEL)
    subln = da_subln.reshape(DEPTH, 1, DA_V_DIM)
    subln_col = da_subln.reshape(DEPTH, DA_V_DIM, 1)

    w1p = jnp.pad(hy_filt_w1, ((0, 0), (0, HY_FILT_HIDDEN - HY_POS_DIM), (0, 0)))
    b1 = hy_filt_b1.reshape(DEPTH, 1, HY_FILT_HIDDEN)
    b2 = hy_filt_b2.reshape(DEPTH, 1, HY_FILT_HIDDEN)
    fr = hy_filt_freq.reshape(DEPTH, 1, HY_FILT_HIDDEN)
    mf, mi = _dft_direct_mats()
    mats = _dft_two_stage_mats()
    h_p = _hyena_filters(SEQ, w1p, b1, hy_filt_w2, b2, hy_filt_w3, fr)
    h_s = _hyena_filters(DEC_SEQ, w1p, b1, hy_filt_w2, b2, hy_filt_w3, fr)
    spec_p = _spec_direct(h_p, jnp.asarray(_dft_real_mat()))
    mf_b, mi_b = jnp.asarray(mf, dtype=BF16), jnp.asarray(mi, dtype=BF16)
    mats_b = tuple(jnp.asarray(m, dtype=BF16) for m in mats)
    spec_s = _spec_two_stage(h_s, jnp.asarray(_dft_stage1_real_mat(), dtype=BF16), mats_b[1])
    conv_b = hy_conv_b.reshape(DEPTH, 1, 3 * BRANCH_W)

    na_bias = _na_bias_table(na_rpb)
    rope_tables = _rope_tables()
    ck_na = cache_na_k.reshape(DEC_BATCH, DEPTH, PAST_LEN, BRANCH_W)
    cv_na = cache_na_v.reshape(DEC_BATCH, DEPTH, PAST_LEN, BRANCH_W)
    ck_da = cache_da_k.reshape(DEC_BATCH, DEPTH, PAST_LEN, BRANCH_W)
    cv_da = cache_da_v.reshape(DEC_BATCH, DEPTH, PAST_LEN, BRANCH_W)

    caches = tuple(jnp.zeros((BATCH, DEPTH, SEQ, BRANCH_W), F32) for _ in CACHE_BLOCKS)
    for l in range(DEPTH):
        lam_init = 0.8 - 0.6 * math.exp(-0.3 * l)
        final = l == DEPTH - 1

        u, caches = _in_proj(xp, g_mix[l], mod_p[l], w_mix, l, TP, BF16, caches=caches)
        y_hy = _hyena_direct(u, hy_conv_w[l], conv_b[l], spec_p, l, hy_bias[l], mf_b, mi_b)
        y_na, y_da = _ctx_attention(u, da_lambda[l], subln_col[l], lam_init)
        xp = _merge_out(xp, g_mix[l], mod_p[l], y_hy, y_na, y_da, w_gate, w_lift_b, w_out_b, l, TP)
        xp = _ffn(xp, g_ffn[l], mod_p[l], w_ffn_in_b, w_ffn_out_b, g_fin, l, TP, final)

        u = _in_proj(xs, g_mix[l], mod_s[l], w_mix, l, DEC_SEQ, BF16)
        u3 = u.reshape(DEC_BATCH, DEC_SEQ, MIX_W)
        z1 = _lconv_two_stage(u3, 0, u3, 1, hy_conv_w[l], conv_b[l], spec_s, l, 0, hy_bias[l, 0:1], mats_b, True)
        y_hy = _lconv_two_stage(z1, 0, u3, 2, hy_conv_w[l], conv_b[l], spec_s, l, 1, hy_bias[l, 1:2], mats_b, False)
        y_hy = y_hy.reshape(TS, BRANCH_W)
        qn, kn, vn = _attn_prep(u, 3, ck_na[:, l], cv_na[:, l], NA_HEAD_DIM)
        y_na = _nbr_attention(qn, kn, vn, na_bias, l)
        q, kt, v = _attn_prep(u, 6, ck_da[:, l], cv_da[:, l], DA_HEAD_DIM, rope_tables)
        y_da = _diff_attention(q, kt, v, da_lambda[l], subln_col[l], lam_init)
        xs = _merge_out(xs, g_mix[l], mod_s[l], y_hy, y_na, y_da, w_gate, w_lift_b, w_out_b, l, DEC_SEQ)
        xs = _ffn(xs, g_ffn[l], mod_s[l], w_ffn_in_b, w_ffn_out_b, g_fin, l, DEC_SEQ, final)

    y_prompt = xp.reshape(BATCH, SEQ, D_MODEL)
    y_sample = xs.reshape(DEC_BATCH, DEC_SEQ, D_MODEL)
    heads = lambda a, d: a.reshape(BATCH, DEPTH, SEQ, BRANCH_W // d, d)
    return (y_prompt, y_sample, heads(caches[0], NA_HEAD_DIM), heads(caches[1], NA_HEAD_DIM),
            heads(caches[2], 2 * DA_HEAD_DIM), heads(caches[3], DA_V_DIM))
```

```python
import functools
import math

import numpy as np
import jax
import jax.numpy as jnp
from jax import lax
from jax.experimental import pallas as pl
from jax.experimental.pallas import tpu as pltpu

F32 = jnp.float32
BF16 = jnp.bfloat16
HIGHEST = lax.Precision.HIGHEST

D_MODEL = 1024
BATCH = 32
SEQ = 256
DEPTH = 4
DEC_BATCH = 4
DEC_SEQ = 4096
PAST_LEN = 256
GRID_W = 64
GRID_H = DEC_SEQ // GRID_W
BRANCH_W = 512
HY_POS_BANDS = 16
HY_POS_DIM = 1 + 2 * HY_POS_BANDS
HY_FILT_HIDDEN = 64
HY_DECAY_TARGET = 1e-2
HY_FAST_DECAY = 0.3
HY_SLOW_DECAY = 1.5
NA_HEADS = 8
NA_HEAD_DIM = 64
NA_WIN_ROWS = 8
NA_WIN_COLS = 16
DA_HEADS = 8
DA_HEAD_DIM = 32
DA_V_DIM = 64
D_FF = 2816
MIX_W = 9 * BRANCH_W
ROPE_BASE = 10000.0
EPS = 1e-6
NEG_INF = -1e30

VMEM_LIMIT_BYTES = 56 * 1024 * 1024
LANES = 128
MXU_DIM = 256

FFT_N = 2 * DEC_SEQ
FFT_NO = 64
FFT_NI = 128
FFT_HALF = FFT_NO // 2
FFT_UNROLL = 8
FFT_MID_UNROLL = 16
FFT_W_PITCH = 2 * FFT_NI + 8
FFT_Z_PITCH = FFT_NI + 8
LCONV_CB = LANES


def _cparams(*sem):
    return pltpu.CompilerParams(dimension_semantics=sem, vmem_limit_bytes=VMEM_LIMIT_BYTES)


def _sigmoid(x):
    return 1.0 / (1.0 + jnp.exp(-x))


def _rms(x, g):
    return x * lax.rsqrt(jnp.mean(x * x, axis=-1, keepdims=True) + EPS) * g


def _modnorm(x, g, shift, scale):
    return _rms(x, g) * (1.0 + scale) + shift


def _bdot(a, b):
    return jnp.dot(a.astype(BF16), b.astype(BF16), preferred_element_type=F32)


def _mod_kernel(c_ref, w_ref, b_ref, o_ref):
    c = c_ref[...]
    s = c * _sigmoid(c)
    o_ref[0] = jnp.dot(s, w_ref[0], precision=HIGHEST, preferred_element_type=F32) + b_ref[0]


def _modulation(cc, w_ada, b_ada):
    nt = 6
    return pl.pallas_call(
        _mod_kernel,
        grid=(DEPTH, nt),
        in_specs=[
            pl.BlockSpec((8, D_MODEL), lambda l, j: (0, 0)),
            pl.BlockSpec((1, D_MODEL, D_MODEL), lambda l, j: (l, 0, j)),
            pl.BlockSpec((1, 1, D_MODEL), lambda l, j: (l, 0, j)),
        ],
        out_specs=pl.BlockSpec((1, 8, D_MODEL), lambda l, j: (l, 0, j)),
        out_shape=jax.ShapeDtypeStruct((DEPTH, 8, 6 * D_MODEL), F32),
        compiler_params=_cparams("arbitrary", "arbitrary"),
        name="modulation",
    )(cc, w_ada, b_ada.reshape(DEPTH, 1, 6 * D_MODEL))


IN_TM = 512
CACHE_BLOCKS = (4, 5, 7, 8)


def _in_kernel(*refs, n_cache):
    x_ref, g_ref, mod_ref, w_ref = refs[:4]
    o_ref = refs[4 + n_cache]
    cache_refs = refs[5 + n_cache:]
    m = mod_ref[0]
    h = _modnorm(x_ref[...], g_ref[...], m[:, 0:D_MODEL], m[:, D_MODEL:2 * D_MODEL]).astype(BF16)
    res = jnp.dot(h, w_ref[...], preferred_element_type=F32)
    o_ref[...] = res.astype(o_ref.dtype)
    for c, c_ref in zip(CACHE_BLOCKS, cache_refs):
        c_ref[...] = res[:, c * BRANCH_W:(c + 1) * BRANCH_W].reshape(c_ref.shape)


def _in_proj(x, g, mod, w, l, rows_per_mod, out_dtype, caches=None):
    T = x.shape[0]
    tm = IN_TM
    per = rows_per_mod // tm
    in_specs = [
        pl.BlockSpec((tm, D_MODEL), lambda i: (i, 0)),
        pl.BlockSpec((1, D_MODEL), lambda i: (0, 0)),
        pl.BlockSpec((1, 1, 6 * D_MODEL), lambda i: (i // per, 0, 0)),
        pl.BlockSpec((None, D_MODEL, MIX_W), lambda i: (l, 0, 0), pipeline_mode=pl.Buffered(1)),
    ]
    out_specs = [pl.BlockSpec((tm, MIX_W), lambda i: (i, 0))]
    out_shape = [jax.ShapeDtypeStruct((T, MIX_W), out_dtype)]
    args = [x, g, mod, w]
    aliases = {}
    if caches is not None:
        out_specs += [pl.BlockSpec((tm // SEQ, 1, SEQ, BRANCH_W), lambda i: (i, l, 0, 0))] * len(caches)
        out_shape += [jax.ShapeDtypeStruct(c.shape, c.dtype) for c in caches]
        in_specs += [pl.BlockSpec(memory_space=pl.ANY)] * len(caches)
        aliases = {4 + n: 1 + n for n in range(len(caches))}
        args += list(caches)
    outs = pl.pallas_call(
        functools.partial(_in_kernel, n_cache=len(args) - 4),
        grid=(T // tm,),
        in_specs=in_specs,
        out_specs=out_specs,
        out_shape=out_shape,
        input_output_aliases=aliases,
        compiler_params=_cparams("arbitrary"),
        name="in_proj",
    )(*args)
    return outs[0] if caches is None else (outs[0], tuple(outs[1:]))


def _mid_kernel(x_ref, g_ref, mod_ref, yh_ref, yn_ref, yd_ref, wg_ref, wl_ref, wo_ref, o_ref):
    m = mod_ref[0]
    x = x_ref[...]
    h = _modnorm(x, g_ref[...], m[:, 0:D_MODEL], m[:, D_MODEL:2 * D_MODEL]).astype(BF16)
    merged = None
    for br, y_ref in enumerate((yh_ref, yn_ref, yd_ref)):
        gate = _sigmoid(jnp.dot(h, wg_ref[:, br * D_MODEL:(br + 1) * D_MODEL], preferred_element_type=F32))
        lift = jnp.dot(y_ref[...].astype(BF16), wl_ref[br], preferred_element_type=F32)
        t = gate * lift
        merged = t if merged is None else merged + t
    o_ref[...] = x + m[:, 2 * D_MODEL:3 * D_MODEL] * _bdot(merged, wo_ref[...])


def _merge_out(x, g, mod, y_hy, y_na, y_da, w_gate, w_lift, w_out, l, rows_per_mod):
    T = x.shape[0]
    tm = 512
    per = rows_per_mod // tm
    row = lambda i: (i, 0)
    const2 = lambda i: (0, 0)
    return pl.pallas_call(
        _mid_kernel,
        grid=(T // tm,),
        in_specs=[
            pl.BlockSpec((tm, D_MODEL), row),
            pl.BlockSpec((1, D_MODEL), const2),
            pl.BlockSpec((1, 1, 6 * D_MODEL), lambda i: (i // per, 0, 0)),
            pl.BlockSpec((tm, BRANCH_W), row),
            pl.BlockSpec((tm, BRANCH_W), row),
            pl.BlockSpec((tm, BRANCH_W), row),
            pl.BlockSpec((None, D_MODEL, 3 * D_MODEL), lambda i: (l, 0, 0)),
            pl.BlockSpec((None, 3, BRANCH_W, D_MODEL), lambda i: (l, 0, 0, 0)),
            pl.BlockSpec((None, D_MODEL, D_MODEL), lambda i: (l, 0, 0)),
        ],
        out_specs=pl.BlockSpec((tm, D_MODEL), row),
        out_shape=jax.ShapeDtypeStruct((T, D_MODEL), F32),
        compiler_params=_cparams("arbitrary"),
        name="merge_out",
    )(x, g, mod, y_hy, y_na, y_da, w_gate, w_lift, w_out)


def _ffn_kernel(x_ref, g_ref, mod_ref, w1_ref, w2_ref, gf_ref, o_ref, *, final):
    m = mod_ref[0]
    x = x_ref[...]
    h = _modnorm(x, g_ref[...], m[:, 3 * D_MODEL:4 * D_MODEL], m[:, 4 * D_MODEL:5 * D_MODEL]).astype(BF16)
    a = jnp.dot(h, w1_ref[:, 0:D_FF], preferred_element_type=F32)
    b = jnp.dot(h, w1_ref[:, D_FF:2 * D_FF], preferred_element_type=F32)
    xn = x + m[:, 5 * D_MODEL:6 * D_MODEL] * _bdot(a * _sigmoid(a) * b, w2_ref[...])
    if final:
        xn = _rms(xn, gf_ref[...])
    o_ref[...] = xn


def _ffn(x, g, mod, w_ffn_in, w_ffn_out, g_final, l, rows_per_mod, final):
    T = x.shape[0]
    tm = 512
    per = rows_per_mod // tm
    resident = pl.Buffered(1)
    return pl.pallas_call(
        functools.partial(_ffn_kernel, final=final),
        grid=(T // tm,),
        in_specs=[
            pl.BlockSpec((tm, D_MODEL), lambda i: (i, 0)),
            pl.BlockSpec((1, D_MODEL), lambda i: (0, 0)),
            pl.BlockSpec((1, 1, 6 * D_MODEL), lambda i: (i // per, 0, 0)),
            pl.BlockSpec((None, D_MODEL, 2 * D_FF), lambda i: (l, 0, 0), pipeline_mode=resident),
            pl.BlockSpec((None, D_FF, D_MODEL), lambda i: (l, 0, 0), pipeline_mode=resident),
            pl.BlockSpec((1, D_MODEL), lambda i: (0, 0)),
        ],
        out_specs=pl.BlockSpec((tm, D_MODEL), lambda i: (i, 0)),
        out_shape=jax.ShapeDtypeStruct((T, D_MODEL), F32),
        compiler_params=_cparams("arbitrary"),
        name="ffn",
    )(x, g, mod, w_ffn_in, w_ffn_out, g_final)


def _da_lambda(lam_ref, lam_init):
    lp = lam_ref[...]
    a = jnp.sum(lp[0:1] * lp[1:2], axis=1, keepdims=True)
    b = jnp.sum(lp[2:3] * lp[3:4], axis=1, keepdims=True)
    return jnp.exp(a) - jnp.exp(b) + lam_init


ATT_ONES_ROWS = 16
ATT_TQ = 256
ATT_PREP_T = 512
ATT_KEYS = DEC_SEQ + PAST_LEN
ATT_MIN_DENOM = 2.0 ** -64
LOG2E = math.log2(math.e)


def _masked_q_blocks(qt, d):
    row = lax.broadcasted_iota(jnp.int32, qt.shape, 0)
    zero = jnp.zeros_like(qt)
    return jnp.concatenate([jnp.where((row >= j * d) & (row < (j + 1) * d), qt, zero) for j in range(LANES // d)],
                           axis=1)


def _colmax(st):
    keys, n = st.shape
    return jnp.max(jnp.max(st.reshape(keys // MXU_DIM, MXU_DIM, n), axis=0), axis=0, keepdims=True)


CTX_SEQS = 2


def _ctx_attn_kernel(nq_ref, nk_ref, nv_ref, dq_ref, dk_ref, dv_ref, lam_ref, sub_ref, yn_ref, yd_ref, accs_ref,
                     *, lam_init):
    lam = _da_lambda(lam_ref, lam_init)
    ones = jnp.ones((ATT_ONES_ROWS, SEQ), BF16)

    def attend(s, q_ref, k_ref, v_ref, d, maps_per_head, finish):
        rows = pl.ds(s * SEQ, SEQ)
        acc_ref = accs_ref.at[s]
        qt = (q_ref[rows, :].astype(F32) * (d ** -0.5 * LOG2E)).T.astype(BF16)
        vt = v_ref[rows, :].astype(F32).T.astype(BF16)
        kb = k_ref[rows, :].astype(BF16)
        dv = NA_HEAD_DIM
        heads_per_group = LANES // (d * maps_per_head)
        w = maps_per_head * SEQ
        for g in range(BRANCH_W // LANES):
            lanes = slice(g * LANES, (g + 1) * LANES)
            st = jnp.dot(kb[:, lanes], _masked_q_blocks(qt[lanes], d), preferred_element_type=F32)
            pt = jnp.exp2(st - _colmax(st)).astype(BF16)
            for j in range(heads_per_group):
                h = g * heads_per_group + j
                ve = jnp.concatenate([vt[h * dv:(h + 1) * dv], ones], axis=0)
                oe = jnp.dot(ve, pt[:, j * w:(j + 1) * w], preferred_element_type=F32)
                os = [oe[0:dv, i * SEQ:(i + 1) * SEQ] / oe[dv:dv + 1, i * SEQ:(i + 1) * SEQ]
                      for i in range(maps_per_head)]
                acc_ref[h * dv:(h + 1) * dv, :] = finish(os)

    def da_finish(os):
        ot = os[0] - lam * os[1]
        ot = ot * lax.rsqrt(jnp.mean(ot * ot, axis=0, keepdims=True) + EPS) * sub_ref[...]
        return ot * (1.0 - lam_init)

    for s in range(CTX_SEQS):
        attend(s, nq_ref, nk_ref, nv_ref, NA_HEAD_DIM, 1, lambda os: os[0])
    for s in range(CTX_SEQS):
        yn_ref[pl.ds(s * SEQ, SEQ), :] = accs_ref[s].T.astype(yn_ref.dtype)
    for s in range(CTX_SEQS):
        attend(s, dq_ref, dk_ref, dv_ref, DA_HEAD_DIM, 2, da_finish)
    for s in range(CTX_SEQS):
        yd_ref[pl.ds(s * SEQ, SEQ), :] = accs_ref[s].T.astype(yd_ref.dtype)


def _ctx_attention(u, da_lambda, subln_col, lam_init):
    col = lambda j: pl.BlockSpec((CTX_SEQS * SEQ, BRANCH_W), lambda b, j=j: (b, j))
    out = pl.BlockSpec((CTX_SEQS * SEQ, BRANCH_W), lambda b: (b, 0))
    shape = jax.ShapeDtypeStruct((BATCH * SEQ, BRANCH_W), BF16)
    return pl.pallas_call(
        functools.partial(_ctx_attn_kernel, lam_init=lam_init),
        grid=(BATCH // CTX_SEQS,),
        in_specs=[col(3), col(4), col(5), col(6), col(7), col(8),
                  pl.BlockSpec((4, DA_HEAD_DIM), lambda b: (0, 0)),
                  pl.BlockSpec((DA_V_DIM, 1), lambda b: (0, 0))],
        out_specs=[out, out],
        out_shape=[shape, shape],
        scratch_shapes=[pltpu.VMEM((CTX_SEQS, BRANCH_W, SEQ), F32)],
        compiler_params=_cparams("arbitrary"),
        name="ctx_attention",
    )(u, u, u, u, u, u, da_lambda, subln_col)


def _rope(x, cos, sin_signed):
    n = x.shape[-1]
    lane = lax.broadcasted_iota(jnp.int32, x.shape, 1)
    partner = jnp.where(lane % 2 == 0, pltpu.roll(x, n - 1, axis=1), pltpu.roll(x, 1, axis=1))
    return x * cos + partner * sin_signed


def _attn_prep_kernel(q_ref, k_ref, v_ref, kc_ref, vc_ref, *refs, rope, scale):
    cos_ref, sin_ref = refs[:2] if rope else (None, None)
    qt_ref, ko_ref, vt_ref = refs[-3:]
    t = pl.program_id(1)
    dv = NA_HEAD_DIM

    def put_v(v):
        n = v.shape[0]
        vt = v.astype(F32).T.astype(BF16)
        ones = jnp.ones((ATT_ONES_ROWS, n), BF16)
        for h in range(BRANCH_W // dv):
            vt_ref[0, h, 0:dv, 0:n] = vt[h * dv:(h + 1) * dv]
            vt_ref[0, h, dv:dv + ATT_ONES_ROWS, 0:n] = ones

    @pl.when(t < DEC_SEQ // ATT_PREP_T)
    def _():
        q = q_ref[...].astype(F32)
        k = k_ref[...].astype(F32)
        if rope:
            cos = jnp.concatenate([cos_ref[...]] * (BRANCH_W // LANES), axis=1)
            sin = jnp.concatenate([sin_ref[...]] * (BRANCH_W // LANES), axis=1)
            q = _rope(q, cos, sin)
            k = _rope(k, cos, sin)
        qt_ref[0] = (q * scale).T.astype(BF16)
        ko_ref[0] = k.astype(BF16)
        put_v(v_ref[...])

    @pl.when(t == DEC_SEQ // ATT_PREP_T)
    def _():
        ko_ref[0, 0:PAST_LEN, :] = kc_ref[0].astype(BF16)
        put_v(vc_ref[0])


def _attn_prep(u, first_col, k_ctx, v_ctx, head_dim, rope_tables=None):
    rope = rope_tables is not None
    tile = ATT_PREP_T
    nt = DEC_SEQ // tile
    last = nt - 1
    rowblk = lambda j: pl.BlockSpec((tile, BRANCH_W), lambda b, t, j=j: (b * nt + jnp.minimum(t, last), j))
    tab = pl.BlockSpec((tile, LANES), lambda b, t: (jnp.minimum(t, last), 0))
    ctx = pl.BlockSpec((1, PAST_LEN, BRANCH_W), lambda b, t: (b, 0, 0))
    heads = BRANCH_W // NA_HEAD_DIM
    vrows = NA_HEAD_DIM + ATT_ONES_ROWS
    return pl.pallas_call(
        functools.partial(_attn_prep_kernel, rope=rope, scale=head_dim ** -0.5 * LOG2E),
        grid=(DEC_BATCH, nt + 1),
        in_specs=[rowblk(first_col), rowblk(first_col + 1), rowblk(first_col + 2), ctx, ctx] + [tab, tab] * rope,
        out_specs=[
            pl.BlockSpec((1, BRANCH_W, tile), lambda b, t: (b, 0, jnp.minimum(t, last))),
            pl.BlockSpec((1, tile, BRANCH_W), lambda b, t: (b, t, 0)),
            pl.BlockSpec((1, heads, vrows, tile), lambda b, t: (b, 0, 0, t)),
        ],
        out_shape=[
            jax.ShapeDtypeStruct((DEC_BATCH, BRANCH_W, DEC_SEQ), BF16),
            jax.ShapeDtypeStruct((DEC_BATCH, ATT_KEYS, BRANCH_W), BF16),
            jax.ShapeDtypeStruct((DEC_BATCH, heads, vrows, ATT_KEYS), BF16),
        ],
        compiler_params=_cparams("arbitrary", "arbitrary"),
        name="attn_prep",
    )(u, u, u, k_ctx, v_ctx, *(rope_tables or ()))


NA_ROWS = ATT_TQ // GRID_W
NA_UNION = 3 * NA_ROWS
NA_STEPS = GRID_H // NA_ROWS
NA_SLABS = NA_UNION // NA_ROWS
NA_VARIANT_OFFSET = (0, -NA_ROWS, -2 * NA_ROWS)


def _na_variant(s):
    return jnp.minimum(s, 1) + s // (NA_STEPS - 1)


def _na_window_block(s):
    return jnp.clip(s - 1, 0, NA_STEPS - NA_SLABS)


def _na_bias_kernel(rpb_ref, o_ref):
    kc = lax.broadcasted_iota(jnp.int32, (GRID_W, GRID_W), 0)
    qc = lax.broadcasted_iota(jnp.int32, (GRID_W, GRID_W), 1)
    dc = jnp.clip(kc - qc, -(NA_WIN_COLS - 1), NA_WIN_COLS - 1) + (NA_WIN_COLS - 1)
    c0 = jnp.clip(qc - NA_WIN_COLS // 2, 0, GRID_W - NA_WIN_COLS)
    col_ok = (kc >= c0) & (kc < c0 + NA_WIN_COLS)
    r = rpb_ref[0, 0] * LOG2E
    masked = jnp.full((GRID_W, GRID_W), NEG_INF, F32)
    tiles = []
    for dr in range(2 * NA_WIN_ROWS - 1):
        acc = jnp.zeros((GRID_W, GRID_W), F32)
        for d in range(2 * NA_WIN_COLS - 1):
            acc = jnp.where(dc == d, r[dr:dr + 1, d:d + 1], acc)
        tiles.append(jnp.where(col_ok, acc, masked))
    for v, off in enumerate(NA_VARIANT_OFFSET):
        for kr in range(NA_UNION):
            for rr in range(NA_ROWS):
                w0 = (0, rr, NA_UNION - NA_WIN_ROWS)[v]
                dr = kr + off - rr
                inside = w0 <= kr < w0 + NA_WIN_ROWS
                o_ref[0, v, 0, kr * GRID_W:(kr + 1) * GRID_W, rr * GRID_W:(rr + 1) * GRID_W] = (
                    tiles[dr + NA_WIN_ROWS - 1] if inside else masked)


def _na_bias_table(na_rpb):
    n_dr, n_dc = 2 * NA_WIN_ROWS - 1, 2 * NA_WIN_COLS - 1
    nv = len(NA_VARIANT_OFFSET)
    return pl.pallas_call(
        _na_bias_kernel,
        grid=(DEPTH, NA_HEADS),
        in_specs=[pl.BlockSpec((1, 1, n_dr, n_dc), lambda l, h: (l, h, 0, 0))],
        out_specs=pl.BlockSpec((1, nv, 1, NA_UNION * GRID_W, ATT_TQ), lambda l, h: (l, 0, h, 0, 0)),
        out_shape=jax.ShapeDtypeStruct((DEPTH, nv, NA_HEADS, NA_UNION * GRID_W, ATT_TQ), F32),
        compiler_params=_cparams("arbitrary", "arbitrary"),
        name="na_bias_table",
    )(na_rpb)


def _na_kernel(qt_ref, *refs):
    n = NA_SLABS + 1
    k_refs, vt_refs = refs[:n], refs[n:2 * n]
    bias_ref, o_ref, acc_ref = refs[2 * n:]
    dv = NA_HEAD_DIM
    heads_per_group = LANES // dv
    for g in range(BRANCH_W // LANES):
        lanes = slice(g * LANES, (g + 1) * LANES)
        qbd = _masked_q_blocks(qt_ref[0, lanes, :], dv)
        keys = jnp.concatenate([k_ref[0, :, lanes] for k_ref in k_refs], axis=0)
        st = jnp.dot(keys, qbd, preferred_element_type=F32)
        n_win = NA_SLABS * ATT_TQ
        st_win = st[0:n_win] + jnp.concatenate(
            [bias_ref[0, g * heads_per_group + hh] for hh in range(heads_per_group)], axis=1)
        st_ctx = st[n_win:]
        mx = jnp.maximum(_colmax(st_win), _colmax(st_ctx))
        pt = jnp.concatenate([jnp.exp2(st_win - mx), jnp.exp2(st_ctx - mx)], axis=0).astype(BF16)
        for hh in range(heads_per_group):
            h = g * heads_per_group + hh
            ve = jnp.concatenate([vt_ref[0, h] for vt_ref in vt_refs], axis=1)
            oe = jnp.dot(ve, pt[:, hh * ATT_TQ:(hh + 1) * ATT_TQ], preferred_element_type=F32)
            acc_ref[h * dv:(h + 1) * dv, :] = oe[0:dv] / oe[dv:dv + 1]
    o_ref[...] = acc_ref[...].T.astype(o_ref.dtype)


def _nbr_attention(qt, k, vt, bias, l):
    vrows = NA_HEAD_DIM + ATT_ONES_ROWS
    ctx_blk = DEC_SEQ // ATT_TQ
    k_specs = [pl.BlockSpec((1, ATT_TQ, BRANCH_W), lambda b, s, j=j: (b, _na_window_block(s) + j, 0))
               for j in range(NA_SLABS)]
    k_specs.append(pl.BlockSpec((1, ATT_TQ, BRANCH_W), lambda b, s: (b, ctx_blk, 0)))
    vt_specs = [pl.BlockSpec((1, NA_HEADS, vrows, ATT_TQ), lambda b, s, j=j: (b, 0, 0, _na_window_block(s) + j))
                for j in range(NA_SLABS)]
    vt_specs.append(pl.BlockSpec((1, NA_HEADS, vrows, ATT_TQ), lambda b, s: (b, 0, 0, ctx_blk)))
    n = NA_SLABS + 1
    return pl.pallas_call(
        _na_kernel,
        grid=(DEC_BATCH, NA_STEPS),
        in_specs=[pl.BlockSpec((1, BRANCH_W, ATT_TQ), lambda b, s: (b, 0, s))] + k_specs + vt_specs + [
            pl.BlockSpec((None, 1, NA_HEADS, NA_UNION * GRID_W, ATT_TQ), lambda b, s: (l, _na_variant(s), 0, 0, 0))],
        out_specs=pl.BlockSpec((ATT_TQ, BRANCH_W), lambda b, s: (b * NA_STEPS + s, 0)),
        out_shape=jax.ShapeDtypeStruct((DEC_BATCH * DEC_SEQ, BRANCH_W), BF16),
        scratch_shapes=[pltpu.VMEM((BRANCH_W, ATT_TQ), F32)],
        compiler_params=_cparams("arbitrary", "arbitrary"),
        name="nbr_attention",
    )(qt, *([k] * n), *([vt] * n), bias)


DA_TQ = ATT_TQ
DA_KEYS = ATT_KEYS
DA_ONES_ROWS = ATT_ONES_ROWS
DA_MAPS_PER_TILE = LANES // DA_HEAD_DIM


def _da_kernel(qt_ref, k_ref, vt_ref, lam_ref, sub_ref, o_ref, acc_ref, kn_ref, *, lam_init):
    lam = _da_lambda(lam_ref, lam_init)
    heads = DA_MAPS_PER_TILE // 2
    w = 2 * DA_TQ
    r = lax.broadcasted_iota(jnp.int32, (LANES, LANES), 0) // DA_HEAD_DIM
    c = lax.broadcasted_iota(jnp.int32, (LANES, LANES), 1) // DA_HEAD_DIM
    same_map = (r == c).astype(F32)

    @pl.when(pl.program_id(1) == 0)
    def _():
        for g in range(BRANCH_W // LANES):
            lanes = slice(g * LANES, (g + 1) * LANES)
            kf = k_ref[0, :, lanes].astype(F32)
            n2 = _bdot(kf * kf, same_map)
            kn_ref[:, lanes] = jnp.max(n2, axis=0, keepdims=True)

    jrow = lax.broadcasted_iota(jnp.int32, (8, LANES), 0)
    dmap = lax.broadcasted_iota(jnp.int32, (8, LANES), 1) // DA_HEAD_DIM
    for g in range(BRANCH_W // LANES):
        lanes = slice(g * LANES, (g + 1) * LANES)
        qg = qt_ref[0, lanes, :]
        qbd = _masked_q_blocks(qg, DA_HEAD_DIM)
        qf = qg.astype(F32)
        b2 = _bdot(jnp.where(jrow == dmap, kn_ref[:, lanes], 0.0), qf * qf)
        bound = jnp.concatenate([jnp.sqrt(b2[j:j + 1]) for j in range(DA_MAPS_PER_TILE)], axis=1) * 1.01 + 1e-3

        def attend(carry, g=g, lanes=lanes, qbd=qbd):
            it, shift, _ = carry
            st = jnp.dot(k_ref[0, :, lanes], qbd, preferred_element_type=F32)
            pt = jnp.exp2(st - shift).astype(BF16)
            low = jnp.float32(jnp.inf)
            for hh in range(heads):
                h = g * heads + hh
                oe = jnp.dot(vt_ref[0, h], pt[:, hh * w:(hh + 1) * w], preferred_element_type=F32)
                den = oe[DA_V_DIM:DA_V_DIM + 1]
                low = jnp.minimum(low, jnp.min(den))
                os = [oe[0:DA_V_DIM, i * DA_TQ:(i + 1) * DA_TQ] / den[:, i * DA_TQ:(i + 1) * DA_TQ] for i in range(2)]
                ot = os[0] - lam * os[1]
                ot = ot * lax.rsqrt(jnp.mean(ot * ot, axis=0, keepdims=True) + EPS) * sub_ref[...]
                acc_ref[h * DA_V_DIM:(h + 1) * DA_V_DIM, :] = ot * (1.0 - lam_init)
            return it + 1, _colmax(st), low

        def again(carry):
            it, _, low = carry
            return (it == 0) | ((it == 1) & jnp.logical_not(low >= ATT_MIN_DENOM))

        lax.while_loop(again, attend, (jnp.int32(0), bound, jnp.float32(0.0)))
    o_ref[...] = acc_ref[...].T.astype(o_ref.dtype)


def _diff_attention(qt, k, vt, da_lambda, subln_col, lam_init):
    nt = DEC_SEQ // DA_TQ
    vrows = DA_V_DIM + DA_ONES_ROWS
    return pl.pallas_call(
        functools.partial(_da_kernel, lam_init=lam_init),
        grid=(DEC_BATCH, nt),
        in_specs=[
            pl.BlockSpec((1, BRANCH_W, DA_TQ), lambda b, t: (b, 0, t)),
            pl.BlockSpec((1, DA_KEYS, BRANCH_W), lambda b, t: (b, 0, 0)),
            pl.BlockSpec((1, DA_HEADS, vrows, DA_KEYS), lambda b, t: (b, 0, 0, 0)),
            pl.BlockSpec((4, DA_HEAD_DIM), lambda b, t: (0, 0)),
            pl.BlockSpec((DA_V_DIM, 1), lambda b, t: (0, 0)),
        ],
        out_specs=pl.BlockSpec((DA_TQ, BRANCH_W), lambda b, t: (b * nt + t, 0)),
        out_shape=jax.ShapeDtypeStruct((DEC_BATCH * DEC_SEQ, BRANCH_W), BF16),
        scratch_shapes=[pltpu.VMEM((BRANCH_W, DA_TQ), F32), pltpu.VMEM((1, BRANCH_W), F32)],
        compiler_params=_cparams("arbitrary", "arbitrary"),
        name="diff_attention",
    )(qt, k, vt, da_lambda, subln_col)


def _rope_tables():
    pos = np.arange(DEC_SEQ)
    row = (pos // GRID_W).astype(np.float32)
    col = (pos % GRID_W).astype(np.float32)
    n_freq = DA_HEAD_DIM // 4
    inv = (np.float32(ROPE_BASE) ** (-np.arange(n_freq, dtype=np.float32) / n_freq)).astype(np.float32)
    ang = np.concatenate([row[:, None] * inv[None, :], col[:, None] * inv[None, :]], axis=-1)
    ang = ang.astype(np.float64)
    cos = np.repeat(np.cos(ang), 2, axis=-1)
    sin = np.repeat(np.sin(ang), 2, axis=-1)
    sign = np.where(np.arange(DA_HEAD_DIM) % 2 == 0, -1.0, 1.0)
    reps = LANES // DA_HEAD_DIM
    cos = np.tile(cos, (1, reps)).astype(np.float32)
    sin = np.tile(sin * sign[None, :], (1, reps)).astype(np.float32)
    return jnp.asarray(cos), jnp.asarray(sin)


def _filt_hidden_kernel(feat_ref, w1_ref, b1_ref, w2_ref, b2_ref, fr_ref, o_ref):
    half = feat_ref.shape[0]
    fr = fr_ref[0]
    h = jnp.sin(fr * (jnp.dot(feat_ref[...], w1_ref[0], precision=HIGHEST, preferred_element_type=F32) + b1_ref[0]))
    h = jnp.sin(fr * (jnp.dot(h, w2_ref[0], precision=HIGHEST, preferred_element_type=F32) + b2_ref[0]))
    o_ref[0, 0:half] = h
    blk = MXU_DIM
    r = lax.broadcasted_iota(jnp.int32, (blk, blk), 0)
    c = lax.broadcasted_iota(jnp.int32, (blk, blk), 1)
    exchange = (r + c == blk - 1).astype(F32)
    nb = half // blk
    rev = jnp.concatenate(
        [jnp.dot(exchange, h[(nb - 1 - b) * blk:(nb - b) * blk], precision=HIGHEST, preferred_element_type=F32)
         for b in range(nb)], axis=0)
    o_ref[0, half:2 * half] = pltpu.roll(rev, 1, axis=0)


def _filt_kernel(h_ref, w3f_ref, w3b_ref, dec_ref, o_ref):
    L = dec_ref.shape[0] // 2
    hf = jnp.dot(h_ref[0, 0:L], w3f_ref[0], precision=HIGHEST, preferred_element_type=F32) * dec_ref[0:L]
    hb = jnp.dot(h_ref[0, L:2 * L], w3b_ref[0], precision=HIGHEST, preferred_element_type=F32) * dec_ref[L:2 * L]
    row = lax.broadcasted_iota(jnp.int32, hb.shape, 0)
    hb = jnp.where(row == 0, 0.0, hb)
    nrm = jnp.sum(jnp.abs(hf), axis=0, keepdims=True) + jnp.sum(jnp.abs(hb), axis=0, keepdims=True)
    o_ref[0, 0, 0:L] = hf / nrm
    o_ref[0, 0, L:2 * L] = hb / nrm


def _circular_order(a):
    return np.concatenate([a, a[:1], a[1:][::-1]], axis=0)


def _hyena_pos_tables(L):
    f32 = np.float32
    pos = np.arange(L, dtype=f32)
    t = (pos / f32(L)).astype(f32)
    bands = np.linspace(1e-4, HY_POS_BANDS - 1, HY_POS_BANDS, dtype=f32)
    ang = (f32(2 * math.pi / L) * pos[:, None] * bands[None, :]).astype(np.float64)
    feats = np.zeros((L, HY_FILT_HIDDEN), f32)
    feats[:, 0] = t
    feats[:, 1:1 + HY_POS_BANDS] = np.cos(ang)
    feats[:, 1 + HY_POS_BANDS:HY_POS_DIM] = -np.sin(ang)
    deltas = np.linspace(math.log(HY_DECAY_TARGET) / HY_SLOW_DECAY,
                         math.log(HY_DECAY_TARGET) / HY_FAST_DECAY, BRANCH_W, dtype=f32)
    decay = np.exp((-t[:, None] * np.abs(deltas)[None, :]).astype(np.float64)).astype(f32)
    return jnp.asarray(feats), jnp.asarray(_circular_order(decay))


def _hyena_filters(half, w1p, b1, w2, b2, w3, freq):
    feats, decay = _hyena_pos_tables(half)
    L = 2 * half
    cb = LANES
    ncb = BRANCH_W // cb
    small = lambda shape: pl.BlockSpec((1,) + shape, lambda l: (l, 0, 0))
    hidden = pl.pallas_call(
        _filt_hidden_kernel,
        grid=(DEPTH,),
        in_specs=[
            pl.BlockSpec((half, HY_FILT_HIDDEN), lambda l: (0, 0)),
            small((HY_FILT_HIDDEN, HY_FILT_HIDDEN)), small((1, HY_FILT_HIDDEN)),
            small((HY_FILT_HIDDEN, HY_FILT_HIDDEN)), small((1, HY_FILT_HIDDEN)),
            small((1, HY_FILT_HIDDEN)),
        ],
        out_specs=pl.BlockSpec((1, L, HY_FILT_HIDDEN), lambda l: (l, 0, 0)),
        out_shape=jax.ShapeDtypeStruct((DEPTH, L, HY_FILT_HIDDEN), F32),
        compiler_params=_cparams("arbitrary"),
        name=f"hyena_filter_hidden_{L}",
    )(feats, w1p, b1, w2, b2, freq)
    return pl.pallas_call(
        _filt_kernel,
        grid=(DEPTH, 2, ncb),
        in_specs=[
            pl.BlockSpec((1, L, HY_FILT_HIDDEN), lambda l, o, c: (l, 0, 0)),
            pl.BlockSpec((1, HY_FILT_HIDDEN, cb), lambda l, o, c: (l, 0, o * 2 * ncb + c)),
            pl.BlockSpec((1, HY_FILT_HIDDEN, cb), lambda l, o, c: (l, 0, o * 2 * ncb + ncb + c)),
            pl.BlockSpec((L, cb), lambda l, o, c: (0, c)),
        ],
        out_specs=pl.BlockSpec((1, 1, L, cb), lambda l, o, c: (l, o, 0, c)),
        out_shape=jax.ShapeDtypeStruct((DEPTH, 2, L, BRANCH_W), F32),
        compiler_params=_cparams("arbitrary", "arbitrary", "arbitrary"),
        name=f"hyena_filters_{L}",
    )(hidden, w3, w3, decay)


def _short_conv(u, w_ref, b_ref, seq_len):
    n = u.shape[0]
    t = lax.broadcasted_iota(jnp.int32, u.shape, 0) % seq_len
    prev = jnp.where(t == 0, 0.0, pltpu.roll(u, 1, axis=0))
    nxt = jnp.where(t == seq_len - 1, 0.0, pltpu.roll(u, n - 1, axis=0))
    return prev * w_ref[0:1, :] + u * w_ref[1:2, :] + nxt * w_ref[2:3, :] + b_ref[...]


def _dft_direct_mats():
    n, half = 2 * SEQ, SEQ
    k = np.arange(n)[:, None].astype(np.float64)
    t = np.arange(half)[None, :].astype(np.float64)
    ang = 2 * np.pi * k * t / n
    fr, fi = np.cos(ang), -np.sin(ang)
    mf = np.block([[fr, -fi], [fi, fr]])
    gr, gi = np.cos(ang).T / n, np.sin(ang).T / n
    mi = np.block([[gr, -gi], [gi, gr]])
    return mf.astype(np.float32), mi.astype(np.float32)


def _dft_real_mat():
    n = 2 * SEQ
    ang = 2 * np.pi * np.arange(n)[:, None].astype(np.float64) * np.arange(n)[None, :] / n
    return np.concatenate([np.cos(ang), -np.sin(ang)], axis=0).astype(np.float32)


def _spec_direct_kernel(h_ref, m_ref, o_ref):
    o_ref[0, 0] = jnp.dot(m_ref[...], h_ref[0, 0], precision=HIGHEST, preferred_element_type=F32)


def _spec_direct(h, m_real):
    n = 2 * SEQ
    return pl.pallas_call(
        _spec_direct_kernel,
        grid=(DEPTH, 2),
        in_specs=[pl.BlockSpec((1, 1, n, BRANCH_W), lambda l, o: (l, o, 0, 0)),
                  pl.BlockSpec((2 * n, n), lambda l, o: (0, 0))],
        out_specs=pl.BlockSpec((1, 1, 2 * n, BRANCH_W), lambda l, o: (l, o, 0, 0)),
        out_shape=jax.ShapeDtypeStruct((DEPTH, 2, 2 * n, BRANCH_W), F32),
        compiler_params=_cparams("arbitrary", "arbitrary"),
        name="hyena_spectrum_direct",
    )(h, m_real)


def _hyena_direct_kernel(u_ref, cw_ref, cb_ref, h_ref, bias_ref, mf_ref, mi_ref, o_ref):
    n = 2 * SEQ
    c = _short_conv(u_ref[...].astype(F32), cw_ref, cb_ref, SEQ)
    z = c[:, 0:BRANCH_W]
    for order in range(2):
        gate = c[:, (order + 1) * BRANCH_W:(order + 2) * BRANCH_W]
        f = jnp.dot(mf_ref[...], z.astype(BF16), preferred_element_type=F32)
        fr, fi = f[0:n], f[n:2 * n]
        hr, hi = h_ref[order, 0:n], h_ref[order, n:2 * n]
        y = jnp.concatenate([fr * hr - fi * hi, fr * hi + fi * hr], axis=0)
        y = jnp.dot(mi_ref[...], y.astype(BF16), preferred_element_type=F32)
        z = gate * (y + z * bias_ref[order:order + 1, :])
    o_ref[...] = z.astype(o_ref.dtype)


def _hyena_direct(u, conv_w, conv_b, spec, l, bias, mf, mi):
    n = 2 * SEQ
    rows = 2 * SEQ
    T = u.shape[0]
    return pl.pallas_call(
        _hyena_direct_kernel,
        grid=(T // rows,),
        in_specs=[
            pl.BlockSpec((rows, 3 * BRANCH_W), lambda p: (p, 0)),
            pl.BlockSpec((3, 3 * BRANCH_W), lambda p: (0, 0)),
            pl.BlockSpec((1, 3 * BRANCH_W), lambda p: (0, 0)),
            pl.BlockSpec((None, 2, 2 * n, BRANCH_W), lambda p: (l, 0, 0, 0)),
            pl.BlockSpec((2, BRANCH_W), lambda p: (0, 0)),
            pl.BlockSpec((2 * n, rows), lambda p: (0, 0)),
            pl.BlockSpec((rows, 2 * n), lambda p: (0, 0)),
        ],
        out_specs=pl.BlockSpec((rows, BRANCH_W), lambda p: (p, 0)),
        out_shape=jax.ShapeDtypeStruct((T, BRANCH_W), BF16),
        compiler_params=_cparams("arbitrary"),
        name="hyena_direct",
    )(u, conv_w, conv_b, spec, bias, mf, mi)


def _dft_two_stage_mats():
    no, ni, half, n = FFT_NO, FFT_NI, FFT_HALF, FFT_N
    f64 = np.float64
    k1 = np.arange(no, dtype=f64)
    n_o = np.arange(half, dtype=f64)
    n_i = np.arange(ni, dtype=f64)
    ang = 2 * np.pi * (n_i[:, None, None] * k1[None, :, None] / n + k1[None, :, None] * n_o[None, None, :] / no)
    tr, ti = np.cos(ang), -np.sin(ang)
    m1 = np.concatenate([np.concatenate([tr, -ti], axis=2), np.concatenate([ti, tr], axis=2)], axis=1)
    k2 = np.arange(ni, dtype=f64)
    ang2 = 2 * np.pi * k2[:, None] * n_i[None, :] / ni
    f2r, f2i = np.cos(ang2), -np.sin(ang2)
    m2 = np.block([[f2r, -f2i], [f2i, f2r]])
    m2c = np.block([[f2r, f2i], [-f2i, f2r]])
    sr, si = np.transpose(tr, (0, 2, 1)) / n, -np.transpose(ti, (0, 2, 1)) / n
    m3 = np.concatenate([np.concatenate([sr, -si], axis=2), np.concatenate([si, sr], axis=2)], axis=1)
    return (m1.astype(np.float32), m2.astype(np.float32), m2c.astype(np.float32), m3.astype(np.float32))


def _dft_stage1_real_mat():
    no, ni, n = FFT_NO, FFT_NI, FFT_N
    k1 = np.arange(no, dtype=np.float64)
    n_o = np.arange(no, dtype=np.float64)
    n_i = np.arange(ni, dtype=np.float64)
    ang = 2 * np.pi * (n_i[:, None, None] * k1[None, :, None] / n + k1[None, :, None] * n_o[None, None, :] / no)
    return np.concatenate([np.cos(ang), -np.sin(ang)], axis=1).astype(np.float32)


def _store_stage1(w_ref, ni, out):
    w_ref[pl.ds(ni, FFT_NO, stride=FFT_W_PITCH), :] = out[0:FFT_NO]
    w_ref[pl.ds(FFT_NI + ni, FFT_NO, stride=FFT_W_PITCH), :] = out[FFT_NO:2 * FFT_NO]


def _fwd_stage1(za_ref, zb_ref, m1_ref, w_ref):
    def body(ni, carry):
        a = za_ref[pl.ds(ni, FFT_HALF, stride=FFT_Z_PITCH), :]
        b = zb_ref[pl.ds(ni, FFT_HALF, stride=FFT_Z_PITCH), :]
        out = jnp.dot(m1_ref[ni], jnp.concatenate([a, b], axis=0).astype(BF16), preferred_element_type=F32)
        _store_stage1(w_ref, ni, out)
        return carry

    lax.fori_loop(0, FFT_NI, body, 0, unroll=FFT_UNROLL)


def _spec_two_stage_kernel(h_ref, m1_ref, m2_ref, o_ref, w_ref, hp_ref):
    for no in range(FFT_NO):
        hp_ref[no * FFT_Z_PITCH:no * FFT_Z_PITCH + FFT_NI, :] = h_ref[0, 0, no * FFT_NI:(no + 1) * FFT_NI, :]

    def stage1(ni, carry):
        a = hp_ref[pl.ds(ni, FFT_NO, stride=FFT_Z_PITCH), :]
        _store_stage1(w_ref, ni, jnp.dot(m1_ref[ni], a.astype(BF16), preferred_element_type=F32))
        return carry

    lax.fori_loop(0, FFT_NI, stage1, 0, unroll=FFT_UNROLL)
    blk = 2 * FFT_NI

    cb = w_ref.shape[1]

    def stage2(kp, carry):
        rows = [pl.ds(pl.multiple_of((2 * kp + j) * blk, blk), blk) for j in range(2)]
        wrows = [pl.ds(pl.multiple_of((2 * kp + j) * FFT_W_PITCH, 8), blk) for j in range(2)]
        x = jnp.dot(m2_ref[...], jnp.concatenate([w_ref[r, :] for r in wrows], axis=1).astype(BF16),
                    preferred_element_type=F32)
        for j in range(2):
            o_ref[0, 0, rows[j], :] = x[:, j * cb:(j + 1) * cb]
        return carry

    lax.fori_loop(0, FFT_NO // 2, stage2, 0, unroll=FFT_MID_UNROLL)


def _spec_two_stage(h, m1_real, m2):
    cb = LCONV_CB
    return pl.pallas_call(
        _spec_two_stage_kernel,
        grid=(DEPTH, 2, BRANCH_W // cb),
        in_specs=[pl.BlockSpec((1, 1, FFT_N, cb), lambda l, o, c: (l, o, 0, c)),
                  pl.BlockSpec((FFT_NI, 2 * FFT_NO, FFT_NO), lambda l, o, c: (0, 0, 0)),
                  pl.BlockSpec((2 * FFT_NI, 2 * FFT_NI), lambda l, o, c: (0, 0))],
        out_specs=pl.BlockSpec((1, 1, 2 * FFT_N, cb), lambda l, o, c: (l, o, 0, c)),
        out_shape=jax.ShapeDtypeStruct((DEPTH, 2, 2 * FFT_N, BRANCH_W), F32),
        scratch_shapes=[pltpu.VMEM((FFT_NO * FFT_W_PITCH, cb), F32), pltpu.VMEM((FFT_NO * FFT_Z_PITCH, cb), F32)],
        compiler_params=_cparams("arbitrary", "arbitrary", "arbitrary"),
        name="hyena_spectrum_two_stage",
    )(h, m1_real, m2)


def _lconv_two_stage_kernel(s_ref, g_ref, cws_ref, cbs_ref, cwg_ref, cbg_ref, h_ref, bias_ref,
                            m1_ref, m2_ref, m2c_ref, m3_ref, o_ref, z_ref, w_ref, y_ref, *, conv_sig):
    def chunk(no):
        return slice(no * FFT_NI, (no + 1) * FFT_NI), slice(no * FFT_Z_PITCH, no * FFT_Z_PITCH + FFT_NI)

    for b in range(2):
        sig = s_ref[b].astype(F32)
        if conv_sig:
            sig = _short_conv(sig, cws_ref, cbs_ref, DEC_SEQ)
        for no in range(FFT_HALF):
            rows, prow = chunk(no)
            z_ref[b, prow, :] = sig[rows]
    _fwd_stage1(z_ref.at[0], z_ref.at[1], m1_ref, w_ref)
    blk = 2 * FFT_NI

    cb = w_ref.shape[1]

    def mid(kp, carry):
        rows = [pl.ds(pl.multiple_of((2 * kp + j) * blk, blk), blk) for j in range(2)]
        wrows = [pl.ds(pl.multiple_of((2 * kp + j) * FFT_W_PITCH, 8), blk) for j in range(2)]
        x = jnp.dot(m2_ref[...], jnp.concatenate([w_ref[r, :] for r in wrows], axis=1).astype(BF16),
                    preferred_element_type=F32)
        h = jnp.concatenate([h_ref[r, :] for r in rows], axis=1)
        xr, xi = x[0:FFT_NI], x[FFT_NI:blk]
        hr, hi = h[0:FFT_NI], h[FFT_NI:blk]
        y = jnp.concatenate([xr * hr - xi * hi, xr * hi + xi * hr], axis=0)
        c = jnp.dot(m2c_ref[...], y.astype(BF16), preferred_element_type=F32)
        for j in range(2):
            w_ref[wrows[j], :] = c[:, j * cb:(j + 1) * cb]
        return carry

    lax.fori_loop(0, FFT_NO // 2, mid, 0, unroll=FFT_MID_UNROLL)

    def last(ni, carry):
        cr = w_ref[pl.ds(ni, FFT_NO, stride=FFT_W_PITCH), :]
        ci = w_ref[pl.ds(FFT_NI + ni, FFT_NO, stride=FFT_W_PITCH), :]
        y = jnp.dot(m3_ref[ni], jnp.concatenate([cr, ci], axis=0).astype(BF16), preferred_element_type=F32)
        y_ref[0, pl.ds(ni, FFT_HALF, stride=FFT_Z_PITCH), :] = y[0:FFT_HALF]
        y_ref[1, pl.ds(ni, FFT_HALF, stride=FFT_Z_PITCH), :] = y[FFT_HALF:2 * FFT_HALF]
        return carry

    lax.fori_loop(0, FFT_NI, last, 0, unroll=FFT_UNROLL)
    for b in range(2):
        gate = _short_conv(g_ref[b].astype(F32), cwg_ref, cbg_ref, DEC_SEQ)
        for no in range(FFT_HALF):
            rows, prow = chunk(no)
            o_ref[b, rows, :] = (gate[rows] * (y_ref[b, prow, :] + z_ref[b, prow, :] * bias_ref[...])).astype(o_ref.dtype)


def _lconv_two_stage(sig, sig_col, gate_src, gate_col, conv_w, conv_b, spec, l, order, bias, mats, conv_sig):
    cb = LCONV_CB
    ncb = BRANCH_W // cb
    m1, m2, m2c, m3 = mats
    const3 = lambda c, p: (0, 0, 0)
    const2 = lambda c, p: (0, 0)
    return pl.pallas_call(
        functools.partial(_lconv_two_stage_kernel, conv_sig=conv_sig),
        grid=(ncb, DEC_BATCH // 2),
        in_specs=[
            pl.BlockSpec((2, DEC_SEQ, cb), lambda c, p: (p, 0, sig_col * ncb + c)),
            pl.BlockSpec((2, DEC_SEQ, cb), lambda c, p: (p, 0, gate_col * ncb + c)),
            pl.BlockSpec((3, cb), lambda c, p: (0, c)),
            pl.BlockSpec((1, cb), lambda c, p: (0, c)),
            pl.BlockSpec((3, cb), lambda c, p: (0, gate_col * ncb + c)),
            pl.BlockSpec((1, cb), lambda c, p: (0, gate_col * ncb + c)),
            pl.BlockSpec((None, None, 2 * FFT_N, cb), lambda c, p: (l, order, 0, c)),
            pl.BlockSpec((1, cb), lambda c, p: (0, c)),
            pl.BlockSpec(m1.shape, const3, pipeline_mode=pl.Buffered(1)),
            pl.BlockSpec(m2.shape, const2, pipeline_mode=pl.Buffered(1)),
            pl.BlockSpec(m2c.shape, const2, pipeline_mode=pl.Buffered(1)),
            pl.BlockSpec(m3.shape, const3, pipeline_mode=pl.Buffered(1)),
        ],
        out_specs=pl.BlockSpec((2, DEC_SEQ, cb), lambda c, p: (p, 0, c)),
        out_shape=jax.ShapeDtypeStruct((DEC_BATCH, DEC_SEQ, BRANCH_W), BF16),
        scratch_shapes=[pltpu.VMEM((2, FFT_HALF * FFT_Z_PITCH, cb), F32),
                        pltpu.VMEM((FFT_NO * FFT_W_PITCH, cb), F32),
                        pltpu.VMEM((2, FFT_HALF * FFT_Z_PITCH, cb), F32)],
        compiler_params=_cparams("arbitrary", "arbitrary"),
        name="hyena_lconv_two_stage",
    )(sig, gate_src, conv_w, conv_b, conv_w, conv_b, spec, bias, m1, m2, m2c, m3)


def kernel(x_prompt, x_sample, cache_na_k, cache_na_v, cache_da_k, cache_da_v, c, c_ctx, w_ada, b_ada, norm_mix,
           norm_ffn, w_in, hy_conv_w, hy_conv_b, hy_filt_w1, hy_filt_b1, hy_filt_w2, hy_filt_b2, hy_filt_w3,
           hy_filt_freq, hy_bias, na_rpb, da_lambda, da_subln, w_lift, w_out, w_ffn_in, w_ffn_out, norm_final):
    TP, TS = BATCH * SEQ, DEC_BATCH * DEC_SEQ
    xp = x_prompt.reshape(TP, D_MODEL)
    xs = x_sample.reshape(TS, D_MODEL)

    cc = jnp.concatenate([c_ctx[None, :], c, jnp.zeros((8 - 1 - DEC_BATCH, D_MODEL), F32)], axis=0)
    mod = _modulation(cc, w_ada, b_ada)
    mod_p = mod[:, 0:1].reshape(DEPTH, 1, 1, 6 * D_MODEL)
    mod_s = mod[:, 1:1 + DEC_BATCH].reshape(DEPTH, DEC_BATCH, 1, 6 * D_MODEL)

    w_mix = w_in[:, :, :MIX_W].astype(BF16)
    w_gate = w_in[:, :, MIX_W:].astype(BF16)
    w_lift_b = w_lift.astype(BF16)
    w_out_b = w_out.astype(BF16)
    w_ffn_in_b = w_ffn_in.astype(BF16)
    w_ffn_out_b = w_ffn_out.astype(BF16)
    g_mix = norm_mix.reshape(DEPTH, 1, D_MODEL)
    g_ffn = norm_ffn.reshape(DEPTH, 1, D_MODEL)
    g_fin = norm_final.reshape(1, D_MODEL)
    subln = da_subln.reshape(DEPTH, 1, DA_V_DIM)
    subln_col = da_subln.reshape(DEPTH, DA_V_DIM, 1)

    w1p = jnp.pad(hy_filt_w1, ((0, 0), (0, HY_FILT_HIDDEN - HY_POS_DIM), (0, 0)))
    b1 = hy_filt_b1.reshape(DEPTH, 1, HY_FILT_HIDDEN)
    b2 = hy_filt_b2.reshape(DEPTH, 1, HY_FILT_HIDDEN)
    fr = hy_filt_freq.reshape(DEPTH, 1, HY_FILT_HIDDEN)
    mf, mi = _dft_direct_mats()
    mats = _dft_two_stage_mats()
    h_p = _hyena_filters(SEQ, w1p, b1, hy_filt_w2, b2, hy_filt_w3, fr)
    h_s = _hyena_filters(DEC_SEQ, w1p, b1, hy_filt_w2, b2, hy_filt_w3, fr)
    spec_p = _spec_direct(h_p, jnp.asarray(_dft_real_mat()))
    mf_b, mi_b = jnp.asarray(mf, dtype=BF16), jnp.asarray(mi, dtype=BF16)
    mats_b = tuple(jnp.asarray(m, dtype=BF16) for m in mats)
    spec_s = _spec_two_stage(h_s, jnp.asarray(_dft_stage1_real_mat(), dtype=BF16), mats_b[1])
    conv_b = hy_conv_b.reshape(DEPTH, 1, 3 * BRANCH_W)

    na_bias = _na_bias_table(na_rpb)
    rope_tables = _rope_tables()
    ck_na = cache_na_k.reshape(DEC_BATCH, DEPTH, PAST_LEN, BRANCH_W)
    cv_na = cache_na_v.reshape(DEC_BATCH, DEPTH, PAST_LEN, BRANCH_W)
    ck_da = cache_da_k.reshape(DEC_BATCH, DEPTH, PAST_LEN, BRANCH_W)
    cv_da = cache_da_v.reshape(DEC_BATCH, DEPTH, PAST_LEN, BRANCH_W)

    caches = tuple(jnp.zeros((BATCH, DEPTH, SEQ, BRANCH_W), F32) for _ in CACHE_BLOCKS)
    for l in range(DEPTH):
        lam_init = 0.8 - 0.6 * math.exp(-0.3 * l)
        final = l == DEPTH - 1

        u, caches = _in_proj(xp, g_mix[l], mod_p[l], w_mix, l, TP, BF16, caches=caches)
        y_hy = _hyena_direct(u, hy_conv_w[l], conv_b[l], spec_p, l, hy_bias[l], mf_b, mi_b)
        y_na, y_da = _ctx_attention(u, da_lambda[l], subln_col[l], lam_init)
        xp = _merge_out(xp, g_mix[l], mod_p[l], y_hy, y_na, y_da, w_gate, w_lift_b, w_out_b, l, TP)
        xp = _ffn(xp, g_ffn[l], mod_p[l], w_ffn_in_b, w_ffn_out_b, g_fin, l, TP, final)

        u = _in_proj(xs, g_mix[l], mod_s[l], w_mix, l, DEC_SEQ, BF16)
        u3 = u.reshape(DEC_BATCH, DEC_SEQ, MIX_W)
        z1 = _lconv_two_stage(u3, 0, u3, 1, hy_conv_w[l], conv_b[l], spec_s, l, 0, hy_bias[l, 0:1], mats_b, True)
        y_hy = _lconv_two_stage(z1, 0, u3, 2, hy_conv_w[l], conv_b[l], spec_s, l, 1, hy_bias[l, 1:2], mats_b, False)
        y_hy = y_hy.reshape(TS, BRANCH_W)
        qn, kn, vn = _attn_prep(u, 3, ck_na[:, l], cv_na[:, l], NA_HEAD_DIM)
        y_na = _nbr_attention(qn, kn, vn, na_bias, l)
        q, kt, v = _attn_prep(u, 6, ck_da[:, l], cv_da[:, l], DA_HEAD_DIM, rope_tables)
        y_da = _diff_attention(q, kt, v, da_lambda[l], subln_col[l], lam_init)
        xs = _merge_out(xs, g_mix[l], mod_s[l], y_hy, y_na, y_da, w_gate, w_lift_b, w_out_b, l, DEC_SEQ)
        xs = _ffn(xs, g_ffn[l], mod_s[l], w_ffn_in_b, w_ffn_out_b, g_fin, l, DEC_SEQ, final)

    y_prompt = xp.reshape(BATCH, SEQ, D_MODEL)
    y_sample = xs.reshape(DEC_BATCH, DEC_SEQ, D_MODEL)
    heads = lambda a, d: a.reshape(BATCH, DEPTH, SEQ, BRANCH_W // d, d)
    return (y_prompt, y_sample, heads(caches[0], NA_HEAD_DIM), heads(caches[1], NA_HEAD_DIM),
            heads(caches[2], 2 * DA_HEAD_DIM), heads(caches[3], DA_V_DIM))
```
